```python
import math
import jax, jax.numpy as jnp
from jax import lax
import numpy as np

D_MODEL = 1024
BATCH = 8
SEQ = 4096
DEPTH = 4

ATTN_WIDTH = D_MODEL // 2
SSM_WIDTH = D_MODEL - ATTN_WIDTH
HEAD_DIM = 64
N_HEADS = ATTN_WIDTH // HEAD_DIM
SSM_GROUP = 16
N_SSM_GROUPS = SSM_WIDTH // SSM_GROUP
SSM_STATE = 64
IN_WIDTH = 3 * ATTN_WIDTH + SSM_WIDTH
D_FF = ((8 * D_MODEL // 3) + 127) // 128 * 128
CONV_WIDTH = 3
BLOCK_Q = 128
N_MOD = 6
EPS = 1e-6

kernel_name = "hybrid_stickbreak_s5_convffn_adaln"


def _rmsnorm(x, g):
    xf = x.astype(jnp.float32)
    inv = lax.rsqrt(jnp.mean(xf * xf, axis=-1, keepdims=True) + EPS)
    return (xf * inv * g.astype(jnp.float32)).astype(x.dtype)


def _modulate(h, shift, scale):
    return h * (1.0 + scale[:, None, :]) + shift[:, None, :]


def _stick_breaking(q, k, v):
    B, H, L, Dh = q.shape
    nb = L // BLOCK_Q
    qf = q.astype(jnp.float32)
    kf = k.astype(jnp.float32)
    vf = v.astype(jnp.float32)
    q_blocks = qf.reshape(B, H, nb, BLOCK_Q, Dh).transpose(2, 0, 1, 3, 4)
    k_pos = jnp.arange(L, dtype=jnp.int32)
    inv_sqrt = 1.0 / math.sqrt(Dh)

    def block_fn(args):
        qb, bi = args
        z = jnp.einsum('bhqd,bhkd->bhqk', qb, kf) * inv_sqrt
        q_pos = bi * BLOCK_Q + jnp.arange(BLOCK_Q, dtype=jnp.int32)
        mask = k_pos[None, :] < q_pos[:, None]
        log_1m = jnp.where(mask, jax.nn.log_sigmoid(-z), 0.0)
        tail = lax.cumsum(log_1m, axis=3, reverse=True) - log_1m
        w = jnp.where(mask, jnp.exp(jax.nn.log_sigmoid(z) + tail), 0.0)
        return jnp.einsum('bhqk,bhkd->bhqd', w, vf)

    o = lax.map(block_fn, (q_blocks, jnp.arange(nb, dtype=jnp.int32)))
    o = o.transpose(1, 0, 3, 2, 4).reshape(B, L, H * Dh)
    return o.astype(q.dtype)


def _s5(u, a_re, a_im, log_dt, b_re, b_im, c_re, c_im, d_skip, glu_w, glu_b):
    Bsz, L, _ = u.shape
    f32 = jnp.float32
    ug = u.astype(f32).reshape(Bsz, L, N_SSM_GROUPS, SSM_GROUP)
    dt = jnp.exp(log_dt.astype(f32))[:, None]
    ar = a_re.astype(f32)
    ai = a_im.astype(f32)
    mag = jnp.exp(dt * ar)
    abar_re = mag * jnp.cos(dt * ai)
    abar_im = mag * jnp.sin(dt * ai)
    em_re = abar_re - 1.0
    em_im = abar_im
    den = ar * ar + ai * ai
    f_re = (em_re * ar + em_im * ai) / den
    f_im = (em_im * ar - em_re * ai) / den
    br = b_re.astype(f32)
    bim = b_im.astype(f32)
    bb_re = f_re[..., None] * br - f_im[..., None] * bim
    bb_im = f_re[..., None] * bim + f_im[..., None] * br
    bu_re = jnp.einsum('blgh,gph->blgp', ug, bb_re)
    bu_im = jnp.einsum('blgh,gph->blgp', ug, bb_im)
    a_r = jnp.broadcast_to(abar_re, bu_re.shape)
    a_i = jnp.broadcast_to(abar_im, bu_re.shape)

    def combine(e1, e2):
        a1r, a1i, b1r, b1i = e1
        a2r, a2i, b2r, b2i = e2
        return (a2r * a1r - a2i * a1i,
                a2r * a1i + a2i * a1r,
                a2r * b1r - a2i * b1i + b2r,
                a2r * b1i + a2i * b1r + b2i)

    _, _, s_re, s_im = lax.associative_scan(combine, (a_r, a_i, bu_re, bu_im), axis=1)
    y = (jnp.einsum('ghp,blgp->blgh', c_re.astype(f32), s_re)
         - jnp.einsum('ghp,blgp->blgh', c_im.astype(f32), s_im)
         + d_skip.astype(f32) * ug)
    y = jax.nn.gelu(y)
    gate = jax.nn.sigmoid(jnp.einsum('blgh,ghk->blgk', y, glu_w.astype(f32)) + glu_b.astype(f32))
    return (y * gate).reshape(Bsz, L, SSM_WIDTH).astype(u.dtype)


def _causal_dwconv(h, w, b):
    L = h.shape[1]
    hp = jnp.pad(h, ((0, 0), (CONV_WIDTH - 1, 0), (0, 0)))
    out = b
    for i in range(CONV_WIDTH):
        out = out + hp[:, i:i + L, :] * w[i]
    return out


def _fwd_setup_inputs(seed: int = 0) -> dict:
    key = jax.random.key(seed)
    ks = jax.random.split(key, 32)
    f32 = jnp.float32
    D, G, P, H = D_MODEL, N_SSM_GROUPS, SSM_STATE, SSM_GROUP
    nrm = lambda k, shape, s: jax.random.normal(k, shape, f32) * s
    n_idx = jnp.arange(P, dtype=f32)
    return {
        "x": jax.random.normal(ks[0], (BATCH, SEQ, D), f32),
        "c": jax.random.normal(ks[1], (BATCH, D), f32),
        "ada_w": nrm(ks[2], (DEPTH, D, N_MOD * D), 0.5 * D ** -0.5),
        "ada_b": nrm(ks[3], (DEPTH, N_MOD * D), 0.01),
        "norm1_g": 1.0 + nrm(ks[4], (DEPTH, D), 0.02),
        "w_in": nrm(ks[5], (DEPTH, D, IN_WIDTH), D ** -0.5),
        "q_norm_g": 1.0 + nrm(ks[6], (DEPTH, HEAD_DIM), 0.02),
        "k_norm_g": 1.0 + nrm(ks[7], (DEPTH, HEAD_DIM), 0.02),
        "ssm_a_re": -0.5 + nrm(ks[8], (DEPTH, G, P), 0.01),
        "ssm_a_im": math.pi * n_idx + nrm(ks[9], (DEPTH, G, P), 0.01),
        "ssm_log_dt": jax.random.uniform(ks[10], (DEPTH, G), f32, math.log(1e-3), math.log(1e-1)),
        "ssm_b_re": nrm(ks[11], (DEPTH, G, P, H), (2.0 * H) ** -0.5),
        "ssm_b_im": nrm(ks[12], (DEPTH, G, P, H), (2.0 * H) ** -0.5),
        "ssm_c_re": nrm(ks[13], (DEPTH, G, H, P), (2.0 * P) ** -0.5),
        "ssm_c_im": nrm(ks[14], (DEPTH, G, H, P), (2.0 * P) ** -0.5),
        "ssm_d": nrm(ks[15], (DEPTH, G, H), 1.0),
        "glu_w": nrm(ks[16], (DEPTH, G, H, H), H ** -0.5),
        "glu_b": nrm(ks[17], (DEPTH, G, H), 0.01),
        "attn_out_g": 1.0 + nrm(ks[18], (DEPTH, ATTN_WIDTH), 0.02),
        "ssm_out_g": 1.0 + nrm(ks[19], (DEPTH, SSM_WIDTH), 0.02),
        "w_out": nrm(ks[20], (DEPTH, D, D), D ** -0.5),
        "norm2_g": 1.0 + nrm(ks[21], (DEPTH, D), 0.02),
        "ffn_w_up": nrm(ks[22], (DEPTH, D, 2 * D_FF), D ** -0.5),
        "ffn_conv_w": nrm(ks[23], (DEPTH, CONV_WIDTH, 2 * D_FF), CONV_WIDTH ** -0.5),
        "ffn_conv_b": nrm(ks[24], (DEPTH, 2 * D_FF), 0.01),
        "ffn_w_down": nrm(ks[25], (DEPTH, D_FF, D), D_FF ** -0.5),
    }


def _fwd_reference(x, c, ada_w, ada_b, norm1_g, w_in, q_norm_g, k_norm_g,
              ssm_a_re, ssm_a_im, ssm_log_dt, ssm_b_re, ssm_b_im, ssm_c_re, ssm_c_im,
              ssm_d, glu_w, glu_b, attn_out_g, ssm_out_g, w_out, norm2_g,
              ffn_w_up, ffn_conv_w, ffn_conv_b, ffn_w_down):
    B, L, D = x.shape
    c_act = jax.nn.silu(c)
    for l in range(DEPTH):
        mod = c_act @ ada_w[l] + ada_b[l]
        sh1, sc1, g1, sh2, sc2, g2 = jnp.split(mod, N_MOD, axis=-1)

        h = _modulate(_rmsnorm(x, norm1_g[l]), sh1, sc1)
        p = h @ w_in[l]
        q = p[..., :ATTN_WIDTH].reshape(B, L, N_HEADS, HEAD_DIM)
        k = p[..., ATTN_WIDTH:2 * ATTN_WIDTH].reshape(B, L, N_HEADS, HEAD_DIM)
        v = p[..., 2 * ATTN_WIDTH:3 * ATTN_WIDTH].reshape(B, L, N_HEADS, HEAD_DIM)
        u = p[..., 3 * ATTN_WIDTH:]
        q = _rmsnorm(q, q_norm_g[l]).transpose(0, 2, 1, 3)
        k = _rmsnorm(k, k_norm_g[l]).transpose(0, 2, 1, 3)
        v = v.transpose(0, 2, 1, 3)
        o_attn = _stick_breaking(q, k, v)
        o_ssm = _s5(u, ssm_a_re[l], ssm_a_im[l], ssm_log_dt[l], ssm_b_re[l], ssm_b_im[l],
                    ssm_c_re[l], ssm_c_im[l], ssm_d[l], glu_w[l], glu_b[l])
        o = jnp.concatenate([_rmsnorm(o_attn, attn_out_g[l]),
                             _rmsnorm(o_ssm, ssm_out_g[l])], axis=-1)
        x = x + g1[:, None, :] * (o @ w_out[l])

        h = _modulate(_rmsnorm(x, norm2_g[l]), sh2, sc2)
        up = _causal_dwconv(h @ ffn_w_up[l], ffn_conv_w[l], ffn_conv_b[l])
        val, gate = jnp.split(up, 2, axis=-1)
        x = x + g2[:, None, :] * ((jax.nn.gelu(gate) * val) @ ffn_w_down[l])
    return x


import jax as _jax
import jax.numpy as _jnp

TWIN_FORMAT = 'train_step'
FWD_PARAMS = ['x', 'c', 'ada_w', 'ada_b', 'norm1_g', 'w_in', 'q_norm_g', 'k_norm_g', 'ssm_a_re', 'ssm_a_im', 'ssm_log_dt', 'ssm_b_re', 'ssm_b_im', 'ssm_c_re', 'ssm_c_im', 'ssm_d', 'glu_w', 'glu_b', 'attn_out_g', 'ssm_out_g', 'w_out', 'norm2_g', 'ffn_w_up', 'ffn_conv_w', 'ffn_conv_b', 'ffn_w_down']
TWIN_WEIGHTS = ['ada_w', 'ada_b', 'norm1_g', 'w_in', 'q_norm_g', 'k_norm_g', 'ssm_a_re', 'ssm_a_im', 'ssm_log_dt', 'ssm_b_re', 'ssm_b_im', 'ssm_c_re', 'ssm_c_im', 'ssm_d', 'glu_w', 'glu_b', 'attn_out_g', 'ssm_out_g', 'w_out', 'norm2_g', 'ffn_w_up', 'ffn_conv_w', 'ffn_conv_b', 'ffn_w_down']
TWIN_DIFF_INPUT = 'x'
TWIN_INPUTS = ['x', 'c', 'ada_w', 'ada_b', 'norm1_g', 'w_in', 'q_norm_g', 'k_norm_g', 'ssm_a_re', 'ssm_a_im', 'ssm_log_dt', 'ssm_b_re', 'ssm_b_im', 'ssm_c_re', 'ssm_c_im', 'ssm_d', 'glu_w', 'glu_b', 'attn_out_g', 'ssm_out_g', 'w_out', 'norm2_g', 'ffn_w_up', 'ffn_conv_w', 'ffn_conv_b', 'ffn_w_down', 'loss_target', 'm_ada_w', 'm_ada_b', 'm_norm1_g', 'm_w_in', 'm_q_norm_g', 'm_k_norm_g', 'm_ssm_a_re', 'm_ssm_a_im', 'm_ssm_log_dt', 'm_ssm_b_re', 'm_ssm_b_im', 'm_ssm_c_re', 'm_ssm_c_im', 'm_ssm_d', 'm_glu_w', 'm_glu_b', 'm_attn_out_g', 'm_ssm_out_g', 'm_w_out', 'm_norm2_g', 'm_ffn_w_up', 'm_ffn_conv_w', 'm_ffn_conv_b', 'm_ffn_w_down', 'v_ada_w', 'v_ada_b', 'v_norm1_g', 'v_w_in', 'v_q_norm_g', 'v_k_norm_g', 'v_ssm_a_re', 'v_ssm_a_im', 'v_ssm_log_dt', 'v_ssm_b_re', 'v_ssm_b_im', 'v_ssm_c_re', 'v_ssm_c_im', 'v_ssm_d', 'v_glu_w', 'v_glu_b', 'v_attn_out_g', 'v_ssm_out_g', 'v_w_out', 'v_norm2_g', 'v_ffn_w_up', 'v_ffn_conv_w', 'v_ffn_conv_b', 'v_ffn_w_down']
TWIN_OUTPUTS = ['loss', 'grad_x', 'grad_ada_w', 'grad_ada_b', 'grad_norm1_g', 'grad_w_in', 'grad_q_norm_g', 'grad_k_norm_g', 'grad_ssm_a_re', 'grad_ssm_a_im', 'grad_ssm_log_dt', 'grad_ssm_b_re', 'grad_ssm_b_im', 'grad_ssm_c_re', 'grad_ssm_c_im', 'grad_ssm_d', 'grad_glu_w', 'grad_glu_b', 'grad_attn_out_g', 'grad_ssm_out_g', 'grad_w_out', 'grad_norm2_g', 'grad_ffn_w_up', 'grad_ffn_conv_w', 'grad_ffn_conv_b', 'grad_ffn_w_down', 'delta_ada_w', 'delta_ada_b', 'delta_norm1_g', 'delta_w_in', 'delta_q_norm_g', 'delta_k_norm_g', 'delta_ssm_a_re', 'delta_ssm_a_im', 'delta_ssm_log_dt', 'delta_ssm_b_re', 'delta_ssm_b_im', 'delta_ssm_c_re', 'delta_ssm_c_im', 'delta_ssm_d', 'delta_glu_w', 'delta_glu_b', 'delta_attn_out_g', 'delta_ssm_out_g', 'delta_w_out', 'delta_norm2_g', 'delta_ffn_w_up', 'delta_ffn_conv_w', 'delta_ffn_conv_b', 'delta_ffn_w_down', 'new_m_ada_w', 'new_m_ada_b', 'new_m_norm1_g', 'new_m_w_in', 'new_m_q_norm_g', 'new_m_k_norm_g', 'new_m_ssm_a_re', 'new_m_ssm_a_im', 'new_m_ssm_log_dt', 'new_m_ssm_b_re', 'new_m_ssm_b_im', 'new_m_ssm_c_re', 'new_m_ssm_c_im', 'new_m_ssm_d', 'new_m_glu_w', 'new_m_glu_b', 'new_m_attn_out_g', 'new_m_ssm_out_g', 'new_m_w_out', 'new_m_norm2_g', 'new_m_ffn_w_up', 'new_m_ffn_conv_w', 'new_m_ffn_conv_b', 'new_m_ffn_w_down', 'new_v_ada_w', 'new_v_ada_b', 'new_v_norm1_g', 'new_v_w_in', 'new_v_q_norm_g', 'new_v_k_norm_g', 'new_v_ssm_a_re', 'new_v_ssm_a_im', 'new_v_ssm_log_dt', 'new_v_ssm_b_re', 'new_v_ssm_b_im', 'new_v_ssm_c_re', 'new_v_ssm_c_im', 'new_v_ssm_d', 'new_v_glu_w', 'new_v_glu_b', 'new_v_attn_out_g', 'new_v_ssm_out_g', 'new_v_w_out', 'new_v_norm2_g', 'new_v_ffn_w_up', 'new_v_ffn_conv_w', 'new_v_ffn_conv_b', 'new_v_ffn_w_down']
TWIN_LEAF_KINDS = {'loss': 'loss', 'grad_x': 'grad_x', 'grad_ada_w': 'grad_w', 'grad_ada_b': 'grad_w', 'grad_norm1_g': 'grad_w', 'grad_w_in': 'grad_w', 'grad_q_norm_g': 'grad_w', 'grad_k_norm_g': 'grad_w', 'grad_ssm_a_re': 'grad_w', 'grad_ssm_a_im': 'grad_w', 'grad_ssm_log_dt': 'grad_w', 'grad_ssm_b_re': 'grad_w', 'grad_ssm_b_im': 'grad_w', 'grad_ssm_c_re': 'grad_w', 'grad_ssm_c_im': 'grad_w', 'grad_ssm_d': 'grad_w', 'grad_glu_w': 'grad_w', 'grad_glu_b': 'grad_w', 'grad_attn_out_g': 'grad_w', 'grad_ssm_out_g': 'grad_w', 'grad_w_out': 'grad_w', 'grad_norm2_g': 'grad_w', 'grad_ffn_w_up': 'grad_w', 'grad_ffn_conv_w': 'grad_w', 'grad_ffn_conv_b': 'grad_w', 'grad_ffn_w_down': 'grad_w', 'delta_ada_w': 'delta_w', 'delta_ada_b': 'delta_w', 'delta_norm1_g': 'delta_w', 'delta_w_in': 'delta_w', 'delta_q_norm_g': 'delta_w', 'delta_k_norm_g': 'delta_w', 'delta_ssm_a_re': 'delta_w', 'delta_ssm_a_im': 'delta_w', 'delta_ssm_log_dt': 'delta_w', 'delta_ssm_b_re': 'delta_w', 'delta_ssm_b_im': 'delta_w', 'delta_ssm_c_re': 'delta_w', 'delta_ssm_c_im': 'delta_w', 'delta_ssm_d': 'delta_w', 'delta_glu_w': 'delta_w', 'delta_glu_b': 'delta_w', 'delta_attn_out_g': 'delta_w', 'delta_ssm_out_g': 'delta_w', 'delta_w_out': 'delta_w', 'delta_norm2_g': 'delta_w', 'delta_ffn_w_up': 'delta_w', 'delta_ffn_conv_w': 'delta_w', 'delta_ffn_conv_b': 'delta_w', 'delta_ffn_w_down': 'delta_w', 'new_m_ada_w': 'new_m', 'new_m_ada_b': 'new_m', 'new_m_norm1_g': 'new_m', 'new_m_w_in': 'new_m', 'new_m_q_norm_g': 'new_m', 'new_m_k_norm_g': 'new_m', 'new_m_ssm_a_re': 'new_m', 'new_m_ssm_a_im': 'new_m', 'new_m_ssm_log_dt': 'new_m', 'new_m_ssm_b_re': 'new_m', 'new_m_ssm_b_im': 'new_m', 'new_m_ssm_c_re': 'new_m', 'new_m_ssm_c_im': 'new_m', 'new_m_ssm_d': 'new_m', 'new_m_glu_w': 'new_m', 'new_m_glu_b': 'new_m', 'new_m_attn_out_g': 'new_m', 'new_m_ssm_out_g': 'new_m', 'new_m_w_out': 'new_m', 'new_m_norm2_g': 'new_m', 'new_m_ffn_w_up': 'new_m', 'new_m_ffn_conv_w': 'new_m', 'new_m_ffn_conv_b': 'new_m', 'new_m_ffn_w_down': 'new_m', 'new_v_ada_w': 'new_v', 'new_v_ada_b': 'new_v', 'new_v_norm1_g': 'new_v', 'new_v_w_in': 'new_v', 'new_v_q_norm_g': 'new_v', 'new_v_k_norm_g': 'new_v', 'new_v_ssm_a_re': 'new_v', 'new_v_ssm_a_im': 'new_v', 'new_v_ssm_log_dt': 'new_v', 'new_v_ssm_b_re': 'new_v', 'new_v_ssm_b_im': 'new_v', 'new_v_ssm_c_re': 'new_v', 'new_v_ssm_c_im': 'new_v', 'new_v_ssm_d': 'new_v', 'new_v_glu_w': 'new_v', 'new_v_glu_b': 'new_v', 'new_v_attn_out_g': 'new_v', 'new_v_ssm_out_g': 'new_v', 'new_v_w_out': 'new_v', 'new_v_norm2_g': 'new_v', 'new_v_ffn_w_up': 'new_v', 'new_v_ffn_conv_w': 'new_v', 'new_v_ffn_conv_b': 'new_v', 'new_v_ffn_w_down': 'new_v'}


def _forward(args):
    return _fwd_reference(*[args[k] for k in FWD_PARAMS])


def _output_shape():
    out = _jax.eval_shape(lambda: _forward(_fwd_setup_inputs(0)))
    return out.shape, out.dtype

N_MICROBATCH = 1
ADAM_LR = 0.001
ADAM_B1 = 0.9
ADAM_B2 = 0.999
ADAM_EPS = 1e-08
ADAM_WD = 0.01
ADAM_STEP = 10
PER_EXAMPLE_BATCH_AXIS = {'x': 0, 'c': 0, 'loss_target': 0}
SHARED_INPUTS = []
_WEIGHT_DTYPES = {'ada_w': _jnp.float32, 'ada_b': _jnp.float32, 'norm1_g': _jnp.float32, 'w_in': _jnp.float32, 'q_norm_g': _jnp.float32, 'k_norm_g': _jnp.float32, 'ssm_a_re': _jnp.float32, 'ssm_a_im': _jnp.float32, 'ssm_log_dt': _jnp.float32, 'ssm_b_re': _jnp.float32, 'ssm_b_im': _jnp.float32, 'ssm_c_re': _jnp.float32, 'ssm_c_im': _jnp.float32, 'ssm_d': _jnp.float32, 'glu_w': _jnp.float32, 'glu_b': _jnp.float32, 'attn_out_g': _jnp.float32, 'ssm_out_g': _jnp.float32, 'w_out': _jnp.float32, 'norm2_g': _jnp.float32, 'ffn_w_up': _jnp.float32, 'ffn_conv_w': _jnp.float32, 'ffn_conv_b': _jnp.float32, 'ffn_w_down': _jnp.float32}
MOMENT_SCALE = {'ada_w': 1.864629e+00, 'ada_b': 3.566594e+00, 'norm1_g': 2.801229e-01, 'w_in': 4.655143e-01, 'q_norm_g': 1.323089e-01, 'k_norm_g': 1.328977e-01, 'ssm_a_re': 5.180486e-02, 'ssm_a_im': 4.438469e-02, 'ssm_log_dt': 5.285022e+00, 'ssm_b_re': 3.587910e-02, 'ssm_b_im': 3.314422e-02, 'ssm_c_re': 6.247753e-02, 'ssm_c_im': 6.261109e-02, 'ssm_d': 1.163431e+00, 'glu_w': 3.194724e-01, 'glu_b': 4.538264e-01, 'attn_out_g': 3.224632e+00, 'ssm_out_g': 6.789931e+00, 'w_out': 1.012367e+00, 'norm2_g': 3.990959e+00, 'ffn_w_up': 1.926298e-01, 'ffn_conv_w': 6.132269e-01, 'ffn_conv_b': 6.051227e-01, 'ffn_w_down': 2.153050e-01}


def _to_microbatches(a, axis):
    t = _jnp.moveaxis(a, axis, 0)
    t = t.reshape((N_MICROBATCH, t.shape[0] // N_MICROBATCH) + t.shape[1:])
    return _jnp.moveaxis(t, 1, axis + 1)


def setup_inputs(seed: int = 0) -> dict:
    inp = _fwd_setup_inputs(seed)
    key = _jax.random.fold_in(_jax.random.key(seed), 7919)
    shape, _ = _output_shape()
    out = dict(inp)
    out["loss_target"] = _jax.random.normal(_jax.random.fold_in(key, 0), shape, _jnp.float32)
    for i, name in enumerate(TWIN_WEIGHTS):
        w = inp[name].astype(_jnp.float32)
        if MOMENT_SCALE is None:
            s = _jnp.sqrt(_jnp.mean(_jnp.square(w)) + 1e-30)
        else:
            s = MOMENT_SCALE[name]
        km, kv = _jax.random.split(_jax.random.fold_in(key, i + 1))
        out[name] = w
        out["m_" + name] = s * _jax.random.normal(km, w.shape, _jnp.float32)
        out["v_" + name] = (s * s) * _jax.random.uniform(kv, w.shape, _jnp.float32, 0.5, 1.5)
    if N_MICROBATCH > 1:
        for name, axis in PER_EXAMPLE_BATCH_AXIS.items():
            out[name] = _to_microbatches(out[name], axis)
    return {'x': out['x'], 'c': out['c'], 'ada_w': out['ada_w'], 'ada_b': out['ada_b'], 'norm1_g': out['norm1_g'], 'w_in': out['w_in'], 'q_norm_g': out['q_norm_g'], 'k_norm_g': out['k_norm_g'], 'ssm_a_re': out['ssm_a_re'], 'ssm_a_im': out['ssm_a_im'], 'ssm_log_dt': out['ssm_log_dt'], 'ssm_b_re': out['ssm_b_re'], 'ssm_b_im': out['ssm_b_im'], 'ssm_c_re': out['ssm_c_re'], 'ssm_c_im': out['ssm_c_im'], 'ssm_d': out['ssm_d'], 'glu_w': out['glu_w'], 'glu_b': out['glu_b'], 'attn_out_g': out['attn_out_g'], 'ssm_out_g': out['ssm_out_g'], 'w_out': out['w_out'], 'norm2_g': out['norm2_g'], 'ffn_w_up': out['ffn_w_up'], 'ffn_conv_w': out['ffn_conv_w'], 'ffn_conv_b': out['ffn_conv_b'], 'ffn_w_down': out['ffn_w_down'], 'loss_target': out['loss_target'], 'm_ada_w': out['m_ada_w'], 'm_ada_b': out['m_ada_b'], 'm_norm1_g': out['m_norm1_g'], 'm_w_in': out['m_w_in'], 'm_q_norm_g': out['m_q_norm_g'], 'm_k_norm_g': out['m_k_norm_g'], 'm_ssm_a_re': out['m_ssm_a_re'], 'm_ssm_a_im': out['m_ssm_a_im'], 'm_ssm_log_dt': out['m_ssm_log_dt'], 'm_ssm_b_re': out['m_ssm_b_re'], 'm_ssm_b_im': out['m_ssm_b_im'], 'm_ssm_c_re': out['m_ssm_c_re'], 'm_ssm_c_im': out['m_ssm_c_im'], 'm_ssm_d': out['m_ssm_d'], 'm_glu_w': out['m_glu_w'], 'm_glu_b': out['m_glu_b'], 'm_attn_out_g': out['m_attn_out_g'], 'm_ssm_out_g': out['m_ssm_out_g'], 'm_w_out': out['m_w_out'], 'm_norm2_g': out['m_norm2_g'], 'm_ffn_w_up': out['m_ffn_w_up'], 'm_ffn_conv_w': out['m_ffn_conv_w'], 'm_ffn_conv_b': out['m_ffn_conv_b'], 'm_ffn_w_down': out['m_ffn_w_down'], 'v_ada_w': out['v_ada_w'], 'v_ada_b': out['v_ada_b'], 'v_norm1_g': out['v_norm1_g'], 'v_w_in': out['v_w_in'], 'v_q_norm_g': out['v_q_norm_g'], 'v_k_norm_g': out['v_k_norm_g'], 'v_ssm_a_re': out['v_ssm_a_re'], 'v_ssm_a_im': out['v_ssm_a_im'], 'v_ssm_log_dt': out['v_ssm_log_dt'], 'v_ssm_b_re': out['v_ssm_b_re'], 'v_ssm_b_im': out['v_ssm_b_im'], 'v_ssm_c_re': out['v_ssm_c_re'], 'v_ssm_c_im': out['v_ssm_c_im'], 'v_ssm_d': out['v_ssm_d'], 'v_glu_w': out['v_glu_w'], 'v_glu_b': out['v_glu_b'], 'v_attn_out_g': out['v_attn_out_g'], 'v_ssm_out_g': out['v_ssm_out_g'], 'v_w_out': out['v_w_out'], 'v_norm2_g': out['v_norm2_g'], 'v_ffn_w_up': out['v_ffn_w_up'], 'v_ffn_conv_w': out['v_ffn_conv_w'], 'v_ffn_conv_b': out['v_ffn_conv_b'], 'v_ffn_w_down': out['v_ffn_w_down']}


def _loss(weights, diff, rest, loss_target):
    with _jax.named_scope("forward"):
        args = {**rest, TWIN_DIFF_INPUT: diff, **{k: w.astype(_WEIGHT_DTYPES[k]) for k, w in weights.items()}}
        y = _forward(args)
    with _jax.named_scope("loss_head"):
        err = _jnp.square(y.astype(_jnp.float32) - loss_target)
        return 0.5 * _jnp.sum(_jnp.mean(err, axis=-1)) if err.ndim else 0.5 * err


def _adamw(w, g, m, v):
    m = ADAM_B1 * m + (1.0 - ADAM_B1) * g
    v = ADAM_B2 * v + (1.0 - ADAM_B2) * _jnp.square(g)
    m_hat = m / (1.0 - ADAM_B1 ** ADAM_STEP)
    v_hat = v / (1.0 - ADAM_B2 ** ADAM_STEP)
    delta = -ADAM_LR * (m_hat / (_jnp.sqrt(v_hat) + ADAM_EPS) + ADAM_WD * w)
    return delta, m, v


def reference(x, c, ada_w, ada_b, norm1_g, w_in, q_norm_g, k_norm_g, ssm_a_re, ssm_a_im, ssm_log_dt, ssm_b_re, ssm_b_im, ssm_c_re, ssm_c_im, ssm_d, glu_w, glu_b, attn_out_g, ssm_out_g, w_out, norm2_g, ffn_w_up, ffn_conv_w, ffn_conv_b, ffn_w_down, loss_target, m_ada_w, m_ada_b, m_norm1_g, m_w_in, m_q_norm_g, m_k_norm_g, m_ssm_a_re, m_ssm_a_im, m_ssm_log_dt, m_ssm_b_re, m_ssm_b_im, m_ssm_c_re, m_ssm_c_im, m_ssm_d, m_glu_w, m_glu_b, m_attn_out_g, m_ssm_out_g, m_w_out, m_norm2_g, m_ffn_w_up, m_ffn_conv_w, m_ffn_conv_b, m_ffn_w_down, v_ada_w, v_ada_b, v_norm1_g, v_w_in, v_q_norm_g, v_k_norm_g, v_ssm_a_re, v_ssm_a_im, v_ssm_log_dt, v_ssm_b_re, v_ssm_b_im, v_ssm_c_re, v_ssm_c_im, v_ssm_d, v_glu_w, v_glu_b, v_attn_out_g, v_ssm_out_g, v_w_out, v_norm2_g, v_ffn_w_up, v_ffn_conv_w, v_ffn_conv_b, v_ffn_w_down):
    given = dict(x=x, c=c, ada_w=ada_w, ada_b=ada_b, norm1_g=norm1_g, w_in=w_in, q_norm_g=q_norm_g, k_norm_g=k_norm_g, ssm_a_re=ssm_a_re, ssm_a_im=ssm_a_im, ssm_log_dt=ssm_log_dt, ssm_b_re=ssm_b_re, ssm_b_im=ssm_b_im, ssm_c_re=ssm_c_re, ssm_c_im=ssm_c_im, ssm_d=ssm_d, glu_w=glu_w, glu_b=glu_b, attn_out_g=attn_out_g, ssm_out_g=ssm_out_g, w_out=w_out, norm2_g=norm2_g, ffn_w_up=ffn_w_up, ffn_conv_w=ffn_conv_w, ffn_conv_b=ffn_conv_b, ffn_w_down=ffn_w_down, loss_target=loss_target, m_ada_w=m_ada_w, m_ada_b=m_ada_b, m_norm1_g=m_norm1_g, m_w_in=m_w_in, m_q_norm_g=m_q_norm_g, m_k_norm_g=m_k_norm_g, m_ssm_a_re=m_ssm_a_re, m_ssm_a_im=m_ssm_a_im, m_ssm_log_dt=m_ssm_log_dt, m_ssm_b_re=m_ssm_b_re, m_ssm_b_im=m_ssm_b_im, m_ssm_c_re=m_ssm_c_re, m_ssm_c_im=m_ssm_c_im, m_ssm_d=m_ssm_d, m_glu_w=m_glu_w, m_glu_b=m_glu_b, m_attn_out_g=m_attn_out_g, m_ssm_out_g=m_ssm_out_g, m_w_out=m_w_out, m_norm2_g=m_norm2_g, m_ffn_w_up=m_ffn_w_up, m_ffn_conv_w=m_ffn_conv_w, m_ffn_conv_b=m_ffn_conv_b, m_ffn_w_down=m_ffn_w_down, v_ada_w=v_ada_w, v_ada_b=v_ada_b, v_norm1_g=v_norm1_g, v_w_in=v_w_in, v_q_norm_g=v_q_norm_g, v_k_norm_g=v_k_norm_g, v_ssm_a_re=v_ssm_a_re, v_ssm_a_im=v_ssm_a_im, v_ssm_log_dt=v_ssm_log_dt, v_ssm_b_re=v_ssm_b_re, v_ssm_b_im=v_ssm_b_im, v_ssm_c_re=v_ssm_c_re, v_ssm_c_im=v_ssm_c_im, v_ssm_d=v_ssm_d, v_glu_w=v_glu_w, v_glu_b=v_glu_b, v_attn_out_g=v_attn_out_g, v_ssm_out_g=v_ssm_out_g, v_w_out=v_w_out, v_norm2_g=v_norm2_g, v_ffn_w_up=v_ffn_w_up, v_ffn_conv_w=v_ffn_conv_w, v_ffn_conv_b=v_ffn_conv_b, v_ffn_w_down=v_ffn_w_down)
    weights = {n: given[n] for n in TWIN_WEIGHTS}
    shared = {n: given[n] for n in SHARED_INPUTS}
    per_example = {n: given[n] for n in ['x', 'c']}
    grad_fn = _jax.value_and_grad(_loss, argnums=(0, 1))

    def one_microbatch(ex, loss_target):
        ex = dict(ex)
        diff = ex.pop(TWIN_DIFF_INPUT)
        return grad_fn(weights, diff, {**shared, **ex}, loss_target)

    if N_MICROBATCH == 1:
        loss, (grad_w, grad_x) = one_microbatch(per_example, given["loss_target"])
    else:
        def body(carry, xs):
            loss_sum, grad_sum = carry
            l_k, (gw_k, gx_k) = one_microbatch(xs[0], xs[1])
            with _jax.named_scope("update"):
                return (loss_sum + l_k, _jax.tree.map(_jnp.add, grad_sum, gw_k)), gx_k

        init = (_jnp.zeros((), _jnp.float32), _jax.tree.map(_jnp.zeros_like, weights))
        (loss, grad_w), grad_x = _jax.lax.scan(body, init, (per_example, given["loss_target"]))
    with _jax.named_scope("update"):
        delta_w, new_m, new_v = {}, {}, {}
        for n in TWIN_WEIGHTS:
            delta_w[n], new_m[n], new_v[n] = _adamw(weights[n], grad_w[n], given["m_" + n], given["v_" + n])
    return (loss, grad_x, *[grad_w[n] for n in TWIN_WEIGHTS], *[delta_w[n] for n in TWIN_WEIGHTS],
            *[new_m[n] for n in TWIN_WEIGHTS], *[new_v[n] for n in TWIN_WEIGHTS])
```

```python
import functools
import math

import jax
import jax.numpy as jnp
from jax import lax
from jax.experimental import pallas as pl
from jax.experimental.pallas import tpu as pltpu

F32 = jnp.float32
BF16 = jnp.bfloat16
NDEV = 8
LANES = 128
HEAD_DIM = 64
SSM_GROUP = 16
SSM_STATE = 64
GROUPS_PER_TILE = LANES // SSM_GROUP
STATE_TILE = GROUPS_PER_TILE * SSM_STATE
N_MOD = 6
EPS = 1e-6
ADAM_LR, ADAM_B1, ADAM_B2, ADAM_EPS, ADAM_WD, ADAM_STEP = 0.001, 0.9, 0.999, 1e-08, 0.01, 10
VMEM_LIMIT = 48 * 1024 * 1024
MESH_IDS = pl.DeviceIdType.MESH

NN = (((1,), (0,)), ((), ()))
NT = (((1,), (1,)), ((), ()))
TN = (((0,), (0,)), ((), ()))


def _dot(a, b, dims=NN):
    return lax.dot_general(a, b, dims, preferred_element_type=F32)


def _pcall(body, *, name, out_shape, in_specs, out_specs, grid=(), scratch=()):
    return pl.pallas_call(
        body, name=name, grid=grid, in_specs=in_specs, out_specs=out_specs, out_shape=out_shape,
        scratch_shapes=list(scratch),
        compiler_params=pltpu.CompilerParams(vmem_limit_bytes=VMEM_LIMIT))


def _row_tile(n, want=512):
    t = min(n, want)
    assert n % t == 0
    return t


def _my_index():
    return 4 * lax.axis_index("x") + 2 * lax.axis_index("y") + lax.axis_index("c")


def _exchange(arrs, axes, scatter, name):
    n = len(arrs)
    out_shape = []
    for a, ax in zip(arrs, axes):
        shp = a.shape if scatter else a.shape[:ax] + (NDEV,) + a.shape[ax:]
        out_shape.append(jax.ShapeDtypeStruct(shp, a.dtype))

    def body(*refs):
        ins, outs = refs[:n], refs[n:2 * n]
        send_sems, recv_sems, local_sems = refs[2 * n:]
        x, y, c = lax.axis_index("x"), lax.axis_index("y"), lax.axis_index("c")
        me = 4 * x + 2 * y + c

        def slab(ref, ax, idx):
            return ref.at[(slice(None),) * ax + (idx,)]

        local = []
        for a in range(n):
            src = slab(ins[a], axes[a], me) if scatter else ins[a]
            cp = pltpu.make_async_copy(src, slab(outs[a], axes[a], me), local_sems.at[a])
            cp.start()
            local.append(cp)
        sends, recvs = [], []
        for k in range(1, NDEV):
            px = 1 - x if k & 4 else x
            py = 1 - y if k & 2 else y
            pc = 1 - c if k & 1 else c
            peer = 4 * px + 2 * py + pc
            for a in range(n):
                src = slab(ins[a], axes[a], peer) if scatter else ins[a]
                common = dict(send_sem=send_sems.at[k - 1, a], recv_sem=recv_sems.at[k - 1, a],
                              device_id=(px, py, pc), device_id_type=MESH_IDS)
                snd = pltpu.make_async_remote_copy(src_ref=src, dst_ref=slab(outs[a], axes[a], me), **common)
                snd.start()
                sends.append(snd)
                recvs.append(pltpu.make_async_remote_copy(
                    src_ref=src, dst_ref=slab(outs[a], axes[a], peer), **common))
        for r in recvs:
            r.wait_recv()
        for s in sends:
            s.wait_send()
        for cp in local:
            cp.wait()

    hbm = pl.BlockSpec(memory_space=pltpu.HBM)
    outs = pl.pallas_call(
        body, name=name, out_shape=out_shape, in_specs=[hbm] * n, out_specs=[hbm] * n,
        scratch_shapes=[pltpu.SemaphoreType.DMA((NDEV - 1, n)), pltpu.SemaphoreType.DMA((NDEV - 1, n)),
                        pltpu.SemaphoreType.DMA((n,))],
    )(*arrs)
    return list(outs)


def _matmul(name, a, b, *, dims, grid, a_spec, b_spec, out_shape, out_specs, kaxis=None, acc_shape=None,
            extra=(), extra_specs=(), epilogue=None):
    nk = grid[kaxis] if kaxis is not None else 1
    ne = len(extra)
    multi = isinstance(out_shape, (list, tuple))
    n_out = len(out_shape) if multi else 1

    def body(*refs):
        a_ref, b_ref = refs[0], refs[1]
        ex = refs[2:2 + ne]
        outs = refs[2 + ne:2 + ne + n_out]

        def write(res):
            vals = epilogue(res, *[e[...] for e in ex]) if epilogue is not None else (res,)
            for o, v in zip(outs, vals):
                o[...] = v.astype(o.dtype)

        part = _dot(a_ref[...].astype(BF16), b_ref[...].astype(BF16), dims)
        if nk == 1:
            write(part)
        else:
            acc = refs[-1]
            k = pl.program_id(kaxis)

            @pl.when(k == 0)
            def _():
                acc[...] = part

            @pl.when(k > 0)
            def _():
                acc[...] += part

            @pl.when(k == nk - 1)
            def _():
                write(acc[...])

    scratch = [pltpu.VMEM(acc_shape, F32)] if nk > 1 else []
    return _pcall(body, name=name, grid=grid, in_specs=[a_spec, b_spec, *extra_specs],
                  out_specs=out_specs, out_shape=out_shape, scratch=scratch)(a, b, *extra)


def _rms(x, g):
    inv = lax.rsqrt(jnp.mean(x * x, axis=-1, keepdims=True) + EPS)
    return x * inv * g


def _lnmod_math(x, g, sh, sc):
    return _rms(x, g) * (1.0 + sc) + sh


def _qkn_math(p, g):
    m0 = lax.broadcasted_iota(jnp.int32, (1, LANES), 1) < HEAD_DIM
    sq = p * p
    s0 = jnp.sum(jnp.where(m0, sq, 0.0), axis=-1, keepdims=True)
    s1 = jnp.sum(jnp.where(m0, 0.0, sq), axis=-1, keepdims=True)
    inv = jnp.where(m0, lax.rsqrt(s0 / HEAD_DIM + EPS), lax.rsqrt(s1 / HEAD_DIM + EPS))
    return p * inv * g


def _glu_math(val, gate):
    return jax.nn.gelu(gate) * val


def _accumulate(ref, val, first):
    @pl.when(first)
    def _():
        ref[...] = val

    @pl.when(jnp.logical_not(first))
    def _():
        ref[...] += val


def _lnmod_fwd(x, g, sh, sc):
    L, D = x.shape
    tm = _row_tile(L)

    def body(x_ref, g_ref, sh_ref, sc_ref, h_ref):
        h_ref[...] = _lnmod_math(x_ref[...], g_ref[...], sh_ref[...], sc_ref[...]).astype(BF16)

    row = pl.BlockSpec((tm, D), lambda i: (i, 0))
    vec = pl.BlockSpec((1, D), lambda i: (0, 0))
    return _pcall(body, name="lnmod_fwd", grid=(L // tm,), in_specs=[row, vec, vec, vec], out_specs=row,
                  out_shape=jax.ShapeDtypeStruct((L, D), BF16))(x, g, sh, sc)


def _lnmod_bwd(dh, x, g, sh, sc, dres):
    L, D = x.shape
    tm = _row_tile(L)

    def body(dh_ref, x_ref, g_ref, sh_ref, sc_ref, res_ref, dx_ref, dg_ref, dsh_ref, dsc_ref):
        _, vjp = jax.vjp(_lnmod_math, x_ref[...], g_ref[...], sh_ref[...], sc_ref[...])
        dx, dg, dsh, dsc = vjp(dh_ref[...])
        dx_ref[...] = dx + res_ref[...]
        first = pl.program_id(0) == 0
        _accumulate(dg_ref, dg, first)
        _accumulate(dsh_ref, dsh, first)
        _accumulate(dsc_ref, dsc, first)

    row = pl.BlockSpec((tm, D), lambda i: (i, 0))
    vec = pl.BlockSpec((1, D), lambda i: (0, 0))
    vs = jax.ShapeDtypeStruct((1, D), F32)
    return _pcall(body, name="lnmod_bwd", grid=(L // tm,), in_specs=[row, row, vec, vec, vec, row],
                  out_specs=[row, vec, vec, vec],
                  out_shape=[jax.ShapeDtypeStruct((L, D), F32), vs, vs, vs])(dh, x, g, sh, sc, dres)


def _gate_bwd(dx, a, gate):
    L, D = dx.shape
    tm = _row_tile(L)

    def body(dx_ref, a_ref, g_ref, da_ref, dg_ref):
        dxv = dx_ref[...]
        da_ref[...] = (g_ref[...] * dxv).astype(BF16)
        _accumulate(dg_ref, jnp.sum(dxv * a_ref[...], axis=0, keepdims=True), pl.program_id(0) == 0)

    row = pl.BlockSpec((tm, D), lambda i: (i, 0))
    vec = pl.BlockSpec((1, D), lambda i: (0, 0))
    return _pcall(body, name="gate_bwd", grid=(L // tm,), in_specs=[row, row, vec], out_specs=[row, vec],
                  out_shape=[jax.ShapeDtypeStruct((L, D), BF16), jax.ShapeDtypeStruct((1, D), F32)])(dx, a, gate)


def _qknorm_fwd(p, gqk, AW):
    L = p.shape[0]
    tm = _row_tile(L)
    ncol = 2 * AW // LANES

    def body(p_ref, g_ref, o_ref):
        o_ref[...] = _qkn_math(p_ref[...], g_ref[...]).astype(BF16)

    blk = pl.BlockSpec((tm, LANES), lambda i, j: (i, j))
    vec = pl.BlockSpec((1, LANES), lambda i, j: (0, j))
    return _pcall(body, name="qknorm_fwd", grid=(L // tm, ncol), in_specs=[blk, vec], out_specs=blk,
                  out_shape=jax.ShapeDtypeStruct((L, 2 * AW), BF16))(p, gqk)


def _qknorm_bwd(dqk, p, gqk, AW):
    L = p.shape[0]
    tm = _row_tile(L)
    ncol = 2 * AW // LANES

    def body(d_ref, p_ref, g_ref, dp_ref, dg_ref):
        _, vjp = jax.vjp(_qkn_math, p_ref[...], g_ref[...])
        dp, dg = vjp(d_ref[...])
        dp_ref[...] = dp
        _accumulate(dg_ref, dg, pl.program_id(1) == 0)

    blk = pl.BlockSpec((tm, LANES), lambda j, i: (i, j))
    vec = pl.BlockSpec((1, LANES), lambda j, i: (0, j))
    return _pcall(body, name="qknorm_bwd", grid=(ncol, L // tm), in_specs=[blk, blk, vec], out_specs=[blk, vec],
                  out_shape=[jax.ShapeDtypeStruct((L, 2 * AW), F32),
                             jax.ShapeDtypeStruct((1, 2 * AW), F32)])(dqk, p, gqk)


def _outnorm_fwd(oa, os_, ga, gs):
    L, AW = oa.shape
    SW = os_.shape[1]
    tm = _row_tile(L)

    def body(oa_ref, os_ref, ga_ref, gs_ref, o_ref):
        o_ref[:, :AW] = _rms(oa_ref[...], ga_ref[...]).astype(BF16)
        o_ref[:, AW:] = _rms(os_ref[...], gs_ref[...]).astype(BF16)

    ra = pl.BlockSpec((tm, AW), lambda i: (i, 0))
    rs = pl.BlockSpec((tm, SW), lambda i: (i, 0))
    va = pl.BlockSpec((1, AW), lambda i: (0, 0))
    vs = pl.BlockSpec((1, SW), lambda i: (0, 0))
    ro = pl.BlockSpec((tm, AW + SW), lambda i: (i, 0))
    return _pcall(body, name="outnorm_fwd", grid=(L // tm,), in_specs=[ra, rs, va, vs], out_specs=ro,
                  out_shape=jax.ShapeDtypeStruct((L, AW + SW), BF16))(oa, os_, ga, gs)


def _outnorm_bwd(do, oa, os_, ga, gs):
    L, AW = oa.shape
    SW = os_.shape[1]
    tm = _row_tile(L)

    def body(do_ref, oa_ref, os_ref, ga_ref, gs_ref, doa_ref, dos_ref, dga_ref, dgs_ref):
        first = pl.program_id(0) == 0
        _, vjp_a = jax.vjp(_rms, oa_ref[...], ga_ref[...])
        doa, dga = vjp_a(do_ref[:, :AW])
        _, vjp_s = jax.vjp(_rms, os_ref[...], gs_ref[...])
        dos, dgs = vjp_s(do_ref[:, AW:])
        doa_ref[...] = doa
        dos_ref[...] = dos
        _accumulate(dga_ref, dga, first)
        _accumulate(dgs_ref, dgs, first)

    ra = pl.BlockSpec((tm, AW), lambda i: (i, 0))
    rs = pl.BlockSpec((tm, SW), lambda i: (i, 0))
    va = pl.BlockSpec((1, AW), lambda i: (0, 0))
    vs = pl.BlockSpec((1, SW), lambda i: (0, 0))
    ro = pl.BlockSpec((tm, AW + SW), lambda i: (i, 0))
    return _pcall(body, name="outnorm_bwd", grid=(L // tm,), in_specs=[ro, ra, rs, va, vs],
                  out_specs=[ra, rs, va, vs],
                  out_shape=[jax.ShapeDtypeStruct((L, AW), F32), jax.ShapeDtypeStruct((L, SW), F32),
                             jax.ShapeDtypeStruct((1, AW), F32), jax.ShapeDtypeStruct((1, SW), F32)])(
                                 do, oa, os_, ga, gs)


def _softplus_neg_abs(z):
    return jnp.log1p(jnp.exp(-jnp.abs(z)))


def _split_dot(x, m):
    hi = x.astype(BF16)
    lo = (x - hi.astype(F32)).astype(BF16)
    return _dot(hi, m) + _dot(lo, m)


def _attn_masks(B):
    row = lax.broadcasted_iota(jnp.int32, (B, B), 0)
    col = lax.broadcasted_iota(jnp.int32, (B, B), 1)
    strict = col < row
    upper = jnp.where(row > col, 1.0, 0.0).astype(BF16)
    lower = jnp.where(row < col, 1.0, 0.0).astype(BF16)
    return strict, upper, lower


def _attn_fwd(qk, p, AW, B):
    L = qk.shape[0]
    HP = AW // LANES
    nb = L // B

    def body(q_ref, k_ref, v_ref, o_ref, tot_ref):
        i = pl.program_id(1)
        m0 = lax.broadcasted_iota(jnp.int32, (1, LANES), 1) < HEAD_DIM
        strict, upper, _ = _attn_masks(B)
        q = q_ref[...] * 0.125
        zq = jnp.zeros_like(q)
        qh = (jnp.where(m0, q, zq), jnp.where(m0, zq, q))

        def block(j, carry, diag):
            r0, r1, acc = carry
            start = pl.multiple_of(j * B, B)
            kj = k_ref[pl.ds(start, B), :]
            vj = v_ref[pl.ds(start, B), :].astype(BF16)
            pv, rs = [], []
            for h, r in ((0, r0), (1, r1)):
                z = _dot(qh[h], kj, NT)
                sp = _softplus_neg_abs(z)
                lb = jnp.minimum(z, 0.0) - sp
                l1 = -jnp.maximum(z, 0.0) - sp
                if diag:
                    l1 = jnp.where(strict, l1, 0.0)
                tail = _split_dot(l1, upper)
                w = jnp.exp(lb + tail + r)
                if diag:
                    w = jnp.where(strict, w, 0.0)
                pv.append(_dot(w.astype(BF16), vj))
                rs.append(r + jnp.sum(l1, axis=-1, keepdims=True))
            return rs[0], rs[1], acc + jnp.where(m0, pv[0], pv[1])

        zero = jnp.zeros((B, 1), F32)
        carry = block(i, (zero, zero, jnp.zeros((B, LANES), F32)), True)
        carry = lax.fori_loop(0, i, lambda t, c: block(i - 1 - t, c, False), carry)
        r0, r1, acc = carry
        o_ref[...] = acc
        tot_ref[...] = jnp.where(m0, r0, r1)

    qspec = pl.BlockSpec((B, LANES), lambda hp, i: (i, hp))
    kspec = pl.BlockSpec((L, LANES), lambda hp, i: (0, HP + hp))
    vspec = pl.BlockSpec((L, LANES), lambda hp, i: (0, 2 * HP + hp))
    ospec = pl.BlockSpec((B, LANES), lambda hp, i: (i, hp))
    shp = jax.ShapeDtypeStruct((L, AW), F32)
    return _pcall(body, name="attn_fwd", grid=(HP, nb), in_specs=[qspec, kspec, vspec],
                  out_specs=[ospec, ospec], out_shape=[shp, shp])(qk, qk, p)


def _attn_bwd(qk, p, do, tot, AW, B):
    L = qk.shape[0]
    HP = AW // LANES
    nb = L // B

    def body(q_ref, k_ref, v_ref, do_ref, tot_ref, dq_ref, dk_ref, dv_ref):
        i = pl.program_id(1)
        m0 = lax.broadcasted_iota(jnp.int32, (1, LANES), 1) < HEAD_DIM
        strict, upper, lower = _attn_masks(B)

        @pl.when(i == 0)
        def _():
            dk_ref[...] = jnp.zeros_like(dk_ref)
            dv_ref[...] = jnp.zeros_like(dv_ref)

        q = q_ref[...] * 0.125
        zq = jnp.zeros_like(q)
        qh = (jnp.where(m0, q, zq), jnp.where(m0, zq, q))
        dob = do_ref[...].astype(BF16)
        doh = (jnp.where(m0, dob, zq), jnp.where(m0, zq, dob))
        tv = tot_ref[...]
        th = (jnp.sum(jnp.where(m0, tv, 0.0), axis=-1, keepdims=True) / HEAD_DIM,
              jnp.sum(jnp.where(m0, 0.0, tv), axis=-1, keepdims=True) / HEAD_DIM)

        def block(j, carry, diag):
            pls, es, dq = carry
            start = pl.multiple_of(j * B, B)
            kj = k_ref[pl.ds(start, B), :]
            vj = v_ref[pl.ds(start, B), :].astype(BF16)
            zk = jnp.zeros_like(kj)
            kh = (jnp.where(m0, kj, zk), jnp.where(m0, zk, kj))
            new_pl, new_e = [], []
            dk_blk = jnp.zeros((B, LANES), F32)
            dv_blk = jnp.zeros((B, LANES), F32)
            for h in (0, 1):
                z = _dot(qh[h], kj, NT)
                sp = _softplus_neg_abs(z)
                lb = jnp.minimum(z, 0.0) - sp
                l1 = -jnp.maximum(z, 0.0) - sp
                if diag:
                    l1 = jnp.where(strict, l1, 0.0)
                bsum = jnp.sum(l1, axis=-1, keepdims=True)
                right = th[h] - pls[h] - bsum
                tail = _split_dot(l1, upper)
                w = jnp.exp(lb + tail + right)
                if diag:
                    w = jnp.where(strict, w, 0.0)
                dw = _dot(doh[h], vj, NT)
                e = dw * w
                dl1 = _split_dot(e, lower) + es[h]
                beta = jnp.exp(lb)
                dz = e * (1.0 - beta) - dl1 * beta
                if diag:
                    dz = jnp.where(strict, dz, 0.0)
                dzb = dz.astype(BF16)
                wb = w.astype(BF16)
                dq = dq + _dot(dzb, kh[h])
                dk_blk = dk_blk + _dot(dzb, qh[h], TN)
                dv_blk = dv_blk + _dot(wb, doh[h], TN)
                new_pl.append(pls[h] + bsum)
                new_e.append(es[h] + jnp.sum(e, axis=-1, keepdims=True))
            dk_ref[pl.ds(start, B), :] += dk_blk
            dv_ref[pl.ds(start, B), :] += dv_blk
            return tuple(new_pl), tuple(new_e), dq

        zero = jnp.zeros((B, 1), F32)
        carry = ((zero, zero), (zero, zero), jnp.zeros((B, LANES), F32))
        carry = lax.fori_loop(0, i, lambda j, c: block(j, c, False), carry)
        _, _, dq = block(i, carry, True)
        dq_ref[...] = dq * 0.125

    qspec = pl.BlockSpec((B, LANES), lambda hp, i: (i, hp))
    kspec = pl.BlockSpec((L, LANES), lambda hp, i: (0, HP + hp))
    vspec = pl.BlockSpec((L, LANES), lambda hp, i: (0, 2 * HP + hp))
    full = pl.BlockSpec((L, LANES), lambda hp, i: (0, hp))
    shp = jax.ShapeDtypeStruct((L, AW), F32)
    return _pcall(body, name="attn_bwd", grid=(HP, nb), in_specs=[qspec, kspec, vspec, qspec, qspec],
                  out_specs=[qspec, full, full], out_shape=[shp, shp, shp])(qk, qk, p, do, tot)


def _s5_disc(ar, ai, ldt):
    dt = jnp.exp(ldt)
    mag = jnp.exp(dt * ar)
    abr = mag * jnp.cos(dt * ai)
    abi = mag * jnp.sin(dt * ai)
    emr = abr - 1.0
    emi = abi
    den = ar * ar + ai * ai
    fr = (emr * ar + emi * ai) / den
    fi = (emi * ar - emr * ai) / den
    return abr, abi, fr, fi


def _s5_params_math(ar, ai, ldt, br, bi):
    abr, abi, fr, fi = _s5_disc(ar, ai, ldt)
    return abr, abi, fr * br - fi * bi, fr * bi + fi * br


def _scan_rows(xr, xi, lev, T, reverse):
    row = lax.broadcasted_iota(jnp.int32, xr.shape, 0)
    d, k = 1, 0
    while d < T:
        kr, ki = lev(k)
        if reverse:
            keep = row < T - d
            sr = jnp.where(keep, pltpu.roll(xr, T - d, 0), 0.0)
            si = jnp.where(keep, pltpu.roll(xi, T - d, 0), 0.0)
            ki = -ki
        else:
            keep = row >= d
            sr = jnp.where(keep, pltpu.roll(xr, d, 0), 0.0)
            si = jnp.where(keep, pltpu.roll(xi, d, 0), 0.0)
        xr, xi = xr + (kr * sr - ki * si), xi + (kr * si + ki * sr)
        d, k = 2 * d, k + 1
    return xr, xi


def _s5_prep(ar, ai, ldt, braw_r, braw_i, T):
    NL, _, NS = ar.shape
    GT = NS // STATE_TILE
    nlev = max(1, int(math.log2(T)))
    LV = 8 * ((nlev + 7) // 8)

    def body(ar_ref, ai_ref, ldt_ref, br_ref, bi_ref,
             abr_ref, abi_ref, levr_ref, levi_ref, powr_ref, powi_ref, bsr_ref, bsi_ref):
        levr_ref[...] = jnp.zeros_like(levr_ref)
        levi_ref[...] = jnp.zeros_like(levi_ref)
        for t in range(GT):
            sl = slice(t * STATE_TILE, (t + 1) * STATE_TILE)
            abr, abi, bsr, bsi = _s5_params_math(ar_ref[:, sl], ai_ref[:, sl], ldt_ref[:, sl],
                                                 br_ref[t], bi_ref[t])
            abr_ref[:, sl] = abr
            abi_ref[:, sl] = abi
            bsr_ref[t] = bsr.astype(BF16)
            bsi_ref[t] = bsi.astype(BF16)
            kr, ki = abr, abi
            for k in range(nlev):
                levr_ref[k:k + 1, sl] = kr
                levi_ref[k:k + 1, sl] = ki
                kr, ki = kr * kr - ki * ki, 2.0 * kr * ki
        for s in range(NS // LANES):
            sl = slice(s * LANES, (s + 1) * LANES)
            row = lax.broadcasted_iota(jnp.int32, (T, LANES), 0)
            xr = jnp.where(row == 0, abr_ref[:, sl], 0.0)
            xi = jnp.where(row == 0, abi_ref[:, sl], 0.0)
            lev = lambda k, sl=sl: (levr_ref[k:k + 1, sl], levi_ref[k:k + 1, sl])
            xr, xi = _scan_rows(xr, xi, lev, T, False)
            powr_ref[:, sl] = xr
            powi_ref[:, sl] = xi

    rowspec = pl.BlockSpec((None, 1, NS), lambda l: (l, 0, 0))
    bspec = pl.BlockSpec((None, GT, LANES, STATE_TILE), lambda l: (l, 0, 0, 0))
    levspec = pl.BlockSpec((None, LV, NS), lambda l: (l, 0, 0))
    powspec = pl.BlockSpec((None, T, NS), lambda l: (l, 0, 0))
    rs = jax.ShapeDtypeStruct((NL, 1, NS), F32)
    ls = jax.ShapeDtypeStruct((NL, LV, NS), F32)
    ps = jax.ShapeDtypeStruct((NL, T, NS), F32)
    bs = jax.ShapeDtypeStruct((NL, GT, LANES, STATE_TILE), BF16)
    return _pcall(body, name="s5_prep", grid=(NL,), in_specs=[rowspec] * 3 + [bspec] * 2,
                  out_specs=[rowspec, rowspec, levspec, levspec, powspec, powspec, bspec, bspec],
                  out_shape=[rs, rs, ls, ls, ps, ps, bs, bs])(ar, ai, ldt, braw_r, braw_i)


def _s5_prep_bwd(ar, ai, ldt, braw_r, braw_i, dabr, dabi, dbsr, dbsi):
    NL, _, NS = ar.shape
    GT = NS // STATE_TILE

    def body(ar_ref, ai_ref, ldt_ref, br_ref, bi_ref, dabr_ref, dabi_ref, dbsr_ref, dbsi_ref,
             dar_ref, dai_ref, dldt_ref, dbr_ref, dbi_ref):
        for t in range(GT):
            sl = slice(t * STATE_TILE, (t + 1) * STATE_TILE)
            _, vjp = jax.vjp(_s5_params_math, ar_ref[:, sl], ai_ref[:, sl], ldt_ref[:, sl],
                             br_ref[t], bi_ref[t])
            dar, dai, dldt, dbr, dbi = vjp((dabr_ref[:, sl], dabi_ref[:, sl], dbsr_ref[t], dbsi_ref[t]))
            dar_ref[:, sl] = dar
            dai_ref[:, sl] = dai
            dldt_ref[:, sl] = dldt
            dbr_ref[t] = dbr
            dbi_ref[t] = dbi

    rowspec = pl.BlockSpec((None, 1, NS), lambda l: (l, 0, 0))
    bspec = pl.BlockSpec((None, GT, LANES, STATE_TILE), lambda l: (l, 0, 0, 0))
    rs = jax.ShapeDtypeStruct((NL, 1, NS), F32)
    bs = jax.ShapeDtypeStruct((NL, GT, LANES, STATE_TILE), F32)
    return _pcall(body, name="s5_prep_bwd", grid=(NL,), in_specs=[rowspec] * 3 + [bspec] * 2 + [rowspec] * 2 + [bspec] * 2,
                  out_specs=[rowspec] * 3 + [bspec] * 2, out_shape=[rs, rs, rs, bs, bs])(
                      ar, ai, ldt, braw_r, braw_i, dabr, dabi, dbsr, dbsi)


def _s5_specs(L, SW, T, ucol0, rev):
    GT = SW // LANES
    nc = L // T
    cidx = (lambda c: nc - 1 - c) if rev else (lambda c: c)
    return dict(
        GT=GT, nc=nc, cidx=cidx,
        u=pl.BlockSpec((T, LANES), lambda j, c: (cidx(c), ucol0 + j)),
        chan=pl.BlockSpec((T, LANES), lambda j, c: (cidx(c), j)),
        state=pl.BlockSpec((T, STATE_TILE), lambda j, c: (cidx(c), j)),
        bmat=pl.BlockSpec((None, LANES, STATE_TILE), lambda j, c: (j, 0, 0)),
        cmat=pl.BlockSpec((None, STATE_TILE, LANES), lambda j, c: (j, 0, 0)),
        gmat=pl.BlockSpec((None, LANES, LANES), lambda j, c: (j, 0, 0)),
        cvec=pl.BlockSpec((1, LANES), lambda j, c: (0, j)),
        svec=pl.BlockSpec((1, STATE_TILE), lambda j, c: (0, j)),
    )


def _s5_fwd(p, sp, SW, ucol0, T):
    L = p.shape[0]
    S = _s5_specs(L, SW, T, ucol0, False)
    NS = S["GT"] * STATE_TILE
    LV = sp["levr"].shape[0]

    def body(u_ref, bsr_ref, bsi_ref, levr_ref, levi_ref, powr_ref, powi_ref, cr_ref, ci_ref,
             d_ref, wg_ref, gb_ref, o_ref, sr_ref, si_ref, y_ref, carry_ref):
        @pl.when(pl.program_id(1) == 0)
        def _():
            carry_ref[...] = jnp.zeros_like(carry_ref)

        u = u_ref[...]
        ub = u.astype(BF16)
        bur = _dot(ub, bsr_ref[...])
        bui = _dot(ub, bsi_ref[...])
        for s in range(STATE_TILE // LANES):
            sl = slice(s * LANES, (s + 1) * LANES)
            lev = lambda k, sl=sl: (levr_ref[k:k + 1, sl], levi_ref[k:k + 1, sl])
            xr, xi = _scan_rows(bur[:, sl], bui[:, sl], lev, T, False)
            cr, ci = carry_ref[0:1, sl], carry_ref[1:2, sl]
            pr, pi = powr_ref[:, sl], powi_ref[:, sl]
            sr_ref[:, sl] = xr + (pr * cr - pi * ci)
            si_ref[:, sl] = xi + (pr * ci + pi * cr)
            carry_ref[0:1, sl] = sr_ref[T - 1:T, sl]
            carry_ref[1:2, sl] = si_ref[T - 1:T, sl]
        y = (_dot(sr_ref[...].astype(BF16), cr_ref[...]) - _dot(si_ref[...].astype(BF16), ci_ref[...])
             + d_ref[...] * u)
        y_ref[...] = y
        yg = jax.nn.gelu(y)
        gate = jax.nn.sigmoid(_dot(yg.astype(BF16), wg_ref[...]) + gb_ref[...])
        o_ref[...] = yg * gate

    lvspec = pl.BlockSpec((LV, STATE_TILE), lambda j, c: (0, j))
    pwspec = pl.BlockSpec((T, STATE_TILE), lambda j, c: (0, j))
    cs = jax.ShapeDtypeStruct((L, SW), F32)
    ss = jax.ShapeDtypeStruct((L, NS), F32)
    return _pcall(
        body, name="s5_fwd", grid=(S["GT"], S["nc"]),
        in_specs=[S["u"], S["bmat"], S["bmat"], lvspec, lvspec, pwspec, pwspec, S["cmat"], S["cmat"],
                  S["cvec"], S["gmat"], S["cvec"]],
        out_specs=[S["chan"], S["state"], S["state"], S["chan"]], out_shape=[cs, ss, ss, cs],
        scratch=[pltpu.VMEM((8, STATE_TILE), F32)],
    )(p, sp["bsr"], sp["bsi"], sp["levr"], sp["levi"], sp["powr"], sp["powi"], sp["crT"], sp["ciT"],
      sp["d"], sp["wg"], sp["gb"])


def _s5_bwd(do, p, s_re, s_im, ypre, sp, SW, ucol0, T):
    L = p.shape[0]
    S = _s5_specs(L, SW, T, ucol0, True)
    GT, nc, cidx = S["GT"], S["nc"], S["cidx"]
    NS = GT * STATE_TILE
    LV = sp["levr"].shape[0]
    T8 = T // 8

    def body(do_ref, u_ref, sr_ref, si_ref, hr_ref, hi_ref, y_ref, bsr_ref, bsi_ref, levr_ref, levi_ref,
             rpr_ref, rpi_ref, cr_ref, ci_ref, d_ref, wg_ref, gb_ref,
             du_ref, dbsr_ref, dbsi_ref, dcr_ref, dci_ref, dd_ref, dwg_ref, dgb_ref, dar_ref, dai_ref,
             carry_ref, lam_r, lam_i):
        c = pl.program_id(1)
        first = c == 0

        @pl.when(first)
        def _():
            carry_ref[...] = jnp.zeros_like(carry_ref)

        u = u_ref[...]
        ub = u.astype(BF16)
        y = y_ref[...]
        yg, gelu_vjp = jax.vjp(jax.nn.gelu, y)
        ygb = yg.astype(BF16)
        gate = jax.nn.sigmoid(_dot(ygb, wg_ref[...]) + gb_ref[...])
        dout = do_ref[...]
        dt = dout * yg * gate * (1.0 - gate)
        dtb = dt.astype(BF16)
        dyg = dout * gate + _dot(dtb, wg_ref[...], NT)
        (dy,) = gelu_vjp(dyg)
        dyb = dy.astype(BF16)
        _accumulate(dwg_ref, _dot(ygb, dtb, TN), first)
        _accumulate(dgb_ref, jnp.sum(dt, axis=0, keepdims=True), first)
        _accumulate(dd_ref, jnp.sum(dy * u, axis=0, keepdims=True), first)
        srb = sr_ref[...].astype(BF16)
        sib = si_ref[...].astype(BF16)
        _accumulate(dcr_ref, _dot(srb, dyb, TN), first)
        _accumulate(dci_ref, -_dot(sib, dyb, TN), first)
        dsr = _dot(dyb, cr_ref[...], NT)
        dsi = -_dot(dyb, ci_ref[...], NT)
        last_chunk = cidx(c) == 0
        row = lax.broadcasted_iota(jnp.int32, (T, LANES), 0)
        for s in range(STATE_TILE // LANES):
            sl = slice(s * LANES, (s + 1) * LANES)
            lev = lambda k, sl=sl: (levr_ref[k:k + 1, sl], levi_ref[k:k + 1, sl])
            xr, xi = _scan_rows(dsr[:, sl], dsi[:, sl], lev, T, True)
            cr, ci = carry_ref[0:1, sl], carry_ref[1:2, sl]
            pr, pi = rpr_ref[:, sl], -rpi_ref[:, sl]
            lam_r[:, sl] = xr + (pr * cr - pi * ci)
            lam_i[:, sl] = xi + (pr * ci + pi * cr)
            carry_ref[0:1, sl] = lam_r[0:1, sl]
            carry_ref[1:2, sl] = lam_i[0:1, sl]
            keep = jnp.logical_not(last_chunk)
            pr0 = jnp.where(keep, hr_ref[7:8, sl], 0.0)
            pi0 = jnp.where(keep, hi_ref[7:8, sl], 0.0)
            spr = jnp.where(row == 0, pr0, pltpu.roll(sr_ref[:, sl], 1, 0))
            spi = jnp.where(row == 0, pi0, pltpu.roll(si_ref[:, sl], 1, 0))
            lr, li = lam_r[:, sl], lam_i[:, sl]
            dar = jnp.sum(lr * spr + li * spi, axis=0, keepdims=True)
            dai = jnp.sum(li * spr - lr * spi, axis=0, keepdims=True)

            @pl.when(first)
            def _():
                dar_ref[:, sl] = dar
                dai_ref[:, sl] = dai

            @pl.when(jnp.logical_not(first))
            def _():
                dar_ref[:, sl] += dar
                dai_ref[:, sl] += dai

        lrb = lam_r[...].astype(BF16)
        lib = lam_i[...].astype(BF16)
        _accumulate(dbsr_ref, _dot(ub, lrb, TN), first)
        _accumulate(dbsi_ref, _dot(ub, lib, TN), first)
        du_ref[...] = dy * d_ref[...] + _dot(lrb, bsr_ref[...], NT) + _dot(lib, bsi_ref[...], NT)

    halo = pl.BlockSpec((8, STATE_TILE), lambda j, c: (jnp.maximum(cidx(c) * T8 - 1, 0), j))
    lvspec = pl.BlockSpec((LV, STATE_TILE), lambda j, c: (0, j))
    pwspec = pl.BlockSpec((T, STATE_TILE), lambda j, c: (0, j))
    f = lambda *s: jax.ShapeDtypeStruct(s, F32)
    return _pcall(
        body, name="s5_bwd", grid=(GT, nc),
        in_specs=[S["chan"], S["u"], S["state"], S["state"], halo, halo, S["chan"], S["bmat"], S["bmat"],
                  lvspec, lvspec, pwspec, pwspec, S["cmat"], S["cmat"], S["cvec"], S["gmat"], S["cvec"]],
        out_specs=[S["chan"], S["bmat"], S["bmat"], S["cmat"], S["cmat"], S["cvec"], S["gmat"], S["cvec"],
                   S["svec"], S["svec"]],
        out_shape=[f(L, SW), f(GT, LANES, STATE_TILE), f(GT, LANES, STATE_TILE), f(GT, STATE_TILE, LANES),
                   f(GT, STATE_TILE, LANES), f(1, SW), f(GT, LANES, LANES), f(1, SW), f(1, NS), f(1, NS)],
        scratch=[pltpu.VMEM((8, STATE_TILE), F32), pltpu.VMEM((T, STATE_TILE), F32),
                 pltpu.VMEM((T, STATE_TILE), F32)],
    )(do, p, s_re, s_im, s_re, s_im, ypre, sp["bsr"], sp["bsi"], sp["levr"], sp["levi"],
      sp["rpowr"], sp["rpowi"], sp["crT"], sp["ciT"], sp["d"], sp["wg"], sp["gb"])


def _conv_taps(xc, h6, h7, row):
    x1 = jnp.where(row == 0, h7, pltpu.roll(xc, 1, 0))
    x2 = jnp.where(row == 0, h6, jnp.where(row == 1, h7, pltpu.roll(xc, 2, 0)))
    return x1, x2


def _conv_halves(up_ref, halo_ref, w_ref, b_ref, tm, FS):
    first = pl.program_id(1) == 0
    row = lax.broadcasted_iota(jnp.int32, (tm, FS), 0)
    outs, taps = [], []
    for s in (0, 1):
        xc = up_ref[s]
        h6 = jnp.where(first, 0.0, halo_ref[s, 6:7, :])
        h7 = jnp.where(first, 0.0, halo_ref[s, 7:8, :])
        x1, x2 = _conv_taps(xc, h6, h7, row)
        outs.append(b_ref[s] + x2 * w_ref[s, 0:1, :] + x1 * w_ref[s, 1:2, :] + xc * w_ref[s, 2:3, :])
        taps.append((x2, x1, xc))
    return outs, taps


def _convglu_specs(L, FS, tm):
    t8 = tm // 8
    return dict(
        up=pl.BlockSpec((2, None, tm, FS), lambda j, i: (0, j, i, 0)),
        halo=pl.BlockSpec((2, None, 8, FS), lambda j, i: (0, j, jnp.maximum(i * t8 - 1, 0), 0)),
        w=pl.BlockSpec((2, None, 3, FS), lambda j, i: (0, j, 0, 0)),
        b=pl.BlockSpec((2, None, 1, FS), lambda j, i: (0, j, 0, 0)),
        act=pl.BlockSpec((None, tm, FS), lambda j, i: (j, i, 0)),
    )


def _convglu_fwd(up, w, b):
    _, NSH, L, FS = up.shape
    tm = _row_tile(L)
    S = _convglu_specs(L, FS, tm)

    def body(up_ref, halo_ref, w_ref, b_ref, act_ref):
        (val, gate), _ = _conv_halves(up_ref, halo_ref, w_ref, b_ref, tm, FS)
        act_ref[...] = _glu_math(val, gate).astype(BF16)

    return _pcall(body, name="convglu_fwd", grid=(NSH, L // tm), in_specs=[S["up"], S["halo"], S["w"], S["b"]],
                  out_specs=S["act"], out_shape=jax.ShapeDtypeStruct((NSH, L, FS), BF16))(up, up, w, b)


def _convglu_bwd(up, dact, w, b):
    _, NSH, L, FS = up.shape
    tm = _row_tile(L)
    S = _convglu_specs(L, FS, tm)

    def body(up_ref, halo_ref, w_ref, b_ref, dact_ref, dc_ref, dw_ref, db_ref):
        first = pl.program_id(1) == 0
        (val, gate), taps = _conv_halves(up_ref, halo_ref, w_ref, b_ref, tm, FS)
        _, vjp = jax.vjp(_glu_math, val, gate)
        dcs = vjp(dact_ref[...])
        for s in (0, 1):
            dc = dcs[s]
            dc_ref[s] = dc
            sums = [jnp.sum(dc * t, axis=0, keepdims=True) for t in taps[s]]
            dbs = jnp.sum(dc, axis=0, keepdims=True)

            @pl.when(first)
            def _():
                for t in range(3):
                    dw_ref[s, t:t + 1, :] = sums[t]
                db_ref[s] = dbs

            @pl.when(jnp.logical_not(first))
            def _():
                for t in range(3):
                    dw_ref[s, t:t + 1, :] += sums[t]
                db_ref[s] += dbs

    f = lambda *s: jax.ShapeDtypeStruct(s, F32)
    return _pcall(body, name="convglu_bwd", grid=(NSH, L // tm),
                  in_specs=[S["up"], S["halo"], S["w"], S["b"], S["act"]],
                  out_specs=[S["up"], S["w"], S["b"]],
                  out_shape=[f(2, NSH, L, FS), f(2, NSH, 3, FS), f(2, NSH, 1, FS)])(up, up, w, b, dact)


def _conv_transpose(dc, w):
    NS8, L, FS = dc.shape
    tm = _row_tile(L)
    t8 = tm // 8
    nt = L // tm

    def body(dc_ref, nxt_ref, w_ref, o_ref):
        last = pl.program_id(1) == nt - 1
        row = lax.broadcasted_iota(jnp.int32, (tm, FS), 0)
        xc = dc_ref[...]
        n0 = jnp.where(last, 0.0, nxt_ref[0:1, :])
        n1 = jnp.where(last, 0.0, nxt_ref[1:2, :])
        x1 = jnp.where(row == tm - 1, n0, pltpu.roll(xc, tm - 1, 0))
        x2 = jnp.where(row == tm - 1, n1, jnp.where(row == tm - 2, n0, pltpu.roll(xc, tm - 2, 0)))
        o_ref[...] = (xc * w_ref[2:3, :] + x1 * w_ref[1:2, :] + x2 * w_ref[0:1, :]).astype(BF16)

    blk = pl.BlockSpec((None, tm, FS), lambda j, i: (j, i, 0))
    nxt = pl.BlockSpec((None, 8, FS), lambda j, i: (j, jnp.minimum((i + 1) * t8, L // 8 - 1), 0))
    wsp = pl.BlockSpec((None, 3, FS), lambda j, i: (j, 0, 0))
    return _pcall(body, name="conv_transpose", grid=(NS8, nt), in_specs=[blk, nxt, wsp], out_specs=blk,
                  out_shape=jax.ShapeDtypeStruct((NS8, L, FS), BF16))(dc, dc, w)


def _loss_head(y, target):
    L, D = y.shape
    tm = _row_tile(L)

    def body(y_ref, t_ref, loss_ref, dy_ref):
        err = y_ref[...] - t_ref[...]
        dy_ref[...] = err / D
        part = 0.5 * jnp.sum(jnp.mean(err * err, axis=-1, keepdims=True), axis=0, keepdims=True)
        _accumulate(loss_ref, jnp.broadcast_to(part, (1, LANES)), pl.program_id(0) == 0)

    row = pl.BlockSpec((tm, D), lambda i: (i, 0))
    vec = pl.BlockSpec((1, LANES), lambda i: (0, 0))
    return _pcall(body, name="loss_head", grid=(L // tm,), in_specs=[row, row], out_specs=[vec, row],
                  out_shape=[jax.ShapeDtypeStruct((1, LANES), F32), jax.ShapeDtypeStruct((L, D), F32)])(y, target)


def _ada_fwd(c_all, ada_w, ada_b):
    NL, D, NC = ada_w.shape
    NB = c_all.shape[0]

    def body(c_ref, w_ref, b_ref, o_ref):
        cact = jax.nn.silu(c_ref[...])
        o_ref[...] = _dot(cact.astype(BF16), w_ref[...].astype(BF16)) + b_ref[...]

    return _pcall(body, name="ada_fwd", grid=(NL,),
                  in_specs=[pl.BlockSpec((NB, D), lambda l: (0, 0)), pl.BlockSpec((None, D, NC), lambda l: (l, 0, 0)),
                            pl.BlockSpec((None, 1, NC), lambda l: (l, 0, 0))],
                  out_specs=pl.BlockSpec((None, NB, NC), lambda l: (l, 0, 0)),
                  out_shape=jax.ShapeDtypeStruct((NL, NB, NC), F32))(c_all, ada_w, ada_b)


def _ada_bwd(c_all_t, dmod):
    D, NB = c_all_t.shape
    NL, _, NC = dmod.shape

    def body(c_ref, d_ref, o_ref):
        cact = jax.nn.silu(c_ref[...]).astype(BF16).astype(F32)
        o_ref[...] = _dot(cact, d_ref[...].astype(BF16).astype(F32))

    return _pcall(body, name="ada_bwd", grid=(NL,),
                  in_specs=[pl.BlockSpec((D, NB), lambda l: (0, 0)), pl.BlockSpec((None, NB, NC), lambda l: (l, 0, 0))],
                  out_specs=pl.BlockSpec((None, D, NC), lambda l: (l, 0, 0)),
                  out_shape=jax.ShapeDtypeStruct((NL, D, NC), F32))(c_all_t, dmod)


def _adamw(parts, w, m, v):
    NL, P, R, C = parts.shape
    tr = R
    for cand in (256, 128, 64, 32, 16, 8):
        if R % cand == 0 and R > cand:
            tr = cand
            break

    def body(p_ref, w_ref, m_ref, v_ref, g_ref, d_ref, nm_ref, nv_ref):
        g = p_ref[0]
        for k in range(1, P):
            g = g + p_ref[k]
        m2 = ADAM_B1 * m_ref[...] + (1.0 - ADAM_B1) * g
        v2 = ADAM_B2 * v_ref[...] + (1.0 - ADAM_B2) * jnp.square(g)
        m_hat = m2 / (1.0 - ADAM_B1 ** ADAM_STEP)
        v_hat = v2 / (1.0 - ADAM_B2 ** ADAM_STEP)
        g_ref[...] = g
        d_ref[...] = -ADAM_LR * (m_hat / (jnp.sqrt(v_hat) + ADAM_EPS) + ADAM_WD * w_ref[...])
        nm_ref[...] = m2
        nv_ref[...] = v2

    pspec = pl.BlockSpec((None, P, tr, C), lambda l, i: (l, 0, i, 0))
    wspec = pl.BlockSpec((None, tr, C), lambda l, i: (l, i, 0))
    shp = jax.ShapeDtypeStruct((NL, R, C), F32)
    return _pcall(body, name="adamw", grid=(NL, R // tr), in_specs=[pspec, wspec, wspec, wspec],
                  out_specs=[wspec] * 4, out_shape=[shp] * 4)(parts, w, m, v)


def _block_diag(blocks):
    *lead, g, r, c = blocks.shape
    eye = jnp.eye(g, dtype=bool)[:, None, :, None]
    full = jnp.where(eye, blocks[..., :, :, None, :], 0.0)
    return full.reshape(*lead, g * r, g * c)


def _block_diag_extract(m, r, c):
    g = GROUPS_PER_TILE
    m5 = m.reshape(*m.shape[:-2], g, r, g, c)
    idx = jnp.arange(g)
    out = m5[..., idx, :, idx, :]
    return jnp.moveaxis(out, 0, -3)


def kernel(x, c, ada_w, ada_b, norm1_g, w_in, q_norm_g, k_norm_g, ssm_a_re, ssm_a_im, ssm_log_dt, ssm_b_re, ssm_b_im, ssm_c_re, ssm_c_im, ssm_d, glu_w, glu_b, attn_out_g, ssm_out_g, w_out, norm2_g, ffn_w_up, ffn_conv_w, ffn_conv_b, ffn_w_down, loss_target, m_ada_w, m_ada_b, m_norm1_g, m_w_in, m_q_norm_g, m_k_norm_g, m_ssm_a_re, m_ssm_a_im, m_ssm_log_dt, m_ssm_b_re, m_ssm_b_im, m_ssm_c_re, m_ssm_c_im, m_ssm_d, m_glu_w, m_glu_b, m_attn_out_g, m_ssm_out_g, m_w_out, m_norm2_g, m_ffn_w_up, m_ffn_conv_w, m_ffn_conv_b, m_ffn_w_down, v_ada_w, v_ada_b, v_norm1_g, v_w_in, v_q_norm_g, v_k_norm_g, v_ssm_a_re, v_ssm_a_im, v_ssm_log_dt, v_ssm_b_re, v_ssm_b_im, v_ssm_c_re, v_ssm_c_im, v_ssm_d, v_glu_w, v_glu_b, v_attn_out_g, v_ssm_out_g, v_w_out, v_norm2_g, v_ffn_w_up, v_ffn_conv_w, v_ffn_conv_b, v_ffn_w_down):
    weights = dict(ada_w=ada_w, ada_b=ada_b, norm1_g=norm1_g, w_in=w_in, q_norm_g=q_norm_g, k_norm_g=k_norm_g,
                   ssm_a_re=ssm_a_re, ssm_a_im=ssm_a_im, ssm_log_dt=ssm_log_dt, ssm_b_re=ssm_b_re,
                   ssm_b_im=ssm_b_im, ssm_c_re=ssm_c_re, ssm_c_im=ssm_c_im, ssm_d=ssm_d, glu_w=glu_w, glu_b=glu_b,
                   attn_out_g=attn_out_g, ssm_out_g=ssm_out_g, w_out=w_out, norm2_g=norm2_g, ffn_w_up=ffn_w_up,
                   ffn_conv_w=ffn_conv_w, ffn_conv_b=ffn_conv_b, ffn_w_down=ffn_w_down)
    mom_m = dict(ada_w=m_ada_w, ada_b=m_ada_b, norm1_g=m_norm1_g, w_in=m_w_in, q_norm_g=m_q_norm_g,
                 k_norm_g=m_k_norm_g, ssm_a_re=m_ssm_a_re, ssm_a_im=m_ssm_a_im, ssm_log_dt=m_ssm_log_dt,
                 ssm_b_re=m_ssm_b_re, ssm_b_im=m_ssm_b_im, ssm_c_re=m_ssm_c_re, ssm_c_im=m_ssm_c_im, ssm_d=m_ssm_d,
                 glu_w=m_glu_w, glu_b=m_glu_b, attn_out_g=m_attn_out_g, ssm_out_g=m_ssm_out_g, w_out=m_w_out,
                 norm2_g=m_norm2_g, ffn_w_up=m_ffn_w_up, ffn_conv_w=m_ffn_conv_w, ffn_conv_b=m_ffn_conv_b,
                 ffn_w_down=m_ffn_w_down)
    mom_v = dict(ada_w=v_ada_w, ada_b=v_ada_b, norm1_g=v_norm1_g, w_in=v_w_in, q_norm_g=v_q_norm_g,
                 k_norm_g=v_k_norm_g, ssm_a_re=v_ssm_a_re, ssm_a_im=v_ssm_a_im, ssm_log_dt=v_ssm_log_dt,
                 ssm_b_re=v_ssm_b_re, ssm_b_im=v_ssm_b_im, ssm_c_re=v_ssm_c_re, ssm_c_im=v_ssm_c_im, ssm_d=v_ssm_d,
                 glu_w=v_glu_w, glu_b=v_glu_b, attn_out_g=v_attn_out_g, ssm_out_g=v_ssm_out_g, w_out=v_w_out,
                 norm2_g=v_norm2_g, ffn_w_up=v_ffn_w_up, ffn_conv_w=v_ffn_conv_w, ffn_conv_b=v_ffn_conv_b,
                 ffn_w_down=v_ffn_w_down)
    names = list(weights)
    big = ("ada_w", "w_in", "w_out", "ffn_w_up", "ffn_conv_w", "ffn_w_down")
    small = [n for n in names if n not in big]

    x = x[0]
    target = loss_target[0]
    L, D = x.shape
    NL = ada_w.shape[0]
    AW = D // 2
    SW = D - AW
    NH = AW // HEAD_DIM
    HP = AW // LANES
    G = SW // SSM_GROUP
    GT = SW // LANES
    NS = G * SSM_STATE
    NIN = w_in.shape[-1]
    FS = ffn_w_up.shape[-1]
    NSH = NDEV // 2
    NCA = ada_w.shape[-1]
    ROWS_OUT = w_out.shape[1]
    ROWS_DOWN = ffn_w_down.shape[1]
    B_ATT = min(L, 128)
    T_S5 = min(L, 128)
    tm = _row_tile(L)
    me = _my_index()

    cpad = jnp.reshape(c, (D // LANES, LANES))
    (c_all,) = _exchange([cpad], [0], False, "gather_c")
    c_all = c_all.reshape(NDEV, D)
    w_in_g, w_out_g, w_up_g, conv_w_g, w_down_g = _exchange(
        [w_in.astype(BF16), w_out.astype(BF16), ffn_w_up.astype(BF16), ffn_conv_w, ffn_w_down.astype(BF16)],
        [1, 1, 1, 1, 1], False, "gather_weights")
    w_out_g = w_out_g.reshape(NL, D, D)
    w_down_g = w_down_g.reshape(NL, NSH, 2 * ROWS_DOWN, D)
    conv_w_g = conv_w_g.reshape(NL, 2, NSH, 3, FS)
    conv_b_g = ffn_conv_b.reshape(NL, 2, NSH, 1, FS)

    ada_b_mine = lax.dynamic_slice_in_dim(ada_b, me * NCA, NCA, axis=1).reshape(NL, 1, NCA)
    mod_part = _ada_fwd(c_all, ada_w, ada_b_mine)
    (mod_all,) = _exchange([mod_part], [0], False, "gather_mod")
    mod = lax.dynamic_index_in_dim(mod_all, me, axis=2, keepdims=False)
    mod = jnp.transpose(mod, (1, 0, 2)).reshape(NL, N_MOD, 1, D)

    row = lambda a: a.reshape(NL, 1, NS)
    ar_row, ai_row = row(ssm_a_re), row(ssm_a_im)
    ldt_row = row(jnp.broadcast_to(ssm_log_dt[:, :, None], (NL, G, SSM_STATE)))
    tiles = lambda a: a.reshape((NL, GT, GROUPS_PER_TILE) + a.shape[2:])
    braw_r = _block_diag(jnp.swapaxes(tiles(ssm_b_re), -1, -2))
    braw_i = _block_diag(jnp.swapaxes(tiles(ssm_b_im), -1, -2))
    crT = _block_diag(jnp.swapaxes(tiles(ssm_c_re), -1, -2)).astype(BF16)
    ciT = _block_diag(jnp.swapaxes(tiles(ssm_c_im), -1, -2)).astype(BF16)
    wg = _block_diag(tiles(glu_w)).astype(BF16)
    abr, abi, levr, levi, powr, powi, bsr, bsi = _s5_prep(ar_row, ai_row, ldt_row, braw_r, braw_i, T_S5)
    s5p = dict(bsr=bsr, bsi=bsi, levr=levr, levi=levi, powr=powr, powi=powi,
               rpowr=jnp.flip(powr, axis=1), rpowi=jnp.flip(powi, axis=1), crT=crT, ciT=ciT,
               d=ssm_d.reshape(NL, 1, SW), wg=wg, gb=glu_b.reshape(NL, 1, SW))

    gqk = jnp.concatenate([jnp.tile(q_norm_g, (1, NH)), jnp.tile(k_norm_g, (1, NH))], axis=1).reshape(NL, 1, 2 * AW)
    layer_params = dict(
        mod=mod, norm1_g=norm1_g.reshape(NL, 1, D), norm2_g=norm2_g.reshape(NL, 1, D), gqk=gqk,
        ga=attn_out_g.reshape(NL, 1, AW), gs=ssm_out_g.reshape(NL, 1, SW),
        w_in=w_in_g, w_out=w_out_g, w_up=w_up_g, conv_w=conv_w_g, conv_b=conv_b_g, w_down=w_down_g, s5=s5p)

    ucol0 = 3 * AW // LANES

    def mm_rows_shards(name, a, b, n, out_dtype=F32):
        K = a.shape[1]
        return _matmul(name, a, b, dims=NN, grid=(L // tm, NDEV),
                       a_spec=pl.BlockSpec((tm, K), lambda i, j: (i, 0)),
                       b_spec=pl.BlockSpec((None, K, n), lambda i, j: (j, 0, 0)),
                       out_shape=jax.ShapeDtypeStruct((L, NDEV * n), out_dtype),
                       out_specs=pl.BlockSpec((tm, n), lambda i, j: (i, j)))

    tn = min(D, 512)

    def resid_epilogue(acc, xres, gate):
        return acc, xres + gate * acc

    def layer_fwd(xin, lp):
        sh1, sc1, g1, sh2, sc2, g2 = (lp["mod"][k] for k in range(N_MOD))
        h = _lnmod_fwd(xin, lp["norm1_g"], sh1, sc1)
        p = mm_rows_shards("mm_in", h, lp["w_in"], NIN)
        qk = _qknorm_fwd(p, lp["gqk"], AW)
        o_attn, tot = _attn_fwd(qk, p, AW, B_ATT)
        o_ssm, s_re, s_im, ypre = _s5_fwd(p, lp["s5"], SW, ucol0, T_S5)
        o = _outnorm_fwd(o_attn, o_ssm, lp["ga"], lp["gs"])
        a1, x_mid = _matmul(
            "mm_out", o, lp["w_out"], dims=NN, grid=(L // tm, D // tn),
            a_spec=pl.BlockSpec((tm, D), lambda i, j: (i, 0)), b_spec=pl.BlockSpec((D, tn), lambda i, j: (0, j)),
            extra=(xin, g1), extra_specs=(pl.BlockSpec((tm, tn), lambda i, j: (i, j)),
                                          pl.BlockSpec((1, tn), lambda i, j: (0, j))),
            epilogue=resid_epilogue,
            out_shape=[jax.ShapeDtypeStruct((L, D), F32)] * 2,
            out_specs=[pl.BlockSpec((tm, tn), lambda i, j: (i, j))] * 2)
        h2 = _lnmod_fwd(x_mid, lp["norm2_g"], sh2, sc2)
        up = _matmul(
            "mm_up", h2, lp["w_up"], dims=NN, grid=(L // tm, NDEV),
            a_spec=pl.BlockSpec((tm, D), lambda i, j: (i, 0)), b_spec=pl.BlockSpec((None, D, FS), lambda i, j: (j, 0, 0)),
            out_shape=jax.ShapeDtypeStruct((NDEV, L, FS), F32),
            out_specs=pl.BlockSpec((None, tm, FS), lambda i, j: (j, i, 0)))
        up = up.reshape(2, NSH, L, FS)
        act = _convglu_fwd(up, lp["conv_w"], lp["conv_b"])
        a2, x_out = _matmul(
            "mm_down", act, lp["w_down"], dims=NN, grid=(L // tm, D // tn, NSH), kaxis=2, acc_shape=(tm, tn),
            a_spec=pl.BlockSpec((None, tm, FS), lambda i, j, k: (k, i, 0)),
            b_spec=pl.BlockSpec((None, FS, tn), lambda i, j, k: (k, 0, j)),
            extra=(x_mid, g2), extra_specs=(pl.BlockSpec((tm, tn), lambda i, j, k: (i, j)),
                                            pl.BlockSpec((1, tn), lambda i, j, k: (0, j))),
            epilogue=resid_epilogue,
            out_shape=[jax.ShapeDtypeStruct((L, D), F32)] * 2,
            out_specs=[pl.BlockSpec((tm, tn), lambda i, j, k: (i, j))] * 2)
        res = dict(x=xin, h=h, p=p, qk=qk, tot=tot, o_attn=o_attn, o_ssm=o_ssm, s_re=s_re, s_im=s_im, ypre=ypre,
                   o=o, a1=a1, x_mid=x_mid, h2=h2, up=up, act=act, a2=a2)
        return x_out, res

    y, residuals = lax.scan(layer_fwd, x, layer_params)

    loss_row, dy = _loss_head(y, target)
    loss = lax.psum(loss_row[0, 0], ("x", "y", "c"))

    def layer_bwd(dx, args):
        lp, r = args
        sh1, sc1, g1, sh2, sc2, g2 = (lp["mod"][k] for k in range(N_MOD))
        da2, dg2 = _gate_bwd(dx, r["a2"], g2)
        dact = _matmul(
            "mm_dact", da2, lp["w_down"], dims=NT, grid=(L // tm, NSH),
            a_spec=pl.BlockSpec((tm, D), lambda i, j: (i, 0)), b_spec=pl.BlockSpec((None, FS, D), lambda i, j: (j, 0, 0)),
            out_shape=jax.ShapeDtypeStruct((NSH, L, FS), F32),
            out_specs=pl.BlockSpec((None, tm, FS), lambda i, j: (j, i, 0)))
        dw_down = _matmul(
            "mm_dw_down", r["act"], da2, dims=TN, grid=(NSH, D // tn, L // tm), kaxis=2, acc_shape=(FS, tn),
            a_spec=pl.BlockSpec((None, tm, FS), lambda j, n, k: (j, k, 0)),
            b_spec=pl.BlockSpec((tm, tn), lambda j, n, k: (k, n)),
            out_shape=jax.ShapeDtypeStruct((NSH, FS, D), F32),
            out_specs=pl.BlockSpec((None, FS, tn), lambda j, n, k: (j, 0, n)))
        dconv, dcw, dcb = _convglu_bwd(r["up"], dact, lp["conv_w"], lp["conv_b"])
        dup = _conv_transpose(dconv.reshape(NDEV, L, FS), lp["conv_w"].reshape(NDEV, 3, FS))
        dh2 = _matmul(
            "mm_dh2", dup, lp["w_up"], dims=NT, grid=(L // tm, NDEV), kaxis=1, acc_shape=(tm, D),
            a_spec=pl.BlockSpec((None, tm, FS), lambda i, k: (k, i, 0)),
            b_spec=pl.BlockSpec((None, D, FS), lambda i, k: (k, 0, 0)),
            out_shape=jax.ShapeDtypeStruct((L, D), F32), out_specs=pl.BlockSpec((tm, D), lambda i, k: (i, 0)))
        dw_up = _matmul(
            "mm_dw_up", r["h2"], dup, dims=TN, grid=(NDEV, L // tm), kaxis=1, acc_shape=(D, FS),
            a_spec=pl.BlockSpec((tm, D), lambda j, k: (k, 0)), b_spec=pl.BlockSpec((None, tm, FS), lambda j, k: (j, k, 0)),
            out_shape=jax.ShapeDtypeStruct((NDEV, D, FS), F32),
            out_specs=pl.BlockSpec((None, D, FS), lambda j, k: (j, 0, 0)))
        dxm, dn2, dsh2, dsc2 = _lnmod_bwd(dh2, r["x_mid"], lp["norm2_g"], sh2, sc2, dx)
        da1, dg1 = _gate_bwd(dxm, r["a1"], g1)
        do = _matmul(
            "mm_do", da1, lp["w_out"], dims=NT, grid=(L // tm, D // tn),
            a_spec=pl.BlockSpec((tm, D), lambda i, j: (i, 0)), b_spec=pl.BlockSpec((tn, D), lambda i, j: (j, 0)),
            out_shape=jax.ShapeDtypeStruct((L, D), F32), out_specs=pl.BlockSpec((tm, tn), lambda i, j: (i, j)))
        dw_out = _matmul(
            "mm_dw_out", r["o"], da1, dims=TN, grid=(D // tn, D // tn, L // tm), kaxis=2, acc_shape=(tn, tn),
            a_spec=pl.BlockSpec((tm, tn), lambda m, n, k: (k, m)), b_spec=pl.BlockSpec((tm, tn), lambda m, n, k: (k, n)),
            out_shape=jax.ShapeDtypeStruct((D, D), F32), out_specs=pl.BlockSpec((tn, tn), lambda m, n, k: (m, n)))
        doa, dos, dga, dgs = _outnorm_bwd(do, r["o_attn"], r["o_ssm"], lp["ga"], lp["gs"])
        (du, dbsr, dbsi, dcr, dci, dd, dwg, dgb, dabr, dabi) = _s5_bwd(
            dos, r["p"], r["s_re"], r["s_im"], r["ypre"], lp["s5"], SW, ucol0, T_S5)
        dq, dk, dv = _attn_bwd(r["qk"], r["p"], doa, r["tot"], AW, B_ATT)
        dqk, dgqk = _qknorm_bwd(jnp.concatenate([dq, dk], axis=1), r["p"], lp["gqk"], AW)
        dp = jnp.concatenate([dqk, dv, du], axis=1).astype(BF16)
        dh = _matmul(
            "mm_dh", dp, lp["w_in"], dims=NT, grid=(L // tm, NDEV), kaxis=1, acc_shape=(tm, D),
            a_spec=pl.BlockSpec((tm, NIN), lambda i, k: (i, k)), b_spec=pl.BlockSpec((None, D, NIN), lambda i, k: (k, 0, 0)),
            out_shape=jax.ShapeDtypeStruct((L, D), F32), out_specs=pl.BlockSpec((tm, D), lambda i, k: (i, 0)))
        dw_in = _matmul(
            "mm_dw_in", r["h"], dp, dims=TN, grid=(NDEV, L // tm), kaxis=1, acc_shape=(D, NIN),
            a_spec=pl.BlockSpec((tm, D), lambda j, k: (k, 0)), b_spec=pl.BlockSpec((tm, NIN), lambda j, k: (k, j)),
            out_shape=jax.ShapeDtypeStruct((NDEV, D, NIN), F32),
            out_specs=pl.BlockSpec((None, D, NIN), lambda j, k: (j, 0, 0)))
        dx0, dn1, dsh1, dsc1 = _lnmod_bwd(dh, r["x"], lp["norm1_g"], sh1, sc1, dxm)
        grads = dict(
            dmod=jnp.concatenate([dsh1, dsc1, dg1, dsh2, dsc2, dg2], axis=1), dn1=dn1, dn2=dn2, dgqk=dgqk,
            dga=dga, dgs=dgs, dbsr=dbsr, dbsi=dbsi, dcr=dcr, dci=dci, dd=dd, dwg=dwg, dgb=dgb, dabr=dabr, dabi=dabi,
            dcb=dcb, dw_in=dw_in, dw_out=dw_out.reshape(NDEV, ROWS_OUT, D), dw_up=dw_up,
            dcw=dcw.reshape(NDEV, 3, FS), dw_down=dw_down.reshape(NDEV, ROWS_DOWN, D))
        return dx0, grads

    grad_x, gr = lax.scan(layer_bwd, dy, (layer_params, residuals), reverse=True)

    dar, dai, dldt, dbr_bd, dbi_bd = _s5_prep_bwd(ar_row, ai_row, ldt_row, braw_r, braw_i,
                                                  gr["dabr"], gr["dabi"], gr["dbsr"], gr["dbsi"])
    unt = lambda a: a.reshape((NL, G) + a.shape[3:])
    local = dict(
        ada_b=gr["dmod"].reshape(NL, N_MOD * D),
        norm1_g=gr["dn1"].reshape(NL, D), norm2_g=gr["dn2"].reshape(NL, D),
        q_norm_g=gr["dgqk"].reshape(NL, 2, NH, HEAD_DIM)[:, 0].sum(axis=1),
        k_norm_g=gr["dgqk"].reshape(NL, 2, NH, HEAD_DIM)[:, 1].sum(axis=1),
        ssm_a_re=dar.reshape(NL, G, SSM_STATE), ssm_a_im=dai.reshape(NL, G, SSM_STATE),
        ssm_log_dt=dldt.reshape(NL, G, SSM_STATE).sum(axis=-1),
        ssm_b_re=jnp.swapaxes(unt(_block_diag_extract(dbr_bd, SSM_GROUP, SSM_STATE)), -1, -2),
        ssm_b_im=jnp.swapaxes(unt(_block_diag_extract(dbi_bd, SSM_GROUP, SSM_STATE)), -1, -2),
        ssm_c_re=jnp.swapaxes(unt(_block_diag_extract(gr["dcr"], SSM_STATE, SSM_GROUP)), -1, -2),
        ssm_c_im=jnp.swapaxes(unt(_block_diag_extract(gr["dci"], SSM_STATE, SSM_GROUP)), -1, -2),
        ssm_d=gr["dd"].reshape(NL, G, SSM_GROUP),
        glu_w=unt(_block_diag_extract(gr["dwg"], SSM_GROUP, SSM_GROUP)),
        glu_b=gr["dgb"].reshape(NL, G, SSM_GROUP),
        attn_out_g=gr["dga"].reshape(NL, AW), ssm_out_g=gr["dgs"].reshape(NL, SW),
        ffn_conv_b=gr["dcb"].reshape(NL, 2 * NSH * FS),
    )

    def pack(tree):
        flat = jnp.concatenate([tree[n].reshape(-1) for n in small])
        pad = (-flat.shape[0]) % (8 * LANES)
        return jnp.pad(flat, (0, pad)).reshape(-1, LANES)

    (small_parts,) = _exchange([pack(local)], [0], False, "gather_small_grads")
    sg, sd, sm, sv = _adamw(small_parts[None], pack(weights)[None], pack(mom_m)[None], pack(mom_v)[None])

    def unpack(buf):
        flat = buf.reshape(-1)
        out, off = {}, 0
        for n in small:
            size = weights[n].size
            out[n] = flat[off:off + size].reshape(weights[n].shape)
            off += size
        return out

    ug, ud, um, uv = unpack(sg), unpack(sd), unpack(sm), unpack(sv)
    results = {n: (ug[n], ud[n], um[n], uv[n]) for n in small}

    (dmod_all,) = _exchange([gr["dmod"].reshape(NL, N_MOD * D)], [0], False, "gather_dmod")
    dmod_mine = lax.dynamic_slice_in_dim(dmod_all, me * NCA, NCA, axis=2)
    d_ada_w = _ada_bwd(jnp.transpose(c_all), jnp.transpose(dmod_mine, (1, 0, 2)))
    results["ada_w"] = tuple(_adamw(d_ada_w[:, None], ada_w, m_ada_w, v_ada_w))

    parts = _exchange([gr["dw_in"], gr["dw_out"], gr["dw_up"], gr["dcw"], gr["dw_down"]], [1] * 5, True,
                      "scatter_weight_grads")
    for n, pt in zip(("w_in", "w_out", "ffn_w_up", "ffn_conv_w", "ffn_w_down"), parts):
        results[n] = tuple(_adamw(pt, weights[n], mom_m[n], mom_v[n]))

    out = [loss, grad_x[None]]
    for k in range(4):
        out.extend(results[n][k] for n in names)
    return tuple(out)
```

```python
import functools
import math

import jax
import jax.numpy as jnp
from jax import lax
from jax.experimental import pallas as pl
from jax.experimental.pallas import tpu as pltpu

F32 = jnp.float32
BF16 = jnp.bfloat16
NDEV = 8
LANES = 128
HEAD_DIM = 64
SSM_GROUP = 16
SSM_STATE = 64
GROUPS_PER_TILE = LANES // SSM_GROUP
STATE_TILE = GROUPS_PER_TILE * SSM_STATE
N_MOD = 6
ATTN_STRIP = 32
EPS = 1e-6
ADAM_LR, ADAM_B1, ADAM_B2, ADAM_EPS, ADAM_WD, ADAM_STEP = 0.001, 0.9, 0.999, 1e-08, 0.01, 10
VMEM_LIMIT = 48 * 1024 * 1024
MESH_IDS = pl.DeviceIdType.MESH

NN = (((1,), (0,)), ((), ()))
NT = (((1,), (1,)), ((), ()))
TN = (((0,), (0,)), ((), ()))


def _dot(a, b, dims=NN):
    return lax.dot_general(a, b, dims, preferred_element_type=F32)


def _pcall(body, *, name, out_shape, in_specs, out_specs, grid=(), scratch=()):
    return pl.pallas_call(
        body, name=name, grid=grid, in_specs=in_specs, out_specs=out_specs, out_shape=out_shape,
        scratch_shapes=list(scratch),
        compiler_params=pltpu.CompilerParams(vmem_limit_bytes=VMEM_LIMIT))


def _row_tile(n, want=512):
    t = min(n, want)
    assert n % t == 0
    return t


def _my_index():
    return 4 * lax.axis_index("x") + 2 * lax.axis_index("y") + lax.axis_index("c")


def _exchange(arrs, axes, scatter, name):
    n = len(arrs)
    out_shape = []
    for a, ax in zip(arrs, axes):
        shp = a.shape if scatter else a.shape[:ax] + (NDEV,) + a.shape[ax:]
        out_shape.append(jax.ShapeDtypeStruct(shp, a.dtype))

    def body(*refs):
        ins, outs = refs[:n], refs[n:2 * n]
        send_sems, recv_sems, local_sems = refs[2 * n:]
        x, y, c = lax.axis_index("x"), lax.axis_index("y"), lax.axis_index("c")
        me = 4 * x + 2 * y + c

        def slab(ref, ax, idx):
            return ref.at[(slice(None),) * ax + (idx,)]

        local = []
        for a in range(n):
            src = slab(ins[a], axes[a], me) if scatter else ins[a]
            cp = pltpu.make_async_copy(src, slab(outs[a], axes[a], me), local_sems.at[a])
            cp.start()
            local.append(cp)
        sends, recvs = [], []
        for k in range(1, NDEV):
            px = 1 - x if k & 4 else x
            py = 1 - y if k & 2 else y
            pc = 1 - c if k & 1 else c
            peer = 4 * px + 2 * py + pc
            for a in range(n):
                src = slab(ins[a], axes[a], peer) if scatter else ins[a]
                common = dict(send_sem=send_sems.at[k - 1, a], recv_sem=recv_sems.at[k - 1, a],
                              device_id=(px, py, pc), device_id_type=MESH_IDS)
                snd = pltpu.make_async_remote_copy(src_ref=src, dst_ref=slab(outs[a], axes[a], me), **common)
                snd.start()
                sends.append(snd)
                recvs.append(pltpu.make_async_remote_copy(
                    src_ref=src, dst_ref=slab(outs[a], axes[a], peer), **common))
        for r in recvs:
            r.wait_recv()
        for s in sends:
            s.wait_send()
        for cp in local:
            cp.wait()

    hbm = pl.BlockSpec(memory_space=pltpu.HBM)
    outs = pl.pallas_call(
        body, name=name, out_shape=out_shape, in_specs=[hbm] * n, out_specs=[hbm] * n,
        scratch_shapes=[pltpu.SemaphoreType.DMA((NDEV - 1, n)), pltpu.SemaphoreType.DMA((NDEV - 1, n)),
                        pltpu.SemaphoreType.DMA((n,))],
    )(*arrs)
    return list(outs)


def _matmul(name, a, b, *, dims, grid, a_spec, b_spec, out_shape, out_specs, kaxis=None, acc_shape=None,
            extra=(), extra_specs=(), epilogue=None):
    nk = grid[kaxis] if kaxis is not None else 1
    ne = len(extra)
    multi = isinstance(out_shape, (list, tuple))
    n_out = len(out_shape) if multi else 1

    def body(*refs):
        a_ref, b_ref = refs[0], refs[1]
        ex = refs[2:2 + ne]
        outs = refs[2 + ne:2 + ne + n_out]

        def write(res):
            vals = epilogue(res, *[e[...] for e in ex]) if epilogue is not None else (res,)
            for o, v in zip(outs, vals):
                o[...] = v.astype(o.dtype)

        part = _dot(a_ref[...].astype(BF16), b_ref[...].astype(BF16), dims)
        if nk == 1:
            write(part)
        else:
            acc = refs[-1]
            k = pl.program_id(kaxis)

            @pl.when(k == 0)
            def _():
                acc[...] = part

            @pl.when(k > 0)
            def _():
                acc[...] += part

            @pl.when(k == nk - 1)
            def _():
                write(acc[...])

    scratch = [pltpu.VMEM(acc_shape, F32)] if nk > 1 else []
    return _pcall(body, name=name, grid=grid, in_specs=[a_spec, b_spec, *extra_specs],
                  out_specs=out_specs, out_shape=out_shape, scratch=scratch)(a, b, *extra)


def _rms(x, g):
    inv = lax.rsqrt(jnp.mean(x * x, axis=-1, keepdims=True) + EPS)
    return x * inv * g


def _lnmod_math(x, g, sh, sc):
    return _rms(x, g) * (1.0 + sc) + sh


def _qkn_math(p, g):
    m0 = lax.broadcasted_iota(jnp.int32, (1, LANES), 1) < HEAD_DIM
    sq = p * p
    s0 = jnp.sum(jnp.where(m0, sq, 0.0), axis=-1, keepdims=True)
    s1 = jnp.sum(jnp.where(m0, 0.0, sq), axis=-1, keepdims=True)
    inv = jnp.where(m0, lax.rsqrt(s0 / HEAD_DIM + EPS), lax.rsqrt(s1 / HEAD_DIM + EPS))
    return p * inv * g


def _glu_math(val, gate):
    return jax.nn.gelu(gate) * val


def _accumulate(ref, val, first):
    @pl.when(first)
    def _():
        ref[...] = val

    @pl.when(jnp.logical_not(first))
    def _():
        ref[...] += val


def _lnmod_fwd(x, g, sh, sc):
    L, D = x.shape
    tm = _row_tile(L)

    def body(x_ref, g_ref, sh_ref, sc_ref, h_ref):
        h_ref[...] = _lnmod_math(x_ref[...], g_ref[...], sh_ref[...], sc_ref[...]).astype(BF16)

    row = pl.BlockSpec((tm, D), lambda i: (i, 0))
    vec = pl.BlockSpec((1, D), lambda i: (0, 0))
    return _pcall(body, name="lnmod_fwd", grid=(L // tm,), in_specs=[row, vec, vec, vec], out_specs=row,
                  out_shape=jax.ShapeDtypeStruct((L, D), BF16))(x, g, sh, sc)


def _lnmod_bwd(dh, x, g, sh, sc, dres):
    L, D = x.shape
    tm = _row_tile(L)

    def body(dh_ref, x_ref, g_ref, sh_ref, sc_ref, res_ref, dx_ref, dg_ref, dsh_ref, dsc_ref):
        _, vjp = jax.vjp(_lnmod_math, x_ref[...], g_ref[...], sh_ref[...], sc_ref[...])
        dx, dg, dsh, dsc = vjp(dh_ref[...])
        dx_ref[...] = dx + res_ref[...]
        first = pl.program_id(0) == 0
        _accumulate(dg_ref, dg, first)
        _accumulate(dsh_ref, dsh, first)
        _accumulate(dsc_ref, dsc, first)

    row = pl.BlockSpec((tm, D), lambda i: (i, 0))
    vec = pl.BlockSpec((1, D), lambda i: (0, 0))
    vs = jax.ShapeDtypeStruct((1, D), F32)
    return _pcall(body, name="lnmod_bwd", grid=(L // tm,), in_specs=[row, row, vec, vec, vec, row],
                  out_specs=[row, vec, vec, vec],
                  out_shape=[jax.ShapeDtypeStruct((L, D), F32), vs, vs, vs])(dh, x, g, sh, sc, dres)


def _gate_bwd(dx, a, gate):
    L, D = dx.shape
    tm = _row_tile(L)

    def body(dx_ref, a_ref, g_ref, da_ref, dg_ref):
        dxv = dx_ref[...]
        da_ref[...] = (g_ref[...] * dxv).astype(BF16)
        _accumulate(dg_ref, jnp.sum(dxv * a_ref[...], axis=0, keepdims=True), pl.program_id(0) == 0)

    row = pl.BlockSpec((tm, D), lambda i: (i, 0))
    vec = pl.BlockSpec((1, D), lambda i: (0, 0))
    return _pcall(body, name="gate_bwd", grid=(L // tm,), in_specs=[row, row, vec], out_specs=[row, vec],
                  out_shape=[jax.ShapeDtypeStruct((L, D), BF16), jax.ShapeDtypeStruct((1, D), F32)])(dx, a, gate)


def _qknorm_fwd(p, gqk, AW):
    L = p.shape[0]
    tm = _row_tile(L)
    ncol = 2 * AW // LANES

    def body(p_ref, g_ref, o_ref):
        o_ref[...] = _qkn_math(p_ref[...], g_ref[...]).astype(BF16)

    blk = pl.BlockSpec((tm, LANES), lambda i, j: (i, j))
    vec = pl.BlockSpec((1, LANES), lambda i, j: (0, j))
    return _pcall(body, name="qknorm_fwd", grid=(L // tm, ncol), in_specs=[blk, vec], out_specs=blk,
                  out_shape=jax.ShapeDtypeStruct((L, 2 * AW), BF16))(p, gqk)


def _qknorm_bwd(dqk, p, gqk, AW):
    L = p.shape[0]
    tm = _row_tile(L)
    ncol = 2 * AW // LANES

    def body(d_ref, p_ref, g_ref, dp_ref, dg_ref):
        _, vjp = jax.vjp(_qkn_math, p_ref[...], g_ref[...])
        dp, dg = vjp(d_ref[...])
        dp_ref[...] = dp
        _accumulate(dg_ref, dg, pl.program_id(1) == 0)

    blk = pl.BlockSpec((tm, LANES), lambda j, i: (i, j))
    vec = pl.BlockSpec((1, LANES), lambda j, i: (0, j))
    return _pcall(body, name="qknorm_bwd", grid=(ncol, L // tm), in_specs=[blk, blk, vec], out_specs=[blk, vec],
                  out_shape=[jax.ShapeDtypeStruct((L, 2 * AW), F32),
                             jax.ShapeDtypeStruct((1, 2 * AW), F32)])(dqk, p, gqk)


def _outnorm_fwd(oa, os_, ga, gs):
    L, AW = oa.shape
    SW = os_.shape[1]
    tm = _row_tile(L)

    def body(oa_ref, os_ref, ga_ref, gs_ref, o_ref):
        o_ref[:, :AW] = _rms(oa_ref[...], ga_ref[...]).astype(BF16)
        o_ref[:, AW:] = _rms(os_ref[...], gs_ref[...]).astype(BF16)

    ra = pl.BlockSpec((tm, AW), lambda i: (i, 0))
    rs = pl.BlockSpec((tm, SW), lambda i: (i, 0))
    va = pl.BlockSpec((1, AW), lambda i: (0, 0))
    vs = pl.BlockSpec((1, SW), lambda i: (0, 0))
    ro = pl.BlockSpec((tm, AW + SW), lambda i: (i, 0))
    return _pcall(body, name="outnorm_fwd", grid=(L // tm,), in_specs=[ra, rs, va, vs], out_specs=ro,
                  out_shape=jax.ShapeDtypeStruct((L, AW + SW), BF16))(oa, os_, ga, gs)


def _outnorm_bwd(do, oa, os_, ga, gs):
    L, AW = oa.shape
    SW = os_.shape[1]
    tm = _row_tile(L)

    def body(do_ref, oa_ref, os_ref, ga_ref, gs_ref, doa_ref, dos_ref, dga_ref, dgs_ref):
        first = pl.program_id(0) == 0
        _, vjp_a = jax.vjp(_rms, oa_ref[...], ga_ref[...])
        doa, dga = vjp_a(do_ref[:, :AW])
        _, vjp_s = jax.vjp(_rms, os_ref[...], gs_ref[...])
        dos, dgs = vjp_s(do_ref[:, AW:])
        doa_ref[...] = doa
        dos_ref[...] = dos
        _accumulate(dga_ref, dga, first)
        _accumulate(dgs_ref, dgs, first)

    ra = pl.BlockSpec((tm, AW), lambda i: (i, 0))
    rs = pl.BlockSpec((tm, SW), lambda i: (i, 0))
    va = pl.BlockSpec((1, AW), lambda i: (0, 0))
    vs = pl.BlockSpec((1, SW), lambda i: (0, 0))
    ro = pl.BlockSpec((tm, AW + SW), lambda i: (i, 0))
    return _pcall(body, name="outnorm_bwd", grid=(L // tm,), in_specs=[ro, ra, rs, va, vs],
                  out_specs=[ra, rs, va, vs],
                  out_shape=[jax.ShapeDtypeStruct((L, AW), F32), jax.ShapeDtypeStruct((L, SW), F32),
                             jax.ShapeDtypeStruct((1, AW), F32), jax.ShapeDtypeStruct((1, SW), F32)])(
                                 do, oa, os_, ga, gs)


def _softplus_neg_abs(z):
    return jnp.log(1.0 + jnp.exp(-jnp.abs(z)))


def _split_dot(x, m):
    hi = x.astype(BF16)
    lo = (x - hi.astype(F32)).astype(BF16)
    return _dot(hi, m) + _dot(lo, m)


def _attn_masks(B):
    row = lax.broadcasted_iota(jnp.int32, (B, B), 0)
    col = lax.broadcasted_iota(jnp.int32, (B, B), 1)
    strict = col < row
    upper = jnp.where(row > col, 1.0, 0.0).astype(BF16)
    lower = jnp.where(row < col, 1.0, 0.0).astype(BF16)
    return strict, upper, lower


def _attn_fwd(qk, p, AW, B):
    L = qk.shape[0]
    HP = AW // LANES
    nb = L // B

    def body(q_ref, k_ref, v_ref, o_ref, tot_ref, *scr):
        lb_scr = (scr[0:2], scr[2:4])
        tail_scr = (scr[4:6], scr[6:8])
        sum_scr = scr[8:10]
        i = pl.program_id(1)
        m0 = lax.broadcasted_iota(jnp.int32, (1, LANES), 1) < HEAD_DIM
        strict, upper, _ = _attn_masks(B)
        q = q_ref[...] * 0.125
        zq = jnp.zeros_like(q)
        qh = (jnp.where(m0, q, zq), jnp.where(m0, zq, q))

        def keys(j):
            start = pl.multiple_of(jnp.maximum(j, 0) * B, B)
            return k_ref[pl.ds(start, B), :]

        def vals(j):
            start = pl.multiple_of(jnp.maximum(j, 0) * B, B)
            return v_ref[pl.ds(start, B), :].astype(BF16)

        strips = [slice(s, min(s + ATTN_STRIP, B)) for s in range(0, B, ATTN_STRIP)]

        def scores(j):
            kj = keys(j)
            return tuple(_dot(qh[h], kj, NT) for h in (0, 1))

        def logits(zs2, slot, diag):
            for h in (0, 1):
                z = zs2[h]
                his, los = [], []
                for rows in strips:
                    zs = z[rows]
                    lb = jnp.minimum(zs, 0.0) - _softplus_neg_abs(zs)
                    l1 = lb - zs
                    if diag:
                        l1 = jnp.where(strict[rows], l1, 0.0)
                    lb_scr[slot][h][rows, :] = lb
                    hi = l1.astype(BF16)
                    his.append(hi)
                    los.append((l1 - hi.astype(F32)).astype(BF16))
                    rsum = jnp.sum(l1, axis=-1, keepdims=True)
                    if h == 0:
                        sum_scr[slot][rows, :] = jnp.broadcast_to(rsum, (rows.stop - rows.start, LANES))
                    else:
                        sum_scr[slot][rows, :] = jnp.where(m0, sum_scr[slot][rows, :], rsum)
                cat = lambda xs: jnp.concatenate(xs, axis=0)
                tail_scr[slot][h][...] = _dot(cat(his), upper) + _dot(cat(los), upper)

        def attend(j, slot, diag):
            vj = vals(j)
            pv = []
            for h in (0, 1):
                ws = []
                for rows in strips:
                    w = jnp.exp(lb_scr[slot][h][rows, :] + tail_scr[slot][h][rows, :])
                    if diag:
                        w = jnp.where(strict[rows], w, 0.0)
                    ws.append(w.astype(BF16))
                pv.append(_dot(jnp.concatenate(ws, axis=0), vj))
            return jnp.where(m0, pv[0], pv[1])

        logits(scores(i), 0, True)
        o_ref[...] = attend(i, 0, True)
        tot_ref[...] = sum_scr[0][...]

        def half(j, slot):
            z = scores(j - 1)
            pv = attend(j, slot, False)
            logits(z, 1 - slot, False)
            o_ref[...] += pv * jnp.exp(tot_ref[...])
            tot_ref[...] += sum_scr[slot][...]

        @pl.when(i > 0)
        def _():
            logits(scores(i - 1), 1, False)

            @pl.loop(0, (i + 1) // 2)
            def _(t):
                j = i - 1 - 2 * t
                half(j, 1)

                @pl.when(j > 0)
                def _():
                    half(j - 1, 0)


    qspec = pl.BlockSpec((B, LANES), lambda hp, i: (i, hp))
    kspec = pl.BlockSpec((L, LANES), lambda hp, i: (0, HP + hp))
    vspec = pl.BlockSpec((L, LANES), lambda hp, i: (0, 2 * HP + hp))
    ospec = pl.BlockSpec((B, LANES), lambda hp, i: (i, hp))
    shp = jax.ShapeDtypeStruct((L, AW), F32)
    return _pcall(body, name="attn_fwd", grid=(HP, nb), in_specs=[qspec, kspec, vspec],
                  out_specs=[ospec, ospec], out_shape=[shp, shp],
                  scratch=[pltpu.VMEM((B, B), F32)] * 8 + [pltpu.VMEM((B, LANES), F32)] * 2)(qk, qk, p)


def _attn_bwd(qk, p, do, tot, AW, B):
    L = qk.shape[0]
    HP = AW // LANES
    nb = L // B

    def body(q_ref, k_ref, v_ref, do_ref, tot_ref, dq_ref, dk_ref, dv_ref):
        i = pl.program_id(1)
        m0 = lax.broadcasted_iota(jnp.int32, (1, LANES), 1) < HEAD_DIM
        strict, upper, lower = _attn_masks(B)

        @pl.when(i == 0)
        def _():
            dk_ref[...] = jnp.zeros_like(dk_ref)
            dv_ref[...] = jnp.zeros_like(dv_ref)

        q = q_ref[...] * 0.125
        zq = jnp.zeros_like(q)
        qh = (jnp.where(m0, q, zq), jnp.where(m0, zq, q))
        dob = do_ref[...].astype(BF16)
        doh = (jnp.where(m0, dob, zq), jnp.where(m0, zq, dob))
        tv = tot_ref[...]
        th = (jnp.sum(jnp.where(m0, tv, 0.0), axis=-1, keepdims=True) / HEAD_DIM,
              jnp.sum(jnp.where(m0, 0.0, tv), axis=-1, keepdims=True) / HEAD_DIM)

        def block(j, carry, diag):
            pls, es, dq = carry
            start = pl.multiple_of(j * B, B)
            kj = k_ref[pl.ds(start, B), :]
            vj = v_ref[pl.ds(start, B), :].astype(BF16)
            zk = jnp.zeros_like(kj)
            kh = (jnp.where(m0, kj, zk), jnp.where(m0, zk, kj))
            new_pl, new_e = [], []
            dk_blk = jnp.zeros((B, LANES), F32)
            dv_blk = jnp.zeros((B, LANES), F32)
            for h in (0, 1):
                z = _dot(qh[h], kj, NT)
                sp = _softplus_neg_abs(z)
                lb = jnp.minimum(z, 0.0) - sp
                l1 = lb - z
                if diag:
                    l1 = jnp.where(strict, l1, 0.0)
                bsum = jnp.sum(l1, axis=-1, keepdims=True)
                right = th[h] - pls[h] - bsum
                tail = _split_dot(l1, upper)
                w = jnp.exp(lb + tail + right)
                if diag:
                    w = jnp.where(strict, w, 0.0)
                dw = _dot(doh[h], vj, NT)
                e = dw * w
                dl1 = _split_dot(e, lower) + es[h]
                beta = jnp.exp(lb)
                dz = e * (1.0 - beta) - dl1 * beta
                if diag:
                    dz = jnp.where(strict, dz, 0.0)
                dzb = dz.astype(BF16)
                wb = w.astype(BF16)
                dq = dq + _dot(dzb, kh[h])
                dk_blk = dk_blk + _dot(dzb, qh[h], TN)
                dv_blk = dv_blk + _dot(wb, doh[h], TN)
                new_pl.append(pls[h] + bsum)
                new_e.append(es[h] + jnp.sum(e, axis=-1, keepdims=True))
            dk_ref[pl.ds(start, B), :] += dk_blk
            dv_ref[pl.ds(start, B), :] += dv_blk
            return tuple(new_pl), tuple(new_e), dq

        zero = jnp.zeros((B, 1), F32)
        carry = ((zero, zero), (zero, zero), jnp.zeros((B, LANES), F32))
        carry = lax.fori_loop(0, i, lambda j, c: block(j, c, False), carry)
        _, _, dq = block(i, carry, True)
        dq_ref[...] = dq * 0.125

    qspec = pl.BlockSpec((B, LANES), lambda hp, i: (i, hp))
    kspec = pl.BlockSpec((L, LANES), lambda hp, i: (0, HP + hp))
    vspec = pl.BlockSpec((L, LANES), lambda hp, i: (0, 2 * HP + hp))
    full = pl.BlockSpec((L, LANES), lambda hp, i: (0, hp))
    shp = jax.ShapeDtypeStruct((L, AW), F32)
    return _pcall(body, name="attn_bwd", grid=(HP, nb), in_specs=[qspec, kspec, vspec, qspec, qspec],
                  out_specs=[qspec, full, full], out_shape=[shp, shp, shp])(qk, qk, p, do, tot)


def _s5_disc(ar, ai, ldt):
    dt = jnp.exp(ldt)
    mag = jnp.exp(dt * ar)
    abr = mag * jnp.cos(dt * ai)
    abi = mag * jnp.sin(dt * ai)
    emr = abr - 1.0
    emi = abi
    den = ar * ar + ai * ai
    fr = (emr * ar + emi * ai) / den
    fi = (emi * ar - emr * ai) / den
    return abr, abi, fr, fi


def _s5_params_math(ar, ai, ldt, br, bi):
    abr, abi, fr, fi = _s5_disc(ar, ai, ldt)
    return abr, abi, fr * br - fi * bi, fr * bi + fi * br


def _scan_rows(xr, xi, lev, T, reverse):
    row = lax.broadcasted_iota(jnp.int32, xr.shape, 0)
    d, k = 1, 0
    while d < T:
        kr, ki = lev(k)
        if reverse:
            keep = row < T - d
            sr = jnp.where(keep, pltpu.roll(xr, T - d, 0), 0.0)
            si = jnp.where(keep, pltpu.roll(xi, T - d, 0), 0.0)
            ki = -ki
        else:
            keep = row >= d
            sr = jnp.where(keep, pltpu.roll(xr, d, 0), 0.0)
            si = jnp.where(keep, pltpu.roll(xi, d, 0), 0.0)
        xr, xi = xr + (kr * sr - ki * si), xi + (kr * si + ki * sr)
        d, k = 2 * d, k + 1
    return xr, xi


def _s5_prep(ar, ai, ldt, braw_r, braw_i, T):
    NL, _, NS = ar.shape
    GT = NS // STATE_TILE
    nlev = max(1, int(math.log2(T)))
    LV = 8 * ((nlev + 7) // 8)

    def body(ar_ref, ai_ref, ldt_ref, br_ref, bi_ref,
             abr_ref, abi_ref, levr_ref, levi_ref, powr_ref, powi_ref, bsr_ref, bsi_ref):
        levr_ref[...] = jnp.zeros_like(levr_ref)
        levi_ref[...] = jnp.zeros_like(levi_ref)
        for t in range(GT):
            sl = slice(t * STATE_TILE, (t + 1) * STATE_TILE)
            abr, abi, bsr, bsi = _s5_params_math(ar_ref[:, sl], ai_ref[:, sl], ldt_ref[:, sl],
                                                 br_ref[t], bi_ref[t])
            abr_ref[:, sl] = abr
            abi_ref[:, sl] = abi
            bsr_ref[t] = bsr.astype(BF16)
            bsi_ref[t] = bsi.astype(BF16)
            kr, ki = abr, abi
            for k in range(nlev):
                levr_ref[k:k + 1, sl] = kr
                levi_ref[k:k + 1, sl] = ki
                kr, ki = kr * kr - ki * ki, 2.0 * kr * ki
        for s in range(NS // LANES):
            sl = slice(s * LANES, (s + 1) * LANES)
            row = lax.broadcasted_iota(jnp.int32, (T, LANES), 0)
            xr = jnp.where(row == 0, abr_ref[:, sl], 0.0)
            xi = jnp.where(row == 0, abi_ref[:, sl], 0.0)
            lev = lambda k, sl=sl: (levr_ref[k:k + 1, sl], levi_ref[k:k + 1, sl])
            xr, xi = _scan_rows(xr, xi, lev, T, False)
            powr_ref[:, sl] = xr
            powi_ref[:, sl] = xi

    rowspec = pl.BlockSpec((None, 1, NS), lambda l: (l, 0, 0))
    bspec = pl.BlockSpec((None, GT, LANES, STATE_TILE), lambda l: (l, 0, 0, 0))
    levspec = pl.BlockSpec((None, LV, NS), lambda l: (l, 0, 0))
    powspec = pl.BlockSpec((None, T, NS), lambda l: (l, 0, 0))
    rs = jax.ShapeDtypeStruct((NL, 1, NS), F32)
    ls = jax.ShapeDtypeStruct((NL, LV, NS), F32)
    ps = jax.ShapeDtypeStruct((NL, T, NS), F32)
    bs = jax.ShapeDtypeStruct((NL, GT, LANES, STATE_TILE), BF16)
    return _pcall(body, name="s5_prep", grid=(NL,), in_specs=[rowspec] * 3 + [bspec] * 2,
                  out_specs=[rowspec, rowspec, levspec, levspec, powspec, powspec, bspec, bspec],
                  out_shape=[rs, rs, ls, ls, ps, ps, bs, bs])(ar, ai, ldt, braw_r, braw_i)


def _s5_prep_bwd(ar, ai, ldt, braw_r, braw_i, dabr, dabi, dbsr, dbsi):
    NL, _, NS = ar.shape
    GT = NS // STATE_TILE

    def body(ar_ref, ai_ref, ldt_ref, br_ref, bi_ref, dabr_ref, dabi_ref, dbsr_ref, dbsi_ref,
             dar_ref, dai_ref, dldt_ref, dbr_ref, dbi_ref):
        for t in range(GT):
            sl = slice(t * STATE_TILE, (t + 1) * STATE_TILE)
            _, vjp = jax.vjp(_s5_params_math, ar_ref[:, sl], ai_ref[:, sl], ldt_ref[:, sl],
                             br_ref[t], bi_ref[t])
            dar, dai, dldt, dbr, dbi = vjp((dabr_ref[:, sl], dabi_ref[:, sl], dbsr_ref[t], dbsi_ref[t]))
            dar_ref[:, sl] = dar
            dai_ref[:, sl] = dai
            dldt_ref[:, sl] = dldt
            dbr_ref[t] = dbr
            dbi_ref[t] = dbi

    rowspec = pl.BlockSpec((None, 1, NS), lambda l: (l, 0, 0))
    bspec = pl.BlockSpec((None, GT, LANES, STATE_TILE), lambda l: (l, 0, 0, 0))
    rs = jax.ShapeDtypeStruct((NL, 1, NS), F32)
    bs = jax.ShapeDtypeStruct((NL, GT, LANES, STATE_TILE), F32)
    return _pcall(body, name="s5_prep_bwd", grid=(NL,), in_specs=[rowspec] * 3 + [bspec] * 2 + [rowspec] * 2 + [bspec] * 2,
                  out_specs=[rowspec] * 3 + [bspec] * 2, out_shape=[rs, rs, rs, bs, bs])(
                      ar, ai, ldt, braw_r, braw_i, dabr, dabi, dbsr, dbsi)


def _s5_specs(L, SW, T, ucol0, rev):
    GT = SW // LANES
    nc = L // T
    cidx = (lambda c: nc - 1 - c) if rev else (lambda c: c)
    return dict(
        GT=GT, nc=nc, cidx=cidx,
        u=pl.BlockSpec((T, LANES), lambda j, c: (cidx(c), ucol0 + j)),
        chan=pl.BlockSpec((T, LANES), lambda j, c: (cidx(c), j)),
        state=pl.BlockSpec((T, STATE_TILE), lambda j, c: (cidx(c), j)),
        bmat=pl.BlockSpec((None, LANES, STATE_TILE), lambda j, c: (j, 0, 0)),
        cmat=pl.BlockSpec((None, STATE_TILE, LANES), lambda j, c: (j, 0, 0)),
        gmat=pl.BlockSpec((None, LANES, LANES), lambda j, c: (j, 0, 0)),
        cvec=pl.BlockSpec((1, LANES), lambda j, c: (0, j)),
        svec=pl.BlockSpec((1, STATE_TILE), lambda j, c: (0, j)),
    )


def _s5_fwd(p, sp, SW, ucol0, T):
    L = p.shape[0]
    S = _s5_specs(L, SW, T, ucol0, False)
    NS = S["GT"] * STATE_TILE
    LV = sp["levr"].shape[0]

    def body(u_ref, bsr_ref, bsi_ref, levr_ref, levi_ref, powr_ref, powi_ref, cr_ref, ci_ref,
             d_ref, wg_ref, gb_ref, o_ref, sr_ref, si_ref, y_ref, carry_ref):
        @pl.when(pl.program_id(1) == 0)
        def _():
            carry_ref[...] = jnp.zeros_like(carry_ref)

        u = u_ref[...]
        ub = u.astype(BF16)
        bur = _dot(ub, bsr_ref[...])
        bui = _dot(ub, bsi_ref[...])
        for s in range(STATE_TILE // LANES):
            sl = slice(s * LANES, (s + 1) * LANES)
            lev = lambda k, sl=sl: (levr_ref[k:k + 1, sl], levi_ref[k:k + 1, sl])
            xr, xi = _scan_rows(bur[:, sl], bui[:, sl], lev, T, False)
            cr, ci = carry_ref[0:1, sl], carry_ref[1:2, sl]
            pr, pi = powr_ref[:, sl], powi_ref[:, sl]
            sr_ref[:, sl] = xr + (pr * cr - pi * ci)
            si_ref[:, sl] = xi + (pr * ci + pi * cr)
            carry_ref[0:1, sl] = sr_ref[T - 1:T, sl]
            carry_ref[1:2, sl] = si_ref[T - 1:T, sl]
        y = (_dot(sr_ref[...].astype(BF16), cr_ref[...]) - _dot(si_ref[...].astype(BF16), ci_ref[...])
             + d_ref[...] * u)
        y_ref[...] = y
        yg = jax.nn.gelu(y)
        gate = jax.nn.sigmoid(_dot(yg.astype(BF16), wg_ref[...]) + gb_ref[...])
        o_ref[...] = yg * gate

    lvspec = pl.BlockSpec((LV, STATE_TILE), lambda j, c: (0, j))
    pwspec = pl.BlockSpec((T, STATE_TILE), lambda j, c: (0, j))
    cs = jax.ShapeDtypeStruct((L, SW), F32)
    ss = jax.ShapeDtypeStruct((L, NS), F32)
    return _pcall(
        body, name="s5_fwd", grid=(S["GT"], S["nc"]),
        in_specs=[S["u"], S["bmat"], S["bmat"], lvspec, lvspec, pwspec, pwspec, S["cmat"], S["cmat"],
                  S["cvec"], S["gmat"], S["cvec"]],
        out_specs=[S["chan"], S["state"], S["state"], S["chan"]], out_shape=[cs, ss, ss, cs],
        scratch=[pltpu.VMEM((8, STATE_TILE), F32)],
    )(p, sp["bsr"], sp["bsi"], sp["levr"], sp["levi"], sp["powr"], sp["powi"], sp["crT"], sp["ciT"],
      sp["d"], sp["wg"], sp["gb"])


def _s5_bwd(do, p, s_re, s_im, ypre, sp, SW, ucol0, T):
    L = p.shape[0]
    S = _s5_specs(L, SW, T, ucol0, True)
    GT, nc, cidx = S["GT"], S["nc"], S["cidx"]
    NS = GT * STATE_TILE
    LV = sp["levr"].shape[0]
    T8 = T // 8

    def body(do_ref, u_ref, sr_ref, si_ref, hr_ref, hi_ref, y_ref, bsr_ref, bsi_ref, levr_ref, levi_ref,
             rpr_ref, rpi_ref, cr_ref, ci_ref, d_ref, wg_ref, gb_ref,
             du_ref, dbsr_ref, dbsi_ref, dcr_ref, dci_ref, dd_ref, dwg_ref, dgb_ref, dar_ref, dai_ref,
             carry_ref, lam_r, lam_i):
        c = pl.program_id(1)
        first = c == 0

        @pl.when(first)
        def _():
            carry_ref[...] = jnp.zeros_like(carry_ref)

        u = u_ref[...]
        ub = u.astype(BF16)
        y = y_ref[...]
        yg, gelu_vjp = jax.vjp(jax.nn.gelu, y)
        ygb = yg.astype(BF16)
        gate = jax.nn.sigmoid(_dot(ygb, wg_ref[...]) + gb_ref[...])
        dout = do_ref[...]
        dt = dout * yg * gate * (1.0 - gate)
        dtb = dt.astype(BF16)
        dyg = dout * gate + _dot(dtb, wg_ref[...], NT)
        (dy,) = gelu_vjp(dyg)
        dyb = dy.astype(BF16)
        _accumulate(dwg_ref, _dot(ygb, dtb, TN), first)
        _accumulate(dgb_ref, jnp.sum(dt, axis=0, keepdims=True), first)
        _accumulate(dd_ref, jnp.sum(dy * u, axis=0, keepdims=True), first)
        srb = sr_ref[...].astype(BF16)
        sib = si_ref[...].astype(BF16)
        _accumulate(dcr_ref, _dot(srb, dyb, TN), first)
        _accumulate(dci_ref, -_dot(sib, dyb, TN), first)
        dsr = _dot(dyb, cr_ref[...], NT)
        dsi = -_dot(dyb, ci_ref[...], NT)
        last_chunk = cidx(c) == 0
        row = lax.broadcasted_iota(jnp.int32, (T, LANES), 0)
        for s in range(STATE_TILE // LANES):
            sl = slice(s * LANES, (s + 1) * LANES)
            lev = lambda k, sl=sl: (levr_ref[k:k + 1, sl], levi_ref[k:k + 1, sl])
            xr, xi = _scan_rows(dsr[:, sl], dsi[:, sl], lev, T, True)
            cr, ci = carry_ref[0:1, sl], carry_ref[1:2, sl]
            pr, pi = rpr_ref[:, sl], -rpi_ref[:, sl]
            lam_r[:, sl] = xr + (pr * cr - pi * ci)
            lam_i[:, sl] = xi + (pr * ci + pi * cr)
            carry_ref[0:1, sl] = lam_r[0:1, sl]
            carry_ref[1:2, sl] = lam_i[0:1, sl]
            keep = jnp.logical_not(last_chunk)
            pr0 = jnp.where(keep, hr_ref[7:8, sl], 0.0)
            pi0 = jnp.where(keep, hi_ref[7:8, sl], 0.0)
            spr = jnp.where(row == 0, pr0, pltpu.roll(sr_ref[:, sl], 1, 0))
            spi = jnp.where(row == 0, pi0, pltpu.roll(si_ref[:, sl], 1, 0))
            lr, li = lam_r[:, sl], lam_i[:, sl]
            dar = jnp.sum(lr * spr + li * spi, axis=0, keepdims=True)
            dai = jnp.sum(li * spr - lr * spi, axis=0, keepdims=True)

            @pl.when(first)
            def _():
                dar_ref[:, sl] = dar
                dai_ref[:, sl] = dai

            @pl.when(jnp.logical_not(first))
            def _():
                dar_ref[:, sl] += dar
                dai_ref[:, sl] += dai

        lrb = lam_r[...].astype(BF16)
        lib = lam_i[...].astype(BF16)
        _accumulate(dbsr_ref, _dot(ub, lrb, TN), first)
        _accumulate(dbsi_ref, _dot(ub, lib, TN), first)
        du_ref[...] = dy * d_ref[...] + _dot(lrb, bsr_ref[...], NT) + _dot(lib, bsi_ref[...], NT)

    halo = pl.BlockSpec((8, STATE_TILE), lambda j, c: (jnp.maximum(cidx(c) * T8 - 1, 0), j))
    lvspec = pl.BlockSpec((LV, STATE_TILE), lambda j, c: (0, j))
    pwspec = pl.BlockSpec((T, STATE_TILE), lambda j, c: (0, j))
    f = lambda *s: jax.ShapeDtypeStruct(s, F32)
    return _pcall(
        body, name="s5_bwd", grid=(GT, nc),
        in_specs=[S["chan"], S["u"], S["state"], S["state"], halo, halo, S["chan"], S["bmat"], S["bmat"],
                  lvspec, lvspec, pwspec, pwspec, S["cmat"], S["cmat"], S["cvec"], S["gmat"], S["cvec"]],
        out_specs=[S["chan"], S["bmat"], S["bmat"], S["cmat"], S["cmat"], S["cvec"], S["gmat"], S["cvec"],
                   S["svec"], S["svec"]],
        out_shape=[f(L, SW), f(GT, LANES, STATE_TILE), f(GT, LANES, STATE_TILE), f(GT, STATE_TILE, LANES),
                   f(GT, STATE_TILE, LANES), f(1, SW), f(GT, LANES, LANES), f(1, SW), f(1, NS), f(1, NS)],
        scratch=[pltpu.VMEM((8, STATE_TILE), F32), pltpu.VMEM((T, STATE_TILE), F32),
                 pltpu.VMEM((T, STATE_TILE), F32)],
    )(do, p, s_re, s_im, s_re, s_im, ypre, sp["bsr"], sp["bsi"], sp["levr"], sp["levi"],
      sp["rpowr"], sp["rpowi"], sp["crT"], sp["ciT"], sp["d"], sp["wg"], sp["gb"])


def _conv_taps(xc, h6, h7, row):
    x1 = jnp.where(row == 0, h7, pltpu.roll(xc, 1, 0))
    x2 = jnp.where(row == 0, h6, jnp.where(row == 1, h7, pltpu.roll(xc, 2, 0)))
    return x1, x2


def _conv_halves(up_ref, halo_ref, w_ref, b_ref, tm, FS):
    first = pl.program_id(1) == 0
    row = lax.broadcasted_iota(jnp.int32, (tm, FS), 0)
    outs, taps = [], []
    for s in (0, 1):
        xc = up_ref[s]
        h6 = jnp.where(first, 0.0, halo_ref[s, 6:7, :])
        h7 = jnp.where(first, 0.0, halo_ref[s, 7:8, :])
        x1, x2 = _conv_taps(xc, h6, h7, row)
        outs.append(b_ref[s] + x2 * w_ref[s, 0:1, :] + x1 * w_ref[s, 1:2, :] + xc * w_ref[s, 2:3, :])
        taps.append((x2, x1, xc))
    return outs, taps


def _convglu_specs(L, FS, tm):
    t8 = tm // 8
    return dict(
        up=pl.BlockSpec((2, None, tm, FS), lambda j, i: (0, j, i, 0)),
        halo=pl.BlockSpec((2, None, 8, FS), lambda j, i: (0, j, jnp.maximum(i * t8 - 1, 0), 0)),
        w=pl.BlockSpec((2, None, 3, FS), lambda j, i: (0, j, 0, 0)),
        b=pl.BlockSpec((2, None, 1, FS), lambda j, i: (0, j, 0, 0)),
        act=pl.BlockSpec((None, tm, FS), lambda j, i: (j, i, 0)),
    )


def _convglu_fwd(up, w, b):
    _, NSH, L, FS = up.shape
    tm = _row_tile(L)
    S = _convglu_specs(L, FS, tm)

    def body(up_ref, halo_ref, w_ref, b_ref, act_ref):
        (val, gate), _ = _conv_halves(up_ref, halo_ref, w_ref, b_ref, tm, FS)
        act_ref[...] = _glu_math(val, gate).astype(BF16)

    return _pcall(body, name="convglu_fwd", grid=(NSH, L // tm), in_specs=[S["up"], S["halo"], S["w"], S["b"]],
                  out_specs=S["act"], out_shape=jax.ShapeDtypeStruct((NSH, L, FS), BF16))(up, up, w, b)


def _convglu_bwd(up, dact, w, b):
    _, NSH, L, FS = up.shape
    tm = _row_tile(L)
    S = _convglu_specs(L, FS, tm)

    def body(up_ref, halo_ref, w_ref, b_ref, dact_ref, dc_ref, dw_ref, db_ref):
        first = pl.program_id(1) == 0
        (val, gate), taps = _conv_halves(up_ref, halo_ref, w_ref, b_ref, tm, FS)
        _, vjp = jax.vjp(_glu_math, val, gate)
        dcs = vjp(dact_ref[...])
        for s in (0, 1):
            dc = dcs[s]
            dc_ref[s] = dc
            sums = [jnp.sum(dc * t, axis=0, keepdims=True) for t in taps[s]]
            dbs = jnp.sum(dc, axis=0, keepdims=True)

            @pl.when(first)
            def _():
                for t in range(3):
                    dw_ref[s, t:t + 1, :] = sums[t]
                db_ref[s] = dbs

            @pl.when(jnp.logical_not(first))
            def _():
                for t in range(3):
                    dw_ref[s, t:t + 1, :] += sums[t]
                db_ref[s] += dbs

    f = lambda *s: jax.ShapeDtypeStruct(s, F32)
    return _pcall(body, name="convglu_bwd", grid=(NSH, L // tm),
                  in_specs=[S["up"], S["halo"], S["w"], S["b"], S["act"]],
                  out_specs=[S["up"], S["w"], S["b"]],
                  out_shape=[f(2, NSH, L, FS), f(2, NSH, 3, FS), f(2, NSH, 1, FS)])(up, up, w, b, dact)


def _conv_transpose(dc, w):
    NS8, L, FS = dc.shape
    tm = _row_tile(L)
    t8 = tm // 8
    nt = L // tm

    def body(dc_ref, nxt_ref, w_ref, o_ref):
        last = pl.program_id(1) == nt - 1
        row = lax.broadcasted_iota(jnp.int32, (tm, FS), 0)
        xc = dc_ref[...]
        n0 = jnp.where(last, 0.0, nxt_ref[0:1, :])
        n1 = jnp.where(last, 0.0, nxt_ref[1:2, :])
        x1 = jnp.where(row == tm - 1, n0, pltpu.roll(xc, tm - 1, 0))
        x2 = jnp.where(row == tm - 1, n1, jnp.where(row == tm - 2, n0, pltpu.roll(xc, tm - 2, 0)))
        o_ref[...] = (xc * w_ref[2:3, :] + x1 * w_ref[1:2, :] + x2 * w_ref[0:1, :]).astype(BF16)

    blk = pl.BlockSpec((None, tm, FS), lambda j, i: (j, i, 0))
    nxt = pl.BlockSpec((None, 8, FS), lambda j, i: (j, jnp.minimum((i + 1) * t8, L // 8 - 1), 0))
    wsp = pl.BlockSpec((None, 3, FS), lambda j, i: (j, 0, 0))
    return _pcall(body, name="conv_transpose", grid=(NS8, nt), in_specs=[blk, nxt, wsp], out_specs=blk,
                  out_shape=jax.ShapeDtypeStruct((NS8, L, FS), BF16))(dc, dc, w)


def _loss_head(y, target):
    L, D = y.shape
    tm = _row_tile(L)

    def body(y_ref, t_ref, loss_ref, dy_ref):
        err = y_ref[...] - t_ref[...]
        dy_ref[...] = err / D
        part = 0.5 * jnp.sum(jnp.mean(err * err, axis=-1, keepdims=True), axis=0, keepdims=True)
        _accumulate(loss_ref, jnp.broadcast_to(part, (1, LANES)), pl.program_id(0) == 0)

    row = pl.BlockSpec((tm, D), lambda i: (i, 0))
    vec = pl.BlockSpec((1, LANES), lambda i: (0, 0))
    return _pcall(body, name="loss_head", grid=(L // tm,), in_specs=[row, row], out_specs=[vec, row],
                  out_shape=[jax.ShapeDtypeStruct((1, LANES), F32), jax.ShapeDtypeStruct((L, D), F32)])(y, target)


def _ada_fwd(c_all, ada_w, ada_b):
    NL, D, NC = ada_w.shape
    NB = c_all.shape[0]

    def body(c_ref, w_ref, b_ref, o_ref):
        cact = jax.nn.silu(c_ref[...])
        o_ref[...] = _dot(cact.astype(BF16), w_ref[...].astype(BF16)) + b_ref[...]

    return _pcall(body, name="ada_fwd", grid=(NL,),
                  in_specs=[pl.BlockSpec((NB, D), lambda l: (0, 0)), pl.BlockSpec((None, D, NC), lambda l: (l, 0, 0)),
                            pl.BlockSpec((None, 1, NC), lambda l: (l, 0, 0))],
                  out_specs=pl.BlockSpec((None, NB, NC), lambda l: (l, 0, 0)),
                  out_shape=jax.ShapeDtypeStruct((NL, NB, NC), F32))(c_all, ada_w, ada_b)


def _ada_bwd(c_all_t, dmod):
    D, NB = c_all_t.shape
    NL, _, NC = dmod.shape

    def body(c_ref, d_ref, o_ref):
        cact = jax.nn.silu(c_ref[...]).astype(BF16).astype(F32)
        o_ref[...] = _dot(cact, d_ref[...].astype(BF16).astype(F32))

    return _pcall(body, name="ada_bwd", grid=(NL,),
                  in_specs=[pl.BlockSpec((D, NB), lambda l: (0, 0)), pl.BlockSpec((None, NB, NC), lambda l: (l, 0, 0))],
                  out_specs=pl.BlockSpec((None, D, NC), lambda l: (l, 0, 0)),
                  out_shape=jax.ShapeDtypeStruct((NL, D, NC), F32))(c_all_t, dmod)


def _adamw(parts, w, m, v):
    NL, P, R, C = parts.shape
    tr = R
    for cand in (256, 128, 64, 32, 16, 8):
        if R % cand == 0 and R > cand:
            tr = cand
            break

    def body(p_ref, w_ref, m_ref, v_ref, g_ref, d_ref, nm_ref, nv_ref):
        g = p_ref[0]
        for k in range(1, P):
            g = g + p_ref[k]
        m2 = ADAM_B1 * m_ref[...] + (1.0 - ADAM_B1) * g
        v2 = ADAM_B2 * v_ref[...] + (1.0 - ADAM_B2) * jnp.square(g)
        m_hat = m2 / (1.0 - ADAM_B1 ** ADAM_STEP)
        v_hat = v2 / (1.0 - ADAM_B2 ** ADAM_STEP)
        g_ref[...] = g
        d_ref[...] = -ADAM_LR * (m_hat / (jnp.sqrt(v_hat) + ADAM_EPS) + ADAM_WD * w_ref[...])
        nm_ref[...] = m2
        nv_ref[...] = v2

    pspec = pl.BlockSpec((None, P, tr, C), lambda l, i: (l, 0, i, 0))
    wspec = pl.BlockSpec((None, tr, C), lambda l, i: (l, i, 0))
    shp = jax.ShapeDtypeStruct((NL, R, C), F32)
    return _pcall(body, name="adamw", grid=(NL, R // tr), in_specs=[pspec, wspec, wspec, wspec],
                  out_specs=[wspec] * 4, out_shape=[shp] * 4)(parts, w, m, v)


def _block_diag(blocks):
    *lead, g, r, c = blocks.shape
    eye = jnp.eye(g, dtype=bool)[:, None, :, None]
    full = jnp.where(eye, blocks[..., :, :, None, :], 0.0)
    return full.reshape(*lead, g * r, g * c)


def _block_diag_extract(m, r, c):
    g = GROUPS_PER_TILE
    m5 = m.reshape(*m.shape[:-2], g, r, g, c)
    idx = jnp.arange(g)
    out = m5[..., idx, :, idx, :]
    return jnp.moveaxis(out, 0, -3)


def kernel(x, c, ada_w, ada_b, norm1_g, w_in, q_norm_g, k_norm_g, ssm_a_re, ssm_a_im, ssm_log_dt, ssm_b_re, ssm_b_im, ssm_c_re, ssm_c_im, ssm_d, glu_w, glu_b, attn_out_g, ssm_out_g, w_out, norm2_g, ffn_w_up, ffn_conv_w, ffn_conv_b, ffn_w_down, loss_target, m_ada_w, m_ada_b, m_norm1_g, m_w_in, m_q_norm_g, m_k_norm_g, m_ssm_a_re, m_ssm_a_im, m_ssm_log_dt, m_ssm_b_re, m_ssm_b_im, m_ssm_c_re, m_ssm_c_im, m_ssm_d, m_glu_w, m_glu_b, m_attn_out_g, m_ssm_out_g, m_w_out, m_norm2_g, m_ffn_w_up, m_ffn_conv_w, m_ffn_conv_b, m_ffn_w_down, v_ada_w, v_ada_b, v_norm1_g, v_w_in, v_q_norm_g, v_k_norm_g, v_ssm_a_re, v_ssm_a_im, v_ssm_log_dt, v_ssm_b_re, v_ssm_b_im, v_ssm_c_re, v_ssm_c_im, v_ssm_d, v_glu_w, v_glu_b, v_attn_out_g, v_ssm_out_g, v_w_out, v_norm2_g, v_ffn_w_up, v_ffn_conv_w, v_ffn_conv_b, v_ffn_w_down):
    weights = dict(ada_w=ada_w, ada_b=ada_b, norm1_g=norm1_g, w_in=w_in, q_norm_g=q_norm_g, k_norm_g=k_norm_g,
                   ssm_a_re=ssm_a_re, ssm_a_im=ssm_a_im, ssm_log_dt=ssm_log_dt, ssm_b_re=ssm_b_re,
                   ssm_b_im=ssm_b_im, ssm_c_re=ssm_c_re, ssm_c_im=ssm_c_im, ssm_d=ssm_d, glu_w=glu_w, glu_b=glu_b,
                   attn_out_g=attn_out_g, ssm_out_g=ssm_out_g, w_out=w_out, norm2_g=norm2_g, ffn_w_up=ffn_w_up,
                   ffn_conv_w=ffn_conv_w, ffn_conv_b=ffn_conv_b, ffn_w_down=ffn_w_down)
    mom_m = dict(ada_w=m_ada_w, ada_b=m_ada_b, norm1_g=m_norm1_g, w_in=m_w_in, q_norm_g=m_q_norm_g,
                 k_norm_g=m_k_norm_g, ssm_a_re=m_ssm_a_re, ssm_a_im=m_ssm_a_im, ssm_log_dt=m_ssm_log_dt,
                 ssm_b_re=m_ssm_b_re, ssm_b_im=m_ssm_b_im, ssm_c_re=m_ssm_c_re, ssm_c_im=m_ssm_c_im, ssm_d=m_ssm_d,
                 glu_w=m_glu_w, glu_b=m_glu_b, attn_out_g=m_attn_out_g, ssm_out_g=m_ssm_out_g, w_out=m_w_out,
                 norm2_g=m_norm2_g, ffn_w_up=m_ffn_w_up, ffn_conv_w=m_ffn_conv_w, ffn_conv_b=m_ffn_conv_b,
                 ffn_w_down=m_ffn_w_down)
    mom_v = dict(ada_w=v_ada_w, ada_b=v_ada_b, norm1_g=v_norm1_g, w_in=v_w_in, q_norm_g=v_q_norm_g,
                 k_norm_g=v_k_norm_g, ssm_a_re=v_ssm_a_re, ssm_a_im=v_ssm_a_im, ssm_log_dt=v_ssm_log_dt,
                 ssm_b_re=v_ssm_b_re, ssm_b_im=v_ssm_b_im, ssm_c_re=v_ssm_c_re, ssm_c_im=v_ssm_c_im, ssm_d=v_ssm_d,
                 glu_w=v_glu_w, glu_b=v_glu_b, attn_out_g=v_attn_out_g, ssm_out_g=v_ssm_out_g, w_out=v_w_out,
                 norm2_g=v_norm2_g, ffn_w_up=v_ffn_w_up, ffn_conv_w=v_ffn_conv_w, ffn_conv_b=v_ffn_conv_b,
                 ffn_w_down=v_ffn_w_down)
    names = list(weights)
    big = ("ada_w", "w_in", "w_out", "ffn_w_up", "ffn_conv_w", "ffn_w_down")
    small = [n for n in names if n not in big]

    x = x[0]
    target = loss_target[0]
    L, D = x.shape
    NL = ada_w.shape[0]
    AW = D // 2
    SW = D - AW
    NH = AW // HEAD_DIM
    HP = AW // LANES
    G = SW // SSM_GROUP
    GT = SW // LANES
    NS = G * SSM_STATE
    NIN = w_in.shape[-1]
    FS = ffn_w_up.shape[-1]
    NSH = NDEV // 2
    NCA = ada_w.shape[-1]
    ROWS_OUT = w_out.shape[1]
    ROWS_DOWN = ffn_w_down.shape[1]
    B_ATT = min(L, 256)
    T_S5 = min(L, 128)
    tm = _row_tile(L)
    me = _my_index()

    cpad = jnp.reshape(c, (D // LANES, LANES))
    (c_all,) = _exchange([cpad], [0], False, "gather_c")
    c_all = c_all.reshape(NDEV, D)
    w_in_g, w_out_g, w_up_g, conv_w_g, w_down_g = _exchange(
        [w_in.astype(BF16), w_out.astype(BF16), ffn_w_up.astype(BF16), ffn_conv_w, ffn_w_down.astype(BF16)],
        [1, 1, 1, 1, 1], False, "gather_weights")
    w_out_g = w_out_g.reshape(NL, D, D)
    w_down_g = w_down_g.reshape(NL, NSH, 2 * ROWS_DOWN, D)
    conv_w_g = conv_w_g.reshape(NL, 2, NSH, 3, FS)
    conv_b_g = ffn_conv_b.reshape(NL, 2, NSH, 1, FS)

    ada_b_mine = lax.dynamic_slice_in_dim(ada_b, me * NCA, NCA, axis=1).reshape(NL, 1, NCA)
    mod_part = _ada_fwd(c_all, ada_w, ada_b_mine)
    (mod_all,) = _exchange([mod_part], [0], False, "gather_mod")
    mod = lax.dynamic_index_in_dim(mod_all, me, axis=2, keepdims=False)
    mod = jnp.transpose(mod, (1, 0, 2)).reshape(NL, N_MOD, 1, D)

    row = lambda a: a.reshape(NL, 1, NS)
    ar_row, ai_row = row(ssm_a_re), row(ssm_a_im)
    ldt_row = row(jnp.broadcast_to(ssm_log_dt[:, :, None], (NL, G, SSM_STATE)))
    tiles = lambda a: a.reshape((NL, GT, GROUPS_PER_TILE) + a.shape[2:])
    braw_r = _block_diag(jnp.swapaxes(tiles(ssm_b_re), -1, -2))
    braw_i = _block_diag(jnp.swapaxes(tiles(ssm_b_im), -1, -2))
    crT = _block_diag(jnp.swapaxes(tiles(ssm_c_re), -1, -2)).astype(BF16)
    ciT = _block_diag(jnp.swapaxes(tiles(ssm_c_im), -1, -2)).astype(BF16)
    wg = _block_diag(tiles(glu_w)).astype(BF16)
    abr, abi, levr, levi, powr, powi, bsr, bsi = _s5_prep(ar_row, ai_row, ldt_row, braw_r, braw_i, T_S5)
    s5p = dict(bsr=bsr, bsi=bsi, levr=levr, levi=levi, powr=powr, powi=powi,
               rpowr=jnp.flip(powr, axis=1), rpowi=jnp.flip(powi, axis=1), crT=crT, ciT=ciT,
               d=ssm_d.reshape(NL, 1, SW), wg=wg, gb=glu_b.reshape(NL, 1, SW))

    gqk = jnp.concatenate([jnp.tile(q_norm_g, (1, NH)), jnp.tile(k_norm_g, (1, NH))], axis=1).reshape(NL, 1, 2 * AW)
    layer_params = dict(
        mod=mod, norm1_g=norm1_g.reshape(NL, 1, D), norm2_g=norm2_g.reshape(NL, 1, D), gqk=gqk,
        ga=attn_out_g.reshape(NL, 1, AW), gs=ssm_out_g.reshape(NL, 1, SW),
        w_in=w_in_g, w_out=w_out_g, w_up=w_up_g, conv_w=conv_w_g, conv_b=conv_b_g, w_down=w_down_g, s5=s5p)

    ucol0 = 3 * AW // LANES

    def mm_rows_shards(name, a, b, n, out_dtype=F32):
        K = a.shape[1]
        return _matmul(name, a, b, dims=NN, grid=(L // tm, NDEV),
                       a_spec=pl.BlockSpec((tm, K), lambda i, j: (i, 0)),
                       b_spec=pl.BlockSpec((None, K, n), lambda i, j: (j, 0, 0)),
                       out_shape=jax.ShapeDtypeStruct((L, NDEV * n), out_dtype),
                       out_specs=pl.BlockSpec((tm, n), lambda i, j: (i, j)))

    tn = min(D, 512)

    def resid_epilogue(acc, xres, gate):
        return acc, xres + gate * acc

    def layer_fwd(xin, lp):
        sh1, sc1, g1, sh2, sc2, g2 = (lp["mod"][k] for k in range(N_MOD))
        h = _lnmod_fwd(xin, lp["norm1_g"], sh1, sc1)
        p = mm_rows_shards("mm_in", h, lp["w_in"], NIN)
        qk = _qknorm_fwd(p, lp["gqk"], AW)
        o_attn, tot = _attn_fwd(qk, p, AW, B_ATT)
        o_ssm, s_re, s_im, ypre = _s5_fwd(p, lp["s5"], SW, ucol0, T_S5)
        o = _outnorm_fwd(o_attn, o_ssm, lp["ga"], lp["gs"])
        a1, x_mid = _matmul(
            "mm_out", o, lp["w_out"], dims=NN, grid=(L // tm, D // tn),
            a_spec=pl.BlockSpec((tm, D), lambda i, j: (i, 0)), b_spec=pl.BlockSpec((D, tn), lambda i, j: (0, j)),
            extra=(xin, g1), extra_specs=(pl.BlockSpec((tm, tn), lambda i, j: (i, j)),
                                          pl.BlockSpec((1, tn), lambda i, j: (0, j))),
            epilogue=resid_epilogue,
            out_shape=[jax.ShapeDtypeStruct((L, D), F32)] * 2,
            out_specs=[pl.BlockSpec((tm, tn), lambda i, j: (i, j))] * 2)
        h2 = _lnmod_fwd(x_mid, lp["norm2_g"], sh2, sc2)
        up = _matmul(
            "mm_up", h2, lp["w_up"], dims=NN, grid=(L // tm, NDEV),
            a_spec=pl.BlockSpec((tm, D), lambda i, j: (i, 0)), b_spec=pl.BlockSpec((None, D, FS), lambda i, j: (j, 0, 0)),
            out_shape=jax.ShapeDtypeStruct((NDEV, L, FS), F32),
            out_specs=pl.BlockSpec((None, tm, FS), lambda i, j: (j, i, 0)))
        up = up.reshape(2, NSH, L, FS)
        act = _convglu_fwd(up, lp["conv_w"], lp["conv_b"])
        a2, x_out = _matmul(
            "mm_down", act, lp["w_down"], dims=NN, grid=(L // tm, D // tn, NSH), kaxis=2, acc_shape=(tm, tn),
            a_spec=pl.BlockSpec((None, tm, FS), lambda i, j, k: (k, i, 0)),
            b_spec=pl.BlockSpec((None, FS, tn), lambda i, j, k: (k, 0, j)),
            extra=(x_mid, g2), extra_specs=(pl.BlockSpec((tm, tn), lambda i, j, k: (i, j)),
                                            pl.BlockSpec((1, tn), lambda i, j, k: (0, j))),
            epilogue=resid_epilogue,
            out_shape=[jax.ShapeDtypeStruct((L, D), F32)] * 2,
            out_specs=[pl.BlockSpec((tm, tn), lambda i, j, k: (i, j))] * 2)
        res = dict(x=xin, h=h, p=p, qk=qk, tot=tot, o_attn=o_attn, o_ssm=o_ssm, s_re=s_re, s_im=s_im, ypre=ypre,
                   o=o, a1=a1, x_mid=x_mid, h2=h2, up=up, act=act, a2=a2)
        return x_out, res

    y, residuals = lax.scan(layer_fwd, x, layer_params)

    loss_row, dy = _loss_head(y, target)
    loss = lax.psum(loss_row[0, 0], ("x", "y", "c"))

    def layer_bwd(dx, args):
        lp, r = args
        sh1, sc1, g1, sh2, sc2, g2 = (lp["mod"][k] for k in range(N_MOD))
        da2, dg2 = _gate_bwd(dx, r["a2"], g2)
        dact = _matmul(
            "mm_dact", da2, lp["w_down"], dims=NT, grid=(L // tm, NSH),
            a_spec=pl.BlockSpec((tm, D), lambda i, j: (i, 0)), b_spec=pl.BlockSpec((None, FS, D), lambda i, j: (j, 0, 0)),
            out_shape=jax.ShapeDtypeStruct((NSH, L, FS), F32),
            out_specs=pl.BlockSpec((None, tm, FS), lambda i, j: (j, i, 0)))
        dw_down = _matmul(
            "mm_dw_down", r["act"], da2, dims=TN, grid=(NSH, D // tn, L // tm), kaxis=2, acc_shape=(FS, tn),
            a_spec=pl.BlockSpec((None, tm, FS), lambda j, n, k: (j, k, 0)),
            b_spec=pl.BlockSpec((tm, tn), lambda j, n, k: (k, n)),
            out_shape=jax.ShapeDtypeStruct((NSH, FS, D), F32),
            out_specs=pl.BlockSpec((None, FS, tn), lambda j, n, k: (j, 0, n)))
        dconv, dcw, dcb = _convglu_bwd(r["up"], dact, lp["conv_w"], lp["conv_b"])
        dup = _conv_transpose(dconv.reshape(NDEV, L, FS), lp["conv_w"].reshape(NDEV, 3, FS))
        dh2 = _matmul(
            "mm_dh2", dup, lp["w_up"], dims=NT, grid=(L // tm, NDEV), kaxis=1, acc_shape=(tm, D),
            a_spec=pl.BlockSpec((None, tm, FS), lambda i, k: (k, i, 0)),
            b_spec=pl.BlockSpec((None, D, FS), lambda i, k: (k, 0, 0)),
            out_shape=jax.ShapeDtypeStruct((L, D), F32), out_specs=pl.BlockSpec((tm, D), lambda i, k: (i, 0)))
        dw_up = _matmul(
            "mm_dw_up", r["h2"], dup, dims=TN, grid=(NDEV, L // tm), kaxis=1, acc_shape=(D, FS),
            a_spec=pl.BlockSpec((tm, D), lambda j, k: (k, 0)), b_spec=pl.BlockSpec((None, tm, FS), lambda j, k: (j, k, 0)),
            out_shape=jax.ShapeDtypeStruct((NDEV, D, FS), F32),
            out_specs=pl.BlockSpec((None, D, FS), lambda j, k: (j, 0, 0)))
        dxm, dn2, dsh2, dsc2 = _lnmod_bwd(dh2, r["x_mid"], lp["norm2_g"], sh2, sc2, dx)
        da1, dg1 = _gate_bwd(dxm, r["a1"], g1)
        do = _matmul(
            "mm_do", da1, lp["w_out"], dims=NT, grid=(L // tm, D // tn),
            a_spec=pl.BlockSpec((tm, D), lambda i, j: (i, 0)), b_spec=pl.BlockSpec((tn, D), lambda i, j: (j, 0)),
            out_shape=jax.ShapeDtypeStruct((L, D), F32), out_specs=pl.BlockSpec((tm, tn), lambda i, j: (i, j)))
        dw_out = _matmul(
            "mm_dw_out", r["o"], da1, dims=TN, grid=(D // tn, D // tn, L // tm), kaxis=2, acc_shape=(tn, tn),
            a_spec=pl.BlockSpec((tm, tn), lambda m, n, k: (k, m)), b_spec=pl.BlockSpec((tm, tn), lambda m, n, k: (k, n)),
            out_shape=jax.ShapeDtypeStruct((D, D), F32), out_specs=pl.BlockSpec((tn, tn), lambda m, n, k: (m, n)))
        doa, dos, dga, dgs = _outnorm_bwd(do, r["o_attn"], r["o_ssm"], lp["ga"], lp["gs"])
        (du, dbsr, dbsi, dcr, dci, dd, dwg, dgb, dabr, dabi) = _s5_bwd(
            dos, r["p"], r["s_re"], r["s_im"], r["ypre"], lp["s5"], SW, ucol0, T_S5)
        dq, dk, dv = _attn_bwd(r["qk"], r["p"], doa, r["tot"], AW, B_ATT)
        dqk, dgqk = _qknorm_bwd(jnp.concatenate([dq, dk], axis=1), r["p"], lp["gqk"], AW)
        dp = jnp.concatenate([dqk, dv, du], axis=1).astype(BF16)
        dh = _matmul(
            "mm_dh", dp, lp["w_in"], dims=NT, grid=(L // tm, NDEV), kaxis=1, acc_shape=(tm, D),
            a_spec=pl.BlockSpec((tm, NIN), lambda i, k: (i, k)), b_spec=pl.BlockSpec((None, D, NIN), lambda i, k: (k, 0, 0)),
            out_shape=jax.ShapeDtypeStruct((L, D), F32), out_specs=pl.BlockSpec((tm, D), lambda i, k: (i, 0)))
        dw_in = _matmul(
            "mm_dw_in", r["h"], dp, dims=TN, grid=(NDEV, L // tm), kaxis=1, acc_shape=(D, NIN),
            a_spec=pl.BlockSpec((tm, D), lambda j, k: (k, 0)), b_spec=pl.BlockSpec((tm, NIN), lambda j, k: (k, j)),
            out_shape=jax.ShapeDtypeStruct((NDEV, D, NIN), F32),
            out_specs=pl.BlockSpec((None, D, NIN), lambda j, k: (j, 0, 0)))
        dx0, dn1, dsh1, dsc1 = _lnmod_bwd(dh, r["x"], lp["norm1_g"], sh1, sc1, dxm)
        grads = dict(
            dmod=jnp.concatenate([dsh1, dsc1, dg1, dsh2, dsc2, dg2], axis=1), dn1=dn1, dn2=dn2, dgqk=dgqk,
            dga=dga, dgs=dgs, dbsr=dbsr, dbsi=dbsi, dcr=dcr, dci=dci, dd=dd, dwg=dwg, dgb=dgb, dabr=dabr, dabi=dabi,
            dcb=dcb, dw_in=dw_in, dw_out=dw_out.reshape(NDEV, ROWS_OUT, D), dw_up=dw_up,
            dcw=dcw.reshape(NDEV, 3, FS), dw_down=dw_down.reshape(NDEV, ROWS_DOWN, D))
        return dx0, grads

    grad_x, gr = lax.scan(layer_bwd, dy, (layer_params, residuals), reverse=True)

    dar, dai, dldt, dbr_bd, dbi_bd = _s5_prep_bwd(ar_row, ai_row, ldt_row, braw_r, braw_i,
                                                  gr["dabr"], gr["dabi"], gr["dbsr"], gr["dbsi"])
    unt = lambda a: a.reshape((NL, G) + a.shape[3:])
    local = dict(
        ada_b=gr["dmod"].reshape(NL, N_MOD * D),
        norm1_g=gr["dn1"].reshape(NL, D), norm2_g=gr["dn2"].reshape(NL, D),
        q_norm_g=gr["dgqk"].reshape(NL, 2, NH, HEAD_DIM)[:, 0].sum(axis=1),
        k_norm_g=gr["dgqk"].reshape(NL, 2, NH, HEAD_DIM)[:, 1].sum(axis=1),
        ssm_a_re=dar.reshape(NL, G, SSM_STATE), ssm_a_im=dai.reshape(NL, G, SSM_STATE),
        ssm_log_dt=dldt.reshape(NL, G, SSM_STATE).sum(axis=-1),
        ssm_b_re=jnp.swapaxes(unt(_block_diag_extract(dbr_bd, SSM_GROUP, SSM_STATE)), -1, -2),
        ssm_b_im=jnp.swapaxes(unt(_block_diag_extract(dbi_bd, SSM_GROUP, SSM_STATE)), -1, -2),
        ssm_c_re=jnp.swapaxes(unt(_block_diag_extract(gr["dcr"], SSM_STATE, SSM_GROUP)), -1, -2),
        ssm_c_im=jnp.swapaxes(unt(_block_diag_extract(gr["dci"], SSM_STATE, SSM_GROUP)), -1, -2),
        ssm_d=gr["dd"].reshape(NL, G, SSM_GROUP),
        glu_w=unt(_block_diag_extract(gr["dwg"], SSM_GROUP, SSM_GROUP)),
        glu_b=gr["dgb"].reshape(NL, G, SSM_GROUP),
        attn_out_g=gr["dga"].reshape(NL, AW), ssm_out_g=gr["dgs"].reshape(NL, SW),
        ffn_conv_b=gr["dcb"].reshape(NL, 2 * NSH * FS),
    )

    def pack(tree):
        flat = jnp.concatenate([tree[n].reshape(-1) for n in small])
        pad = (-flat.shape[0]) % (8 * LANES)
        return jnp.pad(flat, (0, pad)).reshape(-1, LANES)

    (small_parts,) = _exchange([pack(local)], [0], False, "gather_small_grads")
    sg, sd, sm, sv = _adamw(small_parts[None], pack(weights)[None], pack(mom_m)[None], pack(mom_v)[None])

    def unpack(buf):
        flat = buf.reshape(-1)
        out, off = {}, 0
        for n in small:
            size = weights[n].size
            out[n] = flat[off:off + size].reshape(weights[n].shape)
            off += size
        return out

    ug, ud, um, uv = unpack(sg), unpack(sd), unpack(sm), unpack(sv)
    results = {n: (ug[n], ud[n], um[n], uv[n]) for n in small}

    (dmod_all,) = _exchange([gr["dmod"].reshape(NL, N_MOD * D)], [0], False, "gather_dmod")
    dmod_mine = lax.dynamic_slice_in_dim(dmod_all, me * NCA, NCA, axis=2)
    d_ada_w = _ada_bwd(jnp.transpose(c_all), jnp.transpose(dmod_mine, (1, 0, 2)))
    results["ada_w"] = tuple(_adamw(d_ada_w[:, None], ada_w, m_ada_w, v_ada_w))

    parts = _exchange([gr["dw_in"], gr["dw_out"], gr["dw_up"], gr["dcw"], gr["dw_down"]], [1] * 5, True,
                      "scatter_weight_grads")
    for n, pt in zip(("w_in", "w_out", "ffn_w_up", "ffn_conv_w", "ffn_w_down"), parts):
        results[n] = tuple(_adamw(pt, weights[n], mom_m[n], mom_v[n]))

    out = [loss, grad_x[None]]
    for k in range(4):
        out.extend(results[n][k] for n in names)
    return tuple(out)
```

```python
import functools
import math

import jax
import jax.numpy as jnp
from jax import lax
from jax.experimental import pallas as pl
from jax.experimental.pallas import tpu as pltpu

F32 = jnp.float32
BF16 = jnp.bfloat16
NDEV = 8
LANES = 128
HEAD_DIM = 64
SSM_GROUP = 16
SSM_STATE = 64
GROUPS_PER_TILE = LANES // SSM_GROUP
STATE_TILE = GROUPS_PER_TILE * SSM_STATE
N_MOD = 6
ATTN_STRIP = 32
EPS = 1e-6
ADAM_LR, ADAM_B1, ADAM_B2, ADAM_EPS, ADAM_WD, ADAM_STEP = 0.001, 0.9, 0.999, 1e-08, 0.01, 10
VMEM_LIMIT = 48 * 1024 * 1024
MESH_IDS = pl.DeviceIdType.MESH

NN = (((1,), (0,)), ((), ()))
NT = (((1,), (1,)), ((), ()))
TN = (((0,), (0,)), ((), ()))


def _dot(a, b, dims=NN):
    return lax.dot_general(a, b, dims, preferred_element_type=F32)


def _pcall(body, *, name, out_shape, in_specs, out_specs, grid=(), scratch=()):
    return pl.pallas_call(
        body, name=name, grid=grid, in_specs=in_specs, out_specs=out_specs, out_shape=out_shape,
        scratch_shapes=list(scratch),
        compiler_params=pltpu.CompilerParams(vmem_limit_bytes=VMEM_LIMIT))


def _row_tile(n, want=512):
    t = min(n, want)
    assert n % t == 0
    return t


def _my_index():
    return 4 * lax.axis_index("x") + 2 * lax.axis_index("y") + lax.axis_index("c")


HBM_SPEC = pl.BlockSpec(memory_space=pltpu.HBM)


def _mesh_place():
    x, y, c = lax.axis_index("x"), lax.axis_index("y"), lax.axis_index("c")
    chips = [(1 - x, y), (x, 1 - y), (1 - x, 1 - y)]
    return x, y, c, chips


def _gather_small(arr, name):
    def body(in_ref, out_ref, send_sems, recv_sems, local_sem):
        x, y, c, _ = _mesh_place()
        me = 4 * x + 2 * y + c
        own = pltpu.make_async_copy(in_ref, out_ref.at[me], local_sem)
        own.start()
        sends, recvs = [], []
        for k in range(1, NDEV):
            px = 1 - x if k & 4 else x
            py = 1 - y if k & 2 else y
            pc = 1 - c if k & 1 else c
            common = dict(send_sem=send_sems.at[k - 1], recv_sem=recv_sems.at[k - 1],
                          device_id=(px, py, pc), device_id_type=MESH_IDS)
            snd = pltpu.make_async_remote_copy(src_ref=in_ref, dst_ref=out_ref.at[me], **common)
            snd.start()
            sends.append(snd)
            recvs.append(pltpu.make_async_remote_copy(
                src_ref=in_ref, dst_ref=out_ref.at[4 * px + 2 * py + pc], **common))
        for r in recvs:
            r.wait_recv()
        for s in sends:
            s.wait_send()
        own.wait()

    return pl.pallas_call(
        body, name=name, out_shape=jax.ShapeDtypeStruct((NDEV,) + arr.shape, arr.dtype),
        in_specs=[HBM_SPEC], out_specs=HBM_SPEC,
        scratch_shapes=[pltpu.SemaphoreType.DMA((NDEV - 1,)), pltpu.SemaphoreType.DMA((NDEV - 1,)),
                        pltpu.SemaphoreType.DMA(())],
    )(arr)


def _gather_shards(arrs, name):
    n = len(arrs)

    def body(*refs):
        ins, outs = refs[:n], refs[n:2 * n]
        send_sems, recv_sems, local_sems = refs[2 * n:]
        x, y, c, chips = _mesh_place()
        dev = lambda px, py, pc: 4 * px + 2 * py + pc

        def copy(k, a, src, block, to):
            return pltpu.make_async_remote_copy(
                src_ref=src, dst_ref=outs[a].at[:, block], send_sem=send_sems.at[k, a],
                recv_sem=recv_sems.at[k, a], device_id=to, device_id_type=MESH_IDS)

        local = [pltpu.make_async_copy(ins[a], outs[a].at[:, dev(x, y, c)], local_sems.at[a]) for a in range(n)]
        for cp in local:
            cp.start()
        sends = []
        for a in range(n):
            sends.append(copy(0, a, ins[a], dev(x, y, c), (x, y, 1 - c)))
        for j, (px, py) in enumerate(chips):
            for a in range(n):
                sends.append(copy(1 + j, a, ins[a], dev(x, y, c), (px, py, c)))
        for s in sends:
            s.start()
        for j, (px, py) in enumerate(chips):
            for a in range(n):
                got = outs[a].at[:, dev(px, py, c)]
                copy(1 + j, a, got, dev(px, py, c), (x, y, c)).wait_recv()
                fwd = copy(4 + j, a, got, dev(px, py, c), (x, y, 1 - c))
                fwd.start()
                sends.append(fwd)
        for a in range(n):
            copy(0, a, ins[a], dev(x, y, 1 - c), (x, y, c)).wait_recv()
        for j, (px, py) in enumerate(chips):
            for a in range(n):
                copy(4 + j, a, ins[a], dev(px, py, 1 - c), (x, y, c)).wait_recv()
        for s in sends:
            s.wait_send()
        for cp in local:
            cp.wait()

    out_shape = [jax.ShapeDtypeStruct((a.shape[0], NDEV) + a.shape[1:], a.dtype) for a in arrs]
    return list(pl.pallas_call(
        body, name=name, out_shape=out_shape, in_specs=[HBM_SPEC] * n, out_specs=[HBM_SPEC] * n,
        scratch_shapes=[pltpu.SemaphoreType.DMA((7, n)), pltpu.SemaphoreType.DMA((7, n)),
                        pltpu.SemaphoreType.DMA((n,))],
    )(*arrs))


def _sibling_exchange(arrs, name):
    n = len(arrs)

    def body(*refs):
        ins, own, got = refs[:n], refs[n:2 * n], refs[2 * n:3 * n]
        send_sems, recv_sems, local_sems = refs[3 * n:]
        x, y, c, _ = _mesh_place()
        local, sends = [], []
        for a in range(n):
            cp = pltpu.make_async_copy(ins[a].at[:, :, c], own[a], local_sems.at[a])
            cp.start()
            local.append(cp)
            snd = pltpu.make_async_remote_copy(
                src_ref=ins[a].at[:, :, 1 - c], dst_ref=got[a], send_sem=send_sems.at[a],
                recv_sem=recv_sems.at[a], device_id=(x, y, 1 - c), device_id_type=MESH_IDS)
            snd.start()
            sends.append(snd)
        for s in sends:
            s.wait_recv()
        for s in sends:
            s.wait_send()
        for cp in local:
            cp.wait()

    shapes = [jax.ShapeDtypeStruct(a.shape[:2] + a.shape[3:], a.dtype) for a in arrs]
    outs = pl.pallas_call(
        body, name=name, out_shape=shapes + shapes, in_specs=[HBM_SPEC] * n, out_specs=[HBM_SPEC] * (2 * n),
        scratch_shapes=[pltpu.SemaphoreType.DMA((n,)), pltpu.SemaphoreType.DMA((n,)),
                        pltpu.SemaphoreType.DMA((n,))],
    )(*arrs)
    return list(outs[:n]), list(outs[n:])


def _chip_exchange(arrs, name):
    n = len(arrs)

    def body(*refs):
        ins, outs = refs[:n], refs[n:2 * n]
        send_sems, recv_sems, local_sems = refs[2 * n:]
        x, y, c, chips = _mesh_place()
        mine = 2 * x + y
        local, sends, recvs = [], [], []
        for a in range(n):
            cp = pltpu.make_async_copy(ins[a].at[:, mine], outs[a].at[:, mine], local_sems.at[a])
            cp.start()
            local.append(cp)
        for j, (px, py) in enumerate(chips):
            for a in range(n):
                common = dict(send_sem=send_sems.at[j, a], recv_sem=recv_sems.at[j, a],
                              device_id=(px, py, c), device_id_type=MESH_IDS)
                src = ins[a].at[:, 2 * px + py]
                snd = pltpu.make_async_remote_copy(src_ref=src, dst_ref=outs[a].at[:, mine], **common)
                snd.start()
                sends.append(snd)
                recvs.append(pltpu.make_async_remote_copy(src_ref=src, dst_ref=outs[a].at[:, 2 * px + py], **common))
        for r in recvs:
            r.wait_recv()
        for s in sends:
            s.wait_send()
        for cp in local:
            cp.wait()

    out_shape = [jax.ShapeDtypeStruct(a.shape, a.dtype) for a in arrs]
    return list(pl.pallas_call(
        body, name=name, out_shape=out_shape, in_specs=[HBM_SPEC] * n, out_specs=[HBM_SPEC] * n,
        scratch_shapes=[pltpu.SemaphoreType.DMA((3, n)), pltpu.SemaphoreType.DMA((3, n)),
                        pltpu.SemaphoreType.DMA((n,))],
    )(*arrs))


def _pair_add(a, b):
    NL, NC, R, C = a.shape
    tr = R
    for cand in (256, 128, 64, 32, 16):
        if R % cand == 0 and R > cand:
            tr = cand
            break

    def body(a_ref, b_ref, o_ref):
        o_ref[...] = (a_ref[...].astype(F32) + b_ref[...].astype(F32)).astype(o_ref.dtype)

    spec = pl.BlockSpec((None, NC, tr, C), lambda l, i: (l, 0, i, 0))
    return _pcall(body, name="pair_add", grid=(NL, R // tr), in_specs=[spec, spec], out_specs=spec,
                  out_shape=jax.ShapeDtypeStruct(a.shape, a.dtype))(a, b)


def _matmul(name, a, b, *, dims, grid, a_spec, b_spec, out_shape, out_specs, kaxis=None, acc_shape=None,
            extra=(), extra_specs=(), epilogue=None):
    nk = grid[kaxis] if kaxis is not None else 1
    ne = len(extra)
    multi = isinstance(out_shape, (list, tuple))
    n_out = len(out_shape) if multi else 1

    def body(*refs):
        a_ref, b_ref = refs[0], refs[1]
        ex = refs[2:2 + ne]
        outs = refs[2 + ne:2 + ne + n_out]

        def write(res):
            vals = epilogue(res, *[e[...] for e in ex]) if epilogue is not None else (res,)
            for o, v in zip(outs, vals):
                o[...] = v.astype(o.dtype)

        part = _dot(a_ref[...].astype(BF16), b_ref[...].astype(BF16), dims)
        if nk == 1:
            write(part)
        else:
            acc = refs[-1]
            k = pl.program_id(kaxis)

            @pl.when(k == 0)
            def _():
                acc[...] = part

            @pl.when(k > 0)
            def _():
                acc[...] += part

            @pl.when(k == nk - 1)
            def _():
                write(acc[...])

    scratch = [pltpu.VMEM(acc_shape, F32)] if nk > 1 else []
    return _pcall(body, name=name, grid=grid, in_specs=[a_spec, b_spec, *extra_specs],
                  out_specs=out_specs, out_shape=out_shape, scratch=scratch)(a, b, *extra)


def _rms(x, g):
    inv = lax.rsqrt(jnp.mean(x * x, axis=-1, keepdims=True) + EPS)
    return x * inv * g


def _lnmod_math(x, g, sh, sc):
    return _rms(x, g) * (1.0 + sc) + sh


def _qkn_math(p, g):
    m0 = lax.broadcasted_iota(jnp.int32, (1, LANES), 1) < HEAD_DIM
    sq = p * p
    s0 = jnp.sum(jnp.where(m0, sq, 0.0), axis=-1, keepdims=True)
    s1 = jnp.sum(jnp.where(m0, 0.0, sq), axis=-1, keepdims=True)
    inv = jnp.where(m0, lax.rsqrt(s0 / HEAD_DIM + EPS), lax.rsqrt(s1 / HEAD_DIM + EPS))
    return p * inv * g


def _glu_math(val, gate):
    return jax.nn.gelu(gate) * val


def _accumulate(ref, val, first):
    @pl.when(first)
    def _():
        ref[...] = val

    @pl.when(jnp.logical_not(first))
    def _():
        ref[...] += val


def _lnmod_fwd(x, g, sh, sc):
    L, D = x.shape
    tm = _row_tile(L)

    def body(x_ref, g_ref, sh_ref, sc_ref, h_ref):
        h_ref[...] = _lnmod_math(x_ref[...], g_ref[...], sh_ref[...], sc_ref[...]).astype(BF16)

    row = pl.BlockSpec((tm, D), lambda i: (i, 0))
    vec = pl.BlockSpec((1, D), lambda i: (0, 0))
    return _pcall(body, name="lnmod_fwd", grid=(L // tm,), in_specs=[row, vec, vec, vec], out_specs=row,
                  out_shape=jax.ShapeDtypeStruct((L, D), BF16))(x, g, sh, sc)


def _lnmod_bwd(dh, x, g, sh, sc, dres):
    L, D = x.shape
    tm = _row_tile(L)

    def body(dh_ref, x_ref, g_ref, sh_ref, sc_ref, res_ref, dx_ref, dg_ref, dsh_ref, dsc_ref):
        _, vjp = jax.vjp(_lnmod_math, x_ref[...], g_ref[...], sh_ref[...], sc_ref[...])
        dx, dg, dsh, dsc = vjp(dh_ref[...])
        dx_ref[...] = dx + res_ref[...]
        first = pl.program_id(0) == 0
        _accumulate(dg_ref, dg, first)
        _accumulate(dsh_ref, dsh, first)
        _accumulate(dsc_ref, dsc, first)

    row = pl.BlockSpec((tm, D), lambda i: (i, 0))
    vec = pl.BlockSpec((1, D), lambda i: (0, 0))
    vs = jax.ShapeDtypeStruct((1, D), F32)
    return _pcall(body, name="lnmod_bwd", grid=(L // tm,), in_specs=[row, row, vec, vec, vec, row],
                  out_specs=[row, vec, vec, vec],
                  out_shape=[jax.ShapeDtypeStruct((L, D), F32), vs, vs, vs])(dh, x, g, sh, sc, dres)


def _gate_bwd(dx, a, gate):
    L, D = dx.shape
    tm = _row_tile(L)

    def body(dx_ref, a_ref, g_ref, da_ref, dg_ref):
        dxv = dx_ref[...]
        da_ref[...] = (g_ref[...] * dxv).astype(BF16)
        _accumulate(dg_ref, jnp.sum(dxv * a_ref[...], axis=0, keepdims=True), pl.program_id(0) == 0)

    row = pl.BlockSpec((tm, D), lambda i: (i, 0))
    vec = pl.BlockSpec((1, D), lambda i: (0, 0))
    return _pcall(body, name="gate_bwd", grid=(L // tm,), in_specs=[row, row, vec], out_specs=[row, vec],
                  out_shape=[jax.ShapeDtypeStruct((L, D), BF16), jax.ShapeDtypeStruct((1, D), F32)])(dx, a, gate)


def _qknorm_fwd(p, gqk, AW):
    L = p.shape[0]
    tm = _row_tile(L)
    ncol = 2 * AW // LANES

    def body(p_ref, g_ref, o_ref):
        o_ref[...] = _qkn_math(p_ref[...], g_ref[...]).astype(BF16)

    blk = pl.BlockSpec((tm, LANES), lambda i, j: (i, j))
    vec = pl.BlockSpec((1, LANES), lambda i, j: (0, j))
    return _pcall(body, name="qknorm_fwd", grid=(L // tm, ncol), in_specs=[blk, vec], out_specs=blk,
                  out_shape=jax.ShapeDtypeStruct((L, 2 * AW), BF16))(p, gqk)


def _qknorm_bwd(dqk, p, gqk, AW):
    L = p.shape[0]
    tm = _row_tile(L)
    ncol = 2 * AW // LANES

    def body(d_ref, p_ref, g_ref, dp_ref, dg_ref):
        _, vjp = jax.vjp(_qkn_math, p_ref[...], g_ref[...])
        dp, dg = vjp(d_ref[...])
        dp_ref[...] = dp
        _accumulate(dg_ref, dg, pl.program_id(1) == 0)

    blk = pl.BlockSpec((tm, LANES), lambda j, i: (i, j))
    vec = pl.BlockSpec((1, LANES), lambda j, i: (0, j))
    return _pcall(body, name="qknorm_bwd", grid=(ncol, L // tm), in_specs=[blk, blk, vec], out_specs=[blk, vec],
                  out_shape=[jax.ShapeDtypeStruct((L, 2 * AW), F32),
                             jax.ShapeDtypeStruct((1, 2 * AW), F32)])(dqk, p, gqk)


def _outnorm_fwd(oa, os_, ga, gs):
    L, AW = oa.shape
    SW = os_.shape[1]
    tm = _row_tile(L)

    def body(oa_ref, os_ref, ga_ref, gs_ref, o_ref):
        o_ref[:, :AW] = _rms(oa_ref[...], ga_ref[...]).astype(BF16)
        o_ref[:, AW:] = _rms(os_ref[...], gs_ref[...]).astype(BF16)

    ra = pl.BlockSpec((tm, AW), lambda i: (i, 0))
    rs = pl.BlockSpec((tm, SW), lambda i: (i, 0))
    va = pl.BlockSpec((1, AW), lambda i: (0, 0))
    vs = pl.BlockSpec((1, SW), lambda i: (0, 0))
    ro = pl.BlockSpec((tm, AW + SW), lambda i: (i, 0))
    return _pcall(body, name="outnorm_fwd", grid=(L // tm,), in_specs=[ra, rs, va, vs], out_specs=ro,
                  out_shape=jax.ShapeDtypeStruct((L, AW + SW), BF16))(oa, os_, ga, gs)


def _outnorm_bwd(do, oa, os_, ga, gs):
    L, AW = oa.shape
    SW = os_.shape[1]
    tm = _row_tile(L)

    def body(do_ref, oa_ref, os_ref, ga_ref, gs_ref, doa_ref, dos_ref, dga_ref, dgs_ref):
        first = pl.program_id(0) == 0
        _, vjp_a = jax.vjp(_rms, oa_ref[...], ga_ref[...])
        doa, dga = vjp_a(do_ref[:, :AW])
        _, vjp_s = jax.vjp(_rms, os_ref[...], gs_ref[...])
        dos, dgs = vjp_s(do_ref[:, AW:])
        doa_ref[...] = doa
        dos_ref[...] = dos
        _accumulate(dga_ref, dga, first)
        _accumulate(dgs_ref, dgs, first)

    ra = pl.BlockSpec((tm, AW), lambda i: (i, 0))
    rs = pl.BlockSpec((tm, SW), lambda i: (i, 0))
    va = pl.BlockSpec((1, AW), lambda i: (0, 0))
    vs = pl.BlockSpec((1, SW), lambda i: (0, 0))
    ro = pl.BlockSpec((tm, AW + SW), lambda i: (i, 0))
    return _pcall(body, name="outnorm_bwd", grid=(L // tm,), in_specs=[ro, ra, rs, va, vs],
                  out_specs=[ra, rs, va, vs],
                  out_shape=[jax.ShapeDtypeStruct((L, AW), F32), jax.ShapeDtypeStruct((L, SW), F32),
                             jax.ShapeDtypeStruct((1, AW), F32), jax.ShapeDtypeStruct((1, SW), F32)])(
                                 do, oa, os_, ga, gs)


def _softplus_neg_abs(z):
    return jnp.log(1.0 + jnp.exp(-jnp.abs(z)))


def _split_dot(x, m):
    hi = x.astype(BF16)
    lo = (x - hi.astype(F32)).astype(BF16)
    return _dot(hi, m) + _dot(lo, m)


def _attn_masks(B):
    row = lax.broadcasted_iota(jnp.int32, (B, B), 0)
    col = lax.broadcasted_iota(jnp.int32, (B, B), 1)
    strict = col < row
    upper = jnp.where(row > col, 1.0, 0.0).astype(BF16)
    lower = jnp.where(row < col, 1.0, 0.0).astype(BF16)
    return strict, upper, lower


def _attn_fwd(qk, p, AW, B):
    L = qk.shape[0]
    HP = AW // LANES
    nb = L // B

    def body(q_ref, k_ref, v_ref, o_ref, tot_ref, *scr):
        lb_scr = (scr[0:2], scr[2:4])
        tail_scr = (scr[4:6], scr[6:8])
        sum_scr = scr[8:10]
        i = pl.program_id(1)
        m0 = lax.broadcasted_iota(jnp.int32, (1, LANES), 1) < HEAD_DIM
        strict, upper, _ = _attn_masks(B)
        q = q_ref[...] * 0.125
        zq = jnp.zeros_like(q)
        qh = (jnp.where(m0, q, zq), jnp.where(m0, zq, q))

        def keys(j):
            start = pl.multiple_of(jnp.maximum(j, 0) * B, B)
            return k_ref[pl.ds(start, B), :]

        def vals(j):
            start = pl.multiple_of(jnp.maximum(j, 0) * B, B)
            return v_ref[pl.ds(start, B), :].astype(BF16)

        strips = [slice(s, min(s + ATTN_STRIP, B)) for s in range(0, B, ATTN_STRIP)]

        def scores(j):
            kj = keys(j)
            return tuple(_dot(qh[h], kj, NT) for h in (0, 1))

        def logits(zs2, slot, diag):
            for h in (0, 1):
                z = zs2[h]
                his, los = [], []
                for rows in strips:
                    zs = z[rows]
                    lb = jnp.minimum(zs, 0.0) - _softplus_neg_abs(zs)
                    l1 = lb - zs
                    if diag:
                        l1 = jnp.where(strict[rows], l1, 0.0)
                    lb_scr[slot][h][rows, :] = lb
                    hi = l1.astype(BF16)
                    his.append(hi)
                    los.append((l1 - hi.astype(F32)).astype(BF16))
                    rsum = jnp.sum(l1, axis=-1, keepdims=True)
                    if h == 0:
                        sum_scr[slot][rows, :] = jnp.broadcast_to(rsum, (rows.stop - rows.start, LANES))
                    else:
                        sum_scr[slot][rows, :] = jnp.where(m0, sum_scr[slot][rows, :], rsum)
                cat = lambda xs: jnp.concatenate(xs, axis=0)
                tail_scr[slot][h][...] = _dot(cat(his), upper) + _dot(cat(los), upper)

        def attend(j, slot, diag):
            vj = vals(j)
            pv = []
            for h in (0, 1):
                ws = []
                for rows in strips:
                    w = jnp.exp(lb_scr[slot][h][rows, :] + tail_scr[slot][h][rows, :])
                    if diag:
                        w = jnp.where(strict[rows], w, 0.0)
                    ws.append(w.astype(BF16))
                pv.append(_dot(jnp.concatenate(ws, axis=0), vj))
            return jnp.where(m0, pv[0], pv[1])

        logits(scores(i), 0, True)
        o_ref[...] = attend(i, 0, True)
        tot_ref[...] = sum_scr[0][...]

        def half(j, slot):
            z = scores(j - 1)
            pv = attend(j, slot, False)
            logits(z, 1 - slot, False)
            o_ref[...] += pv * jnp.exp(tot_ref[...])
            tot_ref[...] += sum_scr[slot][...]

        @pl.when(i > 0)
        def _():
            logits(scores(i - 1), 1, False)

            @pl.loop(0, (i + 1) // 2)
            def _(t):
                j = i - 1 - 2 * t
                half(j, 1)

                @pl.when(j > 0)
                def _():
                    half(j - 1, 0)


    qspec = pl.BlockSpec((B, LANES), lambda hp, i: (i, hp))
    kspec = pl.BlockSpec((L, LANES), lambda hp, i: (0, HP + hp))
    vspec = pl.BlockSpec((L, LANES), lambda hp, i: (0, 2 * HP + hp))
    ospec = pl.BlockSpec((B, LANES), lambda hp, i: (i, hp))
    shp = jax.ShapeDtypeStruct((L, AW), F32)
    return _pcall(body, name="attn_fwd", grid=(HP, nb), in_specs=[qspec, kspec, vspec],
                  out_specs=[ospec, ospec], out_shape=[shp, shp],
                  scratch=[pltpu.VMEM((B, B), F32)] * 8 + [pltpu.VMEM((B, LANES), F32)] * 2)(qk, qk, p)


def _attn_bwd(qk, p, do, tot, AW, B):
    L = qk.shape[0]
    HP = AW // LANES
    nb = L // B

    def body(q_ref, k_ref, v_ref, do_ref, tot_ref, dq_ref, dk_ref, dv_ref):
        i = pl.program_id(1)
        m0 = lax.broadcasted_iota(jnp.int32, (1, LANES), 1) < HEAD_DIM
        strict, upper, lower = _attn_masks(B)

        @pl.when(i == 0)
        def _():
            dk_ref[...] = jnp.zeros_like(dk_ref)
            dv_ref[...] = jnp.zeros_like(dv_ref)

        q = q_ref[...] * 0.125
        zq = jnp.zeros_like(q)
        qh = (jnp.where(m0, q, zq), jnp.where(m0, zq, q))
        dob = do_ref[...].astype(BF16)
        doh = (jnp.where(m0, dob, zq), jnp.where(m0, zq, dob))
        tv = tot_ref[...]
        th = (jnp.sum(jnp.where(m0, tv, 0.0), axis=-1, keepdims=True) / HEAD_DIM,
              jnp.sum(jnp.where(m0, 0.0, tv), axis=-1, keepdims=True) / HEAD_DIM)

        def block(j, carry, diag):
            pls, es, dq = carry
            start = pl.multiple_of(j * B, B)
            kj = k_ref[pl.ds(start, B), :]
            vj = v_ref[pl.ds(start, B), :].astype(BF16)
            zk = jnp.zeros_like(kj)
            kh = (jnp.where(m0, kj, zk), jnp.where(m0, zk, kj))
            new_pl, new_e = [], []
            dk_blk = jnp.zeros((B, LANES), F32)
            dv_blk = jnp.zeros((B, LANES), F32)
            for h in (0, 1):
                z = _dot(qh[h], kj, NT)
                sp = _softplus_neg_abs(z)
                lb = jnp.minimum(z, 0.0) - sp
                l1 = lb - z
                if diag:
                    l1 = jnp.where(strict, l1, 0.0)
                bsum = jnp.sum(l1, axis=-1, keepdims=True)
                right = th[h] - pls[h] - bsum
                tail = _split_dot(l1, upper)
                w = jnp.exp(lb + tail + right)
                if diag:
                    w = jnp.where(strict, w, 0.0)
                dw = _dot(doh[h], vj, NT)
                e = dw * w
                dl1 = _split_dot(e, lower) + es[h]
                beta = jnp.exp(lb)
                dz = e * (1.0 - beta) - dl1 * beta
                if diag:
                    dz = jnp.where(strict, dz, 0.0)
                dzb = dz.astype(BF16)
                wb = w.astype(BF16)
                dq = dq + _dot(dzb, kh[h])
                dk_blk = dk_blk + _dot(dzb, qh[h], TN)
                dv_blk = dv_blk + _dot(wb, doh[h], TN)
                new_pl.append(pls[h] + bsum)
                new_e.append(es[h] + jnp.sum(e, axis=-1, keepdims=True))
            dk_ref[pl.ds(start, B), :] += dk_blk
            dv_ref[pl.ds(start, B), :] += dv_blk
            return tuple(new_pl), tuple(new_e), dq

        zero = jnp.zeros((B, 1), F32)
        carry = ((zero, zero), (zero, zero), jnp.zeros((B, LANES), F32))
        carry = lax.fori_loop(0, i, lambda j, c: block(j, c, False), carry)
        _, _, dq = block(i, carry, True)
        dq_ref[...] = dq * 0.125

    qspec = pl.BlockSpec((B, LANES), lambda hp, i: (i, hp))
    kspec = pl.BlockSpec((L, LANES), lambda hp, i: (0, HP + hp))
    vspec = pl.BlockSpec((L, LANES), lambda hp, i: (0, 2 * HP + hp))
    full = pl.BlockSpec((L, LANES), lambda hp, i: (0, hp))
    shp = jax.ShapeDtypeStruct((L, AW), F32)
    return _pcall(body, name="attn_bwd", grid=(HP, nb), in_specs=[qspec, kspec, vspec, qspec, qspec],
                  out_specs=[qspec, full, full], out_shape=[shp, shp, shp])(qk, qk, p, do, tot)


def _s5_disc(ar, ai, ldt):
    dt = jnp.exp(ldt)
    mag = jnp.exp(dt * ar)
    abr = mag * jnp.cos(dt * ai)
    abi = mag * jnp.sin(dt * ai)
    emr = abr - 1.0
    emi = abi
    den = ar * ar + ai * ai
    fr = (emr * ar + emi * ai) / den
    fi = (emi * ar - emr * ai) / den
    return abr, abi, fr, fi


def _s5_params_math(ar, ai, ldt, br, bi):
    abr, abi, fr, fi = _s5_disc(ar, ai, ldt)
    return abr, abi, fr * br - fi * bi, fr * bi + fi * br


def _scan_rows(xr, xi, lev, T, reverse):
    row = lax.broadcasted_iota(jnp.int32, xr.shape, 0)
    d, k = 1, 0
    while d < T:
        kr, ki = lev(k)
        if reverse:
            keep = row < T - d
            sr = jnp.where(keep, pltpu.roll(xr, T - d, 0), 0.0)
            si = jnp.where(keep, pltpu.roll(xi, T - d, 0), 0.0)
            ki = -ki
        else:
            keep = row >= d
            sr = jnp.where(keep, pltpu.roll(xr, d, 0), 0.0)
            si = jnp.where(keep, pltpu.roll(xi, d, 0), 0.0)
        xr, xi = xr + (kr * sr - ki * si), xi + (kr * si + ki * sr)
        d, k = 2 * d, k + 1
    return xr, xi


def _s5_prep(ar, ai, ldt, braw_r, braw_i, T):
    NL, _, NS = ar.shape
    GT = NS // STATE_TILE
    nlev = max(1, int(math.log2(T)))
    LV = 8 * ((nlev + 7) // 8)

    def body(ar_ref, ai_ref, ldt_ref, br_ref, bi_ref,
             abr_ref, abi_ref, levr_ref, levi_ref, powr_ref, powi_ref, bsr_ref, bsi_ref):
        levr_ref[...] = jnp.zeros_like(levr_ref)
        levi_ref[...] = jnp.zeros_like(levi_ref)
        for t in range(GT):
            sl = slice(t * STATE_TILE, (t + 1) * STATE_TILE)
            abr, abi, bsr, bsi = _s5_params_math(ar_ref[:, sl], ai_ref[:, sl], ldt_ref[:, sl],
                                                 br_ref[t], bi_ref[t])
            abr_ref[:, sl] = abr
            abi_ref[:, sl] = abi
            bsr_ref[t] = bsr.astype(BF16)
            bsi_ref[t] = bsi.astype(BF16)
            kr, ki = abr, abi
            for k in range(nlev):
                levr_ref[k:k + 1, sl] = kr
                levi_ref[k:k + 1, sl] = ki
                kr, ki = kr * kr - ki * ki, 2.0 * kr * ki
        for s in range(NS // LANES):
            sl = slice(s * LANES, (s + 1) * LANES)
            row = lax.broadcasted_iota(jnp.int32, (T, LANES), 0)
            xr = jnp.where(row == 0, abr_ref[:, sl], 0.0)
            xi = jnp.where(row == 0, abi_ref[:, sl], 0.0)
            lev = lambda k, sl=sl: (levr_ref[k:k + 1, sl], levi_ref[k:k + 1, sl])
            xr, xi = _scan_rows(xr, xi, lev, T, False)
            powr_ref[:, sl] = xr
            powi_ref[:, sl] = xi

    rowspec = pl.BlockSpec((None, 1, NS), lambda l: (l, 0, 0))
    bspec = pl.BlockSpec((None, GT, LANES, STATE_TILE), lambda l: (l, 0, 0, 0))
    levspec = pl.BlockSpec((None, LV, NS), lambda l: (l, 0, 0))
    powspec = pl.BlockSpec((None, T, NS), lambda l: (l, 0, 0))
    rs = jax.ShapeDtypeStruct((NL, 1, NS), F32)
    ls = jax.ShapeDtypeStruct((NL, LV, NS), F32)
    ps = jax.ShapeDtypeStruct((NL, T, NS), F32)
    bs = jax.ShapeDtypeStruct((NL, GT, LANES, STATE_TILE), BF16)
    return _pcall(body, name="s5_prep", grid=(NL,), in_specs=[rowspec] * 3 + [bspec] * 2,
                  out_specs=[rowspec, rowspec, levspec, levspec, powspec, powspec, bspec, bspec],
                  out_shape=[rs, rs, ls, ls, ps, ps, bs, bs])(ar, ai, ldt, braw_r, braw_i)


def _s5_prep_bwd(ar, ai, ldt, braw_r, braw_i, dabr, dabi, dbsr, dbsi):
    NL, _, NS = ar.shape
    GT = NS // STATE_TILE

    def body(ar_ref, ai_ref, ldt_ref, br_ref, bi_ref, dabr_ref, dabi_ref, dbsr_ref, dbsi_ref,
             dar_ref, dai_ref, dldt_ref, dbr_ref, dbi_ref):
        for t in range(GT):
            sl = slice(t * STATE_TILE, (t + 1) * STATE_TILE)
            _, vjp = jax.vjp(_s5_params_math, ar_ref[:, sl], ai_ref[:, sl], ldt_ref[:, sl],
                             br_ref[t], bi_ref[t])
            dar, dai, dldt, dbr, dbi = vjp((dabr_ref[:, sl], dabi_ref[:, sl], dbsr_ref[t], dbsi_ref[t]))
            dar_ref[:, sl] = dar
            dai_ref[:, sl] = dai
            dldt_ref[:, sl] = dldt
            dbr_ref[t] = dbr
            dbi_ref[t] = dbi

    rowspec = pl.BlockSpec((None, 1, NS), lambda l: (l, 0, 0))
    bspec = pl.BlockSpec((None, GT, LANES, STATE_TILE), lambda l: (l, 0, 0, 0))
    rs = jax.ShapeDtypeStruct((NL, 1, NS), F32)
    bs = jax.ShapeDtypeStruct((NL, GT, LANES, STATE_TILE), F32)
    return _pcall(body, name="s5_prep_bwd", grid=(NL,), in_specs=[rowspec] * 3 + [bspec] * 2 + [rowspec] * 2 + [bspec] * 2,
                  out_specs=[rowspec] * 3 + [bspec] * 2, out_shape=[rs, rs, rs, bs, bs])(
                      ar, ai, ldt, braw_r, braw_i, dabr, dabi, dbsr, dbsi)


def _s5_specs(L, SW, T, ucol0, rev):
    GT = SW // LANES
    nc = L // T
    cidx = (lambda c: nc - 1 - c) if rev else (lambda c: c)
    return dict(
        GT=GT, nc=nc, cidx=cidx,
        u=pl.BlockSpec((T, LANES), lambda j, c: (cidx(c), ucol0 + j)),
        chan=pl.BlockSpec((T, LANES), lambda j, c: (cidx(c), j)),
        state=pl.BlockSpec((T, STATE_TILE), lambda j, c: (cidx(c), j)),
        bmat=pl.BlockSpec((None, LANES, STATE_TILE), lambda j, c: (j, 0, 0)),
        cmat=pl.BlockSpec((None, STATE_TILE, LANES), lambda j, c: (j, 0, 0)),
        gmat=pl.BlockSpec((None, LANES, LANES), lambda j, c: (j, 0, 0)),
        cvec=pl.BlockSpec((1, LANES), lambda j, c: (0, j)),
        svec=pl.BlockSpec((1, STATE_TILE), lambda j, c: (0, j)),
    )


def _s5_fwd(p, sp, SW, ucol0, T):
    L = p.shape[0]
    S = _s5_specs(L, SW, T, ucol0, False)
    NS = S["GT"] * STATE_TILE
    LV = sp["levr"].shape[0]

    def body(u_ref, bsr_ref, bsi_ref, levr_ref, levi_ref, powr_ref, powi_ref, cr_ref, ci_ref,
             d_ref, wg_ref, gb_ref, o_ref, sr_ref, si_ref, y_ref, carry_ref):
        @pl.when(pl.program_id(1) == 0)
        def _():
            carry_ref[...] = jnp.zeros_like(carry_ref)

        u = u_ref[...]
        ub = u.astype(BF16)
        bur = _dot(ub, bsr_ref[...])
        bui = _dot(ub, bsi_ref[...])
        for s in range(STATE_TILE // LANES):
            sl = slice(s * LANES, (s + 1) * LANES)
            lev = lambda k, sl=sl: (levr_ref[k:k + 1, sl], levi_ref[k:k + 1, sl])
            xr, xi = _scan_rows(bur[:, sl], bui[:, sl], lev, T, False)
            cr, ci = carry_ref[0:1, sl], carry_ref[1:2, sl]
            pr, pi = powr_ref[:, sl], powi_ref[:, sl]
            sr_ref[:, sl] = xr + (pr * cr - pi * ci)
            si_ref[:, sl] = xi + (pr * ci + pi * cr)
            carry_ref[0:1, sl] = sr_ref[T - 1:T, sl]
            carry_ref[1:2, sl] = si_ref[T - 1:T, sl]
        y = (_dot(sr_ref[...].astype(BF16), cr_ref[...]) - _dot(si_ref[...].astype(BF16), ci_ref[...])
             + d_ref[...] * u)
        y_ref[...] = y
        yg = jax.nn.gelu(y)
        gate = jax.nn.sigmoid(_dot(yg.astype(BF16), wg_ref[...]) + gb_ref[...])
        o_ref[...] = yg * gate

    lvspec = pl.BlockSpec((LV, STATE_TILE), lambda j, c: (0, j))
    pwspec = pl.BlockSpec((T, STATE_TILE), lambda j, c: (0, j))
    cs = jax.ShapeDtypeStruct((L, SW), F32)
    ss = jax.ShapeDtypeStruct((L, NS), F32)
    return _pcall(
        body, name="s5_fwd", grid=(S["GT"], S["nc"]),
        in_specs=[S["u"], S["bmat"], S["bmat"], lvspec, lvspec, pwspec, pwspec, S["cmat"], S["cmat"],
                  S["cvec"], S["gmat"], S["cvec"]],
        out_specs=[S["chan"], S["state"], S["state"], S["chan"]], out_shape=[cs, ss, ss, cs],
        scratch=[pltpu.VMEM((8, STATE_TILE), F32)],
    )(p, sp["bsr"], sp["bsi"], sp["levr"], sp["levi"], sp["powr"], sp["powi"], sp["crT"], sp["ciT"],
      sp["d"], sp["wg"], sp["gb"])


def _s5_bwd(do, p, s_re, s_im, ypre, sp, SW, ucol0, T):
    L = p.shape[0]
    S = _s5_specs(L, SW, T, ucol0, True)
    GT, nc, cidx = S["GT"], S["nc"], S["cidx"]
    NS = GT * STATE_TILE
    LV = sp["levr"].shape[0]
    T8 = T // 8

    def body(do_ref, u_ref, sr_ref, si_ref, hr_ref, hi_ref, y_ref, bsr_ref, bsi_ref, levr_ref, levi_ref,
             rpr_ref, rpi_ref, cr_ref, ci_ref, d_ref, wg_ref, gb_ref,
             du_ref, dbsr_ref, dbsi_ref, dcr_ref, dci_ref, dd_ref, dwg_ref, dgb_ref, dar_ref, dai_ref,
             carry_ref, lam_r, lam_i):
        c = pl.program_id(1)
        first = c == 0

        @pl.when(first)
        def _():
            carry_ref[...] = jnp.zeros_like(carry_ref)

        u = u_ref[...]
        ub = u.astype(BF16)
        y = y_ref[...]
        yg, gelu_vjp = jax.vjp(jax.nn.gelu, y)
        ygb = yg.astype(BF16)
        gate = jax.nn.sigmoid(_dot(ygb, wg_ref[...]) + gb_ref[...])
        dout = do_ref[...]
        dt = dout * yg * gate * (1.0 - gate)
        dtb = dt.astype(BF16)
        dyg = dout * gate + _dot(dtb, wg_ref[...], NT)
        (dy,) = gelu_vjp(dyg)
        dyb = dy.astype(BF16)
        _accumulate(dwg_ref, _dot(ygb, dtb, TN), first)
        _accumulate(dgb_ref, jnp.sum(dt, axis=0, keepdims=True), first)
        _accumulate(dd_ref, jnp.sum(dy * u, axis=0, keepdims=True), first)
        srb = sr_ref[...].astype(BF16)
        sib = si_ref[...].astype(BF16)
        _accumulate(dcr_ref, _dot(srb, dyb, TN), first)
        _accumulate(dci_ref, -_dot(sib, dyb, TN), first)
        dsr = _dot(dyb, cr_ref[...], NT)
        dsi = -_dot(dyb, ci_ref[...], NT)
        last_chunk = cidx(c) == 0
        row = lax.broadcasted_iota(jnp.int32, (T, LANES), 0)
        for s in range(STATE_TILE // LANES):
            sl = slice(s * LANES, (s + 1) * LANES)
            lev = lambda k, sl=sl: (levr_ref[k:k + 1, sl], levi_ref[k:k + 1, sl])
            xr, xi = _scan_rows(dsr[:, sl], dsi[:, sl], lev, T, True)
            cr, ci = carry_ref[0:1, sl], carry_ref[1:2, sl]
            pr, pi = rpr_ref[:, sl], -rpi_ref[:, sl]
            lam_r[:, sl] = xr + (pr * cr - pi * ci)
            lam_i[:, sl] = xi + (pr * ci + pi * cr)
            carry_ref[0:1, sl] = lam_r[0:1, sl]
            carry_ref[1:2, sl] = lam_i[0:1, sl]
            keep = jnp.logical_not(last_chunk)
            pr0 = jnp.where(keep, hr_ref[7:8, sl], 0.0)
            pi0 = jnp.where(keep, hi_ref[7:8, sl], 0.0)
            spr = jnp.where(row == 0, pr0, pltpu.roll(sr_ref[:, sl], 1, 0))
            spi = jnp.where(row == 0, pi0, pltpu.roll(si_ref[:, sl], 1, 0))
            lr, li = lam_r[:, sl], lam_i[:, sl]
            dar = jnp.sum(lr * spr + li * spi, axis=0, keepdims=True)
            dai = jnp.sum(li * spr - lr * spi, axis=0, keepdims=True)

            @pl.when(first)
            def _():
                dar_ref[:, sl] = dar
                dai_ref[:, sl] = dai

            @pl.when(jnp.logical_not(first))
            def _():
                dar_ref[:, sl] += dar
                dai_ref[:, sl] += dai

        lrb = lam_r[...].astype(BF16)
        lib = lam_i[...].astype(BF16)
        _accumulate(dbsr_ref, _dot(ub, lrb, TN), first)
        _accumulate(dbsi_ref, _dot(ub, lib, TN), first)
        du_ref[...] = dy * d_ref[...] + _dot(lrb, bsr_ref[...], NT) + _dot(lib, bsi_ref[...], NT)

    halo = pl.BlockSpec((8, STATE_TILE), lambda j, c: (jnp.maximum(cidx(c) * T8 - 1, 0), j))
    lvspec = pl.BlockSpec((LV, STATE_TILE), lambda j, c: (0, j))
    pwspec = pl.BlockSpec((T, STATE_TILE), lambda j, c: (0, j))
    f = lambda *s: jax.ShapeDtypeStruct(s, F32)
    return _pcall(
        body, name="s5_bwd", grid=(GT, nc),
        in_specs=[S["chan"], S["u"], S["state"], S["state"], halo, halo, S["chan"], S["bmat"], S["bmat"],
                  lvspec, lvspec, pwspec, pwspec, S["cmat"], S["cmat"], S["cvec"], S["gmat"], S["cvec"]],
        out_specs=[S["chan"], S["bmat"], S["bmat"], S["cmat"], S["cmat"], S["cvec"], S["gmat"], S["cvec"],
                   S["svec"], S["svec"]],
        out_shape=[f(L, SW), f(GT, LANES, STATE_TILE), f(GT, LANES, STATE_TILE), f(GT, STATE_TILE, LANES),
                   f(GT, STATE_TILE, LANES), f(1, SW), f(GT, LANES, LANES), f(1, SW), f(1, NS), f(1, NS)],
        scratch=[pltpu.VMEM((8, STATE_TILE), F32), pltpu.VMEM((T, STATE_TILE), F32),
                 pltpu.VMEM((T, STATE_TILE), F32)],
    )(do, p, s_re, s_im, s_re, s_im, ypre, sp["bsr"], sp["bsi"], sp["levr"], sp["levi"],
      sp["rpowr"], sp["rpowi"], sp["crT"], sp["ciT"], sp["d"], sp["wg"], sp["gb"])


def _conv_taps(xc, h6, h7, row):
    x1 = jnp.where(row == 0, h7, pltpu.roll(xc, 1, 0))
    x2 = jnp.where(row == 0, h6, jnp.where(row == 1, h7, pltpu.roll(xc, 2, 0)))
    return x1, x2


def _conv_halves(up_ref, halo_ref, w_ref, b_ref, tm, FS):
    first = pl.program_id(1) == 0
    row = lax.broadcasted_iota(jnp.int32, (tm, FS), 0)
    outs, taps = [], []
    for s in (0, 1):
        xc = up_ref[s]
        h6 = jnp.where(first, 0.0, halo_ref[s, 6:7, :])
        h7 = jnp.where(first, 0.0, halo_ref[s, 7:8, :])
        x1, x2 = _conv_taps(xc, h6, h7, row)
        outs.append(b_ref[s] + x2 * w_ref[s, 0:1, :] + x1 * w_ref[s, 1:2, :] + xc * w_ref[s, 2:3, :])
        taps.append((x2, x1, xc))
    return outs, taps


def _convglu_specs(L, FS, tm):
    t8 = tm // 8
    return dict(
        up=pl.BlockSpec((2, None, tm, FS), lambda j, i: (0, j, i, 0)),
        halo=pl.BlockSpec((2, None, 8, FS), lambda j, i: (0, j, jnp.maximum(i * t8 - 1, 0), 0)),
        w=pl.BlockSpec((2, None, 3, FS), lambda j, i: (0, j, 0, 0)),
        b=pl.BlockSpec((2, None, 1, FS), lambda j, i: (0, j, 0, 0)),
        act=pl.BlockSpec((None, tm, FS), lambda j, i: (j, i, 0)),
    )


def _convglu_fwd(up, w, b):
    _, NSH, L, FS = up.shape
    tm = _row_tile(L)
    S = _convglu_specs(L, FS, tm)

    def body(up_ref, halo_ref, w_ref, b_ref, act_ref):
        (val, gate), _ = _conv_halves(up_ref, halo_ref, w_ref, b_ref, tm, FS)
        act_ref[...] = _glu_math(val, gate).astype(BF16)

    return _pcall(body, name="convglu_fwd", grid=(NSH, L // tm), in_specs=[S["up"], S["halo"], S["w"], S["b"]],
                  out_specs=S["act"], out_shape=jax.ShapeDtypeStruct((NSH, L, FS), BF16))(up, up, w, b)


def _convglu_bwd(up, dact, w, b):
    _, NSH, L, FS = up.shape
    tm = _row_tile(L)
    S = _convglu_specs(L, FS, tm)

    def body(up_ref, halo_ref, w_ref, b_ref, dact_ref, dc_ref, dw_ref, db_ref):
        first = pl.program_id(1) == 0
        (val, gate), taps = _conv_halves(up_ref, halo_ref, w_ref, b_ref, tm, FS)
        _, vjp = jax.vjp(_glu_math, val, gate)
        dcs = vjp(dact_ref[...])
        for s in (0, 1):
            dc = dcs[s]
            dc_ref[s] = dc
            sums = [jnp.sum(dc * t, axis=0, keepdims=True) for t in taps[s]]
            dbs = jnp.sum(dc, axis=0, keepdims=True)

            @pl.when(first)
            def _():
                for t in range(3):
                    dw_ref[s, t:t + 1, :] = sums[t]
                db_ref[s] = dbs

            @pl.when(jnp.logical_not(first))
            def _():
                for t in range(3):
                    dw_ref[s, t:t + 1, :] += sums[t]
                db_ref[s] += dbs

    f = lambda *s: jax.ShapeDtypeStruct(s, F32)
    return _pcall(body, name="convglu_bwd", grid=(NSH, L // tm),
                  in_specs=[S["up"], S["halo"], S["w"], S["b"], S["act"]],
                  out_specs=[S["up"], S["w"], S["b"]],
                  out_shape=[f(2, NSH, L, FS), f(2, NSH, 3, FS), f(2, NSH, 1, FS)])(up, up, w, b, dact)


def _conv_transpose(dc, w):
    NS8, L, FS = dc.shape
    tm = _row_tile(L)
    t8 = tm // 8
    nt = L // tm

    def body(dc_ref, nxt_ref, w_ref, o_ref):
        last = pl.program_id(1) == nt - 1
        row = lax.broadcasted_iota(jnp.int32, (tm, FS), 0)
        xc = dc_ref[...]
        n0 = jnp.where(last, 0.0, nxt_ref[0:1, :])
        n1 = jnp.where(last, 0.0, nxt_ref[1:2, :])
        x1 = jnp.where(row == tm - 1, n0, pltpu.roll(xc, tm - 1, 0))
        x2 = jnp.where(row == tm - 1, n1, jnp.where(row == tm - 2, n0, pltpu.roll(xc, tm - 2, 0)))
        o_ref[...] = (xc * w_ref[2:3, :] + x1 * w_ref[1:2, :] + x2 * w_ref[0:1, :]).astype(BF16)

    blk = pl.BlockSpec((None, tm, FS), lambda j, i: (j, i, 0))
    nxt = pl.BlockSpec((None, 8, FS), lambda j, i: (j, jnp.minimum((i + 1) * t8, L // 8 - 1), 0))
    wsp = pl.BlockSpec((None, 3, FS), lambda j, i: (j, 0, 0))
    return _pcall(body, name="conv_transpose", grid=(NS8, nt), in_specs=[blk, nxt, wsp], out_specs=blk,
                  out_shape=jax.ShapeDtypeStruct((NS8, L, FS), BF16))(dc, dc, w)


def _loss_head(y, target):
    L, D = y.shape
    tm = _row_tile(L)

    def body(y_ref, t_ref, loss_ref, dy_ref):
        err = y_ref[...] - t_ref[...]
        dy_ref[...] = err / D
        part = 0.5 * jnp.sum(jnp.mean(err * err, axis=-1, keepdims=True), axis=0, keepdims=True)
        _accumulate(loss_ref, jnp.broadcast_to(part, (1, LANES)), pl.program_id(0) == 0)

    row = pl.BlockSpec((tm, D), lambda i: (i, 0))
    vec = pl.BlockSpec((1, LANES), lambda i: (0, 0))
    return _pcall(body, name="loss_head", grid=(L // tm,), in_specs=[row, row], out_specs=[vec, row],
                  out_shape=[jax.ShapeDtypeStruct((1, LANES), F32), jax.ShapeDtypeStruct((L, D), F32)])(y, target)


def _ada_fwd(c_all, ada_w, ada_b):
    NL, D, NC = ada_w.shape
    NB = c_all.shape[0]

    def body(c_ref, w_ref, b_ref, o_ref):
        cact = jax.nn.silu(c_ref[...])
        o_ref[...] = _dot(cact.astype(BF16), w_ref[...].astype(BF16)) + b_ref[...]

    return _pcall(body, name="ada_fwd", grid=(NL,),
                  in_specs=[pl.BlockSpec((NB, D), lambda l: (0, 0)), pl.BlockSpec((None, D, NC), lambda l: (l, 0, 0)),
                            pl.BlockSpec((None, 1, NC), lambda l: (l, 0, 0))],
                  out_specs=pl.BlockSpec((None, NB, NC), lambda l: (l, 0, 0)),
                  out_shape=jax.ShapeDtypeStruct((NL, NB, NC), F32))(c_all, ada_w, ada_b)


def _ada_bwd(c_all_t, dmod):
    D, NB = c_all_t.shape
    NL, _, NC = dmod.shape

    def body(c_ref, d_ref, o_ref):
        cact = jax.nn.silu(c_ref[...]).astype(BF16).astype(F32)
        o_ref[...] = _dot(cact, d_ref[...].astype(BF16).astype(F32))

    return _pcall(body, name="ada_bwd", grid=(NL,),
                  in_specs=[pl.BlockSpec((D, NB), lambda l: (0, 0)), pl.BlockSpec((None, NB, NC), lambda l: (l, 0, 0))],
                  out_specs=pl.BlockSpec((None, D, NC), lambda l: (l, 0, 0)),
                  out_shape=jax.ShapeDtypeStruct((NL, D, NC), F32))(c_all_t, dmod)


def _adamw(parts, w, m, v):
    NL, P, R, C = parts.shape
    tr = R
    for cand in (256, 128, 64, 32, 16, 8):
        if R % cand == 0 and R > cand:
            tr = cand
            break

    def body(p_ref, w_ref, m_ref, v_ref, g_ref, d_ref, nm_ref, nv_ref):
        g = p_ref[0].astype(F32)
        for k in range(1, P):
            g = g + p_ref[k].astype(F32)
        m2 = ADAM_B1 * m_ref[...] + (1.0 - ADAM_B1) * g
        v2 = ADAM_B2 * v_ref[...] + (1.0 - ADAM_B2) * jnp.square(g)
        m_hat = m2 / (1.0 - ADAM_B1 ** ADAM_STEP)
        v_hat = v2 / (1.0 - ADAM_B2 ** ADAM_STEP)
        g_ref[...] = g
        d_ref[...] = -ADAM_LR * (m_hat / (jnp.sqrt(v_hat) + ADAM_EPS) + ADAM_WD * w_ref[...])
        nm_ref[...] = m2
        nv_ref[...] = v2

    pspec = pl.BlockSpec((None, P, tr, C), lambda l, i: (l, 0, i, 0))
    wspec = pl.BlockSpec((None, tr, C), lambda l, i: (l, i, 0))
    shp = jax.ShapeDtypeStruct((NL, R, C), F32)
    return _pcall(body, name="adamw", grid=(NL, R // tr), in_specs=[pspec, wspec, wspec, wspec],
                  out_specs=[wspec] * 4, out_shape=[shp] * 4)(parts, w, m, v)


def _block_diag(blocks):
    *lead, g, r, c = blocks.shape
    eye = jnp.eye(g, dtype=bool)[:, None, :, None]
    full = jnp.where(eye, blocks[..., :, :, None, :], 0.0)
    return full.reshape(*lead, g * r, g * c)


def _block_diag_extract(m, r, c):
    g = GROUPS_PER_TILE
    m5 = m.reshape(*m.shape[:-2], g, r, g, c)
    eye = jnp.eye(g, dtype=bool)[:, None, :, None]
    return jnp.sum(jnp.where(eye, m5, 0.0), axis=-2)


def kernel(x, c, ada_w, ada_b, norm1_g, w_in, q_norm_g, k_norm_g, ssm_a_re, ssm_a_im, ssm_log_dt, ssm_b_re, ssm_b_im, ssm_c_re, ssm_c_im, ssm_d, glu_w, glu_b, attn_out_g, ssm_out_g, w_out, norm2_g, ffn_w_up, ffn_conv_w, ffn_conv_b, ffn_w_down, loss_target, m_ada_w, m_ada_b, m_norm1_g, m_w_in, m_q_norm_g, m_k_norm_g, m_ssm_a_re, m_ssm_a_im, m_ssm_log_dt, m_ssm_b_re, m_ssm_b_im, m_ssm_c_re, m_ssm_c_im, m_ssm_d, m_glu_w, m_glu_b, m_attn_out_g, m_ssm_out_g, m_w_out, m_norm2_g, m_ffn_w_up, m_ffn_conv_w, m_ffn_conv_b, m_ffn_w_down, v_ada_w, v_ada_b, v_norm1_g, v_w_in, v_q_norm_g, v_k_norm_g, v_ssm_a_re, v_ssm_a_im, v_ssm_log_dt, v_ssm_b_re, v_ssm_b_im, v_ssm_c_re, v_ssm_c_im, v_ssm_d, v_glu_w, v_glu_b, v_attn_out_g, v_ssm_out_g, v_w_out, v_norm2_g, v_ffn_w_up, v_ffn_conv_w, v_ffn_conv_b, v_ffn_w_down):
    weights = dict(ada_w=ada_w, ada_b=ada_b, norm1_g=norm1_g, w_in=w_in, q_norm_g=q_norm_g, k_norm_g=k_norm_g,
                   ssm_a_re=ssm_a_re, ssm_a_im=ssm_a_im, ssm_log_dt=ssm_log_dt, ssm_b_re=ssm_b_re,
                   ssm_b_im=ssm_b_im, ssm_c_re=ssm_c_re, ssm_c_im=ssm_c_im, ssm_d=ssm_d, glu_w=glu_w, glu_b=glu_b,
                   attn_out_g=attn_out_g, ssm_out_g=ssm_out_g, w_out=w_out, norm2_g=norm2_g, ffn_w_up=ffn_w_up,
                   ffn_conv_w=ffn_conv_w, ffn_conv_b=ffn_conv_b, ffn_w_down=ffn_w_down)
    mom_m = dict(ada_w=m_ada_w, ada_b=m_ada_b, norm1_g=m_norm1_g, w_in=m_w_in, q_norm_g=m_q_norm_g,
                 k_norm_g=m_k_norm_g, ssm_a_re=m_ssm_a_re, ssm_a_im=m_ssm_a_im, ssm_log_dt=m_ssm_log_dt,
                 ssm_b_re=m_ssm_b_re, ssm_b_im=m_ssm_b_im, ssm_c_re=m_ssm_c_re, ssm_c_im=m_ssm_c_im, ssm_d=m_ssm_d,
                 glu_w=m_glu_w, glu_b=m_glu_b, attn_out_g=m_attn_out_g, ssm_out_g=m_ssm_out_g, w_out=m_w_out,
                 norm2_g=m_norm2_g, ffn_w_up=m_ffn_w_up, ffn_conv_w=m_ffn_conv_w, ffn_conv_b=m_ffn_conv_b,
                 ffn_w_down=m_ffn_w_down)
    mom_v = dict(ada_w=v_ada_w, ada_b=v_ada_b, norm1_g=v_norm1_g, w_in=v_w_in, q_norm_g=v_q_norm_g,
                 k_norm_g=v_k_norm_g, ssm_a_re=v_ssm_a_re, ssm_a_im=v_ssm_a_im, ssm_log_dt=v_ssm_log_dt,
                 ssm_b_re=v_ssm_b_re, ssm_b_im=v_ssm_b_im, ssm_c_re=v_ssm_c_re, ssm_c_im=v_ssm_c_im, ssm_d=v_ssm_d,
                 glu_w=v_glu_w, glu_b=v_glu_b, attn_out_g=v_attn_out_g, ssm_out_g=v_ssm_out_g, w_out=v_w_out,
                 norm2_g=v_norm2_g, ffn_w_up=v_ffn_w_up, ffn_conv_w=v_ffn_conv_w, ffn_conv_b=v_ffn_conv_b,
                 ffn_w_down=v_ffn_w_down)
    names = list(weights)
    big = ("ada_w", "w_in", "w_out", "ffn_w_up", "ffn_conv_w", "ffn_w_down")
    small = [n for n in names if n not in big]

    x = x[0]
    target = loss_target[0]
    L, D = x.shape
    NL = ada_w.shape[0]
    AW = D // 2
    SW = D - AW
    NH = AW // HEAD_DIM
    HP = AW // LANES
    G = SW // SSM_GROUP
    GT = SW // LANES
    NS = G * SSM_STATE
    NIN = w_in.shape[-1]
    FS = ffn_w_up.shape[-1]
    NSH = NDEV // 2
    NCA = ada_w.shape[-1]
    ROWS_OUT = w_out.shape[1]
    ROWS_DOWN = ffn_w_down.shape[1]
    B_ATT = min(L, 256)
    T_S5 = min(L, 128)
    tm = _row_tile(L)
    me = _my_index()

    cpad = jnp.reshape(c, (D // LANES, LANES))
    c_all = _gather_small(cpad, "gather_c")
    c_all = c_all.reshape(NDEV, D)
    w_in_g, w_out_g, w_up_g, conv_w_g, w_down_g = _gather_shards(
        [w_in.astype(BF16), w_out.astype(BF16), ffn_w_up.astype(BF16), ffn_conv_w, ffn_w_down.astype(BF16)],
        "gather_weights")
    w_out_g = w_out_g.reshape(NL, D, D)
    w_down_g = w_down_g.reshape(NL, NSH, 2 * ROWS_DOWN, D)
    conv_w_g = conv_w_g.reshape(NL, 2, NSH, 3, FS)
    conv_b_g = ffn_conv_b.reshape(NL, 2, NSH, 1, FS)

    ada_b_mine = lax.dynamic_slice_in_dim(ada_b, me * NCA, NCA, axis=1).reshape(NL, 1, NCA)
    mod_part = _ada_fwd(c_all, ada_w, ada_b_mine)
    mod_all = _gather_small(mod_part, "gather_mod")
    mod = lax.dynamic_index_in_dim(mod_all, me, axis=2, keepdims=False)
    mod = jnp.transpose(mod, (1, 0, 2)).reshape(NL, N_MOD, 1, D)

    row = lambda a: a.reshape(NL, 1, NS)
    ar_row, ai_row = row(ssm_a_re), row(ssm_a_im)
    ldt_row = row(jnp.broadcast_to(ssm_log_dt[:, :, None], (NL, G, SSM_STATE)))
    tiles = lambda a: a.reshape((NL, GT, GROUPS_PER_TILE) + a.shape[2:])
    braw_r = _block_diag(jnp.swapaxes(tiles(ssm_b_re), -1, -2))
    braw_i = _block_diag(jnp.swapaxes(tiles(ssm_b_im), -1, -2))
    crT = _block_diag(jnp.swapaxes(tiles(ssm_c_re), -1, -2)).astype(BF16)
    ciT = _block_diag(jnp.swapaxes(tiles(ssm_c_im), -1, -2)).astype(BF16)
    wg = _block_diag(tiles(glu_w)).astype(BF16)
    abr, abi, levr, levi, powr, powi, bsr, bsi = _s5_prep(ar_row, ai_row, ldt_row, braw_r, braw_i, T_S5)
    s5p = dict(bsr=bsr, bsi=bsi, levr=levr, levi=levi, powr=powr, powi=powi,
               rpowr=jnp.flip(powr, axis=1), rpowi=jnp.flip(powi, axis=1), crT=crT, ciT=ciT,
               d=ssm_d.reshape(NL, 1, SW), wg=wg, gb=glu_b.reshape(NL, 1, SW))

    gqk = jnp.concatenate([jnp.tile(q_norm_g, (1, NH)), jnp.tile(k_norm_g, (1, NH))], axis=1).reshape(NL, 1, 2 * AW)
    layer_params = dict(
        mod=mod, norm1_g=norm1_g.reshape(NL, 1, D), norm2_g=norm2_g.reshape(NL, 1, D), gqk=gqk,
        ga=attn_out_g.reshape(NL, 1, AW), gs=ssm_out_g.reshape(NL, 1, SW),
        w_in=w_in_g, w_out=w_out_g, w_up=w_up_g, conv_w=conv_w_g, conv_b=conv_b_g, w_down=w_down_g, s5=s5p)

    ucol0 = 3 * AW // LANES

    def mm_rows_shards(name, a, b, n, out_dtype=F32):
        K = a.shape[1]
        return _matmul(name, a, b, dims=NN, grid=(L // tm, NDEV),
                       a_spec=pl.BlockSpec((tm, K), lambda i, j: (i, 0)),
                       b_spec=pl.BlockSpec((None, K, n), lambda i, j: (j, 0, 0)),
                       out_shape=jax.ShapeDtypeStruct((L, NDEV * n), out_dtype),
                       out_specs=pl.BlockSpec((tm, n), lambda i, j: (i, j)))

    tn = min(D, 512)

    def resid_epilogue(acc, xres, gate):
        return acc, xres + gate * acc

    def layer_fwd(xin, lp):
        sh1, sc1, g1, sh2, sc2, g2 = (lp["mod"][k] for k in range(N_MOD))
        h = _lnmod_fwd(xin, lp["norm1_g"], sh1, sc1)
        p = mm_rows_shards("mm_in", h, lp["w_in"], NIN)
        qk = _qknorm_fwd(p, lp["gqk"], AW)
        o_attn, tot = _attn_fwd(qk, p, AW, B_ATT)
        o_ssm, s_re, s_im, ypre = _s5_fwd(p, lp["s5"], SW, ucol0, T_S5)
        o = _outnorm_fwd(o_attn, o_ssm, lp["ga"], lp["gs"])
        a1, x_mid = _matmul(
            "mm_out", o, lp["w_out"], dims=NN, grid=(L // tm, D // tn),
            a_spec=pl.BlockSpec((tm, D), lambda i, j: (i, 0)), b_spec=pl.BlockSpec((D, tn), lambda i, j: (0, j)),
            extra=(xin, g1), extra_specs=(pl.BlockSpec((tm, tn), lambda i, j: (i, j)),
                                          pl.BlockSpec((1, tn), lambda i, j: (0, j))),
            epilogue=resid_epilogue,
            out_shape=[jax.ShapeDtypeStruct((L, D), F32)] * 2,
            out_specs=[pl.BlockSpec((tm, tn), lambda i, j: (i, j))] * 2)
        h2 = _lnmod_fwd(x_mid, lp["norm2_g"], sh2, sc2)
        up = _matmul(
            "mm_up", h2, lp["w_up"], dims=NN, grid=(L // tm, NDEV),
            a_spec=pl.BlockSpec((tm, D), lambda i, j: (i, 0)), b_spec=pl.BlockSpec((None, D, FS), lambda i, j: (j, 0, 0)),
            out_shape=jax.ShapeDtypeStruct((NDEV, L, FS), F32),
            out_specs=pl.BlockSpec((None, tm, FS), lambda i, j: (j, i, 0)))
        up = up.reshape(2, NSH, L, FS)
        act = _convglu_fwd(up, lp["conv_w"], lp["conv_b"])
        a2, x_out = _matmul(
            "mm_down", act, lp["w_down"], dims=NN, grid=(L // tm, D // tn, NSH), kaxis=2, acc_shape=(tm, tn),
            a_spec=pl.BlockSpec((None, tm, FS), lambda i, j, k: (k, i, 0)),
            b_spec=pl.BlockSpec((None, FS, tn), lambda i, j, k: (k, 0, j)),
            extra=(x_mid, g2), extra_specs=(pl.BlockSpec((tm, tn), lambda i, j, k: (i, j)),
                                            pl.BlockSpec((1, tn), lambda i, j, k: (0, j))),
            epilogue=resid_epilogue,
            out_shape=[jax.ShapeDtypeStruct((L, D), F32)] * 2,
            out_specs=[pl.BlockSpec((tm, tn), lambda i, j, k: (i, j))] * 2)
        res = dict(x=xin, h=h, p=p, qk=qk, tot=tot, o_attn=o_attn, o_ssm=o_ssm, s_re=s_re, s_im=s_im, ypre=ypre,
                   o=o, a1=a1, x_mid=x_mid, h2=h2, up=up, act=act, a2=a2)
        return x_out, res

    per_layer = [jax.tree.map(lambda a: a[l], layer_params) for l in range(NL)]
    y, residuals = x, []
    for l in range(NL):
        y, res = layer_fwd(y, per_layer[l])
        residuals.append(res)

    loss_row, dy = _loss_head(y, target)
    loss = lax.psum(loss_row[0, 0], ("x", "y", "c"))

    def layer_bwd(dx, args):
        lp, r = args
        sh1, sc1, g1, sh2, sc2, g2 = (lp["mod"][k] for k in range(N_MOD))
        da2, dg2 = _gate_bwd(dx, r["a2"], g2)
        dact = _matmul(
            "mm_dact", da2, lp["w_down"], dims=NT, grid=(L // tm, NSH),
            a_spec=pl.BlockSpec((tm, D), lambda i, j: (i, 0)), b_spec=pl.BlockSpec((None, FS, D), lambda i, j: (j, 0, 0)),
            out_shape=jax.ShapeDtypeStruct((NSH, L, FS), F32),
            out_specs=pl.BlockSpec((None, tm, FS), lambda i, j: (j, i, 0)))
        dw_down = _matmul(
            "mm_dw_down", r["act"], da2, dims=TN, grid=(NSH, D // tn, L // tm), kaxis=2, acc_shape=(FS, tn),
            a_spec=pl.BlockSpec((None, tm, FS), lambda j, n, k: (j, k, 0)),
            b_spec=pl.BlockSpec((tm, tn), lambda j, n, k: (k, n)),
            out_shape=jax.ShapeDtypeStruct((NSH, FS, D), BF16),
            out_specs=pl.BlockSpec((None, FS, tn), lambda j, n, k: (j, 0, n)))
        dconv, dcw, dcb = _convglu_bwd(r["up"], dact, lp["conv_w"], lp["conv_b"])
        dup = _conv_transpose(dconv.reshape(NDEV, L, FS), lp["conv_w"].reshape(NDEV, 3, FS))
        dh2 = _matmul(
            "mm_dh2", dup, lp["w_up"], dims=NT, grid=(L // tm, NDEV), kaxis=1, acc_shape=(tm, D),
            a_spec=pl.BlockSpec((None, tm, FS), lambda i, k: (k, i, 0)),
            b_spec=pl.BlockSpec((None, D, FS), lambda i, k: (k, 0, 0)),
            out_shape=jax.ShapeDtypeStruct((L, D), F32), out_specs=pl.BlockSpec((tm, D), lambda i, k: (i, 0)))
        dw_up = _matmul(
            "mm_dw_up", r["h2"], dup, dims=TN, grid=(NDEV, L // tm), kaxis=1, acc_shape=(D, FS),
            a_spec=pl.BlockSpec((tm, D), lambda j, k: (k, 0)), b_spec=pl.BlockSpec((None, tm, FS), lambda j, k: (j, k, 0)),
            out_shape=jax.ShapeDtypeStruct((NDEV, D, FS), BF16),
            out_specs=pl.BlockSpec((None, D, FS), lambda j, k: (j, 0, 0)))
        dxm, dn2, dsh2, dsc2 = _lnmod_bwd(dh2, r["x_mid"], lp["norm2_g"], sh2, sc2, dx)
        da1, dg1 = _gate_bwd(dxm, r["a1"], g1)
        do = _matmul(
            "mm_do", da1, lp["w_out"], dims=NT, grid=(L // tm, D // tn),
            a_spec=pl.BlockSpec((tm, D), lambda i, j: (i, 0)), b_spec=pl.BlockSpec((tn, D), lambda i, j: (j, 0)),
            out_shape=jax.ShapeDtypeStruct((L, D), F32), out_specs=pl.BlockSpec((tm, tn), lambda i, j: (i, j)))
        dw_out = _matmul(
            "mm_dw_out", r["o"], da1, dims=TN, grid=(D // tn, D // tn, L // tm), kaxis=2, acc_shape=(tn, tn),
            a_spec=pl.BlockSpec((tm, tn), lambda m, n, k: (k, m)), b_spec=pl.BlockSpec((tm, tn), lambda m, n, k: (k, n)),
            out_shape=jax.ShapeDtypeStruct((D, D), BF16), out_specs=pl.BlockSpec((tn, tn), lambda m, n, k: (m, n)))
        doa, dos, dga, dgs = _outnorm_bwd(do, r["o_attn"], r["o_ssm"], lp["ga"], lp["gs"])
        (du, dbsr, dbsi, dcr, dci, dd, dwg, dgb, dabr, dabi) = _s5_bwd(
            dos, r["p"], r["s_re"], r["s_im"], r["ypre"], lp["s5"], SW, ucol0, T_S5)
        dq, dk, dv = _attn_bwd(r["qk"], r["p"], doa, r["tot"], AW, B_ATT)
        dqk, dgqk = _qknorm_bwd(jnp.concatenate([dq, dk], axis=1), r["p"], lp["gqk"], AW)
        dp = jnp.concatenate([dqk, dv, du], axis=1).astype(BF16)
        dh = _matmul(
            "mm_dh", dp, lp["w_in"], dims=NT, grid=(L // tm, NDEV), kaxis=1, acc_shape=(tm, D),
            a_spec=pl.BlockSpec((tm, NIN), lambda i, k: (i, k)), b_spec=pl.BlockSpec((None, D, NIN), lambda i, k: (k, 0, 0)),
            out_shape=jax.ShapeDtypeStruct((L, D), F32), out_specs=pl.BlockSpec((tm, D), lambda i, k: (i, 0)))
        dw_in = _matmul(
            "mm_dw_in", r["h"], dp, dims=TN, grid=(NDEV, L // tm), kaxis=1, acc_shape=(D, NIN),
            a_spec=pl.BlockSpec((tm, D), lambda j, k: (k, 0)), b_spec=pl.BlockSpec((tm, NIN), lambda j, k: (k, j)),
            out_shape=jax.ShapeDtypeStruct((NDEV, D, NIN), BF16),
            out_specs=pl.BlockSpec((None, D, NIN), lambda j, k: (j, 0, 0)))
        dx0, dn1, dsh1, dsc1 = _lnmod_bwd(dh, r["x"], lp["norm1_g"], sh1, sc1, dxm)
        grads = dict(
            dmod=jnp.concatenate([dsh1, dsc1, dg1, dsh2, dsc2, dg2], axis=1), dn1=dn1, dn2=dn2, dgqk=dgqk,
            dga=dga, dgs=dgs, dbsr=dbsr, dbsi=dbsi, dcr=dcr, dci=dci, dd=dd, dwg=dwg, dgb=dgb, dabr=dabr, dabi=dabi,
            dcb=dcb, dw_in=dw_in, dw_out=dw_out.reshape(NDEV, ROWS_OUT, D), dw_up=dw_up,
            dcw=dcw.reshape(NDEV, 3, FS), dw_down=dw_down.reshape(NDEV, ROWS_DOWN, D))
        return dx0, grads

    grad_x, layer_grads = dy, [None] * NL
    for l in reversed(range(NL)):
        grad_x, layer_grads[l] = layer_bwd(grad_x, (per_layer[l], residuals[l]))
    gr = jax.tree.map(lambda *a: jnp.stack(a), *layer_grads)

    dar, dai, dldt, dbr_bd, dbi_bd = _s5_prep_bwd(ar_row, ai_row, ldt_row, braw_r, braw_i,
                                                  gr["dabr"], gr["dabi"], gr["dbsr"], gr["dbsi"])
    unt = lambda a: a.reshape((NL, G) + a.shape[3:])
    local = dict(
        ada_b=gr["dmod"].reshape(NL, N_MOD * D),
        norm1_g=gr["dn1"].reshape(NL, D), norm2_g=gr["dn2"].reshape(NL, D),
        q_norm_g=gr["dgqk"].reshape(NL, 2, NH, HEAD_DIM)[:, 0].sum(axis=1),
        k_norm_g=gr["dgqk"].reshape(NL, 2, NH, HEAD_DIM)[:, 1].sum(axis=1),
        ssm_a_re=dar.reshape(NL, G, SSM_STATE), ssm_a_im=dai.reshape(NL, G, SSM_STATE),
        ssm_log_dt=dldt.reshape(NL, G, SSM_STATE).sum(axis=-1),
        ssm_b_re=jnp.swapaxes(unt(_block_diag_extract(dbr_bd, SSM_GROUP, SSM_STATE)), -1, -2),
        ssm_b_im=jnp.swapaxes(unt(_block_diag_extract(dbi_bd, SSM_GROUP, SSM_STATE)), -1, -2),
        ssm_c_re=jnp.swapaxes(unt(_block_diag_extract(gr["dcr"], SSM_STATE, SSM_GROUP)), -1, -2),
        ssm_c_im=jnp.swapaxes(unt(_block_diag_extract(gr["dci"], SSM_STATE, SSM_GROUP)), -1, -2),
        ssm_d=gr["dd"].reshape(NL, G, SSM_GROUP),
        glu_w=unt(_block_diag_extract(gr["dwg"], SSM_GROUP, SSM_GROUP)),
        glu_b=gr["dgb"].reshape(NL, G, SSM_GROUP),
        attn_out_g=gr["dga"].reshape(NL, AW), ssm_out_g=gr["dgs"].reshape(NL, SW),
        ffn_conv_b=gr["dcb"].reshape(NL, 2 * NSH * FS),
    )

    def pack(tree):
        flat = jnp.concatenate([tree[n].reshape(-1) for n in small])
        pad = (-flat.shape[0]) % (8 * LANES)
        return jnp.pad(flat, (0, pad)).reshape(-1, LANES)

    small_parts = _gather_small(pack(local), "gather_small_grads")
    sg, sd, sm, sv = _adamw(small_parts[None], pack(weights)[None], pack(mom_m)[None], pack(mom_v)[None])

    def unpack(buf):
        flat = buf.reshape(-1)
        out, off = {}, 0
        for n in small:
            size = weights[n].size
            out[n] = flat[off:off + size].reshape(weights[n].shape)
            off += size
        return out

    ug, ud, um, uv = unpack(sg), unpack(sd), unpack(sm), unpack(sv)
    results = {n: (ug[n], ud[n], um[n], uv[n]) for n in small}

    dmod_all = _gather_small(gr["dmod"].reshape(NL, N_MOD * D), "gather_dmod")
    dmod_mine = lax.dynamic_slice_in_dim(dmod_all, me * NCA, NCA, axis=2)
    d_ada_w = _ada_bwd(jnp.transpose(c_all), jnp.transpose(dmod_mine, (1, 0, 2)))
    results["ada_w"] = tuple(_adamw(d_ada_w[:, None], ada_w, m_ada_w, v_ada_w))

    by_target = [g.reshape((NL, NDEV // 2, 2) + g.shape[2:])
                 for g in (gr["dw_in"], gr["dw_out"], gr["dw_up"], gr["dcw"], gr["dw_down"])]
    mine, theirs = _sibling_exchange(by_target, "pair_weight_grads")
    pair_sums = [_pair_add(a, b) for a, b in zip(mine, theirs)]
    parts = _chip_exchange(pair_sums, "scatter_weight_grads")
    for n, pt in zip(("w_in", "w_out", "ffn_w_up", "ffn_conv_w", "ffn_w_down"), parts):
        results[n] = tuple(_adamw(pt, weights[n], mom_m[n], mom_v[n]))

    out = [loss, grad_x[None]]
    for k in range(4):
        out.extend(results[n][k] for n in names)
    return tuple(out)
```

```python
import functools
import math

import jax
import jax.numpy as jnp
from jax import lax
from jax.experimental import pallas as pl
from jax.experimental.pallas import tpu as pltpu

F32 = jnp.float32
BF16 = jnp.bfloat16
NDEV = 8
LANES = 128
HEAD_DIM = 64
SSM_GROUP = 16
SSM_STATE = 64
GROUPS_PER_TILE = LANES // SSM_GROUP
STATE_TILE = GROUPS_PER_TILE * SSM_STATE
N_MOD = 6
ATTN_STRIP = 32
EPS = 1e-6
ADAM_LR, ADAM_B1, ADAM_B2, ADAM_EPS, ADAM_WD, ADAM_STEP = 0.001, 0.9, 0.999, 1e-08, 0.01, 10
VMEM_LIMIT = 48 * 1024 * 1024
MESH_IDS = pl.DeviceIdType.MESH

NN = (((1,), (0,)), ((), ()))
NT = (((1,), (1,)), ((), ()))
TN = (((0,), (0,)), ((), ()))


def _dot(a, b, dims=NN):
    return lax.dot_general(a, b, dims, preferred_element_type=F32)


def _pcall(body, *, name, out_shape, in_specs, out_specs, grid=(), scratch=()):
    return pl.pallas_call(
        body, name=name, grid=grid, in_specs=in_specs, out_specs=out_specs, out_shape=out_shape,
        scratch_shapes=list(scratch),
        compiler_params=pltpu.CompilerParams(vmem_limit_bytes=VMEM_LIMIT))


def _row_tile(n, want=512):
    t = min(n, want)
    assert n % t == 0
    return t


def _my_index():
    return 4 * lax.axis_index("x") + 2 * lax.axis_index("y") + lax.axis_index("c")


HBM_SPEC = pl.BlockSpec(memory_space=pltpu.HBM)


def _mesh_place():
    x, y, c = lax.axis_index("x"), lax.axis_index("y"), lax.axis_index("c")
    chips = [(1 - x, y), (x, 1 - y), (1 - x, 1 - y)]
    return x, y, c, chips


def _gather_small(arr, name):
    def body(in_ref, out_ref, send_sems, recv_sems, local_sem):
        x, y, c, _ = _mesh_place()
        me = 4 * x + 2 * y + c
        own = pltpu.make_async_copy(in_ref, out_ref.at[me], local_sem)
        own.start()
        sends, recvs = [], []
        for k in range(1, NDEV):
            px = 1 - x if k & 4 else x
            py = 1 - y if k & 2 else y
            pc = 1 - c if k & 1 else c
            common = dict(send_sem=send_sems.at[k - 1], recv_sem=recv_sems.at[k - 1],
                          device_id=(px, py, pc), device_id_type=MESH_IDS)
            snd = pltpu.make_async_remote_copy(src_ref=in_ref, dst_ref=out_ref.at[me], **common)
            snd.start()
            sends.append(snd)
            recvs.append(pltpu.make_async_remote_copy(
                src_ref=in_ref, dst_ref=out_ref.at[4 * px + 2 * py + pc], **common))
        for r in recvs:
            r.wait_recv()
        for s in sends:
            s.wait_send()
        own.wait()

    return pl.pallas_call(
        body, name=name, out_shape=jax.ShapeDtypeStruct((NDEV,) + arr.shape, arr.dtype),
        in_specs=[HBM_SPEC], out_specs=HBM_SPEC,
        scratch_shapes=[pltpu.SemaphoreType.DMA((NDEV - 1,)), pltpu.SemaphoreType.DMA((NDEV - 1,)),
                        pltpu.SemaphoreType.DMA(())],
    )(arr)


def _gather_shards(arrs, name):
    n = len(arrs)

    def body(*refs):
        ins, outs = refs[:n], refs[n:2 * n]
        send_sems, recv_sems, local_sems = refs[2 * n:]
        x, y, c, chips = _mesh_place()
        dev = lambda px, py, pc: 4 * px + 2 * py + pc

        def copy(k, a, src, block, to):
            return pltpu.make_async_remote_copy(
                src_ref=src, dst_ref=outs[a].at[:, block], send_sem=send_sems.at[k, a],
                recv_sem=recv_sems.at[k, a], device_id=to, device_id_type=MESH_IDS)

        local = [pltpu.make_async_copy(ins[a], outs[a].at[:, dev(x, y, c)], local_sems.at[a]) for a in range(n)]
        for cp in local:
            cp.start()
        sends = []
        for a in range(n):
            sends.append(copy(0, a, ins[a], dev(x, y, c), (x, y, 1 - c)))
        for j, (px, py) in enumerate(chips):
            for a in range(n):
                sends.append(copy(1 + j, a, ins[a], dev(x, y, c), (px, py, c)))
        for s in sends:
            s.start()
        for j, (px, py) in enumerate(chips):
            for a in range(n):
                got = outs[a].at[:, dev(px, py, c)]
                copy(1 + j, a, got, dev(px, py, c), (x, y, c)).wait_recv()
                fwd = copy(4 + j, a, got, dev(px, py, c), (x, y, 1 - c))
                fwd.start()
                sends.append(fwd)
        for a in range(n):
            copy(0, a, ins[a], dev(x, y, 1 - c), (x, y, c)).wait_recv()
        for j, (px, py) in enumerate(chips):
            for a in range(n):
                copy(4 + j, a, ins[a], dev(px, py, 1 - c), (x, y, c)).wait_recv()
        for s in sends:
            s.wait_send()
        for cp in local:
            cp.wait()

    out_shape = [jax.ShapeDtypeStruct((a.shape[0], NDEV) + a.shape[1:], a.dtype) for a in arrs]
    return list(pl.pallas_call(
        body, name=name, out_shape=out_shape, in_specs=[HBM_SPEC] * n, out_specs=[HBM_SPEC] * n,
        scratch_shapes=[pltpu.SemaphoreType.DMA((7, n)), pltpu.SemaphoreType.DMA((7, n)),
                        pltpu.SemaphoreType.DMA((n,))],
    )(*arrs))


def _sibling_exchange(arrs, name):
    n = len(arrs)

    def body(*refs):
        ins, own, got = refs[:n], refs[n:2 * n], refs[2 * n:3 * n]
        send_sems, recv_sems, local_sems = refs[3 * n:]
        x, y, c, _ = _mesh_place()
        local, sends = [], []
        for a in range(n):
            cp = pltpu.make_async_copy(ins[a].at[:, c], own[a], local_sems.at[a])
            cp.start()
            local.append(cp)
            snd = pltpu.make_async_remote_copy(
                src_ref=ins[a].at[:, 1 - c], dst_ref=got[a], send_sem=send_sems.at[a],
                recv_sem=recv_sems.at[a], device_id=(x, y, 1 - c), device_id_type=MESH_IDS)
            snd.start()
            sends.append(snd)
        for s in sends:
            s.wait_recv()
        for s in sends:
            s.wait_send()
        for cp in local:
            cp.wait()

    shapes = [jax.ShapeDtypeStruct(a.shape[:1] + a.shape[2:], a.dtype) for a in arrs]
    outs = pl.pallas_call(
        body, name=name, out_shape=shapes + shapes, in_specs=[HBM_SPEC] * n, out_specs=[HBM_SPEC] * (2 * n),
        scratch_shapes=[pltpu.SemaphoreType.DMA((n,)), pltpu.SemaphoreType.DMA((n,)),
                        pltpu.SemaphoreType.DMA((n,))],
    )(*arrs)
    return list(outs[:n]), list(outs[n:])


def _chip_exchange(arrs, name):
    n = len(arrs)

    def body(*refs):
        ins, outs = refs[:n], refs[n:2 * n]
        send_sems, recv_sems, local_sems = refs[2 * n:]
        x, y, c, chips = _mesh_place()
        mine = 2 * x + y
        local, sends, recvs = [], [], []
        for a in range(n):
            cp = pltpu.make_async_copy(ins[a].at[:, mine], outs[a].at[:, mine], local_sems.at[a])
            cp.start()
            local.append(cp)
        for j, (px, py) in enumerate(chips):
            for a in range(n):
                common = dict(send_sem=send_sems.at[j, a], recv_sem=recv_sems.at[j, a],
                              device_id=(px, py, c), device_id_type=MESH_IDS)
                src = ins[a].at[:, 2 * px + py]
                snd = pltpu.make_async_remote_copy(src_ref=src, dst_ref=outs[a].at[:, mine], **common)
                snd.start()
                sends.append(snd)
                recvs.append(pltpu.make_async_remote_copy(src_ref=src, dst_ref=outs[a].at[:, 2 * px + py], **common))
        for r in recvs:
            r.wait_recv()
        for s in sends:
            s.wait_send()
        for cp in local:
            cp.wait()

    out_shape = [jax.ShapeDtypeStruct(a.shape, a.dtype) for a in arrs]
    return list(pl.pallas_call(
        body, name=name, out_shape=out_shape, in_specs=[HBM_SPEC] * n, out_specs=[HBM_SPEC] * n,
        scratch_shapes=[pltpu.SemaphoreType.DMA((3, n)), pltpu.SemaphoreType.DMA((3, n)),
                        pltpu.SemaphoreType.DMA((n,))],
    )(*arrs))


def _pair_add(a, b):
    NL, NC, R, C = a.shape
    tr = R
    for cand in (256, 128, 64, 32, 16):
        if R % cand == 0 and R > cand:
            tr = cand
            break

    def body(a_ref, b_ref, o_ref):
        o_ref[...] = (a_ref[...].astype(F32) + b_ref[...].astype(F32)).astype(o_ref.dtype)

    spec = pl.BlockSpec((None, NC, tr, C), lambda l, i: (l, 0, i, 0))
    return _pcall(body, name="pair_add", grid=(NL, R // tr), in_specs=[spec, spec], out_specs=spec,
                  out_shape=jax.ShapeDtypeStruct(a.shape, a.dtype))(a, b)


def _matmul(name, a, b, *, dims, grid, a_spec, b_spec, out_shape, out_specs, kaxis=None, acc_shape=None,
            extra=(), extra_specs=(), epilogue=None):
    nk = grid[kaxis] if kaxis is not None else 1
    ne = len(extra)
    multi = isinstance(out_shape, (list, tuple))
    n_out = len(out_shape) if multi else 1

    def body(*refs):
        a_ref, b_ref = refs[0], refs[1]
        ex = refs[2:2 + ne]
        outs = refs[2 + ne:2 + ne + n_out]

        def write(res):
            vals = epilogue(res, *[e[...] for e in ex]) if epilogue is not None else (res,)
            for o, v in zip(outs, vals):
                o[...] = v.astype(o.dtype)

        part = _dot(a_ref[...].astype(BF16), b_ref[...].astype(BF16), dims)
        if nk == 1:
            write(part)
        else:
            acc = refs[-1]
            k = pl.program_id(kaxis)

            @pl.when(k == 0)
            def _():
                acc[...] = part

            @pl.when(k > 0)
            def _():
                acc[...] += part

            @pl.when(k == nk - 1)
            def _():
                write(acc[...])

    scratch = [pltpu.VMEM(acc_shape, F32)] if nk > 1 else []
    return _pcall(body, name=name, grid=grid, in_specs=[a_spec, b_spec, *extra_specs],
                  out_specs=out_specs, out_shape=out_shape, scratch=scratch)(a, b, *extra)


def _rms(x, g):
    inv = lax.rsqrt(jnp.mean(x * x, axis=-1, keepdims=True) + EPS)
    return x * inv * g


def _lnmod_math(x, g, sh, sc):
    return _rms(x, g) * (1.0 + sc) + sh


def _qkn_math(p, g):
    m0 = lax.broadcasted_iota(jnp.int32, (1, LANES), 1) < HEAD_DIM
    sq = p * p
    s0 = jnp.sum(jnp.where(m0, sq, 0.0), axis=-1, keepdims=True)
    s1 = jnp.sum(jnp.where(m0, 0.0, sq), axis=-1, keepdims=True)
    inv = jnp.where(m0, lax.rsqrt(s0 / HEAD_DIM + EPS), lax.rsqrt(s1 / HEAD_DIM + EPS))
    return p * inv * g


def _glu_math(val, gate):
    return jax.nn.gelu(gate) * val


def _accumulate(ref, val, first):
    @pl.when(first)
    def _():
        ref[...] = val

    @pl.when(jnp.logical_not(first))
    def _():
        ref[...] += val


def _lnmod_fwd(x, g, sh, sc):
    L, D = x.shape
    tm = _row_tile(L)

    def body(x_ref, g_ref, sh_ref, sc_ref, h_ref):
        h_ref[...] = _lnmod_math(x_ref[...], g_ref[...], sh_ref[...], sc_ref[...]).astype(BF16)

    row = pl.BlockSpec((tm, D), lambda i: (i, 0))
    vec = pl.BlockSpec((1, D), lambda i: (0, 0))
    return _pcall(body, name="lnmod_fwd", grid=(L // tm,), in_specs=[row, vec, vec, vec], out_specs=row,
                  out_shape=jax.ShapeDtypeStruct((L, D), BF16))(x, g, sh, sc)


def _lnmod_bwd(dh, x, g, sh, sc, dres):
    L, D = x.shape
    tm = _row_tile(L)

    def body(dh_ref, x_ref, g_ref, sh_ref, sc_ref, res_ref, dx_ref, dg_ref, dsh_ref, dsc_ref):
        _, vjp = jax.vjp(_lnmod_math, x_ref[...], g_ref[...], sh_ref[...], sc_ref[...])
        dx, dg, dsh, dsc = vjp(dh_ref[...])
        dx_ref[...] = dx + res_ref[...]
        first = pl.program_id(0) == 0
        _accumulate(dg_ref, dg, first)
        _accumulate(dsh_ref, dsh, first)
        _accumulate(dsc_ref, dsc, first)

    row = pl.BlockSpec((tm, D), lambda i: (i, 0))
    vec = pl.BlockSpec((1, D), lambda i: (0, 0))
    vs = jax.ShapeDtypeStruct((1, D), F32)
    return _pcall(body, name="lnmod_bwd", grid=(L // tm,), in_specs=[row, row, vec, vec, vec, row],
                  out_specs=[row, vec, vec, vec],
                  out_shape=[jax.ShapeDtypeStruct((L, D), F32), vs, vs, vs])(dh, x, g, sh, sc, dres)


def _gate_bwd(dx, a, gate):
    L, D = dx.shape
    tm = _row_tile(L)

    def body(dx_ref, a_ref, g_ref, da_ref, dg_ref):
        dxv = dx_ref[...]
        da_ref[...] = (g_ref[...] * dxv).astype(BF16)
        _accumulate(dg_ref, jnp.sum(dxv * a_ref[...], axis=0, keepdims=True), pl.program_id(0) == 0)

    row = pl.BlockSpec((tm, D), lambda i: (i, 0))
    vec = pl.BlockSpec((1, D), lambda i: (0, 0))
    return _pcall(body, name="gate_bwd", grid=(L // tm,), in_specs=[row, row, vec], out_specs=[row, vec],
                  out_shape=[jax.ShapeDtypeStruct((L, D), BF16), jax.ShapeDtypeStruct((1, D), F32)])(dx, a, gate)


def _qknorm_fwd(p, gqk, AW):
    L = p.shape[0]
    tm = _row_tile(L)
    ncol = 2 * AW // LANES

    def body(p_ref, g_ref, o_ref):
        o_ref[...] = _qkn_math(p_ref[...], g_ref[...]).astype(BF16)

    blk = pl.BlockSpec((tm, LANES), lambda i, j: (i, j))
    vec = pl.BlockSpec((1, LANES), lambda i, j: (0, j))
    return _pcall(body, name="qknorm_fwd", grid=(L // tm, ncol), in_specs=[blk, vec], out_specs=blk,
                  out_shape=jax.ShapeDtypeStruct((L, 2 * AW), BF16))(p, gqk)


def _qknorm_bwd(dqk, p, gqk, AW):
    L = p.shape[0]
    tm = _row_tile(L)
    ncol = 2 * AW // LANES

    def body(d_ref, p_ref, g_ref, dp_ref, dg_ref):
        _, vjp = jax.vjp(_qkn_math, p_ref[...], g_ref[...])
        dp, dg = vjp(d_ref[...])
        dp_ref[...] = dp
        _accumulate(dg_ref, dg, pl.program_id(1) == 0)

    blk = pl.BlockSpec((tm, LANES), lambda j, i: (i, j))
    vec = pl.BlockSpec((1, LANES), lambda j, i: (0, j))
    return _pcall(body, name="qknorm_bwd", grid=(ncol, L // tm), in_specs=[blk, blk, vec], out_specs=[blk, vec],
                  out_shape=[jax.ShapeDtypeStruct((L, 2 * AW), F32),
                             jax.ShapeDtypeStruct((1, 2 * AW), F32)])(dqk, p, gqk)


def _outnorm_fwd(oa, os_, ga, gs):
    L, AW = oa.shape
    SW = os_.shape[1]
    tm = _row_tile(L)

    def body(oa_ref, os_ref, ga_ref, gs_ref, o_ref):
        o_ref[:, :AW] = _rms(oa_ref[...], ga_ref[...]).astype(BF16)
        o_ref[:, AW:] = _rms(os_ref[...], gs_ref[...]).astype(BF16)

    ra = pl.BlockSpec((tm, AW), lambda i: (i, 0))
    rs = pl.BlockSpec((tm, SW), lambda i: (i, 0))
    va = pl.BlockSpec((1, AW), lambda i: (0, 0))
    vs = pl.BlockSpec((1, SW), lambda i: (0, 0))
    ro = pl.BlockSpec((tm, AW + SW), lambda i: (i, 0))
    return _pcall(body, name="outnorm_fwd", grid=(L // tm,), in_specs=[ra, rs, va, vs], out_specs=ro,
                  out_shape=jax.ShapeDtypeStruct((L, AW + SW), BF16))(oa, os_, ga, gs)


def _outnorm_bwd(do, oa, os_, ga, gs):
    L, AW = oa.shape
    SW = os_.shape[1]
    tm = _row_tile(L)

    def body(do_ref, oa_ref, os_ref, ga_ref, gs_ref, doa_ref, dos_ref, dga_ref, dgs_ref):
        first = pl.program_id(0) == 0
        _, vjp_a = jax.vjp(_rms, oa_ref[...], ga_ref[...])
        doa, dga = vjp_a(do_ref[:, :AW])
        _, vjp_s = jax.vjp(_rms, os_ref[...], gs_ref[...])
        dos, dgs = vjp_s(do_ref[:, AW:])
        doa_ref[...] = doa
        dos_ref[...] = dos
        _accumulate(dga_ref, dga, first)
        _accumulate(dgs_ref, dgs, first)

    ra = pl.BlockSpec((tm, AW), lambda i: (i, 0))
    rs = pl.BlockSpec((tm, SW), lambda i: (i, 0))
    va = pl.BlockSpec((1, AW), lambda i: (0, 0))
    vs = pl.BlockSpec((1, SW), lambda i: (0, 0))
    ro = pl.BlockSpec((tm, AW + SW), lambda i: (i, 0))
    return _pcall(body, name="outnorm_bwd", grid=(L // tm,), in_specs=[ro, ra, rs, va, vs],
                  out_specs=[ra, rs, va, vs],
                  out_shape=[jax.ShapeDtypeStruct((L, AW), F32), jax.ShapeDtypeStruct((L, SW), F32),
                             jax.ShapeDtypeStruct((1, AW), F32), jax.ShapeDtypeStruct((1, SW), F32)])(
                                 do, oa, os_, ga, gs)


def _softplus_neg_abs(z):
    return jnp.log(1.0 + jnp.exp(-jnp.abs(z)))


def _split_dot(x, m):
    hi = x.astype(BF16)
    lo = (x - hi.astype(F32)).astype(BF16)
    return _dot(hi, m) + _dot(lo, m)


def _attn_masks(B):
    row = lax.broadcasted_iota(jnp.int32, (B, B), 0)
    col = lax.broadcasted_iota(jnp.int32, (B, B), 1)
    strict = col < row
    upper = jnp.where(row > col, 1.0, 0.0).astype(BF16)
    lower = jnp.where(row < col, 1.0, 0.0).astype(BF16)
    return strict, upper, lower


def _attn_fwd(qk, p, AW, B):
    L = qk.shape[0]
    HP = AW // LANES
    nb = L // B

    def body(q_ref, k_ref, v_ref, o_ref, tot_ref, *scr):
        lb_scr = (scr[0:2], scr[2:4])
        tail_scr = (scr[4:6], scr[6:8])
        sum_scr = scr[8:10]
        i = pl.program_id(1)
        m0 = lax.broadcasted_iota(jnp.int32, (1, LANES), 1) < HEAD_DIM
        strict, upper, _ = _attn_masks(B)
        q = q_ref[...] * 0.125
        zq = jnp.zeros_like(q)
        qh = (jnp.where(m0, q, zq), jnp.where(m0, zq, q))

        def keys(j):
            start = pl.multiple_of(jnp.maximum(j, 0) * B, B)
            return k_ref[pl.ds(start, B), :]

        def vals(j):
            start = pl.multiple_of(jnp.maximum(j, 0) * B, B)
            return v_ref[pl.ds(start, B), :].astype(BF16)

        strips = [slice(s, min(s + ATTN_STRIP, B)) for s in range(0, B, ATTN_STRIP)]

        def scores(j):
            kj = keys(j)
            return tuple(_dot(qh[h], kj, NT) for h in (0, 1))

        def logits(zs2, slot, diag):
            for h in (0, 1):
                z = zs2[h]
                his, los = [], []
                for rows in strips:
                    zs = z[rows]
                    lb = jnp.minimum(zs, 0.0) - _softplus_neg_abs(zs)
                    l1 = lb - zs
                    if diag:
                        l1 = jnp.where(strict[rows], l1, 0.0)
                    lb_scr[slot][h][rows, :] = lb
                    hi = l1.astype(BF16)
                    his.append(hi)
                    los.append((l1 - hi.astype(F32)).astype(BF16))
                    rsum = jnp.sum(l1, axis=-1, keepdims=True)
                    if h == 0:
                        sum_scr[slot][rows, :] = jnp.broadcast_to(rsum, (rows.stop - rows.start, LANES))
                    else:
                        sum_scr[slot][rows, :] = jnp.where(m0, sum_scr[slot][rows, :], rsum)
                cat = lambda xs: jnp.concatenate(xs, axis=0)
                tail_scr[slot][h][...] = _dot(cat(his), upper) + _dot(cat(los), upper)

        def attend(j, slot, diag):
            vj = vals(j)
            pv = []
            for h in (0, 1):
                ws = []
                for rows in strips:
                    w = jnp.exp(lb_scr[slot][h][rows, :] + tail_scr[slot][h][rows, :])
                    if diag:
                        w = jnp.where(strict[rows], w, 0.0)
                    ws.append(w.astype(BF16))
                pv.append(_dot(jnp.concatenate(ws, axis=0), vj))
            return jnp.where(m0, pv[0], pv[1])

        logits(scores(i), 0, True)
        o_ref[...] = attend(i, 0, True)
        tot_ref[...] = sum_scr[0][...]

        def half(j, slot):
            z = scores(j - 1)
            pv = attend(j, slot, False)
            logits(z, 1 - slot, False)
            o_ref[...] += pv * jnp.exp(tot_ref[...])
            tot_ref[...] += sum_scr[slot][...]

        @pl.when(i > 0)
        def _():
            logits(scores(i - 1), 1, False)

            @pl.loop(0, (i + 1) // 2)
            def _(t):
                j = i - 1 - 2 * t
                half(j, 1)

                @pl.when(j > 0)
                def _():
                    half(j - 1, 0)


    qspec = pl.BlockSpec((B, LANES), lambda hp, i: (i, hp))
    kspec = pl.BlockSpec((L, LANES), lambda hp, i: (0, HP + hp))
    vspec = pl.BlockSpec((L, LANES), lambda hp, i: (0, 2 * HP + hp))
    ospec = pl.BlockSpec((B, LANES), lambda hp, i: (i, hp))
    shp = jax.ShapeDtypeStruct((L, AW), F32)
    return _pcall(body, name="attn_fwd", grid=(HP, nb), in_specs=[qspec, kspec, vspec],
                  out_specs=[ospec, ospec], out_shape=[shp, shp],
                  scratch=[pltpu.VMEM((B, B), F32)] * 8 + [pltpu.VMEM((B, LANES), F32)] * 2)(qk, qk, p)


def _attn_bwd(qk, p, do, tot, AW, B):
    L = qk.shape[0]
    HP = AW // LANES
    nb = L // B

    def body(q_ref, k_ref, v_ref, do_ref, tot_ref, dq_ref, dk_ref, dv_ref):
        i = pl.program_id(1)
        m0 = lax.broadcasted_iota(jnp.int32, (1, LANES), 1) < HEAD_DIM
        strict, upper, lower = _attn_masks(B)

        @pl.when(i == 0)
        def _():
            dk_ref[...] = jnp.zeros_like(dk_ref)
            dv_ref[...] = jnp.zeros_like(dv_ref)

        q = q_ref[...] * 0.125
        zq = jnp.zeros_like(q)
        qh = (jnp.where(m0, q, zq), jnp.where(m0, zq, q))
        dob = do_ref[...].astype(BF16)
        doh = (jnp.where(m0, dob, zq), jnp.where(m0, zq, dob))
        tv = tot_ref[...]
        th = (jnp.sum(jnp.where(m0, tv, 0.0), axis=-1, keepdims=True) / HEAD_DIM,
              jnp.sum(jnp.where(m0, 0.0, tv), axis=-1, keepdims=True) / HEAD_DIM)

        def block(j, carry, diag):
            pls, es, dq = carry
            start = pl.multiple_of(j * B, B)
            kj = k_ref[pl.ds(start, B), :]
            vj = v_ref[pl.ds(start, B), :].astype(BF16)
            zk = jnp.zeros_like(kj)
            kh = (jnp.where(m0, kj, zk), jnp.where(m0, zk, kj))
            new_pl, new_e = [], []
            dk_blk = jnp.zeros((B, LANES), F32)
            dv_blk = jnp.zeros((B, LANES), F32)
            for h in (0, 1):
                z = _dot(qh[h], kj, NT)
                sp = _softplus_neg_abs(z)
                lb = jnp.minimum(z, 0.0) - sp
                l1 = lb - z
                if diag:
                    l1 = jnp.where(strict, l1, 0.0)
                bsum = jnp.sum(l1, axis=-1, keepdims=True)
                right = th[h] - pls[h] - bsum
                tail = _split_dot(l1, upper)
                w = jnp.exp(lb + tail + right)
                if diag:
                    w = jnp.where(strict, w, 0.0)
                dw = _dot(doh[h], vj, NT)
                e = dw * w
                dl1 = _split_dot(e, lower) + es[h]
                beta = jnp.exp(lb)
                dz = e * (1.0 - beta) - dl1 * beta
                if diag:
                    dz = jnp.where(strict, dz, 0.0)
                dzb = dz.astype(BF16)
                wb = w.astype(BF16)
                dq = dq + _dot(dzb, kh[h])
                dk_blk = dk_blk + _dot(dzb, qh[h], TN)
                dv_blk = dv_blk + _dot(wb, doh[h], TN)
                new_pl.append(pls[h] + bsum)
                new_e.append(es[h] + jnp.sum(e, axis=-1, keepdims=True))
            dk_ref[pl.ds(start, B), :] += dk_blk
            dv_ref[pl.ds(start, B), :] += dv_blk
            return tuple(new_pl), tuple(new_e), dq

        zero = jnp.zeros((B, 1), F32)
        carry = ((zero, zero), (zero, zero), jnp.zeros((B, LANES), F32))
        carry = lax.fori_loop(0, i, lambda j, c: block(j, c, False), carry)
        _, _, dq = block(i, carry, True)
        dq_ref[...] = dq * 0.125

    qspec = pl.BlockSpec((B, LANES), lambda hp, i: (i, hp))
    kspec = pl.BlockSpec((L, LANES), lambda hp, i: (0, HP + hp))
    vspec = pl.BlockSpec((L, LANES), lambda hp, i: (0, 2 * HP + hp))
    full = pl.BlockSpec((L, LANES), lambda hp, i: (0, hp))
    shp = jax.ShapeDtypeStruct((L, AW), F32)
    return _pcall(body, name="attn_bwd", grid=(HP, nb), in_specs=[qspec, kspec, vspec, qspec, qspec],
                  out_specs=[qspec, full, full], out_shape=[shp, shp, shp])(qk, qk, p, do, tot)


def _s5_disc(ar, ai, ldt):
    dt = jnp.exp(ldt)
    mag = jnp.exp(dt * ar)
    abr = mag * jnp.cos(dt * ai)
    abi = mag * jnp.sin(dt * ai)
    emr = abr - 1.0
    emi = abi
    den = ar * ar + ai * ai
    fr = (emr * ar + emi * ai) / den
    fi = (emi * ar - emr * ai) / den
    return abr, abi, fr, fi


def _s5_params_math(ar, ai, ldt, br, bi):
    abr, abi, fr, fi = _s5_disc(ar, ai, ldt)
    return abr, abi, fr * br - fi * bi, fr * bi + fi * br


def _scan_rows(xr, xi, lev, T, reverse):
    row = lax.broadcasted_iota(jnp.int32, xr.shape, 0)
    d, k = 1, 0
    while d < T:
        kr, ki = lev(k)
        if reverse:
            keep = row < T - d
            sr = jnp.where(keep, pltpu.roll(xr, T - d, 0), 0.0)
            si = jnp.where(keep, pltpu.roll(xi, T - d, 0), 0.0)
            ki = -ki
        else:
            keep = row >= d
            sr = jnp.where(keep, pltpu.roll(xr, d, 0), 0.0)
            si = jnp.where(keep, pltpu.roll(xi, d, 0), 0.0)
        xr, xi = xr + (kr * sr - ki * si), xi + (kr * si + ki * sr)
        d, k = 2 * d, k + 1
    return xr, xi


def _s5_prep(ar, ai, ldt, braw_r, braw_i, T):
    NL, _, NS = ar.shape
    GT = NS // STATE_TILE
    nlev = max(1, int(math.log2(T)))
    LV = 8 * ((nlev + 7) // 8)

    def body(ar_ref, ai_ref, ldt_ref, br_ref, bi_ref,
             abr_ref, abi_ref, levr_ref, levi_ref, powr_ref, powi_ref, bsr_ref, bsi_ref):
        levr_ref[...] = jnp.zeros_like(levr_ref)
        levi_ref[...] = jnp.zeros_like(levi_ref)
        for t in range(GT):
            sl = slice(t * STATE_TILE, (t + 1) * STATE_TILE)
            abr, abi, bsr, bsi = _s5_params_math(ar_ref[:, sl], ai_ref[:, sl], ldt_ref[:, sl],
                                                 br_ref[t], bi_ref[t])
            abr_ref[:, sl] = abr
            abi_ref[:, sl] = abi
            bsr_ref[t] = bsr.astype(BF16)
            bsi_ref[t] = bsi.astype(BF16)
            kr, ki = abr, abi
            for k in range(nlev):
                levr_ref[k:k + 1, sl] = kr
                levi_ref[k:k + 1, sl] = ki
                kr, ki = kr * kr - ki * ki, 2.0 * kr * ki
        for s in range(NS // LANES):
            sl = slice(s * LANES, (s + 1) * LANES)
            row = lax.broadcasted_iota(jnp.int32, (T, LANES), 0)
            xr = jnp.where(row == 0, abr_ref[:, sl], 0.0)
            xi = jnp.where(row == 0, abi_ref[:, sl], 0.0)
            lev = lambda k, sl=sl: (levr_ref[k:k + 1, sl], levi_ref[k:k + 1, sl])
            xr, xi = _scan_rows(xr, xi, lev, T, False)
            powr_ref[:, sl] = xr
            powi_ref[:, sl] = xi

    rowspec = pl.BlockSpec((None, 1, NS), lambda l: (l, 0, 0))
    bspec = pl.BlockSpec((None, GT, LANES, STATE_TILE), lambda l: (l, 0, 0, 0))
    levspec = pl.BlockSpec((None, LV, NS), lambda l: (l, 0, 0))
    powspec = pl.BlockSpec((None, T, NS), lambda l: (l, 0, 0))
    rs = jax.ShapeDtypeStruct((NL, 1, NS), F32)
    ls = jax.ShapeDtypeStruct((NL, LV, NS), F32)
    ps = jax.ShapeDtypeStruct((NL, T, NS), F32)
    bs = jax.ShapeDtypeStruct((NL, GT, LANES, STATE_TILE), BF16)
    return _pcall(body, name="s5_prep", grid=(NL,), in_specs=[rowspec] * 3 + [bspec] * 2,
                  out_specs=[rowspec, rowspec, levspec, levspec, powspec, powspec, bspec, bspec],
                  out_shape=[rs, rs, ls, ls, ps, ps, bs, bs])(ar, ai, ldt, braw_r, braw_i)


def _s5_prep_bwd(ar, ai, ldt, braw_r, braw_i, dabr, dabi, dbsr, dbsi):
    NL, _, NS = ar.shape
    GT = NS // STATE_TILE

    def body(ar_ref, ai_ref, ldt_ref, br_ref, bi_ref, dabr_ref, dabi_ref, dbsr_ref, dbsi_ref,
             dar_ref, dai_ref, dldt_ref, dbr_ref, dbi_ref):
        for t in range(GT):
            sl = slice(t * STATE_TILE, (t + 1) * STATE_TILE)
            _, vjp = jax.vjp(_s5_params_math, ar_ref[:, sl], ai_ref[:, sl], ldt_ref[:, sl],
                             br_ref[t], bi_ref[t])
            dar, dai, dldt, dbr, dbi = vjp((dabr_ref[:, sl], dabi_ref[:, sl], dbsr_ref[t], dbsi_ref[t]))
            dar_ref[:, sl] = dar
            dai_ref[:, sl] = dai
            dldt_ref[:, sl] = dldt
            dbr_ref[t] = dbr
            dbi_ref[t] = dbi

    rowspec = pl.BlockSpec((None, 1, NS), lambda l: (l, 0, 0))
    bspec = pl.BlockSpec((None, GT, LANES, STATE_TILE), lambda l: (l, 0, 0, 0))
    rs = jax.ShapeDtypeStruct((NL, 1, NS), F32)
    bs = jax.ShapeDtypeStruct((NL, GT, LANES, STATE_TILE), F32)
    return _pcall(body, name="s5_prep_bwd", grid=(NL,), in_specs=[rowspec] * 3 + [bspec] * 2 + [rowspec] * 2 + [bspec] * 2,
                  out_specs=[rowspec] * 3 + [bspec] * 2, out_shape=[rs, rs, rs, bs, bs])(
                      ar, ai, ldt, braw_r, braw_i, dabr, dabi, dbsr, dbsi)


def _s5_specs(L, SW, T, ucol0, rev):
    GT = SW // LANES
    nc = L // T
    cidx = (lambda c: nc - 1 - c) if rev else (lambda c: c)
    return dict(
        GT=GT, nc=nc, cidx=cidx,
        u=pl.BlockSpec((T, LANES), lambda j, c: (cidx(c), ucol0 + j)),
        chan=pl.BlockSpec((T, LANES), lambda j, c: (cidx(c), j)),
        state=pl.BlockSpec((T, STATE_TILE), lambda j, c: (cidx(c), j)),
        bmat=pl.BlockSpec((None, LANES, STATE_TILE), lambda j, c: (j, 0, 0)),
        cmat=pl.BlockSpec((None, STATE_TILE, LANES), lambda j, c: (j, 0, 0)),
        gmat=pl.BlockSpec((None, LANES, LANES), lambda j, c: (j, 0, 0)),
        cvec=pl.BlockSpec((1, LANES), lambda j, c: (0, j)),
        svec=pl.BlockSpec((1, STATE_TILE), lambda j, c: (0, j)),
    )


def _s5_fwd(p, sp, SW, ucol0, T):
    L = p.shape[0]
    S = _s5_specs(L, SW, T, ucol0, False)
    NS = S["GT"] * STATE_TILE
    LV = sp["levr"].shape[0]

    def body(u_ref, bsr_ref, bsi_ref, levr_ref, levi_ref, powr_ref, powi_ref, cr_ref, ci_ref,
             d_ref, wg_ref, gb_ref, o_ref, sr_ref, si_ref, y_ref, carry_ref):
        @pl.when(pl.program_id(1) == 0)
        def _():
            carry_ref[...] = jnp.zeros_like(carry_ref)

        u = u_ref[...]
        ub = u.astype(BF16)
        bur = _dot(ub, bsr_ref[...])
        bui = _dot(ub, bsi_ref[...])
        for s in range(STATE_TILE // LANES):
            sl = slice(s * LANES, (s + 1) * LANES)
            lev = lambda k, sl=sl: (levr_ref[k:k + 1, sl], levi_ref[k:k + 1, sl])
            xr, xi = _scan_rows(bur[:, sl], bui[:, sl], lev, T, False)
            cr, ci = carry_ref[0:1, sl], carry_ref[1:2, sl]
            pr, pi = powr_ref[:, sl], powi_ref[:, sl]
            sr_ref[:, sl] = xr + (pr * cr - pi * ci)
            si_ref[:, sl] = xi + (pr * ci + pi * cr)
            carry_ref[0:1, sl] = sr_ref[T - 1:T, sl]
            carry_ref[1:2, sl] = si_ref[T - 1:T, sl]
        y = (_dot(sr_ref[...].astype(BF16), cr_ref[...]) - _dot(si_ref[...].astype(BF16), ci_ref[...])
             + d_ref[...] * u)
        y_ref[...] = y
        yg = jax.nn.gelu(y)
        gate = jax.nn.sigmoid(_dot(yg.astype(BF16), wg_ref[...]) + gb_ref[...])
        o_ref[...] = yg * gate

    lvspec = pl.BlockSpec((LV, STATE_TILE), lambda j, c: (0, j))
    pwspec = pl.BlockSpec((T, STATE_TILE), lambda j, c: (0, j))
    cs = jax.ShapeDtypeStruct((L, SW), F32)
    ss = jax.ShapeDtypeStruct((L, NS), F32)
    return _pcall(
        body, name="s5_fwd", grid=(S["GT"], S["nc"]),
        in_specs=[S["u"], S["bmat"], S["bmat"], lvspec, lvspec, pwspec, pwspec, S["cmat"], S["cmat"],
                  S["cvec"], S["gmat"], S["cvec"]],
        out_specs=[S["chan"], S["state"], S["state"], S["chan"]], out_shape=[cs, ss, ss, cs],
        scratch=[pltpu.VMEM((8, STATE_TILE), F32)],
    )(p, sp["bsr"], sp["bsi"], sp["levr"], sp["levi"], sp["powr"], sp["powi"], sp["crT"], sp["ciT"],
      sp["d"], sp["wg"], sp["gb"])


def _s5_bwd(do, p, s_re, s_im, ypre, sp, SW, ucol0, T):
    L = p.shape[0]
    S = _s5_specs(L, SW, T, ucol0, True)
    GT, nc, cidx = S["GT"], S["nc"], S["cidx"]
    NS = GT * STATE_TILE
    LV = sp["levr"].shape[0]
    T8 = T // 8

    def body(do_ref, u_ref, sr_ref, si_ref, hr_ref, hi_ref, y_ref, bsr_ref, bsi_ref, levr_ref, levi_ref,
             rpr_ref, rpi_ref, cr_ref, ci_ref, d_ref, wg_ref, gb_ref,
             du_ref, dbsr_ref, dbsi_ref, dcr_ref, dci_ref, dd_ref, dwg_ref, dgb_ref, dar_ref, dai_ref,
             carry_ref, lam_r, lam_i):
        c = pl.program_id(1)
        first = c == 0

        @pl.when(first)
        def _():
            carry_ref[...] = jnp.zeros_like(carry_ref)

        u = u_ref[...]
        ub = u.astype(BF16)
        y = y_ref[...]
        yg, gelu_vjp = jax.vjp(jax.nn.gelu, y)
        ygb = yg.astype(BF16)
        gate = jax.nn.sigmoid(_dot(ygb, wg_ref[...]) + gb_ref[...])
        dout = do_ref[...]
        dt = dout * yg * gate * (1.0 - gate)
        dtb = dt.astype(BF16)
        dyg = dout * gate + _dot(dtb, wg_ref[...], NT)
        (dy,) = gelu_vjp(dyg)
        dyb = dy.astype(BF16)
        _accumulate(dwg_ref, _dot(ygb, dtb, TN), first)
        _accumulate(dgb_ref, jnp.sum(dt, axis=0, keepdims=True), first)
        _accumulate(dd_ref, jnp.sum(dy * u, axis=0, keepdims=True), first)
        srb = sr_ref[...].astype(BF16)
        sib = si_ref[...].astype(BF16)
        _accumulate(dcr_ref, _dot(srb, dyb, TN), first)
        _accumulate(dci_ref, -_dot(sib, dyb, TN), first)
        dsr = _dot(dyb, cr_ref[...], NT)
        dsi = -_dot(dyb, ci_ref[...], NT)
        last_chunk = cidx(c) == 0
        row = lax.broadcasted_iota(jnp.int32, (T, LANES), 0)
        for s in range(STATE_TILE // LANES):
            sl = slice(s * LANES, (s + 1) * LANES)
            lev = lambda k, sl=sl: (levr_ref[k:k + 1, sl], levi_ref[k:k + 1, sl])
            xr, xi = _scan_rows(dsr[:, sl], dsi[:, sl], lev, T, True)
            cr, ci = carry_ref[0:1, sl], carry_ref[1:2, sl]
            pr, pi = rpr_ref[:, sl], -rpi_ref[:, sl]
            lam_r[:, sl] = xr + (pr * cr - pi * ci)
            lam_i[:, sl] = xi + (pr * ci + pi * cr)
            carry_ref[0:1, sl] = lam_r[0:1, sl]
            carry_ref[1:2, sl] = lam_i[0:1, sl]
            keep = jnp.logical_not(last_chunk)
            pr0 = jnp.where(keep, hr_ref[7:8, sl], 0.0)
            pi0 = jnp.where(keep, hi_ref[7:8, sl], 0.0)
            spr = jnp.where(row == 0, pr0, pltpu.roll(sr_ref[:, sl], 1, 0))
            spi = jnp.where(row == 0, pi0, pltpu.roll(si_ref[:, sl], 1, 0))
            lr, li = lam_r[:, sl], lam_i[:, sl]
            dar = jnp.sum(lr * spr + li * spi, axis=0, keepdims=True)
            dai = jnp.sum(li * spr - lr * spi, axis=0, keepdims=True)

            @pl.when(first)
            def _():
                dar_ref[:, sl] = dar
                dai_ref[:, sl] = dai

            @pl.when(jnp.logical_not(first))
            def _():
                dar_ref[:, sl] += dar
                dai_ref[:, sl] += dai

        lrb = lam_r[...].astype(BF16)
        lib = lam_i[...].astype(BF16)
        _accumulate(dbsr_ref, _dot(ub, lrb, TN), first)
        _accumulate(dbsi_ref, _dot(ub, lib, TN), first)
        du_ref[...] = dy * d_ref[...] + _dot(lrb, bsr_ref[...], NT) + _dot(lib, bsi_ref[...], NT)

    halo = pl.BlockSpec((8, STATE_TILE), lambda j, c: (jnp.maximum(cidx(c) * T8 - 1, 0), j))
    lvspec = pl.BlockSpec((LV, STATE_TILE), lambda j, c: (0, j))
    pwspec = pl.BlockSpec((T, STATE_TILE), lambda j, c: (0, j))
    f = lambda *s: jax.ShapeDtypeStruct(s, F32)
    return _pcall(
        body, name="s5_bwd", grid=(GT, nc),
        in_specs=[S["chan"], S["u"], S["state"], S["state"], halo, halo, S["chan"], S["bmat"], S["bmat"],
                  lvspec, lvspec, pwspec, pwspec, S["cmat"], S["cmat"], S["cvec"], S["gmat"], S["cvec"]],
        out_specs=[S["chan"], S["bmat"], S["bmat"], S["cmat"], S["cmat"], S["cvec"], S["gmat"], S["cvec"],
                   S["svec"], S["svec"]],
        out_shape=[f(L, SW), f(GT, LANES, STATE_TILE), f(GT, LANES, STATE_TILE), f(GT, STATE_TILE, LANES),
                   f(GT, STATE_TILE, LANES), f(1, SW), f(GT, LANES, LANES), f(1, SW), f(1, NS), f(1, NS)],
        scratch=[pltpu.VMEM((8, STATE_TILE), F32), pltpu.VMEM((T, STATE_TILE), F32),
                 pltpu.VMEM((T, STATE_TILE), F32)],
    )(do, p, s_re, s_im, s_re, s_im, ypre, sp["bsr"], sp["bsi"], sp["levr"], sp["levi"],
      sp["rpowr"], sp["rpowi"], sp["crT"], sp["ciT"], sp["d"], sp["wg"], sp["gb"])


def _conv_taps(xc, h6, h7, row):
    x1 = jnp.where(row == 0, h7, pltpu.roll(xc, 1, 0))
    x2 = jnp.where(row == 0, h6, jnp.where(row == 1, h7, pltpu.roll(xc, 2, 0)))
    return x1, x2


def _conv_halves(up_ref, halo_ref, w_ref, b_ref, tm, FS):
    first = pl.program_id(1) == 0
    row = lax.broadcasted_iota(jnp.int32, (tm, FS), 0)
    outs, taps = [], []
    for s in (0, 1):
        xc = up_ref[s]
        h6 = jnp.where(first, 0.0, halo_ref[s, 6:7, :])
        h7 = jnp.where(first, 0.0, halo_ref[s, 7:8, :])
        x1, x2 = _conv_taps(xc, h6, h7, row)
        outs.append(b_ref[s] + x2 * w_ref[s, 0:1, :] + x1 * w_ref[s, 1:2, :] + xc * w_ref[s, 2:3, :])
        taps.append((x2, x1, xc))
    return outs, taps


def _convglu_specs(L, FS, tm):
    t8 = tm // 8
    return dict(
        up=pl.BlockSpec((2, None, tm, FS), lambda j, i: (0, j, i, 0)),
        halo=pl.BlockSpec((2, None, 8, FS), lambda j, i: (0, j, jnp.maximum(i * t8 - 1, 0), 0)),
        w=pl.BlockSpec((2, None, 3, FS), lambda j, i: (0, j, 0, 0)),
        b=pl.BlockSpec((2, None, 1, FS), lambda j, i: (0, j, 0, 0)),
        act=pl.BlockSpec((None, tm, FS), lambda j, i: (j, i, 0)),
    )


def _convglu_fwd(up, w, b):
    _, NSH, L, FS = up.shape
    tm = _row_tile(L)
    S = _convglu_specs(L, FS, tm)

    def body(up_ref, halo_ref, w_ref, b_ref, act_ref):
        (val, gate), _ = _conv_halves(up_ref, halo_ref, w_ref, b_ref, tm, FS)
        act_ref[...] = _glu_math(val, gate).astype(BF16)

    return _pcall(body, name="convglu_fwd", grid=(NSH, L // tm), in_specs=[S["up"], S["halo"], S["w"], S["b"]],
                  out_specs=S["act"], out_shape=jax.ShapeDtypeStruct((NSH, L, FS), BF16))(up, up, w, b)


def _convglu_bwd(up, dact, w, b):
    _, NSH, L, FS = up.shape
    tm = _row_tile(L)
    S = _convglu_specs(L, FS, tm)

    def body(up_ref, halo_ref, w_ref, b_ref, dact_ref, dc_ref, dw_ref, db_ref):
        first = pl.program_id(1) == 0
        (val, gate), taps = _conv_halves(up_ref, halo_ref, w_ref, b_ref, tm, FS)
        _, vjp = jax.vjp(_glu_math, val, gate)
        dcs = vjp(dact_ref[...])
        for s in (0, 1):
            dc = dcs[s]
            dc_ref[s] = dc
            sums = [jnp.sum(dc * t, axis=0, keepdims=True) for t in taps[s]]
            dbs = jnp.sum(dc, axis=0, keepdims=True)

            @pl.when(first)
            def _():
                for t in range(3):
                    dw_ref[s, t:t + 1, :] = sums[t]
                db_ref[s] = dbs

            @pl.when(jnp.logical_not(first))
            def _():
                for t in range(3):
                    dw_ref[s, t:t + 1, :] += sums[t]
                db_ref[s] += dbs

    f = lambda *s: jax.ShapeDtypeStruct(s, F32)
    return _pcall(body, name="convglu_bwd", grid=(NSH, L // tm),
                  in_specs=[S["up"], S["halo"], S["w"], S["b"], S["act"]],
                  out_specs=[S["up"], S["w"], S["b"]],
                  out_shape=[f(2, NSH, L, FS), f(2, NSH, 3, FS), f(2, NSH, 1, FS)])(up, up, w, b, dact)


def _conv_transpose(dc, w):
    NS8, L, FS = dc.shape
    tm = _row_tile(L)
    t8 = tm // 8
    nt = L // tm

    def body(dc_ref, nxt_ref, w_ref, o_ref):
        last = pl.program_id(1) == nt - 1
        row = lax.broadcasted_iota(jnp.int32, (tm, FS), 0)
        xc = dc_ref[...]
        n0 = jnp.where(last, 0.0, nxt_ref[0:1, :])
        n1 = jnp.where(last, 0.0, nxt_ref[1:2, :])
        x1 = jnp.where(row == tm - 1, n0, pltpu.roll(xc, tm - 1, 0))
        x2 = jnp.where(row == tm - 1, n1, jnp.where(row == tm - 2, n0, pltpu.roll(xc, tm - 2, 0)))
        o_ref[...] = (xc * w_ref[2:3, :] + x1 * w_ref[1:2, :] + x2 * w_ref[0:1, :]).astype(BF16)

    blk = pl.BlockSpec((None, tm, FS), lambda j, i: (j, i, 0))
    nxt = pl.BlockSpec((None, 8, FS), lambda j, i: (j, jnp.minimum((i + 1) * t8, L // 8 - 1), 0))
    wsp = pl.BlockSpec((None, 3, FS), lambda j, i: (j, 0, 0))
    return _pcall(body, name="conv_transpose", grid=(NS8, nt), in_specs=[blk, nxt, wsp], out_specs=blk,
                  out_shape=jax.ShapeDtypeStruct((NS8, L, FS), BF16))(dc, dc, w)


def _loss_head(y, target):
    L, D = y.shape
    tm = _row_tile(L)

    def body(y_ref, t_ref, loss_ref, dy_ref):
        err = y_ref[...] - t_ref[...]
        dy_ref[...] = err / D
        part = 0.5 * jnp.sum(jnp.mean(err * err, axis=-1, keepdims=True), axis=0, keepdims=True)
        _accumulate(loss_ref, jnp.broadcast_to(part, (1, LANES)), pl.program_id(0) == 0)

    row = pl.BlockSpec((tm, D), lambda i: (i, 0))
    vec = pl.BlockSpec((1, LANES), lambda i: (0, 0))
    return _pcall(body, name="loss_head", grid=(L // tm,), in_specs=[row, row], out_specs=[vec, row],
                  out_shape=[jax.ShapeDtypeStruct((1, LANES), F32), jax.ShapeDtypeStruct((L, D), F32)])(y, target)


def _ada_fwd(c_all, ada_w, ada_b):
    NL, D, NC = ada_w.shape
    NB = c_all.shape[0]

    def body(c_ref, w_ref, b_ref, o_ref):
        cact = jax.nn.silu(c_ref[...])
        o_ref[...] = _dot(cact.astype(BF16), w_ref[...].astype(BF16)) + b_ref[...]

    return _pcall(body, name="ada_fwd", grid=(NL,),
                  in_specs=[pl.BlockSpec((NB, D), lambda l: (0, 0)), pl.BlockSpec((None, D, NC), lambda l: (l, 0, 0)),
                            pl.BlockSpec((None, 1, NC), lambda l: (l, 0, 0))],
                  out_specs=pl.BlockSpec((None, NB, NC), lambda l: (l, 0, 0)),
                  out_shape=jax.ShapeDtypeStruct((NL, NB, NC), F32))(c_all, ada_w, ada_b)


def _ada_bwd(c_all_t, dmod):
    D, NB = c_all_t.shape
    NL, _, NC = dmod.shape

    def body(c_ref, d_ref, o_ref):
        cact = jax.nn.silu(c_ref[...]).astype(BF16).astype(F32)
        o_ref[...] = _dot(cact, d_ref[...].astype(BF16).astype(F32))

    return _pcall(body, name="ada_bwd", grid=(NL,),
                  in_specs=[pl.BlockSpec((D, NB), lambda l: (0, 0)), pl.BlockSpec((None, NB, NC), lambda l: (l, 0, 0))],
                  out_specs=pl.BlockSpec((None, D, NC), lambda l: (l, 0, 0)),
                  out_shape=jax.ShapeDtypeStruct((NL, D, NC), F32))(c_all_t, dmod)


def _adamw(parts, w, m, v):
    NL, P, R, C = parts.shape
    tr = R
    for cand in (256, 128, 64, 32, 16, 8):
        if R % cand == 0 and R > cand:
            tr = cand
            break

    def body(p_ref, w_ref, m_ref, v_ref, g_ref, d_ref, nm_ref, nv_ref):
        g = p_ref[0].astype(F32)
        for k in range(1, P):
            g = g + p_ref[k].astype(F32)
        m2 = ADAM_B1 * m_ref[...] + (1.0 - ADAM_B1) * g
        v2 = ADAM_B2 * v_ref[...] + (1.0 - ADAM_B2) * jnp.square(g)
        m_hat = m2 / (1.0 - ADAM_B1 ** ADAM_STEP)
        v_hat = v2 / (1.0 - ADAM_B2 ** ADAM_STEP)
        g_ref[...] = g
        d_ref[...] = -ADAM_LR * (m_hat / (jnp.sqrt(v_hat) + ADAM_EPS) + ADAM_WD * w_ref[...])
        nm_ref[...] = m2
        nv_ref[...] = v2

    pspec = pl.BlockSpec((None, P, tr, C), lambda l, i: (l, 0, i, 0))
    wspec = pl.BlockSpec((None, tr, C), lambda l, i: (l, i, 0))
    shp = jax.ShapeDtypeStruct((NL, R, C), F32)
    return _pcall(body, name="adamw", grid=(NL, R // tr), in_specs=[pspec, wspec, wspec, wspec],
                  out_specs=[wspec] * 4, out_shape=[shp] * 4)(parts, w, m, v)


def _block_diag(blocks):
    *lead, g, r, c = blocks.shape
    eye = jnp.eye(g, dtype=bool)[:, None, :, None]
    full = jnp.where(eye, blocks[..., :, :, None, :], 0.0)
    return full.reshape(*lead, g * r, g * c)


def _block_diag_extract(m, r, c):
    g = GROUPS_PER_TILE
    m5 = m.reshape(*m.shape[:-2], g, r, g, c)
    eye = jnp.eye(g, dtype=bool)[:, None, :, None]
    return jnp.sum(jnp.where(eye, m5, 0.0), axis=-2)


def kernel(x, c, ada_w, ada_b, norm1_g, w_in, q_norm_g, k_norm_g, ssm_a_re, ssm_a_im, ssm_log_dt, ssm_b_re, ssm_b_im, ssm_c_re, ssm_c_im, ssm_d, glu_w, glu_b, attn_out_g, ssm_out_g, w_out, norm2_g, ffn_w_up, ffn_conv_w, ffn_conv_b, ffn_w_down, loss_target, m_ada_w, m_ada_b, m_norm1_g, m_w_in, m_q_norm_g, m_k_norm_g, m_ssm_a_re, m_ssm_a_im, m_ssm_log_dt, m_ssm_b_re, m_ssm_b_im, m_ssm_c_re, m_ssm_c_im, m_ssm_d, m_glu_w, m_glu_b, m_attn_out_g, m_ssm_out_g, m_w_out, m_norm2_g, m_ffn_w_up, m_ffn_conv_w, m_ffn_conv_b, m_ffn_w_down, v_ada_w, v_ada_b, v_norm1_g, v_w_in, v_q_norm_g, v_k_norm_g, v_ssm_a_re, v_ssm_a_im, v_ssm_log_dt, v_ssm_b_re, v_ssm_b_im, v_ssm_c_re, v_ssm_c_im, v_ssm_d, v_glu_w, v_glu_b, v_attn_out_g, v_ssm_out_g, v_w_out, v_norm2_g, v_ffn_w_up, v_ffn_conv_w, v_ffn_conv_b, v_ffn_w_down):
    weights = dict(ada_w=ada_w, ada_b=ada_b, norm1_g=norm1_g, w_in=w_in, q_norm_g=q_norm_g, k_norm_g=k_norm_g,
                   ssm_a_re=ssm_a_re, ssm_a_im=ssm_a_im, ssm_log_dt=ssm_log_dt, ssm_b_re=ssm_b_re,
                   ssm_b_im=ssm_b_im, ssm_c_re=ssm_c_re, ssm_c_im=ssm_c_im, ssm_d=ssm_d, glu_w=glu_w, glu_b=glu_b,
                   attn_out_g=attn_out_g, ssm_out_g=ssm_out_g, w_out=w_out, norm2_g=norm2_g, ffn_w_up=ffn_w_up,
                   ffn_conv_w=ffn_conv_w, ffn_conv_b=ffn_conv_b, ffn_w_down=ffn_w_down)
    mom_m = dict(ada_w=m_ada_w, ada_b=m_ada_b, norm1_g=m_norm1_g, w_in=m_w_in, q_norm_g=m_q_norm_g,
                 k_norm_g=m_k_norm_g, ssm_a_re=m_ssm_a_re, ssm_a_im=m_ssm_a_im, ssm_log_dt=m_ssm_log_dt,
                 ssm_b_re=m_ssm_b_re, ssm_b_im=m_ssm_b_im, ssm_c_re=m_ssm_c_re, ssm_c_im=m_ssm_c_im, ssm_d=m_ssm_d,
                 glu_w=m_glu_w, glu_b=m_glu_b, attn_out_g=m_attn_out_g, ssm_out_g=m_ssm_out_g, w_out=m_w_out,
                 norm2_g=m_norm2_g, ffn_w_up=m_ffn_w_up, ffn_conv_w=m_ffn_conv_w, ffn_conv_b=m_ffn_conv_b,
                 ffn_w_down=m_ffn_w_down)
    mom_v = dict(ada_w=v_ada_w, ada_b=v_ada_b, norm1_g=v_norm1_g, w_in=v_w_in, q_norm_g=v_q_norm_g,
                 k_norm_g=v_k_norm_g, ssm_a_re=v_ssm_a_re, ssm_a_im=v_ssm_a_im, ssm_log_dt=v_ssm_log_dt,
                 ssm_b_re=v_ssm_b_re, ssm_b_im=v_ssm_b_im, ssm_c_re=v_ssm_c_re, ssm_c_im=v_ssm_c_im, ssm_d=v_ssm_d,
                 glu_w=v_glu_w, glu_b=v_glu_b, attn_out_g=v_attn_out_g, ssm_out_g=v_ssm_out_g, w_out=v_w_out,
                 norm2_g=v_norm2_g, ffn_w_up=v_ffn_w_up, ffn_conv_w=v_ffn_conv_w, ffn_conv_b=v_ffn_conv_b,
                 ffn_w_down=v_ffn_w_down)
    names = list(weights)
    big = ("ada_w", "w_in", "w_out", "ffn_w_up", "ffn_conv_w", "ffn_w_down")
    small = [n for n in names if n not in big]

    x = x[0]
    target = loss_target[0]
    L, D = x.shape
    NL = ada_w.shape[0]
    AW = D // 2
    SW = D - AW
    NH = AW // HEAD_DIM
    HP = AW // LANES
    G = SW // SSM_GROUP
    GT = SW // LANES
    NS = G * SSM_STATE
    NIN = w_in.shape[-1]
    FS = ffn_w_up.shape[-1]
    NSH = NDEV // 2
    NCA = ada_w.shape[-1]
    ROWS_OUT = w_out.shape[1]
    ROWS_DOWN = ffn_w_down.shape[1]
    B_ATT = min(L, 256)
    T_S5 = min(L, 128)
    tm = _row_tile(L, 1024)
    me = _my_index()

    cpad = jnp.reshape(c, (D // LANES, LANES))
    c_all = _gather_small(cpad, "gather_c")
    c_all = c_all.reshape(NDEV, D)
    w_in_g, w_out_g, w_up_g, conv_w_g, w_down_g = _gather_shards(
        [w_in.astype(BF16), w_out.astype(BF16), ffn_w_up.astype(BF16), ffn_conv_w, ffn_w_down.astype(BF16)],
        "gather_weights")
    w_out_g = w_out_g.reshape(NL, D, D)
    w_down_g = w_down_g.reshape(NL, NSH, 2 * ROWS_DOWN, D)
    conv_w_g = conv_w_g.reshape(NL, 2, NSH, 3, FS)
    conv_b_g = ffn_conv_b.reshape(NL, 2, NSH, 1, FS)

    ada_b_mine = lax.dynamic_slice_in_dim(ada_b, me * NCA, NCA, axis=1).reshape(NL, 1, NCA)
    mod_part = _ada_fwd(c_all, ada_w, ada_b_mine)
    mod_all = _gather_small(mod_part, "gather_mod")
    mod = lax.dynamic_index_in_dim(mod_all, me, axis=2, keepdims=False)
    mod = jnp.transpose(mod, (1, 0, 2)).reshape(NL, N_MOD, 1, D)

    row = lambda a: a.reshape(NL, 1, NS)
    ar_row, ai_row = row(ssm_a_re), row(ssm_a_im)
    ldt_row = row(jnp.broadcast_to(ssm_log_dt[:, :, None], (NL, G, SSM_STATE)))
    tiles = lambda a: a.reshape((NL, GT, GROUPS_PER_TILE) + a.shape[2:])
    braw_r = _block_diag(jnp.swapaxes(tiles(ssm_b_re), -1, -2))
    braw_i = _block_diag(jnp.swapaxes(tiles(ssm_b_im), -1, -2))
    crT = _block_diag(jnp.swapaxes(tiles(ssm_c_re), -1, -2)).astype(BF16)
    ciT = _block_diag(jnp.swapaxes(tiles(ssm_c_im), -1, -2)).astype(BF16)
    wg = _block_diag(tiles(glu_w)).astype(BF16)
    abr, abi, levr, levi, powr, powi, bsr, bsi = _s5_prep(ar_row, ai_row, ldt_row, braw_r, braw_i, T_S5)
    s5p = dict(bsr=bsr, bsi=bsi, levr=levr, levi=levi, powr=powr, powi=powi,
               rpowr=jnp.flip(powr, axis=1), rpowi=jnp.flip(powi, axis=1), crT=crT, ciT=ciT,
               d=ssm_d.reshape(NL, 1, SW), wg=wg, gb=glu_b.reshape(NL, 1, SW))

    gqk = jnp.concatenate([jnp.tile(q_norm_g, (1, NH)), jnp.tile(k_norm_g, (1, NH))], axis=1).reshape(NL, 1, 2 * AW)
    layer_params = dict(
        mod=mod, norm1_g=norm1_g.reshape(NL, 1, D), norm2_g=norm2_g.reshape(NL, 1, D), gqk=gqk,
        ga=attn_out_g.reshape(NL, 1, AW), gs=ssm_out_g.reshape(NL, 1, SW),
        w_in=w_in_g, w_out=w_out_g, w_up=w_up_g, conv_w=conv_w_g, conv_b=conv_b_g, w_down=w_down_g, s5=s5p)

    ucol0 = 3 * AW // LANES

    def mm_rows_shards(name, a, b, n, out_dtype=F32):
        K = a.shape[1]
        return _matmul(name, a, b, dims=NN, grid=(L // tm, NDEV),
                       a_spec=pl.BlockSpec((tm, K), lambda i, j: (i, 0)),
                       b_spec=pl.BlockSpec((None, K, n), lambda i, j: (j, 0, 0)),
                       out_shape=jax.ShapeDtypeStruct((L, NDEV * n), out_dtype),
                       out_specs=pl.BlockSpec((tm, n), lambda i, j: (i, j)))

    tn = min(D, 512)

    def resid_epilogue(acc, xres, gate):
        return acc, xres + gate * acc

    def layer_fwd(xin, lp):
        sh1, sc1, g1, sh2, sc2, g2 = (lp["mod"][k] for k in range(N_MOD))
        h = _lnmod_fwd(xin, lp["norm1_g"], sh1, sc1)
        p = mm_rows_shards("mm_in", h, lp["w_in"], NIN)
        qk = _qknorm_fwd(p, lp["gqk"], AW)
        o_attn, tot = _attn_fwd(qk, p, AW, B_ATT)
        o_ssm, s_re, s_im, ypre = _s5_fwd(p, lp["s5"], SW, ucol0, T_S5)
        o = _outnorm_fwd(o_attn, o_ssm, lp["ga"], lp["gs"])
        a1, x_mid = _matmul(
            "mm_out", o, lp["w_out"], dims=NN, grid=(L // tm, D // tn),
            a_spec=pl.BlockSpec((tm, D), lambda i, j: (i, 0)), b_spec=pl.BlockSpec((D, tn), lambda i, j: (0, j)),
            extra=(xin, g1), extra_specs=(pl.BlockSpec((tm, tn), lambda i, j: (i, j)),
                                          pl.BlockSpec((1, tn), lambda i, j: (0, j))),
            epilogue=resid_epilogue,
            out_shape=[jax.ShapeDtypeStruct((L, D), F32)] * 2,
            out_specs=[pl.BlockSpec((tm, tn), lambda i, j: (i, j))] * 2)
        h2 = _lnmod_fwd(x_mid, lp["norm2_g"], sh2, sc2)
        up = _matmul(
            "mm_up", h2, lp["w_up"], dims=NN, grid=(L // tm, NDEV),
            a_spec=pl.BlockSpec((tm, D), lambda i, j: (i, 0)), b_spec=pl.BlockSpec((None, D, FS), lambda i, j: (j, 0, 0)),
            out_shape=jax.ShapeDtypeStruct((NDEV, L, FS), F32),
            out_specs=pl.BlockSpec((None, tm, FS), lambda i, j: (j, i, 0)))
        up = up.reshape(2, NSH, L, FS)
        act = _convglu_fwd(up, lp["conv_w"], lp["conv_b"])
        a2, x_out = _matmul(
            "mm_down", act, lp["w_down"], dims=NN, grid=(L // tm, D // tn, NSH), kaxis=2, acc_shape=(tm, tn),
            a_spec=pl.BlockSpec((None, tm, FS), lambda i, j, k: (k, i, 0)),
            b_spec=pl.BlockSpec((None, FS, tn), lambda i, j, k: (k, 0, j)),
            extra=(x_mid, g2), extra_specs=(pl.BlockSpec((tm, tn), lambda i, j, k: (i, j)),
                                            pl.BlockSpec((1, tn), lambda i, j, k: (0, j))),
            epilogue=resid_epilogue,
            out_shape=[jax.ShapeDtypeStruct((L, D), F32)] * 2,
            out_specs=[pl.BlockSpec((tm, tn), lambda i, j, k: (i, j))] * 2)
        res = dict(x=xin, h=h, p=p, qk=qk, tot=tot, o_attn=o_attn, o_ssm=o_ssm, s_re=s_re, s_im=s_im, ypre=ypre,
                   o=o, a1=a1, x_mid=x_mid, h2=h2, up=up, act=act, a2=a2)
        return x_out, res

    per_layer = [jax.tree.map(lambda a: a[l], layer_params) for l in range(NL)]
    y, residuals = x, []
    for l in range(NL):
        y, res = layer_fwd(y, per_layer[l])
        residuals.append(res)

    loss_row, dy = _loss_head(y, target)
    loss = lax.psum(loss_row[0, 0], ("x", "y", "c"))

    def layer_bwd(dx, args):
        lp, r = args
        sh1, sc1, g1, sh2, sc2, g2 = (lp["mod"][k] for k in range(N_MOD))
        da2, dg2 = _gate_bwd(dx, r["a2"], g2)
        dact = _matmul(
            "mm_dact", da2, lp["w_down"], dims=NT, grid=(L // tm, NSH),
            a_spec=pl.BlockSpec((tm, D), lambda i, j: (i, 0)), b_spec=pl.BlockSpec((None, FS, D), lambda i, j: (j, 0, 0)),
            out_shape=jax.ShapeDtypeStruct((NSH, L, FS), F32),
            out_specs=pl.BlockSpec((None, tm, FS), lambda i, j: (j, i, 0)))
        dw_down = _matmul(
            "mm_dw_down", r["act"], da2, dims=TN, grid=(NSH, D // tn, L // tm), kaxis=2, acc_shape=(FS, tn),
            a_spec=pl.BlockSpec((None, tm, FS), lambda j, n, k: (j, k, 0)),
            b_spec=pl.BlockSpec((tm, tn), lambda j, n, k: (k, n)),
            out_shape=jax.ShapeDtypeStruct((NSH, FS, D), BF16),
            out_specs=pl.BlockSpec((None, FS, tn), lambda j, n, k: (j, 0, n)))
        dconv, dcw, dcb = _convglu_bwd(r["up"], dact, lp["conv_w"], lp["conv_b"])
        dup = _conv_transpose(dconv.reshape(NDEV, L, FS), lp["conv_w"].reshape(NDEV, 3, FS))
        dh2 = _matmul(
            "mm_dh2", dup, lp["w_up"], dims=NT, grid=(L // tm, NDEV), kaxis=1, acc_shape=(tm, D),
            a_spec=pl.BlockSpec((None, tm, FS), lambda i, k: (k, i, 0)),
            b_spec=pl.BlockSpec((None, D, FS), lambda i, k: (k, 0, 0)),
            out_shape=jax.ShapeDtypeStruct((L, D), F32), out_specs=pl.BlockSpec((tm, D), lambda i, k: (i, 0)))
        dw_up = _matmul(
            "mm_dw_up", r["h2"], dup, dims=TN, grid=(NDEV, L // tm), kaxis=1, acc_shape=(D, FS),
            a_spec=pl.BlockSpec((tm, D), lambda j, k: (k, 0)), b_spec=pl.BlockSpec((None, tm, FS), lambda j, k: (j, k, 0)),
            out_shape=jax.ShapeDtypeStruct((NDEV, D, FS), BF16),
            out_specs=pl.BlockSpec((None, D, FS), lambda j, k: (j, 0, 0)))
        dxm, dn2, dsh2, dsc2 = _lnmod_bwd(dh2, r["x_mid"], lp["norm2_g"], sh2, sc2, dx)
        da1, dg1 = _gate_bwd(dxm, r["a1"], g1)
        do = _matmul(
            "mm_do", da1, lp["w_out"], dims=NT, grid=(L // tm, D // tn),
            a_spec=pl.BlockSpec((tm, D), lambda i, j: (i, 0)), b_spec=pl.BlockSpec((tn, D), lambda i, j: (j, 0)),
            out_shape=jax.ShapeDtypeStruct((L, D), F32), out_specs=pl.BlockSpec((tm, tn), lambda i, j: (i, j)))
        dw_out = _matmul(
            "mm_dw_out", r["o"], da1, dims=TN, grid=(D // tn, D // tn, L // tm), kaxis=2, acc_shape=(tn, tn),
            a_spec=pl.BlockSpec((tm, tn), lambda m, n, k: (k, m)), b_spec=pl.BlockSpec((tm, tn), lambda m, n, k: (k, n)),
            out_shape=jax.ShapeDtypeStruct((D, D), BF16), out_specs=pl.BlockSpec((tn, tn), lambda m, n, k: (m, n)))
        doa, dos, dga, dgs = _outnorm_bwd(do, r["o_attn"], r["o_ssm"], lp["ga"], lp["gs"])
        (du, dbsr, dbsi, dcr, dci, dd, dwg, dgb, dabr, dabi) = _s5_bwd(
            dos, r["p"], r["s_re"], r["s_im"], r["ypre"], lp["s5"], SW, ucol0, T_S5)
        dq, dk, dv = _attn_bwd(r["qk"], r["p"], doa, r["tot"], AW, B_ATT)
        dqk, dgqk = _qknorm_bwd(jnp.concatenate([dq, dk], axis=1), r["p"], lp["gqk"], AW)
        dp = jnp.concatenate([dqk, dv, du], axis=1).astype(BF16)
        dh = _matmul(
            "mm_dh", dp, lp["w_in"], dims=NT, grid=(L // tm, NDEV), kaxis=1, acc_shape=(tm, D),
            a_spec=pl.BlockSpec((tm, NIN), lambda i, k: (i, k)), b_spec=pl.BlockSpec((None, D, NIN), lambda i, k: (k, 0, 0)),
            out_shape=jax.ShapeDtypeStruct((L, D), F32), out_specs=pl.BlockSpec((tm, D), lambda i, k: (i, 0)))
        dw_in = _matmul(
            "mm_dw_in", r["h"], dp, dims=TN, grid=(NDEV, L // tm), kaxis=1, acc_shape=(D, NIN),
            a_spec=pl.BlockSpec((tm, D), lambda j, k: (k, 0)), b_spec=pl.BlockSpec((tm, NIN), lambda j, k: (k, j)),
            out_shape=jax.ShapeDtypeStruct((NDEV, D, NIN), BF16),
            out_specs=pl.BlockSpec((None, D, NIN), lambda j, k: (j, 0, 0)))
        dx0, dn1, dsh1, dsc1 = _lnmod_bwd(dh, r["x"], lp["norm1_g"], sh1, sc1, dxm)
        grads = dict(
            dmod=jnp.concatenate([dsh1, dsc1, dg1, dsh2, dsc2, dg2], axis=1), dn1=dn1, dn2=dn2, dgqk=dgqk,
            dga=dga, dgs=dgs, dbsr=dbsr, dbsi=dbsi, dcr=dcr, dci=dci, dd=dd, dwg=dwg, dgb=dgb, dabr=dabr, dabi=dabi,
            dcb=dcb, dw_in=dw_in, dw_out=dw_out.reshape(NDEV, ROWS_OUT, D), dw_up=dw_up,
            dcw=dcw.reshape(NDEV, 3, FS), dw_down=dw_down.reshape(NDEV, ROWS_DOWN, D))
        return dx0, grads

    grad_x, layer_grads = dy, [None] * NL
    for l in reversed(range(NL)):
        grad_x, layer_grads[l] = layer_bwd(grad_x, (per_layer[l], residuals[l]))
    gr = jax.tree.map(lambda *a: jnp.stack(a), *layer_grads)

    dar, dai, dldt, dbr_bd, dbi_bd = _s5_prep_bwd(ar_row, ai_row, ldt_row, braw_r, braw_i,
                                                  gr["dabr"], gr["dabi"], gr["dbsr"], gr["dbsi"])
    unt = lambda a: a.reshape((NL, G) + a.shape[3:])
    local = dict(
        ada_b=gr["dmod"].reshape(NL, N_MOD * D),
        norm1_g=gr["dn1"].reshape(NL, D), norm2_g=gr["dn2"].reshape(NL, D),
        q_norm_g=gr["dgqk"].reshape(NL, 2, NH, HEAD_DIM)[:, 0].sum(axis=1),
        k_norm_g=gr["dgqk"].reshape(NL, 2, NH, HEAD_DIM)[:, 1].sum(axis=1),
        ssm_a_re=dar.reshape(NL, G, SSM_STATE), ssm_a_im=dai.reshape(NL, G, SSM_STATE),
        ssm_log_dt=dldt.reshape(NL, G, SSM_STATE).sum(axis=-1),
        ssm_b_re=jnp.swapaxes(unt(_block_diag_extract(dbr_bd, SSM_GROUP, SSM_STATE)), -1, -2),
        ssm_b_im=jnp.swapaxes(unt(_block_diag_extract(dbi_bd, SSM_GROUP, SSM_STATE)), -1, -2),
        ssm_c_re=jnp.swapaxes(unt(_block_diag_extract(gr["dcr"], SSM_STATE, SSM_GROUP)), -1, -2),
        ssm_c_im=jnp.swapaxes(unt(_block_diag_extract(gr["dci"], SSM_STATE, SSM_GROUP)), -1, -2),
        ssm_d=gr["dd"].reshape(NL, G, SSM_GROUP),
        glu_w=unt(_block_diag_extract(gr["dwg"], SSM_GROUP, SSM_GROUP)),
        glu_b=gr["dgb"].reshape(NL, G, SSM_GROUP),
        attn_out_g=gr["dga"].reshape(NL, AW), ssm_out_g=gr["dgs"].reshape(NL, SW),
        ffn_conv_b=gr["dcb"].reshape(NL, 2 * NSH * FS),
    )

    def pack(tree):
        flat = jnp.concatenate([tree[n].reshape(-1) for n in small])
        pad = (-flat.shape[0]) % (512 * LANES)
        return jnp.pad(flat, (0, pad)).reshape(1, -1, LANES)

    (small_parts,) = _gather_shards([pack(local)], "gather_small_grads")
    sg, sd, sm, sv = _adamw(small_parts, pack(weights), pack(mom_m), pack(mom_v))

    def unpack(buf):
        flat = buf.reshape(-1)
        out, off = {}, 0
        for n in small:
            size = weights[n].size
            out[n] = flat[off:off + size].reshape(weights[n].shape)
            off += size
        return out

    ug, ud, um, uv = unpack(sg), unpack(sd), unpack(sm), unpack(sv)
    results = {n: (ug[n], ud[n], um[n], uv[n]) for n in small}

    dmod_all = _gather_small(gr["dmod"].reshape(NL, N_MOD * D), "gather_dmod")
    dmod_mine = lax.dynamic_slice_in_dim(dmod_all, me * NCA, NCA, axis=2)
    d_ada_w = _ada_bwd(jnp.transpose(c_all), jnp.transpose(dmod_mine, (1, 0, 2)))
    results["ada_w"] = tuple(_adamw(d_ada_w[:, None], ada_w, m_ada_w, v_ada_w))

    by_target = [jnp.swapaxes(g.reshape((NL, NDEV // 2, 2) + g.shape[2:]), 1, 2)
                 for g in (gr["dw_in"], gr["dw_out"], gr["dw_up"], gr["dcw"], gr["dw_down"])]
    mine, theirs = _sibling_exchange(by_target, "pair_weight_grads")
    pair_sums = [_pair_add(a, b) for a, b in zip(mine, theirs)]
    parts = _chip_exchange(pair_sums, "scatter_weight_grads")
    for n, pt in zip(("w_in", "w_out", "ffn_w_up", "ffn_conv_w", "ffn_w_down"), parts):
        results[n] = tuple(_adamw(pt, weights[n], mom_m[n], mom_v[n]))

    out = [loss, grad_x[None]]
    for k in range(4):
        out.extend(results[n][k] for n in names)
    return tuple(out)
```

```python
import functools
import math

import jax
import jax.numpy as jnp
from jax import lax
from jax.experimental import pallas as pl
from jax.experimental.pallas import tpu as pltpu

F32 = jnp.float32
BF16 = jnp.bfloat16
NDEV = 8
LANES = 128
HEAD_DIM = 64
SSM_GROUP = 16
SSM_STATE = 64
GROUPS_PER_TILE = LANES // SSM_GROUP
STATE_TILE = GROUPS_PER_TILE * SSM_STATE
N_MOD = 6
ATTN_STRIP = 32
EPS = 1e-6
ADAM_LR, ADAM_B1, ADAM_B2, ADAM_EPS, ADAM_WD, ADAM_STEP = 0.001, 0.9, 0.999, 1e-08, 0.01, 10
VMEM_LIMIT = 48 * 1024 * 1024
MESH_IDS = pl.DeviceIdType.MESH

NN = (((1,), (0,)), ((), ()))
NT = (((1,), (1,)), ((), ()))
TN = (((0,), (0,)), ((), ()))


def _dot(a, b, dims=NN):
    return lax.dot_general(a, b, dims, preferred_element_type=F32)


def _pcall(body, *, name, out_shape, in_specs, out_specs, grid=(), scratch=()):
    return pl.pallas_call(
        body, name=name, grid=grid, in_specs=in_specs, out_specs=out_specs, out_shape=out_shape,
        scratch_shapes=list(scratch),
        compiler_params=pltpu.CompilerParams(vmem_limit_bytes=VMEM_LIMIT))


def _row_tile(n, want=512):
    t = min(n, want)
    assert n % t == 0
    return t


def _my_index():
    return 4 * lax.axis_index("x") + 2 * lax.axis_index("y") + lax.axis_index("c")


HBM_SPEC = pl.BlockSpec(memory_space=pltpu.HBM)


def _mesh_place():
    x, y, c = lax.axis_index("x"), lax.axis_index("y"), lax.axis_index("c")
    chips = [(1 - x, y), (x, 1 - y), (1 - x, 1 - y)]
    return x, y, c, chips


def _gather_small(arr, name):
    def body(in_ref, out_ref, send_sems, recv_sems, local_sem):
        x, y, c, _ = _mesh_place()
        me = 4 * x + 2 * y + c
        own = pltpu.make_async_copy(in_ref, out_ref.at[me], local_sem)
        own.start()
        sends, recvs = [], []
        for k in range(1, NDEV):
            px = 1 - x if k & 4 else x
            py = 1 - y if k & 2 else y
            pc = 1 - c if k & 1 else c
            common = dict(send_sem=send_sems.at[k - 1], recv_sem=recv_sems.at[k - 1],
                          device_id=(px, py, pc), device_id_type=MESH_IDS)
            snd = pltpu.make_async_remote_copy(src_ref=in_ref, dst_ref=out_ref.at[me], **common)
            snd.start()
            sends.append(snd)
            recvs.append(pltpu.make_async_remote_copy(
                src_ref=in_ref, dst_ref=out_ref.at[4 * px + 2 * py + pc], **common))
        for r in recvs:
            r.wait_recv()
        for s in sends:
            s.wait_send()
        own.wait()

    return pl.pallas_call(
        body, name=name, out_shape=jax.ShapeDtypeStruct((NDEV,) + arr.shape, arr.dtype),
        in_specs=[HBM_SPEC], out_specs=HBM_SPEC,
        scratch_shapes=[pltpu.SemaphoreType.DMA((NDEV - 1,)), pltpu.SemaphoreType.DMA((NDEV - 1,)),
                        pltpu.SemaphoreType.DMA(())],
    )(arr)


def _start_by_layer(make, nl):
    for l in range(nl):
        make(l).start()


def _gather_shards(arrs, name):
    n = len(arrs)
    nl = arrs[0].shape[0]

    def body(*refs):
        ins, outs = refs[:n], refs[n:2 * n]
        send_sems, recv_sems, local_sems = refs[2 * n:]
        x, y, c, chips = _mesh_place()
        dev = lambda px, py, pc: 4 * px + 2 * py + pc

        def copy(k, a, src, block, to, layer=slice(None)):
            return pltpu.make_async_remote_copy(
                src_ref=src.at[layer], dst_ref=outs[a].at[layer, block], send_sem=send_sems.at[k, a],
                recv_sem=recv_sems.at[k, a], device_id=to, device_id_type=MESH_IDS)

        me = dev(x, y, c)
        for a in range(n):
            _start_by_layer(lambda l: pltpu.make_async_copy(ins[a].at[l], outs[a].at[l, me], local_sems.at[a]), nl)
        sent = []
        for a in range(n):
            _start_by_layer(lambda l: copy(0, a, ins[a], me, (x, y, 1 - c), l), nl)
            sent.append(copy(0, a, ins[a], me, (x, y, 1 - c)))
        for j, (px, py) in enumerate(chips):
            for a in range(n):
                _start_by_layer(lambda l: copy(1 + j, a, ins[a], me, (px, py, c), l), nl)
                sent.append(copy(1 + j, a, ins[a], me, (px, py, c)))
        for j, (px, py) in enumerate(chips):
            for a in range(n):
                blk = dev(px, py, c)
                copy(1 + j, a, ins[a], blk, (x, y, c)).wait_recv()
                got = outs[a].at[:, blk]
                _start_by_layer(lambda l: copy(4 + j, a, got, blk, (x, y, 1 - c), l), nl)
                sent.append(copy(4 + j, a, got, blk, (x, y, 1 - c)))
        for a in range(n):
            copy(0, a, ins[a], dev(x, y, 1 - c), (x, y, c)).wait_recv()
        for j, (px, py) in enumerate(chips):
            for a in range(n):
                copy(4 + j, a, ins[a], dev(px, py, 1 - c), (x, y, c)).wait_recv()
        for s in sent:
            s.wait_send()
        for a in range(n):
            pltpu.make_async_copy(ins[a], outs[a].at[:, me], local_sems.at[a]).wait()

    out_shape = [jax.ShapeDtypeStruct((a.shape[0], NDEV) + a.shape[1:], a.dtype) for a in arrs]
    return list(pl.pallas_call(
        body, name=name, out_shape=out_shape, in_specs=[HBM_SPEC] * n, out_specs=[HBM_SPEC] * n,
        scratch_shapes=[pltpu.SemaphoreType.DMA((7, n)), pltpu.SemaphoreType.DMA((7, n)),
                        pltpu.SemaphoreType.DMA((n,))],
    )(*arrs))


def _sibling_exchange(arrs, name):
    n = len(arrs)
    nl, _, nchips = arrs[0].shape[:3]

    def body(*refs):
        ins, own, got = refs[:n], refs[n:2 * n], refs[2 * n:3 * n]
        send_sems, recv_sems, local_sems = refs[3 * n:]
        x, y, c, _ = _mesh_place()

        def send(a, idx):
            return pltpu.make_async_remote_copy(
                src_ref=ins[a].at[idx[0], 1 - c, idx[1]], dst_ref=got[a].at[idx], send_sem=send_sems.at[a],
                recv_sem=recv_sems.at[a], device_id=(x, y, 1 - c), device_id_type=MESH_IDS)

        def keep(a, idx):
            return pltpu.make_async_copy(ins[a].at[idx[0], c, idx[1]], own[a].at[idx], local_sems.at[a])

        everything = (slice(None), slice(None))
        for a in range(n):
            for q in range(nchips):
                _start_by_layer(lambda l: send(a, (l, q)), nl)
                _start_by_layer(lambda l: keep(a, (l, q)), nl)
        for a in range(n):
            send(a, everything).wait_recv()
        for a in range(n):
            send(a, everything).wait_send()
            keep(a, everything).wait()

    shapes = [jax.ShapeDtypeStruct(a.shape[:1] + a.shape[2:], a.dtype) for a in arrs]
    outs = pl.pallas_call(
        body, name=name, out_shape=shapes + shapes, in_specs=[HBM_SPEC] * n, out_specs=[HBM_SPEC] * (2 * n),
        scratch_shapes=[pltpu.SemaphoreType.DMA((n,)), pltpu.SemaphoreType.DMA((n,)),
                        pltpu.SemaphoreType.DMA((n,))],
    )(*arrs)
    return list(outs[:n]), list(outs[n:])


def _chip_exchange(arrs, name):
    n = len(arrs)
    nl = arrs[0].shape[0]

    def body(*refs):
        ins, outs = refs[:n], refs[n:2 * n]
        send_sems, recv_sems, local_sems = refs[2 * n:]
        x, y, c, chips = _mesh_place()
        mine = 2 * x + y

        def copy(j, a, target, source, layer=slice(None)):
            px, py = chips[j]
            return pltpu.make_async_remote_copy(
                src_ref=ins[a].at[layer, target], dst_ref=outs[a].at[layer, source], send_sem=send_sems.at[j, a],
                recv_sem=recv_sems.at[j, a], device_id=(px, py, c), device_id_type=MESH_IDS)

        for a in range(n):
            _start_by_layer(lambda l: pltpu.make_async_copy(ins[a].at[l, mine], outs[a].at[l, mine], local_sems.at[a]), nl)
        for j, (px, py) in enumerate(chips):
            for a in range(n):
                _start_by_layer(lambda l: copy(j, a, 2 * px + py, mine, l), nl)
        for j, (px, py) in enumerate(chips):
            for a in range(n):
                copy(j, a, mine, 2 * px + py).wait_recv()
        for j, (px, py) in enumerate(chips):
            for a in range(n):
                copy(j, a, 2 * px + py, mine).wait_send()
        for a in range(n):
            pltpu.make_async_copy(ins[a].at[:, mine], outs[a].at[:, mine], local_sems.at[a]).wait()

    out_shape = [jax.ShapeDtypeStruct(a.shape, a.dtype) for a in arrs]
    return list(pl.pallas_call(
        body, name=name, out_shape=out_shape, in_specs=[HBM_SPEC] * n, out_specs=[HBM_SPEC] * n,
        scratch_shapes=[pltpu.SemaphoreType.DMA((3, n)), pltpu.SemaphoreType.DMA((3, n)),
                        pltpu.SemaphoreType.DMA((n,))],
    )(*arrs))


def _pair_add(a, b):
    NL, NC, R, C = a.shape
    tr = R
    for cand in (256, 128, 64, 32, 16):
        if R % cand == 0 and R > cand:
            tr = cand
            break

    def body(a_ref, b_ref, o_ref):
        o_ref[...] = (a_ref[...].astype(F32) + b_ref[...].astype(F32)).astype(o_ref.dtype)

    spec = pl.BlockSpec((None, NC, tr, C), lambda l, i: (l, 0, i, 0))
    return _pcall(body, name="pair_add", grid=(NL, R // tr), in_specs=[spec, spec], out_specs=spec,
                  out_shape=jax.ShapeDtypeStruct(a.shape, a.dtype))(a, b)


def _matmul(name, a, b, *, dims, grid, a_spec, b_spec, out_shape, out_specs, kaxis=None, acc_shape=None,
            extra=(), extra_specs=(), epilogue=None):
    nk = grid[kaxis] if kaxis is not None else 1
    ne = len(extra)
    multi = isinstance(out_shape, (list, tuple))
    n_out = len(out_shape) if multi else 1

    def body(*refs):
        a_ref, b_ref = refs[0], refs[1]
        ex = refs[2:2 + ne]
        outs = refs[2 + ne:2 + ne + n_out]

        def write(res):
            vals = epilogue(res, *[e[...] for e in ex]) if epilogue is not None else (res,)
            for o, v in zip(outs, vals):
                o[...] = v.astype(o.dtype)

        part = _dot(a_ref[...].astype(BF16), b_ref[...].astype(BF16), dims)
        if nk == 1:
            write(part)
        else:
            acc = refs[-1]
            k = pl.program_id(kaxis)

            @pl.when(k == 0)
            def _():
                acc[...] = part

            @pl.when(k > 0)
            def _():
                acc[...] += part

            @pl.when(k == nk - 1)
            def _():
                write(acc[...])

    scratch = [pltpu.VMEM(acc_shape, F32)] if nk > 1 else []
    return _pcall(body, name=name, grid=grid, in_specs=[a_spec, b_spec, *extra_specs],
                  out_specs=out_specs, out_shape=out_shape, scratch=scratch)(a, b, *extra)


def _rms(x, g):
    inv = lax.rsqrt(jnp.mean(x * x, axis=-1, keepdims=True) + EPS)
    return x * inv * g


def _lnmod_math(x, g, sh, sc):
    return _rms(x, g) * (1.0 + sc) + sh


def _qkn_math(p, g):
    m0 = lax.broadcasted_iota(jnp.int32, (1, LANES), 1) < HEAD_DIM
    sq = p * p
    s0 = jnp.sum(jnp.where(m0, sq, 0.0), axis=-1, keepdims=True)
    s1 = jnp.sum(jnp.where(m0, 0.0, sq), axis=-1, keepdims=True)
    inv = jnp.where(m0, lax.rsqrt(s0 / HEAD_DIM + EPS), lax.rsqrt(s1 / HEAD_DIM + EPS))
    return p * inv * g


def _glu_math(val, gate):
    return jax.nn.gelu(gate) * val


def _accumulate(ref, val, first):
    @pl.when(first)
    def _():
        ref[...] = val

    @pl.when(jnp.logical_not(first))
    def _():
        ref[...] += val


def _lnmod_fwd(x, g, sh, sc):
    L, D = x.shape
    tm = _row_tile(L)

    def body(x_ref, g_ref, sh_ref, sc_ref, h_ref):
        h_ref[...] = _lnmod_math(x_ref[...], g_ref[...], sh_ref[...], sc_ref[...]).astype(BF16)

    row = pl.BlockSpec((tm, D), lambda i: (i, 0))
    vec = pl.BlockSpec((1, D), lambda i: (0, 0))
    return _pcall(body, name="lnmod_fwd", grid=(L // tm,), in_specs=[row, vec, vec, vec], out_specs=row,
                  out_shape=jax.ShapeDtypeStruct((L, D), BF16))(x, g, sh, sc)


def _lnmod_bwd(dh, x, g, sh, sc, dres):
    L, D = x.shape
    tm = _row_tile(L)

    def body(dh_ref, x_ref, g_ref, sh_ref, sc_ref, res_ref, dx_ref, dg_ref, dsh_ref, dsc_ref):
        _, vjp = jax.vjp(_lnmod_math, x_ref[...], g_ref[...], sh_ref[...], sc_ref[...])
        dx, dg, dsh, dsc = vjp(dh_ref[...])
        dx_ref[...] = dx + res_ref[...]
        first = pl.program_id(0) == 0
        _accumulate(dg_ref, dg, first)
        _accumulate(dsh_ref, dsh, first)
        _accumulate(dsc_ref, dsc, first)

    row = pl.BlockSpec((tm, D), lambda i: (i, 0))
    vec = pl.BlockSpec((1, D), lambda i: (0, 0))
    vs = jax.ShapeDtypeStruct((1, D), F32)
    return _pcall(body, name="lnmod_bwd", grid=(L // tm,), in_specs=[row, row, vec, vec, vec, row],
                  out_specs=[row, vec, vec, vec],
                  out_shape=[jax.ShapeDtypeStruct((L, D), F32), vs, vs, vs])(dh, x, g, sh, sc, dres)


def _gate_bwd(dx, a, gate):
    L, D = dx.shape
    tm = _row_tile(L)

    def body(dx_ref, a_ref, g_ref, da_ref, dg_ref):
        dxv = dx_ref[...]
        da_ref[...] = (g_ref[...] * dxv).astype(BF16)
        _accumulate(dg_ref, jnp.sum(dxv * a_ref[...], axis=0, keepdims=True), pl.program_id(0) == 0)

    row = pl.BlockSpec((tm, D), lambda i: (i, 0))
    vec = pl.BlockSpec((1, D), lambda i: (0, 0))
    return _pcall(body, name="gate_bwd", grid=(L // tm,), in_specs=[row, row, vec], out_specs=[row, vec],
                  out_shape=[jax.ShapeDtypeStruct((L, D), BF16), jax.ShapeDtypeStruct((1, D), F32)])(dx, a, gate)


def _qknorm_fwd(p, gqk, AW):
    L = p.shape[0]
    tm = _row_tile(L)
    ncol = 2 * AW // LANES

    def body(p_ref, g_ref, o_ref):
        o_ref[...] = _qkn_math(p_ref[...], g_ref[...]).astype(BF16)

    blk = pl.BlockSpec((tm, LANES), lambda i, j: (i, j))
    vec = pl.BlockSpec((1, LANES), lambda i, j: (0, j))
    return _pcall(body, name="qknorm_fwd", grid=(L // tm, ncol), in_specs=[blk, vec], out_specs=blk,
                  out_shape=jax.ShapeDtypeStruct((L, 2 * AW), BF16))(p, gqk)


def _qknorm_bwd(dqk, p, gqk, AW):
    L = p.shape[0]
    tm = _row_tile(L)
    ncol = 2 * AW // LANES

    def body(d_ref, p_ref, g_ref, dp_ref, dg_ref):
        _, vjp = jax.vjp(_qkn_math, p_ref[...], g_ref[...])
        dp, dg = vjp(d_ref[...])
        dp_ref[...] = dp
        _accumulate(dg_ref, dg, pl.program_id(1) == 0)

    blk = pl.BlockSpec((tm, LANES), lambda j, i: (i, j))
    vec = pl.BlockSpec((1, LANES), lambda j, i: (0, j))
    return _pcall(body, name="qknorm_bwd", grid=(ncol, L // tm), in_specs=[blk, blk, vec], out_specs=[blk, vec],
                  out_shape=[jax.ShapeDtypeStruct((L, 2 * AW), F32),
                             jax.ShapeDtypeStruct((1, 2 * AW), F32)])(dqk, p, gqk)


def _outnorm_fwd(oa, os_, ga, gs):
    L, AW = oa.shape
    SW = os_.shape[1]
    tm = _row_tile(L)

    def body(oa_ref, os_ref, ga_ref, gs_ref, o_ref):
        o_ref[:, :AW] = _rms(oa_ref[...], ga_ref[...]).astype(BF16)
        o_ref[:, AW:] = _rms(os_ref[...], gs_ref[...]).astype(BF16)

    ra = pl.BlockSpec((tm, AW), lambda i: (i, 0))
    rs = pl.BlockSpec((tm, SW), lambda i: (i, 0))
    va = pl.BlockSpec((1, AW), lambda i: (0, 0))
    vs = pl.BlockSpec((1, SW), lambda i: (0, 0))
    ro = pl.BlockSpec((tm, AW + SW), lambda i: (i, 0))
    return _pcall(body, name="outnorm_fwd", grid=(L // tm,), in_specs=[ra, rs, va, vs], out_specs=ro,
                  out_shape=jax.ShapeDtypeStruct((L, AW + SW), BF16))(oa, os_, ga, gs)


def _outnorm_bwd(do, oa, os_, ga, gs):
    L, AW = oa.shape
    SW = os_.shape[1]
    tm = _row_tile(L)

    def body(do_ref, oa_ref, os_ref, ga_ref, gs_ref, doa_ref, dos_ref, dga_ref, dgs_ref):
        first = pl.program_id(0) == 0
        _, vjp_a = jax.vjp(_rms, oa_ref[...], ga_ref[...])
        doa, dga = vjp_a(do_ref[:, :AW])
        _, vjp_s = jax.vjp(_rms, os_ref[...], gs_ref[...])
        dos, dgs = vjp_s(do_ref[:, AW:])
        doa_ref[...] = doa
        dos_ref[...] = dos
        _accumulate(dga_ref, dga, first)
        _accumulate(dgs_ref, dgs, first)

    ra = pl.BlockSpec((tm, AW), lambda i: (i, 0))
    rs = pl.BlockSpec((tm, SW), lambda i: (i, 0))
    va = pl.BlockSpec((1, AW), lambda i: (0, 0))
    vs = pl.BlockSpec((1, SW), lambda i: (0, 0))
    ro = pl.BlockSpec((tm, AW + SW), lambda i: (i, 0))
    return _pcall(body, name="outnorm_bwd", grid=(L // tm,), in_specs=[ro, ra, rs, va, vs],
                  out_specs=[ra, rs, va, vs],
                  out_shape=[jax.ShapeDtypeStruct((L, AW), F32), jax.ShapeDtypeStruct((L, SW), F32),
                             jax.ShapeDtypeStruct((1, AW), F32), jax.ShapeDtypeStruct((1, SW), F32)])(
                                 do, oa, os_, ga, gs)


def _softplus_neg_abs(z):
    return jnp.log(1.0 + jnp.exp(-jnp.abs(z)))


def _split_dot(x, m):
    hi = x.astype(BF16)
    lo = (x - hi.astype(F32)).astype(BF16)
    return _dot(hi, m) + _dot(lo, m)


def _attn_masks(B):
    row = lax.broadcasted_iota(jnp.int32, (B, B), 0)
    col = lax.broadcasted_iota(jnp.int32, (B, B), 1)
    strict = col < row
    upper = jnp.where(row > col, 1.0, 0.0).astype(BF16)
    lower = jnp.where(row < col, 1.0, 0.0).astype(BF16)
    return strict, upper, lower


def _attn_fwd(qk, p, AW, B):
    L = qk.shape[0]
    HP = AW // LANES
    nb = L // B

    def body(q_ref, k_ref, v_ref, o_ref, tot_ref, *scr):
        lb_scr = (scr[0:2], scr[2:4])
        tail_scr = (scr[4:6], scr[6:8])
        sum_scr = scr[8:10]
        i = pl.program_id(1)
        m0 = lax.broadcasted_iota(jnp.int32, (1, LANES), 1) < HEAD_DIM
        strict, upper, _ = _attn_masks(B)
        q = q_ref[...] * 0.125
        zq = jnp.zeros_like(q)
        qh = (jnp.where(m0, q, zq), jnp.where(m0, zq, q))

        def keys(j):
            start = pl.multiple_of(jnp.maximum(j, 0) * B, B)
            return k_ref[pl.ds(start, B), :]

        def vals(j):
            start = pl.multiple_of(jnp.maximum(j, 0) * B, B)
            return v_ref[pl.ds(start, B), :].astype(BF16)

        strips = [slice(s, min(s + ATTN_STRIP, B)) for s in range(0, B, ATTN_STRIP)]

        def scores(j):
            kj = keys(j)
            return tuple(_dot(qh[h], kj, NT) for h in (0, 1))

        def logits(zs2, slot, diag):
            for h in (0, 1):
                z = zs2[h]
                his, los = [], []
                for rows in strips:
                    zs = z[rows]
                    lb = jnp.minimum(zs, 0.0) - _softplus_neg_abs(zs)
                    l1 = lb - zs
                    if diag:
                        l1 = jnp.where(strict[rows], l1, 0.0)
                    lb_scr[slot][h][rows, :] = lb
                    hi = l1.astype(BF16)
                    his.append(hi)
                    los.append((l1 - hi.astype(F32)).astype(BF16))
                    rsum = jnp.sum(l1, axis=-1, keepdims=True)
                    if h == 0:
                        sum_scr[slot][rows, :] = jnp.broadcast_to(rsum, (rows.stop - rows.start, LANES))
                    else:
                        sum_scr[slot][rows, :] = jnp.where(m0, sum_scr[slot][rows, :], rsum)
                cat = lambda xs: jnp.concatenate(xs, axis=0)
                tail_scr[slot][h][...] = _dot(cat(his), upper) + _dot(cat(los), upper)

        def attend(j, slot, diag):
            vj = vals(j)
            pv = []
            for h in (0, 1):
                ws = []
                for rows in strips:
                    w = jnp.exp(lb_scr[slot][h][rows, :] + tail_scr[slot][h][rows, :])
                    if diag:
                        w = jnp.where(strict[rows], w, 0.0)
                    ws.append(w.astype(BF16))
                pv.append(_dot(jnp.concatenate(ws, axis=0), vj))
            return jnp.where(m0, pv[0], pv[1])

        logits(scores(i), 0, True)
        o_ref[...] = attend(i, 0, True)
        tot_ref[...] = sum_scr[0][...]

        def half(j, slot):
            z = scores(j - 1)
            pv = attend(j, slot, False)
            logits(z, 1 - slot, False)
            o_ref[...] += pv * jnp.exp(tot_ref[...])
            tot_ref[...] += sum_scr[slot][...]

        @pl.when(i > 0)
        def _():
            logits(scores(i - 1), 1, False)

            @pl.loop(0, (i + 1) // 2)
            def _(t):
                j = i - 1 - 2 * t
                half(j, 1)

                @pl.when(j > 0)
                def _():
                    half(j - 1, 0)


    qspec = pl.BlockSpec((B, LANES), lambda hp, i: (i, hp))
    kspec = pl.BlockSpec((L, LANES), lambda hp, i: (0, HP + hp))
    vspec = pl.BlockSpec((L, LANES), lambda hp, i: (0, 2 * HP + hp))
    ospec = pl.BlockSpec((B, LANES), lambda hp, i: (i, hp))
    shp = jax.ShapeDtypeStruct((L, AW), F32)
    return _pcall(body, name="attn_fwd", grid=(HP, nb), in_specs=[qspec, kspec, vspec],
                  out_specs=[ospec, ospec], out_shape=[shp, shp],
                  scratch=[pltpu.VMEM((B, B), F32)] * 8 + [pltpu.VMEM((B, LANES), F32)] * 2)(qk, qk, p)


def _attn_bwd(qk, p, do, tot, AW, B):
    L = qk.shape[0]
    HP = AW // LANES
    nb = L // B

    def body(q_ref, k_ref, v_ref, do_ref, tot_ref, dq_ref, dk_ref, dv_ref):
        i = pl.program_id(1)
        m0 = lax.broadcasted_iota(jnp.int32, (1, LANES), 1) < HEAD_DIM
        strict, upper, lower = _attn_masks(B)

        @pl.when(i == 0)
        def _():
            dk_ref[...] = jnp.zeros_like(dk_ref)
            dv_ref[...] = jnp.zeros_like(dv_ref)

        q = q_ref[...] * 0.125
        zq = jnp.zeros_like(q)
        qh = (jnp.where(m0, q, zq), jnp.where(m0, zq, q))
        dob = do_ref[...].astype(BF16)
        doh = (jnp.where(m0, dob, zq), jnp.where(m0, zq, dob))
        tv = tot_ref[...]
        th = (jnp.sum(jnp.where(m0, tv, 0.0), axis=-1, keepdims=True) / HEAD_DIM,
              jnp.sum(jnp.where(m0, 0.0, tv), axis=-1, keepdims=True) / HEAD_DIM)

        def block(j, carry, diag):
            pls, es, dq = carry
            start = pl.multiple_of(j * B, B)
            kj = k_ref[pl.ds(start, B), :]
            vj = v_ref[pl.ds(start, B), :].astype(BF16)
            zk = jnp.zeros_like(kj)
            kh = (jnp.where(m0, kj, zk), jnp.where(m0, zk, kj))
            new_pl, new_e = [], []
            dk_blk = jnp.zeros((B, LANES), F32)
            dv_blk = jnp.zeros((B, LANES), F32)
            for h in (0, 1):
                z = _dot(qh[h], kj, NT)
                sp = _softplus_neg_abs(z)
                lb = jnp.minimum(z, 0.0) - sp
                l1 = lb - z
                if diag:
                    l1 = jnp.where(strict, l1, 0.0)
                bsum = jnp.sum(l1, axis=-1, keepdims=True)
                right = th[h] - pls[h] - bsum
                tail = _split_dot(l1, upper)
                w = jnp.exp(lb + tail + right)
                if diag:
                    w = jnp.where(strict, w, 0.0)
                dw = _dot(doh[h], vj, NT)
                e = dw * w
                dl1 = _split_dot(e, lower) + es[h]
                beta = jnp.exp(lb)
                dz = e * (1.0 - beta) - dl1 * beta
                if diag:
                    dz = jnp.where(strict, dz, 0.0)
                dzb = dz.astype(BF16)
                wb = w.astype(BF16)
                dq = dq + _dot(dzb, kh[h])
                dk_blk = dk_blk + _dot(dzb, qh[h], TN)
                dv_blk = dv_blk + _dot(wb, doh[h], TN)
                new_pl.append(pls[h] + bsum)
                new_e.append(es[h] + jnp.sum(e, axis=-1, keepdims=True))
            dk_ref[pl.ds(start, B), :] += dk_blk
            dv_ref[pl.ds(start, B), :] += dv_blk
            return tuple(new_pl), tuple(new_e), dq

        zero = jnp.zeros((B, 1), F32)
        carry = ((zero, zero), (zero, zero), jnp.zeros((B, LANES), F32))
        carry = lax.fori_loop(0, i, lambda j, c: block(j, c, False), carry)
        _, _, dq = block(i, carry, True)
        dq_ref[...] = dq * 0.125

    qspec = pl.BlockSpec((B, LANES), lambda hp, i: (i, hp))
    kspec = pl.BlockSpec((L, LANES), lambda hp, i: (0, HP + hp))
    vspec = pl.BlockSpec((L, LANES), lambda hp, i: (0, 2 * HP + hp))
    full = pl.BlockSpec((L, LANES), lambda hp, i: (0, hp))
    shp = jax.ShapeDtypeStruct((L, AW), F32)
    return _pcall(body, name="attn_bwd", grid=(HP, nb), in_specs=[qspec, kspec, vspec, qspec, qspec],
                  out_specs=[qspec, full, full], out_shape=[shp, shp, shp])(qk, qk, p, do, tot)


def _s5_disc(ar, ai, ldt):
    dt = jnp.exp(ldt)
    mag = jnp.exp(dt * ar)
    abr = mag * jnp.cos(dt * ai)
    abi = mag * jnp.sin(dt * ai)
    emr = abr - 1.0
    emi = abi
    den = ar * ar + ai * ai
    fr = (emr * ar + emi * ai) / den
    fi = (emi * ar - emr * ai) / den
    return abr, abi, fr, fi


def _s5_params_math(ar, ai, ldt, br, bi):
    abr, abi, fr, fi = _s5_disc(ar, ai, ldt)
    return abr, abi, fr * br - fi * bi, fr * bi + fi * br


def _scan_rows(xr, xi, lev, T, reverse):
    row = lax.broadcasted_iota(jnp.int32, xr.shape, 0)
    d, k = 1, 0
    while d < T:
        kr, ki = lev(k)
        if reverse:
            keep = row < T - d
            sr = jnp.where(keep, pltpu.roll(xr, T - d, 0), 0.0)
            si = jnp.where(keep, pltpu.roll(xi, T - d, 0), 0.0)
            ki = -ki
        else:
            keep = row >= d
            sr = jnp.where(keep, pltpu.roll(xr, d, 0), 0.0)
            si = jnp.where(keep, pltpu.roll(xi, d, 0), 0.0)
        xr, xi = xr + (kr * sr - ki * si), xi + (kr * si + ki * sr)
        d, k = 2 * d, k + 1
    return xr, xi


def _s5_prep(ar, ai, ldt, braw_r, braw_i, T):
    NL, _, NS = ar.shape
    GT = NS // STATE_TILE
    nlev = max(1, int(math.log2(T)))
    LV = 8 * ((nlev + 7) // 8)

    def body(ar_ref, ai_ref, ldt_ref, br_ref, bi_ref,
             abr_ref, abi_ref, levr_ref, levi_ref, powr_ref, powi_ref, bsr_ref, bsi_ref):
        levr_ref[...] = jnp.zeros_like(levr_ref)
        levi_ref[...] = jnp.zeros_like(levi_ref)
        for t in range(GT):
            sl = slice(t * STATE_TILE, (t + 1) * STATE_TILE)
            abr, abi, bsr, bsi = _s5_params_math(ar_ref[:, sl], ai_ref[:, sl], ldt_ref[:, sl],
                                                 br_ref[t], bi_ref[t])
            abr_ref[:, sl] = abr
            abi_ref[:, sl] = abi
            bsr_ref[t] = bsr.astype(BF16)
            bsi_ref[t] = bsi.astype(BF16)
            kr, ki = abr, abi
            for k in range(nlev):
                levr_ref[k:k + 1, sl] = kr
                levi_ref[k:k + 1, sl] = ki
                kr, ki = kr * kr - ki * ki, 2.0 * kr * ki
        for s in range(NS // LANES):
            sl = slice(s * LANES, (s + 1) * LANES)
            row = lax.broadcasted_iota(jnp.int32, (T, LANES), 0)
            xr = jnp.where(row == 0, abr_ref[:, sl], 0.0)
            xi = jnp.where(row == 0, abi_ref[:, sl], 0.0)
            lev = lambda k, sl=sl: (levr_ref[k:k + 1, sl], levi_ref[k:k + 1, sl])
            xr, xi = _scan_rows(xr, xi, lev, T, False)
            powr_ref[:, sl] = xr
            powi_ref[:, sl] = xi

    rowspec = pl.BlockSpec((None, 1, NS), lambda l: (l, 0, 0))
    bspec = pl.BlockSpec((None, GT, LANES, STATE_TILE), lambda l: (l, 0, 0, 0))
    levspec = pl.BlockSpec((None, LV, NS), lambda l: (l, 0, 0))
    powspec = pl.BlockSpec((None, T, NS), lambda l: (l, 0, 0))
    rs = jax.ShapeDtypeStruct((NL, 1, NS), F32)
    ls = jax.ShapeDtypeStruct((NL, LV, NS), F32)
    ps = jax.ShapeDtypeStruct((NL, T, NS), F32)
    bs = jax.ShapeDtypeStruct((NL, GT, LANES, STATE_TILE), BF16)
    return _pcall(body, name="s5_prep", grid=(NL,), in_specs=[rowspec] * 3 + [bspec] * 2,
                  out_specs=[rowspec, rowspec, levspec, levspec, powspec, powspec, bspec, bspec],
                  out_shape=[rs, rs, ls, ls, ps, ps, bs, bs])(ar, ai, ldt, braw_r, braw_i)


def _s5_prep_bwd(ar, ai, ldt, braw_r, braw_i, dabr, dabi, dbsr, dbsi):
    NL, _, NS = ar.shape
    GT = NS // STATE_TILE

    def body(ar_ref, ai_ref, ldt_ref, br_ref, bi_ref, dabr_ref, dabi_ref, dbsr_ref, dbsi_ref,
             dar_ref, dai_ref, dldt_ref, dbr_ref, dbi_ref):
        for t in range(GT):
            sl = slice(t * STATE_TILE, (t + 1) * STATE_TILE)
            _, vjp = jax.vjp(_s5_params_math, ar_ref[:, sl], ai_ref[:, sl], ldt_ref[:, sl],
                             br_ref[t], bi_ref[t])
            dar, dai, dldt, dbr, dbi = vjp((dabr_ref[:, sl], dabi_ref[:, sl], dbsr_ref[t], dbsi_ref[t]))
            dar_ref[:, sl] = dar
            dai_ref[:, sl] = dai
            dldt_ref[:, sl] = dldt
            dbr_ref[t] = dbr
            dbi_ref[t] = dbi

    rowspec = pl.BlockSpec((None, 1, NS), lambda l: (l, 0, 0))
    bspec = pl.BlockSpec((None, GT, LANES, STATE_TILE), lambda l: (l, 0, 0, 0))
    rs = jax.ShapeDtypeStruct((NL, 1, NS), F32)
    bs = jax.ShapeDtypeStruct((NL, GT, LANES, STATE_TILE), F32)
    return _pcall(body, name="s5_prep_bwd", grid=(NL,), in_specs=[rowspec] * 3 + [bspec] * 2 + [rowspec] * 2 + [bspec] * 2,
                  out_specs=[rowspec] * 3 + [bspec] * 2, out_shape=[rs, rs, rs, bs, bs])(
                      ar, ai, ldt, braw_r, braw_i, dabr, dabi, dbsr, dbsi)


def _s5_specs(L, SW, T, ucol0, rev):
    GT = SW // LANES
    nc = L // T
    cidx = (lambda c: nc - 1 - c) if rev else (lambda c: c)
    return dict(
        GT=GT, nc=nc, cidx=cidx,
        u=pl.BlockSpec((T, LANES), lambda j, c: (cidx(c), ucol0 + j)),
        chan=pl.BlockSpec((T, LANES), lambda j, c: (cidx(c), j)),
        state=pl.BlockSpec((T, STATE_TILE), lambda j, c: (cidx(c), j)),
        bmat=pl.BlockSpec((None, LANES, STATE_TILE), lambda j, c: (j, 0, 0)),
        cmat=pl.BlockSpec((None, STATE_TILE, LANES), lambda j, c: (j, 0, 0)),
        gmat=pl.BlockSpec((None, LANES, LANES), lambda j, c: (j, 0, 0)),
        cvec=pl.BlockSpec((1, LANES), lambda j, c: (0, j)),
        svec=pl.BlockSpec((1, STATE_TILE), lambda j, c: (0, j)),
    )


def _s5_fwd(p, sp, SW, ucol0, T):
    L = p.shape[0]
    S = _s5_specs(L, SW, T, ucol0, False)
    NS = S["GT"] * STATE_TILE
    LV = sp["levr"].shape[0]

    def body(u_ref, bsr_ref, bsi_ref, levr_ref, levi_ref, powr_ref, powi_ref, cr_ref, ci_ref,
             d_ref, wg_ref, gb_ref, o_ref, sr_ref, si_ref, y_ref, carry_ref):
        @pl.when(pl.program_id(1) == 0)
        def _():
            carry_ref[...] = jnp.zeros_like(carry_ref)

        u = u_ref[...]
        ub = u.astype(BF16)
        bur = _dot(ub, bsr_ref[...])
        bui = _dot(ub, bsi_ref[...])
        for s in range(STATE_TILE // LANES):
            sl = slice(s * LANES, (s + 1) * LANES)
            lev = lambda k, sl=sl: (levr_ref[k:k + 1, sl], levi_ref[k:k + 1, sl])
            xr, xi = _scan_rows(bur[:, sl], bui[:, sl], lev, T, False)
            cr, ci = carry_ref[0:1, sl], carry_ref[1:2, sl]
            pr, pi = powr_ref[:, sl], powi_ref[:, sl]
            sr_ref[:, sl] = xr + (pr * cr - pi * ci)
            si_ref[:, sl] = xi + (pr * ci + pi * cr)
            carry_ref[0:1, sl] = sr_ref[T - 1:T, sl]
            carry_ref[1:2, sl] = si_ref[T - 1:T, sl]
        y = (_dot(sr_ref[...].astype(BF16), cr_ref[...]) - _dot(si_ref[...].astype(BF16), ci_ref[...])
             + d_ref[...] * u)
        y_ref[...] = y
        yg = jax.nn.gelu(y)
        gate = jax.nn.sigmoid(_dot(yg.astype(BF16), wg_ref[...]) + gb_ref[...])
        o_ref[...] = yg * gate

    lvspec = pl.BlockSpec((LV, STATE_TILE), lambda j, c: (0, j))
    pwspec = pl.BlockSpec((T, STATE_TILE), lambda j, c: (0, j))
    cs = jax.ShapeDtypeStruct((L, SW), F32)
    ss = jax.ShapeDtypeStruct((L, NS), F32)
    return _pcall(
        body, name="s5_fwd", grid=(S["GT"], S["nc"]),
        in_specs=[S["u"], S["bmat"], S["bmat"], lvspec, lvspec, pwspec, pwspec, S["cmat"], S["cmat"],
                  S["cvec"], S["gmat"], S["cvec"]],
        out_specs=[S["chan"], S["state"], S["state"], S["chan"]], out_shape=[cs, ss, ss, cs],
        scratch=[pltpu.VMEM((8, STATE_TILE), F32)],
    )(p, sp["bsr"], sp["bsi"], sp["levr"], sp["levi"], sp["powr"], sp["powi"], sp["crT"], sp["ciT"],
      sp["d"], sp["wg"], sp["gb"])


def _s5_bwd(do, p, s_re, s_im, ypre, sp, SW, ucol0, T):
    L = p.shape[0]
    S = _s5_specs(L, SW, T, ucol0, True)
    GT, nc, cidx = S["GT"], S["nc"], S["cidx"]
    NS = GT * STATE_TILE
    LV = sp["levr"].shape[0]
    T8 = T // 8

    def body(do_ref, u_ref, sr_ref, si_ref, hr_ref, hi_ref, y_ref, bsr_ref, bsi_ref, levr_ref, levi_ref,
             rpr_ref, rpi_ref, cr_ref, ci_ref, d_ref, wg_ref, gb_ref,
             du_ref, dbsr_ref, dbsi_ref, dcr_ref, dci_ref, dd_ref, dwg_ref, dgb_ref, dar_ref, dai_ref,
             carry_ref, lam_r, lam_i):
        c = pl.program_id(1)
        first = c == 0

        @pl.when(first)
        def _():
            carry_ref[...] = jnp.zeros_like(carry_ref)

        u = u_ref[...]
        ub = u.astype(BF16)
        y = y_ref[...]
        yg, gelu_vjp = jax.vjp(jax.nn.gelu, y)
        ygb = yg.astype(BF16)
        gate = jax.nn.sigmoid(_dot(ygb, wg_ref[...]) + gb_ref[...])
        dout = do_ref[...]
        dt = dout * yg * gate * (1.0 - gate)
        dtb = dt.astype(BF16)
        dyg = dout * gate + _dot(dtb, wg_ref[...], NT)
        (dy,) = gelu_vjp(dyg)
        dyb = dy.astype(BF16)
        _accumulate(dwg_ref, _dot(ygb, dtb, TN), first)
        _accumulate(dgb_ref, jnp.sum(dt, axis=0, keepdims=True), first)
        _accumulate(dd_ref, jnp.sum(dy * u, axis=0, keepdims=True), first)
        srb = sr_ref[...].astype(BF16)
        sib = si_ref[...].astype(BF16)
        _accumulate(dcr_ref, _dot(srb, dyb, TN), first)
        _accumulate(dci_ref, -_dot(sib, dyb, TN), first)
        dsr = _dot(dyb, cr_ref[...], NT)
        dsi = -_dot(dyb, ci_ref[...], NT)
        last_chunk = cidx(c) == 0
        row = lax.broadcasted_iota(jnp.int32, (T, LANES), 0)
        for s in range(STATE_TILE // LANES):
            sl = slice(s * LANES, (s + 1) * LANES)
            lev = lambda k, sl=sl: (levr_ref[k:k + 1, sl], levi_ref[k:k + 1, sl])
            xr, xi = _scan_rows(dsr[:, sl], dsi[:, sl], lev, T, True)
            cr, ci = carry_ref[0:1, sl], carry_ref[1:2, sl]
            pr, pi = rpr_ref[:, sl], -rpi_ref[:, sl]
            lam_r[:, sl] = xr + (pr * cr - pi * ci)
            lam_i[:, sl] = xi + (pr * ci + pi * cr)
            carry_ref[0:1, sl] = lam_r[0:1, sl]
            carry_ref[1:2, sl] = lam_i[0:1, sl]
            keep = jnp.logical_not(last_chunk)
            pr0 = jnp.where(keep, hr_ref[7:8, sl], 0.0)
            pi0 = jnp.where(keep, hi_ref[7:8, sl], 0.0)
            spr = jnp.where(row == 0, pr0, pltpu.roll(sr_ref[:, sl], 1, 0))
            spi = jnp.where(row == 0, pi0, pltpu.roll(si_ref[:, sl], 1, 0))
            lr, li = lam_r[:, sl], lam_i[:, sl]
            dar = jnp.sum(lr * spr + li * spi, axis=0, keepdims=True)
            dai = jnp.sum(li * spr - lr * spi, axis=0, keepdims=True)

            @pl.when(first)
            def _():
                dar_ref[:, sl] = dar
                dai_ref[:, sl] = dai

            @pl.when(jnp.logical_not(first))
            def _():
                dar_ref[:, sl] += dar
                dai_ref[:, sl] += dai

        lrb = lam_r[...].astype(BF16)
        lib = lam_i[...].astype(BF16)
        _accumulate(dbsr_ref, _dot(ub, lrb, TN), first)
        _accumulate(dbsi_ref, _dot(ub, lib, TN), first)
        du_ref[...] = dy * d_ref[...] + _dot(lrb, bsr_ref[...], NT) + _dot(lib, bsi_ref[...], NT)

    halo = pl.BlockSpec((8, STATE_TILE), lambda j, c: (jnp.maximum(cidx(c) * T8 - 1, 0), j))
    lvspec = pl.BlockSpec((LV, STATE_TILE), lambda j, c: (0, j))
    pwspec = pl.BlockSpec((T, STATE_TILE), lambda j, c: (0, j))
    f = lambda *s: jax.ShapeDtypeStruct(s, F32)
    return _pcall(
        body, name="s5_bwd", grid=(GT, nc),
        in_specs=[S["chan"], S["u"], S["state"], S["state"], halo, halo, S["chan"], S["bmat"], S["bmat"],
                  lvspec, lvspec, pwspec, pwspec, S["cmat"], S["cmat"], S["cvec"], S["gmat"], S["cvec"]],
        out_specs=[S["chan"], S["bmat"], S["bmat"], S["cmat"], S["cmat"], S["cvec"], S["gmat"], S["cvec"],
                   S["svec"], S["svec"]],
        out_shape=[f(L, SW), f(GT, LANES, STATE_TILE), f(GT, LANES, STATE_TILE), f(GT, STATE_TILE, LANES),
                   f(GT, STATE_TILE, LANES), f(1, SW), f(GT, LANES, LANES), f(1, SW), f(1, NS), f(1, NS)],
        scratch=[pltpu.VMEM((8, STATE_TILE), F32), pltpu.VMEM((T, STATE_TILE), F32),
                 pltpu.VMEM((T, STATE_TILE), F32)],
    )(do, p, s_re, s_im, s_re, s_im, ypre, sp["bsr"], sp["bsi"], sp["levr"], sp["levi"],
      sp["rpowr"], sp["rpowi"], sp["crT"], sp["ciT"], sp["d"], sp["wg"], sp["gb"])


def _conv_taps(xc, h6, h7, row):
    x1 = jnp.where(row == 0, h7, pltpu.roll(xc, 1, 0))
    x2 = jnp.where(row == 0, h6, jnp.where(row == 1, h7, pltpu.roll(xc, 2, 0)))
    return x1, x2


def _conv_halves(up_ref, halo_ref, w_ref, b_ref, tm, FS):
    first = pl.program_id(1) == 0
    row = lax.broadcasted_iota(jnp.int32, (tm, FS), 0)
    outs, taps = [], []
    for s in (0, 1):
        xc = up_ref[s]
        h6 = jnp.where(first, 0.0, halo_ref[s, 6:7, :])
        h7 = jnp.where(first, 0.0, halo_ref[s, 7:8, :])
        x1, x2 = _conv_taps(xc, h6, h7, row)
        outs.append(b_ref[s] + x2 * w_ref[s, 0:1, :] + x1 * w_ref[s, 1:2, :] + xc * w_ref[s, 2:3, :])
        taps.append((x2, x1, xc))
    return outs, taps


def _convglu_specs(L, FS, tm):
    t8 = tm // 8
    return dict(
        up=pl.BlockSpec((2, None, tm, FS), lambda j, i: (0, j, i, 0)),
        halo=pl.BlockSpec((2, None, 8, FS), lambda j, i: (0, j, jnp.maximum(i * t8 - 1, 0), 0)),
        w=pl.BlockSpec((2, None, 3, FS), lambda j, i: (0, j, 0, 0)),
        b=pl.BlockSpec((2, None, 1, FS), lambda j, i: (0, j, 0, 0)),
        act=pl.BlockSpec((None, tm, FS), lambda j, i: (j, i, 0)),
    )


def _convglu_fwd(up, w, b):
    _, NSH, L, FS = up.shape
    tm = _row_tile(L)
    S = _convglu_specs(L, FS, tm)

    def body(up_ref, halo_ref, w_ref, b_ref, act_ref):
        (val, gate), _ = _conv_halves(up_ref, halo_ref, w_ref, b_ref, tm, FS)
        act_ref[...] = _glu_math(val, gate).astype(BF16)

    return _pcall(body, name="convglu_fwd", grid=(NSH, L // tm), in_specs=[S["up"], S["halo"], S["w"], S["b"]],
                  out_specs=S["act"], out_shape=jax.ShapeDtypeStruct((NSH, L, FS), BF16))(up, up, w, b)


def _convglu_bwd(up, dact, w, b):
    _, NSH, L, FS = up.shape
    tm = _row_tile(L)
    S = _convglu_specs(L, FS, tm)

    def body(up_ref, halo_ref, w_ref, b_ref, dact_ref, dc_ref, dw_ref, db_ref):
        first = pl.program_id(1) == 0
        (val, gate), taps = _conv_halves(up_ref, halo_ref, w_ref, b_ref, tm, FS)
        _, vjp = jax.vjp(_glu_math, val, gate)
        dcs = vjp(dact_ref[...])
        for s in (0, 1):
            dc = dcs[s]
            dc_ref[s] = dc
            sums = [jnp.sum(dc * t, axis=0, keepdims=True) for t in taps[s]]
            dbs = jnp.sum(dc, axis=0, keepdims=True)

            @pl.when(first)
            def _():
                for t in range(3):
                    dw_ref[s, t:t + 1, :] = sums[t]
                db_ref[s] = dbs

            @pl.when(jnp.logical_not(first))
            def _():
                for t in range(3):
                    dw_ref[s, t:t + 1, :] += sums[t]
                db_ref[s] += dbs

    f = lambda *s: jax.ShapeDtypeStruct(s, F32)
    return _pcall(body, name="convglu_bwd", grid=(NSH, L // tm),
                  in_specs=[S["up"], S["halo"], S["w"], S["b"], S["act"]],
                  out_specs=[S["up"], S["w"], S["b"]],
                  out_shape=[f(2, NSH, L, FS), f(2, NSH, 3, FS), f(2, NSH, 1, FS)])(up, up, w, b, dact)


def _conv_transpose(dc, w):
    NS8, L, FS = dc.shape
    tm = _row_tile(L)
    t8 = tm // 8
    nt = L // tm

    def body(dc_ref, nxt_ref, w_ref, o_ref):
        last = pl.program_id(1) == nt - 1
        row = lax.broadcasted_iota(jnp.int32, (tm, FS), 0)
        xc = dc_ref[...]
        n0 = jnp.where(last, 0.0, nxt_ref[0:1, :])
        n1 = jnp.where(last, 0.0, nxt_ref[1:2, :])
        x1 = jnp.where(row == tm - 1, n0, pltpu.roll(xc, tm - 1, 0))
        x2 = jnp.where(row == tm - 1, n1, jnp.where(row == tm - 2, n0, pltpu.roll(xc, tm - 2, 0)))
        o_ref[...] = (xc * w_ref[2:3, :] + x1 * w_ref[1:2, :] + x2 * w_ref[0:1, :]).astype(BF16)

    blk = pl.BlockSpec((None, tm, FS), lambda j, i: (j, i, 0))
    nxt = pl.BlockSpec((None, 8, FS), lambda j, i: (j, jnp.minimum((i + 1) * t8, L // 8 - 1), 0))
    wsp = pl.BlockSpec((None, 3, FS), lambda j, i: (j, 0, 0))
    return _pcall(body, name="conv_transpose", grid=(NS8, nt), in_specs=[blk, nxt, wsp], out_specs=blk,
                  out_shape=jax.ShapeDtypeStruct((NS8, L, FS), BF16))(dc, dc, w)


def _loss_head(y, target):
    L, D = y.shape
    tm = _row_tile(L)

    def body(y_ref, t_ref, loss_ref, dy_ref):
        err = y_ref[...] - t_ref[...]
        dy_ref[...] = err / D
        part = 0.5 * jnp.sum(jnp.mean(err * err, axis=-1, keepdims=True), axis=0, keepdims=True)
        _accumulate(loss_ref, jnp.broadcast_to(part, (1, LANES)), pl.program_id(0) == 0)

    row = pl.BlockSpec((tm, D), lambda i: (i, 0))
    vec = pl.BlockSpec((1, LANES), lambda i: (0, 0))
    return _pcall(body, name="loss_head", grid=(L // tm,), in_specs=[row, row], out_specs=[vec, row],
                  out_shape=[jax.ShapeDtypeStruct((1, LANES), F32), jax.ShapeDtypeStruct((L, D), F32)])(y, target)


def _ada_fwd(c_all, ada_w, ada_b):
    NL, D, NC = ada_w.shape
    NB = c_all.shape[0]

    def body(c_ref, w_ref, b_ref, o_ref):
        cact = jax.nn.silu(c_ref[...])
        o_ref[...] = _dot(cact.astype(BF16), w_ref[...].astype(BF16)) + b_ref[...]

    return _pcall(body, name="ada_fwd", grid=(NL,),
                  in_specs=[pl.BlockSpec((NB, D), lambda l: (0, 0)), pl.BlockSpec((None, D, NC), lambda l: (l, 0, 0)),
                            pl.BlockSpec((None, 1, NC), lambda l: (l, 0, 0))],
                  out_specs=pl.BlockSpec((None, NB, NC), lambda l: (l, 0, 0)),
                  out_shape=jax.ShapeDtypeStruct((NL, NB, NC), F32))(c_all, ada_w, ada_b)


def _ada_bwd(c_all_t, dmod):
    D, NB = c_all_t.shape
    NL, _, NC = dmod.shape

    def body(c_ref, d_ref, o_ref):
        cact = jax.nn.silu(c_ref[...]).astype(BF16).astype(F32)
        o_ref[...] = _dot(cact, d_ref[...].astype(BF16).astype(F32))

    return _pcall(body, name="ada_bwd", grid=(NL,),
                  in_specs=[pl.BlockSpec((D, NB), lambda l: (0, 0)), pl.BlockSpec((None, NB, NC), lambda l: (l, 0, 0))],
                  out_specs=pl.BlockSpec((None, D, NC), lambda l: (l, 0, 0)),
                  out_shape=jax.ShapeDtypeStruct((NL, D, NC), F32))(c_all_t, dmod)


def _adamw(parts, w, m, v):
    NL, P, R, C = parts.shape
    tr = R
    for cand in (256, 128, 64, 32, 16, 8):
        if R % cand == 0 and R > cand:
            tr = cand
            break

    def body(p_ref, w_ref, m_ref, v_ref, g_ref, d_ref, nm_ref, nv_ref):
        g = p_ref[0].astype(F32)
        for k in range(1, P):
            g = g + p_ref[k].astype(F32)
        m2 = ADAM_B1 * m_ref[...] + (1.0 - ADAM_B1) * g
        v2 = ADAM_B2 * v_ref[...] + (1.0 - ADAM_B2) * jnp.square(g)
        m_hat = m2 / (1.0 - ADAM_B1 ** ADAM_STEP)
        v_hat = v2 / (1.0 - ADAM_B2 ** ADAM_STEP)
        g_ref[...] = g
        d_ref[...] = -ADAM_LR * (m_hat / (jnp.sqrt(v_hat) + ADAM_EPS) + ADAM_WD * w_ref[...])
        nm_ref[...] = m2
        nv_ref[...] = v2

    pspec = pl.BlockSpec((None, P, tr, C), lambda l, i: (l, 0, i, 0))
    wspec = pl.BlockSpec((None, tr, C), lambda l, i: (l, i, 0))
    shp = jax.ShapeDtypeStruct((NL, R, C), F32)
    return _pcall(body, name="adamw", grid=(NL, R // tr), in_specs=[pspec, wspec, wspec, wspec],
                  out_specs=[wspec] * 4, out_shape=[shp] * 4)(parts, w, m, v)


def _block_diag(blocks):
    *lead, g, r, c = blocks.shape
    eye = jnp.eye(g, dtype=bool)[:, None, :, None]
    full = jnp.where(eye, blocks[..., :, :, None, :], 0.0)
    return full.reshape(*lead, g * r, g * c)


def _block_diag_extract(m, r, c):
    g = GROUPS_PER_TILE
    m5 = m.reshape(*m.shape[:-2], g, r, g, c)
    eye = jnp.eye(g, dtype=bool)[:, None, :, None]
    return jnp.sum(jnp.where(eye, m5, 0.0), axis=-2)


def kernel(x, c, ada_w, ada_b, norm1_g, w_in, q_norm_g, k_norm_g, ssm_a_re, ssm_a_im, ssm_log_dt, ssm_b_re, ssm_b_im, ssm_c_re, ssm_c_im, ssm_d, glu_w, glu_b, attn_out_g, ssm_out_g, w_out, norm2_g, ffn_w_up, ffn_conv_w, ffn_conv_b, ffn_w_down, loss_target, m_ada_w, m_ada_b, m_norm1_g, m_w_in, m_q_norm_g, m_k_norm_g, m_ssm_a_re, m_ssm_a_im, m_ssm_log_dt, m_ssm_b_re, m_ssm_b_im, m_ssm_c_re, m_ssm_c_im, m_ssm_d, m_glu_w, m_glu_b, m_attn_out_g, m_ssm_out_g, m_w_out, m_norm2_g, m_ffn_w_up, m_ffn_conv_w, m_ffn_conv_b, m_ffn_w_down, v_ada_w, v_ada_b, v_norm1_g, v_w_in, v_q_norm_g, v_k_norm_g, v_ssm_a_re, v_ssm_a_im, v_ssm_log_dt, v_ssm_b_re, v_ssm_b_im, v_ssm_c_re, v_ssm_c_im, v_ssm_d, v_glu_w, v_glu_b, v_attn_out_g, v_ssm_out_g, v_w_out, v_norm2_g, v_ffn_w_up, v_ffn_conv_w, v_ffn_conv_b, v_ffn_w_down):
    weights = dict(ada_w=ada_w, ada_b=ada_b, norm1_g=norm1_g, w_in=w_in, q_norm_g=q_norm_g, k_norm_g=k_norm_g,
                   ssm_a_re=ssm_a_re, ssm_a_im=ssm_a_im, ssm_log_dt=ssm_log_dt, ssm_b_re=ssm_b_re,
                   ssm_b_im=ssm_b_im, ssm_c_re=ssm_c_re, ssm_c_im=ssm_c_im, ssm_d=ssm_d, glu_w=glu_w, glu_b=glu_b,
                   attn_out_g=attn_out_g, ssm_out_g=ssm_out_g, w_out=w_out, norm2_g=norm2_g, ffn_w_up=ffn_w_up,
                   ffn_conv_w=ffn_conv_w, ffn_conv_b=ffn_conv_b, ffn_w_down=ffn_w_down)
    mom_m = dict(ada_w=m_ada_w, ada_b=m_ada_b, norm1_g=m_norm1_g, w_in=m_w_in, q_norm_g=m_q_norm_g,
                 k_norm_g=m_k_norm_g, ssm_a_re=m_ssm_a_re, ssm_a_im=m_ssm_a_im, ssm_log_dt=m_ssm_log_dt,
                 ssm_b_re=m_ssm_b_re, ssm_b_im=m_ssm_b_im, ssm_c_re=m_ssm_c_re, ssm_c_im=m_ssm_c_im, ssm_d=m_ssm_d,
                 glu_w=m_glu_w, glu_b=m_glu_b, attn_out_g=m_attn_out_g, ssm_out_g=m_ssm_out_g, w_out=m_w_out,
                 norm2_g=m_norm2_g, ffn_w_up=m_ffn_w_up, ffn_conv_w=m_ffn_conv_w, ffn_conv_b=m_ffn_conv_b,
                 ffn_w_down=m_ffn_w_down)
    mom_v = dict(ada_w=v_ada_w, ada_b=v_ada_b, norm1_g=v_norm1_g, w_in=v_w_in, q_norm_g=v_q_norm_g,
                 k_norm_g=v_k_norm_g, ssm_a_re=v_ssm_a_re, ssm_a_im=v_ssm_a_im, ssm_log_dt=v_ssm_log_dt,
                 ssm_b_re=v_ssm_b_re, ssm_b_im=v_ssm_b_im, ssm_c_re=v_ssm_c_re, ssm_c_im=v_ssm_c_im, ssm_d=v_ssm_d,
                 glu_w=v_glu_w, glu_b=v_glu_b, attn_out_g=v_attn_out_g, ssm_out_g=v_ssm_out_g, w_out=v_w_out,
                 norm2_g=v_norm2_g, ffn_w_up=v_ffn_w_up, ffn_conv_w=v_ffn_conv_w, ffn_conv_b=v_ffn_conv_b,
                 ffn_w_down=v_ffn_w_down)
    names = list(weights)
    big = ("ada_w", "w_in", "w_out", "ffn_w_up", "ffn_conv_w", "ffn_w_down")
    small = [n for n in names if n not in big]

    x = x[0]
    target = loss_target[0]
    L, D = x.shape
    NL = ada_w.shape[0]
    AW = D // 2
    SW = D - AW
    NH = AW // HEAD_DIM
    HP = AW // LANES
    G = SW // SSM_GROUP
    GT = SW // LANES
    NS = G * SSM_STATE
    NIN = w_in.shape[-1]
    FS = ffn_w_up.shape[-1]
    NSH = NDEV // 2
    NCA = ada_w.shape[-1]
    ROWS_OUT = w_out.shape[1]
    ROWS_DOWN = ffn_w_down.shape[1]
    B_ATT = min(L, 256)
    T_S5 = min(L, 128)
    tm = _row_tile(L, 1024)
    me = _my_index()

    cpad = jnp.reshape(c, (D // LANES, LANES))
    c_all = _gather_small(cpad, "gather_c")
    c_all = c_all.reshape(NDEV, D)
    w_in_g, w_out_g, w_up_g, conv_w_g, w_down_g = _gather_shards(
        [w_in.astype(BF16), w_out.astype(BF16), ffn_w_up.astype(BF16), ffn_conv_w, ffn_w_down.astype(BF16)],
        "gather_weights")
    w_out_g = w_out_g.reshape(NL, D, D)
    w_down_g = w_down_g.reshape(NL, NSH, 2 * ROWS_DOWN, D)
    conv_w_g = conv_w_g.reshape(NL, 2, NSH, 3, FS)
    conv_b_g = ffn_conv_b.reshape(NL, 2, NSH, 1, FS)

    ada_b_mine = lax.dynamic_slice_in_dim(ada_b, me * NCA, NCA, axis=1).reshape(NL, 1, NCA)
    mod_part = _ada_fwd(c_all, ada_w, ada_b_mine)
    mod_all = _gather_small(mod_part, "gather_mod")
    mod = lax.dynamic_index_in_dim(mod_all, me, axis=2, keepdims=False)
    mod = jnp.transpose(mod, (1, 0, 2)).reshape(NL, N_MOD, 1, D)

    row = lambda a: a.reshape(NL, 1, NS)
    ar_row, ai_row = row(ssm_a_re), row(ssm_a_im)
    ldt_row = row(jnp.broadcast_to(ssm_log_dt[:, :, None], (NL, G, SSM_STATE)))
    tiles = lambda a: a.reshape((NL, GT, GROUPS_PER_TILE) + a.shape[2:])
    braw_r = _block_diag(jnp.swapaxes(tiles(ssm_b_re), -1, -2))
    braw_i = _block_diag(jnp.swapaxes(tiles(ssm_b_im), -1, -2))
    crT = _block_diag(jnp.swapaxes(tiles(ssm_c_re), -1, -2)).astype(BF16)
    ciT = _block_diag(jnp.swapaxes(tiles(ssm_c_im), -1, -2)).astype(BF16)
    wg = _block_diag(tiles(glu_w)).astype(BF16)
    abr, abi, levr, levi, powr, powi, bsr, bsi = _s5_prep(ar_row, ai_row, ldt_row, braw_r, braw_i, T_S5)
    s5p = dict(bsr=bsr, bsi=bsi, levr=levr, levi=levi, powr=powr, powi=powi,
               rpowr=jnp.flip(powr, axis=1), rpowi=jnp.flip(powi, axis=1), crT=crT, ciT=ciT,
               d=ssm_d.reshape(NL, 1, SW), wg=wg, gb=glu_b.reshape(NL, 1, SW))

    gqk = jnp.concatenate([jnp.tile(q_norm_g, (1, NH)), jnp.tile(k_norm_g, (1, NH))], axis=1).reshape(NL, 1, 2 * AW)
    layer_params = dict(
        mod=mod, norm1_g=norm1_g.reshape(NL, 1, D), norm2_g=norm2_g.reshape(NL, 1, D), gqk=gqk,
        ga=attn_out_g.reshape(NL, 1, AW), gs=ssm_out_g.reshape(NL, 1, SW),
        w_in=w_in_g, w_out=w_out_g, w_up=w_up_g, conv_w=conv_w_g, conv_b=conv_b_g, w_down=w_down_g, s5=s5p)

    ucol0 = 3 * AW // LANES

    def mm_rows_shards(name, a, b, n, out_dtype=F32):
        K = a.shape[1]
        return _matmul(name, a, b, dims=NN, grid=(L // tm, NDEV),
                       a_spec=pl.BlockSpec((tm, K), lambda i, j: (i, 0)),
                       b_spec=pl.BlockSpec((None, K, n), lambda i, j: (j, 0, 0)),
                       out_shape=jax.ShapeDtypeStruct((L, NDEV * n), out_dtype),
                       out_specs=pl.BlockSpec((tm, n), lambda i, j: (i, j)))

    tn = min(D, 512)

    def resid_epilogue(acc, xres, gate):
        return acc, xres + gate * acc

    def layer_fwd(xin, lp):
        sh1, sc1, g1, sh2, sc2, g2 = (lp["mod"][k] for k in range(N_MOD))
        h = _lnmod_fwd(xin, lp["norm1_g"], sh1, sc1)
        p = mm_rows_shards("mm_in", h, lp["w_in"], NIN)
        qk = _qknorm_fwd(p, lp["gqk"], AW)
        o_attn, tot = _attn_fwd(qk, p, AW, B_ATT)
        o_ssm, s_re, s_im, ypre = _s5_fwd(p, lp["s5"], SW, ucol0, T_S5)
        o = _outnorm_fwd(o_attn, o_ssm, lp["ga"], lp["gs"])
        a1, x_mid = _matmul(
            "mm_out", o, lp["w_out"], dims=NN, grid=(L // tm, D // tn),
            a_spec=pl.BlockSpec((tm, D), lambda i, j: (i, 0)), b_spec=pl.BlockSpec((D, tn), lambda i, j: (0, j)),
            extra=(xin, g1), extra_specs=(pl.BlockSpec((tm, tn), lambda i, j: (i, j)),
                                          pl.BlockSpec((1, tn), lambda i, j: (0, j))),
            epilogue=resid_epilogue,
            out_shape=[jax.ShapeDtypeStruct((L, D), F32)] * 2,
            out_specs=[pl.BlockSpec((tm, tn), lambda i, j: (i, j))] * 2)
        h2 = _lnmod_fwd(x_mid, lp["norm2_g"], sh2, sc2)
        up = _matmul(
            "mm_up", h2, lp["w_up"], dims=NN, grid=(L // tm, NDEV),
            a_spec=pl.BlockSpec((tm, D), lambda i, j: (i, 0)), b_spec=pl.BlockSpec((None, D, FS), lambda i, j: (j, 0, 0)),
            out_shape=jax.ShapeDtypeStruct((NDEV, L, FS), F32),
            out_specs=pl.BlockSpec((None, tm, FS), lambda i, j: (j, i, 0)))
        up = up.reshape(2, NSH, L, FS)
        act = _convglu_fwd(up, lp["conv_w"], lp["conv_b"])
        a2, x_out = _matmul(
            "mm_down", act, lp["w_down"], dims=NN, grid=(L // tm, D // tn, NSH), kaxis=2, acc_shape=(tm, tn),
            a_spec=pl.BlockSpec((None, tm, FS), lambda i, j, k: (k, i, 0)),
            b_spec=pl.BlockSpec((None, FS, tn), lambda i, j, k: (k, 0, j)),
            extra=(x_mid, g2), extra_specs=(pl.BlockSpec((tm, tn), lambda i, j, k: (i, j)),
                                            pl.BlockSpec((1, tn), lambda i, j, k: (0, j))),
            epilogue=resid_epilogue,
            out_shape=[jax.ShapeDtypeStruct((L, D), F32)] * 2,
            out_specs=[pl.BlockSpec((tm, tn), lambda i, j, k: (i, j))] * 2)
        res = dict(x=xin, h=h, p=p, qk=qk, tot=tot, o_attn=o_attn, o_ssm=o_ssm, s_re=s_re, s_im=s_im, ypre=ypre,
                   o=o, a1=a1, x_mid=x_mid, h2=h2, up=up, act=act, a2=a2)
        return x_out, res

    per_layer = [jax.tree.map(lambda a: a[l], layer_params) for l in range(NL)]
    y, residuals = x, []
    for l in range(NL):
        y, res = layer_fwd(y, per_layer[l])
        residuals.append(res)

    loss_row, dy = _loss_head(y, target)
    loss = lax.psum(loss_row[0, 0], ("x", "y", "c"))

    def layer_bwd(dx, args):
        lp, r = args
        sh1, sc1, g1, sh2, sc2, g2 = (lp["mod"][k] for k in range(N_MOD))
        da2, dg2 = _gate_bwd(dx, r["a2"], g2)
        dact = _matmul(
            "mm_dact", da2, lp["w_down"], dims=NT, grid=(L // tm, NSH),
            a_spec=pl.BlockSpec((tm, D), lambda i, j: (i, 0)), b_spec=pl.BlockSpec((None, FS, D), lambda i, j: (j, 0, 0)),
            out_shape=jax.ShapeDtypeStruct((NSH, L, FS), F32),
            out_specs=pl.BlockSpec((None, tm, FS), lambda i, j: (j, i, 0)))
        dw_down = _matmul(
            "mm_dw_down", r["act"], da2, dims=TN, grid=(NSH, D // tn, L // tm), kaxis=2, acc_shape=(FS, tn),
            a_spec=pl.BlockSpec((None, tm, FS), lambda j, n, k: (j, k, 0)),
            b_spec=pl.BlockSpec((tm, tn), lambda j, n, k: (k, n)),
            out_shape=jax.ShapeDtypeStruct((NSH, FS, D), BF16),
            out_specs=pl.BlockSpec((None, FS, tn), lambda j, n, k: (j, 0, n)))
        dconv, dcw, dcb = _convglu_bwd(r["up"], dact, lp["conv_w"], lp["conv_b"])
        dup = _conv_transpose(dconv.reshape(NDEV, L, FS), lp["conv_w"].reshape(NDEV, 3, FS))
        dh2 = _matmul(
            "mm_dh2", dup, lp["w_up"], dims=NT, grid=(L // tm, NDEV), kaxis=1, acc_shape=(tm, D),
            a_spec=pl.BlockSpec((None, tm, FS), lambda i, k: (k, i, 0)),
            b_spec=pl.BlockSpec((None, D, FS), lambda i, k: (k, 0, 0)),
            out_shape=jax.ShapeDtypeStruct((L, D), F32), out_specs=pl.BlockSpec((tm, D), lambda i, k: (i, 0)))
        dw_up = _matmul(
            "mm_dw_up", r["h2"], dup, dims=TN, grid=(NDEV, L // tm), kaxis=1, acc_shape=(D, FS),
            a_spec=pl.BlockSpec((tm, D), lambda j, k: (k, 0)), b_spec=pl.BlockSpec((None, tm, FS), lambda j, k: (j, k, 0)),
            out_shape=jax.ShapeDtypeStruct((NDEV, D, FS), BF16),
            out_specs=pl.BlockSpec((None, D, FS), lambda j, k: (j, 0, 0)))
        dxm, dn2, dsh2, dsc2 = _lnmod_bwd(dh2, r["x_mid"], lp["norm2_g"], sh2, sc2, dx)
        da1, dg1 = _gate_bwd(dxm, r["a1"], g1)
        do = _matmul(
            "mm_do", da1, lp["w_out"], dims=NT, grid=(L // tm, D // tn),
            a_spec=pl.BlockSpec((tm, D), lambda i, j: (i, 0)), b_spec=pl.BlockSpec((tn, D), lambda i, j: (j, 0)),
            out_shape=jax.ShapeDtypeStruct((L, D), F32), out_specs=pl.BlockSpec((tm, tn), lambda i, j: (i, j)))
        dw_out = _matmul(
            "mm_dw_out", r["o"], da1, dims=TN, grid=(D // tn, D // tn, L // tm), kaxis=2, acc_shape=(tn, tn),
            a_spec=pl.BlockSpec((tm, tn), lambda m, n, k: (k, m)), b_spec=pl.BlockSpec((tm, tn), lambda m, n, k: (k, n)),
            out_shape=jax.ShapeDtypeStruct((D, D), BF16), out_specs=pl.BlockSpec((tn, tn), lambda m, n, k: (m, n)))
        doa, dos, dga, dgs = _outnorm_bwd(do, r["o_attn"], r["o_ssm"], lp["ga"], lp["gs"])
        (du, dbsr, dbsi, dcr, dci, dd, dwg, dgb, dabr, dabi) = _s5_bwd(
            dos, r["p"], r["s_re"], r["s_im"], r["ypre"], lp["s5"], SW, ucol0, T_S5)
        dq, dk, dv = _attn_bwd(r["qk"], r["p"], doa, r["tot"], AW, B_ATT)
        dqk, dgqk = _qknorm_bwd(jnp.concatenate([dq, dk], axis=1), r["p"], lp["gqk"], AW)
        dp = jnp.concatenate([dqk, dv, du], axis=1).astype(BF16)
        dh = _matmul(
            "mm_dh", dp, lp["w_in"], dims=NT, grid=(L // tm, NDEV), kaxis=1, acc_shape=(tm, D),
            a_spec=pl.BlockSpec((tm, NIN), lambda i, k: (i, k)), b_spec=pl.BlockSpec((None, D, NIN), lambda i, k: (k, 0, 0)),
            out_shape=jax.ShapeDtypeStruct((L, D), F32), out_specs=pl.BlockSpec((tm, D), lambda i, k: (i, 0)))
        dw_in = _matmul(
            "mm_dw_in", r["h"], dp, dims=TN, grid=(NDEV, L // tm), kaxis=1, acc_shape=(D, NIN),
            a_spec=pl.BlockSpec((tm, D), lambda j, k: (k, 0)), b_spec=pl.BlockSpec((tm, NIN), lambda j, k: (k, j)),
            out_shape=jax.ShapeDtypeStruct((NDEV, D, NIN), BF16),
            out_specs=pl.BlockSpec((None, D, NIN), lambda j, k: (j, 0, 0)))
        dx0, dn1, dsh1, dsc1 = _lnmod_bwd(dh, r["x"], lp["norm1_g"], sh1, sc1, dxm)
        grads = dict(
            dmod=jnp.concatenate([dsh1, dsc1, dg1, dsh2, dsc2, dg2], axis=1), dn1=dn1, dn2=dn2, dgqk=dgqk,
            dga=dga, dgs=dgs, dbsr=dbsr, dbsi=dbsi, dcr=dcr, dci=dci, dd=dd, dwg=dwg, dgb=dgb, dabr=dabr, dabi=dabi,
            dcb=dcb, dw_in=dw_in, dw_out=dw_out.reshape(NDEV, ROWS_OUT, D), dw_up=dw_up,
            dcw=dcw.reshape(NDEV, 3, FS), dw_down=dw_down.reshape(NDEV, ROWS_DOWN, D))
        return dx0, grads

    grad_x, layer_grads = dy, [None] * NL
    for l in reversed(range(NL)):
        grad_x, layer_grads[l] = layer_bwd(grad_x, (per_layer[l], residuals[l]))
    gr = jax.tree.map(lambda *a: jnp.stack(a), *layer_grads)

    dar, dai, dldt, dbr_bd, dbi_bd = _s5_prep_bwd(ar_row, ai_row, ldt_row, braw_r, braw_i,
                                                  gr["dabr"], gr["dabi"], gr["dbsr"], gr["dbsi"])
    unt = lambda a: a.reshape((NL, G) + a.shape[3:])
    local = dict(
        ada_b=gr["dmod"].reshape(NL, N_MOD * D),
        norm1_g=gr["dn1"].reshape(NL, D), norm2_g=gr["dn2"].reshape(NL, D),
        q_norm_g=gr["dgqk"].reshape(NL, 2, NH, HEAD_DIM)[:, 0].sum(axis=1),
        k_norm_g=gr["dgqk"].reshape(NL, 2, NH, HEAD_DIM)[:, 1].sum(axis=1),
        ssm_a_re=dar.reshape(NL, G, SSM_STATE), ssm_a_im=dai.reshape(NL, G, SSM_STATE),
        ssm_log_dt=dldt.reshape(NL, G, SSM_STATE).sum(axis=-1),
        ssm_b_re=jnp.swapaxes(unt(_block_diag_extract(dbr_bd, SSM_GROUP, SSM_STATE)), -1, -2),
        ssm_b_im=jnp.swapaxes(unt(_block_diag_extract(dbi_bd, SSM_GROUP, SSM_STATE)), -1, -2),
        ssm_c_re=jnp.swapaxes(unt(_block_diag_extract(gr["dcr"], SSM_STATE, SSM_GROUP)), -1, -2),
        ssm_c_im=jnp.swapaxes(unt(_block_diag_extract(gr["dci"], SSM_STATE, SSM_GROUP)), -1, -2),
        ssm_d=gr["dd"].reshape(NL, G, SSM_GROUP),
        glu_w=unt(_block_diag_extract(gr["dwg"], SSM_GROUP, SSM_GROUP)),
        glu_b=gr["dgb"].reshape(NL, G, SSM_GROUP),
        attn_out_g=gr["dga"].reshape(NL, AW), ssm_out_g=gr["dgs"].reshape(NL, SW),
        ffn_conv_b=gr["dcb"].reshape(NL, 2 * NSH * FS),
    )

    def pack(tree):
        flat = jnp.concatenate([tree[n].reshape(-1) for n in small])
        pad = (-flat.shape[0]) % (512 * LANES)
        return jnp.pad(flat, (0, pad)).reshape(1, -1, LANES)

    (small_parts,) = _gather_shards([pack(local)], "gather_small_grads")
    sg, sd, sm, sv = _adamw(small_parts, pack(weights), pack(mom_m), pack(mom_v))

    def unpack(buf):
        flat = buf.reshape(-1)
        out, off = {}, 0
        for n in small:
            size = weights[n].size
            out[n] = flat[off:off + size].reshape(weights[n].shape)
            off += size
        return out

    ug, ud, um, uv = unpack(sg), unpack(sd), unpack(sm), unpack(sv)
    results = {n: (ug[n], ud[n], um[n], uv[n]) for n in small}

    dmod_all = _gather_small(gr["dmod"].reshape(NL, N_MOD * D), "gather_dmod")
    dmod_mine = lax.dynamic_slice_in_dim(dmod_all, me * NCA, NCA, axis=2)
    d_ada_w = _ada_bwd(jnp.transpose(c_all), jnp.transpose(dmod_mine, (1, 0, 2)))
    results["ada_w"] = tuple(_adamw(d_ada_w[:, None], ada_w, m_ada_w, v_ada_w))

    by_target = [jnp.swapaxes(g.reshape((NL, NDEV // 2, 2) + g.shape[2:]), 1, 2)
                 for g in (gr["dw_in"], gr["dw_out"], gr["dw_up"], gr["dcw"], gr["dw_down"])]
    mine, theirs = _sibling_exchange(by_target, "pair_weight_grads")
    pair_sums = [_pair_add(a, b) for a, b in zip(mine, theirs)]
    parts = _chip_exchange(pair_sums, "scatter_weight_grads")
    for n, pt in zip(("w_in", "w_out", "ffn_w_up", "ffn_conv_w", "ffn_w_down"), parts):
        results[n] = tuple(_adamw(pt, weights[n], mom_m[n], mom_v[n]))

    out = [loss, grad_x[None]]
    for k in range(4):
        out.extend(results[n][k] for n in names)
    return tuple(out)
```

```python
import functools
import math

import jax
import jax.numpy as jnp
from jax import lax
from jax.experimental import pallas as pl
from jax.experimental.pallas import tpu as pltpu

F32 = jnp.float32
BF16 = jnp.bfloat16
NDEV = 8
LANES = 128
HEAD_DIM = 64
SSM_GROUP = 16
SSM_STATE = 64
GROUPS_PER_TILE = LANES // SSM_GROUP
STATE_TILE = GROUPS_PER_TILE * SSM_STATE
N_MOD = 6
ATTN_STRIP = 32
EPS = 1e-6
ADAM_LR, ADAM_B1, ADAM_B2, ADAM_EPS, ADAM_WD, ADAM_STEP = 0.001, 0.9, 0.999, 1e-08, 0.01, 10
VMEM_LIMIT = 48 * 1024 * 1024
MESH_IDS = pl.DeviceIdType.MESH

NN = (((1,), (0,)), ((), ()))
NT = (((1,), (1,)), ((), ()))
TN = (((0,), (0,)), ((), ()))


def _dot(a, b, dims=NN):
    return lax.dot_general(a, b, dims, preferred_element_type=F32)


def _pcall(body, *, name, out_shape, in_specs, out_specs, grid=(), scratch=()):
    return pl.pallas_call(
        body, name=name, grid=grid, in_specs=in_specs, out_specs=out_specs, out_shape=out_shape,
        scratch_shapes=list(scratch),
        compiler_params=pltpu.CompilerParams(vmem_limit_bytes=VMEM_LIMIT))


def _row_tile(n, want=512):
    t = min(n, want)
    assert n % t == 0
    return t


def _my_index():
    return 4 * lax.axis_index("x") + 2 * lax.axis_index("y") + lax.axis_index("c")


HBM_SPEC = pl.BlockSpec(memory_space=pltpu.HBM)


def _mesh_place():
    x, y, c = lax.axis_index("x"), lax.axis_index("y"), lax.axis_index("c")
    chips = [(1 - x, y), (x, 1 - y), (1 - x, 1 - y)]
    return x, y, c, chips


def _gather_small(arr, name):
    def body(in_ref, out_ref, send_sems, recv_sems, local_sem):
        x, y, c, _ = _mesh_place()
        me = 4 * x + 2 * y + c
        own = pltpu.make_async_copy(in_ref, out_ref.at[me], local_sem)
        own.start()
        sends, recvs = [], []
        for k in range(1, NDEV):
            px = 1 - x if k & 4 else x
            py = 1 - y if k & 2 else y
            pc = 1 - c if k & 1 else c
            common = dict(send_sem=send_sems.at[k - 1], recv_sem=recv_sems.at[k - 1],
                          device_id=(px, py, pc), device_id_type=MESH_IDS)
            snd = pltpu.make_async_remote_copy(src_ref=in_ref, dst_ref=out_ref.at[me], **common)
            snd.start()
            sends.append(snd)
            recvs.append(pltpu.make_async_remote_copy(
                src_ref=in_ref, dst_ref=out_ref.at[4 * px + 2 * py + pc], **common))
        for r in recvs:
            r.wait_recv()
        for s in sends:
            s.wait_send()
        own.wait()

    return pl.pallas_call(
        body, name=name, out_shape=jax.ShapeDtypeStruct((NDEV,) + arr.shape, arr.dtype),
        in_specs=[HBM_SPEC], out_specs=HBM_SPEC,
        scratch_shapes=[pltpu.SemaphoreType.DMA((NDEV - 1,)), pltpu.SemaphoreType.DMA((NDEV - 1,)),
                        pltpu.SemaphoreType.DMA(())],
    )(arr)


def _start_by_layer(make, nl):
    for l in range(nl):
        make(l).start()


def _gather_shards(arrs, name):
    n = len(arrs)
    nl = arrs[0].shape[0]

    def body(*refs):
        ins, outs = refs[:n], refs[n:2 * n]
        send_sems, recv_sems, local_sems = refs[2 * n:]
        x, y, c, chips = _mesh_place()
        dev = lambda px, py, pc: 4 * px + 2 * py + pc

        def copy(k, a, src, block, to, layer=slice(None)):
            return pltpu.make_async_remote_copy(
                src_ref=src.at[layer], dst_ref=outs[a].at[layer, block], send_sem=send_sems.at[k, a],
                recv_sem=recv_sems.at[k, a], device_id=to, device_id_type=MESH_IDS)

        me = dev(x, y, c)
        for a in range(n):
            _start_by_layer(lambda l: pltpu.make_async_copy(ins[a].at[l], outs[a].at[l, me], local_sems.at[a]), nl)
        sent = []
        for a in range(n):
            _start_by_layer(lambda l: copy(0, a, ins[a], me, (x, y, 1 - c), l), nl)
            sent.append(copy(0, a, ins[a], me, (x, y, 1 - c)))
        for j, (px, py) in enumerate(chips):
            for a in range(n):
                _start_by_layer(lambda l: copy(1 + j, a, ins[a], me, (px, py, c), l), nl)
                sent.append(copy(1 + j, a, ins[a], me, (px, py, c)))
        for j, (px, py) in enumerate(chips):
            for a in range(n):
                blk = dev(px, py, c)
                copy(1 + j, a, ins[a], blk, (x, y, c)).wait_recv()
                got = outs[a].at[:, blk]
                _start_by_layer(lambda l: copy(4 + j, a, got, blk, (x, y, 1 - c), l), nl)
                sent.append(copy(4 + j, a, got, blk, (x, y, 1 - c)))
        for a in range(n):
            copy(0, a, ins[a], dev(x, y, 1 - c), (x, y, c)).wait_recv()
        for j, (px, py) in enumerate(chips):
            for a in range(n):
                copy(4 + j, a, ins[a], dev(px, py, 1 - c), (x, y, c)).wait_recv()
        for s in sent:
            s.wait_send()
        for a in range(n):
            pltpu.make_async_copy(ins[a], outs[a].at[:, me], local_sems.at[a]).wait()

    out_shape = [jax.ShapeDtypeStruct((a.shape[0], NDEV) + a.shape[1:], a.dtype) for a in arrs]
    return list(pl.pallas_call(
        body, name=name, out_shape=out_shape, in_specs=[HBM_SPEC] * n, out_specs=[HBM_SPEC] * n,
        scratch_shapes=[pltpu.SemaphoreType.DMA((7, n)), pltpu.SemaphoreType.DMA((7, n)),
                        pltpu.SemaphoreType.DMA((n,))],
    )(*arrs))


def _sibling_exchange(arrs, name):
    n = len(arrs)
    nl, nchips = arrs[0].shape[:2]

    def body(*refs):
        ins, got = refs[:n], refs[n:2 * n]
        send_sems, recv_sems = refs[2 * n:]
        x, y, c, _ = _mesh_place()

        def send(a, idx):
            return pltpu.make_async_remote_copy(
                src_ref=ins[a].at[idx], dst_ref=got[a].at[idx], send_sem=send_sems.at[a],
                recv_sem=recv_sems.at[a], device_id=(x, y, 1 - c), device_id_type=MESH_IDS)

        everything = (slice(None), slice(None))
        for a in range(n):
            for q in range(nchips):
                _start_by_layer(lambda l: send(a, (l, q)), nl)
        for a in range(n):
            send(a, everything).wait_recv()
        for a in range(n):
            send(a, everything).wait_send()

    shapes = [jax.ShapeDtypeStruct(a.shape, a.dtype) for a in arrs]
    return list(pl.pallas_call(
        body, name=name, out_shape=shapes, in_specs=[HBM_SPEC] * n, out_specs=[HBM_SPEC] * n,
        scratch_shapes=[pltpu.SemaphoreType.DMA((n,)), pltpu.SemaphoreType.DMA((n,))],
    )(*arrs))


def _chip_exchange(arrs, name):
    n = len(arrs)
    nl = arrs[0].shape[0]

    def body(*refs):
        ins, outs = refs[:n], refs[n:2 * n]
        send_sems, recv_sems, local_sems = refs[2 * n:]
        x, y, c, chips = _mesh_place()
        mine = 2 * x + y

        def copy(j, a, target, source, layer=slice(None)):
            px, py = chips[j]
            return pltpu.make_async_remote_copy(
                src_ref=ins[a].at[layer, target], dst_ref=outs[a].at[layer, source], send_sem=send_sems.at[j, a],
                recv_sem=recv_sems.at[j, a], device_id=(px, py, c), device_id_type=MESH_IDS)

        for a in range(n):
            _start_by_layer(lambda l: pltpu.make_async_copy(ins[a].at[l, mine], outs[a].at[l, mine], local_sems.at[a]), nl)
        for j, (px, py) in enumerate(chips):
            for a in range(n):
                _start_by_layer(lambda l: copy(j, a, 2 * px + py, mine, l), nl)
        for j, (px, py) in enumerate(chips):
            for a in range(n):
                copy(j, a, mine, 2 * px + py).wait_recv()
        for j, (px, py) in enumerate(chips):
            for a in range(n):
                copy(j, a, 2 * px + py, mine).wait_send()
        for a in range(n):
            pltpu.make_async_copy(ins[a].at[:, mine], outs[a].at[:, mine], local_sems.at[a]).wait()

    out_shape = [jax.ShapeDtypeStruct(a.shape, a.dtype) for a in arrs]
    return list(pl.pallas_call(
        body, name=name, out_shape=out_shape, in_specs=[HBM_SPEC] * n, out_specs=[HBM_SPEC] * n,
        scratch_shapes=[pltpu.SemaphoreType.DMA((3, n)), pltpu.SemaphoreType.DMA((3, n)),
                        pltpu.SemaphoreType.DMA((n,))],
    )(*arrs))


def _pair_add(a, b):
    NL, NC, R, C = a.shape
    tr = R
    for cand in (256, 128, 64, 32, 16):
        if R % cand == 0 and R > cand:
            tr = cand
            break

    def body(a_ref, b_ref, o_ref):
        o_ref[...] = (a_ref[...].astype(F32) + b_ref[...].astype(F32)).astype(o_ref.dtype)

    spec = pl.BlockSpec((None, NC, tr, C), lambda l, i: (l, 0, i, 0))
    return _pcall(body, name="pair_add", grid=(NL, R // tr), in_specs=[spec, spec], out_specs=spec,
                  out_shape=jax.ShapeDtypeStruct(a.shape, a.dtype))(a, b)


def _matmul(name, a, b, *, dims, grid, a_spec, b_spec, out_shape, out_specs, kaxis=None, acc_shape=None,
            extra=(), extra_specs=(), epilogue=None):
    nk = grid[kaxis] if kaxis is not None else 1
    ne = len(extra)
    multi = isinstance(out_shape, (list, tuple))
    n_out = len(out_shape) if multi else 1

    def body(*refs):
        a_ref, b_ref = refs[0], refs[1]
        ex = refs[2:2 + ne]
        outs = refs[2 + ne:2 + ne + n_out]

        def write(res):
            vals = epilogue(res, *[e[...] for e in ex]) if epilogue is not None else (res,)
            for o, v in zip(outs, vals):
                o[...] = v.astype(o.dtype)

        part = _dot(a_ref[...].astype(BF16), b_ref[...].astype(BF16), dims)
        if nk == 1:
            write(part)
        else:
            acc = refs[-1]
            k = pl.program_id(kaxis)

            @pl.when(k == 0)
            def _():
                acc[...] = part

            @pl.when(k > 0)
            def _():
                acc[...] += part

            @pl.when(k == nk - 1)
            def _():
                write(acc[...])

    scratch = [pltpu.VMEM(acc_shape, F32)] if nk > 1 else []
    return _pcall(body, name=name, grid=grid, in_specs=[a_spec, b_spec, *extra_specs],
                  out_specs=out_specs, out_shape=out_shape, scratch=scratch)(a, b, *extra)


def _rms(x, g):
    inv = lax.rsqrt(jnp.mean(x * x, axis=-1, keepdims=True) + EPS)
    return x * inv * g


def _lnmod_math(x, g, sh, sc):
    return _rms(x, g) * (1.0 + sc) + sh


def _qkn_math(p, g):
    m0 = lax.broadcasted_iota(jnp.int32, (1, LANES), 1) < HEAD_DIM
    sq = p * p
    s0 = jnp.sum(jnp.where(m0, sq, 0.0), axis=-1, keepdims=True)
    s1 = jnp.sum(jnp.where(m0, 0.0, sq), axis=-1, keepdims=True)
    inv = jnp.where(m0, lax.rsqrt(s0 / HEAD_DIM + EPS), lax.rsqrt(s1 / HEAD_DIM + EPS))
    return p * inv * g


def _glu_math(val, gate):
    return jax.nn.gelu(gate) * val


def _accumulate(ref, val, first):
    @pl.when(first)
    def _():
        ref[...] = val

    @pl.when(jnp.logical_not(first))
    def _():
        ref[...] += val


def _lnmod_fwd(x, g, sh, sc):
    L, D = x.shape
    tm = _row_tile(L)

    def body(x_ref, g_ref, sh_ref, sc_ref, h_ref):
        h_ref[...] = _lnmod_math(x_ref[...], g_ref[...], sh_ref[...], sc_ref[...]).astype(BF16)

    row = pl.BlockSpec((tm, D), lambda i: (i, 0))
    vec = pl.BlockSpec((1, D), lambda i: (0, 0))
    return _pcall(body, name="lnmod_fwd", grid=(L // tm,), in_specs=[row, vec, vec, vec], out_specs=row,
                  out_shape=jax.ShapeDtypeStruct((L, D), BF16))(x, g, sh, sc)


def _lnmod_bwd(dh, x, g, sh, sc, dres):
    L, D = x.shape
    tm = _row_tile(L)

    def body(dh_ref, x_ref, g_ref, sh_ref, sc_ref, res_ref, dx_ref, dg_ref, dsh_ref, dsc_ref):
        _, vjp = jax.vjp(_lnmod_math, x_ref[...], g_ref[...], sh_ref[...], sc_ref[...])
        dx, dg, dsh, dsc = vjp(dh_ref[...])
        dx_ref[...] = dx + res_ref[...]
        first = pl.program_id(0) == 0
        _accumulate(dg_ref, dg, first)
        _accumulate(dsh_ref, dsh, first)
        _accumulate(dsc_ref, dsc, first)

    row = pl.BlockSpec((tm, D), lambda i: (i, 0))
    vec = pl.BlockSpec((1, D), lambda i: (0, 0))
    vs = jax.ShapeDtypeStruct((1, D), F32)
    return _pcall(body, name="lnmod_bwd", grid=(L // tm,), in_specs=[row, row, vec, vec, vec, row],
                  out_specs=[row, vec, vec, vec],
                  out_shape=[jax.ShapeDtypeStruct((L, D), F32), vs, vs, vs])(dh, x, g, sh, sc, dres)


def _gate_bwd(dx, a, gate):
    L, D = dx.shape
    tm = _row_tile(L)

    def body(dx_ref, a_ref, g_ref, da_ref, dg_ref):
        dxv = dx_ref[...]
        da_ref[...] = (g_ref[...] * dxv).astype(BF16)
        _accumulate(dg_ref, jnp.sum(dxv * a_ref[...], axis=0, keepdims=True), pl.program_id(0) == 0)

    row = pl.BlockSpec((tm, D), lambda i: (i, 0))
    vec = pl.BlockSpec((1, D), lambda i: (0, 0))
    return _pcall(body, name="gate_bwd", grid=(L // tm,), in_specs=[row, row, vec], out_specs=[row, vec],
                  out_shape=[jax.ShapeDtypeStruct((L, D), BF16), jax.ShapeDtypeStruct((1, D), F32)])(dx, a, gate)


def _qknorm_fwd(p, gqk, AW):
    L = p.shape[0]
    tm = _row_tile(L)
    ncol = 2 * AW // LANES

    def body(p_ref, g_ref, o_ref):
        o_ref[...] = _qkn_math(p_ref[...], g_ref[...]).astype(BF16)

    blk = pl.BlockSpec((tm, LANES), lambda i, j: (i, j))
    vec = pl.BlockSpec((1, LANES), lambda i, j: (0, j))
    return _pcall(body, name="qknorm_fwd", grid=(L // tm, ncol), in_specs=[blk, vec], out_specs=blk,
                  out_shape=jax.ShapeDtypeStruct((L, 2 * AW), BF16))(p, gqk)


def _qknorm_bwd(dqk, p, gqk, AW):
    L = p.shape[0]
    tm = _row_tile(L)
    ncol = 2 * AW // LANES

    def body(d_ref, p_ref, g_ref, dp_ref, dg_ref):
        _, vjp = jax.vjp(_qkn_math, p_ref[...], g_ref[...])
        dp, dg = vjp(d_ref[...])
        dp_ref[...] = dp
        _accumulate(dg_ref, dg, pl.program_id(1) == 0)

    blk = pl.BlockSpec((tm, LANES), lambda j, i: (i, j))
    vec = pl.BlockSpec((1, LANES), lambda j, i: (0, j))
    return _pcall(body, name="qknorm_bwd", grid=(ncol, L // tm), in_specs=[blk, blk, vec], out_specs=[blk, vec],
                  out_shape=[jax.ShapeDtypeStruct((L, 2 * AW), F32),
                             jax.ShapeDtypeStruct((1, 2 * AW), F32)])(dqk, p, gqk)


def _outnorm_fwd(oa, os_, ga, gs):
    L, AW = oa.shape
    SW = os_.shape[1]
    tm = _row_tile(L)

    def body(oa_ref, os_ref, ga_ref, gs_ref, o_ref):
        o_ref[:, :AW] = _rms(oa_ref[...], ga_ref[...]).astype(BF16)
        o_ref[:, AW:] = _rms(os_ref[...], gs_ref[...]).astype(BF16)

    ra = pl.BlockSpec((tm, AW), lambda i: (i, 0))
    rs = pl.BlockSpec((tm, SW), lambda i: (i, 0))
    va = pl.BlockSpec((1, AW), lambda i: (0, 0))
    vs = pl.BlockSpec((1, SW), lambda i: (0, 0))
    ro = pl.BlockSpec((tm, AW + SW), lambda i: (i, 0))
    return _pcall(body, name="outnorm_fwd", grid=(L // tm,), in_specs=[ra, rs, va, vs], out_specs=ro,
                  out_shape=jax.ShapeDtypeStruct((L, AW + SW), BF16))(oa, os_, ga, gs)


def _outnorm_bwd(do, oa, os_, ga, gs):
    L, AW = oa.shape
    SW = os_.shape[1]
    tm = _row_tile(L)

    def body(do_ref, oa_ref, os_ref, ga_ref, gs_ref, doa_ref, dos_ref, dga_ref, dgs_ref):
        first = pl.program_id(0) == 0
        _, vjp_a = jax.vjp(_rms, oa_ref[...], ga_ref[...])
        doa, dga = vjp_a(do_ref[:, :AW])
        _, vjp_s = jax.vjp(_rms, os_ref[...], gs_ref[...])
        dos, dgs = vjp_s(do_ref[:, AW:])
        doa_ref[...] = doa
        dos_ref[...] = dos
        _accumulate(dga_ref, dga, first)
        _accumulate(dgs_ref, dgs, first)

    ra = pl.BlockSpec((tm, AW), lambda i: (i, 0))
    rs = pl.BlockSpec((tm, SW), lambda i: (i, 0))
    va = pl.BlockSpec((1, AW), lambda i: (0, 0))
    vs = pl.BlockSpec((1, SW), lambda i: (0, 0))
    ro = pl.BlockSpec((tm, AW + SW), lambda i: (i, 0))
    return _pcall(body, name="outnorm_bwd", grid=(L // tm,), in_specs=[ro, ra, rs, va, vs],
                  out_specs=[ra, rs, va, vs],
                  out_shape=[jax.ShapeDtypeStruct((L, AW), F32), jax.ShapeDtypeStruct((L, SW), F32),
                             jax.ShapeDtypeStruct((1, AW), F32), jax.ShapeDtypeStruct((1, SW), F32)])(
                                 do, oa, os_, ga, gs)


def _softplus_neg_abs(z):
    return jnp.log(1.0 + jnp.exp(-jnp.abs(z)))


def _split_dot(x, m):
    hi = x.astype(BF16)
    lo = (x - hi.astype(F32)).astype(BF16)
    return _dot(hi, m) + _dot(lo, m)


def _attn_masks(B):
    row = lax.broadcasted_iota(jnp.int32, (B, B), 0)
    col = lax.broadcasted_iota(jnp.int32, (B, B), 1)
    strict = col < row
    upper = jnp.where(row > col, 1.0, 0.0).astype(BF16)
    lower = jnp.where(row < col, 1.0, 0.0).astype(BF16)
    return strict, upper, lower


def _attn_fwd(qk, p, AW, B):
    L = qk.shape[0]
    HP = AW // LANES
    nb = L // B

    def body(q_ref, k_ref, v_ref, o_ref, tot_ref, *scr):
        lb_scr = (scr[0:2], scr[2:4])
        tail_scr = (scr[4:6], scr[6:8])
        sum_scr = scr[8:10]
        i = pl.program_id(1)
        m0 = lax.broadcasted_iota(jnp.int32, (1, LANES), 1) < HEAD_DIM
        strict, upper, _ = _attn_masks(B)
        q = q_ref[...] * 0.125
        zq = jnp.zeros_like(q)
        qh = (jnp.where(m0, q, zq), jnp.where(m0, zq, q))

        def keys(j):
            start = pl.multiple_of(jnp.maximum(j, 0) * B, B)
            return k_ref[pl.ds(start, B), :]

        def vals(j):
            start = pl.multiple_of(jnp.maximum(j, 0) * B, B)
            return v_ref[pl.ds(start, B), :].astype(BF16)

        strips = [slice(s, min(s + ATTN_STRIP, B)) for s in range(0, B, ATTN_STRIP)]

        def scores(j):
            kj = keys(j)
            return tuple(_dot(qh[h], kj, NT) for h in (0, 1))

        def logits(zs2, slot, diag):
            for h in (0, 1):
                z = zs2[h]
                his, los = [], []
                for rows in strips:
                    zs = z[rows]
                    lb = jnp.minimum(zs, 0.0) - _softplus_neg_abs(zs)
                    l1 = lb - zs
                    if diag:
                        l1 = jnp.where(strict[rows], l1, 0.0)
                    lb_scr[slot][h][rows, :] = lb
                    hi = l1.astype(BF16)
                    his.append(hi)
                    los.append((l1 - hi.astype(F32)).astype(BF16))
                    rsum = jnp.sum(l1, axis=-1, keepdims=True)
                    if h == 0:
                        sum_scr[slot][rows, :] = jnp.broadcast_to(rsum, (rows.stop - rows.start, LANES))
                    else:
                        sum_scr[slot][rows, :] = jnp.where(m0, sum_scr[slot][rows, :], rsum)
                cat = lambda xs: jnp.concatenate(xs, axis=0)
                tail_scr[slot][h][...] = _dot(cat(his), upper) + _dot(cat(los), upper)

        def attend(j, slot, diag):
            vj = vals(j)
            pv = []
            for h in (0, 1):
                ws = []
                for rows in strips:
                    w = jnp.exp(lb_scr[slot][h][rows, :] + tail_scr[slot][h][rows, :])
                    if diag:
                        w = jnp.where(strict[rows], w, 0.0)
                    ws.append(w.astype(BF16))
                pv.append(_dot(jnp.concatenate(ws, axis=0), vj))
            return jnp.where(m0, pv[0], pv[1])

        logits(scores(i), 0, True)
        o_ref[...] = attend(i, 0, True)
        tot_ref[...] = sum_scr[0][...]

        def half(j, slot):
            z = scores(j - 1)
            pv = attend(j, slot, False)
            logits(z, 1 - slot, False)
            o_ref[...] += pv * jnp.exp(tot_ref[...])
            tot_ref[...] += sum_scr[slot][...]

        @pl.when(i > 0)
        def _():
            logits(scores(i - 1), 1, False)

            @pl.loop(0, (i + 1) // 2)
            def _(t):
                j = i - 1 - 2 * t
                half(j, 1)

                @pl.when(j > 0)
                def _():
                    half(j - 1, 0)


    qspec = pl.BlockSpec((B, LANES), lambda hp, i: (i, hp))
    kspec = pl.BlockSpec((L, LANES), lambda hp, i: (0, HP + hp))
    vspec = pl.BlockSpec((L, LANES), lambda hp, i: (0, 2 * HP + hp))
    ospec = pl.BlockSpec((B, LANES), lambda hp, i: (i, hp))
    shp = jax.ShapeDtypeStruct((L, AW), F32)
    return _pcall(body, name="attn_fwd", grid=(HP, nb), in_specs=[qspec, kspec, vspec],
                  out_specs=[ospec, ospec], out_shape=[shp, shp],
                  scratch=[pltpu.VMEM((B, B), F32)] * 8 + [pltpu.VMEM((B, LANES), F32)] * 2)(qk, qk, p)


def _attn_bwd(qk, p, do, tot, AW, B):
    L = qk.shape[0]
    HP = AW // LANES
    nb = L // B

    def body(q_ref, k_ref, v_ref, do_ref, tot_ref, dq_ref, dk_ref, dv_ref, *scr):
        lb_scr, tail_scr, dw_scr, e_scr, beta_scr = scr[0:2], scr[2:4], scr[4:6], scr[6:8], scr[8:10]
        dos_scr, bsum_scr, left_scr, esum_scr, ecum_scr = scr[10:15]
        i = pl.program_id(1)
        m0 = lax.broadcasted_iota(jnp.int32, (1, LANES), 1) < HEAD_DIM
        strict, upper, lower = _attn_masks(B)
        strips = [slice(s, min(s + ATTN_STRIP, B)) for s in range(0, B, ATTN_STRIP)]
        cat = lambda xs: jnp.concatenate(xs, axis=0)

        @pl.when(i == 0)
        def _():
            dk_ref[...] = jnp.zeros_like(dk_ref)
            dv_ref[...] = jnp.zeros_like(dv_ref)

        q = q_ref[...] * 0.125
        zq = jnp.zeros_like(q)
        qh = (jnp.where(m0, q, zq), jnp.where(m0, zq, q))
        heads = lambda a: (jnp.where(m0, a, jnp.zeros_like(a)), jnp.where(m0, jnp.zeros_like(a), a))
        left_scr[...] = jnp.zeros_like(left_scr)
        ecum_scr[...] = jnp.zeros_like(ecum_scr)
        dq_ref[...] = jnp.zeros_like(dq_ref)

        def block_rows(j):
            return pl.ds(pl.multiple_of(j * B, B), B)

        def put_row_sums(ref, rows, h, rsum):
            if h == 0:
                ref[rows, :] = jnp.broadcast_to(rsum, (rows.stop - rows.start, LANES))
            else:
                ref[rows, :] = jnp.where(m0, ref[rows, :], rsum)

        def head_cols(x, h):
            other = pltpu.roll(x, HEAD_DIM, 1)
            full = jnp.where(m0, x, other) if h == 0 else jnp.where(m0, other, x)
            return jnp.concatenate([full] * (B // LANES), axis=1) if B > LANES else full

        def scores(j):
            kj = k_ref[block_rows(j), :]
            return tuple(_dot(qh[h], kj, NT) for h in (0, 1))

        def stage_a(j, zs2, diag):
            for h in (0, 1):
                his, los = [], []
                for rows in strips:
                    zs = zs2[h][rows]
                    lb = jnp.minimum(zs, 0.0) - _softplus_neg_abs(zs)
                    l1 = lb - zs
                    if diag:
                        l1 = jnp.where(strict[rows], l1, 0.0)
                    lb_scr[h][rows, :] = lb
                    hi = l1.astype(BF16)
                    his.append(hi)
                    los.append((l1 - hi.astype(F32)).astype(BF16))
                    put_row_sums(bsum_scr, rows, h, jnp.sum(l1, axis=-1, keepdims=True))
                tail_scr[h][...] = _dot(cat(his), upper) + _dot(cat(los), upper)
            bs = bsum_scr[...]
            scale = jnp.exp(tot_ref[...] - left_scr[...] - bs)
            left_scr[...] += bs
            dos = (do_ref[...] * scale).astype(BF16)
            dos_scr[...] = dos
            vj = v_ref[block_rows(j), :].astype(BF16)
            dosh = heads(dos)
            for h in (0, 1):
                dw_scr[h][...] = _dot(dosh[h], vj, NT)

        def stage_b(j, diag):
            dosh = heads(dos_scr[...])
            pres = []
            dv_blk = jnp.zeros((B, LANES), F32)
            for h in (0, 1):
                ehs, els, wbs = [], [], []
                for rows in strips:
                    lb = lb_scr[h][rows, :]
                    w = jnp.exp(lb + tail_scr[h][rows, :])
                    if diag:
                        w = jnp.where(strict[rows], w, 0.0)
                    beta_scr[h][rows, :] = jnp.exp(lb)
                    e = dw_scr[h][rows, :] * w
                    e_scr[h][rows, :] = e
                    ehi = e.astype(BF16)
                    ehs.append(ehi)
                    els.append((e - ehi.astype(F32)).astype(BF16))
                    wbs.append(w.astype(BF16))
                    put_row_sums(esum_scr, rows, h, jnp.sum(e, axis=-1, keepdims=True))
                pres.append(_dot(cat(ehs), lower) + _dot(cat(els), lower))
                dv_blk = dv_blk + _dot(cat(wbs), dosh[h], TN)
            dv_ref[block_rows(j), :] += dv_blk
            return pres

        def stage_c(j, pres, diag):
            kh = heads(k_ref[block_rows(j), :])
            dq = jnp.zeros((B, LANES), F32)
            dk_blk = jnp.zeros((B, LANES), F32)
            for h in (0, 1):
                dzs = []
                for rows in strips:
                    e = e_scr[h][rows, :]
                    dl1 = pres[h][rows] + head_cols(ecum_scr[rows, :], h)
                    dz = e - beta_scr[h][rows, :] * (e + dl1)
                    if diag:
                        dz = jnp.where(strict[rows], dz, 0.0)
                    dzs.append(dz.astype(BF16))
                dzb = cat(dzs)
                dq = dq + _dot(dzb, kh[h])
                dk_blk = dk_blk + _dot(dzb, qh[h], TN)
            dq_ref[...] += dq
            dk_ref[block_rows(j), :] += dk_blk
            ecum_scr[...] += esum_scr[...]

        def step(t, diag_next):
            z = scores(t + 1)
            pres = stage_b(t, False)
            stage_a(t + 1, z, diag_next)
            stage_c(t, pres, False)

        @pl.when(i > 0)
        def _():
            stage_a(0, scores(0), False)

            @pl.loop(0, i - 1)
            def _(t):
                step(t, False)

            step(i - 1, True)

        @pl.when(i == 0)
        def _():
            stage_a(0, scores(0), True)

        stage_c(i, stage_b(i, True), True)
        dq_ref[...] = dq_ref[...] * 0.125

    qspec = pl.BlockSpec((B, LANES), lambda hp, i: (i, hp))
    kspec = pl.BlockSpec((L, LANES), lambda hp, i: (0, HP + hp))
    vspec = pl.BlockSpec((L, LANES), lambda hp, i: (0, 2 * HP + hp))
    full = pl.BlockSpec((L, LANES), lambda hp, i: (0, hp))
    shp = jax.ShapeDtypeStruct((L, AW), F32)
    return _pcall(body, name="attn_bwd", grid=(HP, nb), in_specs=[qspec, kspec, vspec, qspec, qspec],
                  out_specs=[qspec, full, full], out_shape=[shp, shp, shp],
                  scratch=[pltpu.VMEM((B, B), F32)] * 10 + [pltpu.VMEM((B, LANES), BF16)]
                  + [pltpu.VMEM((B, LANES), F32)] * 4)(qk, qk, p, do, tot)


def _s5_disc(ar, ai, ldt):
    dt = jnp.exp(ldt)
    mag = jnp.exp(dt * ar)
    abr = mag * jnp.cos(dt * ai)
    abi = mag * jnp.sin(dt * ai)
    emr = abr - 1.0
    emi = abi
    den = ar * ar + ai * ai
    fr = (emr * ar + emi * ai) / den
    fi = (emi * ar - emr * ai) / den
    return abr, abi, fr, fi


def _s5_params_math(ar, ai, ldt, br, bi):
    abr, abi, fr, fi = _s5_disc(ar, ai, ldt)
    return abr, abi, fr * br - fi * bi, fr * bi + fi * br


def _scan_rows(xr, xi, lev, T, reverse):
    row = lax.broadcasted_iota(jnp.int32, xr.shape, 0)
    d, k = 1, 0
    while d < T:
        kr, ki = lev(k)
        if reverse:
            keep = row < T - d
            sr = jnp.where(keep, pltpu.roll(xr, T - d, 0), 0.0)
            si = jnp.where(keep, pltpu.roll(xi, T - d, 0), 0.0)
            ki = -ki
        else:
            keep = row >= d
            sr = jnp.where(keep, pltpu.roll(xr, d, 0), 0.0)
            si = jnp.where(keep, pltpu.roll(xi, d, 0), 0.0)
        xr, xi = xr + (kr * sr - ki * si), xi + (kr * si + ki * sr)
        d, k = 2 * d, k + 1
    return xr, xi


def _s5_prep(ar, ai, ldt, braw_r, braw_i, T):
    NL, _, NS = ar.shape
    GT = NS // STATE_TILE
    nlev = max(1, int(math.log2(T)))
    LV = 8 * ((nlev + 7) // 8)

    def body(ar_ref, ai_ref, ldt_ref, br_ref, bi_ref,
             abr_ref, abi_ref, levr_ref, levi_ref, powr_ref, powi_ref, bsr_ref, bsi_ref):
        levr_ref[...] = jnp.zeros_like(levr_ref)
        levi_ref[...] = jnp.zeros_like(levi_ref)
        for t in range(GT):
            sl = slice(t * STATE_TILE, (t + 1) * STATE_TILE)
            abr, abi, bsr, bsi = _s5_params_math(ar_ref[:, sl], ai_ref[:, sl], ldt_ref[:, sl],
                                                 br_ref[t], bi_ref[t])
            abr_ref[:, sl] = abr
            abi_ref[:, sl] = abi
            bsr_ref[t] = bsr.astype(BF16)
            bsi_ref[t] = bsi.astype(BF16)
            kr, ki = abr, abi
            for k in range(nlev):
                levr_ref[k:k + 1, sl] = kr
                levi_ref[k:k + 1, sl] = ki
                kr, ki = kr * kr - ki * ki, 2.0 * kr * ki
        for s in range(NS // LANES):
            sl = slice(s * LANES, (s + 1) * LANES)
            row = lax.broadcasted_iota(jnp.int32, (T, LANES), 0)
            xr = jnp.where(row == 0, abr_ref[:, sl], 0.0)
            xi = jnp.where(row == 0, abi_ref[:, sl], 0.0)
            lev = lambda k, sl=sl: (levr_ref[k:k + 1, sl], levi_ref[k:k + 1, sl])
            xr, xi = _scan_rows(xr, xi, lev, T, False)
            powr_ref[:, sl] = xr
            powi_ref[:, sl] = xi

    rowspec = pl.BlockSpec((None, 1, NS), lambda l: (l, 0, 0))
    bspec = pl.BlockSpec((None, GT, LANES, STATE_TILE), lambda l: (l, 0, 0, 0))
    levspec = pl.BlockSpec((None, LV, NS), lambda l: (l, 0, 0))
    powspec = pl.BlockSpec((None, T, NS), lambda l: (l, 0, 0))
    rs = jax.ShapeDtypeStruct((NL, 1, NS), F32)
    ls = jax.ShapeDtypeStruct((NL, LV, NS), F32)
    ps = jax.ShapeDtypeStruct((NL, T, NS), F32)
    bs = jax.ShapeDtypeStruct((NL, GT, LANES, STATE_TILE), BF16)
    return _pcall(body, name="s5_prep", grid=(NL,), in_specs=[rowspec] * 3 + [bspec] * 2,
                  out_specs=[rowspec, rowspec, levspec, levspec, powspec, powspec, bspec, bspec],
                  out_shape=[rs, rs, ls, ls, ps, ps, bs, bs])(ar, ai, ldt, braw_r, braw_i)


def _s5_prep_bwd(ar, ai, ldt, braw_r, braw_i, dabr, dabi, dbsr, dbsi):
    NL, _, NS = ar.shape
    GT = NS // STATE_TILE

    def body(ar_ref, ai_ref, ldt_ref, br_ref, bi_ref, dabr_ref, dabi_ref, dbsr_ref, dbsi_ref,
             dar_ref, dai_ref, dldt_ref, dbr_ref, dbi_ref):
        for t in range(GT):
            sl = slice(t * STATE_TILE, (t + 1) * STATE_TILE)
            _, vjp = jax.vjp(_s5_params_math, ar_ref[:, sl], ai_ref[:, sl], ldt_ref[:, sl],
                             br_ref[t], bi_ref[t])
            dar, dai, dldt, dbr, dbi = vjp((dabr_ref[:, sl], dabi_ref[:, sl], dbsr_ref[t], dbsi_ref[t]))
            dar_ref[:, sl] = dar
            dai_ref[:, sl] = dai
            dldt_ref[:, sl] = dldt
            dbr_ref[t] = dbr
            dbi_ref[t] = dbi

    rowspec = pl.BlockSpec((None, 1, NS), lambda l: (l, 0, 0))
    bspec = pl.BlockSpec((None, GT, LANES, STATE_TILE), lambda l: (l, 0, 0, 0))
    rs = jax.ShapeDtypeStruct((NL, 1, NS), F32)
    bs = jax.ShapeDtypeStruct((NL, GT, LANES, STATE_TILE), F32)
    return _pcall(body, name="s5_prep_bwd", grid=(NL,), in_specs=[rowspec] * 3 + [bspec] * 2 + [rowspec] * 2 + [bspec] * 2,
                  out_specs=[rowspec] * 3 + [bspec] * 2, out_shape=[rs, rs, rs, bs, bs])(
                      ar, ai, ldt, braw_r, braw_i, dabr, dabi, dbsr, dbsi)


def _s5_specs(L, SW, T, ucol0, rev):
    GT = SW // LANES
    nc = L // T
    cidx = (lambda c: nc - 1 - c) if rev else (lambda c: c)
    return dict(
        GT=GT, nc=nc, cidx=cidx,
        u=pl.BlockSpec((T, LANES), lambda j, c: (cidx(c), ucol0 + j)),
        chan=pl.BlockSpec((T, LANES), lambda j, c: (cidx(c), j)),
        state=pl.BlockSpec((T, STATE_TILE), lambda j, c: (cidx(c), j)),
        bmat=pl.BlockSpec((None, LANES, STATE_TILE), lambda j, c: (j, 0, 0)),
        cmat=pl.BlockSpec((None, STATE_TILE, LANES), lambda j, c: (j, 0, 0)),
        gmat=pl.BlockSpec((None, LANES, LANES), lambda j, c: (j, 0, 0)),
        cvec=pl.BlockSpec((1, LANES), lambda j, c: (0, j)),
        svec=pl.BlockSpec((1, STATE_TILE), lambda j, c: (0, j)),
    )


def _s5_fwd(p, sp, SW, ucol0, T):
    L = p.shape[0]
    S = _s5_specs(L, SW, T, ucol0, False)
    NS = S["GT"] * STATE_TILE
    LV = sp["levr"].shape[0]

    def body(u_ref, bsr_ref, bsi_ref, levr_ref, levi_ref, powr_ref, powi_ref, cr_ref, ci_ref,
             d_ref, wg_ref, gb_ref, o_ref, sr_ref, si_ref, y_ref, carry_ref):
        @pl.when(pl.program_id(1) == 0)
        def _():
            carry_ref[...] = jnp.zeros_like(carry_ref)

        u = u_ref[...]
        ub = u.astype(BF16)
        bur = _dot(ub, bsr_ref[...])
        bui = _dot(ub, bsi_ref[...])
        for s in range(STATE_TILE // LANES):
            sl = slice(s * LANES, (s + 1) * LANES)
            lev = lambda k, sl=sl: (levr_ref[k:k + 1, sl], levi_ref[k:k + 1, sl])
            xr, xi = _scan_rows(bur[:, sl], bui[:, sl], lev, T, False)
            cr, ci = carry_ref[0:1, sl], carry_ref[1:2, sl]
            pr, pi = powr_ref[:, sl], powi_ref[:, sl]
            sr_ref[:, sl] = xr + (pr * cr - pi * ci)
            si_ref[:, sl] = xi + (pr * ci + pi * cr)
            carry_ref[0:1, sl] = sr_ref[T - 1:T, sl]
            carry_ref[1:2, sl] = si_ref[T - 1:T, sl]
        y = (_dot(sr_ref[...].astype(BF16), cr_ref[...]) - _dot(si_ref[...].astype(BF16), ci_ref[...])
             + d_ref[...] * u)
        y_ref[...] = y
        yg = jax.nn.gelu(y)
        gate = jax.nn.sigmoid(_dot(yg.astype(BF16), wg_ref[...]) + gb_ref[...])
        o_ref[...] = yg * gate

    lvspec = pl.BlockSpec((LV, STATE_TILE), lambda j, c: (0, j))
    pwspec = pl.BlockSpec((T, STATE_TILE), lambda j, c: (0, j))
    cs = jax.ShapeDtypeStruct((L, SW), F32)
    ss = jax.ShapeDtypeStruct((L, NS), F32)
    return _pcall(
        body, name="s5_fwd", grid=(S["GT"], S["nc"]),
        in_specs=[S["u"], S["bmat"], S["bmat"], lvspec, lvspec, pwspec, pwspec, S["cmat"], S["cmat"],
                  S["cvec"], S["gmat"], S["cvec"]],
        out_specs=[S["chan"], S["state"], S["state"], S["chan"]], out_shape=[cs, ss, ss, cs],
        scratch=[pltpu.VMEM((8, STATE_TILE), F32)],
    )(p, sp["bsr"], sp["bsi"], sp["levr"], sp["levi"], sp["powr"], sp["powi"], sp["crT"], sp["ciT"],
      sp["d"], sp["wg"], sp["gb"])


def _s5_bwd(do, p, s_re, s_im, ypre, sp, SW, ucol0, T):
    L = p.shape[0]
    S = _s5_specs(L, SW, T, ucol0, True)
    GT, nc, cidx = S["GT"], S["nc"], S["cidx"]
    NS = GT * STATE_TILE
    LV = sp["levr"].shape[0]
    T8 = T // 8

    def body(do_ref, u_ref, sr_ref, si_ref, hr_ref, hi_ref, y_ref, bsr_ref, bsi_ref, levr_ref, levi_ref,
             rpr_ref, rpi_ref, cr_ref, ci_ref, d_ref, wg_ref, gb_ref,
             du_ref, dbsr_ref, dbsi_ref, dcr_ref, dci_ref, dd_ref, dwg_ref, dgb_ref, dar_ref, dai_ref,
             carry_ref, lam_r, lam_i):
        c = pl.program_id(1)
        first = c == 0

        @pl.when(first)
        def _():
            carry_ref[...] = jnp.zeros_like(carry_ref)

        u = u_ref[...]
        ub = u.astype(BF16)
        y = y_ref[...]
        yg, gelu_vjp = jax.vjp(jax.nn.gelu, y)
        ygb = yg.astype(BF16)
        gate = jax.nn.sigmoid(_dot(ygb, wg_ref[...]) + gb_ref[...])
        dout = do_ref[...]
        dt = dout * yg * gate * (1.0 - gate)
        dtb = dt.astype(BF16)
        dyg = dout * gate + _dot(dtb, wg_ref[...], NT)
        (dy,) = gelu_vjp(dyg)
        dyb = dy.astype(BF16)
        _accumulate(dwg_ref, _dot(ygb, dtb, TN), first)
        _accumulate(dgb_ref, jnp.sum(dt, axis=0, keepdims=True), first)
        _accumulate(dd_ref, jnp.sum(dy * u, axis=0, keepdims=True), first)
        srb = sr_ref[...].astype(BF16)
        sib = si_ref[...].astype(BF16)
        _accumulate(dcr_ref, _dot(srb, dyb, TN), first)
        _accumulate(dci_ref, -_dot(sib, dyb, TN), first)
        dsr = _dot(dyb, cr_ref[...], NT)
        dsi = -_dot(dyb, ci_ref[...], NT)
        last_chunk = cidx(c) == 0
        row = lax.broadcasted_iota(jnp.int32, (T, LANES), 0)
        for s in range(STATE_TILE // LANES):
            sl = slice(s * LANES, (s + 1) * LANES)
            lev = lambda k, sl=sl: (levr_ref[k:k + 1, sl], levi_ref[k:k + 1, sl])
            xr, xi = _scan_rows(dsr[:, sl], dsi[:, sl], lev, T, True)
            cr, ci = carry_ref[0:1, sl], carry_ref[1:2, sl]
            pr, pi = rpr_ref[:, sl], -rpi_ref[:, sl]
            lam_r[:, sl] = xr + (pr * cr - pi * ci)
            lam_i[:, sl] = xi + (pr * ci + pi * cr)
            carry_ref[0:1, sl] = lam_r[0:1, sl]
            carry_ref[1:2, sl] = lam_i[0:1, sl]
            keep = jnp.logical_not(last_chunk)
            pr0 = jnp.where(keep, hr_ref[7:8, sl], 0.0)
            pi0 = jnp.where(keep, hi_ref[7:8, sl], 0.0)
            spr = jnp.where(row == 0, pr0, pltpu.roll(sr_ref[:, sl], 1, 0))
            spi = jnp.where(row == 0, pi0, pltpu.roll(si_ref[:, sl], 1, 0))
            lr, li = lam_r[:, sl], lam_i[:, sl]
            dar = jnp.sum(lr * spr + li * spi, axis=0, keepdims=True)
            dai = jnp.sum(li * spr - lr * spi, axis=0, keepdims=True)

            @pl.when(first)
            def _():
                dar_ref[:, sl] = dar
                dai_ref[:, sl] = dai

            @pl.when(jnp.logical_not(first))
            def _():
                dar_ref[:, sl] += dar
                dai_ref[:, sl] += dai

        lrb = lam_r[...].astype(BF16)
        lib = lam_i[...].astype(BF16)
        _accumulate(dbsr_ref, _dot(ub, lrb, TN), first)
        _accumulate(dbsi_ref, _dot(ub, lib, TN), first)
        du_ref[...] = dy * d_ref[...] + _dot(lrb, bsr_ref[...], NT) + _dot(lib, bsi_ref[...], NT)

    halo = pl.BlockSpec((8, STATE_TILE), lambda j, c: (jnp.maximum(cidx(c) * T8 - 1, 0), j))
    lvspec = pl.BlockSpec((LV, STATE_TILE), lambda j, c: (0, j))
    pwspec = pl.BlockSpec((T, STATE_TILE), lambda j, c: (0, j))
    f = lambda *s: jax.ShapeDtypeStruct(s, F32)
    return _pcall(
        body, name="s5_bwd", grid=(GT, nc),
        in_specs=[S["chan"], S["u"], S["state"], S["state"], halo, halo, S["chan"], S["bmat"], S["bmat"],
                  lvspec, lvspec, pwspec, pwspec, S["cmat"], S["cmat"], S["cvec"], S["gmat"], S["cvec"]],
        out_specs=[S["chan"], S["bmat"], S["bmat"], S["cmat"], S["cmat"], S["cvec"], S["gmat"], S["cvec"],
                   S["svec"], S["svec"]],
        out_shape=[f(L, SW), f(GT, LANES, STATE_TILE), f(GT, LANES, STATE_TILE), f(GT, STATE_TILE, LANES),
                   f(GT, STATE_TILE, LANES), f(1, SW), f(GT, LANES, LANES), f(1, SW), f(1, NS), f(1, NS)],
        scratch=[pltpu.VMEM((8, STATE_TILE), F32), pltpu.VMEM((T, STATE_TILE), F32),
                 pltpu.VMEM((T, STATE_TILE), F32)],
    )(do, p, s_re, s_im, s_re, s_im, ypre, sp["bsr"], sp["bsi"], sp["levr"], sp["levi"],
      sp["rpowr"], sp["rpowi"], sp["crT"], sp["ciT"], sp["d"], sp["wg"], sp["gb"])


def _conv_taps(xc, h6, h7, row):
    x1 = jnp.where(row == 0, h7, pltpu.roll(xc, 1, 0))
    x2 = jnp.where(row == 0, h6, jnp.where(row == 1, h7, pltpu.roll(xc, 2, 0)))
    return x1, x2


def _conv_halves(up_ref, halo_ref, w_ref, b_ref, tm, FS):
    first = pl.program_id(1) == 0
    row = lax.broadcasted_iota(jnp.int32, (tm, FS), 0)
    outs, taps = [], []
    for s in (0, 1):
        xc = up_ref[s]
        h6 = jnp.where(first, 0.0, halo_ref[s, 6:7, :])
        h7 = jnp.where(first, 0.0, halo_ref[s, 7:8, :])
        x1, x2 = _conv_taps(xc, h6, h7, row)
        outs.append(b_ref[s] + x2 * w_ref[s, 0:1, :] + x1 * w_ref[s, 1:2, :] + xc * w_ref[s, 2:3, :])
        taps.append((x2, x1, xc))
    return outs, taps


def _convglu_specs(L, FS, tm):
    t8 = tm // 8
    return dict(
        up=pl.BlockSpec((2, None, tm, FS), lambda j, i: (0, j, i, 0)),
        halo=pl.BlockSpec((2, None, 8, FS), lambda j, i: (0, j, jnp.maximum(i * t8 - 1, 0), 0)),
        w=pl.BlockSpec((2, None, 3, FS), lambda j, i: (0, j, 0, 0)),
        b=pl.BlockSpec((2, None, 1, FS), lambda j, i: (0, j, 0, 0)),
        act=pl.BlockSpec((None, tm, FS), lambda j, i: (j, i, 0)),
    )


def _convglu_fwd(up, w, b):
    _, NSH, L, FS = up.shape
    tm = _row_tile(L)
    S = _convglu_specs(L, FS, tm)

    def body(up_ref, halo_ref, w_ref, b_ref, act_ref):
        (val, gate), _ = _conv_halves(up_ref, halo_ref, w_ref, b_ref, tm, FS)
        act_ref[...] = _glu_math(val, gate).astype(BF16)

    return _pcall(body, name="convglu_fwd", grid=(NSH, L // tm), in_specs=[S["up"], S["halo"], S["w"], S["b"]],
                  out_specs=S["act"], out_shape=jax.ShapeDtypeStruct((NSH, L, FS), BF16))(up, up, w, b)


def _convglu_bwd(up, dact, w, b):
    _, NSH, L, FS = up.shape
    tm = _row_tile(L)
    S = _convglu_specs(L, FS, tm)

    def body(up_ref, halo_ref, w_ref, b_ref, dact_ref, dc_ref, dw_ref, db_ref):
        first = pl.program_id(1) == 0
        (val, gate), taps = _conv_halves(up_ref, halo_ref, w_ref, b_ref, tm, FS)
        _, vjp = jax.vjp(_glu_math, val, gate)
        dcs = vjp(dact_ref[...])
        for s in (0, 1):
            dc = dcs[s]
            dc_ref[s] = dc
            sums = [jnp.sum(dc * t, axis=0, keepdims=True) for t in taps[s]]
            dbs = jnp.sum(dc, axis=0, keepdims=True)

            @pl.when(first)
            def _():
                for t in range(3):
                    dw_ref[s, t:t + 1, :] = sums[t]
                db_ref[s] = dbs

            @pl.when(jnp.logical_not(first))
            def _():
                for t in range(3):
                    dw_ref[s, t:t + 1, :] += sums[t]
                db_ref[s] += dbs

    f = lambda *s: jax.ShapeDtypeStruct(s, F32)
    return _pcall(body, name="convglu_bwd", grid=(NSH, L // tm),
                  in_specs=[S["up"], S["halo"], S["w"], S["b"], S["act"]],
                  out_specs=[S["up"], S["w"], S["b"]],
                  out_shape=[f(2, NSH, L, FS), f(2, NSH, 3, FS), f(2, NSH, 1, FS)])(up, up, w, b, dact)


def _conv_transpose(dc, w):
    NS8, L, FS = dc.shape
    tm = _row_tile(L)
    t8 = tm // 8
    nt = L // tm

    def body(dc_ref, nxt_ref, w_ref, o_ref):
        last = pl.program_id(1) == nt - 1
        row = lax.broadcasted_iota(jnp.int32, (tm, FS), 0)
        xc = dc_ref[...]
        n0 = jnp.where(last, 0.0, nxt_ref[0:1, :])
        n1 = jnp.where(last, 0.0, nxt_ref[1:2, :])
        x1 = jnp.where(row == tm - 1, n0, pltpu.roll(xc, tm - 1, 0))
        x2 = jnp.where(row == tm - 1, n1, jnp.where(row == tm - 2, n0, pltpu.roll(xc, tm - 2, 0)))
        o_ref[...] = (xc * w_ref[2:3, :] + x1 * w_ref[1:2, :] + x2 * w_ref[0:1, :]).astype(BF16)

    blk = pl.BlockSpec((None, tm, FS), lambda j, i: (j, i, 0))
    nxt = pl.BlockSpec((None, 8, FS), lambda j, i: (j, jnp.minimum((i + 1) * t8, L // 8 - 1), 0))
    wsp = pl.BlockSpec((None, 3, FS), lambda j, i: (j, 0, 0))
    return _pcall(body, name="conv_transpose", grid=(NS8, nt), in_specs=[blk, nxt, wsp], out_specs=blk,
                  out_shape=jax.ShapeDtypeStruct((NS8, L, FS), BF16))(dc, dc, w)


def _loss_head(y, target):
    L, D = y.shape
    tm = _row_tile(L)

    def body(y_ref, t_ref, loss_ref, dy_ref):
        err = y_ref[...] - t_ref[...]
        dy_ref[...] = err / D
        part = 0.5 * jnp.sum(jnp.mean(err * err, axis=-1, keepdims=True), axis=0, keepdims=True)
        _accumulate(loss_ref, jnp.broadcast_to(part, (1, LANES)), pl.program_id(0) == 0)

    row = pl.BlockSpec((tm, D), lambda i: (i, 0))
    vec = pl.BlockSpec((1, LANES), lambda i: (0, 0))
    return _pcall(body, name="loss_head", grid=(L // tm,), in_specs=[row, row], out_specs=[vec, row],
                  out_shape=[jax.ShapeDtypeStruct((1, LANES), F32), jax.ShapeDtypeStruct((L, D), F32)])(y, target)


def _ada_fwd(c_all, ada_w, ada_b):
    NL, D, NC = ada_w.shape
    NB = c_all.shape[0]

    def body(c_ref, w_ref, b_ref, o_ref):
        cact = jax.nn.silu(c_ref[...])
        o_ref[...] = _dot(cact.astype(BF16), w_ref[...].astype(BF16)) + b_ref[...]

    return _pcall(body, name="ada_fwd", grid=(NL,),
                  in_specs=[pl.BlockSpec((NB, D), lambda l: (0, 0)), pl.BlockSpec((None, D, NC), lambda l: (l, 0, 0)),
                            pl.BlockSpec((None, 1, NC), lambda l: (l, 0, 0))],
                  out_specs=pl.BlockSpec((None, NB, NC), lambda l: (l, 0, 0)),
                  out_shape=jax.ShapeDtypeStruct((NL, NB, NC), F32))(c_all, ada_w, ada_b)


def _ada_bwd(c_all_t, dmod):
    D, NB = c_all_t.shape
    NL, _, NC = dmod.shape

    def body(c_ref, d_ref, o_ref):
        cact = jax.nn.silu(c_ref[...]).astype(BF16).astype(F32)
        o_ref[...] = _dot(cact, d_ref[...].astype(BF16).astype(F32))

    return _pcall(body, name="ada_bwd", grid=(NL,),
                  in_specs=[pl.BlockSpec((D, NB), lambda l: (0, 0)), pl.BlockSpec((None, NB, NC), lambda l: (l, 0, 0))],
                  out_specs=pl.BlockSpec((None, D, NC), lambda l: (l, 0, 0)),
                  out_shape=jax.ShapeDtypeStruct((NL, D, NC), F32))(c_all_t, dmod)


def _adamw(parts, w, m, v):
    NL, P, R, C = parts.shape
    tr = R
    for cand in (256, 128, 64, 32, 16, 8):
        if R % cand == 0 and R > cand:
            tr = cand
            break

    def body(p_ref, w_ref, m_ref, v_ref, g_ref, d_ref, nm_ref, nv_ref):
        g = p_ref[0].astype(F32)
        for k in range(1, P):
            g = g + p_ref[k].astype(F32)
        m2 = ADAM_B1 * m_ref[...] + (1.0 - ADAM_B1) * g
        v2 = ADAM_B2 * v_ref[...] + (1.0 - ADAM_B2) * jnp.square(g)
        m_hat = m2 / (1.0 - ADAM_B1 ** ADAM_STEP)
        v_hat = v2 / (1.0 - ADAM_B2 ** ADAM_STEP)
        g_ref[...] = g
        d_ref[...] = -ADAM_LR * (m_hat / (jnp.sqrt(v_hat) + ADAM_EPS) + ADAM_WD * w_ref[...])
        nm_ref[...] = m2
        nv_ref[...] = v2

    pspec = pl.BlockSpec((None, P, tr, C), lambda l, i: (l, 0, i, 0))
    wspec = pl.BlockSpec((None, tr, C), lambda l, i: (l, i, 0))
    shp = jax.ShapeDtypeStruct((NL, R, C), F32)
    return _pcall(body, name="adamw", grid=(NL, R // tr), in_specs=[pspec, wspec, wspec, wspec],
                  out_specs=[wspec] * 4, out_shape=[shp] * 4)(parts, w, m, v)


def _block_diag(blocks):
    *lead, g, r, c = blocks.shape
    eye = jnp.eye(g, dtype=bool)[:, None, :, None]
    full = jnp.where(eye, blocks[..., :, :, None, :], 0.0)
    return full.reshape(*lead, g * r, g * c)


def _block_diag_extract(m, r, c):
    g = GROUPS_PER_TILE
    m5 = m.reshape(*m.shape[:-2], g, r, g, c)
    eye = jnp.eye(g, dtype=bool)[:, None, :, None]
    return jnp.sum(jnp.where(eye, m5, 0.0), axis=-2)


def kernel(x, c, ada_w, ada_b, norm1_g, w_in, q_norm_g, k_norm_g, ssm_a_re, ssm_a_im, ssm_log_dt, ssm_b_re, ssm_b_im, ssm_c_re, ssm_c_im, ssm_d, glu_w, glu_b, attn_out_g, ssm_out_g, w_out, norm2_g, ffn_w_up, ffn_conv_w, ffn_conv_b, ffn_w_down, loss_target, m_ada_w, m_ada_b, m_norm1_g, m_w_in, m_q_norm_g, m_k_norm_g, m_ssm_a_re, m_ssm_a_im, m_ssm_log_dt, m_ssm_b_re, m_ssm_b_im, m_ssm_c_re, m_ssm_c_im, m_ssm_d, m_glu_w, m_glu_b, m_attn_out_g, m_ssm_out_g, m_w_out, m_norm2_g, m_ffn_w_up, m_ffn_conv_w, m_ffn_conv_b, m_ffn_w_down, v_ada_w, v_ada_b, v_norm1_g, v_w_in, v_q_norm_g, v_k_norm_g, v_ssm_a_re, v_ssm_a_im, v_ssm_log_dt, v_ssm_b_re, v_ssm_b_im, v_ssm_c_re, v_ssm_c_im, v_ssm_d, v_glu_w, v_glu_b, v_attn_out_g, v_ssm_out_g, v_w_out, v_norm2_g, v_ffn_w_up, v_ffn_conv_w, v_ffn_conv_b, v_ffn_w_down):
    weights = dict(ada_w=ada_w, ada_b=ada_b, norm1_g=norm1_g, w_in=w_in, q_norm_g=q_norm_g, k_norm_g=k_norm_g,
                   ssm_a_re=ssm_a_re, ssm_a_im=ssm_a_im, ssm_log_dt=ssm_log_dt, ssm_b_re=ssm_b_re,
                   ssm_b_im=ssm_b_im, ssm_c_re=ssm_c_re, ssm_c_im=ssm_c_im, ssm_d=ssm_d, glu_w=glu_w, glu_b=glu_b,
                   attn_out_g=attn_out_g, ssm_out_g=ssm_out_g, w_out=w_out, norm2_g=norm2_g, ffn_w_up=ffn_w_up,
                   ffn_conv_w=ffn_conv_w, ffn_conv_b=ffn_conv_b, ffn_w_down=ffn_w_down)
    mom_m = dict(ada_w=m_ada_w, ada_b=m_ada_b, norm1_g=m_norm1_g, w_in=m_w_in, q_norm_g=m_q_norm_g,
                 k_norm_g=m_k_norm_g, ssm_a_re=m_ssm_a_re, ssm_a_im=m_ssm_a_im, ssm_log_dt=m_ssm_log_dt,
                 ssm_b_re=m_ssm_b_re, ssm_b_im=m_ssm_b_im, ssm_c_re=m_ssm_c_re, ssm_c_im=m_ssm_c_im, ssm_d=m_ssm_d,
                 glu_w=m_glu_w, glu_b=m_glu_b, attn_out_g=m_attn_out_g, ssm_out_g=m_ssm_out_g, w_out=m_w_out,
                 norm2_g=m_norm2_g, ffn_w_up=m_ffn_w_up, ffn_conv_w=m_ffn_conv_w, ffn_conv_b=m_ffn_conv_b,
                 ffn_w_down=m_ffn_w_down)
    mom_v = dict(ada_w=v_ada_w, ada_b=v_ada_b, norm1_g=v_norm1_g, w_in=v_w_in, q_norm_g=v_q_norm_g,
                 k_norm_g=v_k_norm_g, ssm_a_re=v_ssm_a_re, ssm_a_im=v_ssm_a_im, ssm_log_dt=v_ssm_log_dt,
                 ssm_b_re=v_ssm_b_re, ssm_b_im=v_ssm_b_im, ssm_c_re=v_ssm_c_re, ssm_c_im=v_ssm_c_im, ssm_d=v_ssm_d,
                 glu_w=v_glu_w, glu_b=v_glu_b, attn_out_g=v_attn_out_g, ssm_out_g=v_ssm_out_g, w_out=v_w_out,
                 norm2_g=v_norm2_g, ffn_w_up=v_ffn_w_up, ffn_conv_w=v_ffn_conv_w, ffn_conv_b=v_ffn_conv_b,
                 ffn_w_down=v_ffn_w_down)
    names = list(weights)
    big = ("ada_w", "w_in", "w_out", "ffn_w_up", "ffn_conv_w", "ffn_w_down")
    small = [n for n in names if n not in big]

    x = x[0]
    target = loss_target[0]
    L, D = x.shape
    NL = ada_w.shape[0]
    AW = D // 2
    SW = D - AW
    NH = AW // HEAD_DIM
    HP = AW // LANES
    G = SW // SSM_GROUP
    GT = SW // LANES
    NS = G * SSM_STATE
    NIN = w_in.shape[-1]
    FS = ffn_w_up.shape[-1]
    NSH = NDEV // 2
    NCA = ada_w.shape[-1]
    ROWS_OUT = w_out.shape[1]
    ROWS_DOWN = ffn_w_down.shape[1]
    B_ATT = min(L, 256)
    T_S5 = min(L, 128)
    tm = _row_tile(L, 1024)
    me = _my_index()

    cpad = jnp.reshape(c, (D // LANES, LANES))
    c_all = _gather_small(cpad, "gather_c")
    c_all = c_all.reshape(NDEV, D)
    w_in_g, w_out_g, w_up_g, conv_w_g, w_down_g = _gather_shards(
        [w_in.astype(BF16), w_out.astype(BF16), ffn_w_up.astype(BF16), ffn_conv_w, ffn_w_down.astype(BF16)],
        "gather_weights")
    w_out_g = w_out_g.reshape(NL, D, D)
    w_down_g = w_down_g.reshape(NL, NSH, 2 * ROWS_DOWN, D)
    conv_w_g = conv_w_g.reshape(NL, 2, NSH, 3, FS)
    conv_b_g = ffn_conv_b.reshape(NL, 2, NSH, 1, FS)

    ada_b_mine = lax.dynamic_slice_in_dim(ada_b, me * NCA, NCA, axis=1).reshape(NL, 1, NCA)
    mod_part = _ada_fwd(c_all, ada_w, ada_b_mine)
    mod_all = _gather_small(mod_part, "gather_mod")
    mod = lax.dynamic_index_in_dim(mod_all, me, axis=2, keepdims=False)
    mod = jnp.transpose(mod, (1, 0, 2)).reshape(NL, N_MOD, 1, D)

    row = lambda a: a.reshape(NL, 1, NS)
    ar_row, ai_row = row(ssm_a_re), row(ssm_a_im)
    ldt_row = row(jnp.broadcast_to(ssm_log_dt[:, :, None], (NL, G, SSM_STATE)))
    tiles = lambda a: a.reshape((NL, GT, GROUPS_PER_TILE) + a.shape[2:])
    braw_r = _block_diag(jnp.swapaxes(tiles(ssm_b_re), -1, -2))
    braw_i = _block_diag(jnp.swapaxes(tiles(ssm_b_im), -1, -2))
    crT = _block_diag(jnp.swapaxes(tiles(ssm_c_re), -1, -2)).astype(BF16)
    ciT = _block_diag(jnp.swapaxes(tiles(ssm_c_im), -1, -2)).astype(BF16)
    wg = _block_diag(tiles(glu_w)).astype(BF16)
    abr, abi, levr, levi, powr, powi, bsr, bsi = _s5_prep(ar_row, ai_row, ldt_row, braw_r, braw_i, T_S5)
    s5p = dict(bsr=bsr, bsi=bsi, levr=levr, levi=levi, powr=powr, powi=powi,
               rpowr=jnp.flip(powr, axis=1), rpowi=jnp.flip(powi, axis=1), crT=crT, ciT=ciT,
               d=ssm_d.reshape(NL, 1, SW), wg=wg, gb=glu_b.reshape(NL, 1, SW))

    gqk = jnp.concatenate([jnp.tile(q_norm_g, (1, NH)), jnp.tile(k_norm_g, (1, NH))], axis=1).reshape(NL, 1, 2 * AW)
    layer_params = dict(
        mod=mod, norm1_g=norm1_g.reshape(NL, 1, D), norm2_g=norm2_g.reshape(NL, 1, D), gqk=gqk,
        ga=attn_out_g.reshape(NL, 1, AW), gs=ssm_out_g.reshape(NL, 1, SW),
        w_in=w_in_g, w_out=w_out_g, w_up=w_up_g, conv_w=conv_w_g, conv_b=conv_b_g, w_down=w_down_g, s5=s5p)

    ucol0 = 3 * AW // LANES

    def mm_rows_shards(name, a, b, n, out_dtype=F32):
        K = a.shape[1]
        return _matmul(name, a, b, dims=NN, grid=(L // tm, NDEV),
                       a_spec=pl.BlockSpec((tm, K), lambda i, j: (i, 0)),
                       b_spec=pl.BlockSpec((None, K, n), lambda i, j: (j, 0, 0)),
                       out_shape=jax.ShapeDtypeStruct((L, NDEV * n), out_dtype),
                       out_specs=pl.BlockSpec((tm, n), lambda i, j: (i, j)))

    tn = min(D, 512)

    def resid_epilogue(acc, xres, gate):
        return acc, xres + gate * acc

    def layer_fwd(xin, lp):
        sh1, sc1, g1, sh2, sc2, g2 = (lp["mod"][k] for k in range(N_MOD))
        h = _lnmod_fwd(xin, lp["norm1_g"], sh1, sc1)
        p = mm_rows_shards("mm_in", h, lp["w_in"], NIN)
        qk = _qknorm_fwd(p, lp["gqk"], AW)
        o_attn, tot = _attn_fwd(qk, p, AW, B_ATT)
        o_ssm, s_re, s_im, ypre = _s5_fwd(p, lp["s5"], SW, ucol0, T_S5)
        o = _outnorm_fwd(o_attn, o_ssm, lp["ga"], lp["gs"])
        a1, x_mid = _matmul(
            "mm_out", o, lp["w_out"], dims=NN, grid=(L // tm, D // tn),
            a_spec=pl.BlockSpec((tm, D), lambda i, j: (i, 0)), b_spec=pl.BlockSpec((D, tn), lambda i, j: (0, j)),
            extra=(xin, g1), extra_specs=(pl.BlockSpec((tm, tn), lambda i, j: (i, j)),
                                          pl.BlockSpec((1, tn), lambda i, j: (0, j))),
            epilogue=resid_epilogue,
            out_shape=[jax.ShapeDtypeStruct((L, D), F32)] * 2,
            out_specs=[pl.BlockSpec((tm, tn), lambda i, j: (i, j))] * 2)
        h2 = _lnmod_fwd(x_mid, lp["norm2_g"], sh2, sc2)
        up = _matmul(
            "mm_up", h2, lp["w_up"], dims=NN, grid=(L // tm, NDEV),
            a_spec=pl.BlockSpec((tm, D), lambda i, j: (i, 0)), b_spec=pl.BlockSpec((None, D, FS), lambda i, j: (j, 0, 0)),
            out_shape=jax.ShapeDtypeStruct((NDEV, L, FS), F32),
            out_specs=pl.BlockSpec((None, tm, FS), lambda i, j: (j, i, 0)))
        up = up.reshape(2, NSH, L, FS)
        act = _convglu_fwd(up, lp["conv_w"], lp["conv_b"])
        a2, x_out = _matmul(
            "mm_down", act, lp["w_down"], dims=NN, grid=(L // tm, D // tn, NSH), kaxis=2, acc_shape=(tm, tn),
            a_spec=pl.BlockSpec((None, tm, FS), lambda i, j, k: (k, i, 0)),
            b_spec=pl.BlockSpec((None, FS, tn), lambda i, j, k: (k, 0, j)),
            extra=(x_mid, g2), extra_specs=(pl.BlockSpec((tm, tn), lambda i, j, k: (i, j)),
                                            pl.BlockSpec((1, tn), lambda i, j, k: (0, j))),
            epilogue=resid_epilogue,
            out_shape=[jax.ShapeDtypeStruct((L, D), F32)] * 2,
            out_specs=[pl.BlockSpec((tm, tn), lambda i, j, k: (i, j))] * 2)
        res = dict(x=xin, h=h, p=p, qk=qk, tot=tot, o_attn=o_attn, o_ssm=o_ssm, s_re=s_re, s_im=s_im, ypre=ypre,
                   o=o, a1=a1, x_mid=x_mid, h2=h2, up=up, act=act, a2=a2)
        return x_out, res

    per_layer = [jax.tree.map(lambda a: a[l], layer_params) for l in range(NL)]
    y, residuals = x, []
    for l in range(NL):
        y, res = layer_fwd(y, per_layer[l])
        residuals.append(res)

    loss_row, dy = _loss_head(y, target)
    loss = lax.psum(loss_row[0, 0], ("x", "y", "c"))

    def layer_bwd(dx, args):
        lp, r = args
        sh1, sc1, g1, sh2, sc2, g2 = (lp["mod"][k] for k in range(N_MOD))
        da2, dg2 = _gate_bwd(dx, r["a2"], g2)
        dact = _matmul(
            "mm_dact", da2, lp["w_down"], dims=NT, grid=(L // tm, NSH),
            a_spec=pl.BlockSpec((tm, D), lambda i, j: (i, 0)), b_spec=pl.BlockSpec((None, FS, D), lambda i, j: (j, 0, 0)),
            out_shape=jax.ShapeDtypeStruct((NSH, L, FS), F32),
            out_specs=pl.BlockSpec((None, tm, FS), lambda i, j: (j, i, 0)))
        dw_down = _matmul(
            "mm_dw_down", r["act"], da2, dims=TN, grid=(NSH, D // tn, L // tm), kaxis=2, acc_shape=(FS, tn),
            a_spec=pl.BlockSpec((None, tm, FS), lambda j, n, k: (j, k, 0)),
            b_spec=pl.BlockSpec((tm, tn), lambda j, n, k: (k, n)),
            out_shape=jax.ShapeDtypeStruct((NSH, FS, D), BF16),
            out_specs=pl.BlockSpec((None, FS, tn), lambda j, n, k: (j, 0, n)))
        dconv, dcw, dcb = _convglu_bwd(r["up"], dact, lp["conv_w"], lp["conv_b"])
        dup = _conv_transpose(dconv.reshape(NDEV, L, FS), lp["conv_w"].reshape(NDEV, 3, FS))
        dh2 = _matmul(
            "mm_dh2", dup, lp["w_up"], dims=NT, grid=(L // tm, NDEV), kaxis=1, acc_shape=(tm, D),
            a_spec=pl.BlockSpec((None, tm, FS), lambda i, k: (k, i, 0)),
            b_spec=pl.BlockSpec((None, D, FS), lambda i, k: (k, 0, 0)),
            out_shape=jax.ShapeDtypeStruct((L, D), F32), out_specs=pl.BlockSpec((tm, D), lambda i, k: (i, 0)))
        dw_up = _matmul(
            "mm_dw_up", r["h2"], dup, dims=TN, grid=(NDEV, L // tm), kaxis=1, acc_shape=(D, FS),
            a_spec=pl.BlockSpec((tm, D), lambda j, k: (k, 0)), b_spec=pl.BlockSpec((None, tm, FS), lambda j, k: (j, k, 0)),
            out_shape=jax.ShapeDtypeStruct((NDEV, D, FS), BF16),
            out_specs=pl.BlockSpec((None, D, FS), lambda j, k: (j, 0, 0)))
        dxm, dn2, dsh2, dsc2 = _lnmod_bwd(dh2, r["x_mid"], lp["norm2_g"], sh2, sc2, dx)
        da1, dg1 = _gate_bwd(dxm, r["a1"], g1)
        do = _matmul(
            "mm_do", da1, lp["w_out"], dims=NT, grid=(L // tm, D // tn),
            a_spec=pl.BlockSpec((tm, D), lambda i, j: (i, 0)), b_spec=pl.BlockSpec((tn, D), lambda i, j: (j, 0)),
            out_shape=jax.ShapeDtypeStruct((L, D), F32), out_specs=pl.BlockSpec((tm, tn), lambda i, j: (i, j)))
        dw_out = _matmul(
            "mm_dw_out", r["o"], da1, dims=TN, grid=(D // tn, D // tn, L // tm), kaxis=2, acc_shape=(tn, tn),
            a_spec=pl.BlockSpec((tm, tn), lambda m, n, k: (k, m)), b_spec=pl.BlockSpec((tm, tn), lambda m, n, k: (k, n)),
            out_shape=jax.ShapeDtypeStruct((D, D), BF16), out_specs=pl.BlockSpec((tn, tn), lambda m, n, k: (m, n)))
        doa, dos, dga, dgs = _outnorm_bwd(do, r["o_attn"], r["o_ssm"], lp["ga"], lp["gs"])
        (du, dbsr, dbsi, dcr, dci, dd, dwg, dgb, dabr, dabi) = _s5_bwd(
            dos, r["p"], r["s_re"], r["s_im"], r["ypre"], lp["s5"], SW, ucol0, T_S5)
        dq, dk, dv = _attn_bwd(r["qk"], r["p"], doa, r["tot"], AW, B_ATT)
        dqk, dgqk = _qknorm_bwd(jnp.concatenate([dq, dk], axis=1), r["p"], lp["gqk"], AW)
        dp = jnp.concatenate([dqk, dv, du], axis=1).astype(BF16)
        dh = _matmul(
            "mm_dh", dp, lp["w_in"], dims=NT, grid=(L // tm, NDEV), kaxis=1, acc_shape=(tm, D),
            a_spec=pl.BlockSpec((tm, NIN), lambda i, k: (i, k)), b_spec=pl.BlockSpec((None, D, NIN), lambda i, k: (k, 0, 0)),
            out_shape=jax.ShapeDtypeStruct((L, D), F32), out_specs=pl.BlockSpec((tm, D), lambda i, k: (i, 0)))
        dw_in = _matmul(
            "mm_dw_in", r["h"], dp, dims=TN, grid=(NDEV, L // tm), kaxis=1, acc_shape=(D, NIN),
            a_spec=pl.BlockSpec((tm, D), lambda j, k: (k, 0)), b_spec=pl.BlockSpec((tm, NIN), lambda j, k: (k, j)),
            out_shape=jax.ShapeDtypeStruct((NDEV, D, NIN), BF16),
            out_specs=pl.BlockSpec((None, D, NIN), lambda j, k: (j, 0, 0)))
        dx0, dn1, dsh1, dsc1 = _lnmod_bwd(dh, r["x"], lp["norm1_g"], sh1, sc1, dxm)
        grads = dict(
            dmod=jnp.concatenate([dsh1, dsc1, dg1, dsh2, dsc2, dg2], axis=1), dn1=dn1, dn2=dn2, dgqk=dgqk,
            dga=dga, dgs=dgs, dbsr=dbsr, dbsi=dbsi, dcr=dcr, dci=dci, dd=dd, dwg=dwg, dgb=dgb, dabr=dabr, dabi=dabi,
            dcb=dcb, dw_in=dw_in, dw_out=dw_out.reshape(NDEV, ROWS_OUT, D), dw_up=dw_up,
            dcw=dcw.reshape(NDEV, 3, FS), dw_down=dw_down.reshape(NDEV, ROWS_DOWN, D))
        return dx0, grads

    grad_x, layer_grads = dy, [None] * NL
    for l in reversed(range(NL)):
        grad_x, layer_grads[l] = layer_bwd(grad_x, (per_layer[l], residuals[l]))
    gr = jax.tree.map(lambda *a: jnp.stack(a), *layer_grads)

    dar, dai, dldt, dbr_bd, dbi_bd = _s5_prep_bwd(ar_row, ai_row, ldt_row, braw_r, braw_i,
                                                  gr["dabr"], gr["dabi"], gr["dbsr"], gr["dbsi"])
    unt = lambda a: a.reshape((NL, G) + a.shape[3:])
    local = dict(
        ada_b=gr["dmod"].reshape(NL, N_MOD * D),
        norm1_g=gr["dn1"].reshape(NL, D), norm2_g=gr["dn2"].reshape(NL, D),
        q_norm_g=gr["dgqk"].reshape(NL, 2, NH, HEAD_DIM)[:, 0].sum(axis=1),
        k_norm_g=gr["dgqk"].reshape(NL, 2, NH, HEAD_DIM)[:, 1].sum(axis=1),
        ssm_a_re=dar.reshape(NL, G, SSM_STATE), ssm_a_im=dai.reshape(NL, G, SSM_STATE),
        ssm_log_dt=dldt.reshape(NL, G, SSM_STATE).sum(axis=-1),
        ssm_b_re=jnp.swapaxes(unt(_block_diag_extract(dbr_bd, SSM_GROUP, SSM_STATE)), -1, -2),
        ssm_b_im=jnp.swapaxes(unt(_block_diag_extract(dbi_bd, SSM_GROUP, SSM_STATE)), -1, -2),
        ssm_c_re=jnp.swapaxes(unt(_block_diag_extract(gr["dcr"], SSM_STATE, SSM_GROUP)), -1, -2),
        ssm_c_im=jnp.swapaxes(unt(_block_diag_extract(gr["dci"], SSM_STATE, SSM_GROUP)), -1, -2),
        ssm_d=gr["dd"].reshape(NL, G, SSM_GROUP),
        glu_w=unt(_block_diag_extract(gr["dwg"], SSM_GROUP, SSM_GROUP)),
        glu_b=gr["dgb"].reshape(NL, G, SSM_GROUP),
        attn_out_g=gr["dga"].reshape(NL, AW), ssm_out_g=gr["dgs"].reshape(NL, SW),
        ffn_conv_b=gr["dcb"].reshape(NL, 2 * NSH * FS),
    )

    def pack(tree):
        flat = jnp.concatenate([tree[n].reshape(-1) for n in small])
        pad = (-flat.shape[0]) % (512 * LANES)
        return jnp.pad(flat, (0, pad)).reshape(1, -1, LANES)

    (small_parts,) = _gather_shards([pack(local)], "gather_small_grads")
    sg, sd, sm, sv = _adamw(small_parts, pack(weights), pack(mom_m), pack(mom_v))

    def unpack(buf):
        flat = buf.reshape(-1)
        out, off = {}, 0
        for n in small:
            size = weights[n].size
            out[n] = flat[off:off + size].reshape(weights[n].shape)
            off += size
        return out

    ug, ud, um, uv = unpack(sg), unpack(sd), unpack(sm), unpack(sv)
    results = {n: (ug[n], ud[n], um[n], uv[n]) for n in small}

    dmod_all = _gather_small(gr["dmod"].reshape(NL, N_MOD * D), "gather_dmod")
    dmod_mine = lax.dynamic_slice_in_dim(dmod_all, me * NCA, NCA, axis=2)
    d_ada_w = _ada_bwd(jnp.transpose(c_all), jnp.transpose(dmod_mine, (1, 0, 2)))
    results["ada_w"] = tuple(_adamw(d_ada_w[:, None], ada_w, m_ada_w, v_ada_w))

    my_core = lax.axis_index("c")
    by_target = [g.reshape((NL, NDEV // 2, 2) + g.shape[2:])
                 for g in (gr["dw_in"], gr["dw_out"], gr["dw_up"], gr["dcw"], gr["dw_down"])]
    mine = [lax.dynamic_index_in_dim(g, my_core, axis=2, keepdims=False) for g in by_target]
    theirs = _sibling_exchange(
        [lax.dynamic_index_in_dim(g, 1 - my_core, axis=2, keepdims=False) for g in by_target], "pair_weight_grads")
    pair_sums = [_pair_add(a, b) for a, b in zip(mine, theirs)]
    parts = _chip_exchange(pair_sums, "scatter_weight_grads")
    for n, pt in zip(("w_in", "w_out", "ffn_w_up", "ffn_conv_w", "ffn_w_down"), parts):
        results[n] = tuple(_adamw(pt, weights[n], mom_m[n], mom_v[n]))

    out = [loss, grad_x[None]]
    for k in range(4):
        out.extend(results[n][k] for n in names)
    return tuple(out)
```

```python
import jax
import jax.numpy as jnp
from jax import lax
from jax.experimental import pallas as pl
from jax.experimental.pallas import tpu as pltpu

F32 = jnp.float32
BF16 = jnp.bfloat16
NDEV = 8
LANES = 128
HEAD_DIM = 64
SSM_GROUP = 16
SSM_STATE = 64
GROUPS_PER_TILE = LANES // SSM_GROUP
STATE_TILE = GROUPS_PER_TILE * SSM_STATE
N_MOD = 6
ATTN_STRIP = 32
EPS = 1e-6
ADAM_LR, ADAM_B1, ADAM_B2, ADAM_EPS, ADAM_WD, ADAM_STEP = 0.001, 0.9, 0.999, 1e-08, 0.01, 10
VMEM_LIMIT = 48 * 1024 * 1024
MESH_IDS = pl.DeviceIdType.MESH

NN = (((1,), (0,)), ((), ()))
NT = (((1,), (1,)), ((), ()))
TN = (((0,), (0,)), ((), ()))


def _dot(a, b, dims=NN):
    return lax.dot_general(a, b, dims, preferred_element_type=F32)


def _pcall(body, *, name, out_shape, in_specs, out_specs, grid=(), scratch=()):
    return pl.pallas_call(
        body, name=name, grid=grid, in_specs=in_specs, out_specs=out_specs, out_shape=out_shape,
        scratch_shapes=list(scratch),
        compiler_params=pltpu.CompilerParams(vmem_limit_bytes=VMEM_LIMIT))


def _row_tile(n, want=512):
    t = min(n, want)
    assert n % t == 0
    return t


def _my_index():
    return 4 * lax.axis_index("x") + 2 * lax.axis_index("y") + lax.axis_index("c")


HBM_SPEC = pl.BlockSpec(memory_space=pltpu.HBM)


def _mesh_place():
    x, y, c = lax.axis_index("x"), lax.axis_index("y"), lax.axis_index("c")
    chips = [(1 - x, y), (x, 1 - y), (1 - x, 1 - y)]
    return x, y, c, chips


def _gather_small(arr, name):
    def body(in_ref, out_ref, send_sems, recv_sems, local_sem):
        x, y, c, _ = _mesh_place()
        me = 4 * x + 2 * y + c
        own = pltpu.make_async_copy(in_ref, out_ref.at[me], local_sem)
        own.start()
        sends, recvs = [], []
        for k in range(1, NDEV):
            px = 1 - x if k & 4 else x
            py = 1 - y if k & 2 else y
            pc = 1 - c if k & 1 else c
            common = dict(send_sem=send_sems.at[k - 1], recv_sem=recv_sems.at[k - 1],
                          device_id=(px, py, pc), device_id_type=MESH_IDS)
            snd = pltpu.make_async_remote_copy(src_ref=in_ref, dst_ref=out_ref.at[me], **common)
            snd.start()
            sends.append(snd)
            recvs.append(pltpu.make_async_remote_copy(
                src_ref=in_ref, dst_ref=out_ref.at[4 * px + 2 * py + pc], **common))
        for r in recvs:
            r.wait_recv()
        for s in sends:
            s.wait_send()
        own.wait()

    return pl.pallas_call(
        body, name=name, out_shape=jax.ShapeDtypeStruct((NDEV,) + arr.shape, arr.dtype),
        in_specs=[HBM_SPEC], out_specs=HBM_SPEC,
        scratch_shapes=[pltpu.SemaphoreType.DMA((NDEV - 1,)), pltpu.SemaphoreType.DMA((NDEV - 1,)),
                        pltpu.SemaphoreType.DMA(())],
    )(arr)


def _start_by_layer(make, nl):
    for l in range(nl):
        make(l).start()


def _gather_shards(arrs, name):
    n = len(arrs)
    nl = arrs[0].shape[0]

    def body(*refs):
        ins, outs = refs[:n], refs[n:2 * n]
        send_sems, recv_sems = refs[2 * n:]
        x, y, c, chips = _mesh_place()
        dev = lambda px, py, pc: 4 * px + 2 * py + pc

        def copy(k, a, src, block, to, layer=slice(None)):
            return pltpu.make_async_remote_copy(
                src_ref=src.at[layer], dst_ref=outs[a].at[layer, block], send_sem=send_sems.at[k, a],
                recv_sem=recv_sems.at[k, a], device_id=to, device_id_type=MESH_IDS)

        me = dev(x, y, c)
        sent = []
        for a in range(n):
            _start_by_layer(lambda l: copy(0, a, ins[a], me, (x, y, 1 - c), l), nl)
            sent.append(copy(0, a, ins[a], me, (x, y, 1 - c)))
        for j, (px, py) in enumerate(chips):
            for a in range(n):
                _start_by_layer(lambda l: copy(1 + j, a, ins[a], me, (px, py, c), l), nl)
                sent.append(copy(1 + j, a, ins[a], me, (px, py, c)))
        for j, (px, py) in enumerate(chips):
            for a in range(n):
                blk = dev(px, py, c)
                copy(1 + j, a, ins[a], blk, (x, y, c)).wait_recv()
                got = outs[a].at[:, blk]
                _start_by_layer(lambda l: copy(4 + j, a, got, blk, (x, y, 1 - c), l), nl)
                sent.append(copy(4 + j, a, got, blk, (x, y, 1 - c)))
        for a in range(n):
            copy(0, a, ins[a], dev(x, y, 1 - c), (x, y, c)).wait_recv()
        for j, (px, py) in enumerate(chips):
            for a in range(n):
                copy(4 + j, a, ins[a], dev(px, py, 1 - c), (x, y, c)).wait_recv()
        for s in sent:
            s.wait_send()

    out_shape = [jax.ShapeDtypeStruct((a.shape[0], NDEV) + a.shape[1:], a.dtype) for a in arrs]
    outs = pl.pallas_call(
        body, name=name, out_shape=out_shape, in_specs=[HBM_SPEC] * n, out_specs=[HBM_SPEC] * n,
        scratch_shapes=[pltpu.SemaphoreType.DMA((7, n)), pltpu.SemaphoreType.DMA((7, n))],
    )(*arrs)
    me = _my_index()
    return [lax.dynamic_update_slice_in_dim(o, a[:, None], me, axis=1) for o, a in zip(outs, arrs)]


def _sibling_exchange(arrs, name):
    n = len(arrs)
    nl, nchips = arrs[0].shape[:2]

    def body(*refs):
        ins, got = refs[:n], refs[n:2 * n]
        send_sems, recv_sems = refs[2 * n:]
        x, y, c, _ = _mesh_place()

        def send(a, idx):
            return pltpu.make_async_remote_copy(
                src_ref=ins[a].at[idx], dst_ref=got[a].at[idx], send_sem=send_sems.at[a],
                recv_sem=recv_sems.at[a], device_id=(x, y, 1 - c), device_id_type=MESH_IDS)

        everything = (slice(None), slice(None))
        for a in range(n):
            for q in range(nchips):
                _start_by_layer(lambda l: send(a, (l, q)), nl)
        for a in range(n):
            send(a, everything).wait_recv()
        for a in range(n):
            send(a, everything).wait_send()

    shapes = [jax.ShapeDtypeStruct(a.shape, a.dtype) for a in arrs]
    return list(pl.pallas_call(
        body, name=name, out_shape=shapes, in_specs=[HBM_SPEC] * n, out_specs=[HBM_SPEC] * n,
        scratch_shapes=[pltpu.SemaphoreType.DMA((n,)), pltpu.SemaphoreType.DMA((n,))],
    )(*arrs))


def _chip_exchange(arrs, name):
    n = len(arrs)
    nl = arrs[0].shape[0]

    def body(*refs):
        ins, outs = refs[:n], refs[n:2 * n]
        send_sems, recv_sems = refs[2 * n:]
        x, y, c, chips = _mesh_place()
        mine = 2 * x + y

        def copy(j, a, target, source, layer=slice(None)):
            px, py = chips[j]
            return pltpu.make_async_remote_copy(
                src_ref=ins[a].at[layer, target], dst_ref=outs[a].at[layer, source], send_sem=send_sems.at[j, a],
                recv_sem=recv_sems.at[j, a], device_id=(px, py, c), device_id_type=MESH_IDS)

        for j, (px, py) in enumerate(chips):
            for a in range(n):
                _start_by_layer(lambda l: copy(j, a, 2 * px + py, mine, l), nl)
        for j, (px, py) in enumerate(chips):
            for a in range(n):
                copy(j, a, mine, 2 * px + py).wait_recv()
        for j, (px, py) in enumerate(chips):
            for a in range(n):
                copy(j, a, 2 * px + py, mine).wait_send()

    out_shape = [jax.ShapeDtypeStruct(a.shape, a.dtype) for a in arrs]
    outs = pl.pallas_call(
        body, name=name, out_shape=out_shape, in_specs=[HBM_SPEC] * n, out_specs=[HBM_SPEC] * n,
        scratch_shapes=[pltpu.SemaphoreType.DMA((3, n)), pltpu.SemaphoreType.DMA((3, n))],
    )(*arrs)
    mine = 2 * lax.axis_index("x") + lax.axis_index("y")
    return [lax.dynamic_update_slice_in_dim(o, lax.dynamic_slice_in_dim(a, mine, 1, axis=1), mine, axis=1)
            for o, a in zip(outs, arrs)]


def _pair_add(a, b):
    NL, NC, R, C = a.shape
    tr = R
    for cand in (256, 128, 64, 32, 16):
        if R % cand == 0 and R > cand:
            tr = cand
            break

    def body(a_ref, b_ref, o_ref):
        o_ref[...] = (a_ref[...].astype(F32) + b_ref[...].astype(F32)).astype(o_ref.dtype)

    spec = pl.BlockSpec((None, NC, tr, C), lambda l, i: (l, 0, i, 0))
    return _pcall(body, name="pair_add", grid=(NL, R // tr), in_specs=[spec, spec], out_specs=spec,
                  out_shape=jax.ShapeDtypeStruct(a.shape, a.dtype))(a, b)


def _matmul(name, a, b, *, dims, grid, a_spec, b_spec, out_shape, out_specs, kaxis=None, acc_shape=None,
            extra=(), extra_specs=(), epilogue=None):
    nk = grid[kaxis] if kaxis is not None else 1
    ne = len(extra)
    multi = isinstance(out_shape, (list, tuple))
    n_out = len(out_shape) if multi else 1

    def body(*refs):
        a_ref, b_ref = refs[0], refs[1]
        ex = refs[2:2 + ne]
        outs = refs[2 + ne:2 + ne + n_out]

        def write(res):
            vals = epilogue(res, *[e[...] for e in ex]) if epilogue is not None else (res,)
            for o, v in zip(outs, vals):
                o[...] = v.astype(o.dtype)

        part = _dot(a_ref[...].astype(BF16), b_ref[...].astype(BF16), dims)
        if nk == 1:
            write(part)
        else:
            acc = refs[-1]
            k = pl.program_id(kaxis)

            @pl.when(k == 0)
            def _():
                acc[...] = part

            @pl.when(k > 0)
            def _():
                acc[...] += part

            @pl.when(k == nk - 1)
            def _():
                write(acc[...])

    scratch = [pltpu.VMEM(acc_shape, F32)] if nk > 1 else []
    return _pcall(body, name=name, grid=grid, in_specs=[a_spec, b_spec, *extra_specs],
                  out_specs=out_specs, out_shape=out_shape, scratch=scratch)(a, b, *extra)


def _rms(x, g):
    inv = lax.rsqrt(jnp.mean(x * x, axis=-1, keepdims=True) + EPS)
    return x * inv * g


def _lnmod_math(x, g, sh, sc):
    return _rms(x, g) * (1.0 + sc) + sh


def _qkn_math(p, g):
    m0 = lax.broadcasted_iota(jnp.int32, (1, LANES), 1) < HEAD_DIM
    sq = p * p
    s0 = jnp.sum(jnp.where(m0, sq, 0.0), axis=-1, keepdims=True)
    s1 = jnp.sum(jnp.where(m0, 0.0, sq), axis=-1, keepdims=True)
    inv = jnp.where(m0, lax.rsqrt(s0 / HEAD_DIM + EPS), lax.rsqrt(s1 / HEAD_DIM + EPS))
    return p * inv * g


def _glu_math(val, gate):
    return jax.nn.gelu(gate) * val


def _accumulate(ref, val, first):
    @pl.when(first)
    def _():
        ref[...] = val

    @pl.when(jnp.logical_not(first))
    def _():
        ref[...] += val


def _lnmod_fwd(x, g, sh, sc):
    L, D = x.shape
    tm = _row_tile(L)

    def body(x_ref, g_ref, sh_ref, sc_ref, h_ref):
        h_ref[...] = _lnmod_math(x_ref[...], g_ref[...], sh_ref[...], sc_ref[...]).astype(BF16)

    row = pl.BlockSpec((tm, D), lambda i: (i, 0))
    vec = pl.BlockSpec((1, D), lambda i: (0, 0))
    return _pcall(body, name="lnmod_fwd", grid=(L // tm,), in_specs=[row, vec, vec, vec], out_specs=row,
                  out_shape=jax.ShapeDtypeStruct((L, D), BF16))(x, g, sh, sc)


def _lnmod_bwd(dh, x, g, sh, sc, dres):
    L, D = x.shape
    tm = _row_tile(L)

    def body(dh_ref, x_ref, g_ref, sh_ref, sc_ref, res_ref, dx_ref, dg_ref, dsh_ref, dsc_ref):
        _, vjp = jax.vjp(_lnmod_math, x_ref[...], g_ref[...], sh_ref[...], sc_ref[...])
        dx, dg, dsh, dsc = vjp(dh_ref[...])
        dx_ref[...] = dx + res_ref[...]
        first = pl.program_id(0) == 0
        _accumulate(dg_ref, dg, first)
        _accumulate(dsh_ref, dsh, first)
        _accumulate(dsc_ref, dsc, first)

    row = pl.BlockSpec((tm, D), lambda i: (i, 0))
    vec = pl.BlockSpec((1, D), lambda i: (0, 0))
    vs = jax.ShapeDtypeStruct((1, D), F32)
    return _pcall(body, name="lnmod_bwd", grid=(L // tm,), in_specs=[row, row, vec, vec, vec, row],
                  out_specs=[row, vec, vec, vec],
                  out_shape=[jax.ShapeDtypeStruct((L, D), F32), vs, vs, vs])(dh, x, g, sh, sc, dres)


def _gate_bwd(dx, a, gate):
    L, D = dx.shape
    tm = _row_tile(L)

    def body(dx_ref, a_ref, g_ref, da_ref, dg_ref):
        dxv = dx_ref[...]
        da_ref[...] = (g_ref[...] * dxv).astype(BF16)
        _accumulate(dg_ref, jnp.sum(dxv * a_ref[...], axis=0, keepdims=True), pl.program_id(0) == 0)

    row = pl.BlockSpec((tm, D), lambda i: (i, 0))
    vec = pl.BlockSpec((1, D), lambda i: (0, 0))
    return _pcall(body, name="gate_bwd", grid=(L // tm,), in_specs=[row, row, vec], out_specs=[row, vec],
                  out_shape=[jax.ShapeDtypeStruct((L, D), BF16), jax.ShapeDtypeStruct((1, D), F32)])(dx, a, gate)


def _qknorm_fwd(p, gqk, AW):
    L = p.shape[0]
    tm = _row_tile(L)
    ncol = 2 * AW // LANES

    def body(p_ref, g_ref, o_ref):
        o_ref[...] = _qkn_math(p_ref[...], g_ref[...]).astype(BF16)

    blk = pl.BlockSpec((tm, LANES), lambda i, j: (i, j))
    vec = pl.BlockSpec((1, LANES), lambda i, j: (0, j))
    return _pcall(body, name="qknorm_fwd", grid=(L // tm, ncol), in_specs=[blk, vec], out_specs=blk,
                  out_shape=jax.ShapeDtypeStruct((L, 2 * AW), BF16))(p, gqk)


def _qknorm_bwd(dqk, p, gqk, AW):
    L = p.shape[0]
    tm = _row_tile(L)
    ncol = 2 * AW // LANES

    def body(d_ref, p_ref, g_ref, dp_ref, dg_ref):
        _, vjp = jax.vjp(_qkn_math, p_ref[...], g_ref[...])
        dp, dg = vjp(d_ref[...])
        dp_ref[...] = dp
        _accumulate(dg_ref, dg, pl.program_id(1) == 0)

    blk = pl.BlockSpec((tm, LANES), lambda j, i: (i, j))
    vec = pl.BlockSpec((1, LANES), lambda j, i: (0, j))
    return _pcall(body, name="qknorm_bwd", grid=(ncol, L // tm), in_specs=[blk, blk, vec], out_specs=[blk, vec],
                  out_shape=[jax.ShapeDtypeStruct((L, 2 * AW), F32),
                             jax.ShapeDtypeStruct((1, 2 * AW), F32)])(dqk, p, gqk)


def _outnorm_fwd(oa, os_, ga, gs):
    L, AW = oa.shape
    SW = os_.shape[1]
    tm = _row_tile(L)

    def body(oa_ref, os_ref, ga_ref, gs_ref, o_ref):
        o_ref[:, :AW] = _rms(oa_ref[...], ga_ref[...]).astype(BF16)
        o_ref[:, AW:] = _rms(os_ref[...], gs_ref[...]).astype(BF16)

    ra = pl.BlockSpec((tm, AW), lambda i: (i, 0))
    rs = pl.BlockSpec((tm, SW), lambda i: (i, 0))
    va = pl.BlockSpec((1, AW), lambda i: (0, 0))
    vs = pl.BlockSpec((1, SW), lambda i: (0, 0))
    ro = pl.BlockSpec((tm, AW + SW), lambda i: (i, 0))
    return _pcall(body, name="outnorm_fwd", grid=(L // tm,), in_specs=[ra, rs, va, vs], out_specs=ro,
                  out_shape=jax.ShapeDtypeStruct((L, AW + SW), BF16))(oa, os_, ga, gs)


def _outnorm_bwd(do, oa, os_, ga, gs):
    L, AW = oa.shape
    SW = os_.shape[1]
    tm = _row_tile(L)

    def body(do_ref, oa_ref, os_ref, ga_ref, gs_ref, doa_ref, dos_ref, dga_ref, dgs_ref):
        first = pl.program_id(0) == 0
        _, vjp_a = jax.vjp(_rms, oa_ref[...], ga_ref[...])
        doa, dga = vjp_a(do_ref[:, :AW])
        _, vjp_s = jax.vjp(_rms, os_ref[...], gs_ref[...])
        dos, dgs = vjp_s(do_ref[:, AW:])
        doa_ref[...] = doa
        dos_ref[...] = dos
        _accumulate(dga_ref, dga, first)
        _accumulate(dgs_ref, dgs, first)

    ra = pl.BlockSpec((tm, AW), lambda i: (i, 0))
    rs = pl.BlockSpec((tm, SW), lambda i: (i, 0))
    va = pl.BlockSpec((1, AW), lambda i: (0, 0))
    vs = pl.BlockSpec((1, SW), lambda i: (0, 0))
    ro = pl.BlockSpec((tm, AW + SW), lambda i: (i, 0))
    return _pcall(body, name="outnorm_bwd", grid=(L // tm,), in_specs=[ro, ra, rs, va, vs],
                  out_specs=[ra, rs, va, vs],
                  out_shape=[jax.ShapeDtypeStruct((L, AW), F32), jax.ShapeDtypeStruct((L, SW), F32),
                             jax.ShapeDtypeStruct((1, AW), F32), jax.ShapeDtypeStruct((1, SW), F32)])(
                                 do, oa, os_, ga, gs)


def _softplus_neg_abs(z):
    return jnp.log(1.0 + jnp.exp(-jnp.abs(z)))


def _split_dot(x, m):
    hi = x.astype(BF16)
    lo = (x - hi.astype(F32)).astype(BF16)
    return _dot(hi, m) + _dot(lo, m)


def _attn_masks(B):
    row = lax.broadcasted_iota(jnp.int32, (B, B), 0)
    col = lax.broadcasted_iota(jnp.int32, (B, B), 1)
    strict = col < row
    upper = jnp.where(row > col, 1.0, 0.0).astype(BF16)
    lower = jnp.where(row < col, 1.0, 0.0).astype(BF16)
    return strict, upper, lower


def _attn_fwd(qk, p, AW, B):
    L = qk.shape[0]
    HP = AW // LANES
    nb = L // B

    def body(q_ref, k_ref, v_ref, o_ref, tot_ref, *scr):
        lb_scr = (scr[0:2], scr[2:4])
        tail_scr = (scr[4:6], scr[6:8])
        sum_scr = scr[8:10]
        i = pl.program_id(1)
        m0 = lax.broadcasted_iota(jnp.int32, (1, LANES), 1) < HEAD_DIM
        strict, upper, _ = _attn_masks(B)
        q = q_ref[...] * 0.125
        zq = jnp.zeros_like(q)
        qh = (jnp.where(m0, q, zq), jnp.where(m0, zq, q))

        def keys(j):
            start = pl.multiple_of(jnp.maximum(j, 0) * B, B)
            return k_ref[pl.ds(start, B), :]

        def vals(j):
            start = pl.multiple_of(jnp.maximum(j, 0) * B, B)
            return v_ref[pl.ds(start, B), :].astype(BF16)

        strips = [slice(s, min(s + ATTN_STRIP, B)) for s in range(0, B, ATTN_STRIP)]

        def scores(j):
            kj = keys(j)
            return tuple(_dot(qh[h], kj, NT) for h in (0, 1))

        def logits(zs2, slot, diag):
            for h in (0, 1):
                z = zs2[h]
                his, los = [], []
                for rows in strips:
                    zs = z[rows]
                    lb = jnp.minimum(zs, 0.0) - _softplus_neg_abs(zs)
                    l1 = lb - zs
                    if diag:
                        l1 = jnp.where(strict[rows], l1, 0.0)
                    lb_scr[slot][h][rows, :] = lb
                    hi = l1.astype(BF16)
                    his.append(hi)
                    los.append((l1 - hi.astype(F32)).astype(BF16))
                    rsum = jnp.sum(l1, axis=-1, keepdims=True)
                    if h == 0:
                        sum_scr[slot][rows, :] = jnp.broadcast_to(rsum, (rows.stop - rows.start, LANES))
                    else:
                        sum_scr[slot][rows, :] = jnp.where(m0, sum_scr[slot][rows, :], rsum)
                cat = lambda xs: jnp.concatenate(xs, axis=0)
                tail_scr[slot][h][...] = _dot(cat(his), upper) + _dot(cat(los), upper)

        def attend(j, slot, diag):
            vj = vals(j)
            pv = []
            for h in (0, 1):
                ws = []
                for rows in strips:
                    w = jnp.exp(lb_scr[slot][h][rows, :] + tail_scr[slot][h][rows, :])
                    if diag:
                        w = jnp.where(strict[rows], w, 0.0)
                    ws.append(w.astype(BF16))
                pv.append(_dot(jnp.concatenate(ws, axis=0), vj))
            return jnp.where(m0, pv[0], pv[1])

        logits(scores(i), 0, True)
        o_ref[...] = attend(i, 0, True)
        tot_ref[...] = sum_scr[0][...]

        def half(j, slot):
            z = scores(j - 1)
            pv = attend(j, slot, False)
            logits(z, 1 - slot, False)
            o_ref[...] += pv * jnp.exp(tot_ref[...])
            tot_ref[...] += sum_scr[slot][...]

        @pl.when(i > 0)
        def _():
            logits(scores(i - 1), 1, False)

            @pl.loop(0, (i + 1) // 2)
            def _(t):
                j = i - 1 - 2 * t
                half(j, 1)

                @pl.when(j > 0)
                def _():
                    half(j - 1, 0)


    qspec = pl.BlockSpec((B, LANES), lambda hp, i: (i, hp))
    kspec = pl.BlockSpec((L, LANES), lambda hp, i: (0, HP + hp))
    vspec = pl.BlockSpec((L, LANES), lambda hp, i: (0, 2 * HP + hp))
    ospec = pl.BlockSpec((B, LANES), lambda hp, i: (i, hp))
    shp = jax.ShapeDtypeStruct((L, AW), F32)
    return _pcall(body, name="attn_fwd", grid=(HP, nb), in_specs=[qspec, kspec, vspec],
                  out_specs=[ospec, ospec], out_shape=[shp, shp],
                  scratch=[pltpu.VMEM((B, B), F32)] * 8 + [pltpu.VMEM((B, LANES), F32)] * 2)(qk, qk, p)


def _attn_bwd(qk, p, do, tot, AW, B):
    L = qk.shape[0]
    HP = AW // LANES
    nb = L // B

    def body(q_ref, k_ref, v_ref, do_ref, tot_ref, dq_ref, dk_ref, dv_ref, *scr):
        lb_scr, tail_scr, dw_scr, e_scr, beta_scr = scr[0:2], scr[2:4], scr[4:6], scr[6:8], scr[8:10]
        dos_scr, bsum_scr, left_scr, esum_scr, ecum_scr = scr[10:15]
        i = pl.program_id(1)
        m0 = lax.broadcasted_iota(jnp.int32, (1, LANES), 1) < HEAD_DIM
        strict, upper, lower = _attn_masks(B)
        strips = [slice(s, min(s + ATTN_STRIP, B)) for s in range(0, B, ATTN_STRIP)]
        cat = lambda xs: jnp.concatenate(xs, axis=0)

        @pl.when(i == 0)
        def _():
            dk_ref[...] = jnp.zeros_like(dk_ref)
            dv_ref[...] = jnp.zeros_like(dv_ref)

        q = q_ref[...] * 0.125
        zq = jnp.zeros_like(q)
        qh = (jnp.where(m0, q, zq), jnp.where(m0, zq, q))
        heads = lambda a: (jnp.where(m0, a, jnp.zeros_like(a)), jnp.where(m0, jnp.zeros_like(a), a))
        left_scr[...] = jnp.zeros_like(left_scr)
        ecum_scr[...] = jnp.zeros_like(ecum_scr)
        dq_ref[...] = jnp.zeros_like(dq_ref)

        def block_rows(j):
            return pl.ds(pl.multiple_of(j * B, B), B)

        def put_row_sums(ref, rows, h, rsum):
            if h == 0:
                ref[rows, :] = jnp.broadcast_to(rsum, (rows.stop - rows.start, LANES))
            else:
                ref[rows, :] = jnp.where(m0, ref[rows, :], rsum)

        def head_cols(x, h):
            other = pltpu.roll(x, HEAD_DIM, 1)
            full = jnp.where(m0, x, other) if h == 0 else jnp.where(m0, other, x)
            return jnp.concatenate([full] * (B // LANES), axis=1) if B > LANES else full

        def scores(j):
            kj = k_ref[block_rows(j), :]
            return tuple(_dot(qh[h], kj, NT) for h in (0, 1))

        def stage_a(j, zs2, diag):
            for h in (0, 1):
                his, los = [], []
                for rows in strips:
                    zs = zs2[h][rows]
                    lb = jnp.minimum(zs, 0.0) - _softplus_neg_abs(zs)
                    l1 = lb - zs
                    if diag:
                        l1 = jnp.where(strict[rows], l1, 0.0)
                    lb_scr[h][rows, :] = lb
                    hi = l1.astype(BF16)
                    his.append(hi)
                    los.append((l1 - hi.astype(F32)).astype(BF16))
                    put_row_sums(bsum_scr, rows, h, jnp.sum(l1, axis=-1, keepdims=True))
                tail_scr[h][...] = _dot(cat(his), upper) + _dot(cat(los), upper)
            bs = bsum_scr[...]
            scale = jnp.exp(tot_ref[...] - left_scr[...] - bs)
            left_scr[...] += bs
            dos = (do_ref[...] * scale).astype(BF16)
            dos_scr[...] = dos
            vj = v_ref[block_rows(j), :].astype(BF16)
            dosh = heads(dos)
            for h in (0, 1):
                dw_scr[h][...] = _dot(dosh[h], vj, NT)

        def stage_b(j, diag):
            dosh = heads(dos_scr[...])
            pres = []
            dv_blk = jnp.zeros((B, LANES), F32)
            for h in (0, 1):
                ehs, els, wbs = [], [], []
                for rows in strips:
                    lb = lb_scr[h][rows, :]
                    w = jnp.exp(lb + tail_scr[h][rows, :])
                    if diag:
                        w = jnp.where(strict[rows], w, 0.0)
                    beta_scr[h][rows, :] = jnp.exp(lb)
                    e = dw_scr[h][rows, :] * w
                    e_scr[h][rows, :] = e
                    ehi = e.astype(BF16)
                    ehs.append(ehi)
                    els.append((e - ehi.astype(F32)).astype(BF16))
                    wbs.append(w.astype(BF16))
                    put_row_sums(esum_scr, rows, h, jnp.sum(e, axis=-1, keepdims=True))
                pres.append(_dot(cat(ehs), lower) + _dot(cat(els), lower))
                dv_blk = dv_blk + _dot(cat(wbs), dosh[h], TN)
            dv_ref[block_rows(j), :] += dv_blk
            return pres

        def stage_c(j, pres, diag):
            kh = heads(k_ref[block_rows(j), :])
            dq = jnp.zeros((B, LANES), F32)
            dk_blk = jnp.zeros((B, LANES), F32)
            for h in (0, 1):
                dzs = []
                for rows in strips:
                    e = e_scr[h][rows, :]
                    dl1 = pres[h][rows] + head_cols(ecum_scr[rows, :], h)
                    dz = e - beta_scr[h][rows, :] * (e + dl1)
                    if diag:
                        dz = jnp.where(strict[rows], dz, 0.0)
                    dzs.append(dz.astype(BF16))
                dzb = cat(dzs)
                dq = dq + _dot(dzb, kh[h])
                dk_blk = dk_blk + _dot(dzb, qh[h], TN)
            dq_ref[...] += dq
            dk_ref[block_rows(j), :] += dk_blk
            ecum_scr[...] += esum_scr[...]

        def step(t, diag_next):
            z = scores(t + 1)
            pres = stage_b(t, False)
            stage_a(t + 1, z, diag_next)
            stage_c(t, pres, False)

        @pl.when(i > 0)
        def _():
            stage_a(0, scores(0), False)

            @pl.loop(0, i - 1)
            def _(t):
                step(t, False)

            step(i - 1, True)

        @pl.when(i == 0)
        def _():
            stage_a(0, scores(0), True)

        stage_c(i, stage_b(i, True), True)
        dq_ref[...] = dq_ref[...] * 0.125

    qspec = pl.BlockSpec((B, LANES), lambda hp, i: (i, hp))
    kspec = pl.BlockSpec((L, LANES), lambda hp, i: (0, HP + hp))
    vspec = pl.BlockSpec((L, LANES), lambda hp, i: (0, 2 * HP + hp))
    full = pl.BlockSpec((L, LANES), lambda hp, i: (0, hp))
    shp = jax.ShapeDtypeStruct((L, AW), F32)
    return _pcall(body, name="attn_bwd", grid=(HP, nb), in_specs=[qspec, kspec, vspec, qspec, qspec],
                  out_specs=[qspec, full, full], out_shape=[shp, shp, shp],
                  scratch=[pltpu.VMEM((B, B), F32)] * 10 + [pltpu.VMEM((B, LANES), BF16)]
                  + [pltpu.VMEM((B, LANES), F32)] * 4)(qk, qk, p, do, tot)


def _s5_disc(ar, ai, ldt):
    dt = jnp.exp(ldt)
    mag = jnp.exp(dt * ar)
    abr = mag * jnp.cos(dt * ai)
    abi = mag * jnp.sin(dt * ai)
    emr = abr - 1.0
    emi = abi
    den = ar * ar + ai * ai
    fr = (emr * ar + emi * ai) / den
    fi = (emi * ar - emr * ai) / den
    return abr, abi, fr, fi


def _s5_params_math(ar, ai, ldt, br, bi):
    abr, abi, fr, fi = _s5_disc(ar, ai, ldt)
    return abr, abi, fr * br - fi * bi, fr * bi + fi * br


def _cmul_add(xr, xi, kr, ki, sr, si):
    return xr + (kr * sr - ki * si), xi + (kr * si + ki * sr)


def _s5_tab_rows(T):
    ng = T // 8
    return dict(NG=ng, POW=0, PA=ng, PAR=ng + 8, LA=ng + 16, ROWS=ng + 24)


def _s5_prep(ar, ai, ldt, braw_r, braw_i, T):
    NL, _, NS = ar.shape
    GT = NS // STATE_TILE
    R = _s5_tab_rows(T)

    def body(ar_ref, ai_ref, ldt_ref, br_ref, bi_ref, abr_ref, abi_ref, tabr_ref, tabi_ref, bsr_ref, bsi_ref):
        for t in range(GT):
            sl = slice(t * STATE_TILE, (t + 1) * STATE_TILE)
            abr, abi, bsr, bsi = _s5_params_math(ar_ref[:, sl], ai_ref[:, sl], ldt_ref[:, sl],
                                                 br_ref[t], bi_ref[t])
            abr_ref[:, sl] = abr
            abi_ref[:, sl] = abi
            bsr_ref[t] = bsr.astype(BF16)
            bsi_ref[t] = bsi.astype(BF16)

            def put(row, vr, vi):
                tabr_ref[row:row + 1, sl] = vr
                tabi_ref[row:row + 1, sl] = vi

            pr, pi = abr, abi
            for g in range(R["NG"]):
                put(R["POW"] + g, pr, pi)
                if g + 1 < R["NG"]:
                    pr, pi = pr * abr - pi * abi, pr * abi + pi * abr
            big_r, big_i = pr, pi
            qr, qi = big_r, big_i
            for r in range(8):
                put(R["PA"] + r, qr, qi)
                put(R["PAR"] + 7 - r, qr, qi)
                qr, qi = qr * big_r - qi * big_i, qr * big_i + qi * big_r
            qr, qi = big_r, big_i
            for k in range(3):
                put(R["LA"] + k, qr, qi)
                qr, qi = qr * qr - qi * qi, 2.0 * qr * qi
            for k in range(3, 8):
                put(R["LA"] + k, jnp.zeros_like(qr), jnp.zeros_like(qi))

    rowspec = pl.BlockSpec((None, 1, NS), lambda l: (l, 0, 0))
    bspec = pl.BlockSpec((None, GT, LANES, STATE_TILE), lambda l: (l, 0, 0, 0))
    tabspec = pl.BlockSpec((None, R["ROWS"], NS), lambda l: (l, 0, 0))
    rs = jax.ShapeDtypeStruct((NL, 1, NS), F32)
    ts = jax.ShapeDtypeStruct((NL, R["ROWS"], NS), F32)
    bs = jax.ShapeDtypeStruct((NL, GT, LANES, STATE_TILE), BF16)
    return _pcall(body, name="s5_prep", grid=(NL,), in_specs=[rowspec] * 3 + [bspec] * 2,
                  out_specs=[rowspec, rowspec, tabspec, tabspec, bspec, bspec],
                  out_shape=[rs, rs, ts, ts, bs, bs])(ar, ai, ldt, braw_r, braw_i)


def _s5_prep_bwd(ar, ai, ldt, braw_r, braw_i, dabr, dabi, dbsr, dbsi):
    NL, _, NS = ar.shape
    GT = NS // STATE_TILE

    def body(ar_ref, ai_ref, ldt_ref, br_ref, bi_ref, dabr_ref, dabi_ref, dbsr_ref, dbsi_ref,
             dar_ref, dai_ref, dldt_ref, dbr_ref, dbi_ref):
        for t in range(GT):
            sl = slice(t * STATE_TILE, (t + 1) * STATE_TILE)
            _, vjp = jax.vjp(_s5_params_math, ar_ref[:, sl], ai_ref[:, sl], ldt_ref[:, sl],
                             br_ref[t], bi_ref[t])
            dabr_row = jnp.sum(dabr_ref[:, sl], axis=0, keepdims=True)
            dabi_row = jnp.sum(dabi_ref[:, sl], axis=0, keepdims=True)
            dar, dai, dldt, dbr, dbi = vjp((dabr_row, dabi_row, dbsr_ref[t], dbsi_ref[t]))
            dar_ref[:, sl] = dar
            dai_ref[:, sl] = dai
            dldt_ref[:, sl] = dldt
            dbr_ref[t] = dbr
            dbi_ref[t] = dbi

    rowspec = pl.BlockSpec((None, 1, NS), lambda l: (l, 0, 0))
    row8spec = pl.BlockSpec((None, 8, NS), lambda l: (l, 0, 0))
    bspec = pl.BlockSpec((None, GT, LANES, STATE_TILE), lambda l: (l, 0, 0, 0))
    rs = jax.ShapeDtypeStruct((NL, 1, NS), F32)
    bs = jax.ShapeDtypeStruct((NL, GT, LANES, STATE_TILE), F32)
    return _pcall(body, name="s5_prep_bwd", grid=(NL,), in_specs=[rowspec] * 3 + [bspec] * 2 + [row8spec] * 2 + [bspec] * 2,
                  out_specs=[rowspec] * 3 + [bspec] * 2, out_shape=[rs, rs, rs, bs, bs])(
                      ar, ai, ldt, braw_r, braw_i, dabr, dabi, dbsr, dbsi)


def _to_streams(x, T):
    L, C = x.shape
    return x.reshape(L // T, 8, T // 8, C).transpose(0, 2, 1, 3).reshape(L, C)


def _from_streams(x, T):
    L, C = x.shape
    return x.reshape(L // T, T // 8, 8, C).transpose(0, 2, 1, 3).reshape(L, C)


def _s5_specs(L, SW, T, rev):
    GT = SW // LANES
    nc = L // T
    cidx = (lambda c: nc - 1 - c) if rev else (lambda c: c)
    rows = _s5_tab_rows(T)["ROWS"]
    return dict(
        GT=GT, nc=nc, cidx=cidx,
        chan=pl.BlockSpec((T, LANES), lambda j, c: (cidx(c), j)),
        state=pl.BlockSpec((T, STATE_TILE), lambda j, c: (cidx(c), j)),
        bmat=pl.BlockSpec((None, LANES, STATE_TILE), lambda j, c: (j, 0, 0)),
        cmat=pl.BlockSpec((None, STATE_TILE, LANES), lambda j, c: (j, 0, 0)),
        gmat=pl.BlockSpec((None, LANES, LANES), lambda j, c: (j, 0, 0)),
        cvec=pl.BlockSpec((1, LANES), lambda j, c: (0, j)),
        svec8=pl.BlockSpec((8, STATE_TILE), lambda j, c: (0, j)),
        tab=pl.BlockSpec((rows, STATE_TILE), lambda j, c: (0, j)),
    )


def _s5_fwd(u, sp, T):
    L, SW = u.shape
    S = _s5_specs(L, SW, T, False)
    NS = S["GT"] * STATE_TILE
    R = _s5_tab_rows(T)
    NG = R["NG"]

    def body(u_ref, bsr_ref, bsi_ref, tabr_ref, tabi_ref, cr_ref, ci_ref, d_ref, wg_ref, gb_ref,
             o_ref, sr_ref, si_ref, y_ref, carr_ref, cari_ref):
        @pl.when(pl.program_id(1) == 0)
        def _():
            carr_ref[...] = jnp.zeros_like(carr_ref)
            cari_ref[...] = jnp.zeros_like(cari_ref)

        uv = u_ref[...]
        ub = uv.astype(BF16)
        row8 = lax.broadcasted_iota(jnp.int32, (8, LANES), 0)
        nstrips = STATE_TILE // LANES

        def project(s):
            sl = slice(s * LANES, (s + 1) * LANES)
            return _dot(ub, bsr_ref[:, sl]), _dot(ub, bsi_ref[:, sl])

        ahead = project(0)
        for s in range(nstrips):
            sl = slice(s * LANES, (s + 1) * LANES)
            tab = lambda r0, n=1, sl=sl: (tabr_ref[r0:r0 + n, sl], tabi_ref[r0:r0 + n, sl])
            xr, xi = ahead
            if s + 1 < nstrips:
                ahead = project(s + 1)
            ar, ai = tab(R["POW"])
            lr, li = [xr[0:8]], [xi[0:8]]
            for g in range(1, NG):
                nr, ni = _cmul_add(xr[8 * g:8 * g + 8], xi[8 * g:8 * g + 8], ar, ai, lr[-1], li[-1])
                lr.append(nr)
                li.append(ni)
            yr, yi = lr[-1], li[-1]
            for k, dist in enumerate((1, 2, 4)):
                kr, ki = tab(R["LA"] + k)
                keep = row8 >= dist
                yr, yi = _cmul_add(yr, yi, kr, ki, jnp.where(keep, pltpu.roll(yr, dist, 0), 0.0),
                                   jnp.where(keep, pltpu.roll(yi, dist, 0), 0.0))
            c0r, c0i = carr_ref[7:8, sl], cari_ref[7:8, sl]
            par, pai = tab(R["PA"], 8)
            er, ei = _cmul_add(yr, yi, par, pai, c0r, c0i)
            carr_ref[:, sl] = er
            cari_ref[:, sl] = ei
            first = row8 == 0
            inr = jnp.where(first, c0r, pltpu.roll(er, 1, 0))
            ini = jnp.where(first, c0i, pltpu.roll(ei, 1, 0))
            for g in range(NG):
                pr, pi = tab(R["POW"] + g)
                outr, outi = _cmul_add(lr[g], li[g], pr, pi, inr, ini)
                sr_ref[8 * g:8 * g + 8, sl] = outr
                si_ref[8 * g:8 * g + 8, sl] = outi
        y = (_dot(sr_ref[...].astype(BF16), cr_ref[...]) - _dot(si_ref[...].astype(BF16), ci_ref[...])
             + d_ref[...] * uv)
        y_ref[...] = y
        yg = jax.nn.gelu(y)
        gate = jax.nn.sigmoid(_dot(yg.astype(BF16), wg_ref[...]) + gb_ref[...])
        o_ref[...] = yg * gate

    cs = jax.ShapeDtypeStruct((L, SW), F32)
    ss = jax.ShapeDtypeStruct((L, NS), F32)
    return _pcall(
        body, name="s5_fwd", grid=(S["GT"], S["nc"]),
        in_specs=[S["chan"], S["bmat"], S["bmat"], S["tab"], S["tab"], S["cmat"], S["cmat"],
                  S["cvec"], S["gmat"], S["cvec"]],
        out_specs=[S["chan"], S["state"], S["state"], S["chan"]], out_shape=[cs, ss, ss, cs],
        scratch=[pltpu.VMEM((8, STATE_TILE), F32), pltpu.VMEM((8, STATE_TILE), F32)],
    )(u, sp["bsr"], sp["bsi"], sp["tabr"], sp["tabi"], sp["crT"], sp["ciT"], sp["d"], sp["wg"], sp["gb"])


def _s5_bwd(do, u, s_re, s_im, ypre, sp, T):
    L, SW = u.shape
    S = _s5_specs(L, SW, T, True)
    GT, nc, cidx = S["GT"], S["nc"], S["cidx"]
    NS = GT * STATE_TILE
    R = _s5_tab_rows(T)
    NG = R["NG"]
    T8 = T // 8

    def body(do_ref, u_ref, sr_ref, si_ref, hr_ref, hi_ref, y_ref, bsr_ref, bsi_ref, tabr_ref, tabi_ref,
             cr_ref, ci_ref, d_ref, wg_ref, gb_ref,
             du_ref, dbsr_ref, dbsi_ref, dcr_ref, dci_ref, dd_ref, dwg_ref, dgb_ref, dar_ref, dai_ref,
             carr_ref, cari_ref, lam_r, lam_i):
        c = pl.program_id(1)
        first = c == 0

        @pl.when(first)
        def _():
            carr_ref[...] = jnp.zeros_like(carr_ref)
            cari_ref[...] = jnp.zeros_like(cari_ref)

        u = u_ref[...]
        ub = u.astype(BF16)
        y = y_ref[...]
        yg, gelu_vjp = jax.vjp(jax.nn.gelu, y)
        ygb = yg.astype(BF16)
        gate = jax.nn.sigmoid(_dot(ygb, wg_ref[...]) + gb_ref[...])
        dout = do_ref[...]
        dt = dout * yg * gate * (1.0 - gate)
        dtb = dt.astype(BF16)
        dyg = dout * gate + _dot(dtb, wg_ref[...], NT)
        (dy,) = gelu_vjp(dyg)
        dyb = dy.astype(BF16)
        _accumulate(dwg_ref, _dot(ygb, dtb, TN), first)
        _accumulate(dgb_ref, jnp.sum(dt, axis=0, keepdims=True), first)
        _accumulate(dd_ref, jnp.sum(dy * u, axis=0, keepdims=True), first)
        _accumulate(dcr_ref, _dot(sr_ref[...].astype(BF16), dyb, TN), first)
        _accumulate(dci_ref, -_dot(si_ref[...].astype(BF16), dyb, TN), first)
        earliest = cidx(c) == 0
        row8 = lax.broadcasted_iota(jnp.int32, (8, LANES), 0)
        nstrips = STATE_TILE // LANES

        def project(s):
            sl = slice(s * LANES, (s + 1) * LANES)
            return _dot(dyb, cr_ref[sl, :], NT), -_dot(dyb, ci_ref[sl, :], NT)

        ahead = project(0)
        for s in range(nstrips):
            sl = slice(s * LANES, (s + 1) * LANES)
            tab = lambda r0, n=1, sl=sl: (tabr_ref[r0:r0 + n, sl], -tabi_ref[r0:r0 + n, sl])
            xr, xi = ahead
            if s + 1 < nstrips:
                ahead = project(s + 1)
            ar, ai = tab(R["POW"])
            lr, li = [None] * NG, [None] * NG
            lr[NG - 1], li[NG - 1] = xr[8 * (NG - 1):8 * NG], xi[8 * (NG - 1):8 * NG]
            for g in range(NG - 2, -1, -1):
                lr[g], li[g] = _cmul_add(xr[8 * g:8 * g + 8], xi[8 * g:8 * g + 8], ar, ai, lr[g + 1], li[g + 1])
            yr, yi = lr[0], li[0]
            for k, dist in enumerate((1, 2, 4)):
                kr, ki = tab(R["LA"] + k)
                keep = row8 < 8 - dist
                yr, yi = _cmul_add(yr, yi, kr, ki, jnp.where(keep, pltpu.roll(yr, 8 - dist, 0), 0.0),
                                   jnp.where(keep, pltpu.roll(yi, 8 - dist, 0), 0.0))
            c0r, c0i = carr_ref[0:1, sl], cari_ref[0:1, sl]
            par, pai = tab(R["PAR"], 8)
            er, ei = _cmul_add(yr, yi, par, pai, c0r, c0i)
            carr_ref[:, sl] = er
            cari_ref[:, sl] = ei
            last = row8 == 7
            inr = jnp.where(last, c0r, pltpu.roll(er, 7, 0))
            ini = jnp.where(last, c0i, pltpu.roll(ei, 7, 0))
            hr0 = jnp.where(earliest, 0.0, hr_ref[7:8, sl])
            hi0 = jnp.where(earliest, 0.0, hi_ref[7:8, sl])
            endr, endi = sr_ref[8 * (NG - 1):8 * NG, sl], si_ref[8 * (NG - 1):8 * NG, sl]
            pvr = jnp.where(row8 == 0, hr0, pltpu.roll(endr, 1, 0))
            pvi = jnp.where(row8 == 0, hi0, pltpu.roll(endi, 1, 0))
            accr = jnp.zeros((8, LANES), F32)
            acci = jnp.zeros((8, LANES), F32)
            for g in range(NG):
                pr, pi = tab(R["POW"] + NG - 1 - g)
                outr, outi = _cmul_add(lr[g], li[g], pr, pi, inr, ini)
                lam_r[8 * g:8 * g + 8, sl] = outr
                lam_i[8 * g:8 * g + 8, sl] = outi
                accr = accr + (outr * pvr + outi * pvi)
                acci = acci + (outi * pvr - outr * pvi)
                pvr, pvi = sr_ref[8 * g:8 * g + 8, sl], si_ref[8 * g:8 * g + 8, sl]

            @pl.when(first)
            def _():
                dar_ref[:, sl] = accr
                dai_ref[:, sl] = acci

            @pl.when(jnp.logical_not(first))
            def _():
                dar_ref[:, sl] += accr
                dai_ref[:, sl] += acci

        lrb = lam_r[...].astype(BF16)
        lib = lam_i[...].astype(BF16)
        _accumulate(dbsr_ref, _dot(ub, lrb, TN), first)
        _accumulate(dbsi_ref, _dot(ub, lib, TN), first)
        du_ref[...] = dy * d_ref[...] + _dot(lrb, bsr_ref[...], NT) + _dot(lib, bsi_ref[...], NT)

    halo = pl.BlockSpec((8, STATE_TILE), lambda j, c: (jnp.maximum(cidx(c) * T8 - 1, 0), j))
    f = lambda *s: jax.ShapeDtypeStruct(s, F32)
    return _pcall(
        body, name="s5_bwd", grid=(GT, nc),
        in_specs=[S["chan"], S["chan"], S["state"], S["state"], halo, halo, S["chan"], S["bmat"], S["bmat"],
                  S["tab"], S["tab"], S["cmat"], S["cmat"], S["cvec"], S["gmat"], S["cvec"]],
        out_specs=[S["chan"], S["bmat"], S["bmat"], S["cmat"], S["cmat"], S["cvec"], S["gmat"], S["cvec"],
                   S["svec8"], S["svec8"]],
        out_shape=[f(L, SW), f(GT, LANES, STATE_TILE), f(GT, LANES, STATE_TILE), f(GT, STATE_TILE, LANES),
                   f(GT, STATE_TILE, LANES), f(1, SW), f(GT, LANES, LANES), f(1, SW), f(8, NS), f(8, NS)],
        scratch=[pltpu.VMEM((8, STATE_TILE), F32), pltpu.VMEM((8, STATE_TILE), F32),
                 pltpu.VMEM((T, STATE_TILE), F32), pltpu.VMEM((T, STATE_TILE), F32)],
    )(do, u, s_re, s_im, s_re, s_im, ypre, sp["bsr"], sp["bsi"], sp["tabr"], sp["tabi"],
      sp["crT"], sp["ciT"], sp["d"], sp["wg"], sp["gb"])


def _conv_taps(xc, h6, h7, row):
    x1 = jnp.where(row == 0, h7, pltpu.roll(xc, 1, 0))
    x2 = jnp.where(row == 0, h6, jnp.where(row == 1, h7, pltpu.roll(xc, 2, 0)))
    return x1, x2


def _conv_halves(up_ref, halo_ref, w_ref, b_ref, tm, FS):
    first = pl.program_id(1) == 0
    row = lax.broadcasted_iota(jnp.int32, (tm, FS), 0)
    outs, taps = [], []
    for s in (0, 1):
        xc = up_ref[s]
        h6 = jnp.where(first, 0.0, halo_ref[s, 6:7, :])
        h7 = jnp.where(first, 0.0, halo_ref[s, 7:8, :])
        x1, x2 = _conv_taps(xc, h6, h7, row)
        outs.append(b_ref[s] + x2 * w_ref[s, 0:1, :] + x1 * w_ref[s, 1:2, :] + xc * w_ref[s, 2:3, :])
        taps.append((x2, x1, xc))
    return outs, taps


def _convglu_specs(L, FS, tm):
    t8 = tm // 8
    return dict(
        up=pl.BlockSpec((2, None, tm, FS), lambda j, i: (0, j, i, 0)),
        halo=pl.BlockSpec((2, None, 8, FS), lambda j, i: (0, j, jnp.maximum(i * t8 - 1, 0), 0)),
        w=pl.BlockSpec((2, None, 3, FS), lambda j, i: (0, j, 0, 0)),
        b=pl.BlockSpec((2, None, 1, FS), lambda j, i: (0, j, 0, 0)),
        act=pl.BlockSpec((None, tm, FS), lambda j, i: (j, i, 0)),
    )


def _convglu_fwd(up, w, b):
    _, NSH, L, FS = up.shape
    tm = _row_tile(L)
    S = _convglu_specs(L, FS, tm)

    def body(up_ref, halo_ref, w_ref, b_ref, act_ref):
        (val, gate), _ = _conv_halves(up_ref, halo_ref, w_ref, b_ref, tm, FS)
        act_ref[...] = _glu_math(val, gate).astype(BF16)

    return _pcall(body, name="convglu_fwd", grid=(NSH, L // tm), in_specs=[S["up"], S["halo"], S["w"], S["b"]],
                  out_specs=S["act"], out_shape=jax.ShapeDtypeStruct((NSH, L, FS), BF16))(up, up, w, b)


def _convglu_bwd(up, dact, w, b):
    _, NSH, L, FS = up.shape
    tm = _row_tile(L)
    S = _convglu_specs(L, FS, tm)

    def body(up_ref, halo_ref, w_ref, b_ref, dact_ref, dc_ref, dw_ref, db_ref):
        first = pl.program_id(1) == 0
        (val, gate), taps = _conv_halves(up_ref, halo_ref, w_ref, b_ref, tm, FS)
        _, vjp = jax.vjp(_glu_math, val, gate)
        dcs = vjp(dact_ref[...])
        for s in (0, 1):
            dc = dcs[s]
            dc_ref[s] = dc
            sums = [jnp.sum(dc * t, axis=0, keepdims=True) for t in taps[s]]
            dbs = jnp.sum(dc, axis=0, keepdims=True)

            @pl.when(first)
            def _():
                for t in range(3):
                    dw_ref[s, t:t + 1, :] = sums[t]
                db_ref[s] = dbs

            @pl.when(jnp.logical_not(first))
            def _():
                for t in range(3):
                    dw_ref[s, t:t + 1, :] += sums[t]
                db_ref[s] += dbs

    f = lambda *s: jax.ShapeDtypeStruct(s, F32)
    return _pcall(body, name="convglu_bwd", grid=(NSH, L // tm),
                  in_specs=[S["up"], S["halo"], S["w"], S["b"], S["act"]],
                  out_specs=[S["up"], S["w"], S["b"]],
                  out_shape=[f(2, NSH, L, FS), f(2, NSH, 3, FS), f(2, NSH, 1, FS)])(up, up, w, b, dact)


def _conv_transpose(dc, w):
    NS8, L, FS = dc.shape
    tm = _row_tile(L)
    t8 = tm // 8
    nt = L // tm

    def body(dc_ref, nxt_ref, w_ref, o_ref):
        last = pl.program_id(1) == nt - 1
        row = lax.broadcasted_iota(jnp.int32, (tm, FS), 0)
        xc = dc_ref[...]
        n0 = jnp.where(last, 0.0, nxt_ref[0:1, :])
        n1 = jnp.where(last, 0.0, nxt_ref[1:2, :])
        x1 = jnp.where(row == tm - 1, n0, pltpu.roll(xc, tm - 1, 0))
        x2 = jnp.where(row == tm - 1, n1, jnp.where(row == tm - 2, n0, pltpu.roll(xc, tm - 2, 0)))
        o_ref[...] = (xc * w_ref[2:3, :] + x1 * w_ref[1:2, :] + x2 * w_ref[0:1, :]).astype(BF16)

    blk = pl.BlockSpec((None, tm, FS), lambda j, i: (j, i, 0))
    nxt = pl.BlockSpec((None, 8, FS), lambda j, i: (j, jnp.minimum((i + 1) * t8, L // 8 - 1), 0))
    wsp = pl.BlockSpec((None, 3, FS), lambda j, i: (j, 0, 0))
    return _pcall(body, name="conv_transpose", grid=(NS8, nt), in_specs=[blk, nxt, wsp], out_specs=blk,
                  out_shape=jax.ShapeDtypeStruct((NS8, L, FS), BF16))(dc, dc, w)


def _loss_head(y, target):
    L, D = y.shape
    tm = _row_tile(L)

    def body(y_ref, t_ref, loss_ref, dy_ref):
        err = y_ref[...] - t_ref[...]
        dy_ref[...] = err / D
        part = 0.5 * jnp.sum(jnp.mean(err * err, axis=-1, keepdims=True), axis=0, keepdims=True)
        _accumulate(loss_ref, jnp.broadcast_to(part, (1, LANES)), pl.program_id(0) == 0)

    row = pl.BlockSpec((tm, D), lambda i: (i, 0))
    vec = pl.BlockSpec((1, LANES), lambda i: (0, 0))
    return _pcall(body, name="loss_head", grid=(L // tm,), in_specs=[row, row], out_specs=[vec, row],
                  out_shape=[jax.ShapeDtypeStruct((1, LANES), F32), jax.ShapeDtypeStruct((L, D), F32)])(y, target)


def _ada_fwd(c_all, ada_w, ada_b):
    NL, D, NC = ada_w.shape
    NB = c_all.shape[0]

    def body(c_ref, w_ref, b_ref, o_ref):
        cact = jax.nn.silu(c_ref[...])
        o_ref[...] = _dot(cact.astype(BF16), w_ref[...].astype(BF16)) + b_ref[...]

    return _pcall(body, name="ada_fwd", grid=(NL,),
                  in_specs=[pl.BlockSpec((NB, D), lambda l: (0, 0)), pl.BlockSpec((None, D, NC), lambda l: (l, 0, 0)),
                            pl.BlockSpec((None, 1, NC), lambda l: (l, 0, 0))],
                  out_specs=pl.BlockSpec((None, NB, NC), lambda l: (l, 0, 0)),
                  out_shape=jax.ShapeDtypeStruct((NL, NB, NC), F32))(c_all, ada_w, ada_b)


def _ada_bwd(c_all_t, dmod):
    D, NB = c_all_t.shape
    NL, _, NC = dmod.shape

    def body(c_ref, d_ref, o_ref):
        cact = jax.nn.silu(c_ref[...]).astype(BF16).astype(F32)
        o_ref[...] = _dot(cact, d_ref[...].astype(BF16).astype(F32))

    return _pcall(body, name="ada_bwd", grid=(NL,),
                  in_specs=[pl.BlockSpec((D, NB), lambda l: (0, 0)), pl.BlockSpec((None, NB, NC), lambda l: (l, 0, 0))],
                  out_specs=pl.BlockSpec((None, D, NC), lambda l: (l, 0, 0)),
                  out_shape=jax.ShapeDtypeStruct((NL, D, NC), F32))(c_all_t, dmod)


def _adamw(parts, w, m, v):
    NL, P, R, C = parts.shape
    tr = R
    for cand in (256, 128, 64, 32, 16, 8):
        if R % cand == 0 and R > cand:
            tr = cand
            break

    def body(p_ref, w_ref, m_ref, v_ref, g_ref, d_ref, nm_ref, nv_ref):
        g = p_ref[0].astype(F32)
        for k in range(1, P):
            g = g + p_ref[k].astype(F32)
        m2 = ADAM_B1 * m_ref[...] + (1.0 - ADAM_B1) * g
        v2 = ADAM_B2 * v_ref[...] + (1.0 - ADAM_B2) * jnp.square(g)
        m_hat = m2 / (1.0 - ADAM_B1 ** ADAM_STEP)
        v_hat = v2 / (1.0 - ADAM_B2 ** ADAM_STEP)
        g_ref[...] = g
        d_ref[...] = -ADAM_LR * (m_hat / (jnp.sqrt(v_hat) + ADAM_EPS) + ADAM_WD * w_ref[...])
        nm_ref[...] = m2
        nv_ref[...] = v2

    pspec = pl.BlockSpec((None, P, tr, C), lambda l, i: (l, 0, i, 0))
    wspec = pl.BlockSpec((None, tr, C), lambda l, i: (l, i, 0))
    shp = jax.ShapeDtypeStruct((NL, R, C), F32)
    return _pcall(body, name="adamw", grid=(NL, R // tr), in_specs=[pspec, wspec, wspec, wspec],
                  out_specs=[wspec] * 4, out_shape=[shp] * 4)(parts, w, m, v)


def _block_diag(blocks):
    *lead, g, r, c = blocks.shape
    eye = jnp.eye(g, dtype=bool)[:, None, :, None]
    full = jnp.where(eye, blocks[..., :, :, None, :], 0.0)
    return full.reshape(*lead, g * r, g * c)


def _block_diag_extract(m, r, c):
    g = GROUPS_PER_TILE
    m5 = m.reshape(*m.shape[:-2], g, r, g, c)
    eye = jnp.eye(g, dtype=bool)[:, None, :, None]
    return jnp.sum(jnp.where(eye, m5, 0.0), axis=-2)


def kernel(x, c, ada_w, ada_b, norm1_g, w_in, q_norm_g, k_norm_g, ssm_a_re, ssm_a_im, ssm_log_dt, ssm_b_re, ssm_b_im, ssm_c_re, ssm_c_im, ssm_d, glu_w, glu_b, attn_out_g, ssm_out_g, w_out, norm2_g, ffn_w_up, ffn_conv_w, ffn_conv_b, ffn_w_down, loss_target, m_ada_w, m_ada_b, m_norm1_g, m_w_in, m_q_norm_g, m_k_norm_g, m_ssm_a_re, m_ssm_a_im, m_ssm_log_dt, m_ssm_b_re, m_ssm_b_im, m_ssm_c_re, m_ssm_c_im, m_ssm_d, m_glu_w, m_glu_b, m_attn_out_g, m_ssm_out_g, m_w_out, m_norm2_g, m_ffn_w_up, m_ffn_conv_w, m_ffn_conv_b, m_ffn_w_down, v_ada_w, v_ada_b, v_norm1_g, v_w_in, v_q_norm_g, v_k_norm_g, v_ssm_a_re, v_ssm_a_im, v_ssm_log_dt, v_ssm_b_re, v_ssm_b_im, v_ssm_c_re, v_ssm_c_im, v_ssm_d, v_glu_w, v_glu_b, v_attn_out_g, v_ssm_out_g, v_w_out, v_norm2_g, v_ffn_w_up, v_ffn_conv_w, v_ffn_conv_b, v_ffn_w_down):
    weights = dict(ada_w=ada_w, ada_b=ada_b, norm1_g=norm1_g, w_in=w_in, q_norm_g=q_norm_g, k_norm_g=k_norm_g,
                   ssm_a_re=ssm_a_re, ssm_a_im=ssm_a_im, ssm_log_dt=ssm_log_dt, ssm_b_re=ssm_b_re,
                   ssm_b_im=ssm_b_im, ssm_c_re=ssm_c_re, ssm_c_im=ssm_c_im, ssm_d=ssm_d, glu_w=glu_w, glu_b=glu_b,
                   attn_out_g=attn_out_g, ssm_out_g=ssm_out_g, w_out=w_out, norm2_g=norm2_g, ffn_w_up=ffn_w_up,
                   ffn_conv_w=ffn_conv_w, ffn_conv_b=ffn_conv_b, ffn_w_down=ffn_w_down)
    mom_m = dict(ada_w=m_ada_w, ada_b=m_ada_b, norm1_g=m_norm1_g, w_in=m_w_in, q_norm_g=m_q_norm_g,
                 k_norm_g=m_k_norm_g, ssm_a_re=m_ssm_a_re, ssm_a_im=m_ssm_a_im, ssm_log_dt=m_ssm_log_dt,
                 ssm_b_re=m_ssm_b_re, ssm_b_im=m_ssm_b_im, ssm_c_re=m_ssm_c_re, ssm_c_im=m_ssm_c_im, ssm_d=m_ssm_d,
                 glu_w=m_glu_w, glu_b=m_glu_b, attn_out_g=m_attn_out_g, ssm_out_g=m_ssm_out_g, w_out=m_w_out,
                 norm2_g=m_norm2_g, ffn_w_up=m_ffn_w_up, ffn_conv_w=m_ffn_conv_w, ffn_conv_b=m_ffn_conv_b,
                 ffn_w_down=m_ffn_w_down)
    mom_v = dict(ada_w=v_ada_w, ada_b=v_ada_b, norm1_g=v_norm1_g, w_in=v_w_in, q_norm_g=v_q_norm_g,
                 k_norm_g=v_k_norm_g, ssm_a_re=v_ssm_a_re, ssm_a_im=v_ssm_a_im, ssm_log_dt=v_ssm_log_dt,
                 ssm_b_re=v_ssm_b_re, ssm_b_im=v_ssm_b_im, ssm_c_re=v_ssm_c_re, ssm_c_im=v_ssm_c_im, ssm_d=v_ssm_d,
                 glu_w=v_glu_w, glu_b=v_glu_b, attn_out_g=v_attn_out_g, ssm_out_g=v_ssm_out_g, w_out=v_w_out,
                 norm2_g=v_norm2_g, ffn_w_up=v_ffn_w_up, ffn_conv_w=v_ffn_conv_w, ffn_conv_b=v_ffn_conv_b,
                 ffn_w_down=v_ffn_w_down)
    names = list(weights)
    big = ("ada_w", "w_in", "w_out", "ffn_w_up", "ffn_conv_w", "ffn_w_down")
    small = [n for n in names if n not in big]

    x = x[0]
    target = loss_target[0]
    L, D = x.shape
    NL = ada_w.shape[0]
    AW = D // 2
    SW = D - AW
    NH = AW // HEAD_DIM
    HP = AW // LANES
    G = SW // SSM_GROUP
    GT = SW // LANES
    NS = G * SSM_STATE
    NIN = w_in.shape[-1]
    FS = ffn_w_up.shape[-1]
    NSH = NDEV // 2
    NCA = ada_w.shape[-1]
    ROWS_OUT = w_out.shape[1]
    ROWS_DOWN = ffn_w_down.shape[1]
    B_ATT = min(L, 256)
    T_S5 = min(L, 512)
    tm = _row_tile(L, 1024)
    me = _my_index()

    cpad = jnp.reshape(c, (D // LANES, LANES))
    c_all = _gather_small(cpad, "gather_c")
    c_all = c_all.reshape(NDEV, D)
    w_in_g, w_out_g, w_up_g, conv_w_g, w_down_g = _gather_shards(
        [w_in.astype(BF16), w_out.astype(BF16), ffn_w_up.astype(BF16), ffn_conv_w, ffn_w_down.astype(BF16)],
        "gather_weights")
    w_out_g = w_out_g.reshape(NL, D, D)
    w_down_g = w_down_g.reshape(NL, NSH, 2 * ROWS_DOWN, D)
    conv_w_g = conv_w_g.reshape(NL, 2, NSH, 3, FS)
    conv_b_g = ffn_conv_b.reshape(NL, 2, NSH, 1, FS)

    ada_b_mine = lax.dynamic_slice_in_dim(ada_b, me * NCA, NCA, axis=1).reshape(NL, 1, NCA)
    mod_part = _ada_fwd(c_all, ada_w, ada_b_mine)
    mod_all = _gather_small(mod_part, "gather_mod")
    mod = lax.dynamic_index_in_dim(mod_all, me, axis=2, keepdims=False)
    mod = jnp.transpose(mod, (1, 0, 2)).reshape(NL, N_MOD, 1, D)

    row = lambda a: a.reshape(NL, 1, NS)
    ar_row, ai_row = row(ssm_a_re), row(ssm_a_im)
    ldt_row = row(jnp.broadcast_to(ssm_log_dt[:, :, None], (NL, G, SSM_STATE)))
    tiles = lambda a: a.reshape((NL, GT, GROUPS_PER_TILE) + a.shape[2:])
    braw_r = _block_diag(jnp.swapaxes(tiles(ssm_b_re), -1, -2))
    braw_i = _block_diag(jnp.swapaxes(tiles(ssm_b_im), -1, -2))
    crT = _block_diag(jnp.swapaxes(tiles(ssm_c_re), -1, -2)).astype(BF16)
    ciT = _block_diag(jnp.swapaxes(tiles(ssm_c_im), -1, -2)).astype(BF16)
    wg = _block_diag(tiles(glu_w)).astype(BF16)
    abr, abi, tabr, tabi, bsr, bsi = _s5_prep(ar_row, ai_row, ldt_row, braw_r, braw_i, T_S5)
    s5p = dict(bsr=bsr, bsi=bsi, tabr=tabr, tabi=tabi, crT=crT, ciT=ciT,
               d=ssm_d.reshape(NL, 1, SW), wg=wg, gb=glu_b.reshape(NL, 1, SW))

    gqk = jnp.concatenate([jnp.tile(q_norm_g, (1, NH)), jnp.tile(k_norm_g, (1, NH))], axis=1).reshape(NL, 1, 2 * AW)
    layer_params = dict(
        mod=mod, norm1_g=norm1_g.reshape(NL, 1, D), norm2_g=norm2_g.reshape(NL, 1, D), gqk=gqk,
        ga=attn_out_g.reshape(NL, 1, AW), gs=ssm_out_g.reshape(NL, 1, SW),
        w_in=w_in_g, w_out=w_out_g, w_up=w_up_g, conv_w=conv_w_g, conv_b=conv_b_g, w_down=w_down_g, s5=s5p)


    def mm_rows_shards(name, a, b, n, out_dtype=F32):
        K = a.shape[1]
        return _matmul(name, a, b, dims=NN, grid=(L // tm, NDEV),
                       a_spec=pl.BlockSpec((tm, K), lambda i, j: (i, 0)),
                       b_spec=pl.BlockSpec((None, K, n), lambda i, j: (j, 0, 0)),
                       out_shape=jax.ShapeDtypeStruct((L, NDEV * n), out_dtype),
                       out_specs=pl.BlockSpec((tm, n), lambda i, j: (i, j)))

    tn = min(D, 512)

    def resid_epilogue(acc, xres, gate):
        return acc, xres + gate * acc

    def layer_fwd(xin, lp):
        sh1, sc1, g1, sh2, sc2, g2 = (lp["mod"][k] for k in range(N_MOD))
        h = _lnmod_fwd(xin, lp["norm1_g"], sh1, sc1)
        p = mm_rows_shards("mm_in", h, lp["w_in"], NIN)
        qk = _qknorm_fwd(p, lp["gqk"], AW)
        o_attn, tot = _attn_fwd(qk, p, AW, B_ATT)
        u_st = _to_streams(p[:, 3 * AW:], T_S5)
        o_st, s_re, s_im, ypre = _s5_fwd(u_st, lp["s5"], T_S5)
        o_ssm = _from_streams(o_st, T_S5)
        o = _outnorm_fwd(o_attn, o_ssm, lp["ga"], lp["gs"])
        a1, x_mid = _matmul(
            "mm_out", o, lp["w_out"], dims=NN, grid=(L // tm, D // tn),
            a_spec=pl.BlockSpec((tm, D), lambda i, j: (i, 0)), b_spec=pl.BlockSpec((D, tn), lambda i, j: (0, j)),
            extra=(xin, g1), extra_specs=(pl.BlockSpec((tm, tn), lambda i, j: (i, j)),
                                          pl.BlockSpec((1, tn), lambda i, j: (0, j))),
            epilogue=resid_epilogue,
            out_shape=[jax.ShapeDtypeStruct((L, D), F32)] * 2,
            out_specs=[pl.BlockSpec((tm, tn), lambda i, j: (i, j))] * 2)
        h2 = _lnmod_fwd(x_mid, lp["norm2_g"], sh2, sc2)
        up = _matmul(
            "mm_up", h2, lp["w_up"], dims=NN, grid=(L // tm, NDEV),
            a_spec=pl.BlockSpec((tm, D), lambda i, j: (i, 0)), b_spec=pl.BlockSpec((None, D, FS), lambda i, j: (j, 0, 0)),
            out_shape=jax.ShapeDtypeStruct((NDEV, L, FS), F32),
            out_specs=pl.BlockSpec((None, tm, FS), lambda i, j: (j, i, 0)))
        up = up.reshape(2, NSH, L, FS)
        act = _convglu_fwd(up, lp["conv_w"], lp["conv_b"])
        a2, x_out = _matmul(
            "mm_down", act, lp["w_down"], dims=NN, grid=(L // tm, D // tn, NSH), kaxis=2, acc_shape=(tm, tn),
            a_spec=pl.BlockSpec((None, tm, FS), lambda i, j, k: (k, i, 0)),
            b_spec=pl.BlockSpec((None, FS, tn), lambda i, j, k: (k, 0, j)),
            extra=(x_mid, g2), extra_specs=(pl.BlockSpec((tm, tn), lambda i, j, k: (i, j)),
                                            pl.BlockSpec((1, tn), lambda i, j, k: (0, j))),
            epilogue=resid_epilogue,
            out_shape=[jax.ShapeDtypeStruct((L, D), F32)] * 2,
            out_specs=[pl.BlockSpec((tm, tn), lambda i, j, k: (i, j))] * 2)
        res = dict(x=xin, h=h, p=p, qk=qk, tot=tot, o_attn=o_attn, o_ssm=o_ssm, u_st=u_st, s_re=s_re, s_im=s_im, ypre=ypre,
                   o=o, a1=a1, x_mid=x_mid, h2=h2, up=up, act=act, a2=a2)
        return x_out, res

    per_layer = [jax.tree.map(lambda a: a[l], layer_params) for l in range(NL)]
    y, residuals = x, []
    for l in range(NL):
        y, res = layer_fwd(y, per_layer[l])
        residuals.append(res)

    loss_row, dy = _loss_head(y, target)
    loss = lax.psum(loss_row[0, 0], ("x", "y", "c"))

    def layer_bwd(dx, args):
        lp, r = args
        sh1, sc1, g1, sh2, sc2, g2 = (lp["mod"][k] for k in range(N_MOD))
        da2, dg2 = _gate_bwd(dx, r["a2"], g2)
        dact = _matmul(
            "mm_dact", da2, lp["w_down"], dims=NT, grid=(L // tm, NSH),
            a_spec=pl.BlockSpec((tm, D), lambda i, j: (i, 0)), b_spec=pl.BlockSpec((None, FS, D), lambda i, j: (j, 0, 0)),
            out_shape=jax.ShapeDtypeStruct((NSH, L, FS), F32),
            out_specs=pl.BlockSpec((None, tm, FS), lambda i, j: (j, i, 0)))
        dw_down = _matmul(
            "mm_dw_down", r["act"], da2, dims=TN, grid=(NSH, D // tn, L // tm), kaxis=2, acc_shape=(FS, tn),
            a_spec=pl.BlockSpec((None, tm, FS), lambda j, n, k: (j, k, 0)),
            b_spec=pl.BlockSpec((tm, tn), lambda j, n, k: (k, n)),
            out_shape=jax.ShapeDtypeStruct((NSH, FS, D), BF16),
            out_specs=pl.BlockSpec((None, FS, tn), lambda j, n, k: (j, 0, n)))
        dconv, dcw, dcb = _convglu_bwd(r["up"], dact, lp["conv_w"], lp["conv_b"])
        dup = _conv_transpose(dconv.reshape(NDEV, L, FS), lp["conv_w"].reshape(NDEV, 3, FS))
        dh2 = _matmul(
            "mm_dh2", dup, lp["w_up"], dims=NT, grid=(L // tm, NDEV), kaxis=1, acc_shape=(tm, D),
            a_spec=pl.BlockSpec((None, tm, FS), lambda i, k: (k, i, 0)),
            b_spec=pl.BlockSpec((None, D, FS), lambda i, k: (k, 0, 0)),
            out_shape=jax.ShapeDtypeStruct((L, D), F32), out_specs=pl.BlockSpec((tm, D), lambda i, k: (i, 0)))
        dw_up = _matmul(
            "mm_dw_up", r["h2"], dup, dims=TN, grid=(NDEV, L // tm), kaxis=1, acc_shape=(D, FS),
            a_spec=pl.BlockSpec((tm, D), lambda j, k: (k, 0)), b_spec=pl.BlockSpec((None, tm, FS), lambda j, k: (j, k, 0)),
            out_shape=jax.ShapeDtypeStruct((NDEV, D, FS), BF16),
            out_specs=pl.BlockSpec((None, D, FS), lambda j, k: (j, 0, 0)))
        dxm, dn2, dsh2, dsc2 = _lnmod_bwd(dh2, r["x_mid"], lp["norm2_g"], sh2, sc2, dx)
        da1, dg1 = _gate_bwd(dxm, r["a1"], g1)
        do = _matmul(
            "mm_do", da1, lp["w_out"], dims=NT, grid=(L // tm, D // tn),
            a_spec=pl.BlockSpec((tm, D), lambda i, j: (i, 0)), b_spec=pl.BlockSpec((tn, D), lambda i, j: (j, 0)),
            out_shape=jax.ShapeDtypeStruct((L, D), F32), out_specs=pl.BlockSpec((tm, tn), lambda i, j: (i, j)))
        dw_out = _matmul(
            "mm_dw_out", r["o"], da1, dims=TN, grid=(D // tn, D // tn, L // tm), kaxis=2, acc_shape=(tn, tn),
            a_spec=pl.BlockSpec((tm, tn), lambda m, n, k: (k, m)), b_spec=pl.BlockSpec((tm, tn), lambda m, n, k: (k, n)),
            out_shape=jax.ShapeDtypeStruct((D, D), BF16), out_specs=pl.BlockSpec((tn, tn), lambda m, n, k: (m, n)))
        doa, dos, dga, dgs = _outnorm_bwd(do, r["o_attn"], r["o_ssm"], lp["ga"], lp["gs"])
        (du_st, dbsr, dbsi, dcr, dci, dd, dwg, dgb, dabr, dabi) = _s5_bwd(
            _to_streams(dos, T_S5), r["u_st"], r["s_re"], r["s_im"], r["ypre"], lp["s5"], T_S5)
        du = _from_streams(du_st, T_S5)
        dq, dk, dv = _attn_bwd(r["qk"], r["p"], doa, r["tot"], AW, B_ATT)
        dqk, dgqk = _qknorm_bwd(jnp.concatenate([dq, dk], axis=1), r["p"], lp["gqk"], AW)
        dp = jnp.concatenate([dqk, dv, du], axis=1).astype(BF16)
        dh = _matmul(
            "mm_dh", dp, lp["w_in"], dims=NT, grid=(L // tm, NDEV), kaxis=1, acc_shape=(tm, D),
            a_spec=pl.BlockSpec((tm, NIN), lambda i, k: (i, k)), b_spec=pl.BlockSpec((None, D, NIN), lambda i, k: (k, 0, 0)),
            out_shape=jax.ShapeDtypeStruct((L, D), F32), out_specs=pl.BlockSpec((tm, D), lambda i, k: (i, 0)))
        dw_in = _matmul(
            "mm_dw_in", r["h"], dp, dims=TN, grid=(NDEV, L // tm), kaxis=1, acc_shape=(D, NIN),
            a_spec=pl.BlockSpec((tm, D), lambda j, k: (k, 0)), b_spec=pl.BlockSpec((tm, NIN), lambda j, k: (k, j)),
            out_shape=jax.ShapeDtypeStruct((NDEV, D, NIN), BF16),
            out_specs=pl.BlockSpec((None, D, NIN), lambda j, k: (j, 0, 0)))
        dx0, dn1, dsh1, dsc1 = _lnmod_bwd(dh, r["x"], lp["norm1_g"], sh1, sc1, dxm)
        grads = dict(
            dmod=jnp.concatenate([dsh1, dsc1, dg1, dsh2, dsc2, dg2], axis=1), dn1=dn1, dn2=dn2, dgqk=dgqk,
            dga=dga, dgs=dgs, dbsr=dbsr, dbsi=dbsi, dcr=dcr, dci=dci, dd=dd, dwg=dwg, dgb=dgb, dabr=dabr, dabi=dabi,
            dcb=dcb, dw_in=dw_in, dw_out=dw_out.reshape(NDEV, ROWS_OUT, D), dw_up=dw_up,
            dcw=dcw.reshape(NDEV, 3, FS), dw_down=dw_down.reshape(NDEV, ROWS_DOWN, D))
        return dx0, grads

    grad_x, layer_grads = dy, [None] * NL
    for l in reversed(range(NL)):
        grad_x, layer_grads[l] = layer_bwd(grad_x, (per_layer[l], residuals[l]))
    gr = jax.tree.map(lambda *a: jnp.stack(a), *layer_grads)

    dar, dai, dldt, dbr_bd, dbi_bd = _s5_prep_bwd(ar_row, ai_row, ldt_row, braw_r, braw_i,
                                                  gr["dabr"], gr["dabi"], gr["dbsr"], gr["dbsi"])
    unt = lambda a: a.reshape((NL, G) + a.shape[3:])
    local = dict(
        ada_b=gr["dmod"].reshape(NL, N_MOD * D),
        norm1_g=gr["dn1"].reshape(NL, D), norm2_g=gr["dn2"].reshape(NL, D),
        q_norm_g=gr["dgqk"].reshape(NL, 2, NH, HEAD_DIM)[:, 0].sum(axis=1),
        k_norm_g=gr["dgqk"].reshape(NL, 2, NH, HEAD_DIM)[:, 1].sum(axis=1),
        ssm_a_re=dar.reshape(NL, G, SSM_STATE), ssm_a_im=dai.reshape(NL, G, SSM_STATE),
        ssm_log_dt=dldt.reshape(NL, G, SSM_STATE).sum(axis=-1),
        ssm_b_re=jnp.swapaxes(unt(_block_diag_extract(dbr_bd, SSM_GROUP, SSM_STATE)), -1, -2),
        ssm_b_im=jnp.swapaxes(unt(_block_diag_extract(dbi_bd, SSM_GROUP, SSM_STATE)), -1, -2),
        ssm_c_re=jnp.swapaxes(unt(_block_diag_extract(gr["dcr"], SSM_STATE, SSM_GROUP)), -1, -2),
        ssm_c_im=jnp.swapaxes(unt(_block_diag_extract(gr["dci"], SSM_STATE, SSM_GROUP)), -1, -2),
        ssm_d=gr["dd"].reshape(NL, G, SSM_GROUP),
        glu_w=unt(_block_diag_extract(gr["dwg"], SSM_GROUP, SSM_GROUP)),
        glu_b=gr["dgb"].reshape(NL, G, SSM_GROUP),
        attn_out_g=gr["dga"].reshape(NL, AW), ssm_out_g=gr["dgs"].reshape(NL, SW),
        ffn_conv_b=gr["dcb"].reshape(NL, 2 * NSH * FS),
    )

    def pack(tree):
        flat = jnp.concatenate([tree[n].reshape(-1) for n in small])
        pad = (-flat.shape[0]) % (512 * LANES)
        return jnp.pad(flat, (0, pad)).reshape(1, -1, LANES)

    (small_parts,) = _gather_shards([pack(local)], "gather_small_grads")
    sg, sd, sm, sv = _adamw(small_parts, pack(weights), pack(mom_m), pack(mom_v))

    def unpack(buf):
        flat = buf.reshape(-1)
        out, off = {}, 0
        for n in small:
            size = weights[n].size
            out[n] = flat[off:off + size].reshape(weights[n].shape)
            off += size
        return out

    ug, ud, um, uv = unpack(sg), unpack(sd), unpack(sm), unpack(sv)
    results = {n: (ug[n], ud[n], um[n], uv[n]) for n in small}

    dmod_all = _gather_small(gr["dmod"].reshape(NL, N_MOD * D), "gather_dmod")
    dmod_mine = lax.dynamic_slice_in_dim(dmod_all, me * NCA, NCA, axis=2)
    d_ada_w = _ada_bwd(jnp.transpose(c_all), jnp.transpose(dmod_mine, (1, 0, 2)))
    results["ada_w"] = tuple(_adamw(d_ada_w[:, None], ada_w, m_ada_w, v_ada_w))

    my_core = lax.axis_index("c")
    by_target = [g.reshape((NL, NDEV // 2, 2) + g.shape[2:])
                 for g in (gr["dw_in"], gr["dw_out"], gr["dw_up"], gr["dcw"], gr["dw_down"])]
    mine = [lax.dynamic_index_in_dim(g, my_core, axis=2, keepdims=False) for g in by_target]
    theirs = _sibling_exchange(
        [lax.dynamic_index_in_dim(g, 1 - my_core, axis=2, keepdims=False) for g in by_target], "pair_weight_grads")
    pair_sums = [_pair_add(a, b) for a, b in zip(mine, theirs)]
    parts = _chip_exchange(pair_sums, "scatter_weight_grads")
    for n, pt in zip(("w_in", "w_out", "ffn_w_up", "ffn_conv_w", "ffn_w_down"), parts):
        results[n] = tuple(_adamw(pt, weights[n], mom_m[n], mom_v[n]))

    out = [loss, grad_x[None]]
    for k in range(4):
        out.extend(results[n][k] for n in names)
    return tuple(out)
```

```python
import jax
import jax.numpy as jnp
from jax import lax
from jax.experimental import pallas as pl
from jax.experimental.pallas import tpu as pltpu

F32 = jnp.float32
BF16 = jnp.bfloat16
NDEV = 8
LANES = 128
HEAD_DIM = 64
SSM_GROUP = 16
SSM_STATE = 64
GROUPS_PER_TILE = LANES // SSM_GROUP
STATE_TILE = GROUPS_PER_TILE * SSM_STATE
N_MOD = 6
ATTN_STRIP = 32
EPS = 1e-6
ADAM_LR, ADAM_B1, ADAM_B2, ADAM_EPS, ADAM_WD, ADAM_STEP = 0.001, 0.9, 0.999, 1e-08, 0.01, 10
VMEM_LIMIT = 48 * 1024 * 1024
MESH_IDS = pl.DeviceIdType.MESH

NN = (((1,), (0,)), ((), ()))
NT = (((1,), (1,)), ((), ()))
TN = (((0,), (0,)), ((), ()))


def _dot(a, b, dims=NN):
    return lax.dot_general(a, b, dims, preferred_element_type=F32)


def _pcall(body, *, name, out_shape, in_specs, out_specs, grid=(), scratch=()):
    return pl.pallas_call(
        body, name=name, grid=grid, in_specs=in_specs, out_specs=out_specs, out_shape=out_shape,
        scratch_shapes=list(scratch),
        compiler_params=pltpu.CompilerParams(vmem_limit_bytes=VMEM_LIMIT))


def _row_tile(n, want=512):
    t = min(n, want)
    assert n % t == 0
    return t


def _my_index():
    return 4 * lax.axis_index("x") + 2 * lax.axis_index("y") + lax.axis_index("c")


HBM_SPEC = pl.BlockSpec(memory_space=pltpu.HBM)


def _mesh_place():
    x, y, c = lax.axis_index("x"), lax.axis_index("y"), lax.axis_index("c")
    chips = [(1 - x, y), (x, 1 - y), (1 - x, 1 - y)]
    return x, y, c, chips


def _gather_small(arr, name):
    def body(in_ref, out_ref, send_sems, recv_sems, local_sem):
        x, y, c, _ = _mesh_place()
        me = 4 * x + 2 * y + c
        own = pltpu.make_async_copy(in_ref, out_ref.at[me], local_sem)
        own.start()
        sends, recvs = [], []
        for k in range(1, NDEV):
            px = 1 - x if k & 4 else x
            py = 1 - y if k & 2 else y
            pc = 1 - c if k & 1 else c
            common = dict(send_sem=send_sems.at[k - 1], recv_sem=recv_sems.at[k - 1],
                          device_id=(px, py, pc), device_id_type=MESH_IDS)
            snd = pltpu.make_async_remote_copy(src_ref=in_ref, dst_ref=out_ref.at[me], **common)
            snd.start()
            sends.append(snd)
            recvs.append(pltpu.make_async_remote_copy(
                src_ref=in_ref, dst_ref=out_ref.at[4 * px + 2 * py + pc], **common))
        for r in recvs:
            r.wait_recv()
        for s in sends:
            s.wait_send()
        own.wait()

    return pl.pallas_call(
        body, name=name, out_shape=jax.ShapeDtypeStruct((NDEV,) + arr.shape, arr.dtype),
        in_specs=[HBM_SPEC], out_specs=HBM_SPEC,
        scratch_shapes=[pltpu.SemaphoreType.DMA((NDEV - 1,)), pltpu.SemaphoreType.DMA((NDEV - 1,)),
                        pltpu.SemaphoreType.DMA(())],
    )(arr)


def _start_by_layer(make, nl):
    for l in range(nl):
        make(l).start()


def _gather_shards(arrs, name):
    n = len(arrs)
    nl = arrs[0].shape[0]

    def body(*refs):
        ins, outs = refs[:n], refs[n:2 * n]
        send_sems, recv_sems = refs[2 * n:]
        x, y, c, chips = _mesh_place()
        dev = lambda px, py, pc: 4 * px + 2 * py + pc

        def copy(k, a, src, block, to, layer=slice(None)):
            return pltpu.make_async_remote_copy(
                src_ref=src.at[layer], dst_ref=outs[a].at[layer, block], send_sem=send_sems.at[k, a],
                recv_sem=recv_sems.at[k, a], device_id=to, device_id_type=MESH_IDS)

        me = dev(x, y, c)
        sent = []
        for a in range(n):
            _start_by_layer(lambda l: copy(0, a, ins[a], me, (x, y, 1 - c), l), nl)
            sent.append(copy(0, a, ins[a], me, (x, y, 1 - c)))
        for j, (px, py) in enumerate(chips):
            for a in range(n):
                _start_by_layer(lambda l: copy(1 + j, a, ins[a], me, (px, py, c), l), nl)
                sent.append(copy(1 + j, a, ins[a], me, (px, py, c)))
        for j, (px, py) in enumerate(chips):
            for a in range(n):
                blk = dev(px, py, c)
                copy(1 + j, a, ins[a], blk, (x, y, c)).wait_recv()
                got = outs[a].at[:, blk]
                _start_by_layer(lambda l: copy(4 + j, a, got, blk, (x, y, 1 - c), l), nl)
                sent.append(copy(4 + j, a, got, blk, (x, y, 1 - c)))
        for a in range(n):
            copy(0, a, ins[a], dev(x, y, 1 - c), (x, y, c)).wait_recv()
        for j, (px, py) in enumerate(chips):
            for a in range(n):
                copy(4 + j, a, ins[a], dev(px, py, 1 - c), (x, y, c)).wait_recv()
        for s in sent:
            s.wait_send()

    out_shape = [jax.ShapeDtypeStruct((a.shape[0], NDEV) + a.shape[1:], a.dtype) for a in arrs]
    outs = pl.pallas_call(
        body, name=name, out_shape=out_shape, in_specs=[HBM_SPEC] * n, out_specs=[HBM_SPEC] * n,
        scratch_shapes=[pltpu.SemaphoreType.DMA((7, n)), pltpu.SemaphoreType.DMA((7, n))],
    )(*arrs)
    me = _my_index()
    return [lax.dynamic_update_slice_in_dim(o, a[:, None], me, axis=1) for o, a in zip(outs, arrs)]


def _sibling_exchange(arrs, name):
    n = len(arrs)
    nl, nchips = arrs[0].shape[:2]

    def body(*refs):
        ins, got = refs[:n], refs[n:2 * n]
        send_sems, recv_sems = refs[2 * n:]
        x, y, c, _ = _mesh_place()

        def send(a, idx):
            return pltpu.make_async_remote_copy(
                src_ref=ins[a].at[idx], dst_ref=got[a].at[idx], send_sem=send_sems.at[a],
                recv_sem=recv_sems.at[a], device_id=(x, y, 1 - c), device_id_type=MESH_IDS)

        everything = (slice(None), slice(None))
        for a in range(n):
            for q in range(nchips):
                _start_by_layer(lambda l: send(a, (l, q)), nl)
        for a in range(n):
            send(a, everything).wait_recv()
        for a in range(n):
            send(a, everything).wait_send()

    shapes = [jax.ShapeDtypeStruct(a.shape, a.dtype) for a in arrs]
    return list(pl.pallas_call(
        body, name=name, out_shape=shapes, in_specs=[HBM_SPEC] * n, out_specs=[HBM_SPEC] * n,
        scratch_shapes=[pltpu.SemaphoreType.DMA((n,)), pltpu.SemaphoreType.DMA((n,))],
    )(*arrs))


def _chip_exchange(arrs, name):
    n = len(arrs)
    nl = arrs[0].shape[0]

    def body(*refs):
        ins, outs = refs[:n], refs[n:2 * n]
        send_sems, recv_sems = refs[2 * n:]
        x, y, c, chips = _mesh_place()
        mine = 2 * x + y

        def copy(j, a, target, source, layer=slice(None)):
            px, py = chips[j]
            return pltpu.make_async_remote_copy(
                src_ref=ins[a].at[layer, target], dst_ref=outs[a].at[layer, source], send_sem=send_sems.at[j, a],
                recv_sem=recv_sems.at[j, a], device_id=(px, py, c), device_id_type=MESH_IDS)

        for j, (px, py) in enumerate(chips):
            for a in range(n):
                _start_by_layer(lambda l: copy(j, a, 2 * px + py, mine, l), nl)
        for j, (px, py) in enumerate(chips):
            for a in range(n):
                copy(j, a, mine, 2 * px + py).wait_recv()
        for j, (px, py) in enumerate(chips):
            for a in range(n):
                copy(j, a, 2 * px + py, mine).wait_send()

    out_shape = [jax.ShapeDtypeStruct(a.shape, a.dtype) for a in arrs]
    outs = pl.pallas_call(
        body, name=name, out_shape=out_shape, in_specs=[HBM_SPEC] * n, out_specs=[HBM_SPEC] * n,
        scratch_shapes=[pltpu.SemaphoreType.DMA((3, n)), pltpu.SemaphoreType.DMA((3, n))],
    )(*arrs)
    mine = 2 * lax.axis_index("x") + lax.axis_index("y")
    return [lax.dynamic_update_slice_in_dim(o, lax.dynamic_slice_in_dim(a, mine, 1, axis=1), mine, axis=1)
            for o, a in zip(outs, arrs)]


def _pair_add(a, b):
    NL, NC, R, C = a.shape
    tr = R
    for cand in (256, 128, 64, 32, 16):
        if R % cand == 0 and R > cand:
            tr = cand
            break

    def body(a_ref, b_ref, o_ref):
        o_ref[...] = (a_ref[...].astype(F32) + b_ref[...].astype(F32)).astype(o_ref.dtype)

    spec = pl.BlockSpec((None, NC, tr, C), lambda l, i: (l, 0, i, 0))
    return _pcall(body, name="pair_add", grid=(NL, R // tr), in_specs=[spec, spec], out_specs=spec,
                  out_shape=jax.ShapeDtypeStruct(a.shape, a.dtype))(a, b)


def _matmul(name, a, b, *, dims, grid, a_spec, b_spec, out_shape, out_specs, kaxis=None, acc_shape=None,
            extra=(), extra_specs=(), epilogue=None):
    nk = grid[kaxis] if kaxis is not None else 1
    ne = len(extra)
    multi = isinstance(out_shape, (list, tuple))
    n_out = len(out_shape) if multi else 1

    def body(*refs):
        a_ref, b_ref = refs[0], refs[1]
        ex = refs[2:2 + ne]
        outs = refs[2 + ne:2 + ne + n_out]

        def write(res):
            vals = epilogue(res, *[e[...] for e in ex]) if epilogue is not None else (res,)
            for o, v in zip(outs, vals):
                o[...] = v.astype(o.dtype)

        part = _dot(a_ref[...].astype(BF16), b_ref[...].astype(BF16), dims)
        if nk == 1:
            write(part)
        else:
            acc = refs[-1]
            k = pl.program_id(kaxis)

            @pl.when(k == 0)
            def _():
                acc[...] = part

            @pl.when(k > 0)
            def _():
                acc[...] += part

            @pl.when(k == nk - 1)
            def _():
                write(acc[...])

    scratch = [pltpu.VMEM(acc_shape, F32)] if nk > 1 else []
    return _pcall(body, name=name, grid=grid, in_specs=[a_spec, b_spec, *extra_specs],
                  out_specs=out_specs, out_shape=out_shape, scratch=scratch)(a, b, *extra)


def _rms(x, g):
    inv = lax.rsqrt(jnp.mean(x * x, axis=-1, keepdims=True) + EPS)
    return x * inv * g


def _lnmod_math(x, g, sh, sc):
    return _rms(x, g) * (1.0 + sc) + sh


def _head_sums(x):
    row = lax.broadcasted_iota(jnp.int32, (LANES, LANES), 0) // HEAD_DIM
    col = lax.broadcasted_iota(jnp.int32, (LANES, LANES), 1) // HEAD_DIM
    same_head = jnp.where(row == col, 1.0, 0.0).astype(BF16)
    hi = x.astype(BF16)
    lo = (x - hi.astype(F32)).astype(BF16)
    return _dot(hi, same_head) + _dot(lo, same_head)


def _qkn_inv(p):
    return lax.rsqrt(_head_sums(p * p) / HEAD_DIM + EPS)


def _glu_math(val, gate):
    return jax.nn.gelu(gate) * val


def _accumulate(ref, val, first):
    @pl.when(first)
    def _():
        ref[...] = val

    @pl.when(jnp.logical_not(first))
    def _():
        ref[...] += val


def _lnmod_fwd(x, g, sh, sc):
    L, D = x.shape
    tm = _row_tile(L)

    def body(x_ref, g_ref, sh_ref, sc_ref, h_ref):
        h_ref[...] = _lnmod_math(x_ref[...], g_ref[...], sh_ref[...], sc_ref[...]).astype(BF16)

    row = pl.BlockSpec((tm, D), lambda i: (i, 0))
    vec = pl.BlockSpec((1, D), lambda i: (0, 0))
    return _pcall(body, name="lnmod_fwd", grid=(L // tm,), in_specs=[row, vec, vec, vec], out_specs=row,
                  out_shape=jax.ShapeDtypeStruct((L, D), BF16))(x, g, sh, sc)


def _lnmod_bwd(dh, x, g, sh, sc, dres):
    L, D = x.shape
    tm = _row_tile(L)

    def body(dh_ref, x_ref, g_ref, sh_ref, sc_ref, res_ref, dx_ref, dg_ref, dsh_ref, dsc_ref):
        _, vjp = jax.vjp(_lnmod_math, x_ref[...], g_ref[...], sh_ref[...], sc_ref[...])
        dx, dg, dsh, dsc = vjp(dh_ref[...])
        dx_ref[...] = dx + res_ref[...]
        first = pl.program_id(0) == 0
        _accumulate(dg_ref, dg, first)
        _accumulate(dsh_ref, dsh, first)
        _accumulate(dsc_ref, dsc, first)

    row = pl.BlockSpec((tm, D), lambda i: (i, 0))
    vec = pl.BlockSpec((1, D), lambda i: (0, 0))
    vs = jax.ShapeDtypeStruct((1, D), F32)
    return _pcall(body, name="lnmod_bwd", grid=(L // tm,), in_specs=[row, row, vec, vec, vec, row],
                  out_specs=[row, vec, vec, vec],
                  out_shape=[jax.ShapeDtypeStruct((L, D), F32), vs, vs, vs])(dh, x, g, sh, sc, dres)


def _gate_bwd(dx, a, gate):
    L, D = dx.shape
    tm = _row_tile(L)

    def body(dx_ref, a_ref, g_ref, da_ref, dg_ref):
        dxv = dx_ref[...]
        da_ref[...] = (g_ref[...] * dxv).astype(BF16)
        _accumulate(dg_ref, jnp.sum(dxv * a_ref[...], axis=0, keepdims=True), pl.program_id(0) == 0)

    row = pl.BlockSpec((tm, D), lambda i: (i, 0))
    vec = pl.BlockSpec((1, D), lambda i: (0, 0))
    return _pcall(body, name="gate_bwd", grid=(L // tm,), in_specs=[row, row, vec], out_specs=[row, vec],
                  out_shape=[jax.ShapeDtypeStruct((L, D), BF16), jax.ShapeDtypeStruct((1, D), F32)])(dx, a, gate)


def _qknorm_fwd(p, gqk, AW):
    L = p.shape[0]
    tm = _row_tile(L)
    ncol = 2 * AW // LANES

    def body(p_ref, g_ref, o_ref):
        p = p_ref[...]
        o_ref[...] = (p * _qkn_inv(p) * g_ref[...]).astype(BF16)

    blk = pl.BlockSpec((tm, LANES), lambda i, j: (i, j))
    vec = pl.BlockSpec((1, LANES), lambda i, j: (0, j))
    return _pcall(body, name="qknorm_fwd", grid=(L // tm, ncol), in_specs=[blk, vec], out_specs=blk,
                  out_shape=jax.ShapeDtypeStruct((L, 2 * AW), BF16))(p, gqk)


def _qknorm_bwd(dqk, p, gqk, AW):
    L = p.shape[0]
    tm = _row_tile(L)
    ncol = 2 * AW // LANES

    def body(d_ref, p_ref, g_ref, dp_ref, dg_ref):
        p, dy = p_ref[...], d_ref[...]
        inv = _qkn_inv(p)
        gdy = g_ref[...] * dy
        dp_ref[...] = (inv * gdy - p * (inv * inv * inv) * (_head_sums(p * gdy) / HEAD_DIM)).astype(BF16)
        _accumulate(dg_ref, jnp.sum(dy * p * inv, axis=0, keepdims=True), pl.program_id(1) == 0)

    blk = pl.BlockSpec((tm, LANES), lambda j, i: (i, j))
    vec = pl.BlockSpec((1, LANES), lambda j, i: (0, j))
    return _pcall(body, name="qknorm_bwd", grid=(ncol, L // tm), in_specs=[blk, blk, vec], out_specs=[blk, vec],
                  out_shape=[jax.ShapeDtypeStruct((L, 2 * AW), BF16),
                             jax.ShapeDtypeStruct((1, 2 * AW), F32)])(dqk, p, gqk)


def _outnorm_fwd(oa, os_, ga, gs):
    L, AW = oa.shape
    SW = os_.shape[1]
    tm = _row_tile(L)

    def body(oa_ref, os_ref, ga_ref, gs_ref, o_ref):
        o_ref[:, :AW] = _rms(oa_ref[...], ga_ref[...]).astype(BF16)
        o_ref[:, AW:] = _rms(os_ref[...], gs_ref[...]).astype(BF16)

    ra = pl.BlockSpec((tm, AW), lambda i: (i, 0))
    rs = pl.BlockSpec((tm, SW), lambda i: (i, 0))
    va = pl.BlockSpec((1, AW), lambda i: (0, 0))
    vs = pl.BlockSpec((1, SW), lambda i: (0, 0))
    ro = pl.BlockSpec((tm, AW + SW), lambda i: (i, 0))
    return _pcall(body, name="outnorm_fwd", grid=(L // tm,), in_specs=[ra, rs, va, vs], out_specs=ro,
                  out_shape=jax.ShapeDtypeStruct((L, AW + SW), BF16))(oa, os_, ga, gs)


def _outnorm_bwd(do, oa, os_, ga, gs):
    L, AW = oa.shape
    SW = os_.shape[1]
    tm = _row_tile(L)

    def body(do_ref, oa_ref, os_ref, ga_ref, gs_ref, doa_ref, dos_ref, dga_ref, dgs_ref):
        first = pl.program_id(0) == 0
        _, vjp_a = jax.vjp(_rms, oa_ref[...], ga_ref[...])
        doa, dga = vjp_a(do_ref[:, :AW])
        _, vjp_s = jax.vjp(_rms, os_ref[...], gs_ref[...])
        dos, dgs = vjp_s(do_ref[:, AW:])
        doa_ref[...] = doa
        dos_ref[...] = dos
        _accumulate(dga_ref, dga, first)
        _accumulate(dgs_ref, dgs, first)

    ra = pl.BlockSpec((tm, AW), lambda i: (i, 0))
    rs = pl.BlockSpec((tm, SW), lambda i: (i, 0))
    va = pl.BlockSpec((1, AW), lambda i: (0, 0))
    vs = pl.BlockSpec((1, SW), lambda i: (0, 0))
    ro = pl.BlockSpec((tm, AW + SW), lambda i: (i, 0))
    return _pcall(body, name="outnorm_bwd", grid=(L // tm,), in_specs=[ro, ra, rs, va, vs],
                  out_specs=[ra, rs, va, vs],
                  out_shape=[jax.ShapeDtypeStruct((L, AW), F32), jax.ShapeDtypeStruct((L, SW), F32),
                             jax.ShapeDtypeStruct((1, AW), F32), jax.ShapeDtypeStruct((1, SW), F32)])(
                                 do, oa, os_, ga, gs)


def _softplus_neg_abs(z):
    return jnp.log(1.0 + jnp.exp(-jnp.abs(z)))


def _split_dot(x, m):
    hi = x.astype(BF16)
    lo = (x - hi.astype(F32)).astype(BF16)
    return _dot(hi, m) + _dot(lo, m)


def _attn_masks(B):
    row = lax.broadcasted_iota(jnp.int32, (B, B), 0)
    col = lax.broadcasted_iota(jnp.int32, (B, B), 1)
    strict = col < row
    upper = jnp.where(row > col, 1.0, 0.0).astype(BF16)
    lower = jnp.where(row < col, 1.0, 0.0).astype(BF16)
    return strict, upper, lower


def _attn_fwd(qk, p, AW, B):
    L = qk.shape[0]
    HP = AW // LANES
    nb = L // B

    def body(q_ref, k_ref, v_ref, o_ref, tot_ref, *scr):
        lb_scr = (scr[0:2], scr[2:4])
        tail_scr = (scr[4:6], scr[6:8])
        sum_scr = scr[8:10]
        i = pl.program_id(1)
        m0 = lax.broadcasted_iota(jnp.int32, (1, LANES), 1) < HEAD_DIM
        strict, upper, _ = _attn_masks(B)
        q = q_ref[...] * 0.125
        zq = jnp.zeros_like(q)
        qh = (jnp.where(m0, q, zq), jnp.where(m0, zq, q))

        def keys(j):
            start = pl.multiple_of(jnp.maximum(j, 0) * B, B)
            return k_ref[pl.ds(start, B), :]

        def vals(j):
            start = pl.multiple_of(jnp.maximum(j, 0) * B, B)
            return v_ref[pl.ds(start, B), :].astype(BF16)

        strips = [slice(s, min(s + ATTN_STRIP, B)) for s in range(0, B, ATTN_STRIP)]

        def scores(j):
            kj = keys(j)
            return tuple(_dot(qh[h], kj, NT) for h in (0, 1))

        def logits(zs2, slot, diag):
            for h in (0, 1):
                z = zs2[h]
                his, los = [], []
                for rows in strips:
                    zs = z[rows]
                    lb = jnp.minimum(zs, 0.0) - _softplus_neg_abs(zs)
                    l1 = lb - zs
                    if diag:
                        l1 = jnp.where(strict[rows], l1, 0.0)
                    lb_scr[slot][h][rows, :] = lb
                    hi = l1.astype(BF16)
                    his.append(hi)
                    los.append((l1 - hi.astype(F32)).astype(BF16))
                    rsum = jnp.sum(l1, axis=-1, keepdims=True)
                    if h == 0:
                        sum_scr[slot][rows, :] = jnp.broadcast_to(rsum, (rows.stop - rows.start, LANES))
                    else:
                        sum_scr[slot][rows, :] = jnp.where(m0, sum_scr[slot][rows, :], rsum)
                cat = lambda xs: jnp.concatenate(xs, axis=0)
                tail_scr[slot][h][...] = _dot(cat(his), upper) + _dot(cat(los), upper)

        def attend(j, slot, diag):
            vj = vals(j)
            pv = []
            for h in (0, 1):
                ws = []
                for rows in strips:
                    w = jnp.exp(lb_scr[slot][h][rows, :] + tail_scr[slot][h][rows, :])
                    if diag:
                        w = jnp.where(strict[rows], w, 0.0)
                    ws.append(w.astype(BF16))
                pv.append(_dot(jnp.concatenate(ws, axis=0), vj))
            return jnp.where(m0, pv[0], pv[1])

        logits(scores(i), 0, True)
        o_ref[...] = attend(i, 0, True)
        tot_ref[...] = sum_scr[0][...]

        def half(j, slot):
            z = scores(j - 1)
            pv = attend(j, slot, False)
            logits(z, 1 - slot, False)
            o_ref[...] += pv * jnp.exp(tot_ref[...])
            tot_ref[...] += sum_scr[slot][...]

        @pl.when(i > 0)
        def _():
            logits(scores(i - 1), 1, False)

            @pl.loop(0, (i + 1) // 2)
            def _(t):
                j = i - 1 - 2 * t
                half(j, 1)

                @pl.when(j > 0)
                def _():
                    half(j - 1, 0)


    qspec = pl.BlockSpec((B, LANES), lambda hp, i: (i, hp))
    kspec = pl.BlockSpec((L, LANES), lambda hp, i: (0, HP + hp))
    vspec = pl.BlockSpec((L, LANES), lambda hp, i: (0, 2 * HP + hp))
    ospec = pl.BlockSpec((B, LANES), lambda hp, i: (i, hp))
    shp = jax.ShapeDtypeStruct((L, AW), F32)
    return _pcall(body, name="attn_fwd", grid=(HP, nb), in_specs=[qspec, kspec, vspec],
                  out_specs=[ospec, ospec], out_shape=[shp, shp],
                  scratch=[pltpu.VMEM((B, B), F32)] * 8 + [pltpu.VMEM((B, LANES), F32)] * 2)(qk, qk, p)


def _attn_bwd(qk, p, do, tot, AW, B):
    L = qk.shape[0]
    HP = AW // LANES
    nb = L // B

    def body(q_ref, k_ref, v_ref, do_ref, tot_ref, dq_ref, dk_ref, dv_ref, *scr):
        lb_scr, tail_scr, dw_scr, e_scr, beta_scr = scr[0:2], scr[2:4], scr[4:6], scr[6:8], scr[8:10]
        dos_scr, bsum_scr, left_scr, esum_scr, ecum_scr = scr[10:15]
        i = pl.program_id(1)
        m0 = lax.broadcasted_iota(jnp.int32, (1, LANES), 1) < HEAD_DIM
        strict, upper, lower = _attn_masks(B)
        strips = [slice(s, min(s + ATTN_STRIP, B)) for s in range(0, B, ATTN_STRIP)]
        cat = lambda xs: jnp.concatenate(xs, axis=0)

        @pl.when(i == 0)
        def _():
            dk_ref[...] = jnp.zeros_like(dk_ref)
            dv_ref[...] = jnp.zeros_like(dv_ref)

        q = q_ref[...] * 0.125
        zq = jnp.zeros_like(q)
        qh = (jnp.where(m0, q, zq), jnp.where(m0, zq, q))
        heads = lambda a: (jnp.where(m0, a, jnp.zeros_like(a)), jnp.where(m0, jnp.zeros_like(a), a))
        left_scr[...] = jnp.zeros_like(left_scr)
        ecum_scr[...] = jnp.zeros_like(ecum_scr)
        dq_ref[...] = jnp.zeros_like(dq_ref)

        def block_rows(j):
            return pl.ds(pl.multiple_of(j * B, B), B)

        def put_row_sums(ref, rows, h, rsum):
            if h == 0:
                ref[rows, :] = jnp.broadcast_to(rsum, (rows.stop - rows.start, LANES))
            else:
                ref[rows, :] = jnp.where(m0, ref[rows, :], rsum)

        def head_cols(x, h):
            other = pltpu.roll(x, HEAD_DIM, 1)
            full = jnp.where(m0, x, other) if h == 0 else jnp.where(m0, other, x)
            return jnp.concatenate([full] * (B // LANES), axis=1) if B > LANES else full

        def scores(j):
            kj = k_ref[block_rows(j), :]
            return tuple(_dot(qh[h], kj, NT) for h in (0, 1))

        def stage_a(j, zs2, diag):
            for h in (0, 1):
                his, los = [], []
                for rows in strips:
                    zs = zs2[h][rows]
                    lb = jnp.minimum(zs, 0.0) - _softplus_neg_abs(zs)
                    l1 = lb - zs
                    if diag:
                        l1 = jnp.where(strict[rows], l1, 0.0)
                    lb_scr[h][rows, :] = lb
                    hi = l1.astype(BF16)
                    his.append(hi)
                    los.append((l1 - hi.astype(F32)).astype(BF16))
                    put_row_sums(bsum_scr, rows, h, jnp.sum(l1, axis=-1, keepdims=True))
                tail_scr[h][...] = _dot(cat(his), upper) + _dot(cat(los), upper)
            bs = bsum_scr[...]
            scale = jnp.exp(tot_ref[...] - left_scr[...] - bs)
            left_scr[...] += bs
            dos = (do_ref[...] * scale).astype(BF16)
            dos_scr[...] = dos
            vj = v_ref[block_rows(j), :].astype(BF16)
            dosh = heads(dos)
            for h in (0, 1):
                dw_scr[h][...] = _dot(dosh[h], vj, NT)

        def stage_b(j, diag):
            dosh = heads(dos_scr[...])
            pres = []
            dv_blk = jnp.zeros((B, LANES), F32)
            for h in (0, 1):
                ehs, els, wbs = [], [], []
                for rows in strips:
                    lb = lb_scr[h][rows, :]
                    w = jnp.exp(lb + tail_scr[h][rows, :])
                    if diag:
                        w = jnp.where(strict[rows], w, 0.0)
                    beta_scr[h][rows, :] = jnp.exp(lb)
                    e = dw_scr[h][rows, :] * w
                    e_scr[h][rows, :] = e
                    ehi = e.astype(BF16)
                    ehs.append(ehi)
                    els.append((e - ehi.astype(F32)).astype(BF16))
                    wbs.append(w.astype(BF16))
                    put_row_sums(esum_scr, rows, h, jnp.sum(e, axis=-1, keepdims=True))
                pres.append(_dot(cat(ehs), lower) + _dot(cat(els), lower))
                dv_blk = dv_blk + _dot(cat(wbs), dosh[h], TN)
            dv_ref[block_rows(j), :] += dv_blk
            return pres

        def stage_c(j, pres, diag):
            kh = heads(k_ref[block_rows(j), :])
            dq = jnp.zeros((B, LANES), F32)
            dk_blk = jnp.zeros((B, LANES), F32)
            for h in (0, 1):
                dzs = []
                for rows in strips:
                    e = e_scr[h][rows, :]
                    dl1 = pres[h][rows] + head_cols(ecum_scr[rows, :], h)
                    dz = e - beta_scr[h][rows, :] * (e + dl1)
                    if diag:
                        dz = jnp.where(strict[rows], dz, 0.0)
                    dzs.append(dz.astype(BF16))
                dzb = cat(dzs)
                dq = dq + _dot(dzb, kh[h])
                dk_blk = dk_blk + _dot(dzb, qh[h], TN)
            dq_ref[...] += dq
            dk_ref[block_rows(j), :] += dk_blk
            ecum_scr[...] += esum_scr[...]

        def step(t, diag_next):
            z = scores(t + 1)
            pres = stage_b(t, False)
            stage_a(t + 1, z, diag_next)
            stage_c(t, pres, False)

        @pl.when(i > 0)
        def _():
            stage_a(0, scores(0), False)

            @pl.loop(0, i - 1)
            def _(t):
                step(t, False)

            step(i - 1, True)

        @pl.when(i == 0)
        def _():
            stage_a(0, scores(0), True)

        stage_c(i, stage_b(i, True), True)
        dq_ref[...] = dq_ref[...] * 0.125

    qspec = pl.BlockSpec((B, LANES), lambda hp, i: (i, hp))
    kspec = pl.BlockSpec((L, LANES), lambda hp, i: (0, HP + hp))
    vspec = pl.BlockSpec((L, LANES), lambda hp, i: (0, 2 * HP + hp))
    full = pl.BlockSpec((L, LANES), lambda hp, i: (0, hp))
    shp = jax.ShapeDtypeStruct((L, AW), F32)
    return _pcall(body, name="attn_bwd", grid=(HP, nb), in_specs=[qspec, kspec, vspec, qspec, qspec],
                  out_specs=[qspec, full, full], out_shape=[shp, shp, shp],
                  scratch=[pltpu.VMEM((B, B), F32)] * 10 + [pltpu.VMEM((B, LANES), BF16)]
                  + [pltpu.VMEM((B, LANES), F32)] * 4)(qk, qk, p, do, tot)


def _s5_disc(ar, ai, ldt):
    dt = jnp.exp(ldt)
    mag = jnp.exp(dt * ar)
    abr = mag * jnp.cos(dt * ai)
    abi = mag * jnp.sin(dt * ai)
    emr = abr - 1.0
    emi = abi
    den = ar * ar + ai * ai
    fr = (emr * ar + emi * ai) / den
    fi = (emi * ar - emr * ai) / den
    return abr, abi, fr, fi


def _s5_params_math(ar, ai, ldt, br, bi):
    abr, abi, fr, fi = _s5_disc(ar, ai, ldt)
    return abr, abi, fr * br - fi * bi, fr * bi + fi * br


def _cmul_add(xr, xi, kr, ki, sr, si):
    return xr + (kr * sr - ki * si), xi + (kr * si + ki * sr)


def _s5_tab_rows(T):
    ng = T // 8
    return dict(NG=ng, POW=0, PA=ng, PAR=ng + 8, LA=ng + 16, ROWS=ng + 24)


def _s5_prep(ar, ai, ldt, braw_r, braw_i, T):
    NL, _, NS = ar.shape
    GT = NS // STATE_TILE
    R = _s5_tab_rows(T)

    def body(ar_ref, ai_ref, ldt_ref, br_ref, bi_ref, abr_ref, abi_ref, tabr_ref, tabi_ref, bsr_ref, bsi_ref):
        for t in range(GT):
            sl = slice(t * STATE_TILE, (t + 1) * STATE_TILE)
            abr, abi, bsr, bsi = _s5_params_math(ar_ref[:, sl], ai_ref[:, sl], ldt_ref[:, sl],
                                                 br_ref[t], bi_ref[t])
            abr_ref[:, sl] = abr
            abi_ref[:, sl] = abi
            bsr_ref[t] = bsr.astype(BF16)
            bsi_ref[t] = bsi.astype(BF16)

            def put(row, vr, vi):
                tabr_ref[row:row + 1, sl] = vr
                tabi_ref[row:row + 1, sl] = vi

            pr, pi = abr, abi
            for g in range(R["NG"]):
                put(R["POW"] + g, pr, pi)
                if g + 1 < R["NG"]:
                    pr, pi = pr * abr - pi * abi, pr * abi + pi * abr
            big_r, big_i = pr, pi
            qr, qi = big_r, big_i
            for r in range(8):
                put(R["PA"] + r, qr, qi)
                put(R["PAR"] + 7 - r, qr, qi)
                qr, qi = qr * big_r - qi * big_i, qr * big_i + qi * big_r
            qr, qi = big_r, big_i
            for k in range(3):
                put(R["LA"] + k, qr, qi)
                qr, qi = qr * qr - qi * qi, 2.0 * qr * qi
            for k in range(3, 8):
                put(R["LA"] + k, jnp.zeros_like(qr), jnp.zeros_like(qi))

    rowspec = pl.BlockSpec((None, 1, NS), lambda l: (l, 0, 0))
    bspec = pl.BlockSpec((None, GT, LANES, STATE_TILE), lambda l: (l, 0, 0, 0))
    tabspec = pl.BlockSpec((None, R["ROWS"], NS), lambda l: (l, 0, 0))
    rs = jax.ShapeDtypeStruct((NL, 1, NS), F32)
    ts = jax.ShapeDtypeStruct((NL, R["ROWS"], NS), F32)
    bs = jax.ShapeDtypeStruct((NL, GT, LANES, STATE_TILE), BF16)
    return _pcall(body, name="s5_prep", grid=(NL,), in_specs=[rowspec] * 3 + [bspec] * 2,
                  out_specs=[rowspec, rowspec, tabspec, tabspec, bspec, bspec],
                  out_shape=[rs, rs, ts, ts, bs, bs])(ar, ai, ldt, braw_r, braw_i)


def _s5_prep_bwd(ar, ai, ldt, braw_r, braw_i, dabr, dabi, dbsr, dbsi):
    NL, _, NS = ar.shape
    GT = NS // STATE_TILE

    def body(ar_ref, ai_ref, ldt_ref, br_ref, bi_ref, dabr_ref, dabi_ref, dbsr_ref, dbsi_ref,
             dar_ref, dai_ref, dldt_ref, dbr_ref, dbi_ref):
        for t in range(GT):
            sl = slice(t * STATE_TILE, (t + 1) * STATE_TILE)
            _, vjp = jax.vjp(_s5_params_math, ar_ref[:, sl], ai_ref[:, sl], ldt_ref[:, sl],
                             br_ref[t], bi_ref[t])
            dabr_row = jnp.sum(dabr_ref[:, sl], axis=0, keepdims=True)
            dabi_row = jnp.sum(dabi_ref[:, sl], axis=0, keepdims=True)
            dar, dai, dldt, dbr, dbi = vjp((dabr_row, dabi_row, dbsr_ref[t], dbsi_ref[t]))
            dar_ref[:, sl] = dar
            dai_ref[:, sl] = dai
            dldt_ref[:, sl] = dldt
            dbr_ref[t] = dbr
            dbi_ref[t] = dbi

    rowspec = pl.BlockSpec((None, 1, NS), lambda l: (l, 0, 0))
    row8spec = pl.BlockSpec((None, 8, NS), lambda l: (l, 0, 0))
    bspec = pl.BlockSpec((None, GT, LANES, STATE_TILE), lambda l: (l, 0, 0, 0))
    rs = jax.ShapeDtypeStruct((NL, 1, NS), F32)
    bs = jax.ShapeDtypeStruct((NL, GT, LANES, STATE_TILE), F32)
    return _pcall(body, name="s5_prep_bwd", grid=(NL,), in_specs=[rowspec] * 3 + [bspec] * 2 + [row8spec] * 2 + [bspec] * 2,
                  out_specs=[rowspec] * 3 + [bspec] * 2, out_shape=[rs, rs, rs, bs, bs])(
                      ar, ai, ldt, braw_r, braw_i, dabr, dabi, dbsr, dbsi)


def _to_streams(x, T):
    L, C = x.shape
    return x.reshape(L // T, 8, T // 8, C).transpose(0, 2, 1, 3).reshape(L, C)


def _from_streams(x, T):
    L, C = x.shape
    return x.reshape(L // T, T // 8, 8, C).transpose(0, 2, 1, 3).reshape(L, C)


def _s5_specs(L, SW, T, rev):
    GT = SW // LANES
    nc = L // T
    cidx = (lambda c: nc - 1 - c) if rev else (lambda c: c)
    rows = _s5_tab_rows(T)["ROWS"]
    return dict(
        GT=GT, nc=nc, cidx=cidx,
        chan=pl.BlockSpec((T, LANES), lambda j, c: (cidx(c), j)),
        state=pl.BlockSpec((T, STATE_TILE), lambda j, c: (cidx(c), j)),
        bmat=pl.BlockSpec((None, LANES, STATE_TILE), lambda j, c: (j, 0, 0)),
        cmat=pl.BlockSpec((None, STATE_TILE, LANES), lambda j, c: (j, 0, 0)),
        gmat=pl.BlockSpec((None, LANES, LANES), lambda j, c: (j, 0, 0)),
        cvec=pl.BlockSpec((1, LANES), lambda j, c: (0, j)),
        svec8=pl.BlockSpec((8, STATE_TILE), lambda j, c: (0, j)),
        tab=pl.BlockSpec((rows, STATE_TILE), lambda j, c: (0, j)),
    )


def _s5_fwd(u, sp, T):
    L, SW = u.shape
    S = _s5_specs(L, SW, T, False)
    NS = S["GT"] * STATE_TILE
    R = _s5_tab_rows(T)
    NG = R["NG"]

    def body(u_ref, bsr_ref, bsi_ref, tabr_ref, tabi_ref, cr_ref, ci_ref, d_ref, wg_ref, gb_ref,
             o_ref, sr_ref, si_ref, y_ref, carr_ref, cari_ref):
        @pl.when(pl.program_id(1) == 0)
        def _():
            carr_ref[...] = jnp.zeros_like(carr_ref)
            cari_ref[...] = jnp.zeros_like(cari_ref)

        uv = u_ref[...]
        ub = uv.astype(BF16)
        row8 = lax.broadcasted_iota(jnp.int32, (8, LANES), 0)
        nstrips = STATE_TILE // LANES

        def project(s):
            sl = slice(s * LANES, (s + 1) * LANES)
            return _dot(ub, bsr_ref[:, sl]), _dot(ub, bsi_ref[:, sl])

        ahead = project(0)
        for s in range(nstrips):
            sl = slice(s * LANES, (s + 1) * LANES)
            tab = lambda r0, n=1, sl=sl: (tabr_ref[r0:r0 + n, sl], tabi_ref[r0:r0 + n, sl])
            xr, xi = ahead
            if s + 1 < nstrips:
                ahead = project(s + 1)
            ar, ai = tab(R["POW"])
            lr, li = [xr[0:8]], [xi[0:8]]
            for g in range(1, NG):
                nr, ni = _cmul_add(xr[8 * g:8 * g + 8], xi[8 * g:8 * g + 8], ar, ai, lr[-1], li[-1])
                lr.append(nr)
                li.append(ni)
            yr, yi = lr[-1], li[-1]
            for k, dist in enumerate((1, 2, 4)):
                kr, ki = tab(R["LA"] + k)
                keep = row8 >= dist
                yr, yi = _cmul_add(yr, yi, kr, ki, jnp.where(keep, pltpu.roll(yr, dist, 0), 0.0),
                                   jnp.where(keep, pltpu.roll(yi, dist, 0), 0.0))
            c0r, c0i = carr_ref[7:8, sl], cari_ref[7:8, sl]
            par, pai = tab(R["PA"], 8)
            er, ei = _cmul_add(yr, yi, par, pai, c0r, c0i)
            carr_ref[:, sl] = er
            cari_ref[:, sl] = ei
            first = row8 == 0
            inr = jnp.where(first, c0r, pltpu.roll(er, 1, 0))
            ini = jnp.where(first, c0i, pltpu.roll(ei, 1, 0))
            for g in range(NG):
                pr, pi = tab(R["POW"] + g)
                outr, outi = _cmul_add(lr[g], li[g], pr, pi, inr, ini)
                sr_ref[8 * g:8 * g + 8, sl] = outr
                si_ref[8 * g:8 * g + 8, sl] = outi
        y = (_dot(sr_ref[...].astype(BF16), cr_ref[...]) - _dot(si_ref[...].astype(BF16), ci_ref[...])
             + d_ref[...] * uv)
        y_ref[...] = y
        yg = jax.nn.gelu(y)
        gate = jax.nn.sigmoid(_dot(yg.astype(BF16), wg_ref[...]) + gb_ref[...])
        o_ref[...] = yg * gate

    cs = jax.ShapeDtypeStruct((L, SW), F32)
    ss = jax.ShapeDtypeStruct((L, NS), F32)
    return _pcall(
        body, name="s5_fwd", grid=(S["GT"], S["nc"]),
        in_specs=[S["chan"], S["bmat"], S["bmat"], S["tab"], S["tab"], S["cmat"], S["cmat"],
                  S["cvec"], S["gmat"], S["cvec"]],
        out_specs=[S["chan"], S["state"], S["state"], S["chan"]], out_shape=[cs, ss, ss, cs],
        scratch=[pltpu.VMEM((8, STATE_TILE), F32), pltpu.VMEM((8, STATE_TILE), F32)],
    )(u, sp["bsr"], sp["bsi"], sp["tabr"], sp["tabi"], sp["crT"], sp["ciT"], sp["d"], sp["wg"], sp["gb"])


def _s5_bwd(do, u, s_re, s_im, ypre, sp, T):
    L, SW = u.shape
    S = _s5_specs(L, SW, T, True)
    GT, nc, cidx = S["GT"], S["nc"], S["cidx"]
    NS = GT * STATE_TILE
    R = _s5_tab_rows(T)
    NG = R["NG"]
    T8 = T // 8

    def body(do_ref, u_ref, sr_ref, si_ref, hr_ref, hi_ref, y_ref, bsr_ref, bsi_ref, tabr_ref, tabi_ref,
             cr_ref, ci_ref, d_ref, wg_ref, gb_ref,
             du_ref, dbsr_ref, dbsi_ref, dcr_ref, dci_ref, dd_ref, dwg_ref, dgb_ref, dar_ref, dai_ref,
             carr_ref, cari_ref, lam_r, lam_i):
        c = pl.program_id(1)
        first = c == 0

        @pl.when(first)
        def _():
            carr_ref[...] = jnp.zeros_like(carr_ref)
            cari_ref[...] = jnp.zeros_like(cari_ref)

        u = u_ref[...]
        ub = u.astype(BF16)
        y = y_ref[...]
        yg, gelu_vjp = jax.vjp(jax.nn.gelu, y)
        ygb = yg.astype(BF16)
        gate = jax.nn.sigmoid(_dot(ygb, wg_ref[...]) + gb_ref[...])
        dout = do_ref[...]
        dt = dout * yg * gate * (1.0 - gate)
        dtb = dt.astype(BF16)
        dyg = dout * gate + _dot(dtb, wg_ref[...], NT)
        (dy,) = gelu_vjp(dyg)
        dyb = dy.astype(BF16)
        _accumulate(dwg_ref, _dot(ygb, dtb, TN), first)
        _accumulate(dgb_ref, jnp.sum(dt, axis=0, keepdims=True), first)
        _accumulate(dd_ref, jnp.sum(dy * u, axis=0, keepdims=True), first)
        _accumulate(dcr_ref, _dot(sr_ref[...].astype(BF16), dyb, TN), first)
        _accumulate(dci_ref, -_dot(si_ref[...].astype(BF16), dyb, TN), first)
        earliest = cidx(c) == 0
        row8 = lax.broadcasted_iota(jnp.int32, (8, LANES), 0)
        nstrips = STATE_TILE // LANES

        def project(s):
            sl = slice(s * LANES, (s + 1) * LANES)
            return _dot(dyb, cr_ref[sl, :], NT), -_dot(dyb, ci_ref[sl, :], NT)

        ahead = project(0)
        for s in range(nstrips):
            sl = slice(s * LANES, (s + 1) * LANES)
            tab = lambda r0, n=1, sl=sl: (tabr_ref[r0:r0 + n, sl], -tabi_ref[r0:r0 + n, sl])
            xr, xi = ahead
            if s + 1 < nstrips:
                ahead = project(s + 1)
            ar, ai = tab(R["POW"])
            lr, li = [None] * NG, [None] * NG
            lr[NG - 1], li[NG - 1] = xr[8 * (NG - 1):8 * NG], xi[8 * (NG - 1):8 * NG]
            for g in range(NG - 2, -1, -1):
                lr[g], li[g] = _cmul_add(xr[8 * g:8 * g + 8], xi[8 * g:8 * g + 8], ar, ai, lr[g + 1], li[g + 1])
            yr, yi = lr[0], li[0]
            for k, dist in enumerate((1, 2, 4)):
                kr, ki = tab(R["LA"] + k)
                keep = row8 < 8 - dist
                yr, yi = _cmul_add(yr, yi, kr, ki, jnp.where(keep, pltpu.roll(yr, 8 - dist, 0), 0.0),
                                   jnp.where(keep, pltpu.roll(yi, 8 - dist, 0), 0.0))
            c0r, c0i = carr_ref[0:1, sl], cari_ref[0:1, sl]
            par, pai = tab(R["PAR"], 8)
            er, ei = _cmul_add(yr, yi, par, pai, c0r, c0i)
            carr_ref[:, sl] = er
            cari_ref[:, sl] = ei
            last = row8 == 7
            inr = jnp.where(last, c0r, pltpu.roll(er, 7, 0))
            ini = jnp.where(last, c0i, pltpu.roll(ei, 7, 0))
            hr0 = jnp.where(earliest, 0.0, hr_ref[7:8, sl])
            hi0 = jnp.where(earliest, 0.0, hi_ref[7:8, sl])
            endr, endi = sr_ref[8 * (NG - 1):8 * NG, sl], si_ref[8 * (NG - 1):8 * NG, sl]
            pvr = jnp.where(row8 == 0, hr0, pltpu.roll(endr, 1, 0))
            pvi = jnp.where(row8 == 0, hi0, pltpu.roll(endi, 1, 0))
            accr = jnp.zeros((8, LANES), F32)
            acci = jnp.zeros((8, LANES), F32)
            for g in range(NG):
                pr, pi = tab(R["POW"] + NG - 1 - g)
                outr, outi = _cmul_add(lr[g], li[g], pr, pi, inr, ini)
                lam_r[8 * g:8 * g + 8, sl] = outr
                lam_i[8 * g:8 * g + 8, sl] = outi
                accr = accr + (outr * pvr + outi * pvi)
                acci = acci + (outi * pvr - outr * pvi)
                pvr, pvi = sr_ref[8 * g:8 * g + 8, sl], si_ref[8 * g:8 * g + 8, sl]

            @pl.when(first)
            def _():
                dar_ref[:, sl] = accr
                dai_ref[:, sl] = acci

            @pl.when(jnp.logical_not(first))
            def _():
                dar_ref[:, sl] += accr
                dai_ref[:, sl] += acci

        lrb = lam_r[...].astype(BF16)
        lib = lam_i[...].astype(BF16)
        _accumulate(dbsr_ref, _dot(ub, lrb, TN), first)
        _accumulate(dbsi_ref, _dot(ub, lib, TN), first)
        du_ref[...] = (dy * d_ref[...] + _dot(lrb, bsr_ref[...], NT) + _dot(lib, bsi_ref[...], NT)).astype(BF16)

    halo = pl.BlockSpec((8, STATE_TILE), lambda j, c: (jnp.maximum(cidx(c) * T8 - 1, 0), j))
    f = lambda *s: jax.ShapeDtypeStruct(s, F32)
    return _pcall(
        body, name="s5_bwd", grid=(GT, nc),
        in_specs=[S["chan"], S["chan"], S["state"], S["state"], halo, halo, S["chan"], S["bmat"], S["bmat"],
                  S["tab"], S["tab"], S["cmat"], S["cmat"], S["cvec"], S["gmat"], S["cvec"]],
        out_specs=[S["chan"], S["bmat"], S["bmat"], S["cmat"], S["cmat"], S["cvec"], S["gmat"], S["cvec"],
                   S["svec8"], S["svec8"]],
        out_shape=[jax.ShapeDtypeStruct((L, SW), BF16), f(GT, LANES, STATE_TILE), f(GT, LANES, STATE_TILE), f(GT, STATE_TILE, LANES),
                   f(GT, STATE_TILE, LANES), f(1, SW), f(GT, LANES, LANES), f(1, SW), f(8, NS), f(8, NS)],
        scratch=[pltpu.VMEM((8, STATE_TILE), F32), pltpu.VMEM((8, STATE_TILE), F32),
                 pltpu.VMEM((T, STATE_TILE), F32), pltpu.VMEM((T, STATE_TILE), F32)],
    )(do, u, s_re, s_im, s_re, s_im, ypre, sp["bsr"], sp["bsi"], sp["tabr"], sp["tabi"],
      sp["crT"], sp["ciT"], sp["d"], sp["wg"], sp["gb"])


def _conv_taps(xc, h6, h7, row):
    x1 = jnp.where(row == 0, h7, pltpu.roll(xc, 1, 0))
    x2 = jnp.where(row == 0, h6, jnp.where(row == 1, h7, pltpu.roll(xc, 2, 0)))
    return x1, x2


def _conv_halves(up_ref, halo_ref, w_ref, b_ref, tm, FS, first):
    row = lax.broadcasted_iota(jnp.int32, (tm, FS), 0)
    outs, taps = [], []
    for s in (0, 1):
        xc = up_ref[s]
        h6 = jnp.where(first, 0.0, halo_ref[s, 6:7, :])
        h7 = jnp.where(first, 0.0, halo_ref[s, 7:8, :])
        x1, x2 = _conv_taps(xc, h6, h7, row)
        outs.append(b_ref[s] + x2 * w_ref[s, 0:1, :] + x1 * w_ref[s, 1:2, :] + xc * w_ref[s, 2:3, :])
        taps.append((x2, x1, xc))
    return outs, taps


def _convglu_specs(L, FS, tm, rev=False):
    t8 = tm // 8
    nt = L // tm
    tile = (lambda i: nt - 1 - i) if rev else (lambda i: i)
    return dict(
        up=pl.BlockSpec((2, None, tm, FS), lambda j, i: (0, j, tile(i), 0)),
        halo=pl.BlockSpec((2, None, 8, FS), lambda j, i: (0, j, jnp.maximum(tile(i) * t8 - 1, 0), 0)),
        w=pl.BlockSpec((2, None, 3, FS), lambda j, i: (0, j, 0, 0)),
        b=pl.BlockSpec((2, None, 1, FS), lambda j, i: (0, j, 0, 0)),
        act=pl.BlockSpec((None, tm, FS), lambda j, i: (j, tile(i), 0)),
    )


def _convglu_fwd(up, w, b):
    _, NSH, L, FS = up.shape
    tm = _row_tile(L)
    S = _convglu_specs(L, FS, tm)

    def body(up_ref, halo_ref, w_ref, b_ref, act_ref):
        (val, gate), _ = _conv_halves(up_ref, halo_ref, w_ref, b_ref, tm, FS, pl.program_id(1) == 0)
        act_ref[...] = _glu_math(val, gate).astype(BF16)

    return _pcall(body, name="convglu_fwd", grid=(NSH, L // tm), in_specs=[S["up"], S["halo"], S["w"], S["b"]],
                  out_specs=S["act"], out_shape=jax.ShapeDtypeStruct((NSH, L, FS), BF16))(up, up, w, b)


def _convglu_bwd(up, dact, w, b):
    _, NSH, L, FS = up.shape
    tm = _row_tile(L)
    nt = L // tm
    S = _convglu_specs(L, FS, tm, rev=True)

    def body(up_ref, halo_ref, w_ref, b_ref, dact_ref, dup_ref, dw_ref, db_ref, after_scr):
        step = pl.program_id(1)
        first = step == 0
        (val, gate), taps = _conv_halves(up_ref, halo_ref, w_ref, b_ref, tm, FS, step == nt - 1)
        _, vjp = jax.vjp(_glu_math, val, gate)
        dcs = vjp(dact_ref[...])
        row = lax.broadcasted_iota(jnp.int32, (tm, FS), 0)
        for s in (0, 1):
            dc = dcs[s]
            n0 = jnp.where(first, 0.0, after_scr[s, 0:1, :])
            n1 = jnp.where(first, 0.0, after_scr[s, 1:2, :])
            x1 = jnp.where(row == tm - 1, n0, pltpu.roll(dc, tm - 1, 0))
            x2 = jnp.where(row == tm - 1, n1, jnp.where(row == tm - 2, n0, pltpu.roll(dc, tm - 2, 0)))
            dup_ref[s] = (dc * w_ref[s, 2:3, :] + x1 * w_ref[s, 1:2, :] + x2 * w_ref[s, 0:1, :]).astype(BF16)
            after_scr[s] = dc[0:8]
            sums = [jnp.sum(dc * t, axis=0, keepdims=True) for t in taps[s]]
            dbs = jnp.sum(dc, axis=0, keepdims=True)

            @pl.when(first)
            def _():
                for t in range(3):
                    dw_ref[s, t:t + 1, :] = sums[t]
                db_ref[s] = dbs

            @pl.when(jnp.logical_not(first))
            def _():
                for t in range(3):
                    dw_ref[s, t:t + 1, :] += sums[t]
                db_ref[s] += dbs

    f = lambda *s: jax.ShapeDtypeStruct(s, F32)
    return _pcall(body, name="convglu_bwd", grid=(NSH, nt),
                  in_specs=[S["up"], S["halo"], S["w"], S["b"], S["act"]],
                  out_specs=[S["up"], S["w"], S["b"]],
                  out_shape=[jax.ShapeDtypeStruct((2, NSH, L, FS), BF16), f(2, NSH, 3, FS), f(2, NSH, 1, FS)],
                  scratch=[pltpu.VMEM((2, 8, FS), F32)])(up, up, w, b, dact)


def _loss_head(y, target):
    L, D = y.shape
    tm = _row_tile(L)

    def body(y_ref, t_ref, loss_ref, dy_ref):
        err = y_ref[...] - t_ref[...]
        dy_ref[...] = err / D
        part = 0.5 * jnp.sum(jnp.mean(err * err, axis=-1, keepdims=True), axis=0, keepdims=True)
        _accumulate(loss_ref, jnp.broadcast_to(part, (1, LANES)), pl.program_id(0) == 0)

    row = pl.BlockSpec((tm, D), lambda i: (i, 0))
    vec = pl.BlockSpec((1, LANES), lambda i: (0, 0))
    return _pcall(body, name="loss_head", grid=(L // tm,), in_specs=[row, row], out_specs=[vec, row],
                  out_shape=[jax.ShapeDtypeStruct((1, LANES), F32), jax.ShapeDtypeStruct((L, D), F32)])(y, target)


def _ada_fwd(c_all, ada_w, ada_b):
    NL, D, NC = ada_w.shape
    NB = c_all.shape[0]

    def body(c_ref, w_ref, b_ref, o_ref):
        cact = jax.nn.silu(c_ref[...])
        o_ref[...] = _dot(cact.astype(BF16), w_ref[...].astype(BF16)) + b_ref[...]

    return _pcall(body, name="ada_fwd", grid=(NL,),
                  in_specs=[pl.BlockSpec((NB, D), lambda l: (0, 0)), pl.BlockSpec((None, D, NC), lambda l: (l, 0, 0)),
                            pl.BlockSpec((None, 1, NC), lambda l: (l, 0, 0))],
                  out_specs=pl.BlockSpec((None, NB, NC), lambda l: (l, 0, 0)),
                  out_shape=jax.ShapeDtypeStruct((NL, NB, NC), F32))(c_all, ada_w, ada_b)


def _ada_bwd(c_all_t, dmod):
    D, NB = c_all_t.shape
    NL, _, NC = dmod.shape

    def body(c_ref, d_ref, o_ref):
        cact = jax.nn.silu(c_ref[...]).astype(BF16).astype(F32)
        o_ref[...] = _dot(cact, d_ref[...].astype(BF16).astype(F32))

    return _pcall(body, name="ada_bwd", grid=(NL,),
                  in_specs=[pl.BlockSpec((D, NB), lambda l: (0, 0)), pl.BlockSpec((None, NB, NC), lambda l: (l, 0, 0))],
                  out_specs=pl.BlockSpec((None, D, NC), lambda l: (l, 0, 0)),
                  out_shape=jax.ShapeDtypeStruct((NL, D, NC), F32))(c_all_t, dmod)


def _adamw(parts, w, m, v):
    NL, P, R, C = parts.shape
    tr = R
    for cand in (256, 128, 64, 32, 16, 8):
        if R % cand == 0 and R > cand:
            tr = cand
            break

    def body(p_ref, w_ref, m_ref, v_ref, g_ref, d_ref, nm_ref, nv_ref):
        g = p_ref[0].astype(F32)
        for k in range(1, P):
            g = g + p_ref[k].astype(F32)
        m2 = ADAM_B1 * m_ref[...] + (1.0 - ADAM_B1) * g
        v2 = ADAM_B2 * v_ref[...] + (1.0 - ADAM_B2) * jnp.square(g)
        m_hat = m2 / (1.0 - ADAM_B1 ** ADAM_STEP)
        v_hat = v2 / (1.0 - ADAM_B2 ** ADAM_STEP)
        g_ref[...] = g
        d_ref[...] = -ADAM_LR * (m_hat / (jnp.sqrt(v_hat) + ADAM_EPS) + ADAM_WD * w_ref[...])
        nm_ref[...] = m2
        nv_ref[...] = v2

    pspec = pl.BlockSpec((None, P, tr, C), lambda l, i: (l, 0, i, 0))
    wspec = pl.BlockSpec((None, tr, C), lambda l, i: (l, i, 0))
    shp = jax.ShapeDtypeStruct((NL, R, C), F32)
    return _pcall(body, name="adamw", grid=(NL, R // tr), in_specs=[pspec, wspec, wspec, wspec],
                  out_specs=[wspec] * 4, out_shape=[shp] * 4)(parts, w, m, v)


def _block_diag(blocks):
    *lead, g, r, c = blocks.shape
    eye = jnp.eye(g, dtype=bool)[:, None, :, None]
    full = jnp.where(eye, blocks[..., :, :, None, :], 0.0)
    return full.reshape(*lead, g * r, g * c)


def _block_diag_extract(m, r, c):
    g = GROUPS_PER_TILE
    m5 = m.reshape(*m.shape[:-2], g, r, g, c)
    eye = jnp.eye(g, dtype=bool)[:, None, :, None]
    return jnp.sum(jnp.where(eye, m5, 0.0), axis=-2)


def kernel(x, c, ada_w, ada_b, norm1_g, w_in, q_norm_g, k_norm_g, ssm_a_re, ssm_a_im, ssm_log_dt, ssm_b_re, ssm_b_im, ssm_c_re, ssm_c_im, ssm_d, glu_w, glu_b, attn_out_g, ssm_out_g, w_out, norm2_g, ffn_w_up, ffn_conv_w, ffn_conv_b, ffn_w_down, loss_target, m_ada_w, m_ada_b, m_norm1_g, m_w_in, m_q_norm_g, m_k_norm_g, m_ssm_a_re, m_ssm_a_im, m_ssm_log_dt, m_ssm_b_re, m_ssm_b_im, m_ssm_c_re, m_ssm_c_im, m_ssm_d, m_glu_w, m_glu_b, m_attn_out_g, m_ssm_out_g, m_w_out, m_norm2_g, m_ffn_w_up, m_ffn_conv_w, m_ffn_conv_b, m_ffn_w_down, v_ada_w, v_ada_b, v_norm1_g, v_w_in, v_q_norm_g, v_k_norm_g, v_ssm_a_re, v_ssm_a_im, v_ssm_log_dt, v_ssm_b_re, v_ssm_b_im, v_ssm_c_re, v_ssm_c_im, v_ssm_d, v_glu_w, v_glu_b, v_attn_out_g, v_ssm_out_g, v_w_out, v_norm2_g, v_ffn_w_up, v_ffn_conv_w, v_ffn_conv_b, v_ffn_w_down):
    weights = dict(ada_w=ada_w, ada_b=ada_b, norm1_g=norm1_g, w_in=w_in, q_norm_g=q_norm_g, k_norm_g=k_norm_g,
                   ssm_a_re=ssm_a_re, ssm_a_im=ssm_a_im, ssm_log_dt=ssm_log_dt, ssm_b_re=ssm_b_re,
                   ssm_b_im=ssm_b_im, ssm_c_re=ssm_c_re, ssm_c_im=ssm_c_im, ssm_d=ssm_d, glu_w=glu_w, glu_b=glu_b,
                   attn_out_g=attn_out_g, ssm_out_g=ssm_out_g, w_out=w_out, norm2_g=norm2_g, ffn_w_up=ffn_w_up,
                   ffn_conv_w=ffn_conv_w, ffn_conv_b=ffn_conv_b, ffn_w_down=ffn_w_down)
    mom_m = dict(ada_w=m_ada_w, ada_b=m_ada_b, norm1_g=m_norm1_g, w_in=m_w_in, q_norm_g=m_q_norm_g,
                 k_norm_g=m_k_norm_g, ssm_a_re=m_ssm_a_re, ssm_a_im=m_ssm_a_im, ssm_log_dt=m_ssm_log_dt,
                 ssm_b_re=m_ssm_b_re, ssm_b_im=m_ssm_b_im, ssm_c_re=m_ssm_c_re, ssm_c_im=m_ssm_c_im, ssm_d=m_ssm_d,
                 glu_w=m_glu_w, glu_b=m_glu_b, attn_out_g=m_attn_out_g, ssm_out_g=m_ssm_out_g, w_out=m_w_out,
                 norm2_g=m_norm2_g, ffn_w_up=m_ffn_w_up, ffn_conv_w=m_ffn_conv_w, ffn_conv_b=m_ffn_conv_b,
                 ffn_w_down=m_ffn_w_down)
    mom_v = dict(ada_w=v_ada_w, ada_b=v_ada_b, norm1_g=v_norm1_g, w_in=v_w_in, q_norm_g=v_q_norm_g,
                 k_norm_g=v_k_norm_g, ssm_a_re=v_ssm_a_re, ssm_a_im=v_ssm_a_im, ssm_log_dt=v_ssm_log_dt,
                 ssm_b_re=v_ssm_b_re, ssm_b_im=v_ssm_b_im, ssm_c_re=v_ssm_c_re, ssm_c_im=v_ssm_c_im, ssm_d=v_ssm_d,
                 glu_w=v_glu_w, glu_b=v_glu_b, attn_out_g=v_attn_out_g, ssm_out_g=v_ssm_out_g, w_out=v_w_out,
                 norm2_g=v_norm2_g, ffn_w_up=v_ffn_w_up, ffn_conv_w=v_ffn_conv_w, ffn_conv_b=v_ffn_conv_b,
                 ffn_w_down=v_ffn_w_down)
    names = list(weights)
    big = ("ada_w", "w_in", "w_out", "ffn_w_up", "ffn_conv_w", "ffn_w_down")
    small = [n for n in names if n not in big]

    x = x[0]
    target = loss_target[0]
    L, D = x.shape
    NL = ada_w.shape[0]
    AW = D // 2
    SW = D - AW
    NH = AW // HEAD_DIM
    HP = AW // LANES
    G = SW // SSM_GROUP
    GT = SW // LANES
    NS = G * SSM_STATE
    NIN = w_in.shape[-1]
    FS = ffn_w_up.shape[-1]
    NSH = NDEV // 2
    NCA = ada_w.shape[-1]
    ROWS_OUT = w_out.shape[1]
    ROWS_DOWN = ffn_w_down.shape[1]
    B_ATT = min(L, 256)
    T_S5 = min(L, 512)
    tm = _row_tile(L, 1024)
    me = _my_index()

    cpad = jnp.reshape(c, (D // LANES, LANES))
    c_all = _gather_small(cpad, "gather_c")
    c_all = c_all.reshape(NDEV, D)
    w_in_g, w_out_g, w_up_g, conv_w_g, w_down_g = _gather_shards(
        [w_in.astype(BF16), w_out.astype(BF16), ffn_w_up.astype(BF16), ffn_conv_w, ffn_w_down.astype(BF16)],
        "gather_weights")
    w_out_g = w_out_g.reshape(NL, D, D)
    w_down_g = w_down_g.reshape(NL, NSH, 2 * ROWS_DOWN, D)
    conv_w_g = conv_w_g.reshape(NL, 2, NSH, 3, FS)
    conv_b_g = ffn_conv_b.reshape(NL, 2, NSH, 1, FS)

    ada_b_mine = lax.dynamic_slice_in_dim(ada_b, me * NCA, NCA, axis=1).reshape(NL, 1, NCA)
    mod_part = _ada_fwd(c_all, ada_w, ada_b_mine)
    mod_all = _gather_small(mod_part, "gather_mod")
    mod = lax.dynamic_index_in_dim(mod_all, me, axis=2, keepdims=False)
    mod = jnp.transpose(mod, (1, 0, 2)).reshape(NL, N_MOD, 1, D)

    row = lambda a: a.reshape(NL, 1, NS)
    ar_row, ai_row = row(ssm_a_re), row(ssm_a_im)
    ldt_row = row(jnp.broadcast_to(ssm_log_dt[:, :, None], (NL, G, SSM_STATE)))
    tiles = lambda a: a.reshape((NL, GT, GROUPS_PER_TILE) + a.shape[2:])
    braw_r = _block_diag(jnp.swapaxes(tiles(ssm_b_re), -1, -2))
    braw_i = _block_diag(jnp.swapaxes(tiles(ssm_b_im), -1, -2))
    crT = _block_diag(jnp.swapaxes(tiles(ssm_c_re), -1, -2)).astype(BF16)
    ciT = _block_diag(jnp.swapaxes(tiles(ssm_c_im), -1, -2)).astype(BF16)
    wg = _block_diag(tiles(glu_w)).astype(BF16)
    abr, abi, tabr, tabi, bsr, bsi = _s5_prep(ar_row, ai_row, ldt_row, braw_r, braw_i, T_S5)
    s5p = dict(bsr=bsr, bsi=bsi, tabr=tabr, tabi=tabi, crT=crT, ciT=ciT,
               d=ssm_d.reshape(NL, 1, SW), wg=wg, gb=glu_b.reshape(NL, 1, SW))

    gqk = jnp.concatenate([jnp.tile(q_norm_g, (1, NH)), jnp.tile(k_norm_g, (1, NH))], axis=1).reshape(NL, 1, 2 * AW)
    layer_params = dict(
        mod=mod, norm1_g=norm1_g.reshape(NL, 1, D), norm2_g=norm2_g.reshape(NL, 1, D), gqk=gqk,
        ga=attn_out_g.reshape(NL, 1, AW), gs=ssm_out_g.reshape(NL, 1, SW),
        w_in=w_in_g, w_out=w_out_g, w_up=w_up_g, conv_w=conv_w_g, conv_b=conv_b_g, w_down=w_down_g, s5=s5p)


    def mm_rows_shards(name, a, b, n, out_dtype=F32):
        K = a.shape[1]
        return _matmul(name, a, b, dims=NN, grid=(L // tm, NDEV),
                       a_spec=pl.BlockSpec((tm, K), lambda i, j: (i, 0)),
                       b_spec=pl.BlockSpec((None, K, n), lambda i, j: (j, 0, 0)),
                       out_shape=jax.ShapeDtypeStruct((L, NDEV * n), out_dtype),
                       out_specs=pl.BlockSpec((tm, n), lambda i, j: (i, j)))

    tn = min(D, 512)

    def resid_epilogue(acc, xres, gate):
        return acc, xres + gate * acc

    def layer_fwd(xin, lp):
        sh1, sc1, g1, sh2, sc2, g2 = (lp["mod"][k] for k in range(N_MOD))
        h = _lnmod_fwd(xin, lp["norm1_g"], sh1, sc1)
        p = mm_rows_shards("mm_in", h, lp["w_in"], NIN)
        qk = _qknorm_fwd(p, lp["gqk"], AW)
        o_attn, tot = _attn_fwd(qk, p, AW, B_ATT)
        u_st = _to_streams(p[:, 3 * AW:], T_S5)
        o_st, s_re, s_im, ypre = _s5_fwd(u_st, lp["s5"], T_S5)
        o_ssm = _from_streams(o_st, T_S5)
        o = _outnorm_fwd(o_attn, o_ssm, lp["ga"], lp["gs"])
        a1, x_mid = _matmul(
            "mm_out", o, lp["w_out"], dims=NN, grid=(L // tm, D // tn),
            a_spec=pl.BlockSpec((tm, D), lambda i, j: (i, 0)), b_spec=pl.BlockSpec((D, tn), lambda i, j: (0, j)),
            extra=(xin, g1), extra_specs=(pl.BlockSpec((tm, tn), lambda i, j: (i, j)),
                                          pl.BlockSpec((1, tn), lambda i, j: (0, j))),
            epilogue=resid_epilogue,
            out_shape=[jax.ShapeDtypeStruct((L, D), F32)] * 2,
            out_specs=[pl.BlockSpec((tm, tn), lambda i, j: (i, j))] * 2)
        h2 = _lnmod_fwd(x_mid, lp["norm2_g"], sh2, sc2)
        up = _matmul(
            "mm_up", h2, lp["w_up"], dims=NN, grid=(L // tm, NDEV),
            a_spec=pl.BlockSpec((tm, D), lambda i, j: (i, 0)), b_spec=pl.BlockSpec((None, D, FS), lambda i, j: (j, 0, 0)),
            out_shape=jax.ShapeDtypeStruct((NDEV, L, FS), F32),
            out_specs=pl.BlockSpec((None, tm, FS), lambda i, j: (j, i, 0)))
        up = up.reshape(2, NSH, L, FS)
        act = _convglu_fwd(up, lp["conv_w"], lp["conv_b"])
        a2, x_out = _matmul(
            "mm_down", act, lp["w_down"], dims=NN, grid=(L // tm, D // tn, NSH), kaxis=2, acc_shape=(tm, tn),
            a_spec=pl.BlockSpec((None, tm, FS), lambda i, j, k: (k, i, 0)),
            b_spec=pl.BlockSpec((None, FS, tn), lambda i, j, k: (k, 0, j)),
            extra=(x_mid, g2), extra_specs=(pl.BlockSpec((tm, tn), lambda i, j, k: (i, j)),
                                            pl.BlockSpec((1, tn), lambda i, j, k: (0, j))),
            epilogue=resid_epilogue,
            out_shape=[jax.ShapeDtypeStruct((L, D), F32)] * 2,
            out_specs=[pl.BlockSpec((tm, tn), lambda i, j, k: (i, j))] * 2)
        res = dict(x=xin, h=h, p=p, qk=qk, tot=tot, o_attn=o_attn, o_ssm=o_ssm, u_st=u_st, s_re=s_re, s_im=s_im, ypre=ypre,
                   o=o, a1=a1, x_mid=x_mid, h2=h2, up=up, act=act, a2=a2)
        return x_out, res

    per_layer = [jax.tree.map(lambda a: a[l], layer_params) for l in range(NL)]
    y, residuals = x, []
    for l in range(NL):
        y, res = layer_fwd(y, per_layer[l])
        residuals.append(res)

    loss_row, dy = _loss_head(y, target)
    loss = lax.psum(loss_row[0, 0], ("x", "y", "c"))

    def layer_bwd(dx, args):
        lp, r = args
        sh1, sc1, g1, sh2, sc2, g2 = (lp["mod"][k] for k in range(N_MOD))
        da2, dg2 = _gate_bwd(dx, r["a2"], g2)
        dact = _matmul(
            "mm_dact", da2, lp["w_down"], dims=NT, grid=(L // tm, NSH),
            a_spec=pl.BlockSpec((tm, D), lambda i, j: (i, 0)), b_spec=pl.BlockSpec((None, FS, D), lambda i, j: (j, 0, 0)),
            out_shape=jax.ShapeDtypeStruct((NSH, L, FS), F32),
            out_specs=pl.BlockSpec((None, tm, FS), lambda i, j: (j, i, 0)))
        dw_down = _matmul(
            "mm_dw_down", r["act"], da2, dims=TN, grid=(NSH, D // tn, L // tm), kaxis=2, acc_shape=(FS, tn),
            a_spec=pl.BlockSpec((None, tm, FS), lambda j, n, k: (j, k, 0)),
            b_spec=pl.BlockSpec((tm, tn), lambda j, n, k: (k, n)),
            out_shape=jax.ShapeDtypeStruct((NSH, FS, D), BF16),
            out_specs=pl.BlockSpec((None, FS, tn), lambda j, n, k: (j, 0, n)))
        dup4, dcw, dcb = _convglu_bwd(r["up"], dact, lp["conv_w"], lp["conv_b"])
        dup = dup4.reshape(NDEV, L, FS)
        dh2 = _matmul(
            "mm_dh2", dup, lp["w_up"], dims=NT, grid=(L // tm, NDEV), kaxis=1, acc_shape=(tm, D),
            a_spec=pl.BlockSpec((None, tm, FS), lambda i, k: (k, i, 0)),
            b_spec=pl.BlockSpec((None, D, FS), lambda i, k: (k, 0, 0)),
            out_shape=jax.ShapeDtypeStruct((L, D), F32), out_specs=pl.BlockSpec((tm, D), lambda i, k: (i, 0)))
        dw_up = _matmul(
            "mm_dw_up", r["h2"], dup, dims=TN, grid=(NDEV, L // tm), kaxis=1, acc_shape=(D, FS),
            a_spec=pl.BlockSpec((tm, D), lambda j, k: (k, 0)), b_spec=pl.BlockSpec((None, tm, FS), lambda j, k: (j, k, 0)),
            out_shape=jax.ShapeDtypeStruct((NDEV, D, FS), BF16),
            out_specs=pl.BlockSpec((None, D, FS), lambda j, k: (j, 0, 0)))
        dxm, dn2, dsh2, dsc2 = _lnmod_bwd(dh2, r["x_mid"], lp["norm2_g"], sh2, sc2, dx)
        da1, dg1 = _gate_bwd(dxm, r["a1"], g1)
        do = _matmul(
            "mm_do", da1, lp["w_out"], dims=NT, grid=(L // tm, D // tn),
            a_spec=pl.BlockSpec((tm, D), lambda i, j: (i, 0)), b_spec=pl.BlockSpec((tn, D), lambda i, j: (j, 0)),
            out_shape=jax.ShapeDtypeStruct((L, D), F32), out_specs=pl.BlockSpec((tm, tn), lambda i, j: (i, j)))
        dw_out = _matmul(
            "mm_dw_out", r["o"], da1, dims=TN, grid=(D // tn, D // tn, L // tm), kaxis=2, acc_shape=(tn, tn),
            a_spec=pl.BlockSpec((tm, tn), lambda m, n, k: (k, m)), b_spec=pl.BlockSpec((tm, tn), lambda m, n, k: (k, n)),
            out_shape=jax.ShapeDtypeStruct((D, D), BF16), out_specs=pl.BlockSpec((tn, tn), lambda m, n, k: (m, n)))
        doa, dos, dga, dgs = _outnorm_bwd(do, r["o_attn"], r["o_ssm"], lp["ga"], lp["gs"])
        (du_st, dbsr, dbsi, dcr, dci, dd, dwg, dgb, dabr, dabi) = _s5_bwd(
            _to_streams(dos, T_S5), r["u_st"], r["s_re"], r["s_im"], r["ypre"], lp["s5"], T_S5)
        du = _from_streams(du_st, T_S5)
        dq, dk, dv = _attn_bwd(r["qk"], r["p"], doa, r["tot"], AW, B_ATT)
        dqk, dgqk = _qknorm_bwd(jnp.concatenate([dq, dk], axis=1), r["p"], lp["gqk"], AW)
        dp = jnp.concatenate([dqk, dv.astype(BF16), du], axis=1)
        dh = _matmul(
            "mm_dh", dp, lp["w_in"], dims=NT, grid=(L // tm, NDEV), kaxis=1, acc_shape=(tm, D),
            a_spec=pl.BlockSpec((tm, NIN), lambda i, k: (i, k)), b_spec=pl.BlockSpec((None, D, NIN), lambda i, k: (k, 0, 0)),
            out_shape=jax.ShapeDtypeStruct((L, D), F32), out_specs=pl.BlockSpec((tm, D), lambda i, k: (i, 0)))
        dw_in = _matmul(
            "mm_dw_in", r["h"], dp, dims=TN, grid=(NDEV, L // tm), kaxis=1, acc_shape=(D, NIN),
            a_spec=pl.BlockSpec((tm, D), lambda j, k: (k, 0)), b_spec=pl.BlockSpec((tm, NIN), lambda j, k: (k, j)),
            out_shape=jax.ShapeDtypeStruct((NDEV, D, NIN), BF16),
            out_specs=pl.BlockSpec((None, D, NIN), lambda j, k: (j, 0, 0)))
        dx0, dn1, dsh1, dsc1 = _lnmod_bwd(dh, r["x"], lp["norm1_g"], sh1, sc1, dxm)
        grads = dict(
            dmod=jnp.concatenate([dsh1, dsc1, dg1, dsh2, dsc2, dg2], axis=1), dn1=dn1, dn2=dn2, dgqk=dgqk,
            dga=dga, dgs=dgs, dbsr=dbsr, dbsi=dbsi, dcr=dcr, dci=dci, dd=dd, dwg=dwg, dgb=dgb, dabr=dabr, dabi=dabi,
            dcb=dcb, dw_in=dw_in, dw_out=dw_out.reshape(NDEV, ROWS_OUT, D), dw_up=dw_up,
            dcw=dcw.reshape(NDEV, 3, FS), dw_down=dw_down.reshape(NDEV, ROWS_DOWN, D))
        return dx0, grads

    grad_x, layer_grads = dy, [None] * NL
    for l in reversed(range(NL)):
        grad_x, layer_grads[l] = layer_bwd(grad_x, (per_layer[l], residuals[l]))
    gr = jax.tree.map(lambda *a: jnp.stack(a), *layer_grads)

    dar, dai, dldt, dbr_bd, dbi_bd = _s5_prep_bwd(ar_row, ai_row, ldt_row, braw_r, braw_i,
                                                  gr["dabr"], gr["dabi"], gr["dbsr"], gr["dbsi"])
    unt = lambda a: a.reshape((NL, G) + a.shape[3:])
    local = dict(
        ada_b=gr["dmod"].reshape(NL, N_MOD * D),
        norm1_g=gr["dn1"].reshape(NL, D), norm2_g=gr["dn2"].reshape(NL, D),
        q_norm_g=gr["dgqk"].reshape(NL, 2, NH, HEAD_DIM)[:, 0].sum(axis=1),
        k_norm_g=gr["dgqk"].reshape(NL, 2, NH, HEAD_DIM)[:, 1].sum(axis=1),
        ssm_a_re=dar.reshape(NL, G, SSM_STATE), ssm_a_im=dai.reshape(NL, G, SSM_STATE),
        ssm_log_dt=dldt.reshape(NL, G, SSM_STATE).sum(axis=-1),
        ssm_b_re=jnp.swapaxes(unt(_block_diag_extract(dbr_bd, SSM_GROUP, SSM_STATE)), -1, -2),
        ssm_b_im=jnp.swapaxes(unt(_block_diag_extract(dbi_bd, SSM_GROUP, SSM_STATE)), -1, -2),
        ssm_c_re=jnp.swapaxes(unt(_block_diag_extract(gr["dcr"], SSM_STATE, SSM_GROUP)), -1, -2),
        ssm_c_im=jnp.swapaxes(unt(_block_diag_extract(gr["dci"], SSM_STATE, SSM_GROUP)), -1, -2),
        ssm_d=gr["dd"].reshape(NL, G, SSM_GROUP),
        glu_w=unt(_block_diag_extract(gr["dwg"], SSM_GROUP, SSM_GROUP)),
        glu_b=gr["dgb"].reshape(NL, G, SSM_GROUP),
        attn_out_g=gr["dga"].reshape(NL, AW), ssm_out_g=gr["dgs"].reshape(NL, SW),
        ffn_conv_b=gr["dcb"].reshape(NL, 2 * NSH * FS),
    )

    def pack(tree):
        flat = jnp.concatenate([tree[n].reshape(-1) for n in small])
        pad = (-flat.shape[0]) % (512 * LANES)
        return jnp.pad(flat, (0, pad)).reshape(1, -1, LANES)

    (small_parts,) = _gather_shards([pack(local)], "gather_small_grads")
    sg, sd, sm, sv = _adamw(small_parts, pack(weights), pack(mom_m), pack(mom_v))

    def unpack(buf):
        flat = buf.reshape(-1)
        out, off = {}, 0
        for n in small:
            size = weights[n].size
            out[n] = flat[off:off + size].reshape(weights[n].shape)
            off += size
        return out

    ug, ud, um, uv = unpack(sg), unpack(sd), unpack(sm), unpack(sv)
    results = {n: (ug[n], ud[n], um[n], uv[n]) for n in small}

    dmod_all = _gather_small(gr["dmod"].reshape(NL, N_MOD * D), "gather_dmod")
    dmod_mine = lax.dynamic_slice_in_dim(dmod_all, me * NCA, NCA, axis=2)
    d_ada_w = _ada_bwd(jnp.transpose(c_all), jnp.transpose(dmod_mine, (1, 0, 2)))
    results["ada_w"] = tuple(_adamw(d_ada_w[:, None], ada_w, m_ada_w, v_ada_w))

    my_core = lax.axis_index("c")
    by_target = [g.reshape((NL, NDEV // 2, 2) + g.shape[2:])
                 for g in (gr["dw_in"], gr["dw_out"], gr["dw_up"], gr["dcw"], gr["dw_down"])]
    mine = [lax.dynamic_index_in_dim(g, my_core, axis=2, keepdims=False) for g in by_target]
    theirs = _sibling_exchange(
        [lax.dynamic_index_in_dim(g, 1 - my_core, axis=2, keepdims=False) for g in by_target], "pair_weight_grads")
    pair_sums = [_pair_add(a, b) for a, b in zip(mine, theirs)]
    parts = _chip_exchange(pair_sums, "scatter_weight_grads")
    for n, pt in zip(("w_in", "w_out", "ffn_w_up", "ffn_conv_w", "ffn_w_down"), parts):
        results[n] = tuple(_adamw(pt, weights[n], mom_m[n], mom_v[n]))

    out = [loss, grad_x[None]]
    for k in range(4):
        out.extend(results[n][k] for n in names)
    return tuple(out)
```

```python
import jax
import jax.numpy as jnp
from jax import lax
from jax.experimental import pallas as pl
from jax.experimental.pallas import tpu as pltpu

F32 = jnp.float32
BF16 = jnp.bfloat16
NDEV = 8
LANES = 128
HEAD_DIM = 64
SSM_GROUP = 16
SSM_STATE = 64
GROUPS_PER_TILE = LANES // SSM_GROUP
STATE_TILE = GROUPS_PER_TILE * SSM_STATE
N_MOD = 6
ATTN_STRIP = 32
EPS = 1e-6
ADAM_LR, ADAM_B1, ADAM_B2, ADAM_EPS, ADAM_WD, ADAM_STEP = 0.001, 0.9, 0.999, 1e-08, 0.01, 10
VMEM_LIMIT = 48 * 1024 * 1024
MESH_IDS = pl.DeviceIdType.MESH

NN = (((1,), (0,)), ((), ()))
NT = (((1,), (1,)), ((), ()))
TN = (((0,), (0,)), ((), ()))


def _dot(a, b, dims=NN):
    return lax.dot_general(a, b, dims, preferred_element_type=F32)


def _pcall(body, *, name, out_shape, in_specs, out_specs, grid=(), scratch=()):
    return pl.pallas_call(
        body, name=name, grid=grid, in_specs=in_specs, out_specs=out_specs, out_shape=out_shape,
        scratch_shapes=list(scratch),
        compiler_params=pltpu.CompilerParams(vmem_limit_bytes=VMEM_LIMIT))


def _row_tile(n, want=512):
    t = min(n, want)
    assert n % t == 0
    return t


def _my_index():
    return 4 * lax.axis_index("x") + 2 * lax.axis_index("y") + lax.axis_index("c")


HBM_SPEC = pl.BlockSpec(memory_space=pltpu.HBM)


def _mesh_place():
    x, y, c = lax.axis_index("x"), lax.axis_index("y"), lax.axis_index("c")
    chips = [(1 - x, y), (x, 1 - y), (1 - x, 1 - y)]
    return x, y, c, chips


def _gather_small(arr, name):
    def body(in_ref, out_ref, send_sems, recv_sems, local_sem):
        x, y, c, _ = _mesh_place()
        me = 4 * x + 2 * y + c
        own = pltpu.make_async_copy(in_ref, out_ref.at[me], local_sem)
        own.start()
        sends, recvs = [], []
        for k in range(1, NDEV):
            px = 1 - x if k & 4 else x
            py = 1 - y if k & 2 else y
            pc = 1 - c if k & 1 else c
            common = dict(send_sem=send_sems.at[k - 1], recv_sem=recv_sems.at[k - 1],
                          device_id=(px, py, pc), device_id_type=MESH_IDS)
            snd = pltpu.make_async_remote_copy(src_ref=in_ref, dst_ref=out_ref.at[me], **common)
            snd.start()
            sends.append(snd)
            recvs.append(pltpu.make_async_remote_copy(
                src_ref=in_ref, dst_ref=out_ref.at[4 * px + 2 * py + pc], **common))
        for r in recvs:
            r.wait_recv()
        for s in sends:
            s.wait_send()
        own.wait()

    return pl.pallas_call(
        body, name=name, out_shape=jax.ShapeDtypeStruct((NDEV,) + arr.shape, arr.dtype),
        in_specs=[HBM_SPEC], out_specs=HBM_SPEC,
        scratch_shapes=[pltpu.SemaphoreType.DMA((NDEV - 1,)), pltpu.SemaphoreType.DMA((NDEV - 1,)),
                        pltpu.SemaphoreType.DMA(())],
    )(arr)


def _start_by_layer(make, nl):
    for l in range(nl):
        make(l).start()


def _gather_shards(arrs, name):
    n = len(arrs)
    nl = arrs[0].shape[0]

    def body(*refs):
        ins, outs = refs[:n], refs[n:2 * n]
        send_sems, recv_sems = refs[2 * n:]
        x, y, c, chips = _mesh_place()
        dev = lambda px, py, pc: 4 * px + 2 * py + pc

        def copy(k, a, src, block, to, layer=slice(None)):
            return pltpu.make_async_remote_copy(
                src_ref=src.at[layer], dst_ref=outs[a].at[layer, block], send_sem=send_sems.at[k, a],
                recv_sem=recv_sems.at[k, a], device_id=to, device_id_type=MESH_IDS)

        me = dev(x, y, c)
        sent = []
        for a in range(n):
            _start_by_layer(lambda l: copy(0, a, ins[a], me, (x, y, 1 - c), l), nl)
            sent.append(copy(0, a, ins[a], me, (x, y, 1 - c)))
        for j, (px, py) in enumerate(chips):
            for a in range(n):
                _start_by_layer(lambda l: copy(1 + j, a, ins[a], me, (px, py, c), l), nl)
                sent.append(copy(1 + j, a, ins[a], me, (px, py, c)))
        for j, (px, py) in enumerate(chips):
            for a in range(n):
                blk = dev(px, py, c)
                copy(1 + j, a, ins[a], blk, (x, y, c)).wait_recv()
                got = outs[a].at[:, blk]
                _start_by_layer(lambda l: copy(4 + j, a, got, blk, (x, y, 1 - c), l), nl)
                sent.append(copy(4 + j, a, got, blk, (x, y, 1 - c)))
        for a in range(n):
            copy(0, a, ins[a], dev(x, y, 1 - c), (x, y, c)).wait_recv()
        for j, (px, py) in enumerate(chips):
            for a in range(n):
                copy(4 + j, a, ins[a], dev(px, py, 1 - c), (x, y, c)).wait_recv()
        for s in sent:
            s.wait_send()

    out_shape = [jax.ShapeDtypeStruct((a.shape[0], NDEV) + a.shape[1:], a.dtype) for a in arrs]
    outs = pl.pallas_call(
        body, name=name, out_shape=out_shape, in_specs=[HBM_SPEC] * n, out_specs=[HBM_SPEC] * n,
        scratch_shapes=[pltpu.SemaphoreType.DMA((7, n)), pltpu.SemaphoreType.DMA((7, n))],
    )(*arrs)
    me = _my_index()
    return [lax.dynamic_update_slice_in_dim(o, a[:, None], me, axis=1) for o, a in zip(outs, arrs)]


def _sibling_exchange(arrs, name):
    n = len(arrs)
    nl, nchips = arrs[0].shape[:2]

    def body(*refs):
        ins, got = refs[:n], refs[n:2 * n]
        send_sems, recv_sems = refs[2 * n:]
        x, y, c, _ = _mesh_place()

        def send(a, idx):
            return pltpu.make_async_remote_copy(
                src_ref=ins[a].at[idx], dst_ref=got[a].at[idx], send_sem=send_sems.at[a],
                recv_sem=recv_sems.at[a], device_id=(x, y, 1 - c), device_id_type=MESH_IDS)

        everything = (slice(None), slice(None))
        for a in range(n):
            for q in range(nchips):
                _start_by_layer(lambda l: send(a, (l, q)), nl)
        for a in range(n):
            send(a, everything).wait_recv()
        for a in range(n):
            send(a, everything).wait_send()

    shapes = [jax.ShapeDtypeStruct(a.shape, a.dtype) for a in arrs]
    return list(pl.pallas_call(
        body, name=name, out_shape=shapes, in_specs=[HBM_SPEC] * n, out_specs=[HBM_SPEC] * n,
        scratch_shapes=[pltpu.SemaphoreType.DMA((n,)), pltpu.SemaphoreType.DMA((n,))],
    )(*arrs))


def _chip_exchange(arrs, name):
    n = len(arrs)
    nl = arrs[0].shape[0]

    def body(*refs):
        ins, outs = refs[:n], refs[n:2 * n]
        send_sems, recv_sems = refs[2 * n:]
        x, y, c, chips = _mesh_place()
        mine = 2 * x + y

        def copy(j, a, target, source, layer=slice(None)):
            px, py = chips[j]
            return pltpu.make_async_remote_copy(
                src_ref=ins[a].at[layer, target], dst_ref=outs[a].at[layer, source], send_sem=send_sems.at[j, a],
                recv_sem=recv_sems.at[j, a], device_id=(px, py, c), device_id_type=MESH_IDS)

        for j, (px, py) in enumerate(chips):
            for a in range(n):
                _start_by_layer(lambda l: copy(j, a, 2 * px + py, mine, l), nl)
        for j, (px, py) in enumerate(chips):
            for a in range(n):
                copy(j, a, mine, 2 * px + py).wait_recv()
        for j, (px, py) in enumerate(chips):
            for a in range(n):
                copy(j, a, 2 * px + py, mine).wait_send()

    out_shape = [jax.ShapeDtypeStruct(a.shape, a.dtype) for a in arrs]
    outs = pl.pallas_call(
        body, name=name, out_shape=out_shape, in_specs=[HBM_SPEC] * n, out_specs=[HBM_SPEC] * n,
        scratch_shapes=[pltpu.SemaphoreType.DMA((3, n)), pltpu.SemaphoreType.DMA((3, n))],
    )(*arrs)
    mine = 2 * lax.axis_index("x") + lax.axis_index("y")
    return [lax.dynamic_update_slice_in_dim(o, lax.dynamic_slice_in_dim(a, mine, 1, axis=1), mine, axis=1)
            for o, a in zip(outs, arrs)]


def _pair_add(a, b):
    NL, NC, R, C = a.shape
    tr = R
    for cand in (256, 128, 64, 32, 16):
        if R % cand == 0 and R > cand:
            tr = cand
            break

    def body(a_ref, b_ref, o_ref):
        o_ref[...] = (a_ref[...].astype(F32) + b_ref[...].astype(F32)).astype(o_ref.dtype)

    spec = pl.BlockSpec((None, NC, tr, C), lambda l, i: (l, 0, i, 0))
    return _pcall(body, name="pair_add", grid=(NL, R // tr), in_specs=[spec, spec], out_specs=spec,
                  out_shape=jax.ShapeDtypeStruct(a.shape, a.dtype))(a, b)


def _matmul(name, a, b, *, dims, grid, a_spec, b_spec, out_shape, out_specs, kaxis=None, acc_shape=None,
            extra=(), extra_specs=(), epilogue=None):
    nk = grid[kaxis] if kaxis is not None else 1
    ne = len(extra)
    multi = isinstance(out_shape, (list, tuple))
    n_out = len(out_shape) if multi else 1

    def body(*refs):
        a_ref, b_ref = refs[0], refs[1]
        ex = refs[2:2 + ne]
        outs = refs[2 + ne:2 + ne + n_out]

        def write(res):
            vals = epilogue(res, *[e[...] for e in ex]) if epilogue is not None else (res,)
            for o, v in zip(outs, vals):
                o[...] = v.astype(o.dtype)

        part = _dot(a_ref[...].astype(BF16), b_ref[...].astype(BF16), dims)
        if nk == 1:
            write(part)
        else:
            acc = refs[-1]
            k = pl.program_id(kaxis)

            @pl.when(k == 0)
            def _():
                acc[...] = part

            @pl.when(k > 0)
            def _():
                acc[...] += part

            @pl.when(k == nk - 1)
            def _():
                write(acc[...])

    scratch = [pltpu.VMEM(acc_shape, F32)] if nk > 1 else []
    return _pcall(body, name=name, grid=grid, in_specs=[a_spec, b_spec, *extra_specs],
                  out_specs=out_specs, out_shape=out_shape, scratch=scratch)(a, b, *extra)


def _rms(x, g):
    inv = lax.rsqrt(jnp.mean(x * x, axis=-1, keepdims=True) + EPS)
    return x * inv * g


def _lnmod_math(x, g, sh, sc):
    return _rms(x, g) * (1.0 + sc) + sh


def _head_sums(x):
    row = lax.broadcasted_iota(jnp.int32, (LANES, LANES), 0) // HEAD_DIM
    col = lax.broadcasted_iota(jnp.int32, (LANES, LANES), 1) // HEAD_DIM
    same_head = jnp.where(row == col, 1.0, 0.0).astype(BF16)
    hi = x.astype(BF16)
    lo = (x - hi.astype(F32)).astype(BF16)
    return _dot(hi, same_head) + _dot(lo, same_head)


def _qkn_inv(p):
    return lax.rsqrt(_head_sums(p * p) / HEAD_DIM + EPS)


def _glu_math(val, gate):
    return jax.nn.gelu(gate) * val


def _accumulate(ref, val, first):
    @pl.when(first)
    def _():
        ref[...] = val

    @pl.when(jnp.logical_not(first))
    def _():
        ref[...] += val


def _lnmod_fwd(x, g, sh, sc):
    L, D = x.shape
    tm = _row_tile(L)

    def body(x_ref, g_ref, sh_ref, sc_ref, h_ref):
        h_ref[...] = _lnmod_math(x_ref[...], g_ref[...], sh_ref[...], sc_ref[...]).astype(BF16)

    row = pl.BlockSpec((tm, D), lambda i: (i, 0))
    vec = pl.BlockSpec((1, D), lambda i: (0, 0))
    return _pcall(body, name="lnmod_fwd", grid=(L // tm,), in_specs=[row, vec, vec, vec], out_specs=row,
                  out_shape=jax.ShapeDtypeStruct((L, D), BF16))(x, g, sh, sc)


def _lnmod_bwd(dh, x, g, sh, sc, dres):
    L, D = x.shape
    tm = _row_tile(L)

    def body(dh_ref, x_ref, g_ref, sh_ref, sc_ref, res_ref, dx_ref, dg_ref, dsh_ref, dsc_ref):
        _, vjp = jax.vjp(_lnmod_math, x_ref[...], g_ref[...], sh_ref[...], sc_ref[...])
        dx, dg, dsh, dsc = vjp(dh_ref[...])
        dx_ref[...] = dx + res_ref[...]
        first = pl.program_id(0) == 0
        _accumulate(dg_ref, dg, first)
        _accumulate(dsh_ref, dsh, first)
        _accumulate(dsc_ref, dsc, first)

    row = pl.BlockSpec((tm, D), lambda i: (i, 0))
    vec = pl.BlockSpec((1, D), lambda i: (0, 0))
    vs = jax.ShapeDtypeStruct((1, D), F32)
    return _pcall(body, name="lnmod_bwd", grid=(L // tm,), in_specs=[row, row, vec, vec, vec, row],
                  out_specs=[row, vec, vec, vec],
                  out_shape=[jax.ShapeDtypeStruct((L, D), F32), vs, vs, vs])(dh, x, g, sh, sc, dres)


def _gate_bwd(dx, a, gate):
    L, D = dx.shape
    tm = _row_tile(L)

    def body(dx_ref, a_ref, g_ref, da_ref, dg_ref):
        dxv = dx_ref[...]
        da_ref[...] = (g_ref[...] * dxv).astype(BF16)
        _accumulate(dg_ref, jnp.sum(dxv * a_ref[...], axis=0, keepdims=True), pl.program_id(0) == 0)

    row = pl.BlockSpec((tm, D), lambda i: (i, 0))
    vec = pl.BlockSpec((1, D), lambda i: (0, 0))
    return _pcall(body, name="gate_bwd", grid=(L // tm,), in_specs=[row, row, vec], out_specs=[row, vec],
                  out_shape=[jax.ShapeDtypeStruct((L, D), BF16), jax.ShapeDtypeStruct((1, D), F32)])(dx, a, gate)


def _qknorm_fwd(p, gqk, AW):
    L = p.shape[0]
    tm = _row_tile(L, 2048)
    ncol = 2 * AW // LANES

    def body(p_ref, g_ref, o_ref):
        p = p_ref[...]
        o_ref[...] = (p * _qkn_inv(p) * g_ref[...]).astype(BF16)

    blk = pl.BlockSpec((tm, LANES), lambda i, j: (i, j))
    vec = pl.BlockSpec((1, LANES), lambda i, j: (0, j))
    return _pcall(body, name="qknorm_fwd", grid=(L // tm, ncol), in_specs=[blk, vec], out_specs=blk,
                  out_shape=jax.ShapeDtypeStruct((L, 2 * AW), BF16))(p, gqk)


def _qknorm_bwd(dqk, p, gqk, AW):
    L = p.shape[0]
    tm = _row_tile(L, 2048)
    ncol = 2 * AW // LANES

    def body(d_ref, p_ref, g_ref, dp_ref, dg_ref):
        p, dy = p_ref[...], d_ref[...]
        inv = _qkn_inv(p)
        gdy = g_ref[...] * dy
        dp_ref[...] = (inv * gdy - p * (inv * inv * inv) * (_head_sums(p * gdy) / HEAD_DIM)).astype(BF16)
        _accumulate(dg_ref, jnp.sum(dy * p * inv, axis=0, keepdims=True), pl.program_id(1) == 0)

    blk = pl.BlockSpec((tm, LANES), lambda j, i: (i, j))
    vec = pl.BlockSpec((1, LANES), lambda j, i: (0, j))
    return _pcall(body, name="qknorm_bwd", grid=(ncol, L // tm), in_specs=[blk, blk, vec], out_specs=[blk, vec],
                  out_shape=[jax.ShapeDtypeStruct((L, 2 * AW), BF16),
                             jax.ShapeDtypeStruct((1, 2 * AW), F32)])(dqk, p, gqk)


def _outnorm_fwd(oa, os_, ga, gs):
    L, AW = oa.shape
    SW = os_.shape[1]
    tm = _row_tile(L)

    def body(oa_ref, os_ref, ga_ref, gs_ref, o_ref):
        o_ref[:, :AW] = _rms(oa_ref[...], ga_ref[...]).astype(BF16)
        o_ref[:, AW:] = _rms(os_ref[...], gs_ref[...]).astype(BF16)

    ra = pl.BlockSpec((tm, AW), lambda i: (i, 0))
    rs = pl.BlockSpec((tm, SW), lambda i: (i, 0))
    va = pl.BlockSpec((1, AW), lambda i: (0, 0))
    vs = pl.BlockSpec((1, SW), lambda i: (0, 0))
    ro = pl.BlockSpec((tm, AW + SW), lambda i: (i, 0))
    return _pcall(body, name="outnorm_fwd", grid=(L // tm,), in_specs=[ra, rs, va, vs], out_specs=ro,
                  out_shape=jax.ShapeDtypeStruct((L, AW + SW), BF16))(oa, os_, ga, gs)


def _outnorm_bwd(do, oa, os_, ga, gs):
    L, AW = oa.shape
    SW = os_.shape[1]
    tm = _row_tile(L)

    def body(do_ref, oa_ref, os_ref, ga_ref, gs_ref, doa_ref, dos_ref, dga_ref, dgs_ref):
        first = pl.program_id(0) == 0
        _, vjp_a = jax.vjp(_rms, oa_ref[...], ga_ref[...])
        doa, dga = vjp_a(do_ref[:, :AW])
        _, vjp_s = jax.vjp(_rms, os_ref[...], gs_ref[...])
        dos, dgs = vjp_s(do_ref[:, AW:])
        doa_ref[...] = doa
        dos_ref[...] = dos
        _accumulate(dga_ref, dga, first)
        _accumulate(dgs_ref, dgs, first)

    ra = pl.BlockSpec((tm, AW), lambda i: (i, 0))
    rs = pl.BlockSpec((tm, SW), lambda i: (i, 0))
    va = pl.BlockSpec((1, AW), lambda i: (0, 0))
    vs = pl.BlockSpec((1, SW), lambda i: (0, 0))
    ro = pl.BlockSpec((tm, AW + SW), lambda i: (i, 0))
    return _pcall(body, name="outnorm_bwd", grid=(L // tm,), in_specs=[ro, ra, rs, va, vs],
                  out_specs=[ra, rs, va, vs],
                  out_shape=[jax.ShapeDtypeStruct((L, AW), F32), jax.ShapeDtypeStruct((L, SW), F32),
                             jax.ShapeDtypeStruct((1, AW), F32), jax.ShapeDtypeStruct((1, SW), F32)])(
                                 do, oa, os_, ga, gs)


def _softplus_neg_abs(z):
    return jnp.log(1.0 + jnp.exp(-jnp.abs(z)))


def _attn_masks(B):
    row = lax.broadcasted_iota(jnp.int32, (B, B), 0)
    col = lax.broadcasted_iota(jnp.int32, (B, B), 1)
    strict = col < row
    upper = jnp.where(row > col, 1.0, 0.0).astype(BF16)
    lower = jnp.where(row < col, 1.0, 0.0).astype(BF16)
    return strict, upper, lower


def _attn_fwd(qk, p, AW, B):
    L = qk.shape[0]
    HP = AW // LANES
    nb = L // B

    def body(q_ref, k_ref, v_ref, o_ref, tot_ref, *scr):
        lb_scr = (scr[0:2], scr[2:4])
        tail_scr = (scr[4:6], scr[6:8])
        sum_scr = scr[8:10]
        i = pl.program_id(1)
        m0 = lax.broadcasted_iota(jnp.int32, (1, LANES), 1) < HEAD_DIM
        strict, upper, _ = _attn_masks(B)
        q = q_ref[...] * 0.125
        zq = jnp.zeros_like(q)
        qh = (jnp.where(m0, q, zq), jnp.where(m0, zq, q))

        def keys(j):
            start = pl.multiple_of(jnp.maximum(j, 0) * B, B)
            return k_ref[pl.ds(start, B), :]

        def vals(j):
            start = pl.multiple_of(jnp.maximum(j, 0) * B, B)
            return v_ref[pl.ds(start, B), :].astype(BF16)

        strips = [slice(s, min(s + ATTN_STRIP, B)) for s in range(0, B, ATTN_STRIP)]

        def scores(j):
            kj = keys(j)
            return tuple(_dot(qh[h], kj, NT) for h in (0, 1))

        def logits(zs2, slot, diag):
            for h in (0, 1):
                z = zs2[h]
                his = []
                for rows in strips:
                    zs = z[rows]
                    lb = jnp.minimum(zs, 0.0) - _softplus_neg_abs(zs)
                    l1 = lb - zs
                    if diag:
                        l1 = jnp.where(strict[rows], l1, 0.0)
                    lb_scr[slot][h][rows, :] = lb
                    his.append(l1.astype(BF16))
                    rsum = jnp.sum(l1, axis=-1, keepdims=True)
                    if h == 0:
                        sum_scr[slot][rows, :] = jnp.broadcast_to(rsum, (rows.stop - rows.start, LANES))
                    else:
                        sum_scr[slot][rows, :] = jnp.where(m0, sum_scr[slot][rows, :], rsum)
                cat = lambda xs: jnp.concatenate(xs, axis=0)
                tail_scr[slot][h][...] = _dot(cat(his), upper)

        def attend(j, slot, diag):
            vj = vals(j)
            pv = []
            for h in (0, 1):
                ws = []
                for rows in strips:
                    w = jnp.exp(lb_scr[slot][h][rows, :] + tail_scr[slot][h][rows, :])
                    if diag:
                        w = jnp.where(strict[rows], w, 0.0)
                    ws.append(w.astype(BF16))
                pv.append(_dot(jnp.concatenate(ws, axis=0), vj))
            return jnp.where(m0, pv[0], pv[1])

        logits(scores(i), 0, True)
        o_ref[...] = attend(i, 0, True)
        tot_ref[...] = sum_scr[0][...]

        def half(j, slot):
            z = scores(j - 1)
            pv = attend(j, slot, False)
            logits(z, 1 - slot, False)
            o_ref[...] += pv * jnp.exp(tot_ref[...])
            tot_ref[...] += sum_scr[slot][...]

        @pl.when(i > 0)
        def _():
            logits(scores(i - 1), 1, False)

            @pl.loop(0, (i + 1) // 2)
            def _(t):
                j = i - 1 - 2 * t
                half(j, 1)

                @pl.when(j > 0)
                def _():
                    half(j - 1, 0)


    qspec = pl.BlockSpec((B, LANES), lambda hp, i: (i, hp))
    kspec = pl.BlockSpec((L, LANES), lambda hp, i: (0, HP + hp))
    vspec = pl.BlockSpec((L, LANES), lambda hp, i: (0, 2 * HP + hp))
    ospec = pl.BlockSpec((B, LANES), lambda hp, i: (i, hp))
    shp = jax.ShapeDtypeStruct((L, AW), F32)
    return _pcall(body, name="attn_fwd", grid=(HP, nb), in_specs=[qspec, kspec, vspec],
                  out_specs=[ospec, ospec], out_shape=[shp, shp],
                  scratch=[pltpu.VMEM((B, B), F32)] * 8 + [pltpu.VMEM((B, LANES), F32)] * 2)(qk, qk, p)


def _attn_bwd(qk, p, do, tot, AW, B):
    L = qk.shape[0]
    HP = AW // LANES
    nb = L // B

    def body(q_ref, k_ref, v_ref, do_ref, tot_ref, dq_ref, dk_ref, dv_ref, *scr):
        lb_scr, tail_scr, dw_scr, e_scr, beta_scr = scr[0:2], scr[2:4], scr[4:6], scr[6:8], scr[8:10]
        dos_scr, bsum_scr, left_scr, esum_scr, ecum_scr = scr[10:15]
        i = pl.program_id(1)
        m0 = lax.broadcasted_iota(jnp.int32, (1, LANES), 1) < HEAD_DIM
        strict, upper, lower = _attn_masks(B)
        strips = [slice(s, min(s + ATTN_STRIP, B)) for s in range(0, B, ATTN_STRIP)]
        cat = lambda xs: jnp.concatenate(xs, axis=0)

        @pl.when(i == 0)
        def _():
            dk_ref[...] = jnp.zeros_like(dk_ref)
            dv_ref[...] = jnp.zeros_like(dv_ref)

        q = q_ref[...] * 0.125
        zq = jnp.zeros_like(q)
        qh = (jnp.where(m0, q, zq), jnp.where(m0, zq, q))
        heads = lambda a: (jnp.where(m0, a, jnp.zeros_like(a)), jnp.where(m0, jnp.zeros_like(a), a))
        left_scr[...] = jnp.zeros_like(left_scr)
        ecum_scr[...] = jnp.zeros_like(ecum_scr)
        dq_ref[...] = jnp.zeros_like(dq_ref)

        def block_rows(j):
            return pl.ds(pl.multiple_of(j * B, B), B)

        def put_row_sums(ref, rows, h, rsum):
            if h == 0:
                ref[rows, :] = jnp.broadcast_to(rsum, (rows.stop - rows.start, LANES))
            else:
                ref[rows, :] = jnp.where(m0, ref[rows, :], rsum)

        def head_cols(x, h):
            other = pltpu.roll(x, HEAD_DIM, 1)
            full = jnp.where(m0, x, other) if h == 0 else jnp.where(m0, other, x)
            return jnp.concatenate([full] * (B // LANES), axis=1) if B > LANES else full

        def scores(j):
            kj = k_ref[block_rows(j), :]
            return tuple(_dot(qh[h], kj, NT) for h in (0, 1))

        def stage_a(j, zs2, diag):
            for h in (0, 1):
                his = []
                for rows in strips:
                    zs = zs2[h][rows]
                    lb = jnp.minimum(zs, 0.0) - _softplus_neg_abs(zs)
                    l1 = lb - zs
                    if diag:
                        l1 = jnp.where(strict[rows], l1, 0.0)
                    lb_scr[h][rows, :] = lb
                    his.append(l1.astype(BF16))
                    put_row_sums(bsum_scr, rows, h, jnp.sum(l1, axis=-1, keepdims=True))
                tail_scr[h][...] = _dot(cat(his), upper)
            bs = bsum_scr[...]
            scale = jnp.exp(tot_ref[...] - left_scr[...] - bs)
            left_scr[...] += bs
            dos = (do_ref[...] * scale).astype(BF16)
            dos_scr[...] = dos
            vj = v_ref[block_rows(j), :].astype(BF16)
            dosh = heads(dos)
            for h in (0, 1):
                dw_scr[h][...] = _dot(dosh[h], vj, NT)

        def stage_b(j, diag):
            dosh = heads(dos_scr[...])
            pres = []
            dv_blk = jnp.zeros((B, LANES), F32)
            for h in (0, 1):
                ehs, wbs = [], []
                for rows in strips:
                    lb = lb_scr[h][rows, :]
                    w = jnp.exp(lb + tail_scr[h][rows, :])
                    if diag:
                        w = jnp.where(strict[rows], w, 0.0)
                    beta_scr[h][rows, :] = jnp.exp(lb)
                    e = dw_scr[h][rows, :] * w
                    e_scr[h][rows, :] = e
                    ehs.append(e.astype(BF16))
                    wbs.append(w.astype(BF16))
                    put_row_sums(esum_scr, rows, h, jnp.sum(e, axis=-1, keepdims=True))
                pres.append(_dot(cat(ehs), lower))
                dv_blk = dv_blk + _dot(cat(wbs), dosh[h], TN)
            dv_ref[block_rows(j), :] += dv_blk
            return pres

        def stage_c(j, pres, diag):
            kh = heads(k_ref[block_rows(j), :])
            dq = jnp.zeros((B, LANES), F32)
            dk_blk = jnp.zeros((B, LANES), F32)
            for h in (0, 1):
                dzs = []
                for rows in strips:
                    e = e_scr[h][rows, :]
                    dl1 = pres[h][rows] + head_cols(ecum_scr[rows, :], h)
                    dz = e - beta_scr[h][rows, :] * (e + dl1)
                    if diag:
                        dz = jnp.where(strict[rows], dz, 0.0)
                    dzs.append(dz.astype(BF16))
                dzb = cat(dzs)
                dq = dq + _dot(dzb, kh[h])
                dk_blk = dk_blk + _dot(dzb, qh[h], TN)
            dq_ref[...] += dq
            dk_ref[block_rows(j), :] += dk_blk
            ecum_scr[...] += esum_scr[...]

        def step(t, diag_next):
            z = scores(t + 1)
            pres = stage_b(t, False)
            stage_a(t + 1, z, diag_next)
            stage_c(t, pres, False)

        @pl.when(i > 0)
        def _():
            stage_a(0, scores(0), False)

            @pl.loop(0, i - 1)
            def _(t):
                step(t, False)

            step(i - 1, True)

        @pl.when(i == 0)
        def _():
            stage_a(0, scores(0), True)

        stage_c(i, stage_b(i, True), True)
        dq_ref[...] = dq_ref[...] * 0.125

    qspec = pl.BlockSpec((B, LANES), lambda hp, i: (i, hp))
    kspec = pl.BlockSpec((L, LANES), lambda hp, i: (0, HP + hp))
    vspec = pl.BlockSpec((L, LANES), lambda hp, i: (0, 2 * HP + hp))
    full = pl.BlockSpec((L, LANES), lambda hp, i: (0, hp))
    shp = jax.ShapeDtypeStruct((L, AW), F32)
    return _pcall(body, name="attn_bwd", grid=(HP, nb), in_specs=[qspec, kspec, vspec, qspec, qspec],
                  out_specs=[qspec, full, full], out_shape=[shp, shp, shp],
                  scratch=[pltpu.VMEM((B, B), F32)] * 10 + [pltpu.VMEM((B, LANES), BF16)]
                  + [pltpu.VMEM((B, LANES), F32)] * 4)(qk, qk, p, do, tot)


def _s5_disc(ar, ai, ldt):
    dt = jnp.exp(ldt)
    mag = jnp.exp(dt * ar)
    abr = mag * jnp.cos(dt * ai)
    abi = mag * jnp.sin(dt * ai)
    emr = abr - 1.0
    emi = abi
    den = ar * ar + ai * ai
    fr = (emr * ar + emi * ai) / den
    fi = (emi * ar - emr * ai) / den
    return abr, abi, fr, fi


def _s5_params_math(ar, ai, ldt, br, bi):
    abr, abi, fr, fi = _s5_disc(ar, ai, ldt)
    return abr, abi, fr * br - fi * bi, fr * bi + fi * br


def _cmul_add(xr, xi, kr, ki, sr, si):
    return xr + (kr * sr - ki * si), xi + (kr * si + ki * sr)


def _s5_tab_rows(T):
    ng = T // 8
    return dict(NG=ng, POW=0, PA=ng, PAR=ng + 8, LA=ng + 16, ROWS=ng + 24)


def _s5_prep(ar, ai, ldt, braw_r, braw_i, T):
    NL, _, NS = ar.shape
    GT = NS // STATE_TILE
    R = _s5_tab_rows(T)

    def body(ar_ref, ai_ref, ldt_ref, br_ref, bi_ref, abr_ref, abi_ref, tabr_ref, tabi_ref, bsr_ref, bsi_ref):
        for t in range(GT):
            sl = slice(t * STATE_TILE, (t + 1) * STATE_TILE)
            abr, abi, bsr, bsi = _s5_params_math(ar_ref[:, sl], ai_ref[:, sl], ldt_ref[:, sl],
                                                 br_ref[t], bi_ref[t])
            abr_ref[:, sl] = abr
            abi_ref[:, sl] = abi
            bsr_ref[t] = bsr.astype(BF16)
            bsi_ref[t] = bsi.astype(BF16)

            def put(row, vr, vi):
                tabr_ref[row:row + 1, sl] = vr
                tabi_ref[row:row + 1, sl] = vi

            pr, pi = abr, abi
            for g in range(R["NG"]):
                put(R["POW"] + g, pr, pi)
                if g + 1 < R["NG"]:
                    pr, pi = pr * abr - pi * abi, pr * abi + pi * abr
            big_r, big_i = pr, pi
            qr, qi = big_r, big_i
            for r in range(8):
                put(R["PA"] + r, qr, qi)
                put(R["PAR"] + 7 - r, qr, qi)
                qr, qi = qr * big_r - qi * big_i, qr * big_i + qi * big_r
            qr, qi = big_r, big_i
            for k in range(3):
                put(R["LA"] + k, qr, qi)
                qr, qi = qr * qr - qi * qi, 2.0 * qr * qi
            for k in range(3, 8):
                put(R["LA"] + k, jnp.zeros_like(qr), jnp.zeros_like(qi))

    rowspec = pl.BlockSpec((None, 1, NS), lambda l: (l, 0, 0))
    bspec = pl.BlockSpec((None, GT, LANES, STATE_TILE), lambda l: (l, 0, 0, 0))
    tabspec = pl.BlockSpec((None, R["ROWS"], NS), lambda l: (l, 0, 0))
    rs = jax.ShapeDtypeStruct((NL, 1, NS), F32)
    ts = jax.ShapeDtypeStruct((NL, R["ROWS"], NS), F32)
    bs = jax.ShapeDtypeStruct((NL, GT, LANES, STATE_TILE), BF16)
    return _pcall(body, name="s5_prep", grid=(NL,), in_specs=[rowspec] * 3 + [bspec] * 2,
                  out_specs=[rowspec, rowspec, tabspec, tabspec, bspec, bspec],
                  out_shape=[rs, rs, ts, ts, bs, bs])(ar, ai, ldt, braw_r, braw_i)


def _s5_prep_bwd(ar, ai, ldt, braw_r, braw_i, dabr, dabi, dbsr, dbsi):
    NL, _, NS = ar.shape
    GT = NS // STATE_TILE

    def body(ar_ref, ai_ref, ldt_ref, br_ref, bi_ref, dabr_ref, dabi_ref, dbsr_ref, dbsi_ref,
             dar_ref, dai_ref, dldt_ref, dbr_ref, dbi_ref):
        for t in range(GT):
            sl = slice(t * STATE_TILE, (t + 1) * STATE_TILE)
            _, vjp = jax.vjp(_s5_params_math, ar_ref[:, sl], ai_ref[:, sl], ldt_ref[:, sl],
                             br_ref[t], bi_ref[t])
            dabr_row = jnp.sum(dabr_ref[:, sl], axis=0, keepdims=True)
            dabi_row = jnp.sum(dabi_ref[:, sl], axis=0, keepdims=True)
            dar, dai, dldt, dbr, dbi = vjp((dabr_row, dabi_row, dbsr_ref[t], dbsi_ref[t]))
            dar_ref[:, sl] = dar
            dai_ref[:, sl] = dai
            dldt_ref[:, sl] = dldt
            dbr_ref[t] = dbr
            dbi_ref[t] = dbi

    rowspec = pl.BlockSpec((None, 1, NS), lambda l: (l, 0, 0))
    row8spec = pl.BlockSpec((None, 8, NS), lambda l: (l, 0, 0))
    bspec = pl.BlockSpec((None, GT, LANES, STATE_TILE), lambda l: (l, 0, 0, 0))
    rs = jax.ShapeDtypeStruct((NL, 1, NS), F32)
    bs = jax.ShapeDtypeStruct((NL, GT, LANES, STATE_TILE), F32)
    return _pcall(body, name="s5_prep_bwd", grid=(NL,), in_specs=[rowspec] * 3 + [bspec] * 2 + [row8spec] * 2 + [bspec] * 2,
                  out_specs=[rowspec] * 3 + [bspec] * 2, out_shape=[rs, rs, rs, bs, bs])(
                      ar, ai, ldt, braw_r, braw_i, dabr, dabi, dbsr, dbsi)


def _to_streams(x, T):
    L, C = x.shape
    return x.reshape(L // T, 8, T // 8, C).transpose(0, 2, 1, 3).reshape(L, C)


def _from_streams(x, T):
    L, C = x.shape
    return x.reshape(L // T, T // 8, 8, C).transpose(0, 2, 1, 3).reshape(L, C)


def _s5_specs(L, SW, T, rev):
    GT = SW // LANES
    nc = L // T
    cidx = (lambda c: nc - 1 - c) if rev else (lambda c: c)
    rows = _s5_tab_rows(T)["ROWS"]
    return dict(
        GT=GT, nc=nc, cidx=cidx,
        chan=pl.BlockSpec((T, LANES), lambda j, c: (cidx(c), j)),
        state=pl.BlockSpec((T, STATE_TILE), lambda j, c: (cidx(c), j)),
        bmat=pl.BlockSpec((None, LANES, STATE_TILE), lambda j, c: (j, 0, 0)),
        cmat=pl.BlockSpec((None, STATE_TILE, LANES), lambda j, c: (j, 0, 0)),
        gmat=pl.BlockSpec((None, LANES, LANES), lambda j, c: (j, 0, 0)),
        cvec=pl.BlockSpec((1, LANES), lambda j, c: (0, j)),
        svec8=pl.BlockSpec((8, STATE_TILE), lambda j, c: (0, j)),
        tab=pl.BlockSpec((rows, STATE_TILE), lambda j, c: (0, j)),
    )


def _s5_fwd(u, sp, T):
    L, SW = u.shape
    S = _s5_specs(L, SW, T, False)
    NS = S["GT"] * STATE_TILE
    R = _s5_tab_rows(T)
    NG = R["NG"]

    def body(u_ref, bsr_ref, bsi_ref, tabr_ref, tabi_ref, cr_ref, ci_ref, d_ref, wg_ref, gb_ref,
             o_ref, sr_ref, si_ref, y_ref, carr_ref, cari_ref):
        @pl.when(pl.program_id(1) == 0)
        def _():
            carr_ref[...] = jnp.zeros_like(carr_ref)
            cari_ref[...] = jnp.zeros_like(cari_ref)

        uv = u_ref[...]
        ub = uv.astype(BF16)
        row8 = lax.broadcasted_iota(jnp.int32, (8, LANES), 0)
        nstrips = STATE_TILE // LANES

        def project(s):
            sl = slice(s * LANES, (s + 1) * LANES)
            return _dot(ub, bsr_ref[:, sl]), _dot(ub, bsi_ref[:, sl])

        ahead = project(0)
        for s in range(nstrips):
            sl = slice(s * LANES, (s + 1) * LANES)
            tab = lambda r0, n=1, sl=sl: (tabr_ref[r0:r0 + n, sl], tabi_ref[r0:r0 + n, sl])
            xr, xi = ahead
            if s + 1 < nstrips:
                ahead = project(s + 1)
            ar, ai = tab(R["POW"])
            lr, li = [xr[0:8]], [xi[0:8]]
            for g in range(1, NG):
                nr, ni = _cmul_add(xr[8 * g:8 * g + 8], xi[8 * g:8 * g + 8], ar, ai, lr[-1], li[-1])
                lr.append(nr)
                li.append(ni)
            yr, yi = lr[-1], li[-1]
            for k, dist in enumerate((1, 2, 4)):
                kr, ki = tab(R["LA"] + k)
                keep = row8 >= dist
                yr, yi = _cmul_add(yr, yi, kr, ki, jnp.where(keep, pltpu.roll(yr, dist, 0), 0.0),
                                   jnp.where(keep, pltpu.roll(yi, dist, 0), 0.0))
            c0r, c0i = carr_ref[7:8, sl], cari_ref[7:8, sl]
            par, pai = tab(R["PA"], 8)
            er, ei = _cmul_add(yr, yi, par, pai, c0r, c0i)
            carr_ref[:, sl] = er
            cari_ref[:, sl] = ei
            first = row8 == 0
            inr = jnp.where(first, c0r, pltpu.roll(er, 1, 0))
            ini = jnp.where(first, c0i, pltpu.roll(ei, 1, 0))
            for g in range(NG):
                pr, pi = tab(R["POW"] + g)
                outr, outi = _cmul_add(lr[g], li[g], pr, pi, inr, ini)
                sr_ref[8 * g:8 * g + 8, sl] = outr
                si_ref[8 * g:8 * g + 8, sl] = outi
        y = (_dot(sr_ref[...].astype(BF16), cr_ref[...]) - _dot(si_ref[...].astype(BF16), ci_ref[...])
             + d_ref[...] * uv)
        y_ref[...] = y
        yg = jax.nn.gelu(y)
        gate = jax.nn.sigmoid(_dot(yg.astype(BF16), wg_ref[...]) + gb_ref[...])
        o_ref[...] = yg * gate

    cs = jax.ShapeDtypeStruct((L, SW), F32)
    ss = jax.ShapeDtypeStruct((L, NS), F32)
    return _pcall(
        body, name="s5_fwd", grid=(S["GT"], S["nc"]),
        in_specs=[S["chan"], S["bmat"], S["bmat"], S["tab"], S["tab"], S["cmat"], S["cmat"],
                  S["cvec"], S["gmat"], S["cvec"]],
        out_specs=[S["chan"], S["state"], S["state"], S["chan"]], out_shape=[cs, ss, ss, cs],
        scratch=[pltpu.VMEM((8, STATE_TILE), F32), pltpu.VMEM((8, STATE_TILE), F32)],
    )(u, sp["bsr"], sp["bsi"], sp["tabr"], sp["tabi"], sp["crT"], sp["ciT"], sp["d"], sp["wg"], sp["gb"])


def _s5_bwd(do, u, s_re, s_im, ypre, sp, T):
    L, SW = u.shape
    S = _s5_specs(L, SW, T, True)
    GT, nc, cidx = S["GT"], S["nc"], S["cidx"]
    NS = GT * STATE_TILE
    R = _s5_tab_rows(T)
    NG = R["NG"]
    T8 = T // 8

    def body(do_ref, u_ref, sr_ref, si_ref, hr_ref, hi_ref, y_ref, bsr_ref, bsi_ref, tabr_ref, tabi_ref,
             cr_ref, ci_ref, d_ref, wg_ref, gb_ref,
             du_ref, dbsr_ref, dbsi_ref, dcr_ref, dci_ref, dd_ref, dwg_ref, dgb_ref, dar_ref, dai_ref,
             carr_ref, cari_ref, lam_r, lam_i):
        c = pl.program_id(1)
        first = c == 0

        @pl.when(first)
        def _():
            carr_ref[...] = jnp.zeros_like(carr_ref)
            cari_ref[...] = jnp.zeros_like(cari_ref)

        u = u_ref[...]
        ub = u.astype(BF16)
        y = y_ref[...]
        yg, gelu_vjp = jax.vjp(jax.nn.gelu, y)
        ygb = yg.astype(BF16)
        gate = jax.nn.sigmoid(_dot(ygb, wg_ref[...]) + gb_ref[...])
        dout = do_ref[...]
        dt = dout * yg * gate * (1.0 - gate)
        dtb = dt.astype(BF16)
        dyg = dout * gate + _dot(dtb, wg_ref[...], NT)
        (dy,) = gelu_vjp(dyg)
        dyb = dy.astype(BF16)
        _accumulate(dwg_ref, _dot(ygb, dtb, TN), first)
        _accumulate(dgb_ref, jnp.sum(dt, axis=0, keepdims=True), first)
        _accumulate(dd_ref, jnp.sum(dy * u, axis=0, keepdims=True), first)
        _accumulate(dcr_ref, _dot(sr_ref[...].astype(BF16), dyb, TN), first)
        _accumulate(dci_ref, -_dot(si_ref[...].astype(BF16), dyb, TN), first)
        earliest = cidx(c) == 0
        row8 = lax.broadcasted_iota(jnp.int32, (8, LANES), 0)
        nstrips = STATE_TILE // LANES

        def project(s):
            sl = slice(s * LANES, (s + 1) * LANES)
            return _dot(dyb, cr_ref[sl, :], NT), -_dot(dyb, ci_ref[sl, :], NT)

        ahead = project(0)
        for s in range(nstrips):
            sl = slice(s * LANES, (s + 1) * LANES)
            tab = lambda r0, n=1, sl=sl: (tabr_ref[r0:r0 + n, sl], -tabi_ref[r0:r0 + n, sl])
            xr, xi = ahead
            if s + 1 < nstrips:
                ahead = project(s + 1)
            ar, ai = tab(R["POW"])
            lr, li = [None] * NG, [None] * NG
            lr[NG - 1], li[NG - 1] = xr[8 * (NG - 1):8 * NG], xi[8 * (NG - 1):8 * NG]
            for g in range(NG - 2, -1, -1):
                lr[g], li[g] = _cmul_add(xr[8 * g:8 * g + 8], xi[8 * g:8 * g + 8], ar, ai, lr[g + 1], li[g + 1])
            yr, yi = lr[0], li[0]
            for k, dist in enumerate((1, 2, 4)):
                kr, ki = tab(R["LA"] + k)
                keep = row8 < 8 - dist
                yr, yi = _cmul_add(yr, yi, kr, ki, jnp.where(keep, pltpu.roll(yr, 8 - dist, 0), 0.0),
                                   jnp.where(keep, pltpu.roll(yi, 8 - dist, 0), 0.0))
            c0r, c0i = carr_ref[0:1, sl], cari_ref[0:1, sl]
            par, pai = tab(R["PAR"], 8)
            er, ei = _cmul_add(yr, yi, par, pai, c0r, c0i)
            carr_ref[:, sl] = er
            cari_ref[:, sl] = ei
            last = row8 == 7
            inr = jnp.where(last, c0r, pltpu.roll(er, 7, 0))
            ini = jnp.where(last, c0i, pltpu.roll(ei, 7, 0))
            hr0 = jnp.where(earliest, 0.0, hr_ref[7:8, sl])
            hi0 = jnp.where(earliest, 0.0, hi_ref[7:8, sl])
            endr, endi = sr_ref[8 * (NG - 1):8 * NG, sl], si_ref[8 * (NG - 1):8 * NG, sl]
            pvr = jnp.where(row8 == 0, hr0, pltpu.roll(endr, 1, 0))
            pvi = jnp.where(row8 == 0, hi0, pltpu.roll(endi, 1, 0))
            accr = jnp.zeros((8, LANES), F32)
            acci = jnp.zeros((8, LANES), F32)
            for g in range(NG):
                pr, pi = tab(R["POW"] + NG - 1 - g)
                outr, outi = _cmul_add(lr[g], li[g], pr, pi, inr, ini)
                lam_r[8 * g:8 * g + 8, sl] = outr
                lam_i[8 * g:8 * g + 8, sl] = outi
                accr = accr + (outr * pvr + outi * pvi)
                acci = acci + (outi * pvr - outr * pvi)
                pvr, pvi = sr_ref[8 * g:8 * g + 8, sl], si_ref[8 * g:8 * g + 8, sl]

            @pl.when(first)
            def _():
                dar_ref[:, sl] = accr
                dai_ref[:, sl] = acci

            @pl.when(jnp.logical_not(first))
            def _():
                dar_ref[:, sl] += accr
                dai_ref[:, sl] += acci

        lrb = lam_r[...].astype(BF16)
        lib = lam_i[...].astype(BF16)
        _accumulate(dbsr_ref, _dot(ub, lrb, TN), first)
        _accumulate(dbsi_ref, _dot(ub, lib, TN), first)
        du_ref[...] = (dy * d_ref[...] + _dot(lrb, bsr_ref[...], NT) + _dot(lib, bsi_ref[...], NT)).astype(BF16)

    halo = pl.BlockSpec((8, STATE_TILE), lambda j, c: (jnp.maximum(cidx(c) * T8 - 1, 0), j))
    f = lambda *s: jax.ShapeDtypeStruct(s, F32)
    return _pcall(
        body, name="s5_bwd", grid=(GT, nc),
        in_specs=[S["chan"], S["chan"], S["state"], S["state"], halo, halo, S["chan"], S["bmat"], S["bmat"],
                  S["tab"], S["tab"], S["cmat"], S["cmat"], S["cvec"], S["gmat"], S["cvec"]],
        out_specs=[S["chan"], S["bmat"], S["bmat"], S["cmat"], S["cmat"], S["cvec"], S["gmat"], S["cvec"],
                   S["svec8"], S["svec8"]],
        out_shape=[jax.ShapeDtypeStruct((L, SW), BF16), f(GT, LANES, STATE_TILE), f(GT, LANES, STATE_TILE), f(GT, STATE_TILE, LANES),
                   f(GT, STATE_TILE, LANES), f(1, SW), f(GT, LANES, LANES), f(1, SW), f(8, NS), f(8, NS)],
        scratch=[pltpu.VMEM((8, STATE_TILE), F32), pltpu.VMEM((8, STATE_TILE), F32),
                 pltpu.VMEM((T, STATE_TILE), F32), pltpu.VMEM((T, STATE_TILE), F32)],
    )(do, u, s_re, s_im, s_re, s_im, ypre, sp["bsr"], sp["bsi"], sp["tabr"], sp["tabi"],
      sp["crT"], sp["ciT"], sp["d"], sp["wg"], sp["gb"])


def _conv_taps(xc, h6, h7, row):
    x1 = jnp.where(row == 0, h7, pltpu.roll(xc, 1, 0))
    x2 = jnp.where(row == 0, h6, jnp.where(row == 1, h7, pltpu.roll(xc, 2, 0)))
    return x1, x2


def _conv_halves(up_ref, halo_ref, w_ref, b_ref, tm, FS, first):
    row = lax.broadcasted_iota(jnp.int32, (tm, FS), 0)
    outs, taps = [], []
    for s in (0, 1):
        xc = up_ref[s]
        h6 = jnp.where(first, 0.0, halo_ref[s, 6:7, :])
        h7 = jnp.where(first, 0.0, halo_ref[s, 7:8, :])
        x1, x2 = _conv_taps(xc, h6, h7, row)
        outs.append(b_ref[s] + x2 * w_ref[s, 0:1, :] + x1 * w_ref[s, 1:2, :] + xc * w_ref[s, 2:3, :])
        taps.append((x2, x1, xc))
    return outs, taps


def _convglu_specs(L, FS, tm, rev=False):
    t8 = tm // 8
    nt = L // tm
    tile = (lambda i: nt - 1 - i) if rev else (lambda i: i)
    return dict(
        up=pl.BlockSpec((2, None, tm, FS), lambda j, i: (0, j, tile(i), 0)),
        halo=pl.BlockSpec((2, None, 8, FS), lambda j, i: (0, j, jnp.maximum(tile(i) * t8 - 1, 0), 0)),
        w=pl.BlockSpec((2, None, 3, FS), lambda j, i: (0, j, 0, 0)),
        b=pl.BlockSpec((2, None, 1, FS), lambda j, i: (0, j, 0, 0)),
        act=pl.BlockSpec((None, tm, FS), lambda j, i: (j, tile(i), 0)),
    )


def _convglu_fwd(up, w, b):
    _, NSH, L, FS = up.shape
    tm = _row_tile(L)
    S = _convglu_specs(L, FS, tm)

    def body(up_ref, halo_ref, w_ref, b_ref, act_ref):
        (val, gate), _ = _conv_halves(up_ref, halo_ref, w_ref, b_ref, tm, FS, pl.program_id(1) == 0)
        act_ref[...] = _glu_math(val, gate).astype(BF16)

    return _pcall(body, name="convglu_fwd", grid=(NSH, L // tm), in_specs=[S["up"], S["halo"], S["w"], S["b"]],
                  out_specs=S["act"], out_shape=jax.ShapeDtypeStruct((NSH, L, FS), BF16))(up, up, w, b)


def _convglu_bwd(up, dact, w, b):
    _, NSH, L, FS = up.shape
    tm = _row_tile(L)
    nt = L // tm
    S = _convglu_specs(L, FS, tm, rev=True)

    def body(up_ref, halo_ref, w_ref, b_ref, dact_ref, dup_ref, dw_ref, db_ref, after_scr):
        step = pl.program_id(1)
        first = step == 0
        (val, gate), taps = _conv_halves(up_ref, halo_ref, w_ref, b_ref, tm, FS, step == nt - 1)
        _, vjp = jax.vjp(_glu_math, val, gate)
        dcs = vjp(dact_ref[...])
        row = lax.broadcasted_iota(jnp.int32, (tm, FS), 0)
        for s in (0, 1):
            dc = dcs[s]
            n0 = jnp.where(first, 0.0, after_scr[s, 0:1, :])
            n1 = jnp.where(first, 0.0, after_scr[s, 1:2, :])
            x1 = jnp.where(row == tm - 1, n0, pltpu.roll(dc, tm - 1, 0))
            x2 = jnp.where(row == tm - 1, n1, jnp.where(row == tm - 2, n0, pltpu.roll(dc, tm - 2, 0)))
            dup_ref[s] = (dc * w_ref[s, 2:3, :] + x1 * w_ref[s, 1:2, :] + x2 * w_ref[s, 0:1, :]).astype(BF16)
            after_scr[s] = dc[0:8]
            sums = [jnp.sum(dc * t, axis=0, keepdims=True) for t in taps[s]]
            dbs = jnp.sum(dc, axis=0, keepdims=True)

            @pl.when(first)
            def _():
                for t in range(3):
                    dw_ref[s, t:t + 1, :] = sums[t]
                db_ref[s] = dbs

            @pl.when(jnp.logical_not(first))
            def _():
                for t in range(3):
                    dw_ref[s, t:t + 1, :] += sums[t]
                db_ref[s] += dbs

    f = lambda *s: jax.ShapeDtypeStruct(s, F32)
    return _pcall(body, name="convglu_bwd", grid=(NSH, nt),
                  in_specs=[S["up"], S["halo"], S["w"], S["b"], S["act"]],
                  out_specs=[S["up"], S["w"], S["b"]],
                  out_shape=[jax.ShapeDtypeStruct((2, NSH, L, FS), BF16), f(2, NSH, 3, FS), f(2, NSH, 1, FS)],
                  scratch=[pltpu.VMEM((2, 8, FS), F32)])(up, up, w, b, dact)


def _loss_head(y, target):
    L, D = y.shape
    tm = _row_tile(L)

    def body(y_ref, t_ref, loss_ref, dy_ref):
        err = y_ref[...] - t_ref[...]
        dy_ref[...] = err / D
        part = 0.5 * jnp.sum(jnp.mean(err * err, axis=-1, keepdims=True), axis=0, keepdims=True)
        _accumulate(loss_ref, jnp.broadcast_to(part, (1, LANES)), pl.program_id(0) == 0)

    row = pl.BlockSpec((tm, D), lambda i: (i, 0))
    vec = pl.BlockSpec((1, LANES), lambda i: (0, 0))
    return _pcall(body, name="loss_head", grid=(L // tm,), in_specs=[row, row], out_specs=[vec, row],
                  out_shape=[jax.ShapeDtypeStruct((1, LANES), F32), jax.ShapeDtypeStruct((L, D), F32)])(y, target)


def _ada_fwd(c_all, ada_w, ada_b):
    NL, D, NC = ada_w.shape
    NB = c_all.shape[0]

    def body(c_ref, w_ref, b_ref, o_ref):
        cact = jax.nn.silu(c_ref[...])
        o_ref[...] = _dot(cact.astype(BF16), w_ref[...].astype(BF16)) + b_ref[...]

    return _pcall(body, name="ada_fwd", grid=(NL,),
                  in_specs=[pl.BlockSpec((NB, D), lambda l: (0, 0)), pl.BlockSpec((None, D, NC), lambda l: (l, 0, 0)),
                            pl.BlockSpec((None, 1, NC), lambda l: (l, 0, 0))],
                  out_specs=pl.BlockSpec((None, NB, NC), lambda l: (l, 0, 0)),
                  out_shape=jax.ShapeDtypeStruct((NL, NB, NC), F32))(c_all, ada_w, ada_b)


def _ada_bwd(c_all_t, dmod):
    D, NB = c_all_t.shape
    NL, _, NC = dmod.shape

    def body(c_ref, d_ref, o_ref):
        cact = jax.nn.silu(c_ref[...]).astype(BF16).astype(F32)
        o_ref[...] = _dot(cact, d_ref[...].astype(BF16).astype(F32))

    return _pcall(body, name="ada_bwd", grid=(NL,),
                  in_specs=[pl.BlockSpec((D, NB), lambda l: (0, 0)), pl.BlockSpec((None, NB, NC), lambda l: (l, 0, 0))],
                  out_specs=pl.BlockSpec((None, D, NC), lambda l: (l, 0, 0)),
                  out_shape=jax.ShapeDtypeStruct((NL, D, NC), F32))(c_all_t, dmod)


def _adamw(parts, w, m, v):
    NL, P, R, C = parts.shape
    tr = R
    for cand in (256, 128, 64, 32, 16, 8):
        if R % cand == 0 and R > cand:
            tr = cand
            break

    def body(p_ref, w_ref, m_ref, v_ref, g_ref, d_ref, nm_ref, nv_ref):
        g = p_ref[0].astype(F32)
        for k in range(1, P):
            g = g + p_ref[k].astype(F32)
        m2 = ADAM_B1 * m_ref[...] + (1.0 - ADAM_B1) * g
        v2 = ADAM_B2 * v_ref[...] + (1.0 - ADAM_B2) * jnp.square(g)
        m_hat = m2 / (1.0 - ADAM_B1 ** ADAM_STEP)
        v_hat = v2 / (1.0 - ADAM_B2 ** ADAM_STEP)
        g_ref[...] = g
        d_ref[...] = -ADAM_LR * (m_hat / (jnp.sqrt(v_hat) + ADAM_EPS) + ADAM_WD * w_ref[...])
        nm_ref[...] = m2
        nv_ref[...] = v2

    pspec = pl.BlockSpec((None, P, tr, C), lambda l, i: (l, 0, i, 0))
    wspec = pl.BlockSpec((None, tr, C), lambda l, i: (l, i, 0))
    shp = jax.ShapeDtypeStruct((NL, R, C), F32)
    return _pcall(body, name="adamw", grid=(NL, R // tr), in_specs=[pspec, wspec, wspec, wspec],
                  out_specs=[wspec] * 4, out_shape=[shp] * 4)(parts, w, m, v)


def _block_diag(blocks):
    *lead, g, r, c = blocks.shape
    eye = jnp.eye(g, dtype=bool)[:, None, :, None]
    full = jnp.where(eye, blocks[..., :, :, None, :], 0.0)
    return full.reshape(*lead, g * r, g * c)


def _block_diag_extract(m, r, c):
    g = GROUPS_PER_TILE
    m5 = m.reshape(*m.shape[:-2], g, r, g, c)
    eye = jnp.eye(g, dtype=bool)[:, None, :, None]
    return jnp.sum(jnp.where(eye, m5, 0.0), axis=-2)


def kernel(x, c, ada_w, ada_b, norm1_g, w_in, q_norm_g, k_norm_g, ssm_a_re, ssm_a_im, ssm_log_dt, ssm_b_re, ssm_b_im, ssm_c_re, ssm_c_im, ssm_d, glu_w, glu_b, attn_out_g, ssm_out_g, w_out, norm2_g, ffn_w_up, ffn_conv_w, ffn_conv_b, ffn_w_down, loss_target, m_ada_w, m_ada_b, m_norm1_g, m_w_in, m_q_norm_g, m_k_norm_g, m_ssm_a_re, m_ssm_a_im, m_ssm_log_dt, m_ssm_b_re, m_ssm_b_im, m_ssm_c_re, m_ssm_c_im, m_ssm_d, m_glu_w, m_glu_b, m_attn_out_g, m_ssm_out_g, m_w_out, m_norm2_g, m_ffn_w_up, m_ffn_conv_w, m_ffn_conv_b, m_ffn_w_down, v_ada_w, v_ada_b, v_norm1_g, v_w_in, v_q_norm_g, v_k_norm_g, v_ssm_a_re, v_ssm_a_im, v_ssm_log_dt, v_ssm_b_re, v_ssm_b_im, v_ssm_c_re, v_ssm_c_im, v_ssm_d, v_glu_w, v_glu_b, v_attn_out_g, v_ssm_out_g, v_w_out, v_norm2_g, v_ffn_w_up, v_ffn_conv_w, v_ffn_conv_b, v_ffn_w_down):
    weights = dict(ada_w=ada_w, ada_b=ada_b, norm1_g=norm1_g, w_in=w_in, q_norm_g=q_norm_g, k_norm_g=k_norm_g,
                   ssm_a_re=ssm_a_re, ssm_a_im=ssm_a_im, ssm_log_dt=ssm_log_dt, ssm_b_re=ssm_b_re,
                   ssm_b_im=ssm_b_im, ssm_c_re=ssm_c_re, ssm_c_im=ssm_c_im, ssm_d=ssm_d, glu_w=glu_w, glu_b=glu_b,
                   attn_out_g=attn_out_g, ssm_out_g=ssm_out_g, w_out=w_out, norm2_g=norm2_g, ffn_w_up=ffn_w_up,
                   ffn_conv_w=ffn_conv_w, ffn_conv_b=ffn_conv_b, ffn_w_down=ffn_w_down)
    mom_m = dict(ada_w=m_ada_w, ada_b=m_ada_b, norm1_g=m_norm1_g, w_in=m_w_in, q_norm_g=m_q_norm_g,
                 k_norm_g=m_k_norm_g, ssm_a_re=m_ssm_a_re, ssm_a_im=m_ssm_a_im, ssm_log_dt=m_ssm_log_dt,
                 ssm_b_re=m_ssm_b_re, ssm_b_im=m_ssm_b_im, ssm_c_re=m_ssm_c_re, ssm_c_im=m_ssm_c_im, ssm_d=m_ssm_d,
                 glu_w=m_glu_w, glu_b=m_glu_b, attn_out_g=m_attn_out_g, ssm_out_g=m_ssm_out_g, w_out=m_w_out,
                 norm2_g=m_norm2_g, ffn_w_up=m_ffn_w_up, ffn_conv_w=m_ffn_conv_w, ffn_conv_b=m_ffn_conv_b,
                 ffn_w_down=m_ffn_w_down)
    mom_v = dict(ada_w=v_ada_w, ada_b=v_ada_b, norm1_g=v_norm1_g, w_in=v_w_in, q_norm_g=v_q_norm_g,
                 k_norm_g=v_k_norm_g, ssm_a_re=v_ssm_a_re, ssm_a_im=v_ssm_a_im, ssm_log_dt=v_ssm_log_dt,
                 ssm_b_re=v_ssm_b_re, ssm_b_im=v_ssm_b_im, ssm_c_re=v_ssm_c_re, ssm_c_im=v_ssm_c_im, ssm_d=v_ssm_d,
                 glu_w=v_glu_w, glu_b=v_glu_b, attn_out_g=v_attn_out_g, ssm_out_g=v_ssm_out_g, w_out=v_w_out,
                 norm2_g=v_norm2_g, ffn_w_up=v_ffn_w_up, ffn_conv_w=v_ffn_conv_w, ffn_conv_b=v_ffn_conv_b,
                 ffn_w_down=v_ffn_w_down)
    names = list(weights)
    big = ("ada_w", "w_in", "w_out", "ffn_w_up", "ffn_conv_w", "ffn_w_down")
    small = [n for n in names if n not in big]

    x = x[0]
    target = loss_target[0]
    L, D = x.shape
    NL = ada_w.shape[0]
    AW = D // 2
    SW = D - AW
    NH = AW // HEAD_DIM
    HP = AW // LANES
    G = SW // SSM_GROUP
    GT = SW // LANES
    NS = G * SSM_STATE
    NIN = w_in.shape[-1]
    FS = ffn_w_up.shape[-1]
    NSH = NDEV // 2
    NCA = ada_w.shape[-1]
    ROWS_OUT = w_out.shape[1]
    ROWS_DOWN = ffn_w_down.shape[1]
    B_ATT = min(L, 256)
    T_S5 = min(L, 512)
    tm = _row_tile(L, 1024)
    me = _my_index()

    cpad = jnp.reshape(c, (D // LANES, LANES))
    c_all = _gather_small(cpad, "gather_c")
    c_all = c_all.reshape(NDEV, D)
    w_in_g, w_out_g, w_up_g, conv_w_g, w_down_g = _gather_shards(
        [w_in.astype(BF16), w_out.astype(BF16), ffn_w_up.astype(BF16), ffn_conv_w, ffn_w_down.astype(BF16)],
        "gather_weights")
    w_out_g = w_out_g.reshape(NL, D, D)
    w_down_g = w_down_g.reshape(NL, NSH, 2 * ROWS_DOWN, D)
    conv_w_g = conv_w_g.reshape(NL, 2, NSH, 3, FS)
    conv_b_g = ffn_conv_b.reshape(NL, 2, NSH, 1, FS)

    ada_b_mine = lax.dynamic_slice_in_dim(ada_b, me * NCA, NCA, axis=1).reshape(NL, 1, NCA)
    mod_part = _ada_fwd(c_all, ada_w, ada_b_mine)
    mod_all = _gather_small(mod_part, "gather_mod")
    mod = lax.dynamic_index_in_dim(mod_all, me, axis=2, keepdims=False)
    mod = jnp.transpose(mod, (1, 0, 2)).reshape(NL, N_MOD, 1, D)

    row = lambda a: a.reshape(NL, 1, NS)
    ar_row, ai_row = row(ssm_a_re), row(ssm_a_im)
    ldt_row = row(jnp.broadcast_to(ssm_log_dt[:, :, None], (NL, G, SSM_STATE)))
    tiles = lambda a: a.reshape((NL, GT, GROUPS_PER_TILE) + a.shape[2:])
    braw_r = _block_diag(jnp.swapaxes(tiles(ssm_b_re), -1, -2))
    braw_i = _block_diag(jnp.swapaxes(tiles(ssm_b_im), -1, -2))
    crT = _block_diag(jnp.swapaxes(tiles(ssm_c_re), -1, -2)).astype(BF16)
    ciT = _block_diag(jnp.swapaxes(tiles(ssm_c_im), -1, -2)).astype(BF16)
    wg = _block_diag(tiles(glu_w)).astype(BF16)
    abr, abi, tabr, tabi, bsr, bsi = _s5_prep(ar_row, ai_row, ldt_row, braw_r, braw_i, T_S5)
    s5p = dict(bsr=bsr, bsi=bsi, tabr=tabr, tabi=tabi, crT=crT, ciT=ciT,
               d=ssm_d.reshape(NL, 1, SW), wg=wg, gb=glu_b.reshape(NL, 1, SW))

    gqk = jnp.concatenate([jnp.tile(q_norm_g, (1, NH)), jnp.tile(k_norm_g, (1, NH))], axis=1).reshape(NL, 1, 2 * AW)
    layer_params = dict(
        mod=mod, norm1_g=norm1_g.reshape(NL, 1, D), norm2_g=norm2_g.reshape(NL, 1, D), gqk=gqk,
        ga=attn_out_g.reshape(NL, 1, AW), gs=ssm_out_g.reshape(NL, 1, SW),
        w_in=w_in_g, w_out=w_out_g, w_up=w_up_g, conv_w=conv_w_g, conv_b=conv_b_g, w_down=w_down_g, s5=s5p)


    def mm_rows_shards(name, a, b, n, out_dtype=F32):
        K = a.shape[1]
        return _matmul(name, a, b, dims=NN, grid=(L // tm, NDEV),
                       a_spec=pl.BlockSpec((tm, K), lambda i, j: (i, 0)),
                       b_spec=pl.BlockSpec((None, K, n), lambda i, j: (j, 0, 0)),
                       out_shape=jax.ShapeDtypeStruct((L, NDEV * n), out_dtype),
                       out_specs=pl.BlockSpec((tm, n), lambda i, j: (i, j)))

    tn = min(D, 512)

    def resid_epilogue(acc, xres, gate):
        return acc, xres + gate * acc

    def layer_fwd(xin, lp):
        sh1, sc1, g1, sh2, sc2, g2 = (lp["mod"][k] for k in range(N_MOD))
        h = _lnmod_fwd(xin, lp["norm1_g"], sh1, sc1)
        p = mm_rows_shards("mm_in", h, lp["w_in"], NIN)
        qk = _qknorm_fwd(p, lp["gqk"], AW)
        o_attn, tot = _attn_fwd(qk, p, AW, B_ATT)
        u_st = _to_streams(p[:, 3 * AW:], T_S5)
        o_st, s_re, s_im, ypre = _s5_fwd(u_st, lp["s5"], T_S5)
        o_ssm = _from_streams(o_st, T_S5)
        o = _outnorm_fwd(o_attn, o_ssm, lp["ga"], lp["gs"])
        a1, x_mid = _matmul(
            "mm_out", o, lp["w_out"], dims=NN, grid=(L // tm, D // tn),
            a_spec=pl.BlockSpec((tm, D), lambda i, j: (i, 0)), b_spec=pl.BlockSpec((D, tn), lambda i, j: (0, j)),
            extra=(xin, g1), extra_specs=(pl.BlockSpec((tm, tn), lambda i, j: (i, j)),
                                          pl.BlockSpec((1, tn), lambda i, j: (0, j))),
            epilogue=resid_epilogue,
            out_shape=[jax.ShapeDtypeStruct((L, D), F32)] * 2,
            out_specs=[pl.BlockSpec((tm, tn), lambda i, j: (i, j))] * 2)
        h2 = _lnmod_fwd(x_mid, lp["norm2_g"], sh2, sc2)
        up = _matmul(
            "mm_up", h2, lp["w_up"], dims=NN, grid=(L // tm, NDEV),
            a_spec=pl.BlockSpec((tm, D), lambda i, j: (i, 0)), b_spec=pl.BlockSpec((None, D, FS), lambda i, j: (j, 0, 0)),
            out_shape=jax.ShapeDtypeStruct((NDEV, L, FS), F32),
            out_specs=pl.BlockSpec((None, tm, FS), lambda i, j: (j, i, 0)))
        up = up.reshape(2, NSH, L, FS)
        act = _convglu_fwd(up, lp["conv_w"], lp["conv_b"])
        a2, x_out = _matmul(
            "mm_down", act, lp["w_down"], dims=NN, grid=(L // tm, D // tn, NSH), kaxis=2, acc_shape=(tm, tn),
            a_spec=pl.BlockSpec((None, tm, FS), lambda i, j, k: (k, i, 0)),
            b_spec=pl.BlockSpec((None, FS, tn), lambda i, j, k: (k, 0, j)),
            extra=(x_mid, g2), extra_specs=(pl.BlockSpec((tm, tn), lambda i, j, k: (i, j)),
                                            pl.BlockSpec((1, tn), lambda i, j, k: (0, j))),
            epilogue=resid_epilogue,
            out_shape=[jax.ShapeDtypeStruct((L, D), F32)] * 2,
            out_specs=[pl.BlockSpec((tm, tn), lambda i, j, k: (i, j))] * 2)
        res = dict(x=xin, h=h, p=p, qk=qk, tot=tot, o_attn=o_attn, o_ssm=o_ssm, u_st=u_st, s_re=s_re, s_im=s_im, ypre=ypre,
                   o=o, a1=a1, x_mid=x_mid, h2=h2, up=up, act=act, a2=a2)
        return x_out, res

    per_layer = [jax.tree.map(lambda a: a[l], layer_params) for l in range(NL)]
    y, residuals = x, []
    for l in range(NL):
        y, res = layer_fwd(y, per_layer[l])
        residuals.append(res)

    loss_row, dy = _loss_head(y, target)
    loss = lax.psum(loss_row[0, 0], ("x", "y", "c"))

    def layer_bwd(dx, args):
        lp, r = args
        sh1, sc1, g1, sh2, sc2, g2 = (lp["mod"][k] for k in range(N_MOD))
        da2, dg2 = _gate_bwd(dx, r["a2"], g2)
        dact = _matmul(
            "mm_dact", da2, lp["w_down"], dims=NT, grid=(L // tm, NSH),
            a_spec=pl.BlockSpec((tm, D), lambda i, j: (i, 0)), b_spec=pl.BlockSpec((None, FS, D), lambda i, j: (j, 0, 0)),
            out_shape=jax.ShapeDtypeStruct((NSH, L, FS), F32),
            out_specs=pl.BlockSpec((None, tm, FS), lambda i, j: (j, i, 0)))
        dw_down = _matmul(
            "mm_dw_down", r["act"], da2, dims=TN, grid=(NSH, D // tn, L // tm), kaxis=2, acc_shape=(FS, tn),
            a_spec=pl.BlockSpec((None, tm, FS), lambda j, n, k: (j, k, 0)),
            b_spec=pl.BlockSpec((tm, tn), lambda j, n, k: (k, n)),
            out_shape=jax.ShapeDtypeStruct((NSH, FS, D), BF16),
            out_specs=pl.BlockSpec((None, FS, tn), lambda j, n, k: (j, 0, n)))
        dup4, dcw, dcb = _convglu_bwd(r["up"], dact, lp["conv_w"], lp["conv_b"])
        dup = dup4.reshape(NDEV, L, FS)
        dh2 = _matmul(
            "mm_dh2", dup, lp["w_up"], dims=NT, grid=(L // tm, NDEV), kaxis=1, acc_shape=(tm, D),
            a_spec=pl.BlockSpec((None, tm, FS), lambda i, k: (k, i, 0)),
            b_spec=pl.BlockSpec((None, D, FS), lambda i, k: (k, 0, 0)),
            out_shape=jax.ShapeDtypeStruct((L, D), F32), out_specs=pl.BlockSpec((tm, D), lambda i, k: (i, 0)))
        dw_up = _matmul(
            "mm_dw_up", r["h2"], dup, dims=TN, grid=(NDEV, L // tm), kaxis=1, acc_shape=(D, FS),
            a_spec=pl.BlockSpec((tm, D), lambda j, k: (k, 0)), b_spec=pl.BlockSpec((None, tm, FS), lambda j, k: (j, k, 0)),
            out_shape=jax.ShapeDtypeStruct((NDEV, D, FS), BF16),
            out_specs=pl.BlockSpec((None, D, FS), lambda j, k: (j, 0, 0)))
        dxm, dn2, dsh2, dsc2 = _lnmod_bwd(dh2, r["x_mid"], lp["norm2_g"], sh2, sc2, dx)
        da1, dg1 = _gate_bwd(dxm, r["a1"], g1)
        do = _matmul(
            "mm_do", da1, lp["w_out"], dims=NT, grid=(L // tm, D // tn),
            a_spec=pl.BlockSpec((tm, D), lambda i, j: (i, 0)), b_spec=pl.BlockSpec((tn, D), lambda i, j: (j, 0)),
            out_shape=jax.ShapeDtypeStruct((L, D), F32), out_specs=pl.BlockSpec((tm, tn), lambda i, j: (i, j)))
        dw_out = _matmul(
            "mm_dw_out", r["o"], da1, dims=TN, grid=(D // tn, D // tn, L // tm), kaxis=2, acc_shape=(tn, tn),
            a_spec=pl.BlockSpec((tm, tn), lambda m, n, k: (k, m)), b_spec=pl.BlockSpec((tm, tn), lambda m, n, k: (k, n)),
            out_shape=jax.ShapeDtypeStruct((D, D), BF16), out_specs=pl.BlockSpec((tn, tn), lambda m, n, k: (m, n)))
        doa, dos, dga, dgs = _outnorm_bwd(do, r["o_attn"], r["o_ssm"], lp["ga"], lp["gs"])
        (du_st, dbsr, dbsi, dcr, dci, dd, dwg, dgb, dabr, dabi) = _s5_bwd(
            _to_streams(dos, T_S5), r["u_st"], r["s_re"], r["s_im"], r["ypre"], lp["s5"], T_S5)
        du = _from_streams(du_st, T_S5)
        dq, dk, dv = _attn_bwd(r["qk"], r["p"], doa, r["tot"], AW, B_ATT)
        dqk, dgqk = _qknorm_bwd(jnp.concatenate([dq, dk], axis=1), r["p"], lp["gqk"], AW)
        dp = jnp.concatenate([dqk, dv.astype(BF16), du], axis=1)
        dh = _matmul(
            "mm_dh", dp, lp["w_in"], dims=NT, grid=(L // tm, NDEV), kaxis=1, acc_shape=(tm, D),
            a_spec=pl.BlockSpec((tm, NIN), lambda i, k: (i, k)), b_spec=pl.BlockSpec((None, D, NIN), lambda i, k: (k, 0, 0)),
            out_shape=jax.ShapeDtypeStruct((L, D), F32), out_specs=pl.BlockSpec((tm, D), lambda i, k: (i, 0)))
        dw_in = _matmul(
            "mm_dw_in", r["h"], dp, dims=TN, grid=(NDEV, L // tm), kaxis=1, acc_shape=(D, NIN),
            a_spec=pl.BlockSpec((tm, D), lambda j, k: (k, 0)), b_spec=pl.BlockSpec((tm, NIN), lambda j, k: (k, j)),
            out_shape=jax.ShapeDtypeStruct((NDEV, D, NIN), BF16),
            out_specs=pl.BlockSpec((None, D, NIN), lambda j, k: (j, 0, 0)))
        dx0, dn1, dsh1, dsc1 = _lnmod_bwd(dh, r["x"], lp["norm1_g"], sh1, sc1, dxm)
        grads = dict(
            dmod=jnp.concatenate([dsh1, dsc1, dg1, dsh2, dsc2, dg2], axis=1), dn1=dn1, dn2=dn2, dgqk=dgqk,
            dga=dga, dgs=dgs, dbsr=dbsr, dbsi=dbsi, dcr=dcr, dci=dci, dd=dd, dwg=dwg, dgb=dgb, dabr=dabr, dabi=dabi,
            dcb=dcb, dw_in=dw_in, dw_out=dw_out.reshape(NDEV, ROWS_OUT, D), dw_up=dw_up,
            dcw=dcw.reshape(NDEV, 3, FS), dw_down=dw_down.reshape(NDEV, ROWS_DOWN, D))
        return dx0, grads

    grad_x, layer_grads = dy, [None] * NL
    for l in reversed(range(NL)):
        grad_x, layer_grads[l] = layer_bwd(grad_x, (per_layer[l], residuals[l]))
    gr = jax.tree.map(lambda *a: jnp.stack(a), *layer_grads)

    dar, dai, dldt, dbr_bd, dbi_bd = _s5_prep_bwd(ar_row, ai_row, ldt_row, braw_r, braw_i,
                                                  gr["dabr"], gr["dabi"], gr["dbsr"], gr["dbsi"])
    unt = lambda a: a.reshape((NL, G) + a.shape[3:])
    local = dict(
        ada_b=gr["dmod"].reshape(NL, N_MOD * D),
        norm1_g=gr["dn1"].reshape(NL, D), norm2_g=gr["dn2"].reshape(NL, D),
        q_norm_g=gr["dgqk"].reshape(NL, 2, NH, HEAD_DIM)[:, 0].sum(axis=1),
        k_norm_g=gr["dgqk"].reshape(NL, 2, NH, HEAD_DIM)[:, 1].sum(axis=1),
        ssm_a_re=dar.reshape(NL, G, SSM_STATE), ssm_a_im=dai.reshape(NL, G, SSM_STATE),
        ssm_log_dt=dldt.reshape(NL, G, SSM_STATE).sum(axis=-1),
        ssm_b_re=jnp.swapaxes(unt(_block_diag_extract(dbr_bd, SSM_GROUP, SSM_STATE)), -1, -2),
        ssm_b_im=jnp.swapaxes(unt(_block_diag_extract(dbi_bd, SSM_GROUP, SSM_STATE)), -1, -2),
        ssm_c_re=jnp.swapaxes(unt(_block_diag_extract(gr["dcr"], SSM_STATE, SSM_GROUP)), -1, -2),
        ssm_c_im=jnp.swapaxes(unt(_block_diag_extract(gr["dci"], SSM_STATE, SSM_GROUP)), -1, -2),
        ssm_d=gr["dd"].reshape(NL, G, SSM_GROUP),
        glu_w=unt(_block_diag_extract(gr["dwg"], SSM_GROUP, SSM_GROUP)),
        glu_b=gr["dgb"].reshape(NL, G, SSM_GROUP),
        attn_out_g=gr["dga"].reshape(NL, AW), ssm_out_g=gr["dgs"].reshape(NL, SW),
        ffn_conv_b=gr["dcb"].reshape(NL, 2 * NSH * FS),
    )

    def pack(tree):
        flat = jnp.concatenate([tree[n].reshape(-1) for n in small])
        pad = (-flat.shape[0]) % (512 * LANES)
        return jnp.pad(flat, (0, pad)).reshape(1, -1, LANES)

    (small_parts,) = _gather_shards([pack(local)], "gather_small_grads")
    sg, sd, sm, sv = _adamw(small_parts, pack(weights), pack(mom_m), pack(mom_v))

    def unpack(buf):
        flat = buf.reshape(-1)
        out, off = {}, 0
        for n in small:
            size = weights[n].size
            out[n] = flat[off:off + size].reshape(weights[n].shape)
            off += size
        return out

    ug, ud, um, uv = unpack(sg), unpack(sd), unpack(sm), unpack(sv)
    results = {n: (ug[n], ud[n], um[n], uv[n]) for n in small}

    dmod_all = _gather_small(gr["dmod"].reshape(NL, N_MOD * D), "gather_dmod")
    dmod_mine = lax.dynamic_slice_in_dim(dmod_all, me * NCA, NCA, axis=2)
    d_ada_w = _ada_bwd(jnp.transpose(c_all), jnp.transpose(dmod_mine, (1, 0, 2)))
    results["ada_w"] = tuple(_adamw(d_ada_w[:, None], ada_w, m_ada_w, v_ada_w))

    my_core = lax.axis_index("c")
    by_target = [g.reshape((NL, NDEV // 2, 2) + g.shape[2:])
                 for g in (gr["dw_in"], gr["dw_out"], gr["dw_up"], gr["dcw"], gr["dw_down"])]
    mine = [lax.dynamic_index_in_dim(g, my_core, axis=2, keepdims=False) for g in by_target]
    theirs = _sibling_exchange(
        [lax.dynamic_index_in_dim(g, 1 - my_core, axis=2, keepdims=False) for g in by_target], "pair_weight_grads")
    pair_sums = [_pair_add(a, b) for a, b in zip(mine, theirs)]
    parts = _chip_exchange(pair_sums, "scatter_weight_grads")
    for n, pt in zip(("w_in", "w_out", "ffn_w_up", "ffn_conv_w", "ffn_w_down"), parts):
        results[n] = tuple(_adamw(pt, weights[n], mom_m[n], mom_v[n]))

    out = [loss, grad_x[None]]
    for k in range(4):
        out.extend(results[n][k] for n in names)
    return tuple(out)
```

```python
import jax
import jax.numpy as jnp
from jax import lax
from jax.experimental import pallas as pl
from jax.experimental.pallas import tpu as pltpu

F32 = jnp.float32
BF16 = jnp.bfloat16
NDEV = 8
LANES = 128
HEAD_DIM = 64
SSM_GROUP = 16
SSM_STATE = 64
GROUPS_PER_TILE = LANES // SSM_GROUP
STATE_TILE = GROUPS_PER_TILE * SSM_STATE
N_MOD = 6
ATTN_STRIP = 32
EPS = 1e-6
ADAM_LR, ADAM_B1, ADAM_B2, ADAM_EPS, ADAM_WD, ADAM_STEP = 0.001, 0.9, 0.999, 1e-08, 0.01, 10
VMEM_LIMIT = 48 * 1024 * 1024
MESH_IDS = pl.DeviceIdType.MESH

NN = (((1,), (0,)), ((), ()))
NT = (((1,), (1,)), ((), ()))
TN = (((0,), (0,)), ((), ()))


def _dot(a, b, dims=NN):
    return lax.dot_general(a, b, dims, preferred_element_type=F32)


def _pcall(body, *, name, out_shape, in_specs, out_specs, grid=(), scratch=()):
    return pl.pallas_call(
        body, name=name, grid=grid, in_specs=in_specs, out_specs=out_specs, out_shape=out_shape,
        scratch_shapes=list(scratch),
        compiler_params=pltpu.CompilerParams(vmem_limit_bytes=VMEM_LIMIT))


def _row_tile(n, want=512):
    t = min(n, want)
    assert n % t == 0
    return t


def _my_index():
    return 4 * lax.axis_index("x") + 2 * lax.axis_index("y") + lax.axis_index("c")


HBM_SPEC = pl.BlockSpec(memory_space=pltpu.HBM)


def _mesh_place():
    x, y, c = lax.axis_index("x"), lax.axis_index("y"), lax.axis_index("c")
    chips = [(1 - x, y), (x, 1 - y), (1 - x, 1 - y)]
    return x, y, c, chips


def _gather_small(arr, name):
    def body(in_ref, out_ref, send_sems, recv_sems, local_sem):
        x, y, c, _ = _mesh_place()
        me = 4 * x + 2 * y + c
        own = pltpu.make_async_copy(in_ref, out_ref.at[me], local_sem)
        own.start()
        sends, recvs = [], []
        for k in range(1, NDEV):
            px = 1 - x if k & 4 else x
            py = 1 - y if k & 2 else y
            pc = 1 - c if k & 1 else c
            common = dict(send_sem=send_sems.at[k - 1], recv_sem=recv_sems.at[k - 1],
                          device_id=(px, py, pc), device_id_type=MESH_IDS)
            snd = pltpu.make_async_remote_copy(src_ref=in_ref, dst_ref=out_ref.at[me], **common)
            snd.start()
            sends.append(snd)
            recvs.append(pltpu.make_async_remote_copy(
                src_ref=in_ref, dst_ref=out_ref.at[4 * px + 2 * py + pc], **common))
        for r in recvs:
            r.wait_recv()
        for s in sends:
            s.wait_send()
        own.wait()

    return pl.pallas_call(
        body, name=name, out_shape=jax.ShapeDtypeStruct((NDEV,) + arr.shape, arr.dtype),
        in_specs=[HBM_SPEC], out_specs=HBM_SPEC,
        scratch_shapes=[pltpu.SemaphoreType.DMA((NDEV - 1,)), pltpu.SemaphoreType.DMA((NDEV - 1,)),
                        pltpu.SemaphoreType.DMA(())],
    )(arr)


def _all_to_all(srcs, dsts, send_sems, recv_sems, scatter):
    x, y, c, _ = _mesh_place()
    me = 4 * x + 2 * y + c
    pairs = []
    for k in range(1, NDEV):
        px = 1 - x if k & 4 else x
        py = 1 - y if k & 2 else y
        pc = 1 - c if k & 1 else c
        peer = 4 * px + 2 * py + pc
        for a in range(len(srcs)):
            src = srcs[a].at[peer] if scatter else srcs[a]
            common = dict(send_sem=send_sems.at[k - 1, a], recv_sem=recv_sems.at[k - 1, a],
                          device_id=(px, py, pc), device_id_type=MESH_IDS)
            pairs.append((pltpu.make_async_remote_copy(src_ref=src, dst_ref=dsts[a].at[me], **common),
                          pltpu.make_async_remote_copy(src_ref=src, dst_ref=dsts[a].at[peer], **common)))

    def start():
        for send, _ in pairs:
            send.start()

    def wait():
        for _, arrival in pairs:
            arrival.wait_recv()
        for send, _ in pairs:
            send.wait_send()

    return start, wait


def _place_own(outs, arrs, scatter):
    me = _my_index()
    owns = [lax.dynamic_slice_in_dim(a, me, 1, axis=0) if scatter else a[None] for a in arrs]
    return [lax.dynamic_update_slice_in_dim(o, own, me, axis=0) for o, own in zip(outs, owns)]


def _exchange_shapes(arrs, scatter):
    return [jax.ShapeDtypeStruct(a.shape if scatter else (NDEV,) + a.shape, a.dtype) for a in arrs]


def _scatter_direct(arrs, name):
    n = len(arrs)

    def body(*refs):
        start, wait = _all_to_all(refs[:n], refs[n:2 * n], refs[2 * n], refs[2 * n + 1], True)
        start()
        wait()

    outs = pl.pallas_call(
        body, name=name, out_shape=_exchange_shapes(arrs, True), in_specs=[HBM_SPEC] * n, out_specs=[HBM_SPEC] * n,
        scratch_shapes=[pltpu.SemaphoreType.DMA((NDEV - 1, n)), pltpu.SemaphoreType.DMA((NDEV - 1, n))],
    )(*arrs)
    return _place_own(outs, arrs, True)


def _start_by_layer(make, nl):
    for l in range(nl):
        make(l).start()


def _gather_shards(arrs, name):
    n = len(arrs)
    nl = arrs[0].shape[0]

    def body(*refs):
        ins, outs = refs[:n], refs[n:2 * n]
        send_sems, recv_sems = refs[2 * n:]
        x, y, c, chips = _mesh_place()
        dev = lambda px, py, pc: 4 * px + 2 * py + pc

        def copy(k, a, src, block, to, layer=slice(None)):
            return pltpu.make_async_remote_copy(
                src_ref=src.at[layer], dst_ref=outs[a].at[layer, block], send_sem=send_sems.at[k, a],
                recv_sem=recv_sems.at[k, a], device_id=to, device_id_type=MESH_IDS)

        me = dev(x, y, c)
        sent = []
        for a in range(n):
            _start_by_layer(lambda l: copy(0, a, ins[a], me, (x, y, 1 - c), l), nl)
            sent.append(copy(0, a, ins[a], me, (x, y, 1 - c)))
        for j, (px, py) in enumerate(chips):
            for a in range(n):
                _start_by_layer(lambda l: copy(1 + j, a, ins[a], me, (px, py, c), l), nl)
                sent.append(copy(1 + j, a, ins[a], me, (px, py, c)))
        for j, (px, py) in enumerate(chips):
            for a in range(n):
                blk = dev(px, py, c)
                copy(1 + j, a, ins[a], blk, (x, y, c)).wait_recv()
                got = outs[a].at[:, blk]
                _start_by_layer(lambda l: copy(4 + j, a, got, blk, (x, y, 1 - c), l), nl)
                sent.append(copy(4 + j, a, got, blk, (x, y, 1 - c)))
        for a in range(n):
            copy(0, a, ins[a], dev(x, y, 1 - c), (x, y, c)).wait_recv()
        for j, (px, py) in enumerate(chips):
            for a in range(n):
                copy(4 + j, a, ins[a], dev(px, py, 1 - c), (x, y, c)).wait_recv()
        for s in sent:
            s.wait_send()

    out_shape = [jax.ShapeDtypeStruct((a.shape[0], NDEV) + a.shape[1:], a.dtype) for a in arrs]
    outs = pl.pallas_call(
        body, name=name, out_shape=out_shape, in_specs=[HBM_SPEC] * n, out_specs=[HBM_SPEC] * n,
        scratch_shapes=[pltpu.SemaphoreType.DMA((7, n)), pltpu.SemaphoreType.DMA((7, n))],
    )(*arrs)
    me = _my_index()
    return [lax.dynamic_update_slice_in_dim(o, a[:, None], me, axis=1) for o, a in zip(outs, arrs)]


def _matmul(name, a, b, *, dims, grid, a_spec, b_spec, out_shape, out_specs, kaxis=None, acc_shape=None,
            extra=(), extra_specs=(), epilogue=None):
    nk = grid[kaxis] if kaxis is not None else 1
    ne = len(extra)
    multi = isinstance(out_shape, (list, tuple))
    n_out = len(out_shape) if multi else 1

    def body(*refs):
        a_ref, b_ref = refs[0], refs[1]
        ex = refs[2:2 + ne]
        outs = refs[2 + ne:2 + ne + n_out]

        def write(res):
            vals = epilogue(res, *[e[...] for e in ex]) if epilogue is not None else (res,)
            for o, v in zip(outs, vals):
                o[...] = v.astype(o.dtype)

        part = _dot(a_ref[...].astype(BF16), b_ref[...].astype(BF16), dims)
        if nk == 1:
            write(part)
        else:
            acc = refs[-1]
            k = pl.program_id(kaxis)

            @pl.when(k == 0)
            def _():
                acc[...] = part

            @pl.when(k > 0)
            def _():
                acc[...] += part

            @pl.when(k == nk - 1)
            def _():
                write(acc[...])

    scratch = [pltpu.VMEM(acc_shape, F32)] if nk > 1 else []
    return _pcall(body, name=name, grid=grid, in_specs=[a_spec, b_spec, *extra_specs],
                  out_specs=out_specs, out_shape=out_shape, scratch=scratch)(a, b, *extra)


def _rms(x, g):
    inv = lax.rsqrt(jnp.mean(x * x, axis=-1, keepdims=True) + EPS)
    return x * inv * g


def _lnmod_math(x, g, sh, sc):
    return _rms(x, g) * (1.0 + sc) + sh


def _head_sums(x):
    row = lax.broadcasted_iota(jnp.int32, (LANES, LANES), 0) // HEAD_DIM
    col = lax.broadcasted_iota(jnp.int32, (LANES, LANES), 1) // HEAD_DIM
    same_head = jnp.where(row == col, 1.0, 0.0).astype(BF16)
    hi = x.astype(BF16)
    lo = (x - hi.astype(F32)).astype(BF16)
    return _dot(hi, same_head) + _dot(lo, same_head)


def _qkn_inv(p):
    return lax.rsqrt(_head_sums(p * p) / HEAD_DIM + EPS)


def _glu_math(val, gate):
    return jax.nn.gelu(gate) * val


def _accumulate(ref, val, first):
    @pl.when(first)
    def _():
        ref[...] = val

    @pl.when(jnp.logical_not(first))
    def _():
        ref[...] += val


def _lnmod_fwd(x, g, sh, sc):
    L, D = x.shape
    tm = _row_tile(L)

    def body(x_ref, g_ref, sh_ref, sc_ref, h_ref):
        h_ref[...] = _lnmod_math(x_ref[...], g_ref[...], sh_ref[...], sc_ref[...]).astype(BF16)

    row = pl.BlockSpec((tm, D), lambda i: (i, 0))
    vec = pl.BlockSpec((1, D), lambda i: (0, 0))
    return _pcall(body, name="lnmod_fwd", grid=(L // tm,), in_specs=[row, vec, vec, vec], out_specs=row,
                  out_shape=jax.ShapeDtypeStruct((L, D), BF16))(x, g, sh, sc)


def _lnmod_bwd(dh, x, g, sh, sc, dres):
    L, D = x.shape
    tm = _row_tile(L)

    def body(dh_ref, x_ref, g_ref, sh_ref, sc_ref, res_ref, dx_ref, dg_ref, dsh_ref, dsc_ref):
        _, vjp = jax.vjp(_lnmod_math, x_ref[...], g_ref[...], sh_ref[...], sc_ref[...])
        dx, dg, dsh, dsc = vjp(dh_ref[...])
        dx_ref[...] = dx + res_ref[...]
        first = pl.program_id(0) == 0
        _accumulate(dg_ref, dg, first)
        _accumulate(dsh_ref, dsh, first)
        _accumulate(dsc_ref, dsc, first)

    row = pl.BlockSpec((tm, D), lambda i: (i, 0))
    vec = pl.BlockSpec((1, D), lambda i: (0, 0))
    vs = jax.ShapeDtypeStruct((1, D), F32)
    return _pcall(body, name="lnmod_bwd", grid=(L // tm,), in_specs=[row, row, vec, vec, vec, row],
                  out_specs=[row, vec, vec, vec],
                  out_shape=[jax.ShapeDtypeStruct((L, D), F32), vs, vs, vs])(dh, x, g, sh, sc, dres)


def _gate_bwd(dx, a, gate):
    L, D = dx.shape
    tm = _row_tile(L)

    def body(dx_ref, a_ref, g_ref, da_ref, dg_ref):
        dxv = dx_ref[...]
        da_ref[...] = (g_ref[...] * dxv).astype(BF16)
        _accumulate(dg_ref, jnp.sum(dxv * a_ref[...], axis=0, keepdims=True), pl.program_id(0) == 0)

    row = pl.BlockSpec((tm, D), lambda i: (i, 0))
    vec = pl.BlockSpec((1, D), lambda i: (0, 0))
    return _pcall(body, name="gate_bwd", grid=(L // tm,), in_specs=[row, row, vec], out_specs=[row, vec],
                  out_shape=[jax.ShapeDtypeStruct((L, D), BF16), jax.ShapeDtypeStruct((1, D), F32)])(dx, a, gate)


def _qknorm_fwd(p, gqk, AW):
    L = p.shape[0]
    tm = _row_tile(L, 2048)
    ncol = 2 * AW // LANES

    def body(p_ref, g_ref, o_ref):
        p = p_ref[...]
        o_ref[...] = (p * _qkn_inv(p) * g_ref[...]).astype(BF16)

    blk = pl.BlockSpec((tm, LANES), lambda i, j: (i, j))
    vec = pl.BlockSpec((1, LANES), lambda i, j: (0, j))
    return _pcall(body, name="qknorm_fwd", grid=(L // tm, ncol), in_specs=[blk, vec], out_specs=blk,
                  out_shape=jax.ShapeDtypeStruct((L, 2 * AW), BF16))(p, gqk)


def _qknorm_bwd(dqk, p, gqk, AW):
    L = p.shape[0]
    tm = _row_tile(L, 2048)
    ncol = 2 * AW // LANES

    def body(d_ref, p_ref, g_ref, dp_ref, dg_ref):
        p, dy = p_ref[...], d_ref[...]
        inv = _qkn_inv(p)
        gdy = g_ref[...] * dy
        dp_ref[...] = (inv * gdy - p * (inv * inv * inv) * (_head_sums(p * gdy) / HEAD_DIM)).astype(BF16)
        _accumulate(dg_ref, jnp.sum(dy * p * inv, axis=0, keepdims=True), pl.program_id(1) == 0)

    blk = pl.BlockSpec((tm, LANES), lambda j, i: (i, j))
    vec = pl.BlockSpec((1, LANES), lambda j, i: (0, j))
    return _pcall(body, name="qknorm_bwd", grid=(ncol, L // tm), in_specs=[blk, blk, vec], out_specs=[blk, vec],
                  out_shape=[jax.ShapeDtypeStruct((L, 2 * AW), BF16),
                             jax.ShapeDtypeStruct((1, 2 * AW), F32)])(dqk, p, gqk)


def _outnorm_fwd(oa, os_, ga, gs):
    L, AW = oa.shape
    SW = os_.shape[1]
    tm = _row_tile(L)

    def body(oa_ref, os_ref, ga_ref, gs_ref, o_ref):
        o_ref[:, :AW] = _rms(oa_ref[...], ga_ref[...]).astype(BF16)
        o_ref[:, AW:] = _rms(os_ref[...], gs_ref[...]).astype(BF16)

    ra = pl.BlockSpec((tm, AW), lambda i: (i, 0))
    rs = pl.BlockSpec((tm, SW), lambda i: (i, 0))
    va = pl.BlockSpec((1, AW), lambda i: (0, 0))
    vs = pl.BlockSpec((1, SW), lambda i: (0, 0))
    ro = pl.BlockSpec((tm, AW + SW), lambda i: (i, 0))
    return _pcall(body, name="outnorm_fwd", grid=(L // tm,), in_specs=[ra, rs, va, vs], out_specs=ro,
                  out_shape=jax.ShapeDtypeStruct((L, AW + SW), BF16))(oa, os_, ga, gs)


def _outnorm_bwd(do, oa, os_, ga, gs):
    L, AW = oa.shape
    SW = os_.shape[1]
    tm = _row_tile(L)

    def body(do_ref, oa_ref, os_ref, ga_ref, gs_ref, doa_ref, dos_ref, dga_ref, dgs_ref):
        first = pl.program_id(0) == 0
        _, vjp_a = jax.vjp(_rms, oa_ref[...], ga_ref[...])
        doa, dga = vjp_a(do_ref[:, :AW])
        _, vjp_s = jax.vjp(_rms, os_ref[...], gs_ref[...])
        dos, dgs = vjp_s(do_ref[:, AW:])
        doa_ref[...] = doa
        dos_ref[...] = dos
        _accumulate(dga_ref, dga, first)
        _accumulate(dgs_ref, dgs, first)

    ra = pl.BlockSpec((tm, AW), lambda i: (i, 0))
    rs = pl.BlockSpec((tm, SW), lambda i: (i, 0))
    va = pl.BlockSpec((1, AW), lambda i: (0, 0))
    vs = pl.BlockSpec((1, SW), lambda i: (0, 0))
    ro = pl.BlockSpec((tm, AW + SW), lambda i: (i, 0))
    return _pcall(body, name="outnorm_bwd", grid=(L // tm,), in_specs=[ro, ra, rs, va, vs],
                  out_specs=[ra, rs, va, vs],
                  out_shape=[jax.ShapeDtypeStruct((L, AW), F32), jax.ShapeDtypeStruct((L, SW), F32),
                             jax.ShapeDtypeStruct((1, AW), F32), jax.ShapeDtypeStruct((1, SW), F32)])(
                                 do, oa, os_, ga, gs)


def _softplus_neg_abs(z):
    return jnp.log(1.0 + jnp.exp(-jnp.abs(z)))


def _attn_masks(B):
    row = lax.broadcasted_iota(jnp.int32, (B, B), 0)
    col = lax.broadcasted_iota(jnp.int32, (B, B), 1)
    strict = col < row
    upper = jnp.where(row > col, 1.0, 0.0).astype(BF16)
    lower = jnp.where(row < col, 1.0, 0.0).astype(BF16)
    return strict, upper, lower


def _ride_along(ride, scatter, HP, nb):
    n = len(ride)
    sems = [pltpu.SemaphoreType.DMA((NDEV - 1, n)), pltpu.SemaphoreType.DMA((NDEV - 1, n))] if n else []

    def hooks(srcs, dsts, send_sems, recv_sems):
        start, wait = _all_to_all(srcs, dsts, send_sems, recv_sems, scatter)
        hp, i = pl.program_id(0), pl.program_id(1)
        return (hp == 0) & (i == 0), start, (hp == HP - 1) & (i == nb - 1), wait

    return [HBM_SPEC] * n, _exchange_shapes(ride, scatter), sems, hooks


def _attn_fwd(qk, p, AW, B, ride=()):
    L = qk.shape[0]
    HP = AW // LANES
    nb = L // B
    n = len(ride)
    ride_specs, ride_shapes, ride_sems, ride_hooks = _ride_along(ride, False, HP, nb)

    def body(*refs):
        q_ref, k_ref, v_ref = refs[:3]
        o_ref, tot_ref = refs[3 + n:5 + n]
        scr = refs[5 + 2 * n:]
        if n:
            first, start, last, wait = ride_hooks(refs[3:3 + n], refs[5 + n:5 + 2 * n], scr[10], scr[11])
            pl.when(first)(start)
        lb_scr = (scr[0:2], scr[2:4])
        tail_scr = (scr[4:6], scr[6:8])
        sum_scr = scr[8:10]
        i = pl.program_id(1)
        m0 = lax.broadcasted_iota(jnp.int32, (1, LANES), 1) < HEAD_DIM
        strict, upper, _ = _attn_masks(B)
        q = q_ref[...] * 0.125
        zq = jnp.zeros_like(q)
        qh = (jnp.where(m0, q, zq), jnp.where(m0, zq, q))

        def keys(j):
            start = pl.multiple_of(jnp.maximum(j, 0) * B, B)
            return k_ref[pl.ds(start, B), :]

        def vals(j):
            start = pl.multiple_of(jnp.maximum(j, 0) * B, B)
            return v_ref[pl.ds(start, B), :].astype(BF16)

        strips = [slice(s, min(s + ATTN_STRIP, B)) for s in range(0, B, ATTN_STRIP)]

        def scores(j):
            kj = keys(j)
            return tuple(_dot(qh[h], kj, NT) for h in (0, 1))

        def logits(zs2, slot, diag):
            for h in (0, 1):
                z = zs2[h]
                his = []
                for rows in strips:
                    zs = z[rows]
                    lb = jnp.minimum(zs, 0.0) - _softplus_neg_abs(zs)
                    l1 = lb - zs
                    if diag:
                        l1 = jnp.where(strict[rows], l1, 0.0)
                    lb_scr[slot][h][rows, :] = lb
                    his.append(l1.astype(BF16))
                    rsum = jnp.sum(l1, axis=-1, keepdims=True)
                    if h == 0:
                        sum_scr[slot][rows, :] = jnp.broadcast_to(rsum, (rows.stop - rows.start, LANES))
                    else:
                        sum_scr[slot][rows, :] = jnp.where(m0, sum_scr[slot][rows, :], rsum)
                cat = lambda xs: jnp.concatenate(xs, axis=0)
                tail_scr[slot][h][...] = _dot(cat(his), upper)

        def attend(j, slot, diag):
            vj = vals(j)
            pv = []
            for h in (0, 1):
                ws = []
                for rows in strips:
                    w = jnp.exp(lb_scr[slot][h][rows, :] + tail_scr[slot][h][rows, :])
                    if diag:
                        w = jnp.where(strict[rows], w, 0.0)
                    ws.append(w.astype(BF16))
                pv.append(_dot(jnp.concatenate(ws, axis=0), vj))
            return jnp.where(m0, pv[0], pv[1])

        logits(scores(i), 0, True)
        o_ref[...] = attend(i, 0, True)
        tot_ref[...] = sum_scr[0][...]

        def half(j, slot):
            z = scores(j - 1)
            pv = attend(j, slot, False)
            logits(z, 1 - slot, False)
            o_ref[...] += pv * jnp.exp(tot_ref[...])
            tot_ref[...] += sum_scr[slot][...]

        @pl.when(i > 0)
        def _():
            logits(scores(i - 1), 1, False)

            @pl.loop(0, (i + 1) // 2)
            def _(t):
                j = i - 1 - 2 * t
                half(j, 1)

                @pl.when(j > 0)
                def _():
                    half(j - 1, 0)

        if n:
            pl.when(last)(wait)

    qspec = pl.BlockSpec((B, LANES), lambda hp, i: (i, hp))
    kspec = pl.BlockSpec((L, LANES), lambda hp, i: (0, HP + hp))
    vspec = pl.BlockSpec((L, LANES), lambda hp, i: (0, 2 * HP + hp))
    ospec = pl.BlockSpec((B, LANES), lambda hp, i: (i, hp))
    shp = jax.ShapeDtypeStruct((L, AW), F32)
    o, tot, *gathered = _pcall(
        body, name="attn_fwd", grid=(HP, nb), in_specs=[qspec, kspec, vspec] + ride_specs,
        out_specs=[ospec, ospec] + ride_specs, out_shape=[shp, shp] + ride_shapes,
        scratch=[pltpu.VMEM((B, B), F32)] * 8 + [pltpu.VMEM((B, LANES), F32)] * 2 + ride_sems)(qk, qk, p, *ride)
    return o, tot, _place_own(gathered, ride, False)


def _attn_bwd(qk, p, do, tot, AW, B, ride=()):
    L = qk.shape[0]
    HP = AW // LANES
    nb = L // B
    n = len(ride)
    ride_specs, ride_shapes, ride_sems, ride_hooks = _ride_along(ride, True, HP, nb)

    def body(*refs):
        q_ref, k_ref, v_ref, do_ref, tot_ref = refs[:5]
        dq_ref, dk_ref, dv_ref = refs[5 + n:8 + n]
        scr = refs[8 + 2 * n:]
        if n:
            first_step, start, last_step, wait = ride_hooks(refs[5:5 + n], refs[8 + n:8 + 2 * n], scr[15], scr[16])
            pl.when(first_step)(start)
        lb_scr, tail_scr, dw_scr, e_scr, beta_scr = scr[0:2], scr[2:4], scr[4:6], scr[6:8], scr[8:10]
        dos_scr, bsum_scr, left_scr, esum_scr, ecum_scr = scr[10:15]
        i = pl.program_id(1)
        m0 = lax.broadcasted_iota(jnp.int32, (1, LANES), 1) < HEAD_DIM
        strict, upper, lower = _attn_masks(B)
        strips = [slice(s, min(s + ATTN_STRIP, B)) for s in range(0, B, ATTN_STRIP)]
        cat = lambda xs: jnp.concatenate(xs, axis=0)

        @pl.when(i == 0)
        def _():
            dk_ref[...] = jnp.zeros_like(dk_ref)
            dv_ref[...] = jnp.zeros_like(dv_ref)

        q = q_ref[...] * 0.125
        zq = jnp.zeros_like(q)
        qh = (jnp.where(m0, q, zq), jnp.where(m0, zq, q))
        heads = lambda a: (jnp.where(m0, a, jnp.zeros_like(a)), jnp.where(m0, jnp.zeros_like(a), a))
        left_scr[...] = jnp.zeros_like(left_scr)
        ecum_scr[...] = jnp.zeros_like(ecum_scr)
        dq_ref[...] = jnp.zeros_like(dq_ref)

        def block_rows(j):
            return pl.ds(pl.multiple_of(j * B, B), B)

        def put_row_sums(ref, rows, h, rsum):
            if h == 0:
                ref[rows, :] = jnp.broadcast_to(rsum, (rows.stop - rows.start, LANES))
            else:
                ref[rows, :] = jnp.where(m0, ref[rows, :], rsum)

        def head_cols(x, h):
            other = pltpu.roll(x, HEAD_DIM, 1)
            full = jnp.where(m0, x, other) if h == 0 else jnp.where(m0, other, x)
            return jnp.concatenate([full] * (B // LANES), axis=1) if B > LANES else full

        def scores(j):
            kj = k_ref[block_rows(j), :]
            return tuple(_dot(qh[h], kj, NT) for h in (0, 1))

        def stage_a(j, zs2, diag):
            for h in (0, 1):
                his = []
                for rows in strips:
                    zs = zs2[h][rows]
                    lb = jnp.minimum(zs, 0.0) - _softplus_neg_abs(zs)
                    l1 = lb - zs
                    if diag:
                        l1 = jnp.where(strict[rows], l1, 0.0)
                    lb_scr[h][rows, :] = lb
                    his.append(l1.astype(BF16))
                    put_row_sums(bsum_scr, rows, h, jnp.sum(l1, axis=-1, keepdims=True))
                tail_scr[h][...] = _dot(cat(his), upper)
            bs = bsum_scr[...]
            scale = jnp.exp(tot_ref[...] - left_scr[...] - bs)
            left_scr[...] += bs
            dos = (do_ref[...] * scale).astype(BF16)
            dos_scr[...] = dos
            vj = v_ref[block_rows(j), :].astype(BF16)
            dosh = heads(dos)
            for h in (0, 1):
                dw_scr[h][...] = _dot(dosh[h], vj, NT)

        def stage_b(j, diag):
            dosh = heads(dos_scr[...])
            pres = []
            dv_blk = jnp.zeros((B, LANES), F32)
            for h in (0, 1):
                ehs, wbs = [], []
                for rows in strips:
                    lb = lb_scr[h][rows, :]
                    w = jnp.exp(lb + tail_scr[h][rows, :])
                    if diag:
                        w = jnp.where(strict[rows], w, 0.0)
                    beta_scr[h][rows, :] = jnp.exp(lb)
                    e = dw_scr[h][rows, :] * w
                    e_scr[h][rows, :] = e
                    ehs.append(e.astype(BF16))
                    wbs.append(w.astype(BF16))
                    put_row_sums(esum_scr, rows, h, jnp.sum(e, axis=-1, keepdims=True))
                pres.append(_dot(cat(ehs), lower))
                dv_blk = dv_blk + _dot(cat(wbs), dosh[h], TN)
            dv_ref[block_rows(j), :] += dv_blk
            return pres

        def stage_c(j, pres, diag):
            kh = heads(k_ref[block_rows(j), :])
            dq = jnp.zeros((B, LANES), F32)
            dk_blk = jnp.zeros((B, LANES), F32)
            for h in (0, 1):
                dzs = []
                for rows in strips:
                    e = e_scr[h][rows, :]
                    dl1 = pres[h][rows] + head_cols(ecum_scr[rows, :], h)
                    dz = e - beta_scr[h][rows, :] * (e + dl1)
                    if diag:
                        dz = jnp.where(strict[rows], dz, 0.0)
                    dzs.append(dz.astype(BF16))
                dzb = cat(dzs)
                dq = dq + _dot(dzb, kh[h])
                dk_blk = dk_blk + _dot(dzb, qh[h], TN)
            dq_ref[...] += dq
            dk_ref[block_rows(j), :] += dk_blk
            ecum_scr[...] += esum_scr[...]

        def step(t, diag_next):
            z = scores(t + 1)
            pres = stage_b(t, False)
            stage_a(t + 1, z, diag_next)
            stage_c(t, pres, False)

        @pl.when(i > 0)
        def _():
            stage_a(0, scores(0), False)

            @pl.loop(0, i - 1)
            def _(t):
                step(t, False)

            step(i - 1, True)

        @pl.when(i == 0)
        def _():
            stage_a(0, scores(0), True)

        stage_c(i, stage_b(i, True), True)
        dq_ref[...] = dq_ref[...] * 0.125
        if n:
            pl.when(last_step)(wait)

    qspec = pl.BlockSpec((B, LANES), lambda hp, i: (i, hp))
    kspec = pl.BlockSpec((L, LANES), lambda hp, i: (0, HP + hp))
    vspec = pl.BlockSpec((L, LANES), lambda hp, i: (0, 2 * HP + hp))
    full = pl.BlockSpec((L, LANES), lambda hp, i: (0, hp))
    shp = jax.ShapeDtypeStruct((L, AW), F32)
    dq, dk, dv, *received = _pcall(
        body, name="attn_bwd", grid=(HP, nb), in_specs=[qspec, kspec, vspec, qspec, qspec] + ride_specs,
        out_specs=[qspec, full, full] + ride_specs, out_shape=[shp, shp, shp] + ride_shapes,
        scratch=[pltpu.VMEM((B, B), F32)] * 10 + [pltpu.VMEM((B, LANES), BF16)]
        + [pltpu.VMEM((B, LANES), F32)] * 4 + ride_sems)(qk, qk, p, do, tot, *ride)
    return dq, dk, dv, _place_own(received, ride, True)


def _s5_disc(ar, ai, ldt):
    dt = jnp.exp(ldt)
    mag = jnp.exp(dt * ar)
    abr = mag * jnp.cos(dt * ai)
    abi = mag * jnp.sin(dt * ai)
    emr = abr - 1.0
    emi = abi
    den = ar * ar + ai * ai
    fr = (emr * ar + emi * ai) / den
    fi = (emi * ar - emr * ai) / den
    return abr, abi, fr, fi


def _s5_params_math(ar, ai, ldt, br, bi):
    abr, abi, fr, fi = _s5_disc(ar, ai, ldt)
    return abr, abi, fr * br - fi * bi, fr * bi + fi * br


def _cmul_add(xr, xi, kr, ki, sr, si):
    return xr + (kr * sr - ki * si), xi + (kr * si + ki * sr)


def _s5_tab_rows(T):
    ng = T // 8
    return dict(NG=ng, POW=0, PA=ng, PAR=ng + 8, LA=ng + 16, ROWS=ng + 24)


def _s5_prep(ar, ai, ldt, braw_r, braw_i, T):
    NL, _, NS = ar.shape
    GT = NS // STATE_TILE
    R = _s5_tab_rows(T)

    def body(ar_ref, ai_ref, ldt_ref, br_ref, bi_ref, abr_ref, abi_ref, tabr_ref, tabi_ref, bsr_ref, bsi_ref):
        for t in range(GT):
            sl = slice(t * STATE_TILE, (t + 1) * STATE_TILE)
            abr, abi, bsr, bsi = _s5_params_math(ar_ref[:, sl], ai_ref[:, sl], ldt_ref[:, sl],
                                                 br_ref[t], bi_ref[t])
            abr_ref[:, sl] = abr
            abi_ref[:, sl] = abi
            bsr_ref[t] = bsr.astype(BF16)
            bsi_ref[t] = bsi.astype(BF16)

            def put(row, vr, vi):
                tabr_ref[row:row + 1, sl] = vr
                tabi_ref[row:row + 1, sl] = vi

            pr, pi = abr, abi
            for g in range(R["NG"]):
                put(R["POW"] + g, pr, pi)
                if g + 1 < R["NG"]:
                    pr, pi = pr * abr - pi * abi, pr * abi + pi * abr
            big_r, big_i = pr, pi
            qr, qi = big_r, big_i
            for r in range(8):
                put(R["PA"] + r, qr, qi)
                put(R["PAR"] + 7 - r, qr, qi)
                qr, qi = qr * big_r - qi * big_i, qr * big_i + qi * big_r
            qr, qi = big_r, big_i
            for k in range(3):
                put(R["LA"] + k, qr, qi)
                qr, qi = qr * qr - qi * qi, 2.0 * qr * qi
            for k in range(3, 8):
                put(R["LA"] + k, jnp.zeros_like(qr), jnp.zeros_like(qi))

    rowspec = pl.BlockSpec((None, 1, NS), lambda l: (l, 0, 0))
    bspec = pl.BlockSpec((None, GT, LANES, STATE_TILE), lambda l: (l, 0, 0, 0))
    tabspec = pl.BlockSpec((None, R["ROWS"], NS), lambda l: (l, 0, 0))
    rs = jax.ShapeDtypeStruct((NL, 1, NS), F32)
    ts = jax.ShapeDtypeStruct((NL, R["ROWS"], NS), F32)
    bs = jax.ShapeDtypeStruct((NL, GT, LANES, STATE_TILE), BF16)
    return _pcall(body, name="s5_prep", grid=(NL,), in_specs=[rowspec] * 3 + [bspec] * 2,
                  out_specs=[rowspec, rowspec, tabspec, tabspec, bspec, bspec],
                  out_shape=[rs, rs, ts, ts, bs, bs])(ar, ai, ldt, braw_r, braw_i)


def _s5_prep_bwd(ar, ai, ldt, braw_r, braw_i, dabr, dabi, dbsr, dbsi):
    NL, _, NS = ar.shape
    GT = NS // STATE_TILE

    def body(ar_ref, ai_ref, ldt_ref, br_ref, bi_ref, dabr_ref, dabi_ref, dbsr_ref, dbsi_ref,
             dar_ref, dai_ref, dldt_ref, dbr_ref, dbi_ref):
        for t in range(GT):
            sl = slice(t * STATE_TILE, (t + 1) * STATE_TILE)
            _, vjp = jax.vjp(_s5_params_math, ar_ref[:, sl], ai_ref[:, sl], ldt_ref[:, sl],
                             br_ref[t], bi_ref[t])
            dabr_row = jnp.sum(dabr_ref[:, sl], axis=0, keepdims=True)
            dabi_row = jnp.sum(dabi_ref[:, sl], axis=0, keepdims=True)
            dar, dai, dldt, dbr, dbi = vjp((dabr_row, dabi_row, dbsr_ref[t], dbsi_ref[t]))
            dar_ref[:, sl] = dar
            dai_ref[:, sl] = dai
            dldt_ref[:, sl] = dldt
            dbr_ref[t] = dbr
            dbi_ref[t] = dbi

    rowspec = pl.BlockSpec((None, 1, NS), lambda l: (l, 0, 0))
    row8spec = pl.BlockSpec((None, 8, NS), lambda l: (l, 0, 0))
    bspec = pl.BlockSpec((None, GT, LANES, STATE_TILE), lambda l: (l, 0, 0, 0))
    rs = jax.ShapeDtypeStruct((NL, 1, NS), F32)
    bs = jax.ShapeDtypeStruct((NL, GT, LANES, STATE_TILE), F32)
    return _pcall(body, name="s5_prep_bwd", grid=(NL,), in_specs=[rowspec] * 3 + [bspec] * 2 + [row8spec] * 2 + [bspec] * 2,
                  out_specs=[rowspec] * 3 + [bspec] * 2, out_shape=[rs, rs, rs, bs, bs])(
                      ar, ai, ldt, braw_r, braw_i, dabr, dabi, dbsr, dbsi)


def _to_streams(x, T):
    L, C = x.shape
    return x.reshape(L // T, 8, T // 8, C).transpose(0, 2, 1, 3).reshape(L, C)


def _from_streams(x, T):
    L, C = x.shape
    return x.reshape(L // T, T // 8, 8, C).transpose(0, 2, 1, 3).reshape(L, C)


def _s5_specs(L, SW, T, rev):
    GT = SW // LANES
    nc = L // T
    cidx = (lambda c: nc - 1 - c) if rev else (lambda c: c)
    rows = _s5_tab_rows(T)["ROWS"]
    return dict(
        GT=GT, nc=nc, cidx=cidx,
        chan=pl.BlockSpec((T, LANES), lambda j, c: (cidx(c), j)),
        state=pl.BlockSpec((T, STATE_TILE), lambda j, c: (cidx(c), j)),
        bmat=pl.BlockSpec((None, LANES, STATE_TILE), lambda j, c: (j, 0, 0)),
        cmat=pl.BlockSpec((None, STATE_TILE, LANES), lambda j, c: (j, 0, 0)),
        gmat=pl.BlockSpec((None, LANES, LANES), lambda j, c: (j, 0, 0)),
        cvec=pl.BlockSpec((1, LANES), lambda j, c: (0, j)),
        svec8=pl.BlockSpec((8, STATE_TILE), lambda j, c: (0, j)),
        tab=pl.BlockSpec((rows, STATE_TILE), lambda j, c: (0, j)),
    )


def _s5_fwd(u, sp, T):
    L, SW = u.shape
    S = _s5_specs(L, SW, T, False)
    NS = S["GT"] * STATE_TILE
    R = _s5_tab_rows(T)
    NG = R["NG"]

    def body(u_ref, bsr_ref, bsi_ref, tabr_ref, tabi_ref, cr_ref, ci_ref, d_ref, wg_ref, gb_ref,
             o_ref, sr_ref, si_ref, y_ref, carr_ref, cari_ref):
        @pl.when(pl.program_id(1) == 0)
        def _():
            carr_ref[...] = jnp.zeros_like(carr_ref)
            cari_ref[...] = jnp.zeros_like(cari_ref)

        uv = u_ref[...]
        ub = uv.astype(BF16)
        row8 = lax.broadcasted_iota(jnp.int32, (8, LANES), 0)
        nstrips = STATE_TILE // LANES

        def project(s):
            sl = slice(s * LANES, (s + 1) * LANES)
            return _dot(ub, bsr_ref[:, sl]), _dot(ub, bsi_ref[:, sl])

        ahead = project(0)
        for s in range(nstrips):
            sl = slice(s * LANES, (s + 1) * LANES)
            tab = lambda r0, n=1, sl=sl: (tabr_ref[r0:r0 + n, sl], tabi_ref[r0:r0 + n, sl])
            xr, xi = ahead
            if s + 1 < nstrips:
                ahead = project(s + 1)
            ar, ai = tab(R["POW"])
            lr, li = [xr[0:8]], [xi[0:8]]
            for g in range(1, NG):
                nr, ni = _cmul_add(xr[8 * g:8 * g + 8], xi[8 * g:8 * g + 8], ar, ai, lr[-1], li[-1])
                lr.append(nr)
                li.append(ni)
            yr, yi = lr[-1], li[-1]
            for k, dist in enumerate((1, 2, 4)):
                kr, ki = tab(R["LA"] + k)
                keep = row8 >= dist
                yr, yi = _cmul_add(yr, yi, kr, ki, jnp.where(keep, pltpu.roll(yr, dist, 0), 0.0),
                                   jnp.where(keep, pltpu.roll(yi, dist, 0), 0.0))
            c0r, c0i = carr_ref[7:8, sl], cari_ref[7:8, sl]
            par, pai = tab(R["PA"], 8)
            er, ei = _cmul_add(yr, yi, par, pai, c0r, c0i)
            carr_ref[:, sl] = er
            cari_ref[:, sl] = ei
            first = row8 == 0
            inr = jnp.where(first, c0r, pltpu.roll(er, 1, 0))
            ini = jnp.where(first, c0i, pltpu.roll(ei, 1, 0))
            for g in range(NG):
                pr, pi = tab(R["POW"] + g)
                outr, outi = _cmul_add(lr[g], li[g], pr, pi, inr, ini)
                sr_ref[8 * g:8 * g + 8, sl] = outr
                si_ref[8 * g:8 * g + 8, sl] = outi
        y = (_dot(sr_ref[...].astype(BF16), cr_ref[...]) - _dot(si_ref[...].astype(BF16), ci_ref[...])
             + d_ref[...] * uv)
        y_ref[...] = y
        yg = jax.nn.gelu(y)
        gate = jax.nn.sigmoid(_dot(yg.astype(BF16), wg_ref[...]) + gb_ref[...])
        o_ref[...] = yg * gate

    cs = jax.ShapeDtypeStruct((L, SW), F32)
    ss = jax.ShapeDtypeStruct((L, NS), F32)
    return _pcall(
        body, name="s5_fwd", grid=(S["GT"], S["nc"]),
        in_specs=[S["chan"], S["bmat"], S["bmat"], S["tab"], S["tab"], S["cmat"], S["cmat"],
                  S["cvec"], S["gmat"], S["cvec"]],
        out_specs=[S["chan"], S["state"], S["state"], S["chan"]], out_shape=[cs, ss, ss, cs],
        scratch=[pltpu.VMEM((8, STATE_TILE), F32), pltpu.VMEM((8, STATE_TILE), F32)],
    )(u, sp["bsr"], sp["bsi"], sp["tabr"], sp["tabi"], sp["crT"], sp["ciT"], sp["d"], sp["wg"], sp["gb"])


def _s5_bwd(do, u, s_re, s_im, ypre, sp, T):
    L, SW = u.shape
    S = _s5_specs(L, SW, T, True)
    GT, nc, cidx = S["GT"], S["nc"], S["cidx"]
    NS = GT * STATE_TILE
    R = _s5_tab_rows(T)
    NG = R["NG"]
    T8 = T // 8

    def body(do_ref, u_ref, sr_ref, si_ref, hr_ref, hi_ref, y_ref, bsr_ref, bsi_ref, tabr_ref, tabi_ref,
             cr_ref, ci_ref, d_ref, wg_ref, gb_ref,
             du_ref, dbsr_ref, dbsi_ref, dcr_ref, dci_ref, dd_ref, dwg_ref, dgb_ref, dar_ref, dai_ref,
             carr_ref, cari_ref, lam_r, lam_i):
        c = pl.program_id(1)
        first = c == 0

        @pl.when(first)
        def _():
            carr_ref[...] = jnp.zeros_like(carr_ref)
            cari_ref[...] = jnp.zeros_like(cari_ref)

        u = u_ref[...]
        ub = u.astype(BF16)
        y = y_ref[...]
        yg, gelu_vjp = jax.vjp(jax.nn.gelu, y)
        ygb = yg.astype(BF16)
        gate = jax.nn.sigmoid(_dot(ygb, wg_ref[...]) + gb_ref[...])
        dout = do_ref[...]
        dt = dout * yg * gate * (1.0 - gate)
        dtb = dt.astype(BF16)
        dyg = dout * gate + _dot(dtb, wg_ref[...], NT)
        (dy,) = gelu_vjp(dyg)
        dyb = dy.astype(BF16)
        _accumulate(dwg_ref, _dot(ygb, dtb, TN), first)
        _accumulate(dgb_ref, jnp.sum(dt, axis=0, keepdims=True), first)
        _accumulate(dd_ref, jnp.sum(dy * u, axis=0, keepdims=True), first)
        _accumulate(dcr_ref, _dot(sr_ref[...].astype(BF16), dyb, TN), first)
        _accumulate(dci_ref, -_dot(si_ref[...].astype(BF16), dyb, TN), first)
        earliest = cidx(c) == 0
        row8 = lax.broadcasted_iota(jnp.int32, (8, LANES), 0)
        nstrips = STATE_TILE // LANES

        def project(s):
            sl = slice(s * LANES, (s + 1) * LANES)
            return _dot(dyb, cr_ref[sl, :], NT), -_dot(dyb, ci_ref[sl, :], NT)

        ahead = project(0)
        for s in range(nstrips):
            sl = slice(s * LANES, (s + 1) * LANES)
            tab = lambda r0, n=1, sl=sl: (tabr_ref[r0:r0 + n, sl], -tabi_ref[r0:r0 + n, sl])
            xr, xi = ahead
            if s + 1 < nstrips:
                ahead = project(s + 1)
            ar, ai = tab(R["POW"])
            lr, li = [None] * NG, [None] * NG
            lr[NG - 1], li[NG - 1] = xr[8 * (NG - 1):8 * NG], xi[8 * (NG - 1):8 * NG]
            for g in range(NG - 2, -1, -1):
                lr[g], li[g] = _cmul_add(xr[8 * g:8 * g + 8], xi[8 * g:8 * g + 8], ar, ai, lr[g + 1], li[g + 1])
            yr, yi = lr[0], li[0]
            for k, dist in enumerate((1, 2, 4)):
                kr, ki = tab(R["LA"] + k)
                keep = row8 < 8 - dist
                yr, yi = _cmul_add(yr, yi, kr, ki, jnp.where(keep, pltpu.roll(yr, 8 - dist, 0), 0.0),
                                   jnp.where(keep, pltpu.roll(yi, 8 - dist, 0), 0.0))
            c0r, c0i = carr_ref[0:1, sl], cari_ref[0:1, sl]
            par, pai = tab(R["PAR"], 8)
            er, ei = _cmul_add(yr, yi, par, pai, c0r, c0i)
            carr_ref[:, sl] = er
            cari_ref[:, sl] = ei
            last = row8 == 7
            inr = jnp.where(last, c0r, pltpu.roll(er, 7, 0))
            ini = jnp.where(last, c0i, pltpu.roll(ei, 7, 0))
            hr0 = jnp.where(earliest, 0.0, hr_ref[7:8, sl])
            hi0 = jnp.where(earliest, 0.0, hi_ref[7:8, sl])
            endr, endi = sr_ref[8 * (NG - 1):8 * NG, sl], si_ref[8 * (NG - 1):8 * NG, sl]
            pvr = jnp.where(row8 == 0, hr0, pltpu.roll(endr, 1, 0))
            pvi = jnp.where(row8 == 0, hi0, pltpu.roll(endi, 1, 0))
            accr = jnp.zeros((8, LANES), F32)
            acci = jnp.zeros((8, LANES), F32)
            for g in range(NG):
                pr, pi = tab(R["POW"] + NG - 1 - g)
                outr, outi = _cmul_add(lr[g], li[g], pr, pi, inr, ini)
                lam_r[8 * g:8 * g + 8, sl] = outr
                lam_i[8 * g:8 * g + 8, sl] = outi
                accr = accr + (outr * pvr + outi * pvi)
                acci = acci + (outi * pvr - outr * pvi)
                pvr, pvi = sr_ref[8 * g:8 * g + 8, sl], si_ref[8 * g:8 * g + 8, sl]

            @pl.when(first)
            def _():
                dar_ref[:, sl] = accr
                dai_ref[:, sl] = acci

            @pl.when(jnp.logical_not(first))
            def _():
                dar_ref[:, sl] += accr
                dai_ref[:, sl] += acci

        lrb = lam_r[...].astype(BF16)
        lib = lam_i[...].astype(BF16)
        _accumulate(dbsr_ref, _dot(ub, lrb, TN), first)
        _accumulate(dbsi_ref, _dot(ub, lib, TN), first)
        du_ref[...] = (dy * d_ref[...] + _dot(lrb, bsr_ref[...], NT) + _dot(lib, bsi_ref[...], NT)).astype(BF16)

    halo = pl.BlockSpec((8, STATE_TILE), lambda j, c: (jnp.maximum(cidx(c) * T8 - 1, 0), j))
    f = lambda *s: jax.ShapeDtypeStruct(s, F32)
    return _pcall(
        body, name="s5_bwd", grid=(GT, nc),
        in_specs=[S["chan"], S["chan"], S["state"], S["state"], halo, halo, S["chan"], S["bmat"], S["bmat"],
                  S["tab"], S["tab"], S["cmat"], S["cmat"], S["cvec"], S["gmat"], S["cvec"]],
        out_specs=[S["chan"], S["bmat"], S["bmat"], S["cmat"], S["cmat"], S["cvec"], S["gmat"], S["cvec"],
                   S["svec8"], S["svec8"]],
        out_shape=[jax.ShapeDtypeStruct((L, SW), BF16), f(GT, LANES, STATE_TILE), f(GT, LANES, STATE_TILE), f(GT, STATE_TILE, LANES),
                   f(GT, STATE_TILE, LANES), f(1, SW), f(GT, LANES, LANES), f(1, SW), f(8, NS), f(8, NS)],
        scratch=[pltpu.VMEM((8, STATE_TILE), F32), pltpu.VMEM((8, STATE_TILE), F32),
                 pltpu.VMEM((T, STATE_TILE), F32), pltpu.VMEM((T, STATE_TILE), F32)],
    )(do, u, s_re, s_im, s_re, s_im, ypre, sp["bsr"], sp["bsi"], sp["tabr"], sp["tabi"],
      sp["crT"], sp["ciT"], sp["d"], sp["wg"], sp["gb"])


def _conv_taps(xc, h6, h7, row):
    x1 = jnp.where(row == 0, h7, pltpu.roll(xc, 1, 0))
    x2 = jnp.where(row == 0, h6, jnp.where(row == 1, h7, pltpu.roll(xc, 2, 0)))
    return x1, x2


def _conv_halves(up_ref, halo_ref, w_ref, b_ref, tm, FS, first):
    row = lax.broadcasted_iota(jnp.int32, (tm, FS), 0)
    outs, taps = [], []
    for s in (0, 1):
        xc = up_ref[s]
        h6 = jnp.where(first, 0.0, halo_ref[s, 6:7, :])
        h7 = jnp.where(first, 0.0, halo_ref[s, 7:8, :])
        x1, x2 = _conv_taps(xc, h6, h7, row)
        outs.append(b_ref[s] + x2 * w_ref[s, 0:1, :] + x1 * w_ref[s, 1:2, :] + xc * w_ref[s, 2:3, :])
        taps.append((x2, x1, xc))
    return outs, taps


def _convglu_specs(L, FS, tm, rev=False):
    t8 = tm // 8
    nt = L // tm
    tile = (lambda i: nt - 1 - i) if rev else (lambda i: i)
    return dict(
        up=pl.BlockSpec((2, None, tm, FS), lambda j, i: (0, j, tile(i), 0)),
        halo=pl.BlockSpec((2, None, 8, FS), lambda j, i: (0, j, jnp.maximum(tile(i) * t8 - 1, 0), 0)),
        w=pl.BlockSpec((2, None, 3, FS), lambda j, i: (0, j, 0, 0)),
        b=pl.BlockSpec((2, None, 1, FS), lambda j, i: (0, j, 0, 0)),
        act=pl.BlockSpec((None, tm, FS), lambda j, i: (j, tile(i), 0)),
    )


def _convglu_fwd(up, w, b):
    _, NSH, L, FS = up.shape
    tm = _row_tile(L)
    S = _convglu_specs(L, FS, tm)

    def body(up_ref, halo_ref, w_ref, b_ref, act_ref):
        (val, gate), _ = _conv_halves(up_ref, halo_ref, w_ref, b_ref, tm, FS, pl.program_id(1) == 0)
        act_ref[...] = _glu_math(val, gate).astype(BF16)

    return _pcall(body, name="convglu_fwd", grid=(NSH, L // tm), in_specs=[S["up"], S["halo"], S["w"], S["b"]],
                  out_specs=S["act"], out_shape=jax.ShapeDtypeStruct((NSH, L, FS), BF16))(up, up, w, b)


def _convglu_bwd(up, dact, w, b):
    _, NSH, L, FS = up.shape
    tm = _row_tile(L)
    nt = L // tm
    S = _convglu_specs(L, FS, tm, rev=True)

    def body(up_ref, halo_ref, w_ref, b_ref, dact_ref, dup_ref, dw_ref, db_ref, after_scr):
        step = pl.program_id(1)
        first = step == 0
        (val, gate), taps = _conv_halves(up_ref, halo_ref, w_ref, b_ref, tm, FS, step == nt - 1)
        _, vjp = jax.vjp(_glu_math, val, gate)
        dcs = vjp(dact_ref[...])
        row = lax.broadcasted_iota(jnp.int32, (tm, FS), 0)
        for s in (0, 1):
            dc = dcs[s]
            n0 = jnp.where(first, 0.0, after_scr[s, 0:1, :])
            n1 = jnp.where(first, 0.0, after_scr[s, 1:2, :])
            x1 = jnp.where(row == tm - 1, n0, pltpu.roll(dc, tm - 1, 0))
            x2 = jnp.where(row == tm - 1, n1, jnp.where(row == tm - 2, n0, pltpu.roll(dc, tm - 2, 0)))
            dup_ref[s] = (dc * w_ref[s, 2:3, :] + x1 * w_ref[s, 1:2, :] + x2 * w_ref[s, 0:1, :]).astype(BF16)
            after_scr[s] = dc[0:8]
            sums = [jnp.sum(dc * t, axis=0, keepdims=True) for t in taps[s]]
            dbs = jnp.sum(dc, axis=0, keepdims=True)

            @pl.when(first)
            def _():
                for t in range(3):
                    dw_ref[s, t:t + 1, :] = sums[t]
                db_ref[s] = dbs

            @pl.when(jnp.logical_not(first))
            def _():
                for t in range(3):
                    dw_ref[s, t:t + 1, :] += sums[t]
                db_ref[s] += dbs

    f = lambda *s: jax.ShapeDtypeStruct(s, F32)
    return _pcall(body, name="convglu_bwd", grid=(NSH, nt),
                  in_specs=[S["up"], S["halo"], S["w"], S["b"], S["act"]],
                  out_specs=[S["up"], S["w"], S["b"]],
                  out_shape=[jax.ShapeDtypeStruct((2, NSH, L, FS), BF16), f(2, NSH, 3, FS), f(2, NSH, 1, FS)],
                  scratch=[pltpu.VMEM((2, 8, FS), F32)])(up, up, w, b, dact)


def _loss_head(y, target):
    L, D = y.shape
    tm = _row_tile(L)

    def body(y_ref, t_ref, loss_ref, dy_ref):
        err = y_ref[...] - t_ref[...]
        dy_ref[...] = err / D
        part = 0.5 * jnp.sum(jnp.mean(err * err, axis=-1, keepdims=True), axis=0, keepdims=True)
        _accumulate(loss_ref, jnp.broadcast_to(part, (1, LANES)), pl.program_id(0) == 0)

    row = pl.BlockSpec((tm, D), lambda i: (i, 0))
    vec = pl.BlockSpec((1, LANES), lambda i: (0, 0))
    return _pcall(body, name="loss_head", grid=(L // tm,), in_specs=[row, row], out_specs=[vec, row],
                  out_shape=[jax.ShapeDtypeStruct((1, LANES), F32), jax.ShapeDtypeStruct((L, D), F32)])(y, target)


def _ada_fwd(c_all, ada_w, ada_b):
    NL, D, NC = ada_w.shape
    NB = c_all.shape[0]

    def body(c_ref, w_ref, b_ref, o_ref):
        cact = jax.nn.silu(c_ref[...])
        o_ref[...] = _dot(cact.astype(BF16), w_ref[...].astype(BF16)) + b_ref[...]

    return _pcall(body, name="ada_fwd", grid=(NL,),
                  in_specs=[pl.BlockSpec((NB, D), lambda l: (0, 0)), pl.BlockSpec((None, D, NC), lambda l: (l, 0, 0)),
                            pl.BlockSpec((None, 1, NC), lambda l: (l, 0, 0))],
                  out_specs=pl.BlockSpec((None, NB, NC), lambda l: (l, 0, 0)),
                  out_shape=jax.ShapeDtypeStruct((NL, NB, NC), F32))(c_all, ada_w, ada_b)


def _ada_bwd(c_all_t, dmod):
    D, NB = c_all_t.shape
    NL, _, NC = dmod.shape

    def body(c_ref, d_ref, o_ref):
        cact = jax.nn.silu(c_ref[...]).astype(BF16).astype(F32)
        o_ref[...] = _dot(cact, d_ref[...].astype(BF16).astype(F32))

    return _pcall(body, name="ada_bwd", grid=(NL,),
                  in_specs=[pl.BlockSpec((D, NB), lambda l: (0, 0)), pl.BlockSpec((None, NB, NC), lambda l: (l, 0, 0))],
                  out_specs=pl.BlockSpec((None, D, NC), lambda l: (l, 0, 0)),
                  out_shape=jax.ShapeDtypeStruct((NL, D, NC), F32))(c_all_t, dmod)


def _adamw(parts, w, m, v):
    NL, P, R, C = parts.shape
    tr = R if R <= 512 else max(t for t in range(16, 513, 16) if R % t == 0)

    def body(p_ref, w_ref, m_ref, v_ref, g_ref, d_ref, nm_ref, nv_ref):
        g = p_ref[0].astype(F32)
        for k in range(1, P):
            g = g + p_ref[k].astype(F32)
        m2 = ADAM_B1 * m_ref[...] + (1.0 - ADAM_B1) * g
        v2 = ADAM_B2 * v_ref[...] + (1.0 - ADAM_B2) * jnp.square(g)
        m_hat = m2 / (1.0 - ADAM_B1 ** ADAM_STEP)
        v_hat = v2 / (1.0 - ADAM_B2 ** ADAM_STEP)
        g_ref[...] = g
        d_ref[...] = -ADAM_LR * (m_hat / (jnp.sqrt(v_hat) + ADAM_EPS) + ADAM_WD * w_ref[...])
        nm_ref[...] = m2
        nv_ref[...] = v2

    pspec = pl.BlockSpec((None, P, tr, C), lambda l, i: (l, 0, i, 0))
    wspec = pl.BlockSpec((None, tr, C), lambda l, i: (l, i, 0))
    shp = jax.ShapeDtypeStruct((NL, R, C), F32)
    return _pcall(body, name="adamw", grid=(NL, R // tr), in_specs=[pspec, wspec, wspec, wspec],
                  out_specs=[wspec] * 4, out_shape=[shp] * 4)(parts, w, m, v)


def _block_diag(blocks):
    *lead, g, r, c = blocks.shape
    eye = jnp.eye(g, dtype=bool)[:, None, :, None]
    full = jnp.where(eye, blocks[..., :, :, None, :], 0.0)
    return full.reshape(*lead, g * r, g * c)


def _block_diag_extract(m, r, c):
    g = GROUPS_PER_TILE
    m5 = m.reshape(*m.shape[:-2], g, r, g, c)
    eye = jnp.eye(g, dtype=bool)[:, None, :, None]
    return jnp.sum(jnp.where(eye, m5, 0.0), axis=-2)


def kernel(x, c, ada_w, ada_b, norm1_g, w_in, q_norm_g, k_norm_g, ssm_a_re, ssm_a_im, ssm_log_dt, ssm_b_re, ssm_b_im, ssm_c_re, ssm_c_im, ssm_d, glu_w, glu_b, attn_out_g, ssm_out_g, w_out, norm2_g, ffn_w_up, ffn_conv_w, ffn_conv_b, ffn_w_down, loss_target, m_ada_w, m_ada_b, m_norm1_g, m_w_in, m_q_norm_g, m_k_norm_g, m_ssm_a_re, m_ssm_a_im, m_ssm_log_dt, m_ssm_b_re, m_ssm_b_im, m_ssm_c_re, m_ssm_c_im, m_ssm_d, m_glu_w, m_glu_b, m_attn_out_g, m_ssm_out_g, m_w_out, m_norm2_g, m_ffn_w_up, m_ffn_conv_w, m_ffn_conv_b, m_ffn_w_down, v_ada_w, v_ada_b, v_norm1_g, v_w_in, v_q_norm_g, v_k_norm_g, v_ssm_a_re, v_ssm_a_im, v_ssm_log_dt, v_ssm_b_re, v_ssm_b_im, v_ssm_c_re, v_ssm_c_im, v_ssm_d, v_glu_w, v_glu_b, v_attn_out_g, v_ssm_out_g, v_w_out, v_norm2_g, v_ffn_w_up, v_ffn_conv_w, v_ffn_conv_b, v_ffn_w_down):
    weights = dict(ada_w=ada_w, ada_b=ada_b, norm1_g=norm1_g, w_in=w_in, q_norm_g=q_norm_g, k_norm_g=k_norm_g,
                   ssm_a_re=ssm_a_re, ssm_a_im=ssm_a_im, ssm_log_dt=ssm_log_dt, ssm_b_re=ssm_b_re,
                   ssm_b_im=ssm_b_im, ssm_c_re=ssm_c_re, ssm_c_im=ssm_c_im, ssm_d=ssm_d, glu_w=glu_w, glu_b=glu_b,
                   attn_out_g=attn_out_g, ssm_out_g=ssm_out_g, w_out=w_out, norm2_g=norm2_g, ffn_w_up=ffn_w_up,
                   ffn_conv_w=ffn_conv_w, ffn_conv_b=ffn_conv_b, ffn_w_down=ffn_w_down)
    mom_m = dict(ada_w=m_ada_w, ada_b=m_ada_b, norm1_g=m_norm1_g, w_in=m_w_in, q_norm_g=m_q_norm_g,
                 k_norm_g=m_k_norm_g, ssm_a_re=m_ssm_a_re, ssm_a_im=m_ssm_a_im, ssm_log_dt=m_ssm_log_dt,
                 ssm_b_re=m_ssm_b_re, ssm_b_im=m_ssm_b_im, ssm_c_re=m_ssm_c_re, ssm_c_im=m_ssm_c_im, ssm_d=m_ssm_d,
                 glu_w=m_glu_w, glu_b=m_glu_b, attn_out_g=m_attn_out_g, ssm_out_g=m_ssm_out_g, w_out=m_w_out,
                 norm2_g=m_norm2_g, ffn_w_up=m_ffn_w_up, ffn_conv_w=m_ffn_conv_w, ffn_conv_b=m_ffn_conv_b,
                 ffn_w_down=m_ffn_w_down)
    mom_v = dict(ada_w=v_ada_w, ada_b=v_ada_b, norm1_g=v_norm1_g, w_in=v_w_in, q_norm_g=v_q_norm_g,
                 k_norm_g=v_k_norm_g, ssm_a_re=v_ssm_a_re, ssm_a_im=v_ssm_a_im, ssm_log_dt=v_ssm_log_dt,
                 ssm_b_re=v_ssm_b_re, ssm_b_im=v_ssm_b_im, ssm_c_re=v_ssm_c_re, ssm_c_im=v_ssm_c_im, ssm_d=v_ssm_d,
                 glu_w=v_glu_w, glu_b=v_glu_b, attn_out_g=v_attn_out_g, ssm_out_g=v_ssm_out_g, w_out=v_w_out,
                 norm2_g=v_norm2_g, ffn_w_up=v_ffn_w_up, ffn_conv_w=v_ffn_conv_w, ffn_conv_b=v_ffn_conv_b,
                 ffn_w_down=v_ffn_w_down)
    names = list(weights)
    big = ("ada_w", "w_in", "w_out", "ffn_w_up", "ffn_conv_w", "ffn_w_down")
    small = [n for n in names if n not in big]

    x = x[0]
    target = loss_target[0]
    L, D = x.shape
    NL = ada_w.shape[0]
    AW = D // 2
    SW = D - AW
    NH = AW // HEAD_DIM
    HP = AW // LANES
    G = SW // SSM_GROUP
    GT = SW // LANES
    NS = G * SSM_STATE
    NIN = w_in.shape[-1]
    FS = ffn_w_up.shape[-1]
    NSH = NDEV // 2
    NCA = ada_w.shape[-1]
    ROWS_OUT = w_out.shape[1]
    ROWS_DOWN = ffn_w_down.shape[1]
    B_ATT = min(L, 256)
    T_S5 = min(L, 512)
    tm = _row_tile(L, 1024)
    me = _my_index()

    cpad = jnp.reshape(c, (D // LANES, LANES))
    c_all = _gather_small(cpad, "gather_c")
    c_all = c_all.reshape(NDEV, D)
    shards = [w_in.astype(BF16), w_out.astype(BF16), ffn_w_up.astype(BF16), ffn_conv_w, ffn_w_down.astype(BF16)]
    first_gathered = [g[0] for g in _gather_shards([s[0:1] for s in shards], "gather_weights")]

    def layer_weights(g):
        w_in_g, w_out_g, w_up_g, conv_w_g, w_down_g = g
        return dict(w_in=w_in_g, w_out=w_out_g.reshape(D, D), w_up=w_up_g,
                    conv_w=conv_w_g.reshape(2, NSH, 3, FS), w_down=w_down_g.reshape(NSH, 2 * ROWS_DOWN, D))

    conv_b_g = ffn_conv_b.reshape(NL, 2, NSH, 1, FS)

    ada_b_mine = lax.dynamic_slice_in_dim(ada_b, me * NCA, NCA, axis=1).reshape(NL, 1, NCA)
    mod_part = _ada_fwd(c_all, ada_w, ada_b_mine)
    mod_all = _gather_small(mod_part, "gather_mod")
    mod = lax.dynamic_index_in_dim(mod_all, me, axis=2, keepdims=False)
    mod = jnp.transpose(mod, (1, 0, 2)).reshape(NL, N_MOD, 1, D)

    row = lambda a: a.reshape(NL, 1, NS)
    ar_row, ai_row = row(ssm_a_re), row(ssm_a_im)
    ldt_row = row(jnp.broadcast_to(ssm_log_dt[:, :, None], (NL, G, SSM_STATE)))
    tiles = lambda a: a.reshape((NL, GT, GROUPS_PER_TILE) + a.shape[2:])
    braw_r = _block_diag(jnp.swapaxes(tiles(ssm_b_re), -1, -2))
    braw_i = _block_diag(jnp.swapaxes(tiles(ssm_b_im), -1, -2))
    crT = _block_diag(jnp.swapaxes(tiles(ssm_c_re), -1, -2)).astype(BF16)
    ciT = _block_diag(jnp.swapaxes(tiles(ssm_c_im), -1, -2)).astype(BF16)
    wg = _block_diag(tiles(glu_w)).astype(BF16)
    abr, abi, tabr, tabi, bsr, bsi = _s5_prep(ar_row, ai_row, ldt_row, braw_r, braw_i, T_S5)
    s5p = dict(bsr=bsr, bsi=bsi, tabr=tabr, tabi=tabi, crT=crT, ciT=ciT,
               d=ssm_d.reshape(NL, 1, SW), wg=wg, gb=glu_b.reshape(NL, 1, SW))

    gqk = jnp.concatenate([jnp.tile(q_norm_g, (1, NH)), jnp.tile(k_norm_g, (1, NH))], axis=1).reshape(NL, 1, 2 * AW)
    layer_params = dict(
        mod=mod, norm1_g=norm1_g.reshape(NL, 1, D), norm2_g=norm2_g.reshape(NL, 1, D), gqk=gqk,
        ga=attn_out_g.reshape(NL, 1, AW), gs=ssm_out_g.reshape(NL, 1, SW),
        conv_b=conv_b_g, s5=s5p)


    def mm_rows_shards(name, a, b, n, out_dtype=F32):
        K = a.shape[1]
        return _matmul(name, a, b, dims=NN, grid=(L // tm, NDEV),
                       a_spec=pl.BlockSpec((tm, K), lambda i, j: (i, 0)),
                       b_spec=pl.BlockSpec((None, K, n), lambda i, j: (j, 0, 0)),
                       out_shape=jax.ShapeDtypeStruct((L, NDEV * n), out_dtype),
                       out_specs=pl.BlockSpec((tm, n), lambda i, j: (i, j)))

    tn = min(D, 512)

    def resid_epilogue(acc, xres, gate):
        return acc, xres + gate * acc

    def layer_fwd(xin, lp, next_shards):
        sh1, sc1, g1, sh2, sc2, g2 = (lp["mod"][k] for k in range(N_MOD))
        h = _lnmod_fwd(xin, lp["norm1_g"], sh1, sc1)
        p = mm_rows_shards("mm_in", h, lp["w_in"], NIN)
        qk = _qknorm_fwd(p, lp["gqk"], AW)
        o_attn, tot, next_gathered = _attn_fwd(qk, p, AW, B_ATT, next_shards)
        u_st = _to_streams(p[:, 3 * AW:], T_S5)
        o_st, s_re, s_im, ypre = _s5_fwd(u_st, lp["s5"], T_S5)
        o_ssm = _from_streams(o_st, T_S5)
        o = _outnorm_fwd(o_attn, o_ssm, lp["ga"], lp["gs"])
        a1, x_mid = _matmul(
            "mm_out", o, lp["w_out"], dims=NN, grid=(L // tm, D // tn),
            a_spec=pl.BlockSpec((tm, D), lambda i, j: (i, 0)), b_spec=pl.BlockSpec((D, tn), lambda i, j: (0, j)),
            extra=(xin, g1), extra_specs=(pl.BlockSpec((tm, tn), lambda i, j: (i, j)),
                                          pl.BlockSpec((1, tn), lambda i, j: (0, j))),
            epilogue=resid_epilogue,
            out_shape=[jax.ShapeDtypeStruct((L, D), F32)] * 2,
            out_specs=[pl.BlockSpec((tm, tn), lambda i, j: (i, j))] * 2)
        h2 = _lnmod_fwd(x_mid, lp["norm2_g"], sh2, sc2)
        up = _matmul(
            "mm_up", h2, lp["w_up"], dims=NN, grid=(L // tm, NDEV),
            a_spec=pl.BlockSpec((tm, D), lambda i, j: (i, 0)), b_spec=pl.BlockSpec((None, D, FS), lambda i, j: (j, 0, 0)),
            out_shape=jax.ShapeDtypeStruct((NDEV, L, FS), F32),
            out_specs=pl.BlockSpec((None, tm, FS), lambda i, j: (j, i, 0)))
        up = up.reshape(2, NSH, L, FS)
        act = _convglu_fwd(up, lp["conv_w"], lp["conv_b"])
        a2, x_out = _matmul(
            "mm_down", act, lp["w_down"], dims=NN, grid=(L // tm, D // tn, NSH), kaxis=2, acc_shape=(tm, tn),
            a_spec=pl.BlockSpec((None, tm, FS), lambda i, j, k: (k, i, 0)),
            b_spec=pl.BlockSpec((None, FS, tn), lambda i, j, k: (k, 0, j)),
            extra=(x_mid, g2), extra_specs=(pl.BlockSpec((tm, tn), lambda i, j, k: (i, j)),
                                            pl.BlockSpec((1, tn), lambda i, j, k: (0, j))),
            epilogue=resid_epilogue,
            out_shape=[jax.ShapeDtypeStruct((L, D), F32)] * 2,
            out_specs=[pl.BlockSpec((tm, tn), lambda i, j, k: (i, j))] * 2)
        res = dict(x=xin, h=h, p=p, qk=qk, tot=tot, o_attn=o_attn, o_ssm=o_ssm, u_st=u_st, s_re=s_re, s_im=s_im, ypre=ypre,
                   o=o, a1=a1, x_mid=x_mid, h2=h2, up=up, act=act, a2=a2)
        return x_out, res, next_gathered

    per_layer = [jax.tree.map(lambda a: a[l], layer_params) for l in range(NL)]
    y, residuals, gathered = x, [], first_gathered
    for l in range(NL):
        per_layer[l].update(layer_weights(gathered))
        y, res, gathered = layer_fwd(y, per_layer[l], [s[l + 1] for s in shards] if l + 1 < NL else [])
        residuals.append(res)

    loss_row, dy = _loss_head(y, target)
    loss = lax.psum(loss_row[0, 0], ("x", "y", "c"))

    def layer_bwd(dx, args, above):
        lp, r = args
        sh1, sc1, g1, sh2, sc2, g2 = (lp["mod"][k] for k in range(N_MOD))
        da2, dg2 = _gate_bwd(dx, r["a2"], g2)
        dact = _matmul(
            "mm_dact", da2, lp["w_down"], dims=NT, grid=(L // tm, NSH),
            a_spec=pl.BlockSpec((tm, D), lambda i, j: (i, 0)), b_spec=pl.BlockSpec((None, FS, D), lambda i, j: (j, 0, 0)),
            out_shape=jax.ShapeDtypeStruct((NSH, L, FS), F32),
            out_specs=pl.BlockSpec((None, tm, FS), lambda i, j: (j, i, 0)))
        dw_down = _matmul(
            "mm_dw_down", r["act"], da2, dims=TN, grid=(NSH, D // tn, L // tm), kaxis=2, acc_shape=(FS, tn),
            a_spec=pl.BlockSpec((None, tm, FS), lambda j, n, k: (j, k, 0)),
            b_spec=pl.BlockSpec((tm, tn), lambda j, n, k: (k, n)),
            out_shape=jax.ShapeDtypeStruct((NSH, FS, D), BF16),
            out_specs=pl.BlockSpec((None, FS, tn), lambda j, n, k: (j, 0, n)))
        dup4, dcw, dcb = _convglu_bwd(r["up"], dact, lp["conv_w"], lp["conv_b"])
        dup = dup4.reshape(NDEV, L, FS)
        dh2 = _matmul(
            "mm_dh2", dup, lp["w_up"], dims=NT, grid=(L // tm, NDEV), kaxis=1, acc_shape=(tm, D),
            a_spec=pl.BlockSpec((None, tm, FS), lambda i, k: (k, i, 0)),
            b_spec=pl.BlockSpec((None, D, FS), lambda i, k: (k, 0, 0)),
            out_shape=jax.ShapeDtypeStruct((L, D), F32), out_specs=pl.BlockSpec((tm, D), lambda i, k: (i, 0)))
        dw_up = _matmul(
            "mm_dw_up", r["h2"], dup, dims=TN, grid=(NDEV, L // tm), kaxis=1, acc_shape=(D, FS),
            a_spec=pl.BlockSpec((tm, D), lambda j, k: (k, 0)), b_spec=pl.BlockSpec((None, tm, FS), lambda j, k: (j, k, 0)),
            out_shape=jax.ShapeDtypeStruct((NDEV, D, FS), BF16),
            out_specs=pl.BlockSpec((None, D, FS), lambda j, k: (j, 0, 0)))
        dxm, dn2, dsh2, dsc2 = _lnmod_bwd(dh2, r["x_mid"], lp["norm2_g"], sh2, sc2, dx)
        da1, dg1 = _gate_bwd(dxm, r["a1"], g1)
        do = _matmul(
            "mm_do", da1, lp["w_out"], dims=NT, grid=(L // tm, D // tn),
            a_spec=pl.BlockSpec((tm, D), lambda i, j: (i, 0)), b_spec=pl.BlockSpec((tn, D), lambda i, j: (j, 0)),
            out_shape=jax.ShapeDtypeStruct((L, D), F32), out_specs=pl.BlockSpec((tm, tn), lambda i, j: (i, j)))
        dw_out = _matmul(
            "mm_dw_out", r["o"], da1, dims=TN, grid=(D // tn, D // tn, L // tm), kaxis=2, acc_shape=(tn, tn),
            a_spec=pl.BlockSpec((tm, tn), lambda m, n, k: (k, m)), b_spec=pl.BlockSpec((tm, tn), lambda m, n, k: (k, n)),
            out_shape=jax.ShapeDtypeStruct((D, D), BF16), out_specs=pl.BlockSpec((tn, tn), lambda m, n, k: (m, n)))
        doa, dos, dga, dgs = _outnorm_bwd(do, r["o_attn"], r["o_ssm"], lp["ga"], lp["gs"])
        (du_st, dbsr, dbsi, dcr, dci, dd, dwg, dgb, dabr, dabi) = _s5_bwd(
            _to_streams(dos, T_S5), r["u_st"], r["s_re"], r["s_im"], r["ypre"], lp["s5"], T_S5)
        du = _from_streams(du_st, T_S5)
        dq, dk, dv, received = _attn_bwd(r["qk"], r["p"], doa, r["tot"], AW, B_ATT, above)
        dqk, dgqk = _qknorm_bwd(jnp.concatenate([dq, dk], axis=1), r["p"], lp["gqk"], AW)
        dp = jnp.concatenate([dqk, dv.astype(BF16), du], axis=1)
        dh = _matmul(
            "mm_dh", dp, lp["w_in"], dims=NT, grid=(L // tm, NDEV), kaxis=1, acc_shape=(tm, D),
            a_spec=pl.BlockSpec((tm, NIN), lambda i, k: (i, k)), b_spec=pl.BlockSpec((None, D, NIN), lambda i, k: (k, 0, 0)),
            out_shape=jax.ShapeDtypeStruct((L, D), F32), out_specs=pl.BlockSpec((tm, D), lambda i, k: (i, 0)))
        dw_in = _matmul(
            "mm_dw_in", r["h"], dp, dims=TN, grid=(NDEV, L // tm), kaxis=1, acc_shape=(D, NIN),
            a_spec=pl.BlockSpec((tm, D), lambda j, k: (k, 0)), b_spec=pl.BlockSpec((tm, NIN), lambda j, k: (k, j)),
            out_shape=jax.ShapeDtypeStruct((NDEV, D, NIN), BF16),
            out_specs=pl.BlockSpec((None, D, NIN), lambda j, k: (j, 0, 0)))
        dx0, dn1, dsh1, dsc1 = _lnmod_bwd(dh, r["x"], lp["norm1_g"], sh1, sc1, dxm)
        grads = dict(
            dmod=jnp.concatenate([dsh1, dsc1, dg1, dsh2, dsc2, dg2], axis=1), dn1=dn1, dn2=dn2, dgqk=dgqk,
            dga=dga, dgs=dgs, dbsr=dbsr, dbsi=dbsi, dcr=dcr, dci=dci, dd=dd, dwg=dwg, dgb=dgb, dabr=dabr, dabi=dabi,
            dcb=dcb)
        sharded = [dw_in, dw_out.reshape(NDEV, ROWS_OUT, D), dw_up, dcw.reshape(NDEV, 3, FS),
                   dw_down.reshape(NDEV, ROWS_DOWN, D)]
        return dx0, grads, sharded, received

    grad_x, layer_grads, parts, above = dy, [None] * NL, [None] * NL, []
    for l in reversed(range(NL)):
        grad_x, layer_grads[l], sharded, received = layer_bwd(grad_x, (per_layer[l], residuals[l]), above)
        if above:
            parts[l + 1] = received
        above = sharded
    parts[0] = _scatter_direct(above, "scatter_weight_grads")
    gr = jax.tree.map(lambda *a: jnp.stack(a), *layer_grads)

    dar, dai, dldt, dbr_bd, dbi_bd = _s5_prep_bwd(ar_row, ai_row, ldt_row, braw_r, braw_i,
                                                  gr["dabr"], gr["dabi"], gr["dbsr"], gr["dbsi"])
    unt = lambda a: a.reshape((NL, G) + a.shape[3:])
    local = dict(
        ada_b=gr["dmod"].reshape(NL, N_MOD * D),
        norm1_g=gr["dn1"].reshape(NL, D), norm2_g=gr["dn2"].reshape(NL, D),
        q_norm_g=gr["dgqk"].reshape(NL, 2, NH, HEAD_DIM)[:, 0].sum(axis=1),
        k_norm_g=gr["dgqk"].reshape(NL, 2, NH, HEAD_DIM)[:, 1].sum(axis=1),
        ssm_a_re=dar.reshape(NL, G, SSM_STATE), ssm_a_im=dai.reshape(NL, G, SSM_STATE),
        ssm_log_dt=dldt.reshape(NL, G, SSM_STATE).sum(axis=-1),
        ssm_b_re=jnp.swapaxes(unt(_block_diag_extract(dbr_bd, SSM_GROUP, SSM_STATE)), -1, -2),
        ssm_b_im=jnp.swapaxes(unt(_block_diag_extract(dbi_bd, SSM_GROUP, SSM_STATE)), -1, -2),
        ssm_c_re=jnp.swapaxes(unt(_block_diag_extract(gr["dcr"], SSM_STATE, SSM_GROUP)), -1, -2),
        ssm_c_im=jnp.swapaxes(unt(_block_diag_extract(gr["dci"], SSM_STATE, SSM_GROUP)), -1, -2),
        ssm_d=gr["dd"].reshape(NL, G, SSM_GROUP),
        glu_w=unt(_block_diag_extract(gr["dwg"], SSM_GROUP, SSM_GROUP)),
        glu_b=gr["dgb"].reshape(NL, G, SSM_GROUP),
        attn_out_g=gr["dga"].reshape(NL, AW), ssm_out_g=gr["dgs"].reshape(NL, SW),
        ffn_conv_b=gr["dcb"].reshape(NL, 2 * NSH * FS),
    )

    def pack(tree):
        flat = jnp.concatenate([tree[n].reshape(-1) for n in small])
        pad = (-flat.shape[0]) % (512 * LANES)
        return jnp.pad(flat, (0, pad)).reshape(1, -1, LANES)

    (small_parts,) = _gather_shards([pack(local)], "gather_small_grads")
    sg, sd, sm, sv = _adamw(small_parts, pack(weights), pack(mom_m), pack(mom_v))

    def unpack(buf):
        flat = buf.reshape(-1)
        out, off = {}, 0
        for n in small:
            size = weights[n].size
            out[n] = flat[off:off + size].reshape(weights[n].shape)
            off += size
        return out

    ug, ud, um, uv = unpack(sg), unpack(sd), unpack(sm), unpack(sv)
    results = {n: (ug[n], ud[n], um[n], uv[n]) for n in small}

    dmod_all = _gather_small(gr["dmod"].reshape(NL, N_MOD * D), "gather_dmod")
    dmod_mine = lax.dynamic_slice_in_dim(dmod_all, me * NCA, NCA, axis=2)
    d_ada_w = _ada_bwd(jnp.transpose(c_all), jnp.transpose(dmod_mine, (1, 0, 2)))
    results["ada_w"] = tuple(_adamw(d_ada_w[:, None], ada_w, m_ada_w, v_ada_w))

    for a, n in enumerate(("w_in", "w_out", "ffn_w_up", "ffn_conv_w", "ffn_w_down")):
        results[n] = tuple(_adamw(jnp.stack([parts[l][a] for l in range(NL)]), weights[n], mom_m[n], mom_v[n]))

    out = [loss, grad_x[None]]
    for k in range(4):
        out.extend(results[n][k] for n in names)
    return tuple(out)
```

```python
import jax
import jax.numpy as jnp
from jax import lax
from jax.experimental import pallas as pl
from jax.experimental.pallas import tpu as pltpu

F32 = jnp.float32
BF16 = jnp.bfloat16
NDEV = 8
LANES = 128
HEAD_DIM = 64
SSM_GROUP = 16
SSM_STATE = 64
GROUPS_PER_TILE = LANES // SSM_GROUP
STATE_TILE = GROUPS_PER_TILE * SSM_STATE
N_MOD = 6
ATTN_STRIP = 32
EPS = 1e-6
ADAM_LR, ADAM_B1, ADAM_B2, ADAM_EPS, ADAM_WD, ADAM_STEP = 0.001, 0.9, 0.999, 1e-08, 0.01, 10
VMEM_LIMIT = 48 * 1024 * 1024
MESH_IDS = pl.DeviceIdType.MESH

NN = (((1,), (0,)), ((), ()))
NT = (((1,), (1,)), ((), ()))
TN = (((0,), (0,)), ((), ()))


def _dot(a, b, dims=NN):
    return lax.dot_general(a, b, dims, preferred_element_type=F32)


def _pcall(body, *, name, out_shape, in_specs, out_specs, grid=(), scratch=()):
    return pl.pallas_call(
        body, name=name, grid=grid, in_specs=in_specs, out_specs=out_specs, out_shape=out_shape,
        scratch_shapes=list(scratch),
        compiler_params=pltpu.CompilerParams(vmem_limit_bytes=VMEM_LIMIT))


def _row_tile(n, want=512):
    t = min(n, want)
    assert n % t == 0
    return t


def _my_index():
    return 4 * lax.axis_index("x") + 2 * lax.axis_index("y") + lax.axis_index("c")


HBM_SPEC = pl.BlockSpec(memory_space=pltpu.HBM)


def _mesh_place():
    x, y, c = lax.axis_index("x"), lax.axis_index("y"), lax.axis_index("c")
    chips = [(1 - x, y), (x, 1 - y), (1 - x, 1 - y)]
    return x, y, c, chips


def _gather_small(arr, name):
    def body(in_ref, out_ref, send_sems, recv_sems, local_sem):
        x, y, c, _ = _mesh_place()
        me = 4 * x + 2 * y + c
        own = pltpu.make_async_copy(in_ref, out_ref.at[me], local_sem)
        own.start()
        sends, recvs = [], []
        for k in range(1, NDEV):
            px = 1 - x if k & 4 else x
            py = 1 - y if k & 2 else y
            pc = 1 - c if k & 1 else c
            common = dict(send_sem=send_sems.at[k - 1], recv_sem=recv_sems.at[k - 1],
                          device_id=(px, py, pc), device_id_type=MESH_IDS)
            snd = pltpu.make_async_remote_copy(src_ref=in_ref, dst_ref=out_ref.at[me], **common)
            snd.start()
            sends.append(snd)
            recvs.append(pltpu.make_async_remote_copy(
                src_ref=in_ref, dst_ref=out_ref.at[4 * px + 2 * py + pc], **common))
        for r in recvs:
            r.wait_recv()
        for s in sends:
            s.wait_send()
        own.wait()

    return pl.pallas_call(
        body, name=name, out_shape=jax.ShapeDtypeStruct((NDEV,) + arr.shape, arr.dtype),
        in_specs=[HBM_SPEC], out_specs=HBM_SPEC,
        scratch_shapes=[pltpu.SemaphoreType.DMA((NDEV - 1,)), pltpu.SemaphoreType.DMA((NDEV - 1,)),
                        pltpu.SemaphoreType.DMA(())],
    )(arr)


def _all_to_all(srcs, dsts, send_sems, recv_sems, scatter):
    x, y, c, _ = _mesh_place()
    me = 4 * x + 2 * y + c
    pairs = []
    for k in range(1, NDEV):
        px = 1 - x if k & 4 else x
        py = 1 - y if k & 2 else y
        pc = 1 - c if k & 1 else c
        peer = 4 * px + 2 * py + pc
        for a in range(len(srcs)):
            src = srcs[a].at[peer] if scatter else srcs[a]
            common = dict(send_sem=send_sems.at[k - 1, a], recv_sem=recv_sems.at[k - 1, a],
                          device_id=(px, py, pc), device_id_type=MESH_IDS)
            pairs.append((pltpu.make_async_remote_copy(src_ref=src, dst_ref=dsts[a].at[me], **common),
                          pltpu.make_async_remote_copy(src_ref=src, dst_ref=dsts[a].at[peer], **common)))

    def start():
        for send, _ in pairs:
            send.start()

    def wait():
        for _, arrival in pairs:
            arrival.wait_recv()
        for send, _ in pairs:
            send.wait_send()

    return start, wait


def _place_own(outs, arrs, scatter):
    me = _my_index()
    owns = [lax.dynamic_slice_in_dim(a, me, 1, axis=0) if scatter else a[None] for a in arrs]
    return [lax.dynamic_update_slice_in_dim(o, own, me, axis=0) for o, own in zip(outs, owns)]


def _exchange_shapes(arrs, scatter):
    return [jax.ShapeDtypeStruct(a.shape if scatter else (NDEV,) + a.shape, a.dtype) for a in arrs]


def _scatter_direct(arrs, name):
    n = len(arrs)

    def body(*refs):
        start, wait = _all_to_all(refs[:n], refs[n:2 * n], refs[2 * n], refs[2 * n + 1], True)
        start()
        wait()

    outs = pl.pallas_call(
        body, name=name, out_shape=_exchange_shapes(arrs, True), in_specs=[HBM_SPEC] * n, out_specs=[HBM_SPEC] * n,
        scratch_shapes=[pltpu.SemaphoreType.DMA((NDEV - 1, n)), pltpu.SemaphoreType.DMA((NDEV - 1, n))],
    )(*arrs)
    return _place_own(outs, arrs, True)


def _start_by_layer(make, nl):
    for l in range(nl):
        make(l).start()


def _gather_shards(arrs, name):
    n = len(arrs)
    nl = arrs[0].shape[0]

    def body(*refs):
        ins, outs = refs[:n], refs[n:2 * n]
        send_sems, recv_sems = refs[2 * n:]
        x, y, c, chips = _mesh_place()
        dev = lambda px, py, pc: 4 * px + 2 * py + pc

        def copy(k, a, src, block, to, layer=slice(None)):
            return pltpu.make_async_remote_copy(
                src_ref=src.at[layer], dst_ref=outs[a].at[layer, block], send_sem=send_sems.at[k, a],
                recv_sem=recv_sems.at[k, a], device_id=to, device_id_type=MESH_IDS)

        me = dev(x, y, c)
        sent = []
        for a in range(n):
            _start_by_layer(lambda l: copy(0, a, ins[a], me, (x, y, 1 - c), l), nl)
            sent.append(copy(0, a, ins[a], me, (x, y, 1 - c)))
        for j, (px, py) in enumerate(chips):
            for a in range(n):
                _start_by_layer(lambda l: copy(1 + j, a, ins[a], me, (px, py, c), l), nl)
                sent.append(copy(1 + j, a, ins[a], me, (px, py, c)))
        for j, (px, py) in enumerate(chips):
            for a in range(n):
                blk = dev(px, py, c)
                copy(1 + j, a, ins[a], blk, (x, y, c)).wait_recv()
                got = outs[a].at[:, blk]
                _start_by_layer(lambda l: copy(4 + j, a, got, blk, (x, y, 1 - c), l), nl)
                sent.append(copy(4 + j, a, got, blk, (x, y, 1 - c)))
        for a in range(n):
            copy(0, a, ins[a], dev(x, y, 1 - c), (x, y, c)).wait_recv()
        for j, (px, py) in enumerate(chips):
            for a in range(n):
                copy(4 + j, a, ins[a], dev(px, py, 1 - c), (x, y, c)).wait_recv()
        for s in sent:
            s.wait_send()

    out_shape = [jax.ShapeDtypeStruct((a.shape[0], NDEV) + a.shape[1:], a.dtype) for a in arrs]
    outs = pl.pallas_call(
        body, name=name, out_shape=out_shape, in_specs=[HBM_SPEC] * n, out_specs=[HBM_SPEC] * n,
        scratch_shapes=[pltpu.SemaphoreType.DMA((7, n)), pltpu.SemaphoreType.DMA((7, n))],
    )(*arrs)
    me = _my_index()
    return [lax.dynamic_update_slice_in_dim(o, a[:, None], me, axis=1) for o, a in zip(outs, arrs)]


def _matmul(name, a, b, *, dims, grid, a_spec, b_spec, out_shape, out_specs, kaxis=None, acc_shape=None,
            extra=(), extra_specs=(), epilogue=None):
    nk = grid[kaxis] if kaxis is not None else 1
    ne = len(extra)
    multi = isinstance(out_shape, (list, tuple))
    n_out = len(out_shape) if multi else 1

    def body(*refs):
        a_ref, b_ref = refs[0], refs[1]
        ex = refs[2:2 + ne]
        outs = refs[2 + ne:2 + ne + n_out]

        def write(res):
            vals = epilogue(res, *[e[...] for e in ex]) if epilogue is not None else (res,)
            for o, v in zip(outs, vals):
                o[...] = v.astype(o.dtype)

        part = _dot(a_ref[...].astype(BF16), b_ref[...].astype(BF16), dims)
        if nk == 1:
            write(part)
        else:
            acc = refs[-1]
            k = pl.program_id(kaxis)

            @pl.when(k == 0)
            def _():
                acc[...] = part

            @pl.when(k > 0)
            def _():
                acc[...] += part

            @pl.when(k == nk - 1)
            def _():
                write(acc[...])

    scratch = [pltpu.VMEM(acc_shape, F32)] if nk > 1 else []
    return _pcall(body, name=name, grid=grid, in_specs=[a_spec, b_spec, *extra_specs],
                  out_specs=out_specs, out_shape=out_shape, scratch=scratch)(a, b, *extra)


def _rms(x, g):
    inv = lax.rsqrt(jnp.mean(x * x, axis=-1, keepdims=True) + EPS)
    return x * inv * g


def _lnmod_math(x, g, sh, sc):
    return _rms(x, g) * (1.0 + sc) + sh


def _head_sums(x):
    row = lax.broadcasted_iota(jnp.int32, (LANES, LANES), 0) // HEAD_DIM
    col = lax.broadcasted_iota(jnp.int32, (LANES, LANES), 1) // HEAD_DIM
    same_head = jnp.where(row == col, 1.0, 0.0).astype(BF16)
    hi = x.astype(BF16)
    lo = (x - hi.astype(F32)).astype(BF16)
    return _dot(hi, same_head) + _dot(lo, same_head)


def _qkn_inv(p):
    return lax.rsqrt(_head_sums(p * p) / HEAD_DIM + EPS)


def _glu_math(val, gate):
    return jax.nn.gelu(gate) * val


def _accumulate(ref, val, first):
    @pl.when(first)
    def _():
        ref[...] = val

    @pl.when(jnp.logical_not(first))
    def _():
        ref[...] += val


def _lnmod_fwd(x, g, sh, sc):
    L, D = x.shape
    tm = _row_tile(L)

    def body(x_ref, g_ref, sh_ref, sc_ref, h_ref):
        h_ref[...] = _lnmod_math(x_ref[...], g_ref[...], sh_ref[...], sc_ref[...]).astype(BF16)

    row = pl.BlockSpec((tm, D), lambda i: (i, 0))
    vec = pl.BlockSpec((1, D), lambda i: (0, 0))
    return _pcall(body, name="lnmod_fwd", grid=(L // tm,), in_specs=[row, vec, vec, vec], out_specs=row,
                  out_shape=jax.ShapeDtypeStruct((L, D), BF16))(x, g, sh, sc)


def _lnmod_bwd(dh, x, g, sh, sc, dres):
    L, D = x.shape
    tm = _row_tile(L)

    def body(dh_ref, x_ref, g_ref, sh_ref, sc_ref, res_ref, dx_ref, dg_ref, dsh_ref, dsc_ref):
        _, vjp = jax.vjp(_lnmod_math, x_ref[...], g_ref[...], sh_ref[...], sc_ref[...])
        dx, dg, dsh, dsc = vjp(dh_ref[...])
        dx_ref[...] = dx + res_ref[...]
        first = pl.program_id(0) == 0
        _accumulate(dg_ref, dg, first)
        _accumulate(dsh_ref, dsh, first)
        _accumulate(dsc_ref, dsc, first)

    row = pl.BlockSpec((tm, D), lambda i: (i, 0))
    vec = pl.BlockSpec((1, D), lambda i: (0, 0))
    vs = jax.ShapeDtypeStruct((1, D), F32)
    return _pcall(body, name="lnmod_bwd", grid=(L // tm,), in_specs=[row, row, vec, vec, vec, row],
                  out_specs=[row, vec, vec, vec],
                  out_shape=[jax.ShapeDtypeStruct((L, D), F32), vs, vs, vs])(dh, x, g, sh, sc, dres)


def _gate_bwd(dx, a, gate):
    L, D = dx.shape
    tm = _row_tile(L)

    def body(dx_ref, a_ref, g_ref, da_ref, dg_ref):
        dxv = dx_ref[...]
        da_ref[...] = (g_ref[...] * dxv).astype(BF16)
        _accumulate(dg_ref, jnp.sum(dxv * a_ref[...], axis=0, keepdims=True), pl.program_id(0) == 0)

    row = pl.BlockSpec((tm, D), lambda i: (i, 0))
    vec = pl.BlockSpec((1, D), lambda i: (0, 0))
    return _pcall(body, name="gate_bwd", grid=(L // tm,), in_specs=[row, row, vec], out_specs=[row, vec],
                  out_shape=[jax.ShapeDtypeStruct((L, D), BF16), jax.ShapeDtypeStruct((1, D), F32)])(dx, a, gate)


def _qknorm_fwd(p, gqk, AW):
    L = p.shape[0]
    tm = _row_tile(L, 2048)
    ncol = 2 * AW // LANES

    def body(p_ref, g_ref, o_ref):
        p = p_ref[...]
        o_ref[...] = (p * _qkn_inv(p) * g_ref[...]).astype(BF16)

    blk = pl.BlockSpec((tm, LANES), lambda i, j: (i, j))
    vec = pl.BlockSpec((1, LANES), lambda i, j: (0, j))
    return _pcall(body, name="qknorm_fwd", grid=(L // tm, ncol), in_specs=[blk, vec], out_specs=blk,
                  out_shape=jax.ShapeDtypeStruct((L, 2 * AW), BF16))(p, gqk)


def _qknorm_bwd(dqk, p, gqk, AW):
    L = p.shape[0]
    tm = _row_tile(L, 2048)
    ncol = 2 * AW // LANES

    def body(d_ref, p_ref, g_ref, dp_ref, dg_ref):
        p, dy = p_ref[...], d_ref[...]
        inv = _qkn_inv(p)
        gdy = g_ref[...] * dy
        dp_ref[...] = (inv * gdy - p * (inv * inv * inv) * (_head_sums(p * gdy) / HEAD_DIM)).astype(BF16)
        _accumulate(dg_ref, jnp.sum(dy * p * inv, axis=0, keepdims=True), pl.program_id(1) == 0)

    blk = pl.BlockSpec((tm, LANES), lambda j, i: (i, j))
    vec = pl.BlockSpec((1, LANES), lambda j, i: (0, j))
    return _pcall(body, name="qknorm_bwd", grid=(ncol, L // tm), in_specs=[blk, blk, vec], out_specs=[blk, vec],
                  out_shape=[jax.ShapeDtypeStruct((L, 2 * AW), BF16),
                             jax.ShapeDtypeStruct((1, 2 * AW), F32)])(dqk, p, gqk)


def _outnorm_fwd(oa, os_, ga, gs):
    L, AW = oa.shape
    SW = os_.shape[1]
    tm = _row_tile(L)

    def body(oa_ref, os_ref, ga_ref, gs_ref, o_ref):
        o_ref[:, :AW] = _rms(oa_ref[...], ga_ref[...]).astype(BF16)
        o_ref[:, AW:] = _rms(os_ref[...], gs_ref[...]).astype(BF16)

    ra = pl.BlockSpec((tm, AW), lambda i: (i, 0))
    rs = pl.BlockSpec((tm, SW), lambda i: (i, 0))
    va = pl.BlockSpec((1, AW), lambda i: (0, 0))
    vs = pl.BlockSpec((1, SW), lambda i: (0, 0))
    ro = pl.BlockSpec((tm, AW + SW), lambda i: (i, 0))
    return _pcall(body, name="outnorm_fwd", grid=(L // tm,), in_specs=[ra, rs, va, vs], out_specs=ro,
                  out_shape=jax.ShapeDtypeStruct((L, AW + SW), BF16))(oa, os_, ga, gs)


def _outnorm_bwd(do, oa, os_, ga, gs):
    L, AW = oa.shape
    SW = os_.shape[1]
    tm = _row_tile(L)

    def body(do_ref, oa_ref, os_ref, ga_ref, gs_ref, doa_ref, dos_ref, dga_ref, dgs_ref):
        first = pl.program_id(0) == 0
        _, vjp_a = jax.vjp(_rms, oa_ref[...], ga_ref[...])
        doa, dga = vjp_a(do_ref[:, :AW])
        _, vjp_s = jax.vjp(_rms, os_ref[...], gs_ref[...])
        dos, dgs = vjp_s(do_ref[:, AW:])
        doa_ref[...] = doa
        dos_ref[...] = dos
        _accumulate(dga_ref, dga, first)
        _accumulate(dgs_ref, dgs, first)

    ra = pl.BlockSpec((tm, AW), lambda i: (i, 0))
    rs = pl.BlockSpec((tm, SW), lambda i: (i, 0))
    va = pl.BlockSpec((1, AW), lambda i: (0, 0))
    vs = pl.BlockSpec((1, SW), lambda i: (0, 0))
    ro = pl.BlockSpec((tm, AW + SW), lambda i: (i, 0))
    return _pcall(body, name="outnorm_bwd", grid=(L // tm,), in_specs=[ro, ra, rs, va, vs],
                  out_specs=[ra, rs, va, vs],
                  out_shape=[jax.ShapeDtypeStruct((L, AW), F32), jax.ShapeDtypeStruct((L, SW), F32),
                             jax.ShapeDtypeStruct((1, AW), F32), jax.ShapeDtypeStruct((1, SW), F32)])(
                                 do, oa, os_, ga, gs)


def _softplus_neg_abs(z):
    return jnp.log(1.0 + jnp.exp(-jnp.abs(z)))


def _attn_masks(B):
    row = lax.broadcasted_iota(jnp.int32, (B, B), 0)
    col = lax.broadcasted_iota(jnp.int32, (B, B), 1)
    strict = col < row
    upper = jnp.where(row > col, 1.0, 0.0).astype(BF16)
    lower = jnp.where(row < col, 1.0, 0.0).astype(BF16)
    return strict, upper, lower


def _ride_along(ride, scatter, HP, nb):
    n = len(ride)
    sems = [pltpu.SemaphoreType.DMA((NDEV - 1, n)), pltpu.SemaphoreType.DMA((NDEV - 1, n))] if n else []

    def hooks(srcs, dsts, send_sems, recv_sems):
        start, wait = _all_to_all(srcs, dsts, send_sems, recv_sems, scatter)
        hp, i = pl.program_id(0), pl.program_id(1)
        return (hp == 0) & (i == 0), start, (hp == HP - 1) & (i == nb - 1), wait

    return [HBM_SPEC] * n, _exchange_shapes(ride, scatter), sems, hooks


def _attn_fwd(qk, p, AW, B, ride=()):
    L = qk.shape[0]
    HP = AW // LANES
    nb = L // B
    n = len(ride)
    ride_specs, ride_shapes, ride_sems, ride_hooks = _ride_along(ride, False, HP, nb)

    def body(*refs):
        q_ref, k_ref, v_ref = refs[:3]
        o_ref, tot_ref = refs[3 + n:5 + n]
        scr = refs[5 + 2 * n:]
        if n:
            first, start, last, wait = ride_hooks(refs[3:3 + n], refs[5 + n:5 + 2 * n], scr[10], scr[11])
            pl.when(first)(start)
        lb_scr = (scr[0:2], scr[2:4])
        tail_scr = (scr[4:6], scr[6:8])
        sum_scr = scr[8:10]
        i = pl.program_id(1)
        m0 = lax.broadcasted_iota(jnp.int32, (1, LANES), 1) < HEAD_DIM
        strict, upper, _ = _attn_masks(B)
        q = q_ref[...] * 0.125
        zq = jnp.zeros_like(q)
        qh = (jnp.where(m0, q, zq), jnp.where(m0, zq, q))

        def keys(j):
            start = pl.multiple_of(jnp.maximum(j, 0) * B, B)
            return k_ref[pl.ds(start, B), :]

        def vals(j):
            start = pl.multiple_of(jnp.maximum(j, 0) * B, B)
            return v_ref[pl.ds(start, B), :].astype(BF16)

        strips = [slice(s, min(s + ATTN_STRIP, B)) for s in range(0, B, ATTN_STRIP)]

        def scores(j):
            kj = keys(j)
            return tuple(_dot(qh[h], kj, NT) for h in (0, 1))

        def logits(zs2, slot, diag):
            for h in (0, 1):
                z = zs2[h]
                his = []
                for rows in strips:
                    zs = z[rows]
                    lb = jnp.minimum(zs, 0.0) - _softplus_neg_abs(zs)
                    l1 = lb - zs
                    if diag:
                        l1 = jnp.where(strict[rows], l1, 0.0)
                    lb_scr[slot][h][rows, :] = lb
                    his.append(l1.astype(BF16))
                    rsum = jnp.sum(l1, axis=-1, keepdims=True)
                    if h == 0:
                        sum_scr[slot][rows, :] = jnp.broadcast_to(rsum, (rows.stop - rows.start, LANES))
                    else:
                        sum_scr[slot][rows, :] = jnp.where(m0, sum_scr[slot][rows, :], rsum)
                cat = lambda xs: jnp.concatenate(xs, axis=0)
                tail_scr[slot][h][...] = _dot(cat(his), upper)

        def attend(j, slot, diag):
            vj = vals(j)
            pv = []
            for h in (0, 1):
                ws = []
                for rows in strips:
                    w = jnp.exp(lb_scr[slot][h][rows, :] + tail_scr[slot][h][rows, :])
                    if diag:
                        w = jnp.where(strict[rows], w, 0.0)
                    ws.append(w.astype(BF16))
                pv.append(_dot(jnp.concatenate(ws, axis=0), vj))
            return jnp.where(m0, pv[0], pv[1])

        logits(scores(i), 0, True)
        o_ref[...] = attend(i, 0, True)
        tot_ref[...] = sum_scr[0][...]

        def half(j, slot):
            z = scores(j - 1)
            pv = attend(j, slot, False)
            logits(z, 1 - slot, False)
            o_ref[...] += pv * jnp.exp(tot_ref[...])
            tot_ref[...] += sum_scr[slot][...]

        @pl.when(i > 0)
        def _():
            logits(scores(i - 1), 1, False)

            @pl.loop(0, (i + 1) // 2)
            def _(t):
                j = i - 1 - 2 * t
                half(j, 1)

                @pl.when(j > 0)
                def _():
                    half(j - 1, 0)

        if n:
            pl.when(last)(wait)

    qspec = pl.BlockSpec((B, LANES), lambda hp, i: (i, hp))
    kspec = pl.BlockSpec((L, LANES), lambda hp, i: (0, HP + hp))
    vspec = pl.BlockSpec((L, LANES), lambda hp, i: (0, 2 * HP + hp))
    ospec = pl.BlockSpec((B, LANES), lambda hp, i: (i, hp))
    shp = jax.ShapeDtypeStruct((L, AW), F32)
    o, tot, *gathered = _pcall(
        body, name="attn_fwd", grid=(HP, nb), in_specs=[qspec, kspec, vspec] + ride_specs,
        out_specs=[ospec, ospec] + ride_specs, out_shape=[shp, shp] + ride_shapes,
        scratch=[pltpu.VMEM((B, B), F32)] * 8 + [pltpu.VMEM((B, LANES), F32)] * 2 + ride_sems)(qk, qk, p, *ride)
    return o, tot, _place_own(gathered, ride, False)


def _attn_bwd(qk, p, do, tot, AW, B, ride=()):
    L = qk.shape[0]
    HP = AW // LANES
    nb = L // B
    n = len(ride)
    ride_specs, ride_shapes, ride_sems, ride_hooks = _ride_along(ride, True, HP, nb)

    def body(*refs):
        q_ref, k_ref, v_ref, do_ref, tot_ref = refs[:5]
        dq_ref, dk_ref, dv_ref = refs[5 + n:8 + n]
        scr = refs[8 + 2 * n:]
        if n:
            first_step, start, last_step, wait = ride_hooks(refs[5:5 + n], refs[8 + n:8 + 2 * n], scr[15], scr[16])
            pl.when(first_step)(start)
        lb_scr, tail_scr, dw_scr, e_scr, beta_scr = scr[0:2], scr[2:4], scr[4:6], scr[6:8], scr[8:10]
        dos_scr, bsum_scr, left_scr, esum_scr, ecum_scr = scr[10:15]
        i = pl.program_id(1)
        m0 = lax.broadcasted_iota(jnp.int32, (1, LANES), 1) < HEAD_DIM
        strict, upper, lower = _attn_masks(B)
        strips = [slice(s, min(s + ATTN_STRIP, B)) for s in range(0, B, ATTN_STRIP)]
        cat = lambda xs: jnp.concatenate(xs, axis=0)

        @pl.when(i == 0)
        def _():
            dk_ref[...] = jnp.zeros_like(dk_ref)
            dv_ref[...] = jnp.zeros_like(dv_ref)

        q = q_ref[...] * 0.125
        zq = jnp.zeros_like(q)
        qh = (jnp.where(m0, q, zq), jnp.where(m0, zq, q))
        heads = lambda a: (jnp.where(m0, a, jnp.zeros_like(a)), jnp.where(m0, jnp.zeros_like(a), a))
        left_scr[...] = jnp.zeros_like(left_scr)
        ecum_scr[...] = jnp.zeros_like(ecum_scr)
        dq_ref[...] = jnp.zeros_like(dq_ref)

        def block_rows(j):
            return pl.ds(pl.multiple_of(j * B, B), B)

        def put_row_sums(ref, rows, h, rsum):
            if h == 0:
                ref[rows, :] = jnp.broadcast_to(rsum, (rows.stop - rows.start, LANES))
            else:
                ref[rows, :] = jnp.where(m0, ref[rows, :], rsum)

        def head_cols(x, h):
            other = pltpu.roll(x, HEAD_DIM, 1)
            full = jnp.where(m0, x, other) if h == 0 else jnp.where(m0, other, x)
            return jnp.concatenate([full] * (B // LANES), axis=1) if B > LANES else full

        def scores(j):
            kj = k_ref[block_rows(j), :]
            return tuple(_dot(qh[h], kj, NT) for h in (0, 1))

        def stage_a(j, zs2, diag):
            for h in (0, 1):
                his = []
                for rows in strips:
                    zs = zs2[h][rows]
                    lb = jnp.minimum(zs, 0.0) - _softplus_neg_abs(zs)
                    l1 = lb - zs
                    if diag:
                        l1 = jnp.where(strict[rows], l1, 0.0)
                    lb_scr[h][rows, :] = lb
                    his.append(l1.astype(BF16))
                    put_row_sums(bsum_scr, rows, h, jnp.sum(l1, axis=-1, keepdims=True))
                tail_scr[h][...] = _dot(cat(his), upper)
            bs = bsum_scr[...]
            scale = jnp.exp(tot_ref[...] - left_scr[...] - bs)
            left_scr[...] += bs
            dos = (do_ref[...] * scale).astype(BF16)
            dos_scr[...] = dos
            vj = v_ref[block_rows(j), :].astype(BF16)
            dosh = heads(dos)
            for h in (0, 1):
                dw_scr[h][...] = _dot(dosh[h], vj, NT)

        def stage_b(j, diag):
            dosh = heads(dos_scr[...])
            pres = []
            dv_blk = jnp.zeros((B, LANES), F32)
            for h in (0, 1):
                ehs, wbs = [], []
                for rows in strips:
                    lb = lb_scr[h][rows, :]
                    w = jnp.exp(lb + tail_scr[h][rows, :])
                    if diag:
                        w = jnp.where(strict[rows], w, 0.0)
                    beta_scr[h][rows, :] = jnp.exp(lb)
                    e = dw_scr[h][rows, :] * w
                    e_scr[h][rows, :] = e
                    ehs.append(e.astype(BF16))
                    wbs.append(w.astype(BF16))
                    put_row_sums(esum_scr, rows, h, jnp.sum(e, axis=-1, keepdims=True))
                pres.append(_dot(cat(ehs), lower))
                dv_blk = dv_blk + _dot(cat(wbs), dosh[h], TN)
            dv_ref[block_rows(j), :] += dv_blk
            return pres

        def stage_c(j, pres, diag):
            kh = heads(k_ref[block_rows(j), :])
            dq = jnp.zeros((B, LANES), F32)
            dk_blk = jnp.zeros((B, LANES), F32)
            for h in (0, 1):
                dzs = []
                for rows in strips:
                    e = e_scr[h][rows, :]
                    dl1 = pres[h][rows] + head_cols(ecum_scr[rows, :], h)
                    dz = e - beta_scr[h][rows, :] * (e + dl1)
                    if diag:
                        dz = jnp.where(strict[rows], dz, 0.0)
                    dzs.append(dz.astype(BF16))
                dzb = cat(dzs)
                dq = dq + _dot(dzb, kh[h])
                dk_blk = dk_blk + _dot(dzb, qh[h], TN)
            dq_ref[...] += dq
            dk_ref[block_rows(j), :] += dk_blk
            ecum_scr[...] += esum_scr[...]

        def step(t, diag_next):
            z = scores(t + 1)
            pres = stage_b(t, False)
            stage_a(t + 1, z, diag_next)
            stage_c(t, pres, False)

        @pl.when(i > 0)
        def _():
            stage_a(0, scores(0), False)

            @pl.loop(0, i - 1)
            def _(t):
                step(t, False)

            step(i - 1, True)

        @pl.when(i == 0)
        def _():
            stage_a(0, scores(0), True)

        stage_c(i, stage_b(i, True), True)
        dq_ref[...] = dq_ref[...] * 0.125
        if n:
            pl.when(last_step)(wait)

    qspec = pl.BlockSpec((B, LANES), lambda hp, i: (i, hp))
    kspec = pl.BlockSpec((L, LANES), lambda hp, i: (0, HP + hp))
    vspec = pl.BlockSpec((L, LANES), lambda hp, i: (0, 2 * HP + hp))
    full = pl.BlockSpec((L, LANES), lambda hp, i: (0, hp))
    shp = jax.ShapeDtypeStruct((L, AW), F32)
    dq, dk, dv, *received = _pcall(
        body, name="attn_bwd", grid=(HP, nb), in_specs=[qspec, kspec, vspec, qspec, qspec] + ride_specs,
        out_specs=[qspec, full, full] + ride_specs, out_shape=[shp, shp, shp] + ride_shapes,
        scratch=[pltpu.VMEM((B, B), F32)] * 10 + [pltpu.VMEM((B, LANES), BF16)]
        + [pltpu.VMEM((B, LANES), F32)] * 4 + ride_sems)(qk, qk, p, do, tot, *ride)
    return dq, dk, dv, _place_own(received, ride, True)


def _s5_disc(ar, ai, ldt):
    dt = jnp.exp(ldt)
    mag = jnp.exp(dt * ar)
    abr = mag * jnp.cos(dt * ai)
    abi = mag * jnp.sin(dt * ai)
    emr = abr - 1.0
    emi = abi
    den = ar * ar + ai * ai
    fr = (emr * ar + emi * ai) / den
    fi = (emi * ar - emr * ai) / den
    return abr, abi, fr, fi


def _s5_params_math(ar, ai, ldt, br, bi):
    abr, abi, fr, fi = _s5_disc(ar, ai, ldt)
    return abr, abi, fr * br - fi * bi, fr * bi + fi * br


def _cmul_add(xr, xi, kr, ki, sr, si):
    return xr + (kr * sr - ki * si), xi + (kr * si + ki * sr)


def _s5_tab_rows(T):
    ng = T // 8
    return dict(NG=ng, POW=0, PA=ng, PAR=ng + 8, LA=ng + 16, ROWS=ng + 24)


def _s5_prep(ar, ai, ldt, braw_r, braw_i, T):
    NL, _, NS = ar.shape
    GT = NS // STATE_TILE
    R = _s5_tab_rows(T)

    def body(ar_ref, ai_ref, ldt_ref, br_ref, bi_ref, abr_ref, abi_ref, tabr_ref, tabi_ref, bsr_ref, bsi_ref):
        for t in range(GT):
            sl = slice(t * STATE_TILE, (t + 1) * STATE_TILE)
            abr, abi, bsr, bsi = _s5_params_math(ar_ref[:, sl], ai_ref[:, sl], ldt_ref[:, sl],
                                                 br_ref[t], bi_ref[t])
            abr_ref[:, sl] = abr
            abi_ref[:, sl] = abi
            bsr_ref[t] = bsr.astype(BF16)
            bsi_ref[t] = bsi.astype(BF16)

            def put(row, vr, vi):
                tabr_ref[row:row + 1, sl] = vr
                tabi_ref[row:row + 1, sl] = vi

            pr, pi = abr, abi
            for g in range(R["NG"]):
                put(R["POW"] + g, pr, pi)
                if g + 1 < R["NG"]:
                    pr, pi = pr * abr - pi * abi, pr * abi + pi * abr
            big_r, big_i = pr, pi
            qr, qi = big_r, big_i
            for r in range(8):
                put(R["PA"] + r, qr, qi)
                put(R["PAR"] + 7 - r, qr, qi)
                qr, qi = qr * big_r - qi * big_i, qr * big_i + qi * big_r
            qr, qi = big_r, big_i
            for k in range(3):
                put(R["LA"] + k, qr, qi)
                qr, qi = qr * qr - qi * qi, 2.0 * qr * qi
            for k in range(3, 8):
                put(R["LA"] + k, jnp.zeros_like(qr), jnp.zeros_like(qi))

    rowspec = pl.BlockSpec((None, 1, NS), lambda l: (l, 0, 0))
    bspec = pl.BlockSpec((None, GT, LANES, STATE_TILE), lambda l: (l, 0, 0, 0))
    tabspec = pl.BlockSpec((None, R["ROWS"], NS), lambda l: (l, 0, 0))
    rs = jax.ShapeDtypeStruct((NL, 1, NS), F32)
    ts = jax.ShapeDtypeStruct((NL, R["ROWS"], NS), F32)
    bs = jax.ShapeDtypeStruct((NL, GT, LANES, STATE_TILE), BF16)
    return _pcall(body, name="s5_prep", grid=(NL,), in_specs=[rowspec] * 3 + [bspec] * 2,
                  out_specs=[rowspec, rowspec, tabspec, tabspec, bspec, bspec],
                  out_shape=[rs, rs, ts, ts, bs, bs])(ar, ai, ldt, braw_r, braw_i)


def _s5_prep_bwd(ar, ai, ldt, braw_r, braw_i, dabr, dabi, dbsr, dbsi):
    NL, _, NS = ar.shape
    GT = NS // STATE_TILE

    def body(ar_ref, ai_ref, ldt_ref, br_ref, bi_ref, dabr_ref, dabi_ref, dbsr_ref, dbsi_ref,
             dar_ref, dai_ref, dldt_ref, dbr_ref, dbi_ref):
        for t in range(GT):
            sl = slice(t * STATE_TILE, (t + 1) * STATE_TILE)
            _, vjp = jax.vjp(_s5_params_math, ar_ref[:, sl], ai_ref[:, sl], ldt_ref[:, sl],
                             br_ref[t], bi_ref[t])
            dabr_row = jnp.sum(dabr_ref[:, sl], axis=0, keepdims=True)
            dabi_row = jnp.sum(dabi_ref[:, sl], axis=0, keepdims=True)
            dar, dai, dldt, dbr, dbi = vjp((dabr_row, dabi_row, dbsr_ref[t], dbsi_ref[t]))
            dar_ref[:, sl] = dar
            dai_ref[:, sl] = dai
            dldt_ref[:, sl] = dldt
            dbr_ref[t] = dbr
            dbi_ref[t] = dbi

    rowspec = pl.BlockSpec((None, 1, NS), lambda l: (l, 0, 0))
    row8spec = pl.BlockSpec((None, 8, NS), lambda l: (l, 0, 0))
    bspec = pl.BlockSpec((None, GT, LANES, STATE_TILE), lambda l: (l, 0, 0, 0))
    rs = jax.ShapeDtypeStruct((NL, 1, NS), F32)
    bs = jax.ShapeDtypeStruct((NL, GT, LANES, STATE_TILE), F32)
    return _pcall(body, name="s5_prep_bwd", grid=(NL,), in_specs=[rowspec] * 3 + [bspec] * 2 + [row8spec] * 2 + [bspec] * 2,
                  out_specs=[rowspec] * 3 + [bspec] * 2, out_shape=[rs, rs, rs, bs, bs])(
                      ar, ai, ldt, braw_r, braw_i, dabr, dabi, dbsr, dbsi)


def _to_streams(x, T):
    L, C = x.shape
    return x.reshape(L // T, 8, T // 8, C).transpose(0, 2, 1, 3).reshape(L, C)


def _from_streams(x, T):
    L, C = x.shape
    return x.reshape(L // T, T // 8, 8, C).transpose(0, 2, 1, 3).reshape(L, C)


def _s5_specs(L, SW, T, rev):
    GT = SW // LANES
    nc = L // T
    cidx = (lambda c: nc - 1 - c) if rev else (lambda c: c)
    rows = _s5_tab_rows(T)["ROWS"]
    return dict(
        GT=GT, nc=nc, cidx=cidx,
        chan=pl.BlockSpec((T, LANES), lambda j, c: (cidx(c), j)),
        state=pl.BlockSpec((T, STATE_TILE), lambda j, c: (cidx(c), j)),
        bmat=pl.BlockSpec((None, LANES, STATE_TILE), lambda j, c: (j, 0, 0)),
        cmat=pl.BlockSpec((None, STATE_TILE, LANES), lambda j, c: (j, 0, 0)),
        gmat=pl.BlockSpec((None, LANES, LANES), lambda j, c: (j, 0, 0)),
        cvec=pl.BlockSpec((1, LANES), lambda j, c: (0, j)),
        svec8=pl.BlockSpec((8, STATE_TILE), lambda j, c: (0, j)),
        tab=pl.BlockSpec((rows, STATE_TILE), lambda j, c: (0, j)),
    )


def _s5_fwd(u, sp, T):
    L, SW = u.shape
    S = _s5_specs(L, SW, T, False)
    NS = S["GT"] * STATE_TILE
    R = _s5_tab_rows(T)
    NG = R["NG"]

    def body(u_ref, bsr_ref, bsi_ref, tabr_ref, tabi_ref, cr_ref, ci_ref, d_ref, wg_ref, gb_ref,
             o_ref, sr_ref, si_ref, y_ref, carr_ref, cari_ref):
        @pl.when(pl.program_id(1) == 0)
        def _():
            carr_ref[...] = jnp.zeros_like(carr_ref)
            cari_ref[...] = jnp.zeros_like(cari_ref)

        uv = u_ref[...]
        ub = uv.astype(BF16)
        row8 = lax.broadcasted_iota(jnp.int32, (8, LANES), 0)
        nstrips = STATE_TILE // LANES

        def project(s):
            sl = slice(s * LANES, (s + 1) * LANES)
            return _dot(ub, bsr_ref[:, sl]), _dot(ub, bsi_ref[:, sl])

        ahead = project(0)
        for s in range(nstrips):
            sl = slice(s * LANES, (s + 1) * LANES)
            tab = lambda r0, n=1, sl=sl: (tabr_ref[r0:r0 + n, sl], tabi_ref[r0:r0 + n, sl])
            xr, xi = ahead
            if s + 1 < nstrips:
                ahead = project(s + 1)
            ar, ai = tab(R["POW"])
            lr, li = [xr[0:8]], [xi[0:8]]
            for g in range(1, NG):
                nr, ni = _cmul_add(xr[8 * g:8 * g + 8], xi[8 * g:8 * g + 8], ar, ai, lr[-1], li[-1])
                lr.append(nr)
                li.append(ni)
            yr, yi = lr[-1], li[-1]
            for k, dist in enumerate((1, 2, 4)):
                kr, ki = tab(R["LA"] + k)
                keep = row8 >= dist
                yr, yi = _cmul_add(yr, yi, kr, ki, jnp.where(keep, pltpu.roll(yr, dist, 0), 0.0),
                                   jnp.where(keep, pltpu.roll(yi, dist, 0), 0.0))
            c0r, c0i = carr_ref[7:8, sl], cari_ref[7:8, sl]
            par, pai = tab(R["PA"], 8)
            er, ei = _cmul_add(yr, yi, par, pai, c0r, c0i)
            carr_ref[:, sl] = er
            cari_ref[:, sl] = ei
            first = row8 == 0
            inr = jnp.where(first, c0r, pltpu.roll(er, 1, 0))
            ini = jnp.where(first, c0i, pltpu.roll(ei, 1, 0))
            for g in range(NG):
                pr, pi = tab(R["POW"] + g)
                outr, outi = _cmul_add(lr[g], li[g], pr, pi, inr, ini)
                sr_ref[8 * g:8 * g + 8, sl] = outr
                si_ref[8 * g:8 * g + 8, sl] = outi
        y = (_dot(sr_ref[...].astype(BF16), cr_ref[...]) - _dot(si_ref[...].astype(BF16), ci_ref[...])
             + d_ref[...] * uv)
        y_ref[...] = y
        yg = jax.nn.gelu(y)
        gate = jax.nn.sigmoid(_dot(yg.astype(BF16), wg_ref[...]) + gb_ref[...])
        o_ref[...] = yg * gate

    cs = jax.ShapeDtypeStruct((L, SW), F32)
    ss = jax.ShapeDtypeStruct((L, NS), F32)
    return _pcall(
        body, name="s5_fwd", grid=(S["GT"], S["nc"]),
        in_specs=[S["chan"], S["bmat"], S["bmat"], S["tab"], S["tab"], S["cmat"], S["cmat"],
                  S["cvec"], S["gmat"], S["cvec"]],
        out_specs=[S["chan"], S["state"], S["state"], S["chan"]], out_shape=[cs, ss, ss, cs],
        scratch=[pltpu.VMEM((8, STATE_TILE), F32), pltpu.VMEM((8, STATE_TILE), F32)],
    )(u, sp["bsr"], sp["bsi"], sp["tabr"], sp["tabi"], sp["crT"], sp["ciT"], sp["d"], sp["wg"], sp["gb"])


def _s5_bwd(do, u, s_re, s_im, ypre, sp, T):
    L, SW = u.shape
    S = _s5_specs(L, SW, T, True)
    GT, nc, cidx = S["GT"], S["nc"], S["cidx"]
    NS = GT * STATE_TILE
    R = _s5_tab_rows(T)
    NG = R["NG"]
    T8 = T // 8

    def body(do_ref, u_ref, sr_ref, si_ref, hr_ref, hi_ref, y_ref, bsr_ref, bsi_ref, tabr_ref, tabi_ref,
             cr_ref, ci_ref, d_ref, wg_ref, gb_ref,
             du_ref, dbsr_ref, dbsi_ref, dcr_ref, dci_ref, dd_ref, dwg_ref, dgb_ref, dar_ref, dai_ref,
             carr_ref, cari_ref, lam_r, lam_i):
        c = pl.program_id(1)
        first = c == 0

        @pl.when(first)
        def _():
            carr_ref[...] = jnp.zeros_like(carr_ref)
            cari_ref[...] = jnp.zeros_like(cari_ref)

        u = u_ref[...]
        ub = u.astype(BF16)
        y = y_ref[...]
        yg, gelu_vjp = jax.vjp(jax.nn.gelu, y)
        ygb = yg.astype(BF16)
        gate = jax.nn.sigmoid(_dot(ygb, wg_ref[...]) + gb_ref[...])
        dout = do_ref[...]
        dt = dout * yg * gate * (1.0 - gate)
        dtb = dt.astype(BF16)
        dyg = dout * gate + _dot(dtb, wg_ref[...], NT)
        (dy,) = gelu_vjp(dyg)
        dyb = dy.astype(BF16)
        _accumulate(dwg_ref, _dot(ygb, dtb, TN), first)
        _accumulate(dgb_ref, jnp.sum(dt, axis=0, keepdims=True), first)
        _accumulate(dd_ref, jnp.sum(dy * u, axis=0, keepdims=True), first)
        _accumulate(dcr_ref, _dot(sr_ref[...].astype(BF16), dyb, TN), first)
        _accumulate(dci_ref, -_dot(si_ref[...].astype(BF16), dyb, TN), first)
        earliest = cidx(c) == 0
        row8 = lax.broadcasted_iota(jnp.int32, (8, LANES), 0)
        nstrips = STATE_TILE // LANES

        def project(s):
            sl = slice(s * LANES, (s + 1) * LANES)
            return _dot(dyb, cr_ref[sl, :], NT), -_dot(dyb, ci_ref[sl, :], NT)

        ahead = project(0)
        for s in range(nstrips):
            sl = slice(s * LANES, (s + 1) * LANES)
            tab = lambda r0, n=1, sl=sl: (tabr_ref[r0:r0 + n, sl], -tabi_ref[r0:r0 + n, sl])
            xr, xi = ahead
            if s + 1 < nstrips:
                ahead = project(s + 1)
            ar, ai = tab(R["POW"])
            lr, li = [None] * NG, [None] * NG
            lr[NG - 1], li[NG - 1] = xr[8 * (NG - 1):8 * NG], xi[8 * (NG - 1):8 * NG]
            for g in range(NG - 2, -1, -1):
                lr[g], li[g] = _cmul_add(xr[8 * g:8 * g + 8], xi[8 * g:8 * g + 8], ar, ai, lr[g + 1], li[g + 1])
            yr, yi = lr[0], li[0]
            for k, dist in enumerate((1, 2, 4)):
                kr, ki = tab(R["LA"] + k)
                keep = row8 < 8 - dist
                yr, yi = _cmul_add(yr, yi, kr, ki, jnp.where(keep, pltpu.roll(yr, 8 - dist, 0), 0.0),
                                   jnp.where(keep, pltpu.roll(yi, 8 - dist, 0), 0.0))
            c0r, c0i = carr_ref[0:1, sl], cari_ref[0:1, sl]
            par, pai = tab(R["PAR"], 8)
            er, ei = _cmul_add(yr, yi, par, pai, c0r, c0i)
            carr_ref[:, sl] = er
            cari_ref[:, sl] = ei
            last = row8 == 7
            inr = jnp.where(last, c0r, pltpu.roll(er, 7, 0))
            ini = jnp.where(last, c0i, pltpu.roll(ei, 7, 0))
            hr0 = jnp.where(earliest, 0.0, hr_ref[7:8, sl])
            hi0 = jnp.where(earliest, 0.0, hi_ref[7:8, sl])
            endr, endi = sr_ref[8 * (NG - 1):8 * NG, sl], si_ref[8 * (NG - 1):8 * NG, sl]
            pvr = jnp.where(row8 == 0, hr0, pltpu.roll(endr, 1, 0))
            pvi = jnp.where(row8 == 0, hi0, pltpu.roll(endi, 1, 0))
            accr = jnp.zeros((8, LANES), F32)
            acci = jnp.zeros((8, LANES), F32)
            for g in range(NG):
                pr, pi = tab(R["POW"] + NG - 1 - g)
                outr, outi = _cmul_add(lr[g], li[g], pr, pi, inr, ini)
                lam_r[8 * g:8 * g + 8, sl] = outr
                lam_i[8 * g:8 * g + 8, sl] = outi
                accr = accr + (outr * pvr + outi * pvi)
                acci = acci + (outi * pvr - outr * pvi)
                pvr, pvi = sr_ref[8 * g:8 * g + 8, sl], si_ref[8 * g:8 * g + 8, sl]

            @pl.when(first)
            def _():
                dar_ref[:, sl] = accr
                dai_ref[:, sl] = acci

            @pl.when(jnp.logical_not(first))
            def _():
                dar_ref[:, sl] += accr
                dai_ref[:, sl] += acci

        lrb = lam_r[...].astype(BF16)
        lib = lam_i[...].astype(BF16)
        _accumulate(dbsr_ref, _dot(ub, lrb, TN), first)
        _accumulate(dbsi_ref, _dot(ub, lib, TN), first)
        du_ref[...] = (dy * d_ref[...] + _dot(lrb, bsr_ref[...], NT) + _dot(lib, bsi_ref[...], NT)).astype(BF16)

    halo = pl.BlockSpec((8, STATE_TILE), lambda j, c: (jnp.maximum(cidx(c) * T8 - 1, 0), j))
    f = lambda *s: jax.ShapeDtypeStruct(s, F32)
    return _pcall(
        body, name="s5_bwd", grid=(GT, nc),
        in_specs=[S["chan"], S["chan"], S["state"], S["state"], halo, halo, S["chan"], S["bmat"], S["bmat"],
                  S["tab"], S["tab"], S["cmat"], S["cmat"], S["cvec"], S["gmat"], S["cvec"]],
        out_specs=[S["chan"], S["bmat"], S["bmat"], S["cmat"], S["cmat"], S["cvec"], S["gmat"], S["cvec"],
                   S["svec8"], S["svec8"]],
        out_shape=[jax.ShapeDtypeStruct((L, SW), BF16), f(GT, LANES, STATE_TILE), f(GT, LANES, STATE_TILE), f(GT, STATE_TILE, LANES),
                   f(GT, STATE_TILE, LANES), f(1, SW), f(GT, LANES, LANES), f(1, SW), f(8, NS), f(8, NS)],
        scratch=[pltpu.VMEM((8, STATE_TILE), F32), pltpu.VMEM((8, STATE_TILE), F32),
                 pltpu.VMEM((T, STATE_TILE), F32), pltpu.VMEM((T, STATE_TILE), F32)],
    )(do, u, s_re, s_im, s_re, s_im, ypre, sp["bsr"], sp["bsi"], sp["tabr"], sp["tabi"],
      sp["crT"], sp["ciT"], sp["d"], sp["wg"], sp["gb"])


def _conv_taps(xc, h6, h7, row):
    x1 = jnp.where(row == 0, h7, pltpu.roll(xc, 1, 0))
    x2 = jnp.where(row == 0, h6, jnp.where(row == 1, h7, pltpu.roll(xc, 2, 0)))
    return x1, x2


def _conv_halves(up_ref, halo_ref, w_ref, b_ref, tm, FS, first):
    row = lax.broadcasted_iota(jnp.int32, (tm, FS), 0)
    outs, taps = [], []
    for s in (0, 1):
        xc = up_ref[s]
        h6 = jnp.where(first, 0.0, halo_ref[s, 6:7, :])
        h7 = jnp.where(first, 0.0, halo_ref[s, 7:8, :])
        x1, x2 = _conv_taps(xc, h6, h7, row)
        outs.append(b_ref[s] + x2 * w_ref[s, 0:1, :] + x1 * w_ref[s, 1:2, :] + xc * w_ref[s, 2:3, :])
        taps.append((x2, x1, xc))
    return outs, taps


def _convglu_specs(L, FS, tm, rev=False):
    t8 = tm // 8
    nt = L // tm
    tile = (lambda i: nt - 1 - i) if rev else (lambda i: i)
    return dict(
        up=pl.BlockSpec((2, None, tm, FS), lambda j, i: (0, j, tile(i), 0)),
        halo=pl.BlockSpec((2, None, 8, FS), lambda j, i: (0, j, jnp.maximum(tile(i) * t8 - 1, 0), 0)),
        w=pl.BlockSpec((2, None, 3, FS), lambda j, i: (0, j, 0, 0)),
        b=pl.BlockSpec((2, None, 1, FS), lambda j, i: (0, j, 0, 0)),
        act=pl.BlockSpec((None, tm, FS), lambda j, i: (j, tile(i), 0)),
    )


def _convglu_fwd(up, w, b):
    _, NSH, L, FS = up.shape
    tm = _row_tile(L)
    S = _convglu_specs(L, FS, tm)

    def body(up_ref, halo_ref, w_ref, b_ref, act_ref):
        (val, gate), _ = _conv_halves(up_ref, halo_ref, w_ref, b_ref, tm, FS, pl.program_id(1) == 0)
        act_ref[...] = _glu_math(val, gate).astype(BF16)

    return _pcall(body, name="convglu_fwd", grid=(NSH, L // tm), in_specs=[S["up"], S["halo"], S["w"], S["b"]],
                  out_specs=S["act"], out_shape=jax.ShapeDtypeStruct((NSH, L, FS), BF16))(up, up, w, b)


def _convglu_bwd(up, dact, w, b):
    _, NSH, L, FS = up.shape
    tm = _row_tile(L)
    nt = L // tm
    S = _convglu_specs(L, FS, tm, rev=True)

    def body(up_ref, halo_ref, w_ref, b_ref, dact_ref, dup_ref, dw_ref, db_ref, after_scr):
        step = pl.program_id(1)
        first = step == 0
        (val, gate), taps = _conv_halves(up_ref, halo_ref, w_ref, b_ref, tm, FS, step == nt - 1)
        _, vjp = jax.vjp(_glu_math, val, gate)
        dcs = vjp(dact_ref[...])
        row = lax.broadcasted_iota(jnp.int32, (tm, FS), 0)
        for s in (0, 1):
            dc = dcs[s]
            n0 = jnp.where(first, 0.0, after_scr[s, 0:1, :])
            n1 = jnp.where(first, 0.0, after_scr[s, 1:2, :])
            x1 = jnp.where(row == tm - 1, n0, pltpu.roll(dc, tm - 1, 0))
            x2 = jnp.where(row == tm - 1, n1, jnp.where(row == tm - 2, n0, pltpu.roll(dc, tm - 2, 0)))
            dup_ref[s] = (dc * w_ref[s, 2:3, :] + x1 * w_ref[s, 1:2, :] + x2 * w_ref[s, 0:1, :]).astype(BF16)
            after_scr[s] = dc[0:8]
            sums = [jnp.sum(dc * t, axis=0, keepdims=True) for t in taps[s]]
            dbs = jnp.sum(dc, axis=0, keepdims=True)

            @pl.when(first)
            def _():
                for t in range(3):
                    dw_ref[s, t:t + 1, :] = sums[t]
                db_ref[s] = dbs

            @pl.when(jnp.logical_not(first))
            def _():
                for t in range(3):
                    dw_ref[s, t:t + 1, :] += sums[t]
                db_ref[s] += dbs

    f = lambda *s: jax.ShapeDtypeStruct(s, F32)
    return _pcall(body, name="convglu_bwd", grid=(NSH, nt),
                  in_specs=[S["up"], S["halo"], S["w"], S["b"], S["act"]],
                  out_specs=[S["up"], S["w"], S["b"]],
                  out_shape=[jax.ShapeDtypeStruct((2, NSH, L, FS), BF16), f(2, NSH, 3, FS), f(2, NSH, 1, FS)],
                  scratch=[pltpu.VMEM((2, 8, FS), F32)])(up, up, w, b, dact)


def _loss_head(y, target):
    L, D = y.shape
    tm = _row_tile(L)

    def body(y_ref, t_ref, loss_ref, dy_ref):
        err = y_ref[...] - t_ref[...]
        dy_ref[...] = err / D
        part = 0.5 * jnp.sum(jnp.mean(err * err, axis=-1, keepdims=True), axis=0, keepdims=True)
        _accumulate(loss_ref, jnp.broadcast_to(part, (1, LANES)), pl.program_id(0) == 0)

    row = pl.BlockSpec((tm, D), lambda i: (i, 0))
    vec = pl.BlockSpec((1, LANES), lambda i: (0, 0))
    return _pcall(body, name="loss_head", grid=(L // tm,), in_specs=[row, row], out_specs=[vec, row],
                  out_shape=[jax.ShapeDtypeStruct((1, LANES), F32), jax.ShapeDtypeStruct((L, D), F32)])(y, target)


def _ada_fwd(c_all, ada_w, ada_b):
    NL, D, NC = ada_w.shape
    NB = c_all.shape[0]

    def body(c_ref, w_ref, b_ref, o_ref):
        cact = jax.nn.silu(c_ref[...])
        o_ref[...] = _dot(cact.astype(BF16), w_ref[...].astype(BF16)) + b_ref[...]

    return _pcall(body, name="ada_fwd", grid=(NL,),
                  in_specs=[pl.BlockSpec((NB, D), lambda l: (0, 0)), pl.BlockSpec((None, D, NC), lambda l: (l, 0, 0)),
                            pl.BlockSpec((None, 1, NC), lambda l: (l, 0, 0))],
                  out_specs=pl.BlockSpec((None, NB, NC), lambda l: (l, 0, 0)),
                  out_shape=jax.ShapeDtypeStruct((NL, NB, NC), F32))(c_all, ada_w, ada_b)


def _ada_bwd(c_all_t, dmod):
    D, NB = c_all_t.shape
    NL, _, NC = dmod.shape

    def body(c_ref, d_ref, o_ref):
        cact = jax.nn.silu(c_ref[...]).astype(BF16).astype(F32)
        o_ref[...] = _dot(cact, d_ref[...].astype(BF16).astype(F32))

    return _pcall(body, name="ada_bwd", grid=(NL,),
                  in_specs=[pl.BlockSpec((D, NB), lambda l: (0, 0)), pl.BlockSpec((None, NB, NC), lambda l: (l, 0, 0))],
                  out_specs=pl.BlockSpec((None, D, NC), lambda l: (l, 0, 0)),
                  out_shape=jax.ShapeDtypeStruct((NL, D, NC), F32))(c_all_t, dmod)


def _adamw(parts, w, m, v):
    NL, P, R, C = parts.shape
    tr = R if R <= 512 else max(t for t in range(16, 513, 16) if R % t == 0)

    def body(p_ref, w_ref, m_ref, v_ref, g_ref, d_ref, nm_ref, nv_ref):
        g = p_ref[0].astype(F32)
        for k in range(1, P):
            g = g + p_ref[k].astype(F32)
        m2 = ADAM_B1 * m_ref[...] + (1.0 - ADAM_B1) * g
        v2 = ADAM_B2 * v_ref[...] + (1.0 - ADAM_B2) * jnp.square(g)
        m_hat = m2 / (1.0 - ADAM_B1 ** ADAM_STEP)
        v_hat = v2 / (1.0 - ADAM_B2 ** ADAM_STEP)
        g_ref[...] = g
        d_ref[...] = -ADAM_LR * (m_hat / (jnp.sqrt(v_hat) + ADAM_EPS) + ADAM_WD * w_ref[...])
        nm_ref[...] = m2
        nv_ref[...] = v2

    pspec = pl.BlockSpec((None, P, tr, C), lambda l, i: (l, 0, i, 0))
    wspec = pl.BlockSpec((None, tr, C), lambda l, i: (l, i, 0))
    shp = jax.ShapeDtypeStruct((NL, R, C), F32)
    return _pcall(body, name="adamw", grid=(NL, R // tr), in_specs=[pspec, wspec, wspec, wspec],
                  out_specs=[wspec] * 4, out_shape=[shp] * 4)(parts, w, m, v)


def _block_diag(blocks):
    *lead, g, r, c = blocks.shape
    eye = jnp.eye(g, dtype=bool)[:, None, :, None]
    full = jnp.where(eye, blocks[..., :, :, None, :], 0.0)
    return full.reshape(*lead, g * r, g * c)


def _block_diag_extract(m, r, c):
    g = GROUPS_PER_TILE
    m5 = m.reshape(*m.shape[:-2], g, r, g, c)
    eye = jnp.eye(g, dtype=bool)[:, None, :, None]
    return jnp.sum(jnp.where(eye, m5, 0.0), axis=-2)


def kernel(x, c, ada_w, ada_b, norm1_g, w_in, q_norm_g, k_norm_g, ssm_a_re, ssm_a_im, ssm_log_dt, ssm_b_re, ssm_b_im, ssm_c_re, ssm_c_im, ssm_d, glu_w, glu_b, attn_out_g, ssm_out_g, w_out, norm2_g, ffn_w_up, ffn_conv_w, ffn_conv_b, ffn_w_down, loss_target, m_ada_w, m_ada_b, m_norm1_g, m_w_in, m_q_norm_g, m_k_norm_g, m_ssm_a_re, m_ssm_a_im, m_ssm_log_dt, m_ssm_b_re, m_ssm_b_im, m_ssm_c_re, m_ssm_c_im, m_ssm_d, m_glu_w, m_glu_b, m_attn_out_g, m_ssm_out_g, m_w_out, m_norm2_g, m_ffn_w_up, m_ffn_conv_w, m_ffn_conv_b, m_ffn_w_down, v_ada_w, v_ada_b, v_norm1_g, v_w_in, v_q_norm_g, v_k_norm_g, v_ssm_a_re, v_ssm_a_im, v_ssm_log_dt, v_ssm_b_re, v_ssm_b_im, v_ssm_c_re, v_ssm_c_im, v_ssm_d, v_glu_w, v_glu_b, v_attn_out_g, v_ssm_out_g, v_w_out, v_norm2_g, v_ffn_w_up, v_ffn_conv_w, v_ffn_conv_b, v_ffn_w_down):
    weights = dict(ada_w=ada_w, ada_b=ada_b, norm1_g=norm1_g, w_in=w_in, q_norm_g=q_norm_g, k_norm_g=k_norm_g,
                   ssm_a_re=ssm_a_re, ssm_a_im=ssm_a_im, ssm_log_dt=ssm_log_dt, ssm_b_re=ssm_b_re,
                   ssm_b_im=ssm_b_im, ssm_c_re=ssm_c_re, ssm_c_im=ssm_c_im, ssm_d=ssm_d, glu_w=glu_w, glu_b=glu_b,
                   attn_out_g=attn_out_g, ssm_out_g=ssm_out_g, w_out=w_out, norm2_g=norm2_g, ffn_w_up=ffn_w_up,
                   ffn_conv_w=ffn_conv_w, ffn_conv_b=ffn_conv_b, ffn_w_down=ffn_w_down)
    mom_m = dict(ada_w=m_ada_w, ada_b=m_ada_b, norm1_g=m_norm1_g, w_in=m_w_in, q_norm_g=m_q_norm_g,
                 k_norm_g=m_k_norm_g, ssm_a_re=m_ssm_a_re, ssm_a_im=m_ssm_a_im, ssm_log_dt=m_ssm_log_dt,
                 ssm_b_re=m_ssm_b_re, ssm_b_im=m_ssm_b_im, ssm_c_re=m_ssm_c_re, ssm_c_im=m_ssm_c_im, ssm_d=m_ssm_d,
                 glu_w=m_glu_w, glu_b=m_glu_b, attn_out_g=m_attn_out_g, ssm_out_g=m_ssm_out_g, w_out=m_w_out,
                 norm2_g=m_norm2_g, ffn_w_up=m_ffn_w_up, ffn_conv_w=m_ffn_conv_w, ffn_conv_b=m_ffn_conv_b,
                 ffn_w_down=m_ffn_w_down)
    mom_v = dict(ada_w=v_ada_w, ada_b=v_ada_b, norm1_g=v_norm1_g, w_in=v_w_in, q_norm_g=v_q_norm_g,
                 k_norm_g=v_k_norm_g, ssm_a_re=v_ssm_a_re, ssm_a_im=v_ssm_a_im, ssm_log_dt=v_ssm_log_dt,
                 ssm_b_re=v_ssm_b_re, ssm_b_im=v_ssm_b_im, ssm_c_re=v_ssm_c_re, ssm_c_im=v_ssm_c_im, ssm_d=v_ssm_d,
                 glu_w=v_glu_w, glu_b=v_glu_b, attn_out_g=v_attn_out_g, ssm_out_g=v_ssm_out_g, w_out=v_w_out,
                 norm2_g=v_norm2_g, ffn_w_up=v_ffn_w_up, ffn_conv_w=v_ffn_conv_w, ffn_conv_b=v_ffn_conv_b,
                 ffn_w_down=v_ffn_w_down)
    names = list(weights)
    big = ("ada_w", "w_in", "w_out", "ffn_w_up", "ffn_conv_w", "ffn_w_down")
    small = [n for n in names if n not in big]

    x = x[0]
    target = loss_target[0]
    L, D = x.shape
    NL = ada_w.shape[0]
    AW = D // 2
    SW = D - AW
    NH = AW // HEAD_DIM
    HP = AW // LANES
    G = SW // SSM_GROUP
    GT = SW // LANES
    NS = G * SSM_STATE
    NIN = w_in.shape[-1]
    FS = ffn_w_up.shape[-1]
    NSH = NDEV // 2
    NCA = ada_w.shape[-1]
    ROWS_OUT = w_out.shape[1]
    ROWS_DOWN = ffn_w_down.shape[1]
    B_ATT = min(L, 256)
    T_S5 = min(L, 1024)
    tm = _row_tile(L, 1024)
    me = _my_index()

    cpad = jnp.reshape(c, (D // LANES, LANES))
    c_all = _gather_small(cpad, "gather_c")
    c_all = c_all.reshape(NDEV, D)
    shards = [w_in.astype(BF16), w_out.astype(BF16), ffn_w_up.astype(BF16), ffn_conv_w, ffn_w_down.astype(BF16)]
    (first_w_in,) = _gather_shards([shards[0][0:1]], "gather_weights")

    def set_w_in(lp, w_in_g):
        lp.update(w_in=w_in_g, w_in_cols=jnp.swapaxes(w_in_g, 0, 1).reshape(D, NDEV * NIN))

    def set_other_weights(lp, gathered):
        w_out_g, w_up_g, conv_w_g, w_down_g = gathered
        lp.update(w_out=w_out_g.reshape(D, D), w_up=w_up_g, conv_w=conv_w_g.reshape(2, NSH, 3, FS),
                  w_down=w_down_g.reshape(NSH, 2 * ROWS_DOWN, D))

    conv_b_g = ffn_conv_b.reshape(NL, 2, NSH, 1, FS)

    ada_b_mine = lax.dynamic_slice_in_dim(ada_b, me * NCA, NCA, axis=1).reshape(NL, 1, NCA)
    mod_part = _ada_fwd(c_all, ada_w, ada_b_mine)
    mod_all = _gather_small(mod_part, "gather_mod")
    mod = lax.dynamic_index_in_dim(mod_all, me, axis=2, keepdims=False)
    mod = jnp.transpose(mod, (1, 0, 2)).reshape(NL, N_MOD, 1, D)

    row = lambda a: a.reshape(NL, 1, NS)
    ar_row, ai_row = row(ssm_a_re), row(ssm_a_im)
    ldt_row = row(jnp.broadcast_to(ssm_log_dt[:, :, None], (NL, G, SSM_STATE)))
    tiles = lambda a: a.reshape((NL, GT, GROUPS_PER_TILE) + a.shape[2:])
    braw_r = _block_diag(jnp.swapaxes(tiles(ssm_b_re), -1, -2))
    braw_i = _block_diag(jnp.swapaxes(tiles(ssm_b_im), -1, -2))
    crT = _block_diag(jnp.swapaxes(tiles(ssm_c_re), -1, -2)).astype(BF16)
    ciT = _block_diag(jnp.swapaxes(tiles(ssm_c_im), -1, -2)).astype(BF16)
    wg = _block_diag(tiles(glu_w)).astype(BF16)
    abr, abi, tabr, tabi, bsr, bsi = _s5_prep(ar_row, ai_row, ldt_row, braw_r, braw_i, T_S5)
    s5p = dict(bsr=bsr, bsi=bsi, tabr=tabr, tabi=tabi, crT=crT, ciT=ciT,
               d=ssm_d.reshape(NL, 1, SW), wg=wg, gb=glu_b.reshape(NL, 1, SW))

    gqk = jnp.concatenate([jnp.tile(q_norm_g, (1, NH)), jnp.tile(k_norm_g, (1, NH))], axis=1).reshape(NL, 1, 2 * AW)
    layer_params = dict(
        mod=mod, norm1_g=norm1_g.reshape(NL, 1, D), norm2_g=norm2_g.reshape(NL, 1, D), gqk=gqk,
        ga=attn_out_g.reshape(NL, 1, AW), gs=ssm_out_g.reshape(NL, 1, SW),
        conv_b=conv_b_g, s5=s5p)


    def mm_rows_shards(name, a, b, n, out_dtype=F32):
        K = a.shape[1]
        return _matmul(name, a, b, dims=NN, grid=(L // tm, NDEV),
                       a_spec=pl.BlockSpec((tm, K), lambda i, j: (i, 0)),
                       b_spec=pl.BlockSpec((None, K, n), lambda i, j: (j, 0, 0)),
                       out_shape=jax.ShapeDtypeStruct((L, NDEV * n), out_dtype),
                       out_specs=pl.BlockSpec((tm, n), lambda i, j: (i, j)))

    tn = min(D, 512)

    def resid_epilogue(acc, xres, gate):
        return acc, xres + gate * acc

    def layer_fwd(xin, lp, ride):
        sh1, sc1, g1, sh2, sc2, g2 = (lp["mod"][k] for k in range(N_MOD))
        h = _lnmod_fwd(xin, lp["norm1_g"], sh1, sc1)
        p = mm_rows_shards("mm_in", h, lp["w_in"], NIN)
        qk = _qknorm_fwd(p, lp["gqk"], AW)
        o_attn, tot, gathered = _attn_fwd(qk, p, AW, B_ATT, ride)
        set_other_weights(lp, gathered[:4])
        next_gathered = gathered[4:]
        u_st = _to_streams(p[:, 3 * AW:], T_S5)
        o_st, s_re, s_im, ypre = _s5_fwd(u_st, lp["s5"], T_S5)
        o_ssm = _from_streams(o_st, T_S5)
        o = _outnorm_fwd(o_attn, o_ssm, lp["ga"], lp["gs"])
        a1, x_mid = _matmul(
            "mm_out", o, lp["w_out"], dims=NN, grid=(L // tm, D // tn),
            a_spec=pl.BlockSpec((tm, D), lambda i, j: (i, 0)), b_spec=pl.BlockSpec((D, tn), lambda i, j: (0, j)),
            extra=(xin, g1), extra_specs=(pl.BlockSpec((tm, tn), lambda i, j: (i, j)),
                                          pl.BlockSpec((1, tn), lambda i, j: (0, j))),
            epilogue=resid_epilogue,
            out_shape=[jax.ShapeDtypeStruct((L, D), F32)] * 2,
            out_specs=[pl.BlockSpec((tm, tn), lambda i, j: (i, j))] * 2)
        h2 = _lnmod_fwd(x_mid, lp["norm2_g"], sh2, sc2)
        up = _matmul(
            "mm_up", h2, lp["w_up"], dims=NN, grid=(L // tm, NDEV),
            a_spec=pl.BlockSpec((tm, D), lambda i, j: (i, 0)), b_spec=pl.BlockSpec((None, D, FS), lambda i, j: (j, 0, 0)),
            out_shape=jax.ShapeDtypeStruct((NDEV, L, FS), F32),
            out_specs=pl.BlockSpec((None, tm, FS), lambda i, j: (j, i, 0)))
        up = up.reshape(2, NSH, L, FS)
        act = _convglu_fwd(up, lp["conv_w"], lp["conv_b"])
        a2, x_out = _matmul(
            "mm_down", act, lp["w_down"], dims=NN, grid=(L // tm, D // tn, NSH), kaxis=2, acc_shape=(tm, tn),
            a_spec=pl.BlockSpec((None, tm, FS), lambda i, j, k: (k, i, 0)),
            b_spec=pl.BlockSpec((None, FS, tn), lambda i, j, k: (k, 0, j)),
            extra=(x_mid, g2), extra_specs=(pl.BlockSpec((tm, tn), lambda i, j, k: (i, j)),
                                            pl.BlockSpec((1, tn), lambda i, j, k: (0, j))),
            epilogue=resid_epilogue,
            out_shape=[jax.ShapeDtypeStruct((L, D), F32)] * 2,
            out_specs=[pl.BlockSpec((tm, tn), lambda i, j, k: (i, j))] * 2)
        res = dict(x=xin, h=h, p=p, qk=qk, tot=tot, o_attn=o_attn, o_ssm=o_ssm, u_st=u_st, s_re=s_re, s_im=s_im, ypre=ypre,
                   o=o, a1=a1, x_mid=x_mid, h2=h2, up=up, act=act, a2=a2)
        return x_out, res, next_gathered

    per_layer = [jax.tree.map(lambda a: a[l], layer_params) for l in range(NL)]
    y, residuals = x, []
    set_w_in(per_layer[0], first_w_in[0])
    for l in range(NL):
        ride = [s[l] for s in shards[1:]] + ([shards[0][l + 1]] if l + 1 < NL else [])
        y, res, next_w_in = layer_fwd(y, per_layer[l], ride)
        if next_w_in:
            set_w_in(per_layer[l + 1], next_w_in[0])
        residuals.append(res)

    loss_row, dy = _loss_head(y, target)
    loss = lax.psum(loss_row[0, 0], ("x", "y", "c"))

    def layer_bwd(dx, args, above):
        lp, r = args
        sh1, sc1, g1, sh2, sc2, g2 = (lp["mod"][k] for k in range(N_MOD))
        da2, dg2 = _gate_bwd(dx, r["a2"], g2)
        dact = _matmul(
            "mm_dact", da2, lp["w_down"], dims=NT, grid=(L // tm, NSH),
            a_spec=pl.BlockSpec((tm, D), lambda i, j: (i, 0)), b_spec=pl.BlockSpec((None, FS, D), lambda i, j: (j, 0, 0)),
            out_shape=jax.ShapeDtypeStruct((NSH, L, FS), F32),
            out_specs=pl.BlockSpec((None, tm, FS), lambda i, j: (j, i, 0)))
        dw_down = _matmul(
            "mm_dw_down", r["act"], da2, dims=TN, grid=(NSH, D // tn, L // tm), kaxis=2, acc_shape=(FS, tn),
            a_spec=pl.BlockSpec((None, tm, FS), lambda j, n, k: (j, k, 0)),
            b_spec=pl.BlockSpec((tm, tn), lambda j, n, k: (k, n)),
            out_shape=jax.ShapeDtypeStruct((NSH, FS, D), BF16),
            out_specs=pl.BlockSpec((None, FS, tn), lambda j, n, k: (j, 0, n)))
        dup4, dcw, dcb = _convglu_bwd(r["up"], dact, lp["conv_w"], lp["conv_b"])
        dup = dup4.reshape(NDEV, L, FS)
        dh2 = _matmul(
            "mm_dh2", dup, lp["w_up"], dims=NT, grid=(L // tm, NDEV), kaxis=1, acc_shape=(tm, D),
            a_spec=pl.BlockSpec((None, tm, FS), lambda i, k: (k, i, 0)),
            b_spec=pl.BlockSpec((None, D, FS), lambda i, k: (k, 0, 0)),
            out_shape=jax.ShapeDtypeStruct((L, D), F32), out_specs=pl.BlockSpec((tm, D), lambda i, k: (i, 0)))
        dw_up = _matmul(
            "mm_dw_up", r["h2"], dup, dims=TN, grid=(NDEV, L // tm), kaxis=1, acc_shape=(D, FS),
            a_spec=pl.BlockSpec((tm, D), lambda j, k: (k, 0)), b_spec=pl.BlockSpec((None, tm, FS), lambda j, k: (j, k, 0)),
            out_shape=jax.ShapeDtypeStruct((NDEV, D, FS), BF16),
            out_specs=pl.BlockSpec((None, D, FS), lambda j, k: (j, 0, 0)))
        dxm, dn2, dsh2, dsc2 = _lnmod_bwd(dh2, r["x_mid"], lp["norm2_g"], sh2, sc2, dx)
        da1, dg1 = _gate_bwd(dxm, r["a1"], g1)
        do = _matmul(
            "mm_do", da1, lp["w_out"], dims=NT, grid=(L // tm, D // tn),
            a_spec=pl.BlockSpec((tm, D), lambda i, j: (i, 0)), b_spec=pl.BlockSpec((tn, D), lambda i, j: (j, 0)),
            out_shape=jax.ShapeDtypeStruct((L, D), F32), out_specs=pl.BlockSpec((tm, tn), lambda i, j: (i, j)))
        dw_out = _matmul(
            "mm_dw_out", r["o"], da1, dims=TN, grid=(D // tn, D // tn, L // tm), kaxis=2, acc_shape=(tn, tn),
            a_spec=pl.BlockSpec((tm, tn), lambda m, n, k: (k, m)), b_spec=pl.BlockSpec((tm, tn), lambda m, n, k: (k, n)),
            out_shape=jax.ShapeDtypeStruct((D, D), BF16), out_specs=pl.BlockSpec((tn, tn), lambda m, n, k: (m, n)))
        doa, dos, dga, dgs = _outnorm_bwd(do, r["o_attn"], r["o_ssm"], lp["ga"], lp["gs"])
        (du_st, dbsr, dbsi, dcr, dci, dd, dwg, dgb, dabr, dabi) = _s5_bwd(
            _to_streams(dos, T_S5), r["u_st"], r["s_re"], r["s_im"], r["ypre"], lp["s5"], T_S5)
        du = _from_streams(du_st, T_S5)
        ride = [dw_out.reshape(NDEV, ROWS_OUT, D), dw_up, dcw.reshape(NDEV, 3, FS),
                dw_down.reshape(NDEV, ROWS_DOWN, D)] + above
        dq, dk, dv, received = _attn_bwd(r["qk"], r["p"], doa, r["tot"], AW, B_ATT, ride)
        dqk, dgqk = _qknorm_bwd(jnp.concatenate([dq, dk], axis=1), r["p"], lp["gqk"], AW)
        dp = jnp.concatenate([dqk, dv.astype(BF16), du], axis=1)
        dh = _matmul(
            "mm_dh", dp, lp["w_in_cols"], dims=NT, grid=(L // tm,),
            a_spec=pl.BlockSpec((tm, NDEV * NIN), lambda i: (i, 0)), b_spec=pl.BlockSpec((D, NDEV * NIN), lambda i: (0, 0)),
            out_shape=jax.ShapeDtypeStruct((L, D), F32), out_specs=pl.BlockSpec((tm, D), lambda i: (i, 0)))
        dw_in = _matmul(
            "mm_dw_in", r["h"], dp, dims=TN, grid=(NDEV, L // tm), kaxis=1, acc_shape=(D, NIN),
            a_spec=pl.BlockSpec((tm, D), lambda j, k: (k, 0)), b_spec=pl.BlockSpec((tm, NIN), lambda j, k: (k, j)),
            out_shape=jax.ShapeDtypeStruct((NDEV, D, NIN), BF16),
            out_specs=pl.BlockSpec((None, D, NIN), lambda j, k: (j, 0, 0)))
        dx0, dn1, dsh1, dsc1 = _lnmod_bwd(dh, r["x"], lp["norm1_g"], sh1, sc1, dxm)
        grads = dict(
            dmod=jnp.concatenate([dsh1, dsc1, dg1, dsh2, dsc2, dg2], axis=1), dn1=dn1, dn2=dn2, dgqk=dgqk,
            dga=dga, dgs=dgs, dbsr=dbsr, dbsi=dbsi, dcr=dcr, dci=dci, dd=dd, dwg=dwg, dgb=dgb, dabr=dabr, dabi=dabi,
            dcb=dcb)
        return dx0, grads, dw_in, received

    grad_x, layer_grads, parts, above = dy, [None] * NL, [[None] * 5 for _ in range(NL)], []
    for l in reversed(range(NL)):
        grad_x, layer_grads[l], dw_in_l, received = layer_bwd(grad_x, (per_layer[l], residuals[l]), above)
        parts[l][1:] = received[:4]
        if above:
            parts[l + 1][0] = received[4]
        above = [dw_in_l]
    (parts[0][0],) = _scatter_direct(above, "scatter_weight_grads")
    gr = jax.tree.map(lambda *a: jnp.stack(a), *layer_grads)

    dar, dai, dldt, dbr_bd, dbi_bd = _s5_prep_bwd(ar_row, ai_row, ldt_row, braw_r, braw_i,
                                                  gr["dabr"], gr["dabi"], gr["dbsr"], gr["dbsi"])
    unt = lambda a: a.reshape((NL, G) + a.shape[3:])
    local = dict(
        ada_b=gr["dmod"].reshape(NL, N_MOD * D),
        norm1_g=gr["dn1"].reshape(NL, D), norm2_g=gr["dn2"].reshape(NL, D),
        q_norm_g=gr["dgqk"].reshape(NL, 2, NH, HEAD_DIM)[:, 0].sum(axis=1),
        k_norm_g=gr["dgqk"].reshape(NL, 2, NH, HEAD_DIM)[:, 1].sum(axis=1),
        ssm_a_re=dar.reshape(NL, G, SSM_STATE), ssm_a_im=dai.reshape(NL, G, SSM_STATE),
        ssm_log_dt=dldt.reshape(NL, G, SSM_STATE).sum(axis=-1),
        ssm_b_re=jnp.swapaxes(unt(_block_diag_extract(dbr_bd, SSM_GROUP, SSM_STATE)), -1, -2),
        ssm_b_im=jnp.swapaxes(unt(_block_diag_extract(dbi_bd, SSM_GROUP, SSM_STATE)), -1, -2),
        ssm_c_re=jnp.swapaxes(unt(_block_diag_extract(gr["dcr"], SSM_STATE, SSM_GROUP)), -1, -2),
        ssm_c_im=jnp.swapaxes(unt(_block_diag_extract(gr["dci"], SSM_STATE, SSM_GROUP)), -1, -2),
        ssm_d=gr["dd"].reshape(NL, G, SSM_GROUP),
        glu_w=unt(_block_diag_extract(gr["dwg"], SSM_GROUP, SSM_GROUP)),
        glu_b=gr["dgb"].reshape(NL, G, SSM_GROUP),
        attn_out_g=gr["dga"].reshape(NL, AW), ssm_out_g=gr["dgs"].reshape(NL, SW),
        ffn_conv_b=gr["dcb"].reshape(NL, 2 * NSH * FS),
    )

    def pack(tree):
        flat = jnp.concatenate([tree[n].reshape(-1) for n in small])
        pad = (-flat.shape[0]) % (512 * LANES)
        return jnp.pad(flat, (0, pad)).reshape(1, -1, LANES)

    (small_parts,) = _gather_shards([pack(local)], "gather_small_grads")
    sg, sd, sm, sv = _adamw(small_parts, pack(weights), pack(mom_m), pack(mom_v))

    def unpack(buf):
        flat = buf.reshape(-1)
        out, off = {}, 0
        for n in small:
            size = weights[n].size
            out[n] = flat[off:off + size].reshape(weights[n].shape)
            off += size
        return out

    ug, ud, um, uv = unpack(sg), unpack(sd), unpack(sm), unpack(sv)
    results = {n: (ug[n], ud[n], um[n], uv[n]) for n in small}

    dmod_all = _gather_small(gr["dmod"].reshape(NL, N_MOD * D), "gather_dmod")
    dmod_mine = lax.dynamic_slice_in_dim(dmod_all, me * NCA, NCA, axis=2)
    d_ada_w = _ada_bwd(jnp.transpose(c_all), jnp.transpose(dmod_mine, (1, 0, 2)))
    results["ada_w"] = tuple(_adamw(d_ada_w[:, None], ada_w, m_ada_w, v_ada_w))

    for a, n in enumerate(("w_in", "w_out", "ffn_w_up", "ffn_conv_w", "ffn_w_down")):
        results[n] = tuple(_adamw(jnp.stack([parts[l][a] for l in range(NL)]), weights[n], mom_m[n], mom_v[n]))

    out = [loss, grad_x[None]]
    for k in range(4):
        out.extend(results[n][k] for n in names)
    return tuple(out)
```

```python
import jax
import jax.numpy as jnp
from jax import lax
from jax.experimental import pallas as pl
from jax.experimental.pallas import tpu as pltpu

F32 = jnp.float32
BF16 = jnp.bfloat16
NDEV = 8
LANES = 128
HEAD_DIM = 64
SSM_GROUP = 16
SSM_STATE = 64
GROUPS_PER_TILE = LANES // SSM_GROUP
STATE_TILE = GROUPS_PER_TILE * SSM_STATE
N_MOD = 6
ATTN_STRIP = 32
EPS = 1e-6
ADAM_LR, ADAM_B1, ADAM_B2, ADAM_EPS, ADAM_WD, ADAM_STEP = 0.001, 0.9, 0.999, 1e-08, 0.01, 10
VMEM_LIMIT = 48 * 1024 * 1024
MESH_IDS = pl.DeviceIdType.MESH

NN = (((1,), (0,)), ((), ()))
NT = (((1,), (1,)), ((), ()))
TN = (((0,), (0,)), ((), ()))


def _dot(a, b, dims=NN):
    return lax.dot_general(a, b, dims, preferred_element_type=F32)


def _pcall(body, *, name, out_shape, in_specs, out_specs, grid=(), scratch=()):
    return pl.pallas_call(
        body, name=name, grid=grid, in_specs=in_specs, out_specs=out_specs, out_shape=out_shape,
        scratch_shapes=list(scratch),
        compiler_params=pltpu.CompilerParams(vmem_limit_bytes=VMEM_LIMIT))


def _row_tile(n, want=512):
    t = min(n, want)
    assert n % t == 0
    return t


def _my_index():
    return 4 * lax.axis_index("x") + 2 * lax.axis_index("y") + lax.axis_index("c")


HBM_SPEC = pl.BlockSpec(memory_space=pltpu.HBM)


def _mesh_place():
    x, y, c = lax.axis_index("x"), lax.axis_index("y"), lax.axis_index("c")
    chips = [(1 - x, y), (x, 1 - y), (1 - x, 1 - y)]
    return x, y, c, chips


def _gather_small(arr, name):
    def body(in_ref, out_ref, send_sems, recv_sems, local_sem):
        x, y, c, _ = _mesh_place()
        me = 4 * x + 2 * y + c
        own = pltpu.make_async_copy(in_ref, out_ref.at[me], local_sem)
        own.start()
        sends, recvs = [], []
        for k in range(1, NDEV):
            px = 1 - x if k & 4 else x
            py = 1 - y if k & 2 else y
            pc = 1 - c if k & 1 else c
            common = dict(send_sem=send_sems.at[k - 1], recv_sem=recv_sems.at[k - 1],
                          device_id=(px, py, pc), device_id_type=MESH_IDS)
            snd = pltpu.make_async_remote_copy(src_ref=in_ref, dst_ref=out_ref.at[me], **common)
            snd.start()
            sends.append(snd)
            recvs.append(pltpu.make_async_remote_copy(
                src_ref=in_ref, dst_ref=out_ref.at[4 * px + 2 * py + pc], **common))
        for r in recvs:
            r.wait_recv()
        for s in sends:
            s.wait_send()
        own.wait()

    return pl.pallas_call(
        body, name=name, out_shape=jax.ShapeDtypeStruct((NDEV,) + arr.shape, arr.dtype),
        in_specs=[HBM_SPEC], out_specs=HBM_SPEC,
        scratch_shapes=[pltpu.SemaphoreType.DMA((NDEV - 1,)), pltpu.SemaphoreType.DMA((NDEV - 1,)),
                        pltpu.SemaphoreType.DMA(())],
    )(arr)


def _all_to_all(srcs, dsts, send_sems, recv_sems, scatter):
    x, y, c, _ = _mesh_place()
    me = 4 * x + 2 * y + c
    pairs = []
    for k in range(1, NDEV):
        px = 1 - x if k & 4 else x
        py = 1 - y if k & 2 else y
        pc = 1 - c if k & 1 else c
        peer = 4 * px + 2 * py + pc
        for a in range(len(srcs)):
            src = srcs[a].at[peer] if scatter else srcs[a]
            common = dict(send_sem=send_sems.at[k - 1, a], recv_sem=recv_sems.at[k - 1, a],
                          device_id=(px, py, pc), device_id_type=MESH_IDS)
            pairs.append((pltpu.make_async_remote_copy(src_ref=src, dst_ref=dsts[a].at[me], **common),
                          pltpu.make_async_remote_copy(src_ref=src, dst_ref=dsts[a].at[peer], **common)))

    def start():
        for send, _ in pairs:
            send.start()

    def wait():
        for _, arrival in pairs:
            arrival.wait_recv()
        for send, _ in pairs:
            send.wait_send()

    return start, wait


def _place_own(outs, arrs, scatter):
    me = _my_index()
    owns = [lax.dynamic_slice_in_dim(a, me, 1, axis=0) if scatter else a[None] for a in arrs]
    return [lax.dynamic_update_slice_in_dim(o, own, me, axis=0) for o, own in zip(outs, owns)]


def _exchange_shapes(arrs, scatter):
    return [jax.ShapeDtypeStruct(a.shape if scatter else (NDEV,) + a.shape, a.dtype) for a in arrs]


def _scatter_direct(arrs, name):
    n = len(arrs)

    def body(*refs):
        start, wait = _all_to_all(refs[:n], refs[n:2 * n], refs[2 * n], refs[2 * n + 1], True)
        start()
        wait()

    outs = pl.pallas_call(
        body, name=name, out_shape=_exchange_shapes(arrs, True), in_specs=[HBM_SPEC] * n, out_specs=[HBM_SPEC] * n,
        scratch_shapes=[pltpu.SemaphoreType.DMA((NDEV - 1, n)), pltpu.SemaphoreType.DMA((NDEV - 1, n))],
    )(*arrs)
    return _place_own(outs, arrs, True)


def _start_by_layer(make, nl):
    for l in range(nl):
        make(l).start()


def _gather_shards(arrs, name):
    n = len(arrs)
    nl = arrs[0].shape[0]

    def body(*refs):
        ins, outs = refs[:n], refs[n:2 * n]
        send_sems, recv_sems = refs[2 * n:]
        x, y, c, chips = _mesh_place()
        dev = lambda px, py, pc: 4 * px + 2 * py + pc

        def copy(k, a, src, block, to, layer=slice(None)):
            return pltpu.make_async_remote_copy(
                src_ref=src.at[layer], dst_ref=outs[a].at[layer, block], send_sem=send_sems.at[k, a],
                recv_sem=recv_sems.at[k, a], device_id=to, device_id_type=MESH_IDS)

        me = dev(x, y, c)
        sent = []
        for a in range(n):
            _start_by_layer(lambda l: copy(0, a, ins[a], me, (x, y, 1 - c), l), nl)
            sent.append(copy(0, a, ins[a], me, (x, y, 1 - c)))
        for j, (px, py) in enumerate(chips):
            for a in range(n):
                _start_by_layer(lambda l: copy(1 + j, a, ins[a], me, (px, py, c), l), nl)
                sent.append(copy(1 + j, a, ins[a], me, (px, py, c)))
        for j, (px, py) in enumerate(chips):
            for a in range(n):
                blk = dev(px, py, c)
                copy(1 + j, a, ins[a], blk, (x, y, c)).wait_recv()
                got = outs[a].at[:, blk]
                _start_by_layer(lambda l: copy(4 + j, a, got, blk, (x, y, 1 - c), l), nl)
                sent.append(copy(4 + j, a, got, blk, (x, y, 1 - c)))
        for a in range(n):
            copy(0, a, ins[a], dev(x, y, 1 - c), (x, y, c)).wait_recv()
        for j, (px, py) in enumerate(chips):
            for a in range(n):
                copy(4 + j, a, ins[a], dev(px, py, 1 - c), (x, y, c)).wait_recv()
        for s in sent:
            s.wait_send()

    out_shape = [jax.ShapeDtypeStruct((a.shape[0], NDEV) + a.shape[1:], a.dtype) for a in arrs]
    outs = pl.pallas_call(
        body, name=name, out_shape=out_shape, in_specs=[HBM_SPEC] * n, out_specs=[HBM_SPEC] * n,
        scratch_shapes=[pltpu.SemaphoreType.DMA((7, n)), pltpu.SemaphoreType.DMA((7, n))],
    )(*arrs)
    me = _my_index()
    return [lax.dynamic_update_slice_in_dim(o, a[:, None], me, axis=1) for o, a in zip(outs, arrs)]


def _matmul(name, a, b, *, dims, grid, a_spec, b_spec, out_shape, out_specs, kaxis=None, acc_shape=None,
            extra=(), extra_specs=(), epilogue=None):
    nk = grid[kaxis] if kaxis is not None else 1
    ne = len(extra)
    multi = isinstance(out_shape, (list, tuple))
    n_out = len(out_shape) if multi else 1

    def body(*refs):
        a_ref, b_ref = refs[0], refs[1]
        ex = refs[2:2 + ne]
        outs = refs[2 + ne:2 + ne + n_out]

        def write(res):
            vals = epilogue(res, *[e[...] for e in ex]) if epilogue is not None else (res,)
            for o, v in zip(outs, vals):
                o[...] = v.astype(o.dtype)

        if len(a_ref.shape) == 3:
            part = _dot(a_ref[0].astype(BF16), b_ref[0].astype(BF16), dims)
            for s in range(1, a_ref.shape[0]):
                part = part + _dot(a_ref[s].astype(BF16), b_ref[s].astype(BF16), dims)
        else:
            part = _dot(a_ref[...].astype(BF16), b_ref[...].astype(BF16), dims)
        if nk == 1:
            write(part)
        else:
            acc = refs[-1]
            k = pl.program_id(kaxis)

            @pl.when(k == 0)
            def _():
                acc[...] = part

            @pl.when(k > 0)
            def _():
                acc[...] += part

            @pl.when(k == nk - 1)
            def _():
                write(acc[...])

    scratch = [pltpu.VMEM(acc_shape, F32)] if nk > 1 else []
    return _pcall(body, name=name, grid=grid, in_specs=[a_spec, b_spec, *extra_specs],
                  out_specs=out_specs, out_shape=out_shape, scratch=scratch)(a, b, *extra)


def _rms(x, g):
    inv = lax.rsqrt(jnp.mean(x * x, axis=-1, keepdims=True) + EPS)
    return x * inv * g


def _lnmod_math(x, g, sh, sc):
    return _rms(x, g) * (1.0 + sc) + sh


def _head_sums(x):
    row = lax.broadcasted_iota(jnp.int32, (LANES, LANES), 0) // HEAD_DIM
    col = lax.broadcasted_iota(jnp.int32, (LANES, LANES), 1) // HEAD_DIM
    same_head = jnp.where(row == col, 1.0, 0.0).astype(BF16)
    hi = x.astype(BF16)
    lo = (x - hi.astype(F32)).astype(BF16)
    return _dot(hi, same_head) + _dot(lo, same_head)


def _qkn_inv(p):
    return lax.rsqrt(_head_sums(p * p) / HEAD_DIM + EPS)


def _glu_math(val, gate):
    return jax.nn.gelu(gate) * val


def _accumulate(ref, val, first):
    @pl.when(first)
    def _():
        ref[...] = val

    @pl.when(jnp.logical_not(first))
    def _():
        ref[...] += val


def _lnmod_fwd(x, g, sh, sc):
    L, D = x.shape
    tm = _row_tile(L)

    def body(x_ref, g_ref, sh_ref, sc_ref, h_ref):
        h_ref[...] = _lnmod_math(x_ref[...], g_ref[...], sh_ref[...], sc_ref[...]).astype(BF16)

    row = pl.BlockSpec((tm, D), lambda i: (i, 0))
    vec = pl.BlockSpec((1, D), lambda i: (0, 0))
    return _pcall(body, name="lnmod_fwd", grid=(L // tm,), in_specs=[row, vec, vec, vec], out_specs=row,
                  out_shape=jax.ShapeDtypeStruct((L, D), BF16))(x, g, sh, sc)


def _lnmod_bwd(dh, x, g, sh, sc, dres):
    L, D = x.shape
    tm = _row_tile(L)

    def body(dh_ref, x_ref, g_ref, sh_ref, sc_ref, res_ref, dx_ref, dg_ref, dsh_ref, dsc_ref):
        _, vjp = jax.vjp(_lnmod_math, x_ref[...], g_ref[...], sh_ref[...], sc_ref[...])
        dx, dg, dsh, dsc = vjp(dh_ref[...])
        dx_ref[...] = dx + res_ref[...]
        first = pl.program_id(0) == 0
        _accumulate(dg_ref, dg, first)
        _accumulate(dsh_ref, dsh, first)
        _accumulate(dsc_ref, dsc, first)

    row = pl.BlockSpec((tm, D), lambda i: (i, 0))
    vec = pl.BlockSpec((1, D), lambda i: (0, 0))
    vs = jax.ShapeDtypeStruct((1, D), F32)
    return _pcall(body, name="lnmod_bwd", grid=(L // tm,), in_specs=[row, row, vec, vec, vec, row],
                  out_specs=[row, vec, vec, vec],
                  out_shape=[jax.ShapeDtypeStruct((L, D), F32), vs, vs, vs])(dh, x, g, sh, sc, dres)


def _gate_bwd(dx, a, gate):
    L, D = dx.shape
    tm = _row_tile(L)

    def body(dx_ref, a_ref, g_ref, da_ref, dg_ref):
        dxv = dx_ref[...]
        da_ref[...] = (g_ref[...] * dxv).astype(BF16)
        _accumulate(dg_ref, jnp.sum(dxv * a_ref[...], axis=0, keepdims=True), pl.program_id(0) == 0)

    row = pl.BlockSpec((tm, D), lambda i: (i, 0))
    vec = pl.BlockSpec((1, D), lambda i: (0, 0))
    return _pcall(body, name="gate_bwd", grid=(L // tm,), in_specs=[row, row, vec], out_specs=[row, vec],
                  out_shape=[jax.ShapeDtypeStruct((L, D), BF16), jax.ShapeDtypeStruct((1, D), F32)])(dx, a, gate)


def _qknorm_fwd(p, gqk, AW):
    L = p.shape[0]
    tm = _row_tile(L, 2048)
    ncol = 2 * AW // LANES

    def body(p_ref, g_ref, o_ref):
        p = p_ref[...]
        o_ref[...] = (p * _qkn_inv(p) * g_ref[...]).astype(BF16)

    blk = pl.BlockSpec((tm, LANES), lambda i, j: (i, j))
    vec = pl.BlockSpec((1, LANES), lambda i, j: (0, j))
    return _pcall(body, name="qknorm_fwd", grid=(L // tm, ncol), in_specs=[blk, vec], out_specs=blk,
                  out_shape=jax.ShapeDtypeStruct((L, 2 * AW), BF16))(p, gqk)


def _qknorm_bwd(dqk, p, gqk, AW):
    L = p.shape[0]
    tm = _row_tile(L, 2048)
    ncol = 2 * AW // LANES

    def body(d_ref, p_ref, g_ref, dp_ref, dg_ref):
        p, dy = p_ref[...], d_ref[...]
        inv = _qkn_inv(p)
        gdy = g_ref[...] * dy
        dp_ref[...] = (inv * gdy - p * (inv * inv * inv) * (_head_sums(p * gdy) / HEAD_DIM)).astype(BF16)
        _accumulate(dg_ref, jnp.sum(dy * p * inv, axis=0, keepdims=True), pl.program_id(1) == 0)

    blk = pl.BlockSpec((tm, LANES), lambda j, i: (i, j))
    vec = pl.BlockSpec((1, LANES), lambda j, i: (0, j))
    return _pcall(body, name="qknorm_bwd", grid=(ncol, L // tm), in_specs=[blk, blk, vec], out_specs=[blk, vec],
                  out_shape=[jax.ShapeDtypeStruct((L, 2 * AW), BF16),
                             jax.ShapeDtypeStruct((1, 2 * AW), F32)])(dqk, p, gqk)


def _outnorm_fwd(oa, os_, ga, gs):
    L, AW = oa.shape
    SW = os_.shape[1]
    tm = _row_tile(L)

    def body(oa_ref, os_ref, ga_ref, gs_ref, o_ref):
        o_ref[:, :AW] = _rms(oa_ref[...], ga_ref[...]).astype(BF16)
        o_ref[:, AW:] = _rms(os_ref[...], gs_ref[...]).astype(BF16)

    ra = pl.BlockSpec((tm, AW), lambda i: (i, 0))
    rs = pl.BlockSpec((tm, SW), lambda i: (i, 0))
    va = pl.BlockSpec((1, AW), lambda i: (0, 0))
    vs = pl.BlockSpec((1, SW), lambda i: (0, 0))
    ro = pl.BlockSpec((tm, AW + SW), lambda i: (i, 0))
    return _pcall(body, name="outnorm_fwd", grid=(L // tm,), in_specs=[ra, rs, va, vs], out_specs=ro,
                  out_shape=jax.ShapeDtypeStruct((L, AW + SW), BF16))(oa, os_, ga, gs)


def _outnorm_bwd(do, oa, os_, ga, gs):
    L, AW = oa.shape
    SW = os_.shape[1]
    tm = _row_tile(L)

    def body(do_ref, oa_ref, os_ref, ga_ref, gs_ref, doa_ref, dos_ref, dga_ref, dgs_ref):
        first = pl.program_id(0) == 0
        _, vjp_a = jax.vjp(_rms, oa_ref[...], ga_ref[...])
        doa, dga = vjp_a(do_ref[:, :AW])
        _, vjp_s = jax.vjp(_rms, os_ref[...], gs_ref[...])
        dos, dgs = vjp_s(do_ref[:, AW:])
        doa_ref[...] = doa
        dos_ref[...] = dos
        _accumulate(dga_ref, dga, first)
        _accumulate(dgs_ref, dgs, first)

    ra = pl.BlockSpec((tm, AW), lambda i: (i, 0))
    rs = pl.BlockSpec((tm, SW), lambda i: (i, 0))
    va = pl.BlockSpec((1, AW), lambda i: (0, 0))
    vs = pl.BlockSpec((1, SW), lambda i: (0, 0))
    ro = pl.BlockSpec((tm, AW + SW), lambda i: (i, 0))
    return _pcall(body, name="outnorm_bwd", grid=(L // tm,), in_specs=[ro, ra, rs, va, vs],
                  out_specs=[ra, rs, va, vs],
                  out_shape=[jax.ShapeDtypeStruct((L, AW), F32), jax.ShapeDtypeStruct((L, SW), F32),
                             jax.ShapeDtypeStruct((1, AW), F32), jax.ShapeDtypeStruct((1, SW), F32)])(
                                 do, oa, os_, ga, gs)


def _softplus_neg_abs(z):
    return jnp.log(1.0 + jnp.exp(-jnp.abs(z)))


def _attn_masks(B):
    row = lax.broadcasted_iota(jnp.int32, (B, B), 0)
    col = lax.broadcasted_iota(jnp.int32, (B, B), 1)
    strict = col < row
    upper = jnp.where(row > col, 1.0, 0.0).astype(BF16)
    lower = jnp.where(row < col, 1.0, 0.0).astype(BF16)
    return strict, upper, lower


def _ride_along(ride, scatter, HP, nb):
    n = len(ride)
    sems = [pltpu.SemaphoreType.DMA((NDEV - 1, n)), pltpu.SemaphoreType.DMA((NDEV - 1, n))] if n else []

    def hooks(srcs, dsts, send_sems, recv_sems):
        start, wait = _all_to_all(srcs, dsts, send_sems, recv_sems, scatter)
        hp, i = pl.program_id(0), pl.program_id(1)
        return (hp == 0) & (i == 0), start, (hp == HP - 1) & (i == nb - 1), wait

    return [HBM_SPEC] * n, _exchange_shapes(ride, scatter), sems, hooks


def _attn_fwd(qk, p, AW, B, ride=()):
    L = qk.shape[0]
    HP = AW // LANES
    nb = L // B
    n = len(ride)
    ride_specs, ride_shapes, ride_sems, ride_hooks = _ride_along(ride, False, HP, nb)

    def body(*refs):
        q_ref, k_ref, v_ref = refs[:3]
        o_ref, tot_ref = refs[3 + n:5 + n]
        scr = refs[5 + 2 * n:]
        if n:
            first, start, last, wait = ride_hooks(refs[3:3 + n], refs[5 + n:5 + 2 * n], scr[10], scr[11])
            pl.when(first)(start)
        lb_scr = (scr[0:2], scr[2:4])
        tail_scr = (scr[4:6], scr[6:8])
        sum_scr = scr[8:10]
        i = pl.program_id(1)
        m0 = lax.broadcasted_iota(jnp.int32, (1, LANES), 1) < HEAD_DIM
        strict, upper, _ = _attn_masks(B)
        q = q_ref[...] * 0.125
        zq = jnp.zeros_like(q)
        qh = (jnp.where(m0, q, zq), jnp.where(m0, zq, q))

        def keys(j):
            start = pl.multiple_of(jnp.maximum(j, 0) * B, B)
            return k_ref[pl.ds(start, B), :]

        def vals(j):
            start = pl.multiple_of(jnp.maximum(j, 0) * B, B)
            return v_ref[pl.ds(start, B), :].astype(BF16)

        strips = [slice(s, min(s + ATTN_STRIP, B)) for s in range(0, B, ATTN_STRIP)]

        def scores(j):
            kj = keys(j)
            return tuple(_dot(qh[h], kj, NT) for h in (0, 1))

        def logits(zs2, slot, diag):
            for h in (0, 1):
                z = zs2[h]
                his = []
                for rows in strips:
                    zs = z[rows]
                    lb = jnp.minimum(zs, 0.0) - _softplus_neg_abs(zs)
                    l1 = lb - zs
                    if diag:
                        l1 = jnp.where(strict[rows], l1, 0.0)
                    lb_scr[slot][h][rows, :] = lb
                    his.append(l1.astype(BF16))
                    rsum = jnp.sum(l1, axis=-1, keepdims=True)
                    if h == 0:
                        sum_scr[slot][rows, :] = jnp.broadcast_to(rsum, (rows.stop - rows.start, LANES))
                    else:
                        sum_scr[slot][rows, :] = jnp.where(m0, sum_scr[slot][rows, :], rsum)
                cat = lambda xs: jnp.concatenate(xs, axis=0)
                tail_scr[slot][h][...] = _dot(cat(his), upper)

        def attend(j, slot, diag):
            vj = vals(j)
            pv = []
            for h in (0, 1):
                ws = []
                for rows in strips:
                    w = jnp.exp(lb_scr[slot][h][rows, :] + tail_scr[slot][h][rows, :])
                    if diag:
                        w = jnp.where(strict[rows], w, 0.0)
                    ws.append(w.astype(BF16))
                pv.append(_dot(jnp.concatenate(ws, axis=0), vj))
            return jnp.where(m0, pv[0], pv[1])

        logits(scores(i), 0, True)
        o_ref[...] = attend(i, 0, True)
        tot_ref[...] = sum_scr[0][...]

        def half(j, slot):
            z = scores(j - 1)
            pv = attend(j, slot, False)
            logits(z, 1 - slot, False)
            o_ref[...] += pv * jnp.exp(tot_ref[...])
            tot_ref[...] += sum_scr[slot][...]

        @pl.when(i > 0)
        def _():
            logits(scores(i - 1), 1, False)

            @pl.loop(0, (i + 1) // 2)
            def _(t):
                j = i - 1 - 2 * t
                half(j, 1)

                @pl.when(j > 0)
                def _():
                    half(j - 1, 0)

        if n:
            pl.when(last)(wait)

    qspec = pl.BlockSpec((B, LANES), lambda hp, i: (i, hp))
    kspec = pl.BlockSpec((L, LANES), lambda hp, i: (0, HP + hp))
    vspec = pl.BlockSpec((L, LANES), lambda hp, i: (0, 2 * HP + hp))
    ospec = pl.BlockSpec((B, LANES), lambda hp, i: (i, hp))
    shp = jax.ShapeDtypeStruct((L, AW), F32)
    o, tot, *gathered = _pcall(
        body, name="attn_fwd", grid=(HP, nb), in_specs=[qspec, kspec, vspec] + ride_specs,
        out_specs=[ospec, ospec] + ride_specs, out_shape=[shp, shp] + ride_shapes,
        scratch=[pltpu.VMEM((B, B), F32)] * 8 + [pltpu.VMEM((B, LANES), F32)] * 2 + ride_sems)(qk, qk, p, *ride)
    return o, tot, _place_own(gathered, ride, False)


def _attn_bwd(qk, p, do, tot, AW, B, ride=()):
    L = qk.shape[0]
    HP = AW // LANES
    nb = L // B
    n = len(ride)
    ride_specs, ride_shapes, ride_sems, ride_hooks = _ride_along(ride, True, HP, nb)

    def body(*refs):
        q_ref, k_ref, v_ref, do_ref, tot_ref = refs[:5]
        dq_ref, dk_ref, dv_ref = refs[5 + n:8 + n]
        scr = refs[8 + 2 * n:]
        if n:
            first_step, start, last_step, wait = ride_hooks(refs[5:5 + n], refs[8 + n:8 + 2 * n], scr[15], scr[16])
            pl.when(first_step)(start)
        lb_scr, tail_scr, dw_scr, e_scr, beta_scr = scr[0:2], scr[2:4], scr[4:6], scr[6:8], scr[8:10]
        dos_scr, bsum_scr, left_scr, esum_scr, ecum_scr = scr[10:15]
        i = pl.program_id(1)
        m0 = lax.broadcasted_iota(jnp.int32, (1, LANES), 1) < HEAD_DIM
        strict, upper, lower = _attn_masks(B)
        strips = [slice(s, min(s + ATTN_STRIP, B)) for s in range(0, B, ATTN_STRIP)]
        cat = lambda xs: jnp.concatenate(xs, axis=0)

        @pl.when(i == 0)
        def _():
            dk_ref[...] = jnp.zeros_like(dk_ref)
            dv_ref[...] = jnp.zeros_like(dv_ref)

        q = q_ref[...] * 0.125
        zq = jnp.zeros_like(q)
        qh = (jnp.where(m0, q, zq), jnp.where(m0, zq, q))
        heads = lambda a: (jnp.where(m0, a, jnp.zeros_like(a)), jnp.where(m0, jnp.zeros_like(a), a))
        left_scr[...] = jnp.zeros_like(left_scr)
        ecum_scr[...] = jnp.zeros_like(ecum_scr)
        dq_ref[...] = jnp.zeros_like(dq_ref)

        def block_rows(j):
            return pl.ds(pl.multiple_of(j * B, B), B)

        def put_row_sums(ref, rows, h, rsum):
            if h == 0:
                ref[rows, :] = jnp.broadcast_to(rsum, (rows.stop - rows.start, LANES))
            else:
                ref[rows, :] = jnp.where(m0, ref[rows, :], rsum)

        def head_cols(x, h):
            other = pltpu.roll(x, HEAD_DIM, 1)
            full = jnp.where(m0, x, other) if h == 0 else jnp.where(m0, other, x)
            return jnp.concatenate([full] * (B // LANES), axis=1) if B > LANES else full

        def scores(j):
            kj = k_ref[block_rows(j), :]
            return tuple(_dot(qh[h], kj, NT) for h in (0, 1))

        def stage_a(j, zs2, diag):
            for h in (0, 1):
                his = []
                for rows in strips:
                    zs = zs2[h][rows]
                    lb = jnp.minimum(zs, 0.0) - _softplus_neg_abs(zs)
                    l1 = lb - zs
                    if diag:
                        l1 = jnp.where(strict[rows], l1, 0.0)
                    lb_scr[h][rows, :] = lb
                    his.append(l1.astype(BF16))
                    put_row_sums(bsum_scr, rows, h, jnp.sum(l1, axis=-1, keepdims=True))
                tail_scr[h][...] = _dot(cat(his), upper)
            bs = bsum_scr[...]
            scale = jnp.exp(tot_ref[...] - left_scr[...] - bs)
            left_scr[...] += bs
            dos = (do_ref[...] * scale).astype(BF16)
            dos_scr[...] = dos
            vj = v_ref[block_rows(j), :].astype(BF16)
            dosh = heads(dos)
            for h in (0, 1):
                dw_scr[h][...] = _dot(dosh[h], vj, NT)

        def stage_b(j, diag):
            dosh = heads(dos_scr[...])
            pres = []
            dv_blk = jnp.zeros((B, LANES), F32)
            for h in (0, 1):
                ehs, wbs = [], []
                for rows in strips:
                    lb = lb_scr[h][rows, :]
                    w = jnp.exp(lb + tail_scr[h][rows, :])
                    if diag:
                        w = jnp.where(strict[rows], w, 0.0)
                    beta_scr[h][rows, :] = jnp.exp(lb)
                    e = dw_scr[h][rows, :] * w
                    e_scr[h][rows, :] = e
                    ehs.append(e.astype(BF16))
                    wbs.append(w.astype(BF16))
                    put_row_sums(esum_scr, rows, h, jnp.sum(e, axis=-1, keepdims=True))
                pres.append(_dot(cat(ehs), lower))
                dv_blk = dv_blk + _dot(cat(wbs), dosh[h], TN)
            dv_ref[block_rows(j), :] += dv_blk
            return pres

        def stage_c(j, pres, diag):
            kh = heads(k_ref[block_rows(j), :])
            dq = jnp.zeros((B, LANES), F32)
            dk_blk = jnp.zeros((B, LANES), F32)
            for h in (0, 1):
                dzs = []
                for rows in strips:
                    e = e_scr[h][rows, :]
                    dl1 = pres[h][rows] + head_cols(ecum_scr[rows, :], h)
                    dz = e - beta_scr[h][rows, :] * (e + dl1)
                    if diag:
                        dz = jnp.where(strict[rows], dz, 0.0)
                    dzs.append(dz.astype(BF16))
                dzb = cat(dzs)
                dq = dq + _dot(dzb, kh[h])
                dk_blk = dk_blk + _dot(dzb, qh[h], TN)
            dq_ref[...] += dq
            dk_ref[block_rows(j), :] += dk_blk
            ecum_scr[...] += esum_scr[...]

        def step(t, diag_next):
            z = scores(t + 1)
            pres = stage_b(t, False)
            stage_a(t + 1, z, diag_next)
            stage_c(t, pres, False)

        @pl.when(i > 0)
        def _():
            stage_a(0, scores(0), False)

            @pl.loop(0, i - 1)
            def _(t):
                step(t, False)

            step(i - 1, True)

        @pl.when(i == 0)
        def _():
            stage_a(0, scores(0), True)

        stage_c(i, stage_b(i, True), True)
        dq_ref[...] = dq_ref[...] * 0.125
        if n:
            pl.when(last_step)(wait)

    qspec = pl.BlockSpec((B, LANES), lambda hp, i: (i, hp))
    kspec = pl.BlockSpec((L, LANES), lambda hp, i: (0, HP + hp))
    vspec = pl.BlockSpec((L, LANES), lambda hp, i: (0, 2 * HP + hp))
    full = pl.BlockSpec((L, LANES), lambda hp, i: (0, hp))
    shp = jax.ShapeDtypeStruct((L, AW), F32)
    dq, dk, dv, *received = _pcall(
        body, name="attn_bwd", grid=(HP, nb), in_specs=[qspec, kspec, vspec, qspec, qspec] + ride_specs,
        out_specs=[qspec, full, full] + ride_specs, out_shape=[shp, shp, shp] + ride_shapes,
        scratch=[pltpu.VMEM((B, B), F32)] * 10 + [pltpu.VMEM((B, LANES), BF16)]
        + [pltpu.VMEM((B, LANES), F32)] * 4 + ride_sems)(qk, qk, p, do, tot, *ride)
    return dq, dk, dv, _place_own(received, ride, True)


def _s5_disc(ar, ai, ldt):
    dt = jnp.exp(ldt)
    mag = jnp.exp(dt * ar)
    abr = mag * jnp.cos(dt * ai)
    abi = mag * jnp.sin(dt * ai)
    emr = abr - 1.0
    emi = abi
    den = ar * ar + ai * ai
    fr = (emr * ar + emi * ai) / den
    fi = (emi * ar - emr * ai) / den
    return abr, abi, fr, fi


def _s5_params_math(ar, ai, ldt, br, bi):
    abr, abi, fr, fi = _s5_disc(ar, ai, ldt)
    return abr, abi, fr * br - fi * bi, fr * bi + fi * br


def _cmul_add(xr, xi, kr, ki, sr, si):
    return xr + (kr * sr - ki * si), xi + (kr * si + ki * sr)


def _s5_tab_rows(T):
    ng = T // 8
    return dict(NG=ng, POW=0, PA=ng, PAR=ng + 8, LA=ng + 16, ROWS=ng + 24)


def _s5_prep(ar, ai, ldt, braw_r, braw_i, T):
    NL, _, NS = ar.shape
    GT = NS // STATE_TILE
    R = _s5_tab_rows(T)

    def body(ar_ref, ai_ref, ldt_ref, br_ref, bi_ref, abr_ref, abi_ref, tabr_ref, tabi_ref, bsr_ref, bsi_ref):
        for t in range(GT):
            sl = slice(t * STATE_TILE, (t + 1) * STATE_TILE)
            abr, abi, bsr, bsi = _s5_params_math(ar_ref[:, sl], ai_ref[:, sl], ldt_ref[:, sl],
                                                 br_ref[t], bi_ref[t])
            abr_ref[:, sl] = abr
            abi_ref[:, sl] = abi
            bsr_ref[t] = bsr.astype(BF16)
            bsi_ref[t] = bsi.astype(BF16)

            def put(row, vr, vi):
                tabr_ref[row:row + 1, sl] = vr
                tabi_ref[row:row + 1, sl] = vi

            pr, pi = abr, abi
            for g in range(R["NG"]):
                put(R["POW"] + g, pr, pi)
                if g + 1 < R["NG"]:
                    pr, pi = pr * abr - pi * abi, pr * abi + pi * abr
            big_r, big_i = pr, pi
            qr, qi = big_r, big_i
            for r in range(8):
                put(R["PA"] + r, qr, qi)
                put(R["PAR"] + 7 - r, qr, qi)
                qr, qi = qr * big_r - qi * big_i, qr * big_i + qi * big_r
            qr, qi = big_r, big_i
            for k in range(3):
                put(R["LA"] + k, qr, qi)
                qr, qi = qr * qr - qi * qi, 2.0 * qr * qi
            for k in range(3, 8):
                put(R["LA"] + k, jnp.zeros_like(qr), jnp.zeros_like(qi))

    rowspec = pl.BlockSpec((None, 1, NS), lambda l: (l, 0, 0))
    bspec = pl.BlockSpec((None, GT, LANES, STATE_TILE), lambda l: (l, 0, 0, 0))
    tabspec = pl.BlockSpec((None, R["ROWS"], NS), lambda l: (l, 0, 0))
    rs = jax.ShapeDtypeStruct((NL, 1, NS), F32)
    ts = jax.ShapeDtypeStruct((NL, R["ROWS"], NS), F32)
    bs = jax.ShapeDtypeStruct((NL, GT, LANES, STATE_TILE), BF16)
    return _pcall(body, name="s5_prep", grid=(NL,), in_specs=[rowspec] * 3 + [bspec] * 2,
                  out_specs=[rowspec, rowspec, tabspec, tabspec, bspec, bspec],
                  out_shape=[rs, rs, ts, ts, bs, bs])(ar, ai, ldt, braw_r, braw_i)


def _s5_prep_bwd(ar, ai, ldt, braw_r, braw_i, dabr, dabi, dbsr, dbsi):
    NL, _, NS = ar.shape
    GT = NS // STATE_TILE

    def body(ar_ref, ai_ref, ldt_ref, br_ref, bi_ref, dabr_ref, dabi_ref, dbsr_ref, dbsi_ref,
             dar_ref, dai_ref, dldt_ref, dbr_ref, dbi_ref):
        for t in range(GT):
            sl = slice(t * STATE_TILE, (t + 1) * STATE_TILE)
            _, vjp = jax.vjp(_s5_params_math, ar_ref[:, sl], ai_ref[:, sl], ldt_ref[:, sl],
                             br_ref[t], bi_ref[t])
            dabr_row = jnp.sum(dabr_ref[:, sl], axis=0, keepdims=True)
            dabi_row = jnp.sum(dabi_ref[:, sl], axis=0, keepdims=True)
            dar, dai, dldt, dbr, dbi = vjp((dabr_row, dabi_row, dbsr_ref[t], dbsi_ref[t]))
            dar_ref[:, sl] = dar
            dai_ref[:, sl] = dai
            dldt_ref[:, sl] = dldt
            dbr_ref[t] = dbr
            dbi_ref[t] = dbi

    rowspec = pl.BlockSpec((None, 1, NS), lambda l: (l, 0, 0))
    row8spec = pl.BlockSpec((None, 8, NS), lambda l: (l, 0, 0))
    bspec = pl.BlockSpec((None, GT, LANES, STATE_TILE), lambda l: (l, 0, 0, 0))
    rs = jax.ShapeDtypeStruct((NL, 1, NS), F32)
    bs = jax.ShapeDtypeStruct((NL, GT, LANES, STATE_TILE), F32)
    return _pcall(body, name="s5_prep_bwd", grid=(NL,), in_specs=[rowspec] * 3 + [bspec] * 2 + [row8spec] * 2 + [bspec] * 2,
                  out_specs=[rowspec] * 3 + [bspec] * 2, out_shape=[rs, rs, rs, bs, bs])(
                      ar, ai, ldt, braw_r, braw_i, dabr, dabi, dbsr, dbsi)


def _to_streams(x, T):
    L, C = x.shape
    return x.reshape(L // T, 8, T // 8, C).transpose(0, 2, 1, 3).reshape(L, C)


def _from_streams(x, T):
    L, C = x.shape
    return x.reshape(L // T, T // 8, 8, C).transpose(0, 2, 1, 3).reshape(L, C)


def _s5_specs(L, SW, T, rev):
    GT = SW // LANES
    nc = L // T
    cidx = (lambda c: nc - 1 - c) if rev else (lambda c: c)
    rows = _s5_tab_rows(T)["ROWS"]
    return dict(
        GT=GT, nc=nc, cidx=cidx,
        chan=pl.BlockSpec((T, LANES), lambda j, c: (cidx(c), j)),
        state=pl.BlockSpec((T, STATE_TILE), lambda j, c: (cidx(c), j)),
        bmat=pl.BlockSpec((None, LANES, STATE_TILE), lambda j, c: (j, 0, 0)),
        cmat=pl.BlockSpec((None, STATE_TILE, LANES), lambda j, c: (j, 0, 0)),
        gmat=pl.BlockSpec((None, LANES, LANES), lambda j, c: (j, 0, 0)),
        cvec=pl.BlockSpec((1, LANES), lambda j, c: (0, j)),
        svec8=pl.BlockSpec((8, STATE_TILE), lambda j, c: (0, j)),
        tab=pl.BlockSpec((rows, STATE_TILE), lambda j, c: (0, j)),
    )


def _s5_fwd(u, sp, T):
    L, SW = u.shape
    S = _s5_specs(L, SW, T, False)
    NS = S["GT"] * STATE_TILE
    R = _s5_tab_rows(T)
    NG = R["NG"]

    def body(u_ref, bsr_ref, bsi_ref, tabr_ref, tabi_ref, cr_ref, ci_ref, d_ref, wg_ref, gb_ref,
             o_ref, sr_ref, si_ref, y_ref, carr_ref, cari_ref):
        @pl.when(pl.program_id(1) == 0)
        def _():
            carr_ref[...] = jnp.zeros_like(carr_ref)
            cari_ref[...] = jnp.zeros_like(cari_ref)

        uv = u_ref[...]
        ub = uv.astype(BF16)
        row8 = lax.broadcasted_iota(jnp.int32, (8, LANES), 0)
        nstrips = STATE_TILE // LANES

        def project(s):
            sl = slice(s * LANES, (s + 1) * LANES)
            return _dot(ub, bsr_ref[:, sl]), _dot(ub, bsi_ref[:, sl])

        ahead = project(0)
        for s in range(nstrips):
            sl = slice(s * LANES, (s + 1) * LANES)
            tab = lambda r0, n=1, sl=sl: (tabr_ref[r0:r0 + n, sl], tabi_ref[r0:r0 + n, sl])
            xr, xi = ahead
            if s + 1 < nstrips:
                ahead = project(s + 1)
            ar, ai = tab(R["POW"])
            lr, li = [xr[0:8]], [xi[0:8]]
            for g in range(1, NG):
                nr, ni = _cmul_add(xr[8 * g:8 * g + 8], xi[8 * g:8 * g + 8], ar, ai, lr[-1], li[-1])
                lr.append(nr)
                li.append(ni)
            yr, yi = lr[-1], li[-1]
            for k, dist in enumerate((1, 2, 4)):
                kr, ki = tab(R["LA"] + k)
                keep = row8 >= dist
                yr, yi = _cmul_add(yr, yi, kr, ki, jnp.where(keep, pltpu.roll(yr, dist, 0), 0.0),
                                   jnp.where(keep, pltpu.roll(yi, dist, 0), 0.0))
            c0r, c0i = carr_ref[7:8, sl], cari_ref[7:8, sl]
            par, pai = tab(R["PA"], 8)
            er, ei = _cmul_add(yr, yi, par, pai, c0r, c0i)
            carr_ref[:, sl] = er
            cari_ref[:, sl] = ei
            first = row8 == 0
            inr = jnp.where(first, c0r, pltpu.roll(er, 1, 0))
            ini = jnp.where(first, c0i, pltpu.roll(ei, 1, 0))
            for g in range(NG):
                pr, pi = tab(R["POW"] + g)
                outr, outi = _cmul_add(lr[g], li[g], pr, pi, inr, ini)
                sr_ref[8 * g:8 * g + 8, sl] = outr
                si_ref[8 * g:8 * g + 8, sl] = outi
        y = (_dot(sr_ref[...].astype(BF16), cr_ref[...]) - _dot(si_ref[...].astype(BF16), ci_ref[...])
             + d_ref[...] * uv)
        y_ref[...] = y
        yg = jax.nn.gelu(y)
        gate = jax.nn.sigmoid(_dot(yg.astype(BF16), wg_ref[...]) + gb_ref[...])
        o_ref[...] = yg * gate

    cs = jax.ShapeDtypeStruct((L, SW), F32)
    ss = jax.ShapeDtypeStruct((L, NS), F32)
    return _pcall(
        body, name="s5_fwd", grid=(S["GT"], S["nc"]),
        in_specs=[S["chan"], S["bmat"], S["bmat"], S["tab"], S["tab"], S["cmat"], S["cmat"],
                  S["cvec"], S["gmat"], S["cvec"]],
        out_specs=[S["chan"], S["state"], S["state"], S["chan"]], out_shape=[cs, ss, ss, cs],
        scratch=[pltpu.VMEM((8, STATE_TILE), F32), pltpu.VMEM((8, STATE_TILE), F32)],
    )(u, sp["bsr"], sp["bsi"], sp["tabr"], sp["tabi"], sp["crT"], sp["ciT"], sp["d"], sp["wg"], sp["gb"])


def _s5_bwd(do, u, s_re, s_im, ypre, sp, T):
    L, SW = u.shape
    S = _s5_specs(L, SW, T, True)
    GT, nc, cidx = S["GT"], S["nc"], S["cidx"]
    NS = GT * STATE_TILE
    R = _s5_tab_rows(T)
    NG = R["NG"]
    T8 = T // 8

    def body(do_ref, u_ref, sr_ref, si_ref, hr_ref, hi_ref, y_ref, bsr_ref, bsi_ref, tabr_ref, tabi_ref,
             cr_ref, ci_ref, d_ref, wg_ref, gb_ref,
             du_ref, dbsr_ref, dbsi_ref, dcr_ref, dci_ref, dd_ref, dwg_ref, dgb_ref, dar_ref, dai_ref,
             carr_ref, cari_ref, lam_r, lam_i):
        c = pl.program_id(1)
        first = c == 0

        @pl.when(first)
        def _():
            carr_ref[...] = jnp.zeros_like(carr_ref)
            cari_ref[...] = jnp.zeros_like(cari_ref)

        u = u_ref[...]
        ub = u.astype(BF16)
        y = y_ref[...]
        yg, gelu_vjp = jax.vjp(jax.nn.gelu, y)
        ygb = yg.astype(BF16)
        gate = jax.nn.sigmoid(_dot(ygb, wg_ref[...]) + gb_ref[...])
        dout = do_ref[...]
        dt = dout * yg * gate * (1.0 - gate)
        dtb = dt.astype(BF16)
        dyg = dout * gate + _dot(dtb, wg_ref[...], NT)
        (dy,) = gelu_vjp(dyg)
        dyb = dy.astype(BF16)
        _accumulate(dwg_ref, _dot(ygb, dtb, TN), first)
        _accumulate(dgb_ref, jnp.sum(dt, axis=0, keepdims=True), first)
        _accumulate(dd_ref, jnp.sum(dy * u, axis=0, keepdims=True), first)
        _accumulate(dcr_ref, _dot(sr_ref[...].astype(BF16), dyb, TN), first)
        _accumulate(dci_ref, -_dot(si_ref[...].astype(BF16), dyb, TN), first)
        earliest = cidx(c) == 0
        row8 = lax.broadcasted_iota(jnp.int32, (8, LANES), 0)
        nstrips = STATE_TILE // LANES

        def project(s):
            sl = slice(s * LANES, (s + 1) * LANES)
            return _dot(dyb, cr_ref[sl, :], NT), -_dot(dyb, ci_ref[sl, :], NT)

        ahead = project(0)
        for s in range(nstrips):
            sl = slice(s * LANES, (s + 1) * LANES)
            tab = lambda r0, n=1, sl=sl: (tabr_ref[r0:r0 + n, sl], -tabi_ref[r0:r0 + n, sl])
            xr, xi = ahead
            if s + 1 < nstrips:
                ahead = project(s + 1)
            ar, ai = tab(R["POW"])
            lr, li = [None] * NG, [None] * NG
            lr[NG - 1], li[NG - 1] = xr[8 * (NG - 1):8 * NG], xi[8 * (NG - 1):8 * NG]
            for g in range(NG - 2, -1, -1):
                lr[g], li[g] = _cmul_add(xr[8 * g:8 * g + 8], xi[8 * g:8 * g + 8], ar, ai, lr[g + 1], li[g + 1])
            yr, yi = lr[0], li[0]
            for k, dist in enumerate((1, 2, 4)):
                kr, ki = tab(R["LA"] + k)
                keep = row8 < 8 - dist
                yr, yi = _cmul_add(yr, yi, kr, ki, jnp.where(keep, pltpu.roll(yr, 8 - dist, 0), 0.0),
                                   jnp.where(keep, pltpu.roll(yi, 8 - dist, 0), 0.0))
            c0r, c0i = carr_ref[0:1, sl], cari_ref[0:1, sl]
            par, pai = tab(R["PAR"], 8)
            er, ei = _cmul_add(yr, yi, par, pai, c0r, c0i)
            carr_ref[:, sl] = er
            cari_ref[:, sl] = ei
            last = row8 == 7
            inr = jnp.where(last, c0r, pltpu.roll(er, 7, 0))
            ini = jnp.where(last, c0i, pltpu.roll(ei, 7, 0))
            hr0 = jnp.where(earliest, 0.0, hr_ref[7:8, sl])
            hi0 = jnp.where(earliest, 0.0, hi_ref[7:8, sl])
            endr, endi = sr_ref[8 * (NG - 1):8 * NG, sl], si_ref[8 * (NG - 1):8 * NG, sl]
            pvr = jnp.where(row8 == 0, hr0, pltpu.roll(endr, 1, 0))
            pvi = jnp.where(row8 == 0, hi0, pltpu.roll(endi, 1, 0))
            accr = jnp.zeros((8, LANES), F32)
            acci = jnp.zeros((8, LANES), F32)
            for g in range(NG):
                pr, pi = tab(R["POW"] + NG - 1 - g)
                outr, outi = _cmul_add(lr[g], li[g], pr, pi, inr, ini)
                lam_r[8 * g:8 * g + 8, sl] = outr
                lam_i[8 * g:8 * g + 8, sl] = outi
                accr = accr + (outr * pvr + outi * pvi)
                acci = acci + (outi * pvr - outr * pvi)
                pvr, pvi = sr_ref[8 * g:8 * g + 8, sl], si_ref[8 * g:8 * g + 8, sl]

            @pl.when(first)
            def _():
                dar_ref[:, sl] = accr
                dai_ref[:, sl] = acci

            @pl.when(jnp.logical_not(first))
            def _():
                dar_ref[:, sl] += accr
                dai_ref[:, sl] += acci

        lrb = lam_r[...].astype(BF16)
        lib = lam_i[...].astype(BF16)
        _accumulate(dbsr_ref, _dot(ub, lrb, TN), first)
        _accumulate(dbsi_ref, _dot(ub, lib, TN), first)
        du_ref[...] = (dy * d_ref[...] + _dot(lrb, bsr_ref[...], NT) + _dot(lib, bsi_ref[...], NT)).astype(BF16)

    halo = pl.BlockSpec((8, STATE_TILE), lambda j, c: (jnp.maximum(cidx(c) * T8 - 1, 0), j))
    f = lambda *s: jax.ShapeDtypeStruct(s, F32)
    return _pcall(
        body, name="s5_bwd", grid=(GT, nc),
        in_specs=[S["chan"], S["chan"], S["state"], S["state"], halo, halo, S["chan"], S["bmat"], S["bmat"],
                  S["tab"], S["tab"], S["cmat"], S["cmat"], S["cvec"], S["gmat"], S["cvec"]],
        out_specs=[S["chan"], S["bmat"], S["bmat"], S["cmat"], S["cmat"], S["cvec"], S["gmat"], S["cvec"],
                   S["svec8"], S["svec8"]],
        out_shape=[jax.ShapeDtypeStruct((L, SW), BF16), f(GT, LANES, STATE_TILE), f(GT, LANES, STATE_TILE), f(GT, STATE_TILE, LANES),
                   f(GT, STATE_TILE, LANES), f(1, SW), f(GT, LANES, LANES), f(1, SW), f(8, NS), f(8, NS)],
        scratch=[pltpu.VMEM((8, STATE_TILE), F32), pltpu.VMEM((8, STATE_TILE), F32),
                 pltpu.VMEM((T, STATE_TILE), F32), pltpu.VMEM((T, STATE_TILE), F32)],
    )(do, u, s_re, s_im, s_re, s_im, ypre, sp["bsr"], sp["bsi"], sp["tabr"], sp["tabi"],
      sp["crT"], sp["ciT"], sp["d"], sp["wg"], sp["gb"])


def _conv_taps(xc, h6, h7, row):
    x1 = jnp.where(row == 0, h7, pltpu.roll(xc, 1, 0))
    x2 = jnp.where(row == 0, h6, jnp.where(row == 1, h7, pltpu.roll(xc, 2, 0)))
    return x1, x2


def _conv_halves(up_ref, halo_ref, w_ref, b_ref, tm, FS, first):
    row = lax.broadcasted_iota(jnp.int32, (tm, FS), 0)
    outs, taps = [], []
    for s in (0, 1):
        xc = up_ref[s]
        h6 = jnp.where(first, 0.0, halo_ref[s, 6:7, :])
        h7 = jnp.where(first, 0.0, halo_ref[s, 7:8, :])
        x1, x2 = _conv_taps(xc, h6, h7, row)
        outs.append(b_ref[s] + x2 * w_ref[s, 0:1, :] + x1 * w_ref[s, 1:2, :] + xc * w_ref[s, 2:3, :])
        taps.append((x2, x1, xc))
    return outs, taps


def _convglu_specs(L, FS, tm, rev=False):
    t8 = tm // 8
    nt = L // tm
    tile = (lambda i: nt - 1 - i) if rev else (lambda i: i)
    return dict(
        up=pl.BlockSpec((2, None, tm, FS), lambda j, i: (0, j, tile(i), 0)),
        halo=pl.BlockSpec((2, None, 8, FS), lambda j, i: (0, j, jnp.maximum(tile(i) * t8 - 1, 0), 0)),
        w=pl.BlockSpec((2, None, 3, FS), lambda j, i: (0, j, 0, 0)),
        b=pl.BlockSpec((2, None, 1, FS), lambda j, i: (0, j, 0, 0)),
        act=pl.BlockSpec((None, tm, FS), lambda j, i: (j, tile(i), 0)),
    )


def _convglu_fwd(up, w, b):
    _, NSH, L, FS = up.shape
    tm = _row_tile(L)
    S = _convglu_specs(L, FS, tm)

    def body(up_ref, halo_ref, w_ref, b_ref, act_ref):
        (val, gate), _ = _conv_halves(up_ref, halo_ref, w_ref, b_ref, tm, FS, pl.program_id(1) == 0)
        act_ref[...] = _glu_math(val, gate).astype(BF16)

    return _pcall(body, name="convglu_fwd", grid=(NSH, L // tm), in_specs=[S["up"], S["halo"], S["w"], S["b"]],
                  out_specs=S["act"], out_shape=jax.ShapeDtypeStruct((NSH, L, FS), BF16))(up, up, w, b)


def _convglu_bwd(up, dact, w, b):
    _, NSH, L, FS = up.shape
    tm = _row_tile(L)
    nt = L // tm
    S = _convglu_specs(L, FS, tm, rev=True)

    def body(up_ref, halo_ref, w_ref, b_ref, dact_ref, dup_ref, dw_ref, db_ref, after_scr):
        step = pl.program_id(1)
        first = step == 0
        (val, gate), taps = _conv_halves(up_ref, halo_ref, w_ref, b_ref, tm, FS, step == nt - 1)
        _, vjp = jax.vjp(_glu_math, val, gate)
        dcs = vjp(dact_ref[...])
        row = lax.broadcasted_iota(jnp.int32, (tm, FS), 0)
        for s in (0, 1):
            dc = dcs[s]
            n0 = jnp.where(first, 0.0, after_scr[s, 0:1, :])
            n1 = jnp.where(first, 0.0, after_scr[s, 1:2, :])
            x1 = jnp.where(row == tm - 1, n0, pltpu.roll(dc, tm - 1, 0))
            x2 = jnp.where(row == tm - 1, n1, jnp.where(row == tm - 2, n0, pltpu.roll(dc, tm - 2, 0)))
            dup_ref[s] = (dc * w_ref[s, 2:3, :] + x1 * w_ref[s, 1:2, :] + x2 * w_ref[s, 0:1, :]).astype(BF16)
            after_scr[s] = dc[0:8]
            sums = [jnp.sum(dc * t, axis=0, keepdims=True) for t in taps[s]]
            dbs = jnp.sum(dc, axis=0, keepdims=True)

            @pl.when(first)
            def _():
                for t in range(3):
                    dw_ref[s, t:t + 1, :] = sums[t]
                db_ref[s] = dbs

            @pl.when(jnp.logical_not(first))
            def _():
                for t in range(3):
                    dw_ref[s, t:t + 1, :] += sums[t]
                db_ref[s] += dbs

    f = lambda *s: jax.ShapeDtypeStruct(s, F32)
    return _pcall(body, name="convglu_bwd", grid=(NSH, nt),
                  in_specs=[S["up"], S["halo"], S["w"], S["b"], S["act"]],
                  out_specs=[S["up"], S["w"], S["b"]],
                  out_shape=[jax.ShapeDtypeStruct((2, NSH, L, FS), BF16), f(2, NSH, 3, FS), f(2, NSH, 1, FS)],
                  scratch=[pltpu.VMEM((2, 8, FS), F32)])(up, up, w, b, dact)


def _loss_head(y, target):
    L, D = y.shape
    tm = _row_tile(L)

    def body(y_ref, t_ref, loss_ref, dy_ref):
        err = y_ref[...] - t_ref[...]
        dy_ref[...] = err / D
        part = 0.5 * jnp.sum(jnp.mean(err * err, axis=-1, keepdims=True), axis=0, keepdims=True)
        _accumulate(loss_ref, jnp.broadcast_to(part, (1, LANES)), pl.program_id(0) == 0)

    row = pl.BlockSpec((tm, D), lambda i: (i, 0))
    vec = pl.BlockSpec((1, LANES), lambda i: (0, 0))
    return _pcall(body, name="loss_head", grid=(L // tm,), in_specs=[row, row], out_specs=[vec, row],
                  out_shape=[jax.ShapeDtypeStruct((1, LANES), F32), jax.ShapeDtypeStruct((L, D), F32)])(y, target)


def _ada_fwd(c_all, ada_w, ada_b):
    NL, D, NC = ada_w.shape
    NB = c_all.shape[0]

    def body(c_ref, w_ref, b_ref, o_ref):
        cact = jax.nn.silu(c_ref[...])
        o_ref[...] = _dot(cact.astype(BF16), w_ref[...].astype(BF16)) + b_ref[...]

    return _pcall(body, name="ada_fwd", grid=(NL,),
                  in_specs=[pl.BlockSpec((NB, D), lambda l: (0, 0)), pl.BlockSpec((None, D, NC), lambda l: (l, 0, 0)),
                            pl.BlockSpec((None, 1, NC), lambda l: (l, 0, 0))],
                  out_specs=pl.BlockSpec((None, NB, NC), lambda l: (l, 0, 0)),
                  out_shape=jax.ShapeDtypeStruct((NL, NB, NC), F32))(c_all, ada_w, ada_b)


def _ada_bwd(c_all_t, dmod):
    D, NB = c_all_t.shape
    NL, _, NC = dmod.shape

    def body(c_ref, d_ref, o_ref):
        cact = jax.nn.silu(c_ref[...]).astype(BF16).astype(F32)
        o_ref[...] = _dot(cact, d_ref[...].astype(BF16).astype(F32))

    return _pcall(body, name="ada_bwd", grid=(NL,),
                  in_specs=[pl.BlockSpec((D, NB), lambda l: (0, 0)), pl.BlockSpec((None, NB, NC), lambda l: (l, 0, 0))],
                  out_specs=pl.BlockSpec((None, D, NC), lambda l: (l, 0, 0)),
                  out_shape=jax.ShapeDtypeStruct((NL, D, NC), F32))(c_all_t, dmod)


def _adamw(parts, w, m, v):
    NL, P, R, C = parts.shape
    tr = R if R <= 512 else max(t for t in range(16, 513, 16) if R % t == 0)

    def body(p_ref, w_ref, m_ref, v_ref, g_ref, d_ref, nm_ref, nv_ref):
        g = p_ref[0].astype(F32)
        for k in range(1, P):
            g = g + p_ref[k].astype(F32)
        m2 = ADAM_B1 * m_ref[...] + (1.0 - ADAM_B1) * g
        v2 = ADAM_B2 * v_ref[...] + (1.0 - ADAM_B2) * jnp.square(g)
        m_hat = m2 / (1.0 - ADAM_B1 ** ADAM_STEP)
        v_hat = v2 / (1.0 - ADAM_B2 ** ADAM_STEP)
        g_ref[...] = g
        d_ref[...] = -ADAM_LR * (m_hat / (jnp.sqrt(v_hat) + ADAM_EPS) + ADAM_WD * w_ref[...])
        nm_ref[...] = m2
        nv_ref[...] = v2

    pspec = pl.BlockSpec((None, P, tr, C), lambda l, i: (l, 0, i, 0))
    wspec = pl.BlockSpec((None, tr, C), lambda l, i: (l, i, 0))
    shp = jax.ShapeDtypeStruct((NL, R, C), F32)
    return _pcall(body, name="adamw", grid=(NL, R // tr), in_specs=[pspec, wspec, wspec, wspec],
                  out_specs=[wspec] * 4, out_shape=[shp] * 4)(parts, w, m, v)


def _block_diag(blocks):
    *lead, g, r, c = blocks.shape
    eye = jnp.eye(g, dtype=bool)[:, None, :, None]
    full = jnp.where(eye, blocks[..., :, :, None, :], 0.0)
    return full.reshape(*lead, g * r, g * c)


def _block_diag_extract(m, r, c):
    g = GROUPS_PER_TILE
    m5 = m.reshape(*m.shape[:-2], g, r, g, c)
    eye = jnp.eye(g, dtype=bool)[:, None, :, None]
    return jnp.sum(jnp.where(eye, m5, 0.0), axis=-2)


def kernel(x, c, ada_w, ada_b, norm1_g, w_in, q_norm_g, k_norm_g, ssm_a_re, ssm_a_im, ssm_log_dt, ssm_b_re, ssm_b_im, ssm_c_re, ssm_c_im, ssm_d, glu_w, glu_b, attn_out_g, ssm_out_g, w_out, norm2_g, ffn_w_up, ffn_conv_w, ffn_conv_b, ffn_w_down, loss_target, m_ada_w, m_ada_b, m_norm1_g, m_w_in, m_q_norm_g, m_k_norm_g, m_ssm_a_re, m_ssm_a_im, m_ssm_log_dt, m_ssm_b_re, m_ssm_b_im, m_ssm_c_re, m_ssm_c_im, m_ssm_d, m_glu_w, m_glu_b, m_attn_out_g, m_ssm_out_g, m_w_out, m_norm2_g, m_ffn_w_up, m_ffn_conv_w, m_ffn_conv_b, m_ffn_w_down, v_ada_w, v_ada_b, v_norm1_g, v_w_in, v_q_norm_g, v_k_norm_g, v_ssm_a_re, v_ssm_a_im, v_ssm_log_dt, v_ssm_b_re, v_ssm_b_im, v_ssm_c_re, v_ssm_c_im, v_ssm_d, v_glu_w, v_glu_b, v_attn_out_g, v_ssm_out_g, v_w_out, v_norm2_g, v_ffn_w_up, v_ffn_conv_w, v_ffn_conv_b, v_ffn_w_down):
    weights = dict(ada_w=ada_w, ada_b=ada_b, norm1_g=norm1_g, w_in=w_in, q_norm_g=q_norm_g, k_norm_g=k_norm_g,
                   ssm_a_re=ssm_a_re, ssm_a_im=ssm_a_im, ssm_log_dt=ssm_log_dt, ssm_b_re=ssm_b_re,
                   ssm_b_im=ssm_b_im, ssm_c_re=ssm_c_re, ssm_c_im=ssm_c_im, ssm_d=ssm_d, glu_w=glu_w, glu_b=glu_b,
                   attn_out_g=attn_out_g, ssm_out_g=ssm_out_g, w_out=w_out, norm2_g=norm2_g, ffn_w_up=ffn_w_up,
                   ffn_conv_w=ffn_conv_w, ffn_conv_b=ffn_conv_b, ffn_w_down=ffn_w_down)
    mom_m = dict(ada_w=m_ada_w, ada_b=m_ada_b, norm1_g=m_norm1_g, w_in=m_w_in, q_norm_g=m_q_norm_g,
                 k_norm_g=m_k_norm_g, ssm_a_re=m_ssm_a_re, ssm_a_im=m_ssm_a_im, ssm_log_dt=m_ssm_log_dt,
                 ssm_b_re=m_ssm_b_re, ssm_b_im=m_ssm_b_im, ssm_c_re=m_ssm_c_re, ssm_c_im=m_ssm_c_im, ssm_d=m_ssm_d,
                 glu_w=m_glu_w, glu_b=m_glu_b, attn_out_g=m_attn_out_g, ssm_out_g=m_ssm_out_g, w_out=m_w_out,
                 norm2_g=m_norm2_g, ffn_w_up=m_ffn_w_up, ffn_conv_w=m_ffn_conv_w, ffn_conv_b=m_ffn_conv_b,
                 ffn_w_down=m_ffn_w_down)
    mom_v = dict(ada_w=v_ada_w, ada_b=v_ada_b, norm1_g=v_norm1_g, w_in=v_w_in, q_norm_g=v_q_norm_g,
                 k_norm_g=v_k_norm_g, ssm_a_re=v_ssm_a_re, ssm_a_im=v_ssm_a_im, ssm_log_dt=v_ssm_log_dt,
                 ssm_b_re=v_ssm_b_re, ssm_b_im=v_ssm_b_im, ssm_c_re=v_ssm_c_re, ssm_c_im=v_ssm_c_im, ssm_d=v_ssm_d,
                 glu_w=v_glu_w, glu_b=v_glu_b, attn_out_g=v_attn_out_g, ssm_out_g=v_ssm_out_g, w_out=v_w_out,
                 norm2_g=v_norm2_g, ffn_w_up=v_ffn_w_up, ffn_conv_w=v_ffn_conv_w, ffn_conv_b=v_ffn_conv_b,
                 ffn_w_down=v_ffn_w_down)
    names = list(weights)
    big = ("ada_w", "w_in", "w_out", "ffn_w_up", "ffn_conv_w", "ffn_w_down")
    small = [n for n in names if n not in big]

    x = x[0]
    target = loss_target[0]
    L, D = x.shape
    NL = ada_w.shape[0]
    AW = D // 2
    SW = D - AW
    NH = AW // HEAD_DIM
    HP = AW // LANES
    G = SW // SSM_GROUP
    GT = SW // LANES
    NS = G * SSM_STATE
    NIN = w_in.shape[-1]
    FS = ffn_w_up.shape[-1]
    NSH = NDEV // 2
    NCA = ada_w.shape[-1]
    ROWS_OUT = w_out.shape[1]
    ROWS_DOWN = ffn_w_down.shape[1]
    B_ATT = min(L, 256)
    T_S5 = min(L, 1024)
    tm = _row_tile(L, 1024)
    ts = _row_tile(L, 512)
    ts2 = _row_tile(L, 256)
    me = _my_index()

    cpad = jnp.reshape(c, (D // LANES, LANES))
    c_all = _gather_small(cpad, "gather_c")
    c_all = c_all.reshape(NDEV, D)
    shards = [w_in.astype(BF16), w_out.astype(BF16), ffn_w_up.astype(BF16), ffn_conv_w, ffn_w_down.astype(BF16)]
    (first_w_in,) = _gather_shards([shards[0][0:1]], "gather_weights")

    def set_w_in(lp, w_in_g):
        lp.update(w_in=w_in_g, w_in_cols=jnp.swapaxes(w_in_g, 0, 1).reshape(D, NDEV * NIN))

    def set_other_weights(lp, gathered):
        w_out_g, w_up_g, conv_w_g, w_down_g = gathered
        lp.update(w_out=w_out_g.reshape(D, D), w_up=w_up_g, conv_w=conv_w_g.reshape(2, NSH, 3, FS),
                  w_down=w_down_g.reshape(NSH, 2 * ROWS_DOWN, D))

    conv_b_g = ffn_conv_b.reshape(NL, 2, NSH, 1, FS)

    ada_b_mine = lax.dynamic_slice_in_dim(ada_b, me * NCA, NCA, axis=1).reshape(NL, 1, NCA)
    mod_part = _ada_fwd(c_all, ada_w, ada_b_mine)
    mod_all = _gather_small(mod_part, "gather_mod")
    mod = lax.dynamic_index_in_dim(mod_all, me, axis=2, keepdims=False)
    mod = jnp.transpose(mod, (1, 0, 2)).reshape(NL, N_MOD, 1, D)

    row = lambda a: a.reshape(NL, 1, NS)
    ar_row, ai_row = row(ssm_a_re), row(ssm_a_im)
    ldt_row = row(jnp.broadcast_to(ssm_log_dt[:, :, None], (NL, G, SSM_STATE)))
    tiles = lambda a: a.reshape((NL, GT, GROUPS_PER_TILE) + a.shape[2:])
    braw_r = _block_diag(jnp.swapaxes(tiles(ssm_b_re), -1, -2))
    braw_i = _block_diag(jnp.swapaxes(tiles(ssm_b_im), -1, -2))
    crT = _block_diag(jnp.swapaxes(tiles(ssm_c_re), -1, -2)).astype(BF16)
    ciT = _block_diag(jnp.swapaxes(tiles(ssm_c_im), -1, -2)).astype(BF16)
    wg = _block_diag(tiles(glu_w)).astype(BF16)
    abr, abi, tabr, tabi, bsr, bsi = _s5_prep(ar_row, ai_row, ldt_row, braw_r, braw_i, T_S5)
    s5p = dict(bsr=bsr, bsi=bsi, tabr=tabr, tabi=tabi, crT=crT, ciT=ciT,
               d=ssm_d.reshape(NL, 1, SW), wg=wg, gb=glu_b.reshape(NL, 1, SW))

    gqk = jnp.concatenate([jnp.tile(q_norm_g, (1, NH)), jnp.tile(k_norm_g, (1, NH))], axis=1).reshape(NL, 1, 2 * AW)
    layer_params = dict(
        mod=mod, norm1_g=norm1_g.reshape(NL, 1, D), norm2_g=norm2_g.reshape(NL, 1, D), gqk=gqk,
        ga=attn_out_g.reshape(NL, 1, AW), gs=ssm_out_g.reshape(NL, 1, SW),
        conv_b=conv_b_g, s5=s5p)


    def mm_rows_shards(name, a, b, n, out_dtype=F32):
        K = a.shape[1]
        return _matmul(name, a, b, dims=NN, grid=(L // tm, NDEV),
                       a_spec=pl.BlockSpec((tm, K), lambda i, j: (i, 0)),
                       b_spec=pl.BlockSpec((None, K, n), lambda i, j: (j, 0, 0)),
                       out_shape=jax.ShapeDtypeStruct((L, NDEV * n), out_dtype),
                       out_specs=pl.BlockSpec((tm, n), lambda i, j: (i, j)))

    tn = min(D, 512)

    def resid_epilogue(acc, xres, gate):
        return acc, xres + gate * acc

    def layer_fwd(xin, lp, ride):
        sh1, sc1, g1, sh2, sc2, g2 = (lp["mod"][k] for k in range(N_MOD))
        h = _lnmod_fwd(xin, lp["norm1_g"], sh1, sc1)
        p = mm_rows_shards("mm_in", h, lp["w_in"], NIN)
        qk = _qknorm_fwd(p, lp["gqk"], AW)
        o_attn, tot, gathered = _attn_fwd(qk, p, AW, B_ATT, ride)
        set_other_weights(lp, gathered[:4])
        next_gathered = gathered[4:]
        u_st = _to_streams(p[:, 3 * AW:], T_S5)
        o_st, s_re, s_im, ypre = _s5_fwd(u_st, lp["s5"], T_S5)
        o_ssm = _from_streams(o_st, T_S5)
        o = _outnorm_fwd(o_attn, o_ssm, lp["ga"], lp["gs"])
        a1, x_mid = _matmul(
            "mm_out", o, lp["w_out"], dims=NN, grid=(L // tm, D // tn),
            a_spec=pl.BlockSpec((tm, D), lambda i, j: (i, 0)), b_spec=pl.BlockSpec((D, tn), lambda i, j: (0, j)),
            extra=(xin, g1), extra_specs=(pl.BlockSpec((tm, tn), lambda i, j: (i, j)),
                                          pl.BlockSpec((1, tn), lambda i, j: (0, j))),
            epilogue=resid_epilogue,
            out_shape=[jax.ShapeDtypeStruct((L, D), F32)] * 2,
            out_specs=[pl.BlockSpec((tm, tn), lambda i, j: (i, j))] * 2)
        h2 = _lnmod_fwd(x_mid, lp["norm2_g"], sh2, sc2)
        up = _matmul(
            "mm_up", h2, lp["w_up"], dims=NN, grid=(L // tm, NDEV),
            a_spec=pl.BlockSpec((tm, D), lambda i, j: (i, 0)), b_spec=pl.BlockSpec((None, D, FS), lambda i, j: (j, 0, 0)),
            out_shape=jax.ShapeDtypeStruct((NDEV, L, FS), F32),
            out_specs=pl.BlockSpec((None, tm, FS), lambda i, j: (j, i, 0)))
        up = up.reshape(2, NSH, L, FS)
        act = _convglu_fwd(up, lp["conv_w"], lp["conv_b"])
        a2, x_out = _matmul(
            "mm_down", act, lp["w_down"], dims=NN, grid=(L // ts, D // tn),
            a_spec=pl.BlockSpec((NSH, ts, FS), lambda i, j: (0, i, 0)),
            b_spec=pl.BlockSpec((NSH, FS, tn), lambda i, j: (0, 0, j)),
            extra=(x_mid, g2), extra_specs=(pl.BlockSpec((ts, tn), lambda i, j: (i, j)),
                                            pl.BlockSpec((1, tn), lambda i, j: (0, j))),
            epilogue=resid_epilogue,
            out_shape=[jax.ShapeDtypeStruct((L, D), F32)] * 2,
            out_specs=[pl.BlockSpec((ts, tn), lambda i, j: (i, j))] * 2)
        res = dict(x=xin, h=h, p=p, qk=qk, tot=tot, o_attn=o_attn, o_ssm=o_ssm, u_st=u_st, s_re=s_re, s_im=s_im, ypre=ypre,
                   o=o, a1=a1, x_mid=x_mid, h2=h2, up=up, act=act, a2=a2)
        return x_out, res, next_gathered

    per_layer = [jax.tree.map(lambda a: a[l], layer_params) for l in range(NL)]
    y, residuals = x, []
    set_w_in(per_layer[0], first_w_in[0])
    for l in range(NL):
        ride = [s[l] for s in shards[1:]] + ([shards[0][l + 1]] if l + 1 < NL else [])
        y, res, next_w_in = layer_fwd(y, per_layer[l], ride)
        if next_w_in:
            set_w_in(per_layer[l + 1], next_w_in[0])
        residuals.append(res)

    loss_row, dy = _loss_head(y, target)
    loss = lax.psum(loss_row[0, 0], ("x", "y", "c"))

    def layer_bwd(dx, args, above):
        lp, r = args
        sh1, sc1, g1, sh2, sc2, g2 = (lp["mod"][k] for k in range(N_MOD))
        da2, dg2 = _gate_bwd(dx, r["a2"], g2)
        dact = _matmul(
            "mm_dact", da2, lp["w_down"], dims=NT, grid=(L // tm, NSH),
            a_spec=pl.BlockSpec((tm, D), lambda i, j: (i, 0)), b_spec=pl.BlockSpec((None, FS, D), lambda i, j: (j, 0, 0)),
            out_shape=jax.ShapeDtypeStruct((NSH, L, FS), F32),
            out_specs=pl.BlockSpec((None, tm, FS), lambda i, j: (j, i, 0)))
        dw_down = _matmul(
            "mm_dw_down", r["act"], da2, dims=TN, grid=(NSH, D // tn),
            a_spec=pl.BlockSpec((None, L, FS), lambda j, n: (j, 0, 0)),
            b_spec=pl.BlockSpec((L, tn), lambda j, n: (0, n)),
            out_shape=jax.ShapeDtypeStruct((NSH, FS, D), BF16),
            out_specs=pl.BlockSpec((None, FS, tn), lambda j, n: (j, 0, n)))
        dup4, dcw, dcb = _convglu_bwd(r["up"], dact, lp["conv_w"], lp["conv_b"])
        dup = dup4.reshape(NDEV, L, FS)
        dh2 = _matmul(
            "mm_dh2", dup, lp["w_up"], dims=NT, grid=(L // ts2,),
            a_spec=pl.BlockSpec((NDEV, ts2, FS), lambda i: (0, i, 0)),
            b_spec=pl.BlockSpec((NDEV, D, FS), lambda i: (0, 0, 0)),
            out_shape=jax.ShapeDtypeStruct((L, D), F32), out_specs=pl.BlockSpec((ts2, D), lambda i: (i, 0)))
        dw_up = _matmul(
            "mm_dw_up", r["h2"], dup, dims=TN, grid=(NDEV,),
            a_spec=pl.BlockSpec((L, D), lambda j: (0, 0)), b_spec=pl.BlockSpec((None, L, FS), lambda j: (j, 0, 0)),
            out_shape=jax.ShapeDtypeStruct((NDEV, D, FS), BF16),
            out_specs=pl.BlockSpec((None, D, FS), lambda j: (j, 0, 0)))
        dxm, dn2, dsh2, dsc2 = _lnmod_bwd(dh2, r["x_mid"], lp["norm2_g"], sh2, sc2, dx)
        da1, dg1 = _gate_bwd(dxm, r["a1"], g1)
        do = _matmul(
            "mm_do", da1, lp["w_out"], dims=NT, grid=(L // tm, D // tn),
            a_spec=pl.BlockSpec((tm, D), lambda i, j: (i, 0)), b_spec=pl.BlockSpec((tn, D), lambda i, j: (j, 0)),
            out_shape=jax.ShapeDtypeStruct((L, D), F32), out_specs=pl.BlockSpec((tm, tn), lambda i, j: (i, j)))
        dw_out = _matmul(
            "mm_dw_out", r["o"], da1, dims=TN, grid=(D // tn, D // tn),
            a_spec=pl.BlockSpec((L, tn), lambda m, n: (0, m)), b_spec=pl.BlockSpec((L, tn), lambda m, n: (0, n)),
            out_shape=jax.ShapeDtypeStruct((D, D), BF16), out_specs=pl.BlockSpec((tn, tn), lambda m, n: (m, n)))
        doa, dos, dga, dgs = _outnorm_bwd(do, r["o_attn"], r["o_ssm"], lp["ga"], lp["gs"])
        (du_st, dbsr, dbsi, dcr, dci, dd, dwg, dgb, dabr, dabi) = _s5_bwd(
            _to_streams(dos, T_S5), r["u_st"], r["s_re"], r["s_im"], r["ypre"], lp["s5"], T_S5)
        du = _from_streams(du_st, T_S5)
        ride = [dw_out.reshape(NDEV, ROWS_OUT, D), dw_up, dcw.reshape(NDEV, 3, FS),
                dw_down.reshape(NDEV, ROWS_DOWN, D)] + above
        dq, dk, dv, received = _attn_bwd(r["qk"], r["p"], doa, r["tot"], AW, B_ATT, ride)
        dqk, dgqk = _qknorm_bwd(jnp.concatenate([dq, dk], axis=1), r["p"], lp["gqk"], AW)
        dp = jnp.concatenate([dqk, dv.astype(BF16), du], axis=1)
        dh = _matmul(
            "mm_dh", dp, lp["w_in_cols"], dims=NT, grid=(L // tm,),
            a_spec=pl.BlockSpec((tm, NDEV * NIN), lambda i: (i, 0)), b_spec=pl.BlockSpec((D, NDEV * NIN), lambda i: (0, 0)),
            out_shape=jax.ShapeDtypeStruct((L, D), F32), out_specs=pl.BlockSpec((tm, D), lambda i: (i, 0)))
        dw_in = _matmul(
            "mm_dw_in", r["h"], dp, dims=TN, grid=(NDEV,),
            a_spec=pl.BlockSpec((L, D), lambda j: (0, 0)), b_spec=pl.BlockSpec((L, NIN), lambda j: (0, j)),
            out_shape=jax.ShapeDtypeStruct((NDEV, D, NIN), BF16),
            out_specs=pl.BlockSpec((None, D, NIN), lambda j: (j, 0, 0)))
        dx0, dn1, dsh1, dsc1 = _lnmod_bwd(dh, r["x"], lp["norm1_g"], sh1, sc1, dxm)
        grads = dict(
            dmod=jnp.concatenate([dsh1, dsc1, dg1, dsh2, dsc2, dg2], axis=1), dn1=dn1, dn2=dn2, dgqk=dgqk,
            dga=dga, dgs=dgs, dbsr=dbsr, dbsi=dbsi, dcr=dcr, dci=dci, dd=dd, dwg=dwg, dgb=dgb, dabr=dabr, dabi=dabi,
            dcb=dcb)
        return dx0, grads, dw_in, received

    grad_x, layer_grads, parts, above = dy, [None] * NL, [[None] * 5 for _ in range(NL)], []
    for l in reversed(range(NL)):
        grad_x, layer_grads[l], dw_in_l, received = layer_bwd(grad_x, (per_layer[l], residuals[l]), above)
        parts[l][1:] = received[:4]
        if above:
            parts[l + 1][0] = received[4]
        above = [dw_in_l]
    (parts[0][0],) = _scatter_direct(above, "scatter_weight_grads")
    gr = jax.tree.map(lambda *a: jnp.stack(a), *layer_grads)

    dar, dai, dldt, dbr_bd, dbi_bd = _s5_prep_bwd(ar_row, ai_row, ldt_row, braw_r, braw_i,
                                                  gr["dabr"], gr["dabi"], gr["dbsr"], gr["dbsi"])
    unt = lambda a: a.reshape((NL, G) + a.shape[3:])
    local = dict(
        ada_b=gr["dmod"].reshape(NL, N_MOD * D),
        norm1_g=gr["dn1"].reshape(NL, D), norm2_g=gr["dn2"].reshape(NL, D),
        q_norm_g=gr["dgqk"].reshape(NL, 2, NH, HEAD_DIM)[:, 0].sum(axis=1),
        k_norm_g=gr["dgqk"].reshape(NL, 2, NH, HEAD_DIM)[:, 1].sum(axis=1),
        ssm_a_re=dar.reshape(NL, G, SSM_STATE), ssm_a_im=dai.reshape(NL, G, SSM_STATE),
        ssm_log_dt=dldt.reshape(NL, G, SSM_STATE).sum(axis=-1),
        ssm_b_re=jnp.swapaxes(unt(_block_diag_extract(dbr_bd, SSM_GROUP, SSM_STATE)), -1, -2),
        ssm_b_im=jnp.swapaxes(unt(_block_diag_extract(dbi_bd, SSM_GROUP, SSM_STATE)), -1, -2),
        ssm_c_re=jnp.swapaxes(unt(_block_diag_extract(gr["dcr"], SSM_STATE, SSM_GROUP)), -1, -2),
        ssm_c_im=jnp.swapaxes(unt(_block_diag_extract(gr["dci"], SSM_STATE, SSM_GROUP)), -1, -2),
        ssm_d=gr["dd"].reshape(NL, G, SSM_GROUP),
        glu_w=unt(_block_diag_extract(gr["dwg"], SSM_GROUP, SSM_GROUP)),
        glu_b=gr["dgb"].reshape(NL, G, SSM_GROUP),
        attn_out_g=gr["dga"].reshape(NL, AW), ssm_out_g=gr["dgs"].reshape(NL, SW),
        ffn_conv_b=gr["dcb"].reshape(NL, 2 * NSH * FS),
    )

    def pack(tree):
        flat = jnp.concatenate([tree[n].reshape(-1) for n in small])
        pad = (-flat.shape[0]) % (512 * LANES)
        return jnp.pad(flat, (0, pad)).reshape(1, -1, LANES)

    (small_parts,) = _gather_shards([pack(local)], "gather_small_grads")
    sg, sd, sm, sv = _adamw(small_parts, pack(weights), pack(mom_m), pack(mom_v))

    def unpack(buf):
        flat = buf.reshape(-1)
        out, off = {}, 0
        for n in small:
            size = weights[n].size
            out[n] = flat[off:off + size].reshape(weights[n].shape)
            off += size
        return out

    ug, ud, um, uv = unpack(sg), unpack(sd), unpack(sm), unpack(sv)
    results = {n: (ug[n], ud[n], um[n], uv[n]) for n in small}

    dmod_all = _gather_small(gr["dmod"].reshape(NL, N_MOD * D), "gather_dmod")
    dmod_mine = lax.dynamic_slice_in_dim(dmod_all, me * NCA, NCA, axis=2)
    d_ada_w = _ada_bwd(jnp.transpose(c_all), jnp.transpose(dmod_mine, (1, 0, 2)))
    results["ada_w"] = tuple(_adamw(d_ada_w[:, None], ada_w, m_ada_w, v_ada_w))

    for a, n in enumerate(("w_in", "w_out", "ffn_w_up", "ffn_conv_w", "ffn_w_down")):
        results[n] = tuple(_adamw(jnp.stack([parts[l][a] for l in range(NL)]), weights[n], mom_m[n], mom_v[n]))

    out = [loss, grad_x[None]]
    for k in range(4):
        out.extend(results[n][k] for n in names)
    return tuple(out)
```

```python
import jax
import jax.numpy as jnp
from jax import lax
from jax.experimental import pallas as pl
from jax.experimental.pallas import tpu as pltpu

F32 = jnp.float32
BF16 = jnp.bfloat16
NDEV = 8
LANES = 128
HEAD_DIM = 64
SSM_GROUP = 16
SSM_STATE = 64
GROUPS_PER_TILE = LANES // SSM_GROUP
STATE_TILE = GROUPS_PER_TILE * SSM_STATE
N_MOD = 6
ATTN_STRIP = 32
EPS = 1e-6
ADAM_LR, ADAM_B1, ADAM_B2, ADAM_EPS, ADAM_WD, ADAM_STEP = 0.001, 0.9, 0.999, 1e-08, 0.01, 10
VMEM_LIMIT = 48 * 1024 * 1024
MESH_IDS = pl.DeviceIdType.MESH

NN = (((1,), (0,)), ((), ()))
NT = (((1,), (1,)), ((), ()))
TN = (((0,), (0,)), ((), ()))


def _dot(a, b, dims=NN):
    return lax.dot_general(a, b, dims, preferred_element_type=F32)


def _pcall(body, *, name, out_shape, in_specs, out_specs, grid=(), scratch=()):
    return pl.pallas_call(
        body, name=name, grid=grid, in_specs=in_specs, out_specs=out_specs, out_shape=out_shape,
        scratch_shapes=list(scratch),
        compiler_params=pltpu.CompilerParams(vmem_limit_bytes=VMEM_LIMIT))


def _row_tile(n, want=512):
    t = min(n, want)
    assert n % t == 0
    return t


def _my_index():
    return 4 * lax.axis_index("x") + 2 * lax.axis_index("y") + lax.axis_index("c")


HBM_SPEC = pl.BlockSpec(memory_space=pltpu.HBM)


def _mesh_place():
    x, y, c = lax.axis_index("x"), lax.axis_index("y"), lax.axis_index("c")
    chips = [(1 - x, y), (x, 1 - y), (1 - x, 1 - y)]
    return x, y, c, chips


def _gather_small(arr, name):
    def body(in_ref, out_ref, send_sems, recv_sems, local_sem):
        x, y, c, _ = _mesh_place()
        me = 4 * x + 2 * y + c
        own = pltpu.make_async_copy(in_ref, out_ref.at[me], local_sem)
        own.start()
        sends, recvs = [], []
        for k in range(1, NDEV):
            px = 1 - x if k & 4 else x
            py = 1 - y if k & 2 else y
            pc = 1 - c if k & 1 else c
            common = dict(send_sem=send_sems.at[k - 1], recv_sem=recv_sems.at[k - 1],
                          device_id=(px, py, pc), device_id_type=MESH_IDS)
            snd = pltpu.make_async_remote_copy(src_ref=in_ref, dst_ref=out_ref.at[me], **common)
            snd.start()
            sends.append(snd)
            recvs.append(pltpu.make_async_remote_copy(
                src_ref=in_ref, dst_ref=out_ref.at[4 * px + 2 * py + pc], **common))
        for r in recvs:
            r.wait_recv()
        for s in sends:
            s.wait_send()
        own.wait()

    return pl.pallas_call(
        body, name=name, out_shape=jax.ShapeDtypeStruct((NDEV,) + arr.shape, arr.dtype),
        in_specs=[HBM_SPEC], out_specs=HBM_SPEC,
        scratch_shapes=[pltpu.SemaphoreType.DMA((NDEV - 1,)), pltpu.SemaphoreType.DMA((NDEV - 1,)),
                        pltpu.SemaphoreType.DMA(())],
    )(arr)


def _all_to_all(srcs, dsts, send_sems, recv_sems, scatter):
    x, y, c, _ = _mesh_place()
    me = 4 * x + 2 * y + c
    pairs = []
    for k in range(1, NDEV):
        px = 1 - x if k & 4 else x
        py = 1 - y if k & 2 else y
        pc = 1 - c if k & 1 else c
        peer = 4 * px + 2 * py + pc
        for a in range(len(srcs)):
            src = srcs[a].at[peer] if scatter else srcs[a]
            common = dict(send_sem=send_sems.at[k - 1, a], recv_sem=recv_sems.at[k - 1, a],
                          device_id=(px, py, pc), device_id_type=MESH_IDS)
            pairs.append((pltpu.make_async_remote_copy(src_ref=src, dst_ref=dsts[a].at[me], **common),
                          pltpu.make_async_remote_copy(src_ref=src, dst_ref=dsts[a].at[peer], **common)))

    def start():
        for send, _ in pairs:
            send.start()

    def wait():
        for _, arrival in pairs:
            arrival.wait_recv()
        for send, _ in pairs:
            send.wait_send()

    return start, wait


def _place_own(outs, arrs, scatter):
    me = _my_index()
    owns = [lax.dynamic_slice_in_dim(a, me, 1, axis=0) if scatter else a[None] for a in arrs]
    return [lax.dynamic_update_slice_in_dim(o, own, me, axis=0) for o, own in zip(outs, owns)]


def _exchange_shapes(arrs, scatter):
    return [jax.ShapeDtypeStruct(a.shape if scatter else (NDEV,) + a.shape, a.dtype) for a in arrs]


def _scatter_direct(arrs, name):
    n = len(arrs)

    def body(*refs):
        start, wait = _all_to_all(refs[:n], refs[n:2 * n], refs[2 * n], refs[2 * n + 1], True)
        start()
        wait()

    outs = pl.pallas_call(
        body, name=name, out_shape=_exchange_shapes(arrs, True), in_specs=[HBM_SPEC] * n, out_specs=[HBM_SPEC] * n,
        scratch_shapes=[pltpu.SemaphoreType.DMA((NDEV - 1, n)), pltpu.SemaphoreType.DMA((NDEV - 1, n))],
    )(*arrs)
    return _place_own(outs, arrs, True)


def _start_by_layer(make, nl):
    for l in range(nl):
        make(l).start()


def _gather_shards(arrs, name):
    n = len(arrs)
    nl = arrs[0].shape[0]

    def body(*refs):
        ins, outs = refs[:n], refs[n:2 * n]
        send_sems, recv_sems = refs[2 * n:]
        x, y, c, chips = _mesh_place()
        dev = lambda px, py, pc: 4 * px + 2 * py + pc

        def copy(k, a, src, block, to, layer=slice(None)):
            return pltpu.make_async_remote_copy(
                src_ref=src.at[layer], dst_ref=outs[a].at[layer, block], send_sem=send_sems.at[k, a],
                recv_sem=recv_sems.at[k, a], device_id=to, device_id_type=MESH_IDS)

        me = dev(x, y, c)
        sent = []
        for a in range(n):
            _start_by_layer(lambda l: copy(0, a, ins[a], me, (x, y, 1 - c), l), nl)
            sent.append(copy(0, a, ins[a], me, (x, y, 1 - c)))
        for j, (px, py) in enumerate(chips):
            for a in range(n):
                _start_by_layer(lambda l: copy(1 + j, a, ins[a], me, (px, py, c), l), nl)
                sent.append(copy(1 + j, a, ins[a], me, (px, py, c)))
        for j, (px, py) in enumerate(chips):
            for a in range(n):
                blk = dev(px, py, c)
                copy(1 + j, a, ins[a], blk, (x, y, c)).wait_recv()
                got = outs[a].at[:, blk]
                _start_by_layer(lambda l: copy(4 + j, a, got, blk, (x, y, 1 - c), l), nl)
                sent.append(copy(4 + j, a, got, blk, (x, y, 1 - c)))
        for a in range(n):
            copy(0, a, ins[a], dev(x, y, 1 - c), (x, y, c)).wait_recv()
        for j, (px, py) in enumerate(chips):
            for a in range(n):
                copy(4 + j, a, ins[a], dev(px, py, 1 - c), (x, y, c)).wait_recv()
        for s in sent:
            s.wait_send()

    out_shape = [jax.ShapeDtypeStruct((a.shape[0], NDEV) + a.shape[1:], a.dtype) for a in arrs]
    outs = pl.pallas_call(
        body, name=name, out_shape=out_shape, in_specs=[HBM_SPEC] * n, out_specs=[HBM_SPEC] * n,
        scratch_shapes=[pltpu.SemaphoreType.DMA((7, n)), pltpu.SemaphoreType.DMA((7, n))],
    )(*arrs)
    me = _my_index()
    return [lax.dynamic_update_slice_in_dim(o, a[:, None], me, axis=1) for o, a in zip(outs, arrs)]


def _matmul(name, a, b, *, dims, grid, a_spec, b_spec, out_shape, out_specs, kaxis=None, acc_shape=None,
            extra=(), extra_specs=(), epilogue=None):
    nk = grid[kaxis] if kaxis is not None else 1
    ne = len(extra)
    multi = isinstance(out_shape, (list, tuple))
    n_out = len(out_shape) if multi else 1

    def body(*refs):
        a_ref, b_ref = refs[0], refs[1]
        ex = refs[2:2 + ne]
        outs = refs[2 + ne:2 + ne + n_out]

        def write(res):
            vals = epilogue(res, *[e[...] for e in ex]) if epilogue is not None else (res,)
            for o, v in zip(outs, vals):
                o[...] = v.astype(o.dtype)

        if len(a_ref.shape) == 3:
            part = _dot(a_ref[0].astype(BF16), b_ref[0].astype(BF16), dims)
            for s in range(1, a_ref.shape[0]):
                part = part + _dot(a_ref[s].astype(BF16), b_ref[s].astype(BF16), dims)
        else:
            part = _dot(a_ref[...].astype(BF16), b_ref[...].astype(BF16), dims)
        if nk == 1:
            write(part)
        else:
            acc = refs[-1]
            k = pl.program_id(kaxis)

            @pl.when(k == 0)
            def _():
                acc[...] = part

            @pl.when(k > 0)
            def _():
                acc[...] += part

            @pl.when(k == nk - 1)
            def _():
                write(acc[...])

    scratch = [pltpu.VMEM(acc_shape, F32)] if nk > 1 else []
    return _pcall(body, name=name, grid=grid, in_specs=[a_spec, b_spec, *extra_specs],
                  out_specs=out_specs, out_shape=out_shape, scratch=scratch)(a, b, *extra)


def _rms(x, g):
    inv = lax.rsqrt(jnp.mean(x * x, axis=-1, keepdims=True) + EPS)
    return x * inv * g


def _lnmod_math(x, g, sh, sc):
    return _rms(x, g) * (1.0 + sc) + sh


def _head_sums(x):
    row = lax.broadcasted_iota(jnp.int32, (LANES, LANES), 0) // HEAD_DIM
    col = lax.broadcasted_iota(jnp.int32, (LANES, LANES), 1) // HEAD_DIM
    same_head = jnp.where(row == col, 1.0, 0.0).astype(BF16)
    hi = x.astype(BF16)
    lo = (x - hi.astype(F32)).astype(BF16)
    return _dot(hi, same_head) + _dot(lo, same_head)


def _qkn_inv(p):
    return lax.rsqrt(_head_sums(p * p) / HEAD_DIM + EPS)


def _glu_math(val, gate):
    return jax.nn.gelu(gate) * val


def _accumulate(ref, val, first):
    @pl.when(first)
    def _():
        ref[...] = val

    @pl.when(jnp.logical_not(first))
    def _():
        ref[...] += val


def _lnmod_fwd(x, g, sh, sc):
    L, D = x.shape
    tm = _row_tile(L)

    def body(x_ref, g_ref, sh_ref, sc_ref, h_ref):
        h_ref[...] = _lnmod_math(x_ref[...], g_ref[...], sh_ref[...], sc_ref[...]).astype(BF16)

    row = pl.BlockSpec((tm, D), lambda i: (i, 0))
    vec = pl.BlockSpec((1, D), lambda i: (0, 0))
    return _pcall(body, name="lnmod_fwd", grid=(L // tm,), in_specs=[row, vec, vec, vec], out_specs=row,
                  out_shape=jax.ShapeDtypeStruct((L, D), BF16))(x, g, sh, sc)


def _lnmod_bwd(dh, x, g, sh, sc, dres):
    L, D = x.shape
    tm = _row_tile(L)

    def body(dh_ref, x_ref, g_ref, sh_ref, sc_ref, res_ref, dx_ref, dg_ref, dsh_ref, dsc_ref):
        _, vjp = jax.vjp(_lnmod_math, x_ref[...], g_ref[...], sh_ref[...], sc_ref[...])
        dx, dg, dsh, dsc = vjp(dh_ref[...])
        dx_ref[...] = dx + res_ref[...]
        first = pl.program_id(0) == 0
        _accumulate(dg_ref, dg, first)
        _accumulate(dsh_ref, dsh, first)
        _accumulate(dsc_ref, dsc, first)

    row = pl.BlockSpec((tm, D), lambda i: (i, 0))
    vec = pl.BlockSpec((1, D), lambda i: (0, 0))
    vs = jax.ShapeDtypeStruct((1, D), F32)
    return _pcall(body, name="lnmod_bwd", grid=(L // tm,), in_specs=[row, row, vec, vec, vec, row],
                  out_specs=[row, vec, vec, vec],
                  out_shape=[jax.ShapeDtypeStruct((L, D), F32), vs, vs, vs])(dh, x, g, sh, sc, dres)


def _gate_bwd(dx, a, gate):
    L, D = dx.shape
    tm = _row_tile(L)

    def body(dx_ref, a_ref, g_ref, da_ref, dg_ref):
        dxv = dx_ref[...]
        da_ref[...] = (g_ref[...] * dxv).astype(BF16)
        _accumulate(dg_ref, jnp.sum(dxv * a_ref[...], axis=0, keepdims=True), pl.program_id(0) == 0)

    row = pl.BlockSpec((tm, D), lambda i: (i, 0))
    vec = pl.BlockSpec((1, D), lambda i: (0, 0))
    return _pcall(body, name="gate_bwd", grid=(L // tm,), in_specs=[row, row, vec], out_specs=[row, vec],
                  out_shape=[jax.ShapeDtypeStruct((L, D), BF16), jax.ShapeDtypeStruct((1, D), F32)])(dx, a, gate)


def _qknorm_fwd(p, gqk, AW):
    L = p.shape[0]
    tm = _row_tile(L, 2048)
    ncol = 2 * AW // LANES

    def body(p_ref, g_ref, o_ref):
        p = p_ref[...]
        o_ref[...] = (p * _qkn_inv(p) * g_ref[...]).astype(BF16)

    blk = pl.BlockSpec((tm, LANES), lambda i, j: (i, j))
    vec = pl.BlockSpec((1, LANES), lambda i, j: (0, j))
    return _pcall(body, name="qknorm_fwd", grid=(L // tm, ncol), in_specs=[blk, vec], out_specs=blk,
                  out_shape=jax.ShapeDtypeStruct((L, 2 * AW), BF16))(p, gqk)


def _qknorm_bwd(dqk, p, gqk, AW):
    L = p.shape[0]
    tm = _row_tile(L, 2048)
    ncol = 2 * AW // LANES

    def body(d_ref, p_ref, g_ref, dp_ref, dg_ref):
        p, dy = p_ref[...], d_ref[...]
        inv = _qkn_inv(p)
        gdy = g_ref[...] * dy
        dp_ref[...] = (inv * gdy - p * (inv * inv * inv) * (_head_sums(p * gdy) / HEAD_DIM)).astype(BF16)
        _accumulate(dg_ref, jnp.sum(dy * p * inv, axis=0, keepdims=True), pl.program_id(1) == 0)

    blk = pl.BlockSpec((tm, LANES), lambda j, i: (i, j))
    vec = pl.BlockSpec((1, LANES), lambda j, i: (0, j))
    return _pcall(body, name="qknorm_bwd", grid=(ncol, L // tm), in_specs=[blk, blk, vec], out_specs=[blk, vec],
                  out_shape=[jax.ShapeDtypeStruct((L, 2 * AW), BF16),
                             jax.ShapeDtypeStruct((1, 2 * AW), F32)])(dqk, p, gqk)


def _outnorm_fwd(oa, os_, ga, gs):
    L, AW = oa.shape
    SW = os_.shape[1]
    tm = _row_tile(L)

    def body(oa_ref, os_ref, ga_ref, gs_ref, o_ref):
        o_ref[:, :AW] = _rms(oa_ref[...], ga_ref[...]).astype(BF16)
        o_ref[:, AW:] = _rms(os_ref[...], gs_ref[...]).astype(BF16)

    ra = pl.BlockSpec((tm, AW), lambda i: (i, 0))
    rs = pl.BlockSpec((tm, SW), lambda i: (i, 0))
    va = pl.BlockSpec((1, AW), lambda i: (0, 0))
    vs = pl.BlockSpec((1, SW), lambda i: (0, 0))
    ro = pl.BlockSpec((tm, AW + SW), lambda i: (i, 0))
    return _pcall(body, name="outnorm_fwd", grid=(L // tm,), in_specs=[ra, rs, va, vs], out_specs=ro,
                  out_shape=jax.ShapeDtypeStruct((L, AW + SW), BF16))(oa, os_, ga, gs)


def _outnorm_bwd(do, oa, os_, ga, gs):
    L, AW = oa.shape
    SW = os_.shape[1]
    tm = _row_tile(L)

    def body(do_ref, oa_ref, os_ref, ga_ref, gs_ref, doa_ref, dos_ref, dga_ref, dgs_ref):
        first = pl.program_id(0) == 0
        _, vjp_a = jax.vjp(_rms, oa_ref[...], ga_ref[...])
        doa, dga = vjp_a(do_ref[:, :AW])
        _, vjp_s = jax.vjp(_rms, os_ref[...], gs_ref[...])
        dos, dgs = vjp_s(do_ref[:, AW:])
        doa_ref[...] = doa
        dos_ref[...] = dos
        _accumulate(dga_ref, dga, first)
        _accumulate(dgs_ref, dgs, first)

    ra = pl.BlockSpec((tm, AW), lambda i: (i, 0))
    rs = pl.BlockSpec((tm, SW), lambda i: (i, 0))
    va = pl.BlockSpec((1, AW), lambda i: (0, 0))
    vs = pl.BlockSpec((1, SW), lambda i: (0, 0))
    ro = pl.BlockSpec((tm, AW + SW), lambda i: (i, 0))
    return _pcall(body, name="outnorm_bwd", grid=(L // tm,), in_specs=[ro, ra, rs, va, vs],
                  out_specs=[ra, rs, va, vs],
                  out_shape=[jax.ShapeDtypeStruct((L, AW), F32), jax.ShapeDtypeStruct((L, SW), F32),
                             jax.ShapeDtypeStruct((1, AW), F32), jax.ShapeDtypeStruct((1, SW), F32)])(
                                 do, oa, os_, ga, gs)


def _softplus_neg_abs(z):
    return jnp.log(1.0 + jnp.exp(-jnp.abs(z)))


def _attn_masks(B):
    row = lax.broadcasted_iota(jnp.int32, (B, B), 0)
    col = lax.broadcasted_iota(jnp.int32, (B, B), 1)
    strict = col < row
    upper = jnp.where(row > col, 1.0, 0.0).astype(BF16)
    lower = jnp.where(row < col, 1.0, 0.0).astype(BF16)
    return strict, upper, lower


def _ride_along(ride, scatter, HP, nb):
    n = len(ride)
    sems = [pltpu.SemaphoreType.DMA((NDEV - 1, n)), pltpu.SemaphoreType.DMA((NDEV - 1, n))] if n else []

    def hooks(srcs, dsts, send_sems, recv_sems):
        start, wait = _all_to_all(srcs, dsts, send_sems, recv_sems, scatter)
        hp, i = pl.program_id(0), pl.program_id(1)
        return (hp == 0) & (i == 0), start, (hp == HP - 1) & (i == nb - 1), wait

    return [HBM_SPEC] * n, _exchange_shapes(ride, scatter), sems, hooks


def _attn_fwd(qk, p, AW, B, ride=()):
    L = qk.shape[0]
    HP = AW // LANES
    nb = L // B
    n = len(ride)
    ride_specs, ride_shapes, ride_sems, ride_hooks = _ride_along(ride, False, HP, nb)

    def body(*refs):
        q_ref, k_ref, v_ref = refs[:3]
        o_ref, tot_ref = refs[3 + n:5 + n]
        scr = refs[5 + 2 * n:]
        if n:
            first, start, last, wait = ride_hooks(refs[3:3 + n], refs[5 + n:5 + 2 * n], scr[10], scr[11])
            pl.when(first)(start)
        lb_scr = (scr[0:2], scr[2:4])
        tail_scr = (scr[4:6], scr[6:8])
        sum_scr = scr[8:10]
        i = pl.program_id(1)
        m0 = lax.broadcasted_iota(jnp.int32, (1, LANES), 1) < HEAD_DIM
        strict, upper, _ = _attn_masks(B)
        q = q_ref[...] * 0.125
        zq = jnp.zeros_like(q)
        qh = (jnp.where(m0, q, zq), jnp.where(m0, zq, q))

        def keys(j):
            start = pl.multiple_of(jnp.maximum(j, 0) * B, B)
            return k_ref[pl.ds(start, B), :]

        def vals(j):
            start = pl.multiple_of(jnp.maximum(j, 0) * B, B)
            return v_ref[pl.ds(start, B), :].astype(BF16)

        strips = [slice(s, min(s + ATTN_STRIP, B)) for s in range(0, B, ATTN_STRIP)]

        def scores(j):
            kj = keys(j)
            return tuple(_dot(qh[h], kj, NT) for h in (0, 1))

        def logits(zs2, slot, diag):
            for h in (0, 1):
                z = zs2[h]
                his = []
                for rows in strips:
                    zs = z[rows]
                    lb = jnp.minimum(zs, 0.0) - _softplus_neg_abs(zs)
                    l1 = lb - zs
                    if diag:
                        l1 = jnp.where(strict[rows], l1, 0.0)
                    lb_scr[slot][h][rows, :] = lb
                    his.append(l1.astype(BF16))
                    rsum = jnp.sum(l1, axis=-1, keepdims=True)
                    if h == 0:
                        sum_scr[slot][rows, :] = jnp.broadcast_to(rsum, (rows.stop - rows.start, LANES))
                    else:
                        sum_scr[slot][rows, :] = jnp.where(m0, sum_scr[slot][rows, :], rsum)
                cat = lambda xs: jnp.concatenate(xs, axis=0)
                tail_scr[slot][h][...] = _dot(cat(his), upper)

        def attend(j, slot, diag):
            vj = vals(j)
            pv = []
            for h in (0, 1):
                ws = []
                for rows in strips:
                    w = jnp.exp(lb_scr[slot][h][rows, :] + tail_scr[slot][h][rows, :])
                    if diag:
                        w = jnp.where(strict[rows], w, 0.0)
                    ws.append(w.astype(BF16))
                pv.append(_dot(jnp.concatenate(ws, axis=0), vj))
            return jnp.where(m0, pv[0], pv[1])

        logits(scores(i), 0, True)
        o_ref[...] = attend(i, 0, True)
        tot_ref[...] = sum_scr[0][...]

        def half(j, slot):
            z = scores(j - 1)
            pv = attend(j, slot, False)
            logits(z, 1 - slot, False)
            o_ref[...] += pv * jnp.exp(tot_ref[...])
            tot_ref[...] += sum_scr[slot][...]

        @pl.when(i > 0)
        def _():
            logits(scores(i - 1), 1, False)

            @pl.loop(0, (i + 1) // 2)
            def _(t):
                j = i - 1 - 2 * t
                half(j, 1)

                @pl.when(j > 0)
                def _():
                    half(j - 1, 0)

        if n:
            pl.when(last)(wait)

    qspec = pl.BlockSpec((B, LANES), lambda hp, i: (i, hp))
    kspec = pl.BlockSpec((L, LANES), lambda hp, i: (0, HP + hp))
    vspec = pl.BlockSpec((L, LANES), lambda hp, i: (0, 2 * HP + hp))
    ospec = pl.BlockSpec((B, LANES), lambda hp, i: (i, hp))
    shp = jax.ShapeDtypeStruct((L, AW), F32)
    o, tot, *gathered = _pcall(
        body, name="attn_fwd", grid=(HP, nb), in_specs=[qspec, kspec, vspec] + ride_specs,
        out_specs=[ospec, ospec] + ride_specs, out_shape=[shp, shp] + ride_shapes,
        scratch=[pltpu.VMEM((B, B), F32)] * 8 + [pltpu.VMEM((B, LANES), F32)] * 2 + ride_sems)(qk, qk, p, *ride)
    return o, tot, _place_own(gathered, ride, False)


def _attn_bwd(qk, p, do, tot, AW, B, ride=()):
    L = qk.shape[0]
    HP = AW // LANES
    nb = L // B
    n = len(ride)
    ride_specs, ride_shapes, ride_sems, ride_hooks = _ride_along(ride, True, HP, nb)

    def body(*refs):
        q_ref, k_ref, v_ref, do_ref, tot_ref = refs[:5]
        dq_ref, dk_ref, dv_ref = refs[5 + n:8 + n]
        scr = refs[8 + 2 * n:]
        if n:
            first_step, start, last_step, wait = ride_hooks(refs[5:5 + n], refs[8 + n:8 + 2 * n], scr[15], scr[16])
            pl.when(first_step)(start)
        lb_scr, tail_scr, dw_scr, e_scr, beta_scr = scr[0:2], scr[2:4], scr[4:6], scr[6:8], scr[8:10]
        dos_scr, bsum_scr, left_scr, esum_scr, ecum_scr = scr[10:15]
        i = pl.program_id(1)
        m0 = lax.broadcasted_iota(jnp.int32, (1, LANES), 1) < HEAD_DIM
        strict, upper, lower = _attn_masks(B)
        strips = [slice(s, min(s + ATTN_STRIP, B)) for s in range(0, B, ATTN_STRIP)]
        cat = lambda xs: jnp.concatenate(xs, axis=0)

        @pl.when(i == 0)
        def _():
            dk_ref[...] = jnp.zeros_like(dk_ref)
            dv_ref[...] = jnp.zeros_like(dv_ref)

        q = q_ref[...] * 0.125
        zq = jnp.zeros_like(q)
        qh = (jnp.where(m0, q, zq), jnp.where(m0, zq, q))
        heads = lambda a: (jnp.where(m0, a, jnp.zeros_like(a)), jnp.where(m0, jnp.zeros_like(a), a))
        left_scr[...] = jnp.zeros_like(left_scr)
        ecum_scr[...] = jnp.zeros_like(ecum_scr)
        dq_ref[...] = jnp.zeros_like(dq_ref)

        def block_rows(j):
            return pl.ds(pl.multiple_of(j * B, B), B)

        def put_row_sums(ref, rows, h, rsum):
            if h == 0:
                ref[rows, :] = jnp.broadcast_to(rsum, (rows.stop - rows.start, LANES))
            else:
                ref[rows, :] = jnp.where(m0, ref[rows, :], rsum)

        def head_cols(x, h):
            other = pltpu.roll(x, HEAD_DIM, 1)
            full = jnp.where(m0, x, other) if h == 0 else jnp.where(m0, other, x)
            return jnp.concatenate([full] * (B // LANES), axis=1) if B > LANES else full

        def scores(j):
            kj = k_ref[block_rows(j), :]
            return tuple(_dot(qh[h], kj, NT) for h in (0, 1))

        def stage_a(j, zs2, diag):
            for h in (0, 1):
                his = []
                for rows in strips:
                    zs = zs2[h][rows]
                    lb = jnp.minimum(zs, 0.0) - _softplus_neg_abs(zs)
                    l1 = lb - zs
                    if diag:
                        l1 = jnp.where(strict[rows], l1, 0.0)
                    lb_scr[h][rows, :] = lb
                    his.append(l1.astype(BF16))
                    put_row_sums(bsum_scr, rows, h, jnp.sum(l1, axis=-1, keepdims=True))
                tail_scr[h][...] = _dot(cat(his), upper)
            bs = bsum_scr[...]
            scale = jnp.exp(tot_ref[...] - left_scr[...] - bs)
            left_scr[...] += bs
            dos = (do_ref[...] * scale).astype(BF16)
            dos_scr[...] = dos
            vj = v_ref[block_rows(j), :].astype(BF16)
            dosh = heads(dos)
            for h in (0, 1):
                dw_scr[h][...] = _dot(dosh[h], vj, NT)

        def stage_b(j, diag):
            dosh = heads(dos_scr[...])
            pres = []
            dv_blk = jnp.zeros((B, LANES), F32)
            for h in (0, 1):
                ehs, wbs = [], []
                for rows in strips:
                    lb = lb_scr[h][rows, :]
                    w = jnp.exp(lb + tail_scr[h][rows, :])
                    if diag:
                        w = jnp.where(strict[rows], w, 0.0)
                    beta_scr[h][rows, :] = jnp.exp(lb)
                    e = dw_scr[h][rows, :] * w
                    e_scr[h][rows, :] = e
                    ehs.append(e.astype(BF16))
                    wbs.append(w.astype(BF16))
                    put_row_sums(esum_scr, rows, h, jnp.sum(e, axis=-1, keepdims=True))
                pres.append(_dot(cat(ehs), lower))
                dv_blk = dv_blk + _dot(cat(wbs), dosh[h], TN)
            dv_ref[block_rows(j), :] += dv_blk
            return pres

        def stage_c(j, pres, diag):
            kh = heads(k_ref[block_rows(j), :])
            dq = jnp.zeros((B, LANES), F32)
            dk_blk = jnp.zeros((B, LANES), F32)
            for h in (0, 1):
                dzs = []
                for rows in strips:
                    e = e_scr[h][rows, :]
                    dl1 = pres[h][rows] + head_cols(ecum_scr[rows, :], h)
                    dz = e - beta_scr[h][rows, :] * (e + dl1)
                    if diag:
                        dz = jnp.where(strict[rows], dz, 0.0)
                    dzs.append(dz.astype(BF16))
                dzb = cat(dzs)
                dq = dq + _dot(dzb, kh[h])
                dk_blk = dk_blk + _dot(dzb, qh[h], TN)
            dq_ref[...] += dq
            dk_ref[block_rows(j), :] += dk_blk
            ecum_scr[...] += esum_scr[...]

        def step(t, diag_next):
            z = scores(t + 1)
            pres = stage_b(t, False)
            stage_a(t + 1, z, diag_next)
            stage_c(t, pres, False)

        @pl.when(i > 0)
        def _():
            stage_a(0, scores(0), False)

            @pl.loop(0, i - 1)
            def _(t):
                step(t, False)

            step(i - 1, True)

        @pl.when(i == 0)
        def _():
            stage_a(0, scores(0), True)

        stage_c(i, stage_b(i, True), True)
        dq_ref[...] = dq_ref[...] * 0.125
        if n:
            pl.when(last_step)(wait)

    qspec = pl.BlockSpec((B, LANES), lambda hp, i: (i, hp))
    kspec = pl.BlockSpec((L, LANES), lambda hp, i: (0, HP + hp))
    vspec = pl.BlockSpec((L, LANES), lambda hp, i: (0, 2 * HP + hp))
    full = pl.BlockSpec((L, LANES), lambda hp, i: (0, hp))
    shp = jax.ShapeDtypeStruct((L, AW), F32)
    dq, dk, dv, *received = _pcall(
        body, name="attn_bwd", grid=(HP, nb), in_specs=[qspec, kspec, vspec, qspec, qspec] + ride_specs,
        out_specs=[qspec, full, full] + ride_specs, out_shape=[shp, shp, shp] + ride_shapes,
        scratch=[pltpu.VMEM((B, B), F32)] * 10 + [pltpu.VMEM((B, LANES), BF16)]
        + [pltpu.VMEM((B, LANES), F32)] * 4 + ride_sems)(qk, qk, p, do, tot, *ride)
    return dq, dk, dv, _place_own(received, ride, True)


def _s5_disc(ar, ai, ldt):
    dt = jnp.exp(ldt)
    mag = jnp.exp(dt * ar)
    abr = mag * jnp.cos(dt * ai)
    abi = mag * jnp.sin(dt * ai)
    emr = abr - 1.0
    emi = abi
    den = ar * ar + ai * ai
    fr = (emr * ar + emi * ai) / den
    fi = (emi * ar - emr * ai) / den
    return abr, abi, fr, fi


def _s5_params_math(ar, ai, ldt, br, bi):
    abr, abi, fr, fi = _s5_disc(ar, ai, ldt)
    return abr, abi, fr * br - fi * bi, fr * bi + fi * br


def _cmul_add(xr, xi, kr, ki, sr, si):
    return xr + (kr * sr - ki * si), xi + (kr * si + ki * sr)


def _s5_tab_rows(T):
    ng = T // 8
    return dict(NG=ng, POW=0, PA=ng, PAR=ng + 8, LA=ng + 16, ROWS=ng + 24)


def _s5_prep(ar, ai, ldt, braw_r, braw_i, T):
    NL, _, NS = ar.shape
    GT = NS // STATE_TILE
    R = _s5_tab_rows(T)

    def body(ar_ref, ai_ref, ldt_ref, br_ref, bi_ref, abr_ref, abi_ref, tabr_ref, tabi_ref, bsr_ref, bsi_ref):
        for t in range(GT):
            sl = slice(t * STATE_TILE, (t + 1) * STATE_TILE)
            abr, abi, bsr, bsi = _s5_params_math(ar_ref[:, sl], ai_ref[:, sl], ldt_ref[:, sl],
                                                 br_ref[t], bi_ref[t])
            abr_ref[:, sl] = abr
            abi_ref[:, sl] = abi
            bsr_ref[t] = bsr.astype(BF16)
            bsi_ref[t] = bsi.astype(BF16)

            def put(row, vr, vi):
                tabr_ref[row:row + 1, sl] = vr
                tabi_ref[row:row + 1, sl] = vi

            pr, pi = abr, abi
            for g in range(R["NG"]):
                put(R["POW"] + g, pr, pi)
                if g + 1 < R["NG"]:
                    pr, pi = pr * abr - pi * abi, pr * abi + pi * abr
            big_r, big_i = pr, pi
            qr, qi = big_r, big_i
            for r in range(8):
                put(R["PA"] + r, qr, qi)
                put(R["PAR"] + 7 - r, qr, qi)
                qr, qi = qr * big_r - qi * big_i, qr * big_i + qi * big_r
            qr, qi = big_r, big_i
            for k in range(3):
                put(R["LA"] + k, qr, qi)
                qr, qi = qr * qr - qi * qi, 2.0 * qr * qi
            for k in range(3, 8):
                put(R["LA"] + k, jnp.zeros_like(qr), jnp.zeros_like(qi))

    rowspec = pl.BlockSpec((None, 1, NS), lambda l: (l, 0, 0))
    bspec = pl.BlockSpec((None, GT, LANES, STATE_TILE), lambda l: (l, 0, 0, 0))
    tabspec = pl.BlockSpec((None, R["ROWS"], NS), lambda l: (l, 0, 0))
    rs = jax.ShapeDtypeStruct((NL, 1, NS), F32)
    ts = jax.ShapeDtypeStruct((NL, R["ROWS"], NS), F32)
    bs = jax.ShapeDtypeStruct((NL, GT, LANES, STATE_TILE), BF16)
    return _pcall(body, name="s5_prep", grid=(NL,), in_specs=[rowspec] * 3 + [bspec] * 2,
                  out_specs=[rowspec, rowspec, tabspec, tabspec, bspec, bspec],
                  out_shape=[rs, rs, ts, ts, bs, bs])(ar, ai, ldt, braw_r, braw_i)


def _s5_prep_bwd(ar, ai, ldt, braw_r, braw_i, dabr, dabi, dbsr, dbsi):
    NL, _, NS = ar.shape
    GT = NS // STATE_TILE

    def body(ar_ref, ai_ref, ldt_ref, br_ref, bi_ref, dabr_ref, dabi_ref, dbsr_ref, dbsi_ref,
             dar_ref, dai_ref, dldt_ref, dbr_ref, dbi_ref):
        for t in range(GT):
            sl = slice(t * STATE_TILE, (t + 1) * STATE_TILE)
            _, vjp = jax.vjp(_s5_params_math, ar_ref[:, sl], ai_ref[:, sl], ldt_ref[:, sl],
                             br_ref[t], bi_ref[t])
            dabr_row = jnp.sum(dabr_ref[:, sl], axis=0, keepdims=True)
            dabi_row = jnp.sum(dabi_ref[:, sl], axis=0, keepdims=True)
            dar, dai, dldt, dbr, dbi = vjp((dabr_row, dabi_row, dbsr_ref[t], dbsi_ref[t]))
            dar_ref[:, sl] = dar
            dai_ref[:, sl] = dai
            dldt_ref[:, sl] = dldt
            dbr_ref[t] = dbr
            dbi_ref[t] = dbi

    rowspec = pl.BlockSpec((None, 1, NS), lambda l: (l, 0, 0))
    row8spec = pl.BlockSpec((None, 8, NS), lambda l: (l, 0, 0))
    bspec = pl.BlockSpec((None, GT, LANES, STATE_TILE), lambda l: (l, 0, 0, 0))
    rs = jax.ShapeDtypeStruct((NL, 1, NS), F32)
    bs = jax.ShapeDtypeStruct((NL, GT, LANES, STATE_TILE), F32)
    return _pcall(body, name="s5_prep_bwd", grid=(NL,), in_specs=[rowspec] * 3 + [bspec] * 2 + [row8spec] * 2 + [bspec] * 2,
                  out_specs=[rowspec] * 3 + [bspec] * 2, out_shape=[rs, rs, rs, bs, bs])(
                      ar, ai, ldt, braw_r, braw_i, dabr, dabi, dbsr, dbsi)


def _to_streams(x, T):
    L, C = x.shape
    return x.reshape(L // T, 8, T // 8, C).transpose(0, 2, 1, 3).reshape(L, C)


def _from_streams(x, T):
    L, C = x.shape
    return x.reshape(L // T, T // 8, 8, C).transpose(0, 2, 1, 3).reshape(L, C)


def _s5_specs(L, SW, T, rev):
    GT = SW // LANES
    nc = L // T
    cidx = (lambda c: nc - 1 - c) if rev else (lambda c: c)
    rows = _s5_tab_rows(T)["ROWS"]
    return dict(
        GT=GT, nc=nc, cidx=cidx,
        chan=pl.BlockSpec((T, LANES), lambda j, c: (cidx(c), j)),
        state=pl.BlockSpec((T, STATE_TILE), lambda j, c: (cidx(c), j)),
        bmat=pl.BlockSpec((None, LANES, STATE_TILE), lambda j, c: (j, 0, 0)),
        cmat=pl.BlockSpec((None, STATE_TILE, LANES), lambda j, c: (j, 0, 0)),
        gmat=pl.BlockSpec((None, LANES, LANES), lambda j, c: (j, 0, 0)),
        cvec=pl.BlockSpec((1, LANES), lambda j, c: (0, j)),
        svec8=pl.BlockSpec((8, STATE_TILE), lambda j, c: (0, j)),
        tab=pl.BlockSpec((rows, STATE_TILE), lambda j, c: (0, j)),
    )


def _s5_fwd(u, sp, T):
    L, SW = u.shape
    S = _s5_specs(L, SW, T, False)
    NS = S["GT"] * STATE_TILE
    R = _s5_tab_rows(T)
    NG = R["NG"]

    def body(u_ref, bsr_ref, bsi_ref, tabr_ref, tabi_ref, cr_ref, ci_ref, d_ref, wg_ref, gb_ref,
             o_ref, sr_ref, si_ref, y_ref, carr_ref, cari_ref):
        @pl.when(pl.program_id(1) == 0)
        def _():
            carr_ref[...] = jnp.zeros_like(carr_ref)
            cari_ref[...] = jnp.zeros_like(cari_ref)

        uv = u_ref[...]
        ub = uv.astype(BF16)
        row8 = lax.broadcasted_iota(jnp.int32, (8, LANES), 0)
        nstrips = STATE_TILE // LANES

        def project(s):
            sl = slice(s * LANES, (s + 1) * LANES)
            return _dot(ub, bsr_ref[:, sl]), _dot(ub, bsi_ref[:, sl])

        ahead = project(0)
        for s in range(nstrips):
            sl = slice(s * LANES, (s + 1) * LANES)
            tab = lambda r0, n=1, sl=sl: (tabr_ref[r0:r0 + n, sl], tabi_ref[r0:r0 + n, sl])
            xr, xi = ahead
            if s + 1 < nstrips:
                ahead = project(s + 1)
            ar, ai = tab(R["POW"])
            lr, li = [xr[0:8]], [xi[0:8]]
            for g in range(1, NG):
                nr, ni = _cmul_add(xr[8 * g:8 * g + 8], xi[8 * g:8 * g + 8], ar, ai, lr[-1], li[-1])
                lr.append(nr)
                li.append(ni)
            yr, yi = lr[-1], li[-1]
            for k, dist in enumerate((1, 2, 4)):
                kr, ki = tab(R["LA"] + k)
                keep = row8 >= dist
                yr, yi = _cmul_add(yr, yi, kr, ki, jnp.where(keep, pltpu.roll(yr, dist, 0), 0.0),
                                   jnp.where(keep, pltpu.roll(yi, dist, 0), 0.0))
            c0r, c0i = carr_ref[7:8, sl], cari_ref[7:8, sl]
            par, pai = tab(R["PA"], 8)
            er, ei = _cmul_add(yr, yi, par, pai, c0r, c0i)
            carr_ref[:, sl] = er
            cari_ref[:, sl] = ei
            first = row8 == 0
            inr = jnp.where(first, c0r, pltpu.roll(er, 1, 0))
            ini = jnp.where(first, c0i, pltpu.roll(ei, 1, 0))
            for g in range(NG):
                pr, pi = tab(R["POW"] + g)
                outr, outi = _cmul_add(lr[g], li[g], pr, pi, inr, ini)
                sr_ref[8 * g:8 * g + 8, sl] = outr
                si_ref[8 * g:8 * g + 8, sl] = outi
        y = (_dot(sr_ref[...].astype(BF16), cr_ref[...]) - _dot(si_ref[...].astype(BF16), ci_ref[...])
             + d_ref[...] * uv)
        y_ref[...] = y
        yg = jax.nn.gelu(y)
        gate = jax.nn.sigmoid(_dot(yg.astype(BF16), wg_ref[...]) + gb_ref[...])
        o_ref[...] = yg * gate

    cs = jax.ShapeDtypeStruct((L, SW), F32)
    ss = jax.ShapeDtypeStruct((L, NS), F32)
    return _pcall(
        body, name="s5_fwd", grid=(S["GT"], S["nc"]),
        in_specs=[S["chan"], S["bmat"], S["bmat"], S["tab"], S["tab"], S["cmat"], S["cmat"],
                  S["cvec"], S["gmat"], S["cvec"]],
        out_specs=[S["chan"], S["state"], S["state"], S["chan"]], out_shape=[cs, ss, ss, cs],
        scratch=[pltpu.VMEM((8, STATE_TILE), F32), pltpu.VMEM((8, STATE_TILE), F32)],
    )(u, sp["bsr"], sp["bsi"], sp["tabr"], sp["tabi"], sp["crT"], sp["ciT"], sp["d"], sp["wg"], sp["gb"])


def _s5_bwd(do, u, s_re, s_im, ypre, sp, T):
    L, SW = u.shape
    S = _s5_specs(L, SW, T, True)
    GT, nc, cidx = S["GT"], S["nc"], S["cidx"]
    NS = GT * STATE_TILE
    R = _s5_tab_rows(T)
    NG = R["NG"]
    T8 = T // 8

    def body(do_ref, u_ref, sr_ref, si_ref, hr_ref, hi_ref, y_ref, bsr_ref, bsi_ref, tabr_ref, tabi_ref,
             cr_ref, ci_ref, d_ref, wg_ref, gb_ref,
             du_ref, dbsr_ref, dbsi_ref, dcr_ref, dci_ref, dd_ref, dwg_ref, dgb_ref, dar_ref, dai_ref,
             carr_ref, cari_ref, lam_r, lam_i):
        c = pl.program_id(1)
        first = c == 0

        @pl.when(first)
        def _():
            carr_ref[...] = jnp.zeros_like(carr_ref)
            cari_ref[...] = jnp.zeros_like(cari_ref)

        u = u_ref[...]
        ub = u.astype(BF16)
        y = y_ref[...]
        yg, gelu_vjp = jax.vjp(jax.nn.gelu, y)
        ygb = yg.astype(BF16)
        gate = jax.nn.sigmoid(_dot(ygb, wg_ref[...]) + gb_ref[...])
        dout = do_ref[...]
        dt = dout * yg * gate * (1.0 - gate)
        dtb = dt.astype(BF16)
        dyg = dout * gate + _dot(dtb, wg_ref[...], NT)
        (dy,) = gelu_vjp(dyg)
        dyb = dy.astype(BF16)
        _accumulate(dwg_ref, _dot(ygb, dtb, TN), first)
        _accumulate(dgb_ref, jnp.sum(dt, axis=0, keepdims=True), first)
        _accumulate(dd_ref, jnp.sum(dy * u, axis=0, keepdims=True), first)
        _accumulate(dcr_ref, _dot(sr_ref[...].astype(BF16), dyb, TN), first)
        _accumulate(dci_ref, -_dot(si_ref[...].astype(BF16), dyb, TN), first)
        earliest = cidx(c) == 0
        row8 = lax.broadcasted_iota(jnp.int32, (8, LANES), 0)
        nstrips = STATE_TILE // LANES

        def project(s):
            sl = slice(s * LANES, (s + 1) * LANES)
            return _dot(dyb, cr_ref[sl, :], NT), -_dot(dyb, ci_ref[sl, :], NT)

        ahead = project(0)
        for s in range(nstrips):
            sl = slice(s * LANES, (s + 1) * LANES)
            tab = lambda r0, n=1, sl=sl: (tabr_ref[r0:r0 + n, sl], -tabi_ref[r0:r0 + n, sl])
            xr, xi = ahead
            if s + 1 < nstrips:
                ahead = project(s + 1)
            ar, ai = tab(R["POW"])
            lr, li = [None] * NG, [None] * NG
            lr[NG - 1], li[NG - 1] = xr[8 * (NG - 1):8 * NG], xi[8 * (NG - 1):8 * NG]
            for g in range(NG - 2, -1, -1):
                lr[g], li[g] = _cmul_add(xr[8 * g:8 * g + 8], xi[8 * g:8 * g + 8], ar, ai, lr[g + 1], li[g + 1])
            yr, yi = lr[0], li[0]
            for k, dist in enumerate((1, 2, 4)):
                kr, ki = tab(R["LA"] + k)
                keep = row8 < 8 - dist
                yr, yi = _cmul_add(yr, yi, kr, ki, jnp.where(keep, pltpu.roll(yr, 8 - dist, 0), 0.0),
                                   jnp.where(keep, pltpu.roll(yi, 8 - dist, 0), 0.0))
            c0r, c0i = carr_ref[0:1, sl], cari_ref[0:1, sl]
            par, pai = tab(R["PAR"], 8)
            er, ei = _cmul_add(yr, yi, par, pai, c0r, c0i)
            carr_ref[:, sl] = er
            cari_ref[:, sl] = ei
            last = row8 == 7
            inr = jnp.where(last, c0r, pltpu.roll(er, 7, 0))
            ini = jnp.where(last, c0i, pltpu.roll(ei, 7, 0))
            hr0 = jnp.where(earliest, 0.0, hr_ref[7:8, sl])
            hi0 = jnp.where(earliest, 0.0, hi_ref[7:8, sl])
            endr, endi = sr_ref[8 * (NG - 1):8 * NG, sl], si_ref[8 * (NG - 1):8 * NG, sl]
            pvr = jnp.where(row8 == 0, hr0, pltpu.roll(endr, 1, 0))
            pvi = jnp.where(row8 == 0, hi0, pltpu.roll(endi, 1, 0))
            accr = jnp.zeros((8, LANES), F32)
            acci = jnp.zeros((8, LANES), F32)
            for g in range(NG):
                pr, pi = tab(R["POW"] + NG - 1 - g)
                outr, outi = _cmul_add(lr[g], li[g], pr, pi, inr, ini)
                lam_r[8 * g:8 * g + 8, sl] = outr
                lam_i[8 * g:8 * g + 8, sl] = outi
                accr = accr + (outr * pvr + outi * pvi)
                acci = acci + (outi * pvr - outr * pvi)
                pvr, pvi = sr_ref[8 * g:8 * g + 8, sl], si_ref[8 * g:8 * g + 8, sl]

            @pl.when(first)
            def _():
                dar_ref[:, sl] = accr
                dai_ref[:, sl] = acci

            @pl.when(jnp.logical_not(first))
            def _():
                dar_ref[:, sl] += accr
                dai_ref[:, sl] += acci

        lrb = lam_r[...].astype(BF16)
        lib = lam_i[...].astype(BF16)
        _accumulate(dbsr_ref, _dot(ub, lrb, TN), first)
        _accumulate(dbsi_ref, _dot(ub, lib, TN), first)
        du_ref[...] = (dy * d_ref[...] + _dot(lrb, bsr_ref[...], NT) + _dot(lib, bsi_ref[...], NT)).astype(BF16)

    halo = pl.BlockSpec((8, STATE_TILE), lambda j, c: (jnp.maximum(cidx(c) * T8 - 1, 0), j))
    f = lambda *s: jax.ShapeDtypeStruct(s, F32)
    return _pcall(
        body, name="s5_bwd", grid=(GT, nc),
        in_specs=[S["chan"], S["chan"], S["state"], S["state"], halo, halo, S["chan"], S["bmat"], S["bmat"],
                  S["tab"], S["tab"], S["cmat"], S["cmat"], S["cvec"], S["gmat"], S["cvec"]],
        out_specs=[S["chan"], S["bmat"], S["bmat"], S["cmat"], S["cmat"], S["cvec"], S["gmat"], S["cvec"],
                   S["svec8"], S["svec8"]],
        out_shape=[jax.ShapeDtypeStruct((L, SW), BF16), f(GT, LANES, STATE_TILE), f(GT, LANES, STATE_TILE), f(GT, STATE_TILE, LANES),
                   f(GT, STATE_TILE, LANES), f(1, SW), f(GT, LANES, LANES), f(1, SW), f(8, NS), f(8, NS)],
        scratch=[pltpu.VMEM((8, STATE_TILE), F32), pltpu.VMEM((8, STATE_TILE), F32),
                 pltpu.VMEM((T, STATE_TILE), F32), pltpu.VMEM((T, STATE_TILE), F32)],
    )(do, u, s_re, s_im, s_re, s_im, ypre, sp["bsr"], sp["bsi"], sp["tabr"], sp["tabi"],
      sp["crT"], sp["ciT"], sp["d"], sp["wg"], sp["gb"])


def _shifted_rows(x, k, edge):
    n = x.shape[0]
    rolled = pltpu.roll(x, k % n, 0)
    row8 = lax.broadcasted_iota(jnp.int32, (8, x.shape[1]), 0)
    if k > 0:
        fixed = rolled[0:8]
        for r, e in enumerate(edge):
            fixed = jnp.where(row8 == r, e, fixed)
        return jnp.concatenate([fixed, rolled[8:]], axis=0) if n > 8 else fixed
    fixed = rolled[n - 8:n]
    for r, e in enumerate(edge):
        fixed = jnp.where(row8 == 8 + k + r, e, fixed)
    return jnp.concatenate([rolled[:n - 8], fixed], axis=0) if n > 8 else fixed


def _conv_taps(xc, h6, h7):
    return _shifted_rows(xc, 1, [h7]), _shifted_rows(xc, 2, [h6, h7])


def _conv_halves(up_ref, halo_ref, w_ref, b_ref, tm, FS, first):
    outs, taps = [], []
    for s in (0, 1):
        xc = up_ref[s]
        h6 = jnp.where(first, 0.0, halo_ref[s, 6:7, :])
        h7 = jnp.where(first, 0.0, halo_ref[s, 7:8, :])
        x1, x2 = _conv_taps(xc, h6, h7)
        outs.append(b_ref[s] + x2 * w_ref[s, 0:1, :] + x1 * w_ref[s, 1:2, :] + xc * w_ref[s, 2:3, :])
        taps.append((x2, x1, xc))
    return outs, taps


def _convglu_specs(L, FS, tm, rev=False):
    t8 = tm // 8
    nt = L // tm
    tile = (lambda i: nt - 1 - i) if rev else (lambda i: i)
    return dict(
        up=pl.BlockSpec((2, None, tm, FS), lambda j, i: (0, j, tile(i), 0)),
        halo=pl.BlockSpec((2, None, 8, FS), lambda j, i: (0, j, jnp.maximum(tile(i) * t8 - 1, 0), 0)),
        w=pl.BlockSpec((2, None, 3, FS), lambda j, i: (0, j, 0, 0)),
        b=pl.BlockSpec((2, None, 1, FS), lambda j, i: (0, j, 0, 0)),
        act=pl.BlockSpec((None, tm, FS), lambda j, i: (j, tile(i), 0)),
    )


def _convglu_fwd(up, w, b):
    _, NSH, L, FS = up.shape
    tm = _row_tile(L)
    S = _convglu_specs(L, FS, tm)

    def body(up_ref, halo_ref, w_ref, b_ref, act_ref):
        (val, gate), _ = _conv_halves(up_ref, halo_ref, w_ref, b_ref, tm, FS, pl.program_id(1) == 0)
        act_ref[...] = _glu_math(val, gate).astype(BF16)

    return _pcall(body, name="convglu_fwd", grid=(NSH, L // tm), in_specs=[S["up"], S["halo"], S["w"], S["b"]],
                  out_specs=S["act"], out_shape=jax.ShapeDtypeStruct((NSH, L, FS), BF16))(up, up, w, b)


def _convglu_bwd(up, dact, w, b):
    _, NSH, L, FS = up.shape
    tm = _row_tile(L)
    nt = L // tm
    S = _convglu_specs(L, FS, tm, rev=True)

    def body(up_ref, halo_ref, w_ref, b_ref, dact_ref, dup_ref, dw_ref, db_ref, after_scr):
        step = pl.program_id(1)
        first = step == 0
        (val, gate), taps = _conv_halves(up_ref, halo_ref, w_ref, b_ref, tm, FS, step == nt - 1)
        _, vjp = jax.vjp(_glu_math, val, gate)
        dcs = vjp(dact_ref[...])
        for s in (0, 1):
            dc = dcs[s]
            n0 = jnp.where(first, 0.0, after_scr[s, 0:1, :])
            n1 = jnp.where(first, 0.0, after_scr[s, 1:2, :])
            x1 = _shifted_rows(dc, -1, [n0])
            x2 = _shifted_rows(dc, -2, [n0, n1])
            dup_ref[s] = (dc * w_ref[s, 2:3, :] + x1 * w_ref[s, 1:2, :] + x2 * w_ref[s, 0:1, :]).astype(BF16)
            after_scr[s] = dc[0:8]
            sums = [jnp.sum(dc * t, axis=0, keepdims=True) for t in taps[s]]
            dbs = jnp.sum(dc, axis=0, keepdims=True)

            @pl.when(first)
            def _():
                for t in range(3):
                    dw_ref[s, t:t + 1, :] = sums[t]
                db_ref[s] = dbs

            @pl.when(jnp.logical_not(first))
            def _():
                for t in range(3):
                    dw_ref[s, t:t + 1, :] += sums[t]
                db_ref[s] += dbs

    f = lambda *s: jax.ShapeDtypeStruct(s, F32)
    return _pcall(body, name="convglu_bwd", grid=(NSH, nt),
                  in_specs=[S["up"], S["halo"], S["w"], S["b"], S["act"]],
                  out_specs=[S["up"], S["w"], S["b"]],
                  out_shape=[jax.ShapeDtypeStruct((2, NSH, L, FS), BF16), f(2, NSH, 3, FS), f(2, NSH, 1, FS)],
                  scratch=[pltpu.VMEM((2, 8, FS), F32)])(up, up, w, b, dact)


def _loss_head(y, target):
    L, D = y.shape
    tm = _row_tile(L)

    def body(y_ref, t_ref, loss_ref, dy_ref):
        err = y_ref[...] - t_ref[...]
        dy_ref[...] = err / D
        part = 0.5 * jnp.sum(jnp.mean(err * err, axis=-1, keepdims=True), axis=0, keepdims=True)
        _accumulate(loss_ref, jnp.broadcast_to(part, (1, LANES)), pl.program_id(0) == 0)

    row = pl.BlockSpec((tm, D), lambda i: (i, 0))
    vec = pl.BlockSpec((1, LANES), lambda i: (0, 0))
    return _pcall(body, name="loss_head", grid=(L // tm,), in_specs=[row, row], out_specs=[vec, row],
                  out_shape=[jax.ShapeDtypeStruct((1, LANES), F32), jax.ShapeDtypeStruct((L, D), F32)])(y, target)


def _ada_fwd(c_all, ada_w, ada_b):
    NL, D, NC = ada_w.shape
    NB = c_all.shape[0]

    def body(c_ref, w_ref, b_ref, o_ref):
        cact = jax.nn.silu(c_ref[...])
        o_ref[...] = _dot(cact.astype(BF16), w_ref[...].astype(BF16)) + b_ref[...]

    return _pcall(body, name="ada_fwd", grid=(NL,),
                  in_specs=[pl.BlockSpec((NB, D), lambda l: (0, 0)), pl.BlockSpec((None, D, NC), lambda l: (l, 0, 0)),
                            pl.BlockSpec((None, 1, NC), lambda l: (l, 0, 0))],
                  out_specs=pl.BlockSpec((None, NB, NC), lambda l: (l, 0, 0)),
                  out_shape=jax.ShapeDtypeStruct((NL, NB, NC), F32))(c_all, ada_w, ada_b)


def _ada_bwd(c_all_t, dmod):
    D, NB = c_all_t.shape
    NL, _, NC = dmod.shape

    def body(c_ref, d_ref, o_ref):
        cact = jax.nn.silu(c_ref[...]).astype(BF16).astype(F32)
        o_ref[...] = _dot(cact, d_ref[...].astype(BF16).astype(F32))

    return _pcall(body, name="ada_bwd", grid=(NL,),
                  in_specs=[pl.BlockSpec((D, NB), lambda l: (0, 0)), pl.BlockSpec((None, NB, NC), lambda l: (l, 0, 0))],
                  out_specs=pl.BlockSpec((None, D, NC), lambda l: (l, 0, 0)),
                  out_shape=jax.ShapeDtypeStruct((NL, D, NC), F32))(c_all_t, dmod)


def _adamw(parts, w, m, v):
    NL, P, R, C = parts.shape
    tr = R if R <= 512 else max(t for t in range(16, 513, 16) if R % t == 0)

    def body(p_ref, w_ref, m_ref, v_ref, g_ref, d_ref, nm_ref, nv_ref):
        g = p_ref[0].astype(F32)
        for k in range(1, P):
            g = g + p_ref[k].astype(F32)
        m2 = ADAM_B1 * m_ref[...] + (1.0 - ADAM_B1) * g
        v2 = ADAM_B2 * v_ref[...] + (1.0 - ADAM_B2) * jnp.square(g)
        m_hat = m2 / (1.0 - ADAM_B1 ** ADAM_STEP)
        v_hat = v2 / (1.0 - ADAM_B2 ** ADAM_STEP)
        g_ref[...] = g
        d_ref[...] = -ADAM_LR * (m_hat / (jnp.sqrt(v_hat) + ADAM_EPS) + ADAM_WD * w_ref[...])
        nm_ref[...] = m2
        nv_ref[...] = v2

    pspec = pl.BlockSpec((None, P, tr, C), lambda l, i: (l, 0, i, 0))
    wspec = pl.BlockSpec((None, tr, C), lambda l, i: (l, i, 0))
    shp = jax.ShapeDtypeStruct((NL, R, C), F32)
    return _pcall(body, name="adamw", grid=(NL, R // tr), in_specs=[pspec, wspec, wspec, wspec],
                  out_specs=[wspec] * 4, out_shape=[shp] * 4)(parts, w, m, v)


def _block_diag(blocks):
    *lead, g, r, c = blocks.shape
    eye = jnp.eye(g, dtype=bool)[:, None, :, None]
    full = jnp.where(eye, blocks[..., :, :, None, :], 0.0)
    return full.reshape(*lead, g * r, g * c)


def _block_diag_extract(m, r, c):
    g = GROUPS_PER_TILE
    m5 = m.reshape(*m.shape[:-2], g, r, g, c)
    eye = jnp.eye(g, dtype=bool)[:, None, :, None]
    return jnp.sum(jnp.where(eye, m5, 0.0), axis=-2)


def kernel(x, c, ada_w, ada_b, norm1_g, w_in, q_norm_g, k_norm_g, ssm_a_re, ssm_a_im, ssm_log_dt, ssm_b_re, ssm_b_im, ssm_c_re, ssm_c_im, ssm_d, glu_w, glu_b, attn_out_g, ssm_out_g, w_out, norm2_g, ffn_w_up, ffn_conv_w, ffn_conv_b, ffn_w_down, loss_target, m_ada_w, m_ada_b, m_norm1_g, m_w_in, m_q_norm_g, m_k_norm_g, m_ssm_a_re, m_ssm_a_im, m_ssm_log_dt, m_ssm_b_re, m_ssm_b_im, m_ssm_c_re, m_ssm_c_im, m_ssm_d, m_glu_w, m_glu_b, m_attn_out_g, m_ssm_out_g, m_w_out, m_norm2_g, m_ffn_w_up, m_ffn_conv_w, m_ffn_conv_b, m_ffn_w_down, v_ada_w, v_ada_b, v_norm1_g, v_w_in, v_q_norm_g, v_k_norm_g, v_ssm_a_re, v_ssm_a_im, v_ssm_log_dt, v_ssm_b_re, v_ssm_b_im, v_ssm_c_re, v_ssm_c_im, v_ssm_d, v_glu_w, v_glu_b, v_attn_out_g, v_ssm_out_g, v_w_out, v_norm2_g, v_ffn_w_up, v_ffn_conv_w, v_ffn_conv_b, v_ffn_w_down):
    weights = dict(ada_w=ada_w, ada_b=ada_b, norm1_g=norm1_g, w_in=w_in, q_norm_g=q_norm_g, k_norm_g=k_norm_g,
                   ssm_a_re=ssm_a_re, ssm_a_im=ssm_a_im, ssm_log_dt=ssm_log_dt, ssm_b_re=ssm_b_re,
                   ssm_b_im=ssm_b_im, ssm_c_re=ssm_c_re, ssm_c_im=ssm_c_im, ssm_d=ssm_d, glu_w=glu_w, glu_b=glu_b,
                   attn_out_g=attn_out_g, ssm_out_g=ssm_out_g, w_out=w_out, norm2_g=norm2_g, ffn_w_up=ffn_w_up,
                   ffn_conv_w=ffn_conv_w, ffn_conv_b=ffn_conv_b, ffn_w_down=ffn_w_down)
    mom_m = dict(ada_w=m_ada_w, ada_b=m_ada_b, norm1_g=m_norm1_g, w_in=m_w_in, q_norm_g=m_q_norm_g,
                 k_norm_g=m_k_norm_g, ssm_a_re=m_ssm_a_re, ssm_a_im=m_ssm_a_im, ssm_log_dt=m_ssm_log_dt,
                 ssm_b_re=m_ssm_b_re, ssm_b_im=m_ssm_b_im, ssm_c_re=m_ssm_c_re, ssm_c_im=m_ssm_c_im, ssm_d=m_ssm_d,
                 glu_w=m_glu_w, glu_b=m_glu_b, attn_out_g=m_attn_out_g, ssm_out_g=m_ssm_out_g, w_out=m_w_out,
                 norm2_g=m_norm2_g, ffn_w_up=m_ffn_w_up, ffn_conv_w=m_ffn_conv_w, ffn_conv_b=m_ffn_conv_b,
                 ffn_w_down=m_ffn_w_down)
    mom_v = dict(ada_w=v_ada_w, ada_b=v_ada_b, norm1_g=v_norm1_g, w_in=v_w_in, q_norm_g=v_q_norm_g,
                 k_norm_g=v_k_norm_g, ssm_a_re=v_ssm_a_re, ssm_a_im=v_ssm_a_im, ssm_log_dt=v_ssm_log_dt,
                 ssm_b_re=v_ssm_b_re, ssm_b_im=v_ssm_b_im, ssm_c_re=v_ssm_c_re, ssm_c_im=v_ssm_c_im, ssm_d=v_ssm_d,
                 glu_w=v_glu_w, glu_b=v_glu_b, attn_out_g=v_attn_out_g, ssm_out_g=v_ssm_out_g, w_out=v_w_out,
                 norm2_g=v_norm2_g, ffn_w_up=v_ffn_w_up, ffn_conv_w=v_ffn_conv_w, ffn_conv_b=v_ffn_conv_b,
                 ffn_w_down=v_ffn_w_down)
    names = list(weights)
    big = ("ada_w", "w_in", "w_out", "ffn_w_up", "ffn_conv_w", "ffn_w_down")
    small = [n for n in names if n not in big]

    x = x[0]
    target = loss_target[0]
    L, D = x.shape
    NL = ada_w.shape[0]
    AW = D // 2
    SW = D - AW
    NH = AW // HEAD_DIM
    HP = AW // LANES
    G = SW // SSM_GROUP
    GT = SW // LANES
    NS = G * SSM_STATE
    NIN = w_in.shape[-1]
    FS = ffn_w_up.shape[-1]
    NSH = NDEV // 2
    NCA = ada_w.shape[-1]
    ROWS_OUT = w_out.shape[1]
    ROWS_DOWN = ffn_w_down.shape[1]
    B_ATT = min(L, 256)
    T_S5 = min(L, 1024)
    tm = _row_tile(L, 1024)
    ts = _row_tile(L, 512)
    ts2 = _row_tile(L, 256)
    me = _my_index()

    cpad = jnp.reshape(c, (D // LANES, LANES))
    c_all = _gather_small(cpad, "gather_c")
    c_all = c_all.reshape(NDEV, D)
    shards = [w_in.astype(BF16), w_out.astype(BF16), ffn_w_up.astype(BF16), ffn_conv_w, ffn_w_down.astype(BF16)]
    (first_w_in,) = _gather_shards([shards[0][0:1]], "gather_weights")

    def set_w_in(lp, w_in_g):
        lp.update(w_in=w_in_g, w_in_cols=jnp.swapaxes(w_in_g, 0, 1).reshape(D, NDEV * NIN))

    def set_other_weights(lp, gathered):
        w_out_g, w_up_g, conv_w_g, w_down_g = gathered
        lp.update(w_out=w_out_g.reshape(D, D), w_up=w_up_g, conv_w=conv_w_g.reshape(2, NSH, 3, FS),
                  w_down=w_down_g.reshape(NSH, 2 * ROWS_DOWN, D))

    conv_b_g = ffn_conv_b.reshape(NL, 2, NSH, 1, FS)

    ada_b_mine = lax.dynamic_slice_in_dim(ada_b, me * NCA, NCA, axis=1).reshape(NL, 1, NCA)
    mod_part = _ada_fwd(c_all, ada_w, ada_b_mine)
    mod_all = _gather_small(mod_part, "gather_mod")
    mod = lax.dynamic_index_in_dim(mod_all, me, axis=2, keepdims=False)
    mod = jnp.transpose(mod, (1, 0, 2)).reshape(NL, N_MOD, 1, D)

    row = lambda a: a.reshape(NL, 1, NS)
    ar_row, ai_row = row(ssm_a_re), row(ssm_a_im)
    ldt_row = row(jnp.broadcast_to(ssm_log_dt[:, :, None], (NL, G, SSM_STATE)))
    tiles = lambda a: a.reshape((NL, GT, GROUPS_PER_TILE) + a.shape[2:])
    braw_r = _block_diag(jnp.swapaxes(tiles(ssm_b_re), -1, -2))
    braw_i = _block_diag(jnp.swapaxes(tiles(ssm_b_im), -1, -2))
    crT = _block_diag(jnp.swapaxes(tiles(ssm_c_re), -1, -2)).astype(BF16)
    ciT = _block_diag(jnp.swapaxes(tiles(ssm_c_im), -1, -2)).astype(BF16)
    wg = _block_diag(tiles(glu_w)).astype(BF16)
    abr, abi, tabr, tabi, bsr, bsi = _s5_prep(ar_row, ai_row, ldt_row, braw_r, braw_i, T_S5)
    s5p = dict(bsr=bsr, bsi=bsi, tabr=tabr, tabi=tabi, crT=crT, ciT=ciT,
               d=ssm_d.reshape(NL, 1, SW), wg=wg, gb=glu_b.reshape(NL, 1, SW))

    gqk = jnp.concatenate([jnp.tile(q_norm_g, (1, NH)), jnp.tile(k_norm_g, (1, NH))], axis=1).reshape(NL, 1, 2 * AW)
    layer_params = dict(
        mod=mod, norm1_g=norm1_g.reshape(NL, 1, D), norm2_g=norm2_g.reshape(NL, 1, D), gqk=gqk,
        ga=attn_out_g.reshape(NL, 1, AW), gs=ssm_out_g.reshape(NL, 1, SW),
        conv_b=conv_b_g, s5=s5p)


    tn = min(D, 512)

    def resid_epilogue(acc, xres, gate):
        return acc, xres + gate * acc

    def layer_fwd(xin, lp, ride):
        sh1, sc1, g1, sh2, sc2, g2 = (lp["mod"][k] for k in range(N_MOD))
        h = _lnmod_fwd(xin, lp["norm1_g"], sh1, sc1)
        p = _matmul(
            "mm_in", h, lp["w_in_cols"], dims=NN, grid=(L // ts,),
            a_spec=pl.BlockSpec((ts, D), lambda i: (i, 0)), b_spec=pl.BlockSpec((D, NDEV * NIN), lambda i: (0, 0)),
            out_shape=jax.ShapeDtypeStruct((L, NDEV * NIN), F32),
            out_specs=pl.BlockSpec((ts, NDEV * NIN), lambda i: (i, 0)))
        qk = _qknorm_fwd(p, lp["gqk"], AW)
        o_attn, tot, gathered = _attn_fwd(qk, p, AW, B_ATT, ride)
        set_other_weights(lp, gathered[:4])
        next_gathered = gathered[4:]
        u_st = _to_streams(p[:, 3 * AW:], T_S5)
        o_st, s_re, s_im, ypre = _s5_fwd(u_st, lp["s5"], T_S5)
        o_ssm = _from_streams(o_st, T_S5)
        o = _outnorm_fwd(o_attn, o_ssm, lp["ga"], lp["gs"])
        a1, x_mid = _matmul(
            "mm_out", o, lp["w_out"], dims=NN, grid=(L // tm, D // tn),
            a_spec=pl.BlockSpec((tm, D), lambda i, j: (i, 0)), b_spec=pl.BlockSpec((D, tn), lambda i, j: (0, j)),
            extra=(xin, g1), extra_specs=(pl.BlockSpec((tm, tn), lambda i, j: (i, j)),
                                          pl.BlockSpec((1, tn), lambda i, j: (0, j))),
            epilogue=resid_epilogue,
            out_shape=[jax.ShapeDtypeStruct((L, D), F32)] * 2,
            out_specs=[pl.BlockSpec((tm, tn), lambda i, j: (i, j))] * 2)
        h2 = _lnmod_fwd(x_mid, lp["norm2_g"], sh2, sc2)
        up = _matmul(
            "mm_up", h2, lp["w_up"], dims=NN, grid=(L // tm, NDEV),
            a_spec=pl.BlockSpec((tm, D), lambda i, j: (i, 0)), b_spec=pl.BlockSpec((None, D, FS), lambda i, j: (j, 0, 0)),
            out_shape=jax.ShapeDtypeStruct((NDEV, L, FS), F32),
            out_specs=pl.BlockSpec((None, tm, FS), lambda i, j: (j, i, 0)))
        up = up.reshape(2, NSH, L, FS)
        act = _convglu_fwd(up, lp["conv_w"], lp["conv_b"])
        a2, x_out = _matmul(
            "mm_down", act, lp["w_down"], dims=NN, grid=(L // ts, D // tn),
            a_spec=pl.BlockSpec((NSH, ts, FS), lambda i, j: (0, i, 0)),
            b_spec=pl.BlockSpec((NSH, FS, tn), lambda i, j: (0, 0, j)),
            extra=(x_mid, g2), extra_specs=(pl.BlockSpec((ts, tn), lambda i, j: (i, j)),
                                            pl.BlockSpec((1, tn), lambda i, j: (0, j))),
            epilogue=resid_epilogue,
            out_shape=[jax.ShapeDtypeStruct((L, D), F32)] * 2,
            out_specs=[pl.BlockSpec((ts, tn), lambda i, j: (i, j))] * 2)
        res = dict(x=xin, h=h, p=p, qk=qk, tot=tot, o_attn=o_attn, o_ssm=o_ssm, u_st=u_st, s_re=s_re, s_im=s_im, ypre=ypre,
                   o=o, a1=a1, x_mid=x_mid, h2=h2, up=up, act=act, a2=a2)
        return x_out, res, next_gathered

    per_layer = [jax.tree.map(lambda a: a[l], layer_params) for l in range(NL)]
    y, residuals = x, []
    set_w_in(per_layer[0], first_w_in[0])
    for l in range(NL):
        ride = [s[l] for s in shards[1:]] + ([shards[0][l + 1]] if l + 1 < NL else [])
        y, res, next_w_in = layer_fwd(y, per_layer[l], ride)
        if next_w_in:
            set_w_in(per_layer[l + 1], next_w_in[0])
        residuals.append(res)

    loss_row, dy = _loss_head(y, target)
    loss = lax.psum(loss_row[0, 0], ("x", "y", "c"))

    def layer_bwd(dx, args, above):
        lp, r = args
        sh1, sc1, g1, sh2, sc2, g2 = (lp["mod"][k] for k in range(N_MOD))
        da2, dg2 = _gate_bwd(dx, r["a2"], g2)
        dact = _matmul(
            "mm_dact", da2, lp["w_down"], dims=NT, grid=(L // tm, NSH),
            a_spec=pl.BlockSpec((tm, D), lambda i, j: (i, 0)), b_spec=pl.BlockSpec((None, FS, D), lambda i, j: (j, 0, 0)),
            out_shape=jax.ShapeDtypeStruct((NSH, L, FS), F32),
            out_specs=pl.BlockSpec((None, tm, FS), lambda i, j: (j, i, 0)))
        dw_down = _matmul(
            "mm_dw_down", r["act"], da2, dims=TN, grid=(NSH, D // tn),
            a_spec=pl.BlockSpec((None, L, FS), lambda j, n: (j, 0, 0)),
            b_spec=pl.BlockSpec((L, tn), lambda j, n: (0, n)),
            out_shape=jax.ShapeDtypeStruct((NSH, FS, D), BF16),
            out_specs=pl.BlockSpec((None, FS, tn), lambda j, n: (j, 0, n)))
        dup4, dcw, dcb = _convglu_bwd(r["up"], dact, lp["conv_w"], lp["conv_b"])
        dup = dup4.reshape(NDEV, L, FS)
        dh2 = _matmul(
            "mm_dh2", dup, lp["w_up"], dims=NT, grid=(L // ts2,),
            a_spec=pl.BlockSpec((NDEV, ts2, FS), lambda i: (0, i, 0)),
            b_spec=pl.BlockSpec((NDEV, D, FS), lambda i: (0, 0, 0)),
            out_shape=jax.ShapeDtypeStruct((L, D), F32), out_specs=pl.BlockSpec((ts2, D), lambda i: (i, 0)))
        dw_up = _matmul(
            "mm_dw_up", r["h2"], dup, dims=TN, grid=(NDEV,),
            a_spec=pl.BlockSpec((L, D), lambda j: (0, 0)), b_spec=pl.BlockSpec((None, L, FS), lambda j: (j, 0, 0)),
            out_shape=jax.ShapeDtypeStruct((NDEV, D, FS), BF16),
            out_specs=pl.BlockSpec((None, D, FS), lambda j: (j, 0, 0)))
        dxm, dn2, dsh2, dsc2 = _lnmod_bwd(dh2, r["x_mid"], lp["norm2_g"], sh2, sc2, dx)
        da1, dg1 = _gate_bwd(dxm, r["a1"], g1)
        do = _matmul(
            "mm_do", da1, lp["w_out"], dims=NT, grid=(L // tm, D // tn),
            a_spec=pl.BlockSpec((tm, D), lambda i, j: (i, 0)), b_spec=pl.BlockSpec((tn, D), lambda i, j: (j, 0)),
            out_shape=jax.ShapeDtypeStruct((L, D), F32), out_specs=pl.BlockSpec((tm, tn), lambda i, j: (i, j)))
        dw_out = _matmul(
            "mm_dw_out", r["o"], da1, dims=TN, grid=(D // tn, D // tn),
            a_spec=pl.BlockSpec((L, tn), lambda m, n: (0, m)), b_spec=pl.BlockSpec((L, tn), lambda m, n: (0, n)),
            out_shape=jax.ShapeDtypeStruct((D, D), BF16), out_specs=pl.BlockSpec((tn, tn), lambda m, n: (m, n)))
        doa, dos, dga, dgs = _outnorm_bwd(do, r["o_attn"], r["o_ssm"], lp["ga"], lp["gs"])
        (du_st, dbsr, dbsi, dcr, dci, dd, dwg, dgb, dabr, dabi) = _s5_bwd(
            _to_streams(dos, T_S5), r["u_st"], r["s_re"], r["s_im"], r["ypre"], lp["s5"], T_S5)
        du = _from_streams(du_st, T_S5)
        ride = [dw_out.reshape(NDEV, ROWS_OUT, D), dw_up, dcw.reshape(NDEV, 3, FS),
                dw_down.reshape(NDEV, ROWS_DOWN, D)] + above
        dq, dk, dv, received = _attn_bwd(r["qk"], r["p"], doa, r["tot"], AW, B_ATT, ride)
        dqk, dgqk = _qknorm_bwd(jnp.concatenate([dq, dk], axis=1), r["p"], lp["gqk"], AW)
        dp = jnp.concatenate([dqk, dv.astype(BF16), du], axis=1)
        dh = _matmul(
            "mm_dh", dp, lp["w_in_cols"], dims=NT, grid=(L // tm,),
            a_spec=pl.BlockSpec((tm, NDEV * NIN), lambda i: (i, 0)), b_spec=pl.BlockSpec((D, NDEV * NIN), lambda i: (0, 0)),
            out_shape=jax.ShapeDtypeStruct((L, D), F32), out_specs=pl.BlockSpec((tm, D), lambda i: (i, 0)))
        dw_in = _matmul(
            "mm_dw_in", r["h"], dp, dims=TN, grid=(NDEV,),
            a_spec=pl.BlockSpec((L, D), lambda j: (0, 0)), b_spec=pl.BlockSpec((L, NIN), lambda j: (0, j)),
            out_shape=jax.ShapeDtypeStruct((NDEV, D, NIN), BF16),
            out_specs=pl.BlockSpec((None, D, NIN), lambda j: (j, 0, 0)))
        dx0, dn1, dsh1, dsc1 = _lnmod_bwd(dh, r["x"], lp["norm1_g"], sh1, sc1, dxm)
        grads = dict(
            dmod=jnp.concatenate([dsh1, dsc1, dg1, dsh2, dsc2, dg2], axis=1), dn1=dn1, dn2=dn2, dgqk=dgqk,
            dga=dga, dgs=dgs, dbsr=dbsr, dbsi=dbsi, dcr=dcr, dci=dci, dd=dd, dwg=dwg, dgb=dgb, dabr=dabr, dabi=dabi,
            dcb=dcb)
        return dx0, grads, dw_in, received

    grad_x, layer_grads, parts, above = dy, [None] * NL, [[None] * 5 for _ in range(NL)], []
    for l in reversed(range(NL)):
        grad_x, layer_grads[l], dw_in_l, received = layer_bwd(grad_x, (per_layer[l], residuals[l]), above)
        parts[l][1:] = received[:4]
        if above:
            parts[l + 1][0] = received[4]
        above = [dw_in_l]
    (parts[0][0],) = _scatter_direct(above, "scatter_weight_grads")
    gr = jax.tree.map(lambda *a: jnp.stack(a), *layer_grads)

    dar, dai, dldt, dbr_bd, dbi_bd = _s5_prep_bwd(ar_row, ai_row, ldt_row, braw_r, braw_i,
                                                  gr["dabr"], gr["dabi"], gr["dbsr"], gr["dbsi"])
    unt = lambda a: a.reshape((NL, G) + a.shape[3:])
    local = dict(
        ada_b=gr["dmod"].reshape(NL, N_MOD * D),
        norm1_g=gr["dn1"].reshape(NL, D), norm2_g=gr["dn2"].reshape(NL, D),
        q_norm_g=gr["dgqk"].reshape(NL, 2, NH, HEAD_DIM)[:, 0].sum(axis=1),
        k_norm_g=gr["dgqk"].reshape(NL, 2, NH, HEAD_DIM)[:, 1].sum(axis=1),
        ssm_a_re=dar.reshape(NL, G, SSM_STATE), ssm_a_im=dai.reshape(NL, G, SSM_STATE),
        ssm_log_dt=dldt.reshape(NL, G, SSM_STATE).sum(axis=-1),
        ssm_b_re=jnp.swapaxes(unt(_block_diag_extract(dbr_bd, SSM_GROUP, SSM_STATE)), -1, -2),
        ssm_b_im=jnp.swapaxes(unt(_block_diag_extract(dbi_bd, SSM_GROUP, SSM_STATE)), -1, -2),
        ssm_c_re=jnp.swapaxes(unt(_block_diag_extract(gr["dcr"], SSM_STATE, SSM_GROUP)), -1, -2),
        ssm_c_im=jnp.swapaxes(unt(_block_diag_extract(gr["dci"], SSM_STATE, SSM_GROUP)), -1, -2),
        ssm_d=gr["dd"].reshape(NL, G, SSM_GROUP),
        glu_w=unt(_block_diag_extract(gr["dwg"], SSM_GROUP, SSM_GROUP)),
        glu_b=gr["dgb"].reshape(NL, G, SSM_GROUP),
        attn_out_g=gr["dga"].reshape(NL, AW), ssm_out_g=gr["dgs"].reshape(NL, SW),
        ffn_conv_b=gr["dcb"].reshape(NL, 2 * NSH * FS),
    )

    def pack(tree):
        flat = jnp.concatenate([tree[n].reshape(-1) for n in small])
        pad = (-flat.shape[0]) % (512 * LANES)
        return jnp.pad(flat, (0, pad)).reshape(1, -1, LANES)

    (small_parts,) = _gather_shards([pack(local)], "gather_small_grads")
    sg, sd, sm, sv = _adamw(small_parts, pack(weights), pack(mom_m), pack(mom_v))

    def unpack(buf):
        flat = buf.reshape(-1)
        out, off = {}, 0
        for n in small:
            size = weights[n].size
            out[n] = flat[off:off + size].reshape(weights[n].shape)
            off += size
        return out

    ug, ud, um, uv = unpack(sg), unpack(sd), unpack(sm), unpack(sv)
    results = {n: (ug[n], ud[n], um[n], uv[n]) for n in small}

    dmod_all = _gather_small(gr["dmod"].reshape(NL, N_MOD * D), "gather_dmod")
    dmod_mine = lax.dynamic_slice_in_dim(dmod_all, me * NCA, NCA, axis=2)
    d_ada_w = _ada_bwd(jnp.transpose(c_all), jnp.transpose(dmod_mine, (1, 0, 2)))
    results["ada_w"] = tuple(_adamw(d_ada_w[:, None], ada_w, m_ada_w, v_ada_w))

    for a, n in enumerate(("w_in", "w_out", "ffn_w_up", "ffn_conv_w", "ffn_w_down")):
        results[n] = tuple(_adamw(jnp.stack([parts[l][a] for l in range(NL)]), weights[n], mom_m[n], mom_v[n]))

    out = [loss, grad_x[None]]
    for k in range(4):
        out.extend(results[n][k] for n in names)
    return tuple(out)
```

```python
import jax
import jax.numpy as jnp
from jax import lax
from jax.experimental import pallas as pl
from jax.experimental.pallas import tpu as pltpu

F32 = jnp.float32
BF16 = jnp.bfloat16
NDEV = 8
LANES = 128
HEAD_DIM = 64
SSM_GROUP = 16
SSM_STATE = 64
GROUPS_PER_TILE = LANES // SSM_GROUP
STATE_TILE = GROUPS_PER_TILE * SSM_STATE
N_MOD = 6
ATTN_STRIP = 32
EPS = 1e-6
ADAM_LR, ADAM_B1, ADAM_B2, ADAM_EPS, ADAM_WD, ADAM_STEP = 0.001, 0.9, 0.999, 1e-08, 0.01, 10
VMEM_LIMIT = 48 * 1024 * 1024
MESH_IDS = pl.DeviceIdType.MESH

NN = (((1,), (0,)), ((), ()))
NT = (((1,), (1,)), ((), ()))
TN = (((0,), (0,)), ((), ()))


def _dot(a, b, dims=NN):
    return lax.dot_general(a, b, dims, preferred_element_type=F32)


def _pcall(body, *, name, out_shape, in_specs, out_specs, grid=(), scratch=()):
    return pl.pallas_call(
        body, name=name, grid=grid, in_specs=in_specs, out_specs=out_specs, out_shape=out_shape,
        scratch_shapes=list(scratch),
        compiler_params=pltpu.CompilerParams(vmem_limit_bytes=VMEM_LIMIT))


def _row_tile(n, want=512):
    t = min(n, want)
    assert n % t == 0
    return t


def _my_index():
    return 4 * lax.axis_index("x") + 2 * lax.axis_index("y") + lax.axis_index("c")


HBM_SPEC = pl.BlockSpec(memory_space=pltpu.HBM)


def _mesh_place():
    x, y, c = lax.axis_index("x"), lax.axis_index("y"), lax.axis_index("c")
    chips = [(1 - x, y), (x, 1 - y), (1 - x, 1 - y)]
    return x, y, c, chips


def _gather_small(arr, name):
    def body(in_ref, out_ref, send_sems, recv_sems, local_sem):
        x, y, c, _ = _mesh_place()
        me = 4 * x + 2 * y + c
        own = pltpu.make_async_copy(in_ref, out_ref.at[me], local_sem)
        own.start()
        sends, recvs = [], []
        for k in range(1, NDEV):
            px = 1 - x if k & 4 else x
            py = 1 - y if k & 2 else y
            pc = 1 - c if k & 1 else c
            common = dict(send_sem=send_sems.at[k - 1], recv_sem=recv_sems.at[k - 1],
                          device_id=(px, py, pc), device_id_type=MESH_IDS)
            snd = pltpu.make_async_remote_copy(src_ref=in_ref, dst_ref=out_ref.at[me], **common)
            snd.start()
            sends.append(snd)
            recvs.append(pltpu.make_async_remote_copy(
                src_ref=in_ref, dst_ref=out_ref.at[4 * px + 2 * py + pc], **common))
        for r in recvs:
            r.wait_recv()
        for s in sends:
            s.wait_send()
        own.wait()

    return pl.pallas_call(
        body, name=name, out_shape=jax.ShapeDtypeStruct((NDEV,) + arr.shape, arr.dtype),
        in_specs=[HBM_SPEC], out_specs=HBM_SPEC,
        scratch_shapes=[pltpu.SemaphoreType.DMA((NDEV - 1,)), pltpu.SemaphoreType.DMA((NDEV - 1,)),
                        pltpu.SemaphoreType.DMA(())],
    )(arr)


def _all_to_all(srcs, dsts, send_sems, recv_sems, scatter):
    x, y, c, _ = _mesh_place()
    me = 4 * x + 2 * y + c
    pairs = []
    for k in range(1, NDEV):
        px = 1 - x if k & 4 else x
        py = 1 - y if k & 2 else y
        pc = 1 - c if k & 1 else c
        peer = 4 * px + 2 * py + pc
        for a in range(len(srcs)):
            src = srcs[a].at[peer] if scatter else srcs[a]
            common = dict(send_sem=send_sems.at[k - 1, a], recv_sem=recv_sems.at[k - 1, a],
                          device_id=(px, py, pc), device_id_type=MESH_IDS)
            pairs.append((pltpu.make_async_remote_copy(src_ref=src, dst_ref=dsts[a].at[me], **common),
                          pltpu.make_async_remote_copy(src_ref=src, dst_ref=dsts[a].at[peer], **common)))

    def start():
        for send, _ in pairs:
            send.start()

    def wait():
        for _, arrival in pairs:
            arrival.wait_recv()
        for send, _ in pairs:
            send.wait_send()

    return start, wait


def _place_own(outs, arrs, scatter):
    me = _my_index()
    owns = [lax.dynamic_slice_in_dim(a, me, 1, axis=0) if scatter else a[None] for a in arrs]
    return [lax.dynamic_update_slice_in_dim(o, own, me, axis=0) for o, own in zip(outs, owns)]


def _exchange_shapes(arrs, scatter):
    return [jax.ShapeDtypeStruct(a.shape if scatter else (NDEV,) + a.shape, a.dtype) for a in arrs]


def _scatter_direct(arrs, name):
    n = len(arrs)

    def body(*refs):
        start, wait = _all_to_all(refs[:n], refs[n:2 * n], refs[2 * n], refs[2 * n + 1], True)
        start()
        wait()

    outs = pl.pallas_call(
        body, name=name, out_shape=_exchange_shapes(arrs, True), in_specs=[HBM_SPEC] * n, out_specs=[HBM_SPEC] * n,
        scratch_shapes=[pltpu.SemaphoreType.DMA((NDEV - 1, n)), pltpu.SemaphoreType.DMA((NDEV - 1, n))],
    )(*arrs)
    return _place_own(outs, arrs, True)


def _gather_shards(arrs, name):
    n = len(arrs)

    def body(*refs):
        ins, outs = refs[:n], refs[n:2 * n]
        send_sems, recv_sems = refs[2 * n:]
        x, y, c, chips = _mesh_place()
        dev = lambda px, py, pc: 4 * px + 2 * py + pc

        def copy(k, a, src, block, to):
            return pltpu.make_async_remote_copy(
                src_ref=src, dst_ref=outs[a].at[:, block], send_sem=send_sems.at[k, a],
                recv_sem=recv_sems.at[k, a], device_id=to, device_id_type=MESH_IDS)

        me = dev(x, y, c)
        sent = [copy(0, a, ins[a], me, (x, y, 1 - c)) for a in range(n)]
        sent += [copy(1 + j, a, ins[a], me, (px, py, c)) for j, (px, py) in enumerate(chips) for a in range(n)]
        for s in sent:
            s.start()
        for j, (px, py) in enumerate(chips):
            for a in range(n):
                blk = dev(px, py, c)
                copy(1 + j, a, ins[a], blk, (x, y, c)).wait_recv()
                forward = copy(4 + j, a, outs[a].at[:, blk], blk, (x, y, 1 - c))
                forward.start()
                sent.append(forward)
        for a in range(n):
            copy(0, a, ins[a], dev(x, y, 1 - c), (x, y, c)).wait_recv()
        for j, (px, py) in enumerate(chips):
            for a in range(n):
                copy(4 + j, a, ins[a], dev(px, py, 1 - c), (x, y, c)).wait_recv()
        for s in sent:
            s.wait_send()

    out_shape = [jax.ShapeDtypeStruct((a.shape[0], NDEV) + a.shape[1:], a.dtype) for a in arrs]
    outs = pl.pallas_call(
        body, name=name, out_shape=out_shape, in_specs=[HBM_SPEC] * n, out_specs=[HBM_SPEC] * n,
        scratch_shapes=[pltpu.SemaphoreType.DMA((7, n)), pltpu.SemaphoreType.DMA((7, n))],
    )(*arrs)
    me = _my_index()
    return [lax.dynamic_update_slice_in_dim(o, a[:, None], me, axis=1) for o, a in zip(outs, arrs)]


def _matmul(name, a, b, *, dims, grid, a_spec, b_spec, out_shape, out_specs, kaxis=None, acc_shape=None,
            extra=(), extra_specs=(), epilogue=None):
    nk = grid[kaxis] if kaxis is not None else 1
    ne = len(extra)
    multi = isinstance(out_shape, (list, tuple))
    n_out = len(out_shape) if multi else 1

    def body(*refs):
        a_ref, b_ref = refs[0], refs[1]
        ex = refs[2:2 + ne]
        outs = refs[2 + ne:2 + ne + n_out]

        def write(res):
            vals = epilogue(res, *[e[...] for e in ex]) if epilogue is not None else (res,)
            for o, v in zip(outs, vals):
                o[...] = v.astype(o.dtype)

        if len(a_ref.shape) == 3:
            part = _dot(a_ref[0].astype(BF16), b_ref[0].astype(BF16), dims)
            for s in range(1, a_ref.shape[0]):
                part = part + _dot(a_ref[s].astype(BF16), b_ref[s].astype(BF16), dims)
        else:
            part = _dot(a_ref[...].astype(BF16), b_ref[...].astype(BF16), dims)
        if nk == 1:
            write(part)
        else:
            acc = refs[-1]
            k = pl.program_id(kaxis)

            @pl.when(k == 0)
            def _():
                acc[...] = part

            @pl.when(k > 0)
            def _():
                acc[...] += part

            @pl.when(k == nk - 1)
            def _():
                write(acc[...])

    scratch = [pltpu.VMEM(acc_shape, F32)] if nk > 1 else []
    return _pcall(body, name=name, grid=grid, in_specs=[a_spec, b_spec, *extra_specs],
                  out_specs=out_specs, out_shape=out_shape, scratch=scratch)(a, b, *extra)


def _rms(x, g):
    inv = lax.rsqrt(jnp.mean(x * x, axis=-1, keepdims=True) + EPS)
    return x * inv * g


def _lnmod_math(x, g, sh, sc):
    return _rms(x, g) * (1.0 + sc) + sh


def _head_sums(x):
    row = lax.broadcasted_iota(jnp.int32, (LANES, LANES), 0) // HEAD_DIM
    col = lax.broadcasted_iota(jnp.int32, (LANES, LANES), 1) // HEAD_DIM
    same_head = jnp.where(row == col, 1.0, 0.0).astype(BF16)
    hi = x.astype(BF16)
    lo = (x - hi.astype(F32)).astype(BF16)
    return _dot(hi, same_head) + _dot(lo, same_head)


def _qkn_inv(p):
    return lax.rsqrt(_head_sums(p * p) / HEAD_DIM + EPS)


def _glu_math(val, gate):
    return jax.nn.gelu(gate) * val


def _accumulate(ref, val, first):
    @pl.when(first)
    def _():
        ref[...] = val

    @pl.when(jnp.logical_not(first))
    def _():
        ref[...] += val


def _lnmod_fwd(x, g, sh, sc):
    L, D = x.shape
    tm = _row_tile(L)

    def body(x_ref, g_ref, sh_ref, sc_ref, h_ref):
        h_ref[...] = _lnmod_math(x_ref[...], g_ref[...], sh_ref[...], sc_ref[...]).astype(BF16)

    row = pl.BlockSpec((tm, D), lambda i: (i, 0))
    vec = pl.BlockSpec((1, D), lambda i: (0, 0))
    return _pcall(body, name="lnmod_fwd", grid=(L // tm,), in_specs=[row, vec, vec, vec], out_specs=row,
                  out_shape=jax.ShapeDtypeStruct((L, D), BF16))(x, g, sh, sc)


def _lnmod_bwd(dh, x, g, sh, sc, dres):
    L, D = x.shape
    tm = _row_tile(L)

    def body(dh_ref, x_ref, g_ref, sh_ref, sc_ref, res_ref, dx_ref, dg_ref, dsh_ref, dsc_ref):
        _, vjp = jax.vjp(_lnmod_math, x_ref[...], g_ref[...], sh_ref[...], sc_ref[...])
        dx, dg, dsh, dsc = vjp(dh_ref[...])
        dx_ref[...] = dx + res_ref[...]
        first = pl.program_id(0) == 0
        _accumulate(dg_ref, dg, first)
        _accumulate(dsh_ref, dsh, first)
        _accumulate(dsc_ref, dsc, first)

    row = pl.BlockSpec((tm, D), lambda i: (i, 0))
    vec = pl.BlockSpec((1, D), lambda i: (0, 0))
    vs = jax.ShapeDtypeStruct((1, D), F32)
    return _pcall(body, name="lnmod_bwd", grid=(L // tm,), in_specs=[row, row, vec, vec, vec, row],
                  out_specs=[row, vec, vec, vec],
                  out_shape=[jax.ShapeDtypeStruct((L, D), F32), vs, vs, vs])(dh, x, g, sh, sc, dres)


def _gate_bwd(dx, a, gate):
    L, D = dx.shape
    tm = _row_tile(L)

    def body(dx_ref, a_ref, g_ref, da_ref, dg_ref):
        dxv = dx_ref[...]
        da_ref[...] = (g_ref[...] * dxv).astype(BF16)
        _accumulate(dg_ref, jnp.sum(dxv * a_ref[...], axis=0, keepdims=True), pl.program_id(0) == 0)

    row = pl.BlockSpec((tm, D), lambda i: (i, 0))
    vec = pl.BlockSpec((1, D), lambda i: (0, 0))
    return _pcall(body, name="gate_bwd", grid=(L // tm,), in_specs=[row, row, vec], out_specs=[row, vec],
                  out_shape=[jax.ShapeDtypeStruct((L, D), BF16), jax.ShapeDtypeStruct((1, D), F32)])(dx, a, gate)


def _qknorm_fwd(p, gqk, AW):
    L = p.shape[0]
    tm = _row_tile(L, 2048)
    ncol = 2 * AW // LANES

    def body(p_ref, g_ref, o_ref):
        p = p_ref[...]
        o_ref[...] = (p * _qkn_inv(p) * g_ref[...]).astype(BF16)

    blk = pl.BlockSpec((tm, LANES), lambda i, j: (i, j))
    vec = pl.BlockSpec((1, LANES), lambda i, j: (0, j))
    return _pcall(body, name="qknorm_fwd", grid=(L // tm, ncol), in_specs=[blk, vec], out_specs=blk,
                  out_shape=jax.ShapeDtypeStruct((L, 2 * AW), BF16))(p, gqk)


def _qknorm_bwd(dqk, p, gqk, AW):
    L = p.shape[0]
    tm = _row_tile(L, 2048)
    ncol = 2 * AW // LANES

    def body(d_ref, p_ref, g_ref, dp_ref, dg_ref):
        p, dy = p_ref[...], d_ref[...]
        inv = _qkn_inv(p)
        gdy = g_ref[...] * dy
        dp_ref[...] = (inv * gdy - p * (inv * inv * inv) * (_head_sums(p * gdy) / HEAD_DIM)).astype(BF16)
        _accumulate(dg_ref, jnp.sum(dy * p * inv, axis=0, keepdims=True), pl.program_id(1) == 0)

    blk = pl.BlockSpec((tm, LANES), lambda j, i: (i, j))
    vec = pl.BlockSpec((1, LANES), lambda j, i: (0, j))
    return _pcall(body, name="qknorm_bwd", grid=(ncol, L // tm), in_specs=[blk, blk, vec], out_specs=[blk, vec],
                  out_shape=[jax.ShapeDtypeStruct((L, 2 * AW), BF16),
                             jax.ShapeDtypeStruct((1, 2 * AW), F32)])(dqk, p, gqk)


def _outnorm_fwd(oa, os_, ga, gs):
    L, AW = oa.shape
    SW = os_.shape[1]
    tm = _row_tile(L)

    def body(oa_ref, os_ref, ga_ref, gs_ref, o_ref):
        o_ref[:, :AW] = _rms(oa_ref[...], ga_ref[...]).astype(BF16)
        o_ref[:, AW:] = _rms(os_ref[...], gs_ref[...]).astype(BF16)

    ra = pl.BlockSpec((tm, AW), lambda i: (i, 0))
    rs = pl.BlockSpec((tm, SW), lambda i: (i, 0))
    va = pl.BlockSpec((1, AW), lambda i: (0, 0))
    vs = pl.BlockSpec((1, SW), lambda i: (0, 0))
    ro = pl.BlockSpec((tm, AW + SW), lambda i: (i, 0))
    return _pcall(body, name="outnorm_fwd", grid=(L // tm,), in_specs=[ra, rs, va, vs], out_specs=ro,
                  out_shape=jax.ShapeDtypeStruct((L, AW + SW), BF16))(oa, os_, ga, gs)


def _outnorm_bwd(do, oa, os_, ga, gs):
    L, AW = oa.shape
    SW = os_.shape[1]
    tm = _row_tile(L)

    def body(do_ref, oa_ref, os_ref, ga_ref, gs_ref, doa_ref, dos_ref, dga_ref, dgs_ref):
        first = pl.program_id(0) == 0
        _, vjp_a = jax.vjp(_rms, oa_ref[...], ga_ref[...])
        doa, dga = vjp_a(do_ref[:, :AW])
        _, vjp_s = jax.vjp(_rms, os_ref[...], gs_ref[...])
        dos, dgs = vjp_s(do_ref[:, AW:])
        doa_ref[...] = doa
        dos_ref[...] = dos
        _accumulate(dga_ref, dga, first)
        _accumulate(dgs_ref, dgs, first)

    ra = pl.BlockSpec((tm, AW), lambda i: (i, 0))
    rs = pl.BlockSpec((tm, SW), lambda i: (i, 0))
    va = pl.BlockSpec((1, AW), lambda i: (0, 0))
    vs = pl.BlockSpec((1, SW), lambda i: (0, 0))
    ro = pl.BlockSpec((tm, AW + SW), lambda i: (i, 0))
    return _pcall(body, name="outnorm_bwd", grid=(L // tm,), in_specs=[ro, ra, rs, va, vs],
                  out_specs=[ra, rs, va, vs],
                  out_shape=[jax.ShapeDtypeStruct((L, AW), F32), jax.ShapeDtypeStruct((L, SW), F32),
                             jax.ShapeDtypeStruct((1, AW), F32), jax.ShapeDtypeStruct((1, SW), F32)])(
                                 do, oa, os_, ga, gs)


def _softplus_neg_abs(z):
    return jnp.log(1.0 + jnp.exp(-jnp.abs(z)))


def _attn_masks(B):
    row = lax.broadcasted_iota(jnp.int32, (B, B), 0)
    col = lax.broadcasted_iota(jnp.int32, (B, B), 1)
    strict = col < row
    upper = jnp.where(row > col, 1.0, 0.0).astype(BF16)
    lower = jnp.where(row < col, 1.0, 0.0).astype(BF16)
    return strict, upper, lower


def _ride_along(ride, scatter, HP, nb):
    n = len(ride)
    sems = [pltpu.SemaphoreType.DMA((NDEV - 1, n)), pltpu.SemaphoreType.DMA((NDEV - 1, n))] if n else []

    def hooks(srcs, dsts, send_sems, recv_sems):
        start, wait = _all_to_all(srcs, dsts, send_sems, recv_sems, scatter)
        hp, i = pl.program_id(0), pl.program_id(1)
        return (hp == 0) & (i == 0), start, (hp == HP - 1) & (i == nb - 1), wait

    return [HBM_SPEC] * n, _exchange_shapes(ride, scatter), sems, hooks


def _attn_fwd(qk, p, AW, B, ride=()):
    L = qk.shape[0]
    HP = AW // LANES
    nb = L // B
    n = len(ride)
    ride_specs, ride_shapes, ride_sems, ride_hooks = _ride_along(ride, False, HP, nb)

    def body(*refs):
        q_ref, k_ref, v_ref = refs[:3]
        o_ref, tot_ref = refs[3 + n:5 + n]
        scr = refs[5 + 2 * n:]
        if n:
            first, start, last, wait = ride_hooks(refs[3:3 + n], refs[5 + n:5 + 2 * n], scr[10], scr[11])
            pl.when(first)(start)
        lb_scr = (scr[0:2], scr[2:4])
        tail_scr = (scr[4:6], scr[6:8])
        sum_scr = scr[8:10]
        i = pl.program_id(1)
        m0 = lax.broadcasted_iota(jnp.int32, (1, LANES), 1) < HEAD_DIM
        strict, upper, _ = _attn_masks(B)
        q = q_ref[...] * 0.125
        zq = jnp.zeros_like(q)
        qh = (jnp.where(m0, q, zq), jnp.where(m0, zq, q))

        def keys(j):
            start = pl.multiple_of(jnp.maximum(j, 0) * B, B)
            return k_ref[pl.ds(start, B), :]

        def vals(j):
            start = pl.multiple_of(jnp.maximum(j, 0) * B, B)
            return v_ref[pl.ds(start, B), :].astype(BF16)

        strips = [slice(s, min(s + ATTN_STRIP, B)) for s in range(0, B, ATTN_STRIP)]

        def scores(j):
            kj = keys(j)
            return tuple(_dot(qh[h], kj, NT) for h in (0, 1))

        def logits(zs2, slot, diag):
            for h in (0, 1):
                z = zs2[h]
                his = []
                for rows in strips:
                    zs = z[rows]
                    lb = jnp.minimum(zs, 0.0) - _softplus_neg_abs(zs)
                    l1 = lb - zs
                    if diag:
                        l1 = jnp.where(strict[rows], l1, 0.0)
                    lb_scr[slot][h][rows, :] = lb
                    his.append(l1.astype(BF16))
                    rsum = jnp.sum(l1, axis=-1, keepdims=True)
                    if h == 0:
                        sum_scr[slot][rows, :] = jnp.broadcast_to(rsum, (rows.stop - rows.start, LANES))
                    else:
                        sum_scr[slot][rows, :] = jnp.where(m0, sum_scr[slot][rows, :], rsum)
                cat = lambda xs: jnp.concatenate(xs, axis=0)
                tail_scr[slot][h][...] = _dot(cat(his), upper)

        def attend(j, slot, diag):
            vj = vals(j)
            pv = []
            for h in (0, 1):
                ws = []
                for rows in strips:
                    w = jnp.exp(lb_scr[slot][h][rows, :] + tail_scr[slot][h][rows, :])
                    if diag:
                        w = jnp.where(strict[rows], w, 0.0)
                    ws.append(w.astype(BF16))
                pv.append(_dot(jnp.concatenate(ws, axis=0), vj))
            return jnp.where(m0, pv[0], pv[1])

        logits(scores(i), 0, True)
        o_ref[...] = attend(i, 0, True)
        tot_ref[...] = sum_scr[0][...]

        def half(j, slot):
            z = scores(j - 1)
            pv = attend(j, slot, False)
            logits(z, 1 - slot, False)
            o_ref[...] += pv * jnp.exp(tot_ref[...])
            tot_ref[...] += sum_scr[slot][...]

        @pl.when(i > 0)
        def _():
            logits(scores(i - 1), 1, False)

            @pl.loop(0, (i + 1) // 2)
            def _(t):
                j = i - 1 - 2 * t
                half(j, 1)

                @pl.when(j > 0)
                def _():
                    half(j - 1, 0)

        if n:
            pl.when(last)(wait)

    qspec = pl.BlockSpec((B, LANES), lambda hp, i: (i, hp))
    kspec = pl.BlockSpec((L, LANES), lambda hp, i: (0, HP + hp))
    vspec = pl.BlockSpec((L, LANES), lambda hp, i: (0, 2 * HP + hp))
    ospec = pl.BlockSpec((B, LANES), lambda hp, i: (i, hp))
    shp = jax.ShapeDtypeStruct((L, AW), F32)
    o, tot, *gathered = _pcall(
        body, name="attn_fwd", grid=(HP, nb), in_specs=[qspec, kspec, vspec] + ride_specs,
        out_specs=[ospec, ospec] + ride_specs, out_shape=[shp, shp] + ride_shapes,
        scratch=[pltpu.VMEM((B, B), F32)] * 8 + [pltpu.VMEM((B, LANES), F32)] * 2 + ride_sems)(qk, qk, p, *ride)
    return o, tot, _place_own(gathered, ride, False)


def _attn_bwd(qk, p, do, tot, AW, B, ride=()):
    L = qk.shape[0]
    HP = AW // LANES
    nb = L // B
    n = len(ride)
    ride_specs, ride_shapes, ride_sems, ride_hooks = _ride_along(ride, True, HP, nb)

    def body(*refs):
        q_ref, k_ref, v_ref, do_ref, tot_ref = refs[:5]
        dq_ref, dk_ref, dv_ref = refs[5 + n:8 + n]
        scr = refs[8 + 2 * n:]
        if n:
            first_step, start, last_step, wait = ride_hooks(refs[5:5 + n], refs[8 + n:8 + 2 * n], scr[15], scr[16])
            pl.when(first_step)(start)
        lb_scr, tail_scr, dw_scr, e_scr, beta_scr = scr[0:2], scr[2:4], scr[4:6], scr[6:8], scr[8:10]
        dos_scr, bsum_scr, left_scr, esum_scr, ecum_scr = scr[10:15]
        i = pl.program_id(1)
        m0 = lax.broadcasted_iota(jnp.int32, (1, LANES), 1) < HEAD_DIM
        strict, upper, lower = _attn_masks(B)
        strips = [slice(s, min(s + ATTN_STRIP, B)) for s in range(0, B, ATTN_STRIP)]
        cat = lambda xs: jnp.concatenate(xs, axis=0)

        @pl.when(i == 0)
        def _():
            dk_ref[...] = jnp.zeros_like(dk_ref)
            dv_ref[...] = jnp.zeros_like(dv_ref)

        q = q_ref[...] * 0.125
        zq = jnp.zeros_like(q)
        qh = (jnp.where(m0, q, zq), jnp.where(m0, zq, q))
        heads = lambda a: (jnp.where(m0, a, jnp.zeros_like(a)), jnp.where(m0, jnp.zeros_like(a), a))
        left_scr[...] = jnp.zeros_like(left_scr)
        ecum_scr[...] = jnp.zeros_like(ecum_scr)
        dq_ref[...] = jnp.zeros_like(dq_ref)

        def block_rows(j):
            return pl.ds(pl.multiple_of(j * B, B), B)

        def put_row_sums(ref, rows, h, rsum):
            if h == 0:
                ref[rows, :] = jnp.broadcast_to(rsum, (rows.stop - rows.start, LANES))
            else:
                ref[rows, :] = jnp.where(m0, ref[rows, :], rsum)

        def head_cols(x, h):
            other = pltpu.roll(x, HEAD_DIM, 1)
            full = jnp.where(m0, x, other) if h == 0 else jnp.where(m0, other, x)
            return jnp.concatenate([full] * (B // LANES), axis=1) if B > LANES else full

        def scores(j):
            kj = k_ref[block_rows(j), :]
            return tuple(_dot(qh[h], kj, NT) for h in (0, 1))

        def stage_a(j, zs2, diag):
            for h in (0, 1):
                his = []
                for rows in strips:
                    zs = zs2[h][rows]
                    lb = jnp.minimum(zs, 0.0) - _softplus_neg_abs(zs)
                    l1 = lb - zs
                    if diag:
                        l1 = jnp.where(strict[rows], l1, 0.0)
                    lb_scr[h][rows, :] = lb
                    his.append(l1.astype(BF16))
                    put_row_sums(bsum_scr, rows, h, jnp.sum(l1, axis=-1, keepdims=True))
                tail_scr[h][...] = _dot(cat(his), upper)
            bs = bsum_scr[...]
            scale = jnp.exp(tot_ref[...] - left_scr[...] - bs)
            left_scr[...] += bs
            dos = (do_ref[...] * scale).astype(BF16)
            dos_scr[...] = dos
            vj = v_ref[block_rows(j), :].astype(BF16)
            dosh = heads(dos)
            for h in (0, 1):
                dw_scr[h][...] = _dot(dosh[h], vj, NT)

        def stage_b(j, diag):
            dosh = heads(dos_scr[...])
            pres = []
            dv_blk = jnp.zeros((B, LANES), F32)
            for h in (0, 1):
                ehs, wbs = [], []
                for rows in strips:
                    lb = lb_scr[h][rows, :]
                    w = jnp.exp(lb + tail_scr[h][rows, :])
                    if diag:
                        w = jnp.where(strict[rows], w, 0.0)
                    beta_scr[h][rows, :] = jnp.exp(lb)
                    e = dw_scr[h][rows, :] * w
                    e_scr[h][rows, :] = e
                    ehs.append(e.astype(BF16))
                    wbs.append(w.astype(BF16))
                    put_row_sums(esum_scr, rows, h, jnp.sum(e, axis=-1, keepdims=True))
                pres.append(_dot(cat(ehs), lower))
                dv_blk = dv_blk + _dot(cat(wbs), dosh[h], TN)
            dv_ref[block_rows(j), :] += dv_blk
            return pres

        def stage_c(j, pres, diag):
            kh = heads(k_ref[block_rows(j), :])
            dq = jnp.zeros((B, LANES), F32)
            dk_blk = jnp.zeros((B, LANES), F32)
            for h in (0, 1):
                dzs = []
                for rows in strips:
                    e = e_scr[h][rows, :]
                    dl1 = pres[h][rows] + head_cols(ecum_scr[rows, :], h)
                    dz = e - beta_scr[h][rows, :] * (e + dl1)
                    if diag:
                        dz = jnp.where(strict[rows], dz, 0.0)
                    dzs.append(dz.astype(BF16))
                dzb = cat(dzs)
                dq = dq + _dot(dzb, kh[h])
                dk_blk = dk_blk + _dot(dzb, qh[h], TN)
            dq_ref[...] += dq
            dk_ref[block_rows(j), :] += dk_blk
            ecum_scr[...] += esum_scr[...]

        def step(t, diag_next):
            z = scores(t + 1)
            pres = stage_b(t, False)
            stage_a(t + 1, z, diag_next)
            stage_c(t, pres, False)

        @pl.when(i > 0)
        def _():
            stage_a(0, scores(0), False)

            @pl.loop(0, i - 1)
            def _(t):
                step(t, False)

            step(i - 1, True)

        @pl.when(i == 0)
        def _():
            stage_a(0, scores(0), True)

        stage_c(i, stage_b(i, True), True)
        dq_ref[...] = dq_ref[...] * 0.125
        if n:
            pl.when(last_step)(wait)

    qspec = pl.BlockSpec((B, LANES), lambda hp, i: (i, hp))
    kspec = pl.BlockSpec((L, LANES), lambda hp, i: (0, HP + hp))
    vspec = pl.BlockSpec((L, LANES), lambda hp, i: (0, 2 * HP + hp))
    full = pl.BlockSpec((L, LANES), lambda hp, i: (0, hp))
    shp = jax.ShapeDtypeStruct((L, AW), F32)
    dq, dk, dv, *received = _pcall(
        body, name="attn_bwd", grid=(HP, nb), in_specs=[qspec, kspec, vspec, qspec, qspec] + ride_specs,
        out_specs=[qspec, full, full] + ride_specs, out_shape=[shp, shp, shp] + ride_shapes,
        scratch=[pltpu.VMEM((B, B), F32)] * 10 + [pltpu.VMEM((B, LANES), BF16)]
        + [pltpu.VMEM((B, LANES), F32)] * 4 + ride_sems)(qk, qk, p, do, tot, *ride)
    return dq, dk, dv, _place_own(received, ride, True)


def _s5_disc(ar, ai, ldt):
    dt = jnp.exp(ldt)
    mag = jnp.exp(dt * ar)
    abr = mag * jnp.cos(dt * ai)
    abi = mag * jnp.sin(dt * ai)
    emr = abr - 1.0
    emi = abi
    den = ar * ar + ai * ai
    fr = (emr * ar + emi * ai) / den
    fi = (emi * ar - emr * ai) / den
    return abr, abi, fr, fi


def _s5_params_math(ar, ai, ldt, br, bi):
    abr, abi, fr, fi = _s5_disc(ar, ai, ldt)
    return abr, abi, fr * br - fi * bi, fr * bi + fi * br


def _cmul_add(xr, xi, kr, ki, sr, si):
    return xr + (kr * sr - ki * si), xi + (kr * si + ki * sr)


def _s5_tab_rows(T):
    ng = T // 8
    return dict(NG=ng, POW=0, PA=ng, PAR=ng + 8, LA=ng + 16, ROWS=ng + 24)


def _s5_prep(ar, ai, ldt, braw_r, braw_i, T):
    NL, _, NS = ar.shape
    GT = NS // STATE_TILE
    R = _s5_tab_rows(T)

    def body(ar_ref, ai_ref, ldt_ref, br_ref, bi_ref, abr_ref, abi_ref, tabr_ref, tabi_ref, bsr_ref, bsi_ref):
        for t in range(GT):
            sl = slice(t * STATE_TILE, (t + 1) * STATE_TILE)
            abr, abi, bsr, bsi = _s5_params_math(ar_ref[:, sl], ai_ref[:, sl], ldt_ref[:, sl],
                                                 br_ref[t], bi_ref[t])
            abr_ref[:, sl] = abr
            abi_ref[:, sl] = abi
            bsr_ref[t] = bsr.astype(BF16)
            bsi_ref[t] = bsi.astype(BF16)

            def put(row, vr, vi):
                tabr_ref[row:row + 1, sl] = vr
                tabi_ref[row:row + 1, sl] = vi

            pr, pi = abr, abi
            for g in range(R["NG"]):
                put(R["POW"] + g, pr, pi)
                if g + 1 < R["NG"]:
                    pr, pi = pr * abr - pi * abi, pr * abi + pi * abr
            big_r, big_i = pr, pi
            qr, qi = big_r, big_i
            for r in range(8):
                put(R["PA"] + r, qr, qi)
                put(R["PAR"] + 7 - r, qr, qi)
                qr, qi = qr * big_r - qi * big_i, qr * big_i + qi * big_r
            qr, qi = big_r, big_i
            for k in range(3):
                put(R["LA"] + k, qr, qi)
                qr, qi = qr * qr - qi * qi, 2.0 * qr * qi
            for k in range(3, 8):
                put(R["LA"] + k, jnp.zeros_like(qr), jnp.zeros_like(qi))

    rowspec = pl.BlockSpec((None, 1, NS), lambda l: (l, 0, 0))
    bspec = pl.BlockSpec((None, GT, LANES, STATE_TILE), lambda l: (l, 0, 0, 0))
    tabspec = pl.BlockSpec((None, R["ROWS"], NS), lambda l: (l, 0, 0))
    rs = jax.ShapeDtypeStruct((NL, 1, NS), F32)
    ts = jax.ShapeDtypeStruct((NL, R["ROWS"], NS), F32)
    bs = jax.ShapeDtypeStruct((NL, GT, LANES, STATE_TILE), BF16)
    return _pcall(body, name="s5_prep", grid=(NL,), in_specs=[rowspec] * 3 + [bspec] * 2,
                  out_specs=[rowspec, rowspec, tabspec, tabspec, bspec, bspec],
                  out_shape=[rs, rs, ts, ts, bs, bs])(ar, ai, ldt, braw_r, braw_i)


def _s5_prep_bwd(ar, ai, ldt, braw_r, braw_i, dabr, dabi, dbsr, dbsi):
    NL, _, NS = ar.shape
    GT = NS // STATE_TILE

    def body(ar_ref, ai_ref, ldt_ref, br_ref, bi_ref, dabr_ref, dabi_ref, dbsr_ref, dbsi_ref,
             dar_ref, dai_ref, dldt_ref, dbr_ref, dbi_ref):
        for t in range(GT):
            sl = slice(t * STATE_TILE, (t + 1) * STATE_TILE)
            _, vjp = jax.vjp(_s5_params_math, ar_ref[:, sl], ai_ref[:, sl], ldt_ref[:, sl],
                             br_ref[t], bi_ref[t])
            dabr_row = jnp.sum(dabr_ref[:, sl], axis=0, keepdims=True)
            dabi_row = jnp.sum(dabi_ref[:, sl], axis=0, keepdims=True)
            dar, dai, dldt, dbr, dbi = vjp((dabr_row, dabi_row, dbsr_ref[t], dbsi_ref[t]))
            dar_ref[:, sl] = dar
            dai_ref[:, sl] = dai
            dldt_ref[:, sl] = dldt
            dbr_ref[t] = dbr
            dbi_ref[t] = dbi

    rowspec = pl.BlockSpec((None, 1, NS), lambda l: (l, 0, 0))
    row8spec = pl.BlockSpec((None, 8, NS), lambda l: (l, 0, 0))
    bspec = pl.BlockSpec((None, GT, LANES, STATE_TILE), lambda l: (l, 0, 0, 0))
    rs = jax.ShapeDtypeStruct((NL, 1, NS), F32)
    bs = jax.ShapeDtypeStruct((NL, GT, LANES, STATE_TILE), F32)
    return _pcall(body, name="s5_prep_bwd", grid=(NL,), in_specs=[rowspec] * 3 + [bspec] * 2 + [row8spec] * 2 + [bspec] * 2,
                  out_specs=[rowspec] * 3 + [bspec] * 2, out_shape=[rs, rs, rs, bs, bs])(
                      ar, ai, ldt, braw_r, braw_i, dabr, dabi, dbsr, dbsi)


def _to_streams(x, T):
    L, C = x.shape
    return x.reshape(L // T, 8, T // 8, C).transpose(0, 2, 1, 3).reshape(L, C)


def _from_streams(x, T):
    L, C = x.shape
    return x.reshape(L // T, T // 8, 8, C).transpose(0, 2, 1, 3).reshape(L, C)


def _s5_specs(L, SW, T, rev):
    GT = SW // LANES
    nc = L // T
    cidx = (lambda c: nc - 1 - c) if rev else (lambda c: c)
    rows = _s5_tab_rows(T)["ROWS"]
    return dict(
        GT=GT, nc=nc, cidx=cidx,
        chan=pl.BlockSpec((T, LANES), lambda j, c: (cidx(c), j)),
        state=pl.BlockSpec((T, STATE_TILE), lambda j, c: (cidx(c), j)),
        bmat=pl.BlockSpec((None, LANES, STATE_TILE), lambda j, c: (j, 0, 0)),
        cmat=pl.BlockSpec((None, STATE_TILE, LANES), lambda j, c: (j, 0, 0)),
        gmat=pl.BlockSpec((None, LANES, LANES), lambda j, c: (j, 0, 0)),
        cvec=pl.BlockSpec((1, LANES), lambda j, c: (0, j)),
        svec8=pl.BlockSpec((8, STATE_TILE), lambda j, c: (0, j)),
        tab=pl.BlockSpec((rows, STATE_TILE), lambda j, c: (0, j)),
    )


def _s5_fwd(u, sp, T):
    L, SW = u.shape
    S = _s5_specs(L, SW, T, False)
    NS = S["GT"] * STATE_TILE
    R = _s5_tab_rows(T)
    NG = R["NG"]

    def body(u_ref, bsr_ref, bsi_ref, tabr_ref, tabi_ref, cr_ref, ci_ref, d_ref, wg_ref, gb_ref,
             o_ref, sr_ref, si_ref, y_ref, carr_ref, cari_ref):
        @pl.when(pl.program_id(1) == 0)
        def _():
            carr_ref[...] = jnp.zeros_like(carr_ref)
            cari_ref[...] = jnp.zeros_like(cari_ref)

        uv = u_ref[...]
        ub = uv.astype(BF16)
        row8 = lax.broadcasted_iota(jnp.int32, (8, LANES), 0)
        nstrips = STATE_TILE // LANES

        def project(s):
            sl = slice(s * LANES, (s + 1) * LANES)
            return _dot(ub, bsr_ref[:, sl]), _dot(ub, bsi_ref[:, sl])

        ahead = project(0)
        for s in range(nstrips):
            sl = slice(s * LANES, (s + 1) * LANES)
            tab = lambda r0, n=1, sl=sl: (tabr_ref[r0:r0 + n, sl], tabi_ref[r0:r0 + n, sl])
            xr, xi = ahead
            if s + 1 < nstrips:
                ahead = project(s + 1)
            ar, ai = tab(R["POW"])
            lr, li = [xr[0:8]], [xi[0:8]]
            for g in range(1, NG):
                nr, ni = _cmul_add(xr[8 * g:8 * g + 8], xi[8 * g:8 * g + 8], ar, ai, lr[-1], li[-1])
                lr.append(nr)
                li.append(ni)
            yr, yi = lr[-1], li[-1]
            for k, dist in enumerate((1, 2, 4)):
                kr, ki = tab(R["LA"] + k)
                keep = row8 >= dist
                yr, yi = _cmul_add(yr, yi, kr, ki, jnp.where(keep, pltpu.roll(yr, dist, 0), 0.0),
                                   jnp.where(keep, pltpu.roll(yi, dist, 0), 0.0))
            c0r, c0i = carr_ref[7:8, sl], cari_ref[7:8, sl]
            par, pai = tab(R["PA"], 8)
            er, ei = _cmul_add(yr, yi, par, pai, c0r, c0i)
            carr_ref[:, sl] = er
            cari_ref[:, sl] = ei
            first = row8 == 0
            inr = jnp.where(first, c0r, pltpu.roll(er, 1, 0))
            ini = jnp.where(first, c0i, pltpu.roll(ei, 1, 0))
            for g in range(NG):
                pr, pi = tab(R["POW"] + g)
                outr, outi = _cmul_add(lr[g], li[g], pr, pi, inr, ini)
                sr_ref[8 * g:8 * g + 8, sl] = outr
                si_ref[8 * g:8 * g + 8, sl] = outi
        y = (_dot(sr_ref[...].astype(BF16), cr_ref[...]) - _dot(si_ref[...].astype(BF16), ci_ref[...])
             + d_ref[...] * uv)
        y_ref[...] = y
        yg = jax.nn.gelu(y)
        gate = jax.nn.sigmoid(_dot(yg.astype(BF16), wg_ref[...]) + gb_ref[...])
        o_ref[...] = yg * gate

    cs = jax.ShapeDtypeStruct((L, SW), F32)
    ss = jax.ShapeDtypeStruct((L, NS), F32)
    return _pcall(
        body, name="s5_fwd", grid=(S["GT"], S["nc"]),
        in_specs=[S["chan"], S["bmat"], S["bmat"], S["tab"], S["tab"], S["cmat"], S["cmat"],
                  S["cvec"], S["gmat"], S["cvec"]],
        out_specs=[S["chan"], S["state"], S["state"], S["chan"]], out_shape=[cs, ss, ss, cs],
        scratch=[pltpu.VMEM((8, STATE_TILE), F32), pltpu.VMEM((8, STATE_TILE), F32)],
    )(u, sp["bsr"], sp["bsi"], sp["tabr"], sp["tabi"], sp["crT"], sp["ciT"], sp["d"], sp["wg"], sp["gb"])


def _s5_bwd(do, u, s_re, s_im, ypre, sp, T):
    L, SW = u.shape
    S = _s5_specs(L, SW, T, True)
    GT, nc, cidx = S["GT"], S["nc"], S["cidx"]
    NS = GT * STATE_TILE
    R = _s5_tab_rows(T)
    NG = R["NG"]
    T8 = T // 8

    def body(do_ref, u_ref, sr_ref, si_ref, hr_ref, hi_ref, y_ref, bsr_ref, bsi_ref, tabr_ref, tabi_ref,
             cr_ref, ci_ref, d_ref, wg_ref, gb_ref,
             du_ref, dbsr_ref, dbsi_ref, dcr_ref, dci_ref, dd_ref, dwg_ref, dgb_ref, dar_ref, dai_ref,
             carr_ref, cari_ref, lam_r, lam_i):
        c = pl.program_id(1)
        first = c == 0

        @pl.when(first)
        def _():
            carr_ref[...] = jnp.zeros_like(carr_ref)
            cari_ref[...] = jnp.zeros_like(cari_ref)

        u = u_ref[...]
        ub = u.astype(BF16)
        y = y_ref[...]
        yg, gelu_vjp = jax.vjp(jax.nn.gelu, y)
        ygb = yg.astype(BF16)
        gate = jax.nn.sigmoid(_dot(ygb, wg_ref[...]) + gb_ref[...])
        dout = do_ref[...]
        dt = dout * yg * gate * (1.0 - gate)
        dtb = dt.astype(BF16)
        dyg = dout * gate + _dot(dtb, wg_ref[...], NT)
        (dy,) = gelu_vjp(dyg)
        dyb = dy.astype(BF16)
        _accumulate(dwg_ref, _dot(ygb, dtb, TN), first)
        _accumulate(dgb_ref, jnp.sum(dt, axis=0, keepdims=True), first)
        _accumulate(dd_ref, jnp.sum(dy * u, axis=0, keepdims=True), first)
        _accumulate(dcr_ref, _dot(sr_ref[...].astype(BF16), dyb, TN), first)
        _accumulate(dci_ref, -_dot(si_ref[...].astype(BF16), dyb, TN), first)
        earliest = cidx(c) == 0
        row8 = lax.broadcasted_iota(jnp.int32, (8, LANES), 0)
        nstrips = STATE_TILE // LANES

        def project(s):
            sl = slice(s * LANES, (s + 1) * LANES)
            return _dot(dyb, cr_ref[sl, :], NT), -_dot(dyb, ci_ref[sl, :], NT)

        ahead = project(0)
        for s in range(nstrips):
            sl = slice(s * LANES, (s + 1) * LANES)
            tab = lambda r0, n=1, sl=sl: (tabr_ref[r0:r0 + n, sl], -tabi_ref[r0:r0 + n, sl])
            xr, xi = ahead
            if s + 1 < nstrips:
                ahead = project(s + 1)
            ar, ai = tab(R["POW"])
            lr, li = [None] * NG, [None] * NG
            lr[NG - 1], li[NG - 1] = xr[8 * (NG - 1):8 * NG], xi[8 * (NG - 1):8 * NG]
            for g in range(NG - 2, -1, -1):
                lr[g], li[g] = _cmul_add(xr[8 * g:8 * g + 8], xi[8 * g:8 * g + 8], ar, ai, lr[g + 1], li[g + 1])
            yr, yi = lr[0], li[0]
            for k, dist in enumerate((1, 2, 4)):
                kr, ki = tab(R["LA"] + k)
                keep = row8 < 8 - dist
                yr, yi = _cmul_add(yr, yi, kr, ki, jnp.where(keep, pltpu.roll(yr, 8 - dist, 0), 0.0),
                                   jnp.where(keep, pltpu.roll(yi, 8 - dist, 0), 0.0))
            c0r, c0i = carr_ref[0:1, sl], cari_ref[0:1, sl]
            par, pai = tab(R["PAR"], 8)
            er, ei = _cmul_add(yr, yi, par, pai, c0r, c0i)
            carr_ref[:, sl] = er
            cari_ref[:, sl] = ei
            last = row8 == 7
            inr = jnp.where(last, c0r, pltpu.roll(er, 7, 0))
            ini = jnp.where(last, c0i, pltpu.roll(ei, 7, 0))
            hr0 = jnp.where(earliest, 0.0, hr_ref[7:8, sl])
            hi0 = jnp.where(earliest, 0.0, hi_ref[7:8, sl])
            endr, endi = sr_ref[8 * (NG - 1):8 * NG, sl], si_ref[8 * (NG - 1):8 * NG, sl]
            pvr = jnp.where(row8 == 0, hr0, pltpu.roll(endr, 1, 0))
            pvi = jnp.where(row8 == 0, hi0, pltpu.roll(endi, 1, 0))
            accr = jnp.zeros((8, LANES), F32)
            acci = jnp.zeros((8, LANES), F32)
            for g in range(NG):
                pr, pi = tab(R["POW"] + NG - 1 - g)
                outr, outi = _cmul_add(lr[g], li[g], pr, pi, inr, ini)
                lam_r[8 * g:8 * g + 8, sl] = outr
                lam_i[8 * g:8 * g + 8, sl] = outi
                accr = accr + (outr * pvr + outi * pvi)
                acci = acci + (outi * pvr - outr * pvi)
                pvr, pvi = sr_ref[8 * g:8 * g + 8, sl], si_ref[8 * g:8 * g + 8, sl]

            @pl.when(first)
            def _():
                dar_ref[:, sl] = accr
                dai_ref[:, sl] = acci

            @pl.when(jnp.logical_not(first))
            def _():
                dar_ref[:, sl] += accr
                dai_ref[:, sl] += acci

        lrb = lam_r[...].astype(BF16)
        lib = lam_i[...].astype(BF16)
        _accumulate(dbsr_ref, _dot(ub, lrb, TN), first)
        _accumulate(dbsi_ref, _dot(ub, lib, TN), first)
        du_ref[...] = (dy * d_ref[...] + _dot(lrb, bsr_ref[...], NT) + _dot(lib, bsi_ref[...], NT)).astype(BF16)

    halo = pl.BlockSpec((8, STATE_TILE), lambda j, c: (jnp.maximum(cidx(c) * T8 - 1, 0), j))
    f = lambda *s: jax.ShapeDtypeStruct(s, F32)
    return _pcall(
        body, name="s5_bwd", grid=(GT, nc),
        in_specs=[S["chan"], S["chan"], S["state"], S["state"], halo, halo, S["chan"], S["bmat"], S["bmat"],
                  S["tab"], S["tab"], S["cmat"], S["cmat"], S["cvec"], S["gmat"], S["cvec"]],
        out_specs=[S["chan"], S["bmat"], S["bmat"], S["cmat"], S["cmat"], S["cvec"], S["gmat"], S["cvec"],
                   S["svec8"], S["svec8"]],
        out_shape=[jax.ShapeDtypeStruct((L, SW), BF16), f(GT, LANES, STATE_TILE), f(GT, LANES, STATE_TILE), f(GT, STATE_TILE, LANES),
                   f(GT, STATE_TILE, LANES), f(1, SW), f(GT, LANES, LANES), f(1, SW), f(8, NS), f(8, NS)],
        scratch=[pltpu.VMEM((8, STATE_TILE), F32), pltpu.VMEM((8, STATE_TILE), F32),
                 pltpu.VMEM((T, STATE_TILE), F32), pltpu.VMEM((T, STATE_TILE), F32)],
    )(do, u, s_re, s_im, s_re, s_im, ypre, sp["bsr"], sp["bsi"], sp["tabr"], sp["tabi"],
      sp["crT"], sp["ciT"], sp["d"], sp["wg"], sp["gb"])


def _shifted_rows(x, k, edge):
    n = x.shape[0]
    rolled = pltpu.roll(x, k % n, 0)
    row8 = lax.broadcasted_iota(jnp.int32, (8, x.shape[1]), 0)
    if k > 0:
        fixed = rolled[0:8]
        for r, e in enumerate(edge):
            fixed = jnp.where(row8 == r, e, fixed)
        return jnp.concatenate([fixed, rolled[8:]], axis=0) if n > 8 else fixed
    fixed = rolled[n - 8:n]
    for r, e in enumerate(edge):
        fixed = jnp.where(row8 == 8 + k + r, e, fixed)
    return jnp.concatenate([rolled[:n - 8], fixed], axis=0) if n > 8 else fixed


def _conv_taps(xc, h6, h7):
    return _shifted_rows(xc, 1, [h7]), _shifted_rows(xc, 2, [h6, h7])


def _conv_halves(up_ref, halo_ref, w_ref, b_ref, tm, FS, first):
    outs, taps = [], []
    for s in (0, 1):
        xc = up_ref[s]
        h6 = jnp.where(first, 0.0, halo_ref[s, 6:7, :])
        h7 = jnp.where(first, 0.0, halo_ref[s, 7:8, :])
        x1, x2 = _conv_taps(xc, h6, h7)
        outs.append(b_ref[s] + x2 * w_ref[s, 0:1, :] + x1 * w_ref[s, 1:2, :] + xc * w_ref[s, 2:3, :])
        taps.append((x2, x1, xc))
    return outs, taps


def _convglu_specs(L, FS, tm, rev=False):
    t8 = tm // 8
    nt = L // tm
    tile = (lambda i: nt - 1 - i) if rev else (lambda i: i)
    return dict(
        up=pl.BlockSpec((2, None, tm, FS), lambda j, i: (0, j, tile(i), 0)),
        halo=pl.BlockSpec((2, None, 8, FS), lambda j, i: (0, j, jnp.maximum(tile(i) * t8 - 1, 0), 0)),
        w=pl.BlockSpec((2, None, 3, FS), lambda j, i: (0, j, 0, 0)),
        b=pl.BlockSpec((2, None, 1, FS), lambda j, i: (0, j, 0, 0)),
        act=pl.BlockSpec((None, tm, FS), lambda j, i: (j, tile(i), 0)),
    )


def _convglu_fwd(up, w, b):
    _, NSH, L, FS = up.shape
    tm = _row_tile(L)
    S = _convglu_specs(L, FS, tm)

    def body(up_ref, halo_ref, w_ref, b_ref, act_ref):
        (val, gate), _ = _conv_halves(up_ref, halo_ref, w_ref, b_ref, tm, FS, pl.program_id(1) == 0)
        act_ref[...] = _glu_math(val, gate).astype(BF16)

    return _pcall(body, name="convglu_fwd", grid=(NSH, L // tm), in_specs=[S["up"], S["halo"], S["w"], S["b"]],
                  out_specs=S["act"], out_shape=jax.ShapeDtypeStruct((NSH, L, FS), BF16))(up, up, w, b)


def _convglu_bwd(up, dact, w, b):
    _, NSH, L, FS = up.shape
    tm = _row_tile(L)
    nt = L // tm
    S = _convglu_specs(L, FS, tm, rev=True)

    def body(up_ref, halo_ref, w_ref, b_ref, dact_ref, dup_ref, dw_ref, db_ref, after_scr):
        step = pl.program_id(1)
        first = step == 0
        (val, gate), taps = _conv_halves(up_ref, halo_ref, w_ref, b_ref, tm, FS, step == nt - 1)
        _, vjp = jax.vjp(_glu_math, val, gate)
        dcs = vjp(dact_ref[...])
        for s in (0, 1):
            dc = dcs[s]
            n0 = jnp.where(first, 0.0, after_scr[s, 0:1, :])
            n1 = jnp.where(first, 0.0, after_scr[s, 1:2, :])
            x1 = _shifted_rows(dc, -1, [n0])
            x2 = _shifted_rows(dc, -2, [n0, n1])
            dup_ref[s] = (dc * w_ref[s, 2:3, :] + x1 * w_ref[s, 1:2, :] + x2 * w_ref[s, 0:1, :]).astype(BF16)
            after_scr[s] = dc[0:8]
            sums = [jnp.sum(dc * t, axis=0, keepdims=True) for t in taps[s]]
            dbs = jnp.sum(dc, axis=0, keepdims=True)

            @pl.when(first)
            def _():
                for t in range(3):
                    dw_ref[s, t:t + 1, :] = sums[t]
                db_ref[s] = dbs

            @pl.when(jnp.logical_not(first))
            def _():
                for t in range(3):
                    dw_ref[s, t:t + 1, :] += sums[t]
                db_ref[s] += dbs

    f = lambda *s: jax.ShapeDtypeStruct(s, F32)
    return _pcall(body, name="convglu_bwd", grid=(NSH, nt),
                  in_specs=[S["up"], S["halo"], S["w"], S["b"], S["act"]],
                  out_specs=[S["up"], S["w"], S["b"]],
                  out_shape=[jax.ShapeDtypeStruct((2, NSH, L, FS), BF16), f(2, NSH, 3, FS), f(2, NSH, 1, FS)],
                  scratch=[pltpu.VMEM((2, 8, FS), F32)])(up, up, w, b, dact)


def _loss_head(y, target):
    L, D = y.shape
    tm = _row_tile(L)

    def body(y_ref, t_ref, loss_ref, dy_ref):
        err = y_ref[...] - t_ref[...]
        dy_ref[...] = err / D
        part = 0.5 * jnp.sum(jnp.mean(err * err, axis=-1, keepdims=True), axis=0, keepdims=True)
        _accumulate(loss_ref, jnp.broadcast_to(part, (1, LANES)), pl.program_id(0) == 0)

    row = pl.BlockSpec((tm, D), lambda i: (i, 0))
    vec = pl.BlockSpec((1, LANES), lambda i: (0, 0))
    return _pcall(body, name="loss_head", grid=(L // tm,), in_specs=[row, row], out_specs=[vec, row],
                  out_shape=[jax.ShapeDtypeStruct((1, LANES), F32), jax.ShapeDtypeStruct((L, D), F32)])(y, target)


def _ada_fwd(c_all, ada_w, ada_b):
    NL, D, NC = ada_w.shape
    NB = c_all.shape[0]

    def body(c_ref, w_ref, b_ref, o_ref):
        cact = jax.nn.silu(c_ref[...])
        o_ref[...] = _dot(cact.astype(BF16), w_ref[...].astype(BF16)) + b_ref[...]

    return _pcall(body, name="ada_fwd", grid=(NL,),
                  in_specs=[pl.BlockSpec((NB, D), lambda l: (0, 0)), pl.BlockSpec((None, D, NC), lambda l: (l, 0, 0)),
                            pl.BlockSpec((None, 1, NC), lambda l: (l, 0, 0))],
                  out_specs=pl.BlockSpec((None, NB, NC), lambda l: (l, 0, 0)),
                  out_shape=jax.ShapeDtypeStruct((NL, NB, NC), F32))(c_all, ada_w, ada_b)


def _ada_bwd(c_all_t, dmod):
    D, NB = c_all_t.shape
    NL, _, NC = dmod.shape

    def body(c_ref, d_ref, o_ref):
        cact = jax.nn.silu(c_ref[...]).astype(BF16).astype(F32)
        o_ref[...] = _dot(cact, d_ref[...].astype(BF16).astype(F32))

    return _pcall(body, name="ada_bwd", grid=(NL,),
                  in_specs=[pl.BlockSpec((D, NB), lambda l: (0, 0)), pl.BlockSpec((None, NB, NC), lambda l: (l, 0, 0))],
                  out_specs=pl.BlockSpec((None, D, NC), lambda l: (l, 0, 0)),
                  out_shape=jax.ShapeDtypeStruct((NL, D, NC), F32))(c_all_t, dmod)


def _adamw(parts, w, m, v):
    NL, P, R, C = parts.shape
    tr = R if R <= 512 else max(t for t in range(16, 513, 16) if R % t == 0)

    def body(p_ref, w_ref, m_ref, v_ref, g_ref, d_ref, nm_ref, nv_ref):
        g = p_ref[0].astype(F32)
        for k in range(1, P):
            g = g + p_ref[k].astype(F32)
        m2 = ADAM_B1 * m_ref[...] + (1.0 - ADAM_B1) * g
        v2 = ADAM_B2 * v_ref[...] + (1.0 - ADAM_B2) * jnp.square(g)
        m_hat = m2 / (1.0 - ADAM_B1 ** ADAM_STEP)
        v_hat = v2 / (1.0 - ADAM_B2 ** ADAM_STEP)
        g_ref[...] = g
        d_ref[...] = -ADAM_LR * (m_hat / (jnp.sqrt(v_hat) + ADAM_EPS) + ADAM_WD * w_ref[...])
        nm_ref[...] = m2
        nv_ref[...] = v2

    pspec = pl.BlockSpec((None, P, tr, C), lambda l, i: (l, 0, i, 0))
    wspec = pl.BlockSpec((None, tr, C), lambda l, i: (l, i, 0))
    shp = jax.ShapeDtypeStruct((NL, R, C), F32)
    return _pcall(body, name="adamw", grid=(NL, R // tr), in_specs=[pspec, wspec, wspec, wspec],
                  out_specs=[wspec] * 4, out_shape=[shp] * 4)(parts, w, m, v)


def _block_diag(blocks):
    *lead, g, r, c = blocks.shape
    eye = jnp.eye(g, dtype=bool)[:, None, :, None]
    full = jnp.where(eye, blocks[..., :, :, None, :], 0.0)
    return full.reshape(*lead, g * r, g * c)


def _block_diag_extract(m, r, c):
    g = GROUPS_PER_TILE
    m5 = m.reshape(*m.shape[:-2], g, r, g, c)
    eye = jnp.eye(g, dtype=bool)[:, None, :, None]
    return jnp.sum(jnp.where(eye, m5, 0.0), axis=-2)


def kernel(x, c, ada_w, ada_b, norm1_g, w_in, q_norm_g, k_norm_g, ssm_a_re, ssm_a_im, ssm_log_dt, ssm_b_re, ssm_b_im, ssm_c_re, ssm_c_im, ssm_d, glu_w, glu_b, attn_out_g, ssm_out_g, w_out, norm2_g, ffn_w_up, ffn_conv_w, ffn_conv_b, ffn_w_down, loss_target, m_ada_w, m_ada_b, m_norm1_g, m_w_in, m_q_norm_g, m_k_norm_g, m_ssm_a_re, m_ssm_a_im, m_ssm_log_dt, m_ssm_b_re, m_ssm_b_im, m_ssm_c_re, m_ssm_c_im, m_ssm_d, m_glu_w, m_glu_b, m_attn_out_g, m_ssm_out_g, m_w_out, m_norm2_g, m_ffn_w_up, m_ffn_conv_w, m_ffn_conv_b, m_ffn_w_down, v_ada_w, v_ada_b, v_norm1_g, v_w_in, v_q_norm_g, v_k_norm_g, v_ssm_a_re, v_ssm_a_im, v_ssm_log_dt, v_ssm_b_re, v_ssm_b_im, v_ssm_c_re, v_ssm_c_im, v_ssm_d, v_glu_w, v_glu_b, v_attn_out_g, v_ssm_out_g, v_w_out, v_norm2_g, v_ffn_w_up, v_ffn_conv_w, v_ffn_conv_b, v_ffn_w_down):
    weights = dict(ada_w=ada_w, ada_b=ada_b, norm1_g=norm1_g, w_in=w_in, q_norm_g=q_norm_g, k_norm_g=k_norm_g,
                   ssm_a_re=ssm_a_re, ssm_a_im=ssm_a_im, ssm_log_dt=ssm_log_dt, ssm_b_re=ssm_b_re,
                   ssm_b_im=ssm_b_im, ssm_c_re=ssm_c_re, ssm_c_im=ssm_c_im, ssm_d=ssm_d, glu_w=glu_w, glu_b=glu_b,
                   attn_out_g=attn_out_g, ssm_out_g=ssm_out_g, w_out=w_out, norm2_g=norm2_g, ffn_w_up=ffn_w_up,
                   ffn_conv_w=ffn_conv_w, ffn_conv_b=ffn_conv_b, ffn_w_down=ffn_w_down)
    mom_m = dict(ada_w=m_ada_w, ada_b=m_ada_b, norm1_g=m_norm1_g, w_in=m_w_in, q_norm_g=m_q_norm_g,
                 k_norm_g=m_k_norm_g, ssm_a_re=m_ssm_a_re, ssm_a_im=m_ssm_a_im, ssm_log_dt=m_ssm_log_dt,
                 ssm_b_re=m_ssm_b_re, ssm_b_im=m_ssm_b_im, ssm_c_re=m_ssm_c_re, ssm_c_im=m_ssm_c_im, ssm_d=m_ssm_d,
                 glu_w=m_glu_w, glu_b=m_glu_b, attn_out_g=m_attn_out_g, ssm_out_g=m_ssm_out_g, w_out=m_w_out,
                 norm2_g=m_norm2_g, ffn_w_up=m_ffn_w_up, ffn_conv_w=m_ffn_conv_w, ffn_conv_b=m_ffn_conv_b,
                 ffn_w_down=m_ffn_w_down)
    mom_v = dict(ada_w=v_ada_w, ada_b=v_ada_b, norm1_g=v_norm1_g, w_in=v_w_in, q_norm_g=v_q_norm_g,
                 k_norm_g=v_k_norm_g, ssm_a_re=v_ssm_a_re, ssm_a_im=v_ssm_a_im, ssm_log_dt=v_ssm_log_dt,
                 ssm_b_re=v_ssm_b_re, ssm_b_im=v_ssm_b_im, ssm_c_re=v_ssm_c_re, ssm_c_im=v_ssm_c_im, ssm_d=v_ssm_d,
                 glu_w=v_glu_w, glu_b=v_glu_b, attn_out_g=v_attn_out_g, ssm_out_g=v_ssm_out_g, w_out=v_w_out,
                 norm2_g=v_norm2_g, ffn_w_up=v_ffn_w_up, ffn_conv_w=v_ffn_conv_w, ffn_conv_b=v_ffn_conv_b,
                 ffn_w_down=v_ffn_w_down)
    names = list(weights)
    big = ("ada_w", "w_in", "w_out", "ffn_w_up", "ffn_conv_w", "ffn_w_down")
    small = [n for n in names if n not in big]

    x = x[0]
    target = loss_target[0]
    L, D = x.shape
    NL = ada_w.shape[0]
    AW = D // 2
    SW = D - AW
    NH = AW // HEAD_DIM
    HP = AW // LANES
    G = SW // SSM_GROUP
    GT = SW // LANES
    NS = G * SSM_STATE
    NIN = w_in.shape[-1]
    FS = ffn_w_up.shape[-1]
    NSH = NDEV // 2
    NCA = ada_w.shape[-1]
    ROWS_OUT = w_out.shape[1]
    ROWS_DOWN = ffn_w_down.shape[1]
    B_ATT = min(L, 256)
    T_S5 = min(L, 1024)
    tm = _row_tile(L, 1024)
    ts = _row_tile(L, 512)
    ts2 = _row_tile(L, 256)
    me = _my_index()

    cpad = jnp.reshape(c, (D // LANES, LANES))
    c_all = _gather_small(cpad, "gather_c")
    c_all = c_all.reshape(NDEV, D)
    shards = [w_in.astype(BF16), w_out.astype(BF16), ffn_w_up.astype(BF16), ffn_conv_w, ffn_w_down.astype(BF16)]
    (first_w_in,) = _gather_shards([shards[0][0:1]], "gather_weights")

    def set_w_in(lp, w_in_g):
        lp.update(w_in_cols=jnp.swapaxes(w_in_g, 0, 1).reshape(D, NDEV * NIN))

    def set_other_weights(lp, gathered):
        w_out_g, w_up_g, conv_w_g, w_down_g = gathered
        lp.update(w_out=w_out_g.reshape(D, D), w_up=w_up_g, conv_w=conv_w_g.reshape(2, NSH, 3, FS),
                  w_down=w_down_g.reshape(NSH, 2 * ROWS_DOWN, D))

    conv_b_g = ffn_conv_b.reshape(NL, 2, NSH, 1, FS)

    ada_b_mine = lax.dynamic_slice_in_dim(ada_b, me * NCA, NCA, axis=1).reshape(NL, 1, NCA)
    mod_part = _ada_fwd(c_all, ada_w, ada_b_mine)
    mod_all = _gather_small(mod_part, "gather_mod")
    mod = lax.dynamic_index_in_dim(mod_all, me, axis=2, keepdims=False)
    mod = jnp.transpose(mod, (1, 0, 2)).reshape(NL, N_MOD, 1, D)

    row = lambda a: a.reshape(NL, 1, NS)
    ar_row, ai_row = row(ssm_a_re), row(ssm_a_im)
    ldt_row = row(jnp.broadcast_to(ssm_log_dt[:, :, None], (NL, G, SSM_STATE)))
    tiles = lambda a: a.reshape((NL, GT, GROUPS_PER_TILE) + a.shape[2:])
    braw_r = _block_diag(jnp.swapaxes(tiles(ssm_b_re), -1, -2))
    braw_i = _block_diag(jnp.swapaxes(tiles(ssm_b_im), -1, -2))
    crT = _block_diag(jnp.swapaxes(tiles(ssm_c_re), -1, -2)).astype(BF16)
    ciT = _block_diag(jnp.swapaxes(tiles(ssm_c_im), -1, -2)).astype(BF16)
    wg = _block_diag(tiles(glu_w)).astype(BF16)
    abr, abi, tabr, tabi, bsr, bsi = _s5_prep(ar_row, ai_row, ldt_row, braw_r, braw_i, T_S5)
    s5p = dict(bsr=bsr, bsi=bsi, tabr=tabr, tabi=tabi, crT=crT, ciT=ciT,
               d=ssm_d.reshape(NL, 1, SW), wg=wg, gb=glu_b.reshape(NL, 1, SW))

    gqk = jnp.concatenate([jnp.tile(q_norm_g, (1, NH)), jnp.tile(k_norm_g, (1, NH))], axis=1).reshape(NL, 1, 2 * AW)
    layer_params = dict(
        mod=mod, norm1_g=norm1_g.reshape(NL, 1, D), norm2_g=norm2_g.reshape(NL, 1, D), gqk=gqk,
        ga=attn_out_g.reshape(NL, 1, AW), gs=ssm_out_g.reshape(NL, 1, SW),
        conv_b=conv_b_g, s5=s5p)


    tn = min(D, 512)

    def resid_epilogue(acc, xres, gate):
        return acc, xres + gate * acc

    def layer_fwd(xin, lp, ride):
        sh1, sc1, g1, sh2, sc2, g2 = (lp["mod"][k] for k in range(N_MOD))
        h = _lnmod_fwd(xin, lp["norm1_g"], sh1, sc1)
        p = _matmul(
            "mm_in", h, lp["w_in_cols"], dims=NN, grid=(L // ts,),
            a_spec=pl.BlockSpec((ts, D), lambda i: (i, 0)), b_spec=pl.BlockSpec((D, NDEV * NIN), lambda i: (0, 0)),
            out_shape=jax.ShapeDtypeStruct((L, NDEV * NIN), F32),
            out_specs=pl.BlockSpec((ts, NDEV * NIN), lambda i: (i, 0)))
        qk = _qknorm_fwd(p, lp["gqk"], AW)
        o_attn, tot, gathered = _attn_fwd(qk, p, AW, B_ATT, ride)
        set_other_weights(lp, gathered[:4])
        next_gathered = gathered[4:]
        u_st = _to_streams(p[:, 3 * AW:], T_S5)
        o_st, s_re, s_im, ypre = _s5_fwd(u_st, lp["s5"], T_S5)
        o_ssm = _from_streams(o_st, T_S5)
        o = _outnorm_fwd(o_attn, o_ssm, lp["ga"], lp["gs"])
        a1, x_mid = _matmul(
            "mm_out", o, lp["w_out"], dims=NN, grid=(L // tm, D // tn),
            a_spec=pl.BlockSpec((tm, D), lambda i, j: (i, 0)), b_spec=pl.BlockSpec((D, tn), lambda i, j: (0, j)),
            extra=(xin, g1), extra_specs=(pl.BlockSpec((tm, tn), lambda i, j: (i, j)),
                                          pl.BlockSpec((1, tn), lambda i, j: (0, j))),
            epilogue=resid_epilogue,
            out_shape=[jax.ShapeDtypeStruct((L, D), F32)] * 2,
            out_specs=[pl.BlockSpec((tm, tn), lambda i, j: (i, j))] * 2)
        h2 = _lnmod_fwd(x_mid, lp["norm2_g"], sh2, sc2)
        up = _matmul(
            "mm_up", h2, lp["w_up"], dims=NN, grid=(L // tm, NDEV),
            a_spec=pl.BlockSpec((tm, D), lambda i, j: (i, 0)), b_spec=pl.BlockSpec((None, D, FS), lambda i, j: (j, 0, 0)),
            out_shape=jax.ShapeDtypeStruct((NDEV, L, FS), F32),
            out_specs=pl.BlockSpec((None, tm, FS), lambda i, j: (j, i, 0)))
        up = up.reshape(2, NSH, L, FS)
        act = _convglu_fwd(up, lp["conv_w"], lp["conv_b"])
        a2, x_out = _matmul(
            "mm_down", act, lp["w_down"], dims=NN, grid=(L // ts, D // tn),
            a_spec=pl.BlockSpec((NSH, ts, FS), lambda i, j: (0, i, 0)),
            b_spec=pl.BlockSpec((NSH, FS, tn), lambda i, j: (0, 0, j)),
            extra=(x_mid, g2), extra_specs=(pl.BlockSpec((ts, tn), lambda i, j: (i, j)),
                                            pl.BlockSpec((1, tn), lambda i, j: (0, j))),
            epilogue=resid_epilogue,
            out_shape=[jax.ShapeDtypeStruct((L, D), F32)] * 2,
            out_specs=[pl.BlockSpec((ts, tn), lambda i, j: (i, j))] * 2)
        res = dict(x=xin, h=h, p=p, qk=qk, tot=tot, o_attn=o_attn, o_ssm=o_ssm, u_st=u_st, s_re=s_re, s_im=s_im, ypre=ypre,
                   o=o, a1=a1, x_mid=x_mid, h2=h2, up=up, act=act, a2=a2)
        return x_out, res, next_gathered

    per_layer = [jax.tree.map(lambda a: a[l], layer_params) for l in range(NL)]
    y, residuals = x, []
    set_w_in(per_layer[0], first_w_in[0])
    for l in range(NL):
        ride = [s[l] for s in shards[1:]] + ([shards[0][l + 1]] if l + 1 < NL else [])
        y, res, next_w_in = layer_fwd(y, per_layer[l], ride)
        if next_w_in:
            set_w_in(per_layer[l + 1], next_w_in[0])
        residuals.append(res)

    loss_row, dy = _loss_head(y, target)
    loss = lax.psum(loss_row[0, 0], ("x", "y", "c"))

    def layer_bwd(dx, args, above):
        lp, r = args
        sh1, sc1, g1, sh2, sc2, g2 = (lp["mod"][k] for k in range(N_MOD))
        da2, dg2 = _gate_bwd(dx, r["a2"], g2)
        dact = _matmul(
            "mm_dact", da2, lp["w_down"], dims=NT, grid=(L // tm, NSH),
            a_spec=pl.BlockSpec((tm, D), lambda i, j: (i, 0)), b_spec=pl.BlockSpec((None, FS, D), lambda i, j: (j, 0, 0)),
            out_shape=jax.ShapeDtypeStruct((NSH, L, FS), F32),
            out_specs=pl.BlockSpec((None, tm, FS), lambda i, j: (j, i, 0)))
        dw_down = _matmul(
            "mm_dw_down", r["act"], da2, dims=TN, grid=(NSH, D // tn),
            a_spec=pl.BlockSpec((None, L, FS), lambda j, n: (j, 0, 0)),
            b_spec=pl.BlockSpec((L, tn), lambda j, n: (0, n)),
            out_shape=jax.ShapeDtypeStruct((NSH, FS, D), BF16),
            out_specs=pl.BlockSpec((None, FS, tn), lambda j, n: (j, 0, n)))
        dup4, dcw, dcb = _convglu_bwd(r["up"], dact, lp["conv_w"], lp["conv_b"])
        dup = dup4.reshape(NDEV, L, FS)
        dh2 = _matmul(
            "mm_dh2", dup, lp["w_up"], dims=NT, grid=(L // ts2,),
            a_spec=pl.BlockSpec((NDEV, ts2, FS), lambda i: (0, i, 0)),
            b_spec=pl.BlockSpec((NDEV, D, FS), lambda i: (0, 0, 0)),
            out_shape=jax.ShapeDtypeStruct((L, D), F32), out_specs=pl.BlockSpec((ts2, D), lambda i: (i, 0)))
        dw_up = _matmul(
            "mm_dw_up", r["h2"], dup, dims=TN, grid=(NDEV,),
            a_spec=pl.BlockSpec((L, D), lambda j: (0, 0)), b_spec=pl.BlockSpec((None, L, FS), lambda j: (j, 0, 0)),
            out_shape=jax.ShapeDtypeStruct((NDEV, D, FS), BF16),
            out_specs=pl.BlockSpec((None, D, FS), lambda j: (j, 0, 0)))
        dxm, dn2, dsh2, dsc2 = _lnmod_bwd(dh2, r["x_mid"], lp["norm2_g"], sh2, sc2, dx)
        da1, dg1 = _gate_bwd(dxm, r["a1"], g1)
        do = _matmul(
            "mm_do", da1, lp["w_out"], dims=NT, grid=(L // tm, D // tn),
            a_spec=pl.BlockSpec((tm, D), lambda i, j: (i, 0)), b_spec=pl.BlockSpec((tn, D), lambda i, j: (j, 0)),
            out_shape=jax.ShapeDtypeStruct((L, D), F32), out_specs=pl.BlockSpec((tm, tn), lambda i, j: (i, j)))
        dw_out = _matmul(
            "mm_dw_out", r["o"], da1, dims=TN, grid=(D // tn, D // tn),
            a_spec=pl.BlockSpec((L, tn), lambda m, n: (0, m)), b_spec=pl.BlockSpec((L, tn), lambda m, n: (0, n)),
            out_shape=jax.ShapeDtypeStruct((D, D), BF16), out_specs=pl.BlockSpec((tn, tn), lambda m, n: (m, n)))
        doa, dos, dga, dgs = _outnorm_bwd(do, r["o_attn"], r["o_ssm"], lp["ga"], lp["gs"])
        (du_st, dbsr, dbsi, dcr, dci, dd, dwg, dgb, dabr, dabi) = _s5_bwd(
            _to_streams(dos, T_S5), r["u_st"], r["s_re"], r["s_im"], r["ypre"], lp["s5"], T_S5)
        du = _from_streams(du_st, T_S5)
        ride = [dw_out.reshape(NDEV, ROWS_OUT, D), dw_up, dcw.reshape(NDEV, 3, FS),
                dw_down.reshape(NDEV, ROWS_DOWN, D)] + above
        dq, dk, dv, received = _attn_bwd(r["qk"], r["p"], doa, r["tot"], AW, B_ATT, ride)
        dqk, dgqk = _qknorm_bwd(jnp.concatenate([dq, dk], axis=1), r["p"], lp["gqk"], AW)
        dp = jnp.concatenate([dqk, dv.astype(BF16), du], axis=1)
        dh = _matmul(
            "mm_dh", dp, lp["w_in_cols"], dims=NT, grid=(L // tm,),
            a_spec=pl.BlockSpec((tm, NDEV * NIN), lambda i: (i, 0)), b_spec=pl.BlockSpec((D, NDEV * NIN), lambda i: (0, 0)),
            out_shape=jax.ShapeDtypeStruct((L, D), F32), out_specs=pl.BlockSpec((tm, D), lambda i: (i, 0)))
        dw_in = _matmul(
            "mm_dw_in", r["h"], dp, dims=TN, grid=(NDEV,),
            a_spec=pl.BlockSpec((L, D), lambda j: (0, 0)), b_spec=pl.BlockSpec((L, NIN), lambda j: (0, j)),
            out_shape=jax.ShapeDtypeStruct((NDEV, D, NIN), BF16),
            out_specs=pl.BlockSpec((None, D, NIN), lambda j: (j, 0, 0)))
        dx0, dn1, dsh1, dsc1 = _lnmod_bwd(dh, r["x"], lp["norm1_g"], sh1, sc1, dxm)
        grads = dict(
            dmod=jnp.concatenate([dsh1, dsc1, dg1, dsh2, dsc2, dg2], axis=1), dn1=dn1, dn2=dn2, dgqk=dgqk,
            dga=dga, dgs=dgs, dbsr=dbsr, dbsi=dbsi, dcr=dcr, dci=dci, dd=dd, dwg=dwg, dgb=dgb, dabr=dabr, dabi=dabi,
            dcb=dcb)
        return dx0, grads, dw_in, received

    grad_x, layer_grads, parts, above = dy, [None] * NL, [[None] * 5 for _ in range(NL)], []
    for l in reversed(range(NL)):
        grad_x, layer_grads[l], dw_in_l, received = layer_bwd(grad_x, (per_layer[l], residuals[l]), above)
        parts[l][1:] = received[:4]
        if above:
            parts[l + 1][0] = received[4]
        above = [dw_in_l]
    (parts[0][0],) = _scatter_direct(above, "scatter_weight_grads")
    gr = jax.tree.map(lambda *a: jnp.stack(a), *layer_grads)

    dar, dai, dldt, dbr_bd, dbi_bd = _s5_prep_bwd(ar_row, ai_row, ldt_row, braw_r, braw_i,
                                                  gr["dabr"], gr["dabi"], gr["dbsr"], gr["dbsi"])
    unt = lambda a: a.reshape((NL, G) + a.shape[3:])
    local = dict(
        ada_b=gr["dmod"].reshape(NL, N_MOD * D),
        norm1_g=gr["dn1"].reshape(NL, D), norm2_g=gr["dn2"].reshape(NL, D),
        q_norm_g=gr["dgqk"].reshape(NL, 2, NH, HEAD_DIM)[:, 0].sum(axis=1),
        k_norm_g=gr["dgqk"].reshape(NL, 2, NH, HEAD_DIM)[:, 1].sum(axis=1),
        ssm_a_re=dar.reshape(NL, G, SSM_STATE), ssm_a_im=dai.reshape(NL, G, SSM_STATE),
        ssm_log_dt=dldt.reshape(NL, G, SSM_STATE).sum(axis=-1),
        ssm_b_re=jnp.swapaxes(unt(_block_diag_extract(dbr_bd, SSM_GROUP, SSM_STATE)), -1, -2),
        ssm_b_im=jnp.swapaxes(unt(_block_diag_extract(dbi_bd, SSM_GROUP, SSM_STATE)), -1, -2),
        ssm_c_re=jnp.swapaxes(unt(_block_diag_extract(gr["dcr"], SSM_STATE, SSM_GROUP)), -1, -2),
        ssm_c_im=jnp.swapaxes(unt(_block_diag_extract(gr["dci"], SSM_STATE, SSM_GROUP)), -1, -2),
        ssm_d=gr["dd"].reshape(NL, G, SSM_GROUP),
        glu_w=unt(_block_diag_extract(gr["dwg"], SSM_GROUP, SSM_GROUP)),
        glu_b=gr["dgb"].reshape(NL, G, SSM_GROUP),
        attn_out_g=gr["dga"].reshape(NL, AW), ssm_out_g=gr["dgs"].reshape(NL, SW),
        ffn_conv_b=gr["dcb"].reshape(NL, 2 * NSH * FS),
    )

    def pack(tree):
        flat = jnp.concatenate([tree[n].reshape(-1) for n in small])
        pad = (-flat.shape[0]) % (512 * LANES)
        return jnp.pad(flat, (0, pad)).reshape(1, -1, LANES)

    (small_parts,) = _gather_shards([pack(local)], "gather_small_grads")
    sg, sd, sm, sv = _adamw(small_parts, pack(weights), pack(mom_m), pack(mom_v))

    def unpack(buf):
        flat = buf.reshape(-1)
        out, off = {}, 0
        for n in small:
            size = weights[n].size
            out[n] = flat[off:off + size].reshape(weights[n].shape)
            off += size
        return out

    ug, ud, um, uv = unpack(sg), unpack(sd), unpack(sm), unpack(sv)
    results = {n: (ug[n], ud[n], um[n], uv[n]) for n in small}

    dmod_all = _gather_small(gr["dmod"].reshape(NL, N_MOD * D), "gather_dmod")
    dmod_mine = lax.dynamic_slice_in_dim(dmod_all, me * NCA, NCA, axis=2)
    d_ada_w = _ada_bwd(jnp.transpose(c_all), jnp.transpose(dmod_mine, (1, 0, 2)))
    results["ada_w"] = tuple(_adamw(d_ada_w[:, None], ada_w, m_ada_w, v_ada_w))

    for a, n in enumerate(("w_in", "w_out", "ffn_w_up", "ffn_conv_w", "ffn_w_down")):
        results[n] = tuple(_adamw(jnp.stack([parts[l][a] for l in range(NL)]), weights[n], mom_m[n], mom_v[n]))

    out = [loss, grad_x[None]]
    for k in range(4):
        out.extend(results[n][k] for n in names)
    return tuple(out)
```

```python
import jax
import jax.numpy as jnp
from jax import lax
from jax.experimental import pallas as pl
from jax.experimental.pallas import tpu as pltpu

F32 = jnp.float32
BF16 = jnp.bfloat16
NDEV = 8
LANES = 128
HEAD_DIM = 64
SSM_GROUP = 16
SSM_STATE = 64
GROUPS_PER_TILE = LANES // SSM_GROUP
STATE_TILE = GROUPS_PER_TILE * SSM_STATE
N_MOD = 6
ATTN_STRIP = 32
EPS = 1e-6
ADAM_LR, ADAM_B1, ADAM_B2, ADAM_EPS, ADAM_WD, ADAM_STEP = 0.001, 0.9, 0.999, 1e-08, 0.01, 10
VMEM_LIMIT = 48 * 1024 * 1024
MESH_IDS = pl.DeviceIdType.MESH

NN = (((1,), (0,)), ((), ()))
NT = (((1,), (1,)), ((), ()))
TN = (((0,), (0,)), ((), ()))


def _dot(a, b, dims=NN):
    return lax.dot_general(a, b, dims, preferred_element_type=F32)


def _pcall(body, *, name, out_shape, in_specs, out_specs, grid=(), scratch=()):
    return pl.pallas_call(
        body, name=name, grid=grid, in_specs=in_specs, out_specs=out_specs, out_shape=out_shape,
        scratch_shapes=list(scratch),
        compiler_params=pltpu.CompilerParams(vmem_limit_bytes=VMEM_LIMIT))


def _row_tile(n, want=512):
    t = min(n, want)
    assert n % t == 0
    return t


def _my_index():
    return 4 * lax.axis_index("x") + 2 * lax.axis_index("y") + lax.axis_index("c")


HBM_SPEC = pl.BlockSpec(memory_space=pltpu.HBM)


def _mesh_place():
    x, y, c = lax.axis_index("x"), lax.axis_index("y"), lax.axis_index("c")
    chips = [(1 - x, y), (x, 1 - y), (1 - x, 1 - y)]
    return x, y, c, chips


def _gather_small(arr, name):
    def body(in_ref, out_ref, send_sems, recv_sems, local_sem):
        x, y, c, _ = _mesh_place()
        me = 4 * x + 2 * y + c
        own = pltpu.make_async_copy(in_ref, out_ref.at[me], local_sem)
        own.start()
        sends, recvs = [], []
        for k in range(1, NDEV):
            px = 1 - x if k & 4 else x
            py = 1 - y if k & 2 else y
            pc = 1 - c if k & 1 else c
            common = dict(send_sem=send_sems.at[k - 1], recv_sem=recv_sems.at[k - 1],
                          device_id=(px, py, pc), device_id_type=MESH_IDS)
            snd = pltpu.make_async_remote_copy(src_ref=in_ref, dst_ref=out_ref.at[me], **common)
            snd.start()
            sends.append(snd)
            recvs.append(pltpu.make_async_remote_copy(
                src_ref=in_ref, dst_ref=out_ref.at[4 * px + 2 * py + pc], **common))
        for r in recvs:
            r.wait_recv()
        for s in sends:
            s.wait_send()
        own.wait()

    return pl.pallas_call(
        body, name=name, out_shape=jax.ShapeDtypeStruct((NDEV,) + arr.shape, arr.dtype),
        in_specs=[HBM_SPEC], out_specs=HBM_SPEC,
        scratch_shapes=[pltpu.SemaphoreType.DMA((NDEV - 1,)), pltpu.SemaphoreType.DMA((NDEV - 1,)),
                        pltpu.SemaphoreType.DMA(())],
    )(arr)


def _all_to_all(srcs, dsts, send_sems, recv_sems, scatter):
    x, y, c, _ = _mesh_place()
    me = 4 * x + 2 * y + c
    pairs = []
    for k in range(1, NDEV):
        px = 1 - x if k & 4 else x
        py = 1 - y if k & 2 else y
        pc = 1 - c if k & 1 else c
        peer = 4 * px + 2 * py + pc
        for a in range(len(srcs)):
            src = srcs[a].at[peer] if scatter else srcs[a]
            common = dict(send_sem=send_sems.at[k - 1, a], recv_sem=recv_sems.at[k - 1, a],
                          device_id=(px, py, pc), device_id_type=MESH_IDS)
            pairs.append((pltpu.make_async_remote_copy(src_ref=src, dst_ref=dsts[a].at[me], **common),
                          pltpu.make_async_remote_copy(src_ref=src, dst_ref=dsts[a].at[peer], **common)))

    def start():
        for send, _ in pairs:
            send.start()

    def wait():
        for _, arrival in pairs:
            arrival.wait_recv()
        for send, _ in pairs:
            send.wait_send()

    return start, wait


def _place_own(outs, arrs, scatter):
    me = _my_index()
    owns = [lax.dynamic_slice_in_dim(a, me, 1, axis=0) if scatter else a[None] for a in arrs]
    return [lax.dynamic_update_slice_in_dim(o, own, me, axis=0) for o, own in zip(outs, owns)]


def _exchange_shapes(arrs, scatter):
    return [jax.ShapeDtypeStruct(a.shape if scatter else (NDEV,) + a.shape, a.dtype) for a in arrs]


def _scatter_direct(arrs, name):
    n = len(arrs)

    def body(*refs):
        start, wait = _all_to_all(refs[:n], refs[n:2 * n], refs[2 * n], refs[2 * n + 1], True)
        start()
        wait()

    outs = pl.pallas_call(
        body, name=name, out_shape=_exchange_shapes(arrs, True), in_specs=[HBM_SPEC] * n, out_specs=[HBM_SPEC] * n,
        scratch_shapes=[pltpu.SemaphoreType.DMA((NDEV - 1, n)), pltpu.SemaphoreType.DMA((NDEV - 1, n))],
    )(*arrs)
    return _place_own(outs, arrs, True)


def _gather_shards(arrs, name):
    n = len(arrs)

    def body(*refs):
        ins, outs = refs[:n], refs[n:2 * n]
        send_sems, recv_sems = refs[2 * n:]
        x, y, c, chips = _mesh_place()
        dev = lambda px, py, pc: 4 * px + 2 * py + pc

        def copy(k, a, src, block, to):
            return pltpu.make_async_remote_copy(
                src_ref=src, dst_ref=outs[a].at[:, block], send_sem=send_sems.at[k, a],
                recv_sem=recv_sems.at[k, a], device_id=to, device_id_type=MESH_IDS)

        me = dev(x, y, c)
        sent = [copy(0, a, ins[a], me, (x, y, 1 - c)) for a in range(n)]
        sent += [copy(1 + j, a, ins[a], me, (px, py, c)) for j, (px, py) in enumerate(chips) for a in range(n)]
        for s in sent:
            s.start()
        for j, (px, py) in enumerate(chips):
            for a in range(n):
                blk = dev(px, py, c)
                copy(1 + j, a, ins[a], blk, (x, y, c)).wait_recv()
                forward = copy(4 + j, a, outs[a].at[:, blk], blk, (x, y, 1 - c))
                forward.start()
                sent.append(forward)
        for a in range(n):
            copy(0, a, ins[a], dev(x, y, 1 - c), (x, y, c)).wait_recv()
        for j, (px, py) in enumerate(chips):
            for a in range(n):
                copy(4 + j, a, ins[a], dev(px, py, 1 - c), (x, y, c)).wait_recv()
        for s in sent:
            s.wait_send()

    out_shape = [jax.ShapeDtypeStruct((a.shape[0], NDEV) + a.shape[1:], a.dtype) for a in arrs]
    outs = pl.pallas_call(
        body, name=name, out_shape=out_shape, in_specs=[HBM_SPEC] * n, out_specs=[HBM_SPEC] * n,
        scratch_shapes=[pltpu.SemaphoreType.DMA((7, n)), pltpu.SemaphoreType.DMA((7, n))],
    )(*arrs)
    me = _my_index()
    return [lax.dynamic_update_slice_in_dim(o, a[:, None], me, axis=1) for o, a in zip(outs, arrs)]


def _matmul(name, a, b, *, dims, grid, a_spec, b_spec, out_shape, out_specs, kaxis=None, acc_shape=None,
            extra=(), extra_specs=(), epilogue=None):
    nk = grid[kaxis] if kaxis is not None else 1
    ne = len(extra)
    multi = isinstance(out_shape, (list, tuple))
    n_out = len(out_shape) if multi else 1

    def body(*refs):
        a_ref, b_ref = refs[0], refs[1]
        ex = refs[2:2 + ne]
        outs = refs[2 + ne:2 + ne + n_out]

        def write(res):
            vals = epilogue(res, *[e[...] for e in ex]) if epilogue is not None else (res,)
            for o, v in zip(outs, vals):
                o[...] = v.astype(o.dtype)

        if len(a_ref.shape) == 3:
            part = _dot(a_ref[0].astype(BF16), b_ref[0].astype(BF16), dims)
            for s in range(1, a_ref.shape[0]):
                part = part + _dot(a_ref[s].astype(BF16), b_ref[s].astype(BF16), dims)
        else:
            part = _dot(a_ref[...].astype(BF16), b_ref[...].astype(BF16), dims)
        if nk == 1:
            write(part)
        else:
            acc = refs[-1]
            k = pl.program_id(kaxis)

            @pl.when(k == 0)
            def _():
                acc[...] = part

            @pl.when(k > 0)
            def _():
                acc[...] += part

            @pl.when(k == nk - 1)
            def _():
                write(acc[...])

    scratch = [pltpu.VMEM(acc_shape, F32)] if nk > 1 else []
    return _pcall(body, name=name, grid=grid, in_specs=[a_spec, b_spec, *extra_specs],
                  out_specs=out_specs, out_shape=out_shape, scratch=scratch)(a, b, *extra)


def _rms(x, g):
    inv = lax.rsqrt(jnp.mean(x * x, axis=-1, keepdims=True) + EPS)
    return x * inv * g


def _lnmod_math(x, g, sh, sc):
    return _rms(x, g) * (1.0 + sc) + sh


def _head_sums(x):
    row = lax.broadcasted_iota(jnp.int32, (LANES, LANES), 0) // HEAD_DIM
    col = lax.broadcasted_iota(jnp.int32, (LANES, LANES), 1) // HEAD_DIM
    same_head = jnp.where(row == col, 1.0, 0.0).astype(BF16)
    hi = x.astype(BF16)
    lo = (x - hi.astype(F32)).astype(BF16)
    return _dot(hi, same_head) + _dot(lo, same_head)


def _qkn_inv(p):
    return lax.rsqrt(_head_sums(p * p) / HEAD_DIM + EPS)


def _glu_math(val, gate):
    return jax.nn.gelu(gate) * val


def _accumulate(ref, val, first):
    @pl.when(first)
    def _():
        ref[...] = val

    @pl.when(jnp.logical_not(first))
    def _():
        ref[...] += val


def _lnmod_fwd(x, g, sh, sc):
    L, D = x.shape
    tm = _row_tile(L, 1024)

    def body(x_ref, g_ref, sh_ref, sc_ref, h_ref):
        h_ref[...] = _lnmod_math(x_ref[...], g_ref[...], sh_ref[...], sc_ref[...]).astype(BF16)

    row = pl.BlockSpec((tm, D), lambda i: (i, 0))
    vec = pl.BlockSpec((1, D), lambda i: (0, 0))
    return _pcall(body, name="lnmod_fwd", grid=(L // tm,), in_specs=[row, vec, vec, vec], out_specs=row,
                  out_shape=jax.ShapeDtypeStruct((L, D), BF16))(x, g, sh, sc)


def _lnmod_bwd(dh, x, g, sh, sc, dres):
    L, D = x.shape
    tm = _row_tile(L)

    def body(dh_ref, x_ref, g_ref, sh_ref, sc_ref, res_ref, dx_ref, dg_ref, dsh_ref, dsc_ref):
        _, vjp = jax.vjp(_lnmod_math, x_ref[...], g_ref[...], sh_ref[...], sc_ref[...])
        dx, dg, dsh, dsc = vjp(dh_ref[...])
        dx_ref[...] = dx + res_ref[...]
        first = pl.program_id(0) == 0
        _accumulate(dg_ref, dg, first)
        _accumulate(dsh_ref, dsh, first)
        _accumulate(dsc_ref, dsc, first)

    row = pl.BlockSpec((tm, D), lambda i: (i, 0))
    vec = pl.BlockSpec((1, D), lambda i: (0, 0))
    vs = jax.ShapeDtypeStruct((1, D), F32)
    return _pcall(body, name="lnmod_bwd", grid=(L // tm,), in_specs=[row, row, vec, vec, vec, row],
                  out_specs=[row, vec, vec, vec],
                  out_shape=[jax.ShapeDtypeStruct((L, D), F32), vs, vs, vs])(dh, x, g, sh, sc, dres)


def _gate_bwd(dx, a, gate):
    L, D = dx.shape
    tm = _row_tile(L, 1024)

    def body(dx_ref, a_ref, g_ref, da_ref, dg_ref):
        dxv = dx_ref[...]
        da_ref[...] = (g_ref[...] * dxv).astype(BF16)
        _accumulate(dg_ref, jnp.sum(dxv * a_ref[...], axis=0, keepdims=True), pl.program_id(0) == 0)

    row = pl.BlockSpec((tm, D), lambda i: (i, 0))
    vec = pl.BlockSpec((1, D), lambda i: (0, 0))
    return _pcall(body, name="gate_bwd", grid=(L // tm,), in_specs=[row, row, vec], out_specs=[row, vec],
                  out_shape=[jax.ShapeDtypeStruct((L, D), BF16), jax.ShapeDtypeStruct((1, D), F32)])(dx, a, gate)


def _qknorm_fwd(p, gqk, AW):
    L = p.shape[0]
    tm = _row_tile(L, 2048)
    ncol = 2 * AW // LANES

    def body(p_ref, g_ref, o_ref):
        p = p_ref[...]
        o_ref[...] = (p * _qkn_inv(p) * g_ref[...]).astype(BF16)

    blk = pl.BlockSpec((tm, LANES), lambda i, j: (i, j))
    vec = pl.BlockSpec((1, LANES), lambda i, j: (0, j))
    return _pcall(body, name="qknorm_fwd", grid=(L // tm, ncol), in_specs=[blk, vec], out_specs=blk,
                  out_shape=jax.ShapeDtypeStruct((L, 2 * AW), BF16))(p, gqk)


def _qknorm_bwd(dqk, p, gqk, AW):
    L = p.shape[0]
    tm = _row_tile(L, 2048)
    ncol = 2 * AW // LANES

    def body(d_ref, p_ref, g_ref, dp_ref, dg_ref):
        p, dy = p_ref[...], d_ref[...]
        inv = _qkn_inv(p)
        gdy = g_ref[...] * dy
        dp_ref[...] = (inv * gdy - p * (inv * inv * inv) * (_head_sums(p * gdy) / HEAD_DIM)).astype(BF16)
        _accumulate(dg_ref, jnp.sum(dy * p * inv, axis=0, keepdims=True), pl.program_id(1) == 0)

    blk = pl.BlockSpec((tm, LANES), lambda j, i: (i, j))
    vec = pl.BlockSpec((1, LANES), lambda j, i: (0, j))
    return _pcall(body, name="qknorm_bwd", grid=(ncol, L // tm), in_specs=[blk, blk, vec], out_specs=[blk, vec],
                  out_shape=[jax.ShapeDtypeStruct((L, 2 * AW), BF16),
                             jax.ShapeDtypeStruct((1, 2 * AW), F32)])(dqk, p, gqk)


def _outnorm_fwd(oa, os_, ga, gs):
    L, AW = oa.shape
    SW = os_.shape[1]
    tm = _row_tile(L)

    def body(oa_ref, os_ref, ga_ref, gs_ref, o_ref):
        o_ref[:, :AW] = _rms(oa_ref[...], ga_ref[...]).astype(BF16)
        o_ref[:, AW:] = _rms(os_ref[...], gs_ref[...]).astype(BF16)

    ra = pl.BlockSpec((tm, AW), lambda i: (i, 0))
    rs = pl.BlockSpec((tm, SW), lambda i: (i, 0))
    va = pl.BlockSpec((1, AW), lambda i: (0, 0))
    vs = pl.BlockSpec((1, SW), lambda i: (0, 0))
    ro = pl.BlockSpec((tm, AW + SW), lambda i: (i, 0))
    return _pcall(body, name="outnorm_fwd", grid=(L // tm,), in_specs=[ra, rs, va, vs], out_specs=ro,
                  out_shape=jax.ShapeDtypeStruct((L, AW + SW), BF16))(oa, os_, ga, gs)


def _outnorm_bwd(do, oa, os_, ga, gs):
    L, AW = oa.shape
    SW = os_.shape[1]
    tm = _row_tile(L)

    def body(do_ref, oa_ref, os_ref, ga_ref, gs_ref, doa_ref, dos_ref, dga_ref, dgs_ref):
        first = pl.program_id(0) == 0
        _, vjp_a = jax.vjp(_rms, oa_ref[...], ga_ref[...])
        doa, dga = vjp_a(do_ref[:, :AW])
        _, vjp_s = jax.vjp(_rms, os_ref[...], gs_ref[...])
        dos, dgs = vjp_s(do_ref[:, AW:])
        doa_ref[...] = doa
        dos_ref[...] = dos
        _accumulate(dga_ref, dga, first)
        _accumulate(dgs_ref, dgs, first)

    ra = pl.BlockSpec((tm, AW), lambda i: (i, 0))
    rs = pl.BlockSpec((tm, SW), lambda i: (i, 0))
    va = pl.BlockSpec((1, AW), lambda i: (0, 0))
    vs = pl.BlockSpec((1, SW), lambda i: (0, 0))
    ro = pl.BlockSpec((tm, AW + SW), lambda i: (i, 0))
    return _pcall(body, name="outnorm_bwd", grid=(L // tm,), in_specs=[ro, ra, rs, va, vs],
                  out_specs=[ra, rs, va, vs],
                  out_shape=[jax.ShapeDtypeStruct((L, AW), F32), jax.ShapeDtypeStruct((L, SW), F32),
                             jax.ShapeDtypeStruct((1, AW), F32), jax.ShapeDtypeStruct((1, SW), F32)])(
                                 do, oa, os_, ga, gs)


def _softplus_neg_abs(z):
    return jnp.log(1.0 + jnp.exp(-jnp.abs(z)))


def _attn_masks(B):
    row = lax.broadcasted_iota(jnp.int32, (B, B), 0)
    col = lax.broadcasted_iota(jnp.int32, (B, B), 1)
    strict = col < row
    upper = jnp.where(row > col, 1.0, 0.0).astype(BF16)
    lower = jnp.where(row < col, 1.0, 0.0).astype(BF16)
    return strict, upper, lower


def _ride_along(ride, scatter, HP, nb):
    n = len(ride)
    sems = [pltpu.SemaphoreType.DMA((NDEV - 1, n)), pltpu.SemaphoreType.DMA((NDEV - 1, n))] if n else []

    def hooks(srcs, dsts, send_sems, recv_sems):
        start, wait = _all_to_all(srcs, dsts, send_sems, recv_sems, scatter)
        hp, i = pl.program_id(0), pl.program_id(1)
        return (hp == 0) & (i == 0), start, (hp == HP - 1) & (i == nb - 1), wait

    return [HBM_SPEC] * n, _exchange_shapes(ride, scatter), sems, hooks


def _attn_fwd(qk, p, AW, B, ride=()):
    L = qk.shape[0]
    HP = AW // LANES
    nb = L // B
    n = len(ride)
    ride_specs, ride_shapes, ride_sems, ride_hooks = _ride_along(ride, False, HP, nb)

    def body(*refs):
        q_ref, k_ref, v_ref = refs[:3]
        o_ref, tot_ref = refs[3 + n:5 + n]
        scr = refs[5 + 2 * n:]
        if n:
            first, start, last, wait = ride_hooks(refs[3:3 + n], refs[5 + n:5 + 2 * n], scr[10], scr[11])
            pl.when(first)(start)
        lb_scr = (scr[0:2], scr[2:4])
        tail_scr = (scr[4:6], scr[6:8])
        sum_scr = scr[8:10]
        i = pl.program_id(1)
        m0 = lax.broadcasted_iota(jnp.int32, (1, LANES), 1) < HEAD_DIM
        strict, upper, _ = _attn_masks(B)
        q = q_ref[...] * 0.125
        zq = jnp.zeros_like(q)
        qh = (jnp.where(m0, q, zq), jnp.where(m0, zq, q))

        def keys(j):
            start = pl.multiple_of(jnp.maximum(j, 0) * B, B)
            return k_ref[pl.ds(start, B), :]

        def vals(j):
            start = pl.multiple_of(jnp.maximum(j, 0) * B, B)
            return v_ref[pl.ds(start, B), :].astype(BF16)

        strips = [slice(s, min(s + ATTN_STRIP, B)) for s in range(0, B, ATTN_STRIP)]

        def scores(j):
            kj = keys(j)
            return tuple(_dot(qh[h], kj, NT) for h in (0, 1))

        def logits(zs2, slot, diag):
            for h in (0, 1):
                z = zs2[h]
                his = []
                for rows in strips:
                    zs = z[rows]
                    lb = jnp.minimum(zs, 0.0) - _softplus_neg_abs(zs)
                    l1 = lb - zs
                    if diag:
                        l1 = jnp.where(strict[rows], l1, 0.0)
                    lb_scr[slot][h][rows, :] = lb
                    his.append(l1.astype(BF16))
                    rsum = jnp.sum(l1, axis=-1, keepdims=True)
                    if h == 0:
                        sum_scr[slot][rows, :] = jnp.broadcast_to(rsum, (rows.stop - rows.start, LANES))
                    else:
                        sum_scr[slot][rows, :] = jnp.where(m0, sum_scr[slot][rows, :], rsum)
                cat = lambda xs: jnp.concatenate(xs, axis=0)
                tail_scr[slot][h][...] = _dot(cat(his), upper)

        def attend(j, slot, diag):
            vj = vals(j)
            pv = []
            for h in (0, 1):
                ws = []
                for rows in strips:
                    w = jnp.exp(lb_scr[slot][h][rows, :] + tail_scr[slot][h][rows, :])
                    if diag:
                        w = jnp.where(strict[rows], w, 0.0)
                    ws.append(w.astype(BF16))
                pv.append(_dot(jnp.concatenate(ws, axis=0), vj))
            return jnp.where(m0, pv[0], pv[1])

        logits(scores(i), 0, True)
        o_ref[...] = attend(i, 0, True)
        tot_ref[...] = sum_scr[0][...]

        def half(j, slot):
            z = scores(j - 1)
            pv = attend(j, slot, False)
            logits(z, 1 - slot, False)
            o_ref[...] += pv * jnp.exp(tot_ref[...])
            tot_ref[...] += sum_scr[slot][...]

        @pl.when(i > 0)
        def _():
            logits(scores(i - 1), 1, False)

            @pl.loop(0, (i + 1) // 2)
            def _(t):
                j = i - 1 - 2 * t
                half(j, 1)

                @pl.when(j > 0)
                def _():
                    half(j - 1, 0)

        if n:
            pl.when(last)(wait)

    qspec = pl.BlockSpec((B, LANES), lambda hp, i: (i, hp))
    kspec = pl.BlockSpec((L, LANES), lambda hp, i: (0, HP + hp))
    vspec = pl.BlockSpec((L, LANES), lambda hp, i: (0, 2 * HP + hp))
    ospec = pl.BlockSpec((B, LANES), lambda hp, i: (i, hp))
    shp = jax.ShapeDtypeStruct((L, AW), F32)
    o, tot, *gathered = _pcall(
        body, name="attn_fwd", grid=(HP, nb), in_specs=[qspec, kspec, vspec] + ride_specs,
        out_specs=[ospec, ospec] + ride_specs, out_shape=[shp, shp] + ride_shapes,
        scratch=[pltpu.VMEM((B, B), F32)] * 8 + [pltpu.VMEM((B, LANES), F32)] * 2 + ride_sems)(qk, qk, p, *ride)
    return o, tot, _place_own(gathered, ride, False)


def _attn_bwd(qk, p, do, tot, AW, B, ride=()):
    L = qk.shape[0]
    HP = AW // LANES
    nb = L // B
    n = len(ride)
    ride_specs, ride_shapes, ride_sems, ride_hooks = _ride_along(ride, True, HP, nb)

    def body(*refs):
        q_ref, k_ref, v_ref, do_ref, tot_ref = refs[:5]
        dq_ref, dk_ref, dv_ref = refs[5 + n:8 + n]
        scr = refs[8 + 2 * n:]
        if n:
            first_step, start, last_step, wait = ride_hooks(refs[5:5 + n], refs[8 + n:8 + 2 * n], scr[15], scr[16])
            pl.when(first_step)(start)
        lb_scr, tail_scr, dw_scr, e_scr, beta_scr = scr[0:2], scr[2:4], scr[4:6], scr[6:8], scr[8:10]
        dos_scr, bsum_scr, left_scr, esum_scr, ecum_scr = scr[10:15]
        i = pl.program_id(1)
        m0 = lax.broadcasted_iota(jnp.int32, (1, LANES), 1) < HEAD_DIM
        strict, upper, lower = _attn_masks(B)
        strips = [slice(s, min(s + ATTN_STRIP, B)) for s in range(0, B, ATTN_STRIP)]
        cat = lambda xs: jnp.concatenate(xs, axis=0)

        @pl.when(i == 0)
        def _():
            dk_ref[...] = jnp.zeros_like(dk_ref)
            dv_ref[...] = jnp.zeros_like(dv_ref)

        q = q_ref[...] * 0.125
        zq = jnp.zeros_like(q)
        qh = (jnp.where(m0, q, zq), jnp.where(m0, zq, q))
        heads = lambda a: (jnp.where(m0, a, jnp.zeros_like(a)), jnp.where(m0, jnp.zeros_like(a), a))
        left_scr[...] = jnp.zeros_like(left_scr)
        ecum_scr[...] = jnp.zeros_like(ecum_scr)
        dq_ref[...] = jnp.zeros_like(dq_ref)

        def block_rows(j):
            return pl.ds(pl.multiple_of(j * B, B), B)

        def put_row_sums(ref, rows, h, rsum):
            if h == 0:
                ref[rows, :] = jnp.broadcast_to(rsum, (rows.stop - rows.start, LANES))
            else:
                ref[rows, :] = jnp.where(m0, ref[rows, :], rsum)

        def head_cols(x, h):
            other = pltpu.roll(x, HEAD_DIM, 1)
            full = jnp.where(m0, x, other) if h == 0 else jnp.where(m0, other, x)
            return jnp.concatenate([full] * (B // LANES), axis=1) if B > LANES else full

        def scores(j):
            kj = k_ref[block_rows(j), :]
            return tuple(_dot(qh[h], kj, NT) for h in (0, 1))

        def stage_a(j, zs2, diag):
            for h in (0, 1):
                his = []
                for rows in strips:
                    zs = zs2[h][rows]
                    lb = jnp.minimum(zs, 0.0) - _softplus_neg_abs(zs)
                    l1 = lb - zs
                    if diag:
                        l1 = jnp.where(strict[rows], l1, 0.0)
                    lb_scr[h][rows, :] = lb
                    his.append(l1.astype(BF16))
                    put_row_sums(bsum_scr, rows, h, jnp.sum(l1, axis=-1, keepdims=True))
                tail_scr[h][...] = _dot(cat(his), upper)
            bs = bsum_scr[...]
            scale = jnp.exp(tot_ref[...] - left_scr[...] - bs)
            left_scr[...] += bs
            dos = (do_ref[...] * scale).astype(BF16)
            dos_scr[...] = dos
            vj = v_ref[block_rows(j), :].astype(BF16)
            dosh = heads(dos)
            for h in (0, 1):
                dw_scr[h][...] = _dot(dosh[h], vj, NT)

        def stage_b(j, diag):
            dosh = heads(dos_scr[...])
            pres = []
            dv_blk = jnp.zeros((B, LANES), F32)
            for h in (0, 1):
                ehs, wbs = [], []
                for rows in strips:
                    lb = lb_scr[h][rows, :]
                    w = jnp.exp(lb + tail_scr[h][rows, :])
                    if diag:
                        w = jnp.where(strict[rows], w, 0.0)
                    beta_scr[h][rows, :] = jnp.exp(lb)
                    e = dw_scr[h][rows, :] * w
                    e_scr[h][rows, :] = e
                    ehs.append(e.astype(BF16))
                    wbs.append(w.astype(BF16))
                    put_row_sums(esum_scr, rows, h, jnp.sum(e, axis=-1, keepdims=True))
                pres.append(_dot(cat(ehs), lower))
                dv_blk = dv_blk + _dot(cat(wbs), dosh[h], TN)
            dv_ref[block_rows(j), :] += dv_blk
            return pres

        def stage_c(j, pres, diag):
            kh = heads(k_ref[block_rows(j), :])
            dq = jnp.zeros((B, LANES), F32)
            dk_blk = jnp.zeros((B, LANES), F32)
            for h in (0, 1):
                dzs = []
                for rows in strips:
                    e = e_scr[h][rows, :]
                    dl1 = pres[h][rows] + head_cols(ecum_scr[rows, :], h)
                    dz = e - beta_scr[h][rows, :] * (e + dl1)
                    if diag:
                        dz = jnp.where(strict[rows], dz, 0.0)
                    dzs.append(dz.astype(BF16))
                dzb = cat(dzs)
                dq = dq + _dot(dzb, kh[h])
                dk_blk = dk_blk + _dot(dzb, qh[h], TN)
            dq_ref[...] += dq
            dk_ref[block_rows(j), :] += dk_blk
            ecum_scr[...] += esum_scr[...]

        def step(t, diag_next):
            z = scores(t + 1)
            pres = stage_b(t, False)
            stage_a(t + 1, z, diag_next)
            stage_c(t, pres, False)

        @pl.when(i > 0)
        def _():
            stage_a(0, scores(0), False)

            @pl.loop(0, i - 1)
            def _(t):
                step(t, False)

            step(i - 1, True)

        @pl.when(i == 0)
        def _():
            stage_a(0, scores(0), True)

        stage_c(i, stage_b(i, True), True)
        dq_ref[...] = dq_ref[...] * 0.125
        if n:
            pl.when(last_step)(wait)

    qspec = pl.BlockSpec((B, LANES), lambda hp, i: (i, hp))
    kspec = pl.BlockSpec((L, LANES), lambda hp, i: (0, HP + hp))
    vspec = pl.BlockSpec((L, LANES), lambda hp, i: (0, 2 * HP + hp))
    full = pl.BlockSpec((L, LANES), lambda hp, i: (0, hp))
    shp = jax.ShapeDtypeStruct((L, AW), F32)
    dq, dk, dv, *received = _pcall(
        body, name="attn_bwd", grid=(HP, nb), in_specs=[qspec, kspec, vspec, qspec, qspec] + ride_specs,
        out_specs=[qspec, full, full] + ride_specs, out_shape=[shp, shp, shp] + ride_shapes,
        scratch=[pltpu.VMEM((B, B), F32)] * 10 + [pltpu.VMEM((B, LANES), BF16)]
        + [pltpu.VMEM((B, LANES), F32)] * 4 + ride_sems)(qk, qk, p, do, tot, *ride)
    return dq, dk, dv, _place_own(received, ride, True)


def _s5_disc(ar, ai, ldt):
    dt = jnp.exp(ldt)
    mag = jnp.exp(dt * ar)
    abr = mag * jnp.cos(dt * ai)
    abi = mag * jnp.sin(dt * ai)
    emr = abr - 1.0
    emi = abi
    den = ar * ar + ai * ai
    fr = (emr * ar + emi * ai) / den
    fi = (emi * ar - emr * ai) / den
    return abr, abi, fr, fi


def _s5_params_math(ar, ai, ldt, br, bi):
    abr, abi, fr, fi = _s5_disc(ar, ai, ldt)
    return abr, abi, fr * br - fi * bi, fr * bi + fi * br


def _cmul_add(xr, xi, kr, ki, sr, si):
    return xr + (kr * sr - ki * si), xi + (kr * si + ki * sr)


def _s5_tab_rows(T):
    ng = T // 8
    return dict(NG=ng, POW=0, PA=ng, PAR=ng + 8, LA=ng + 16, ROWS=ng + 24)


def _s5_prep(ar, ai, ldt, braw_r, braw_i, T):
    NL, _, NS = ar.shape
    GT = NS // STATE_TILE
    R = _s5_tab_rows(T)

    def body(ar_ref, ai_ref, ldt_ref, br_ref, bi_ref, abr_ref, abi_ref, tabr_ref, tabi_ref, bsr_ref, bsi_ref):
        for t in range(GT):
            sl = slice(t * STATE_TILE, (t + 1) * STATE_TILE)
            abr, abi, bsr, bsi = _s5_params_math(ar_ref[:, sl], ai_ref[:, sl], ldt_ref[:, sl],
                                                 br_ref[t], bi_ref[t])
            abr_ref[:, sl] = abr
            abi_ref[:, sl] = abi
            bsr_ref[t] = bsr.astype(BF16)
            bsi_ref[t] = bsi.astype(BF16)

            def put(row, vr, vi):
                tabr_ref[row:row + 1, sl] = vr
                tabi_ref[row:row + 1, sl] = vi

            pr, pi = abr, abi
            for g in range(R["NG"]):
                put(R["POW"] + g, pr, pi)
                if g + 1 < R["NG"]:
                    pr, pi = pr * abr - pi * abi, pr * abi + pi * abr
            big_r, big_i = pr, pi
            qr, qi = big_r, big_i
            for r in range(8):
                put(R["PA"] + r, qr, qi)
                put(R["PAR"] + 7 - r, qr, qi)
                qr, qi = qr * big_r - qi * big_i, qr * big_i + qi * big_r
            qr, qi = big_r, big_i
            for k in range(3):
                put(R["LA"] + k, qr, qi)
                qr, qi = qr * qr - qi * qi, 2.0 * qr * qi
            for k in range(3, 8):
                put(R["LA"] + k, jnp.zeros_like(qr), jnp.zeros_like(qi))

    rowspec = pl.BlockSpec((None, 1, NS), lambda l: (l, 0, 0))
    bspec = pl.BlockSpec((None, GT, LANES, STATE_TILE), lambda l: (l, 0, 0, 0))
    tabspec = pl.BlockSpec((None, R["ROWS"], NS), lambda l: (l, 0, 0))
    rs = jax.ShapeDtypeStruct((NL, 1, NS), F32)
    ts = jax.ShapeDtypeStruct((NL, R["ROWS"], NS), F32)
    bs = jax.ShapeDtypeStruct((NL, GT, LANES, STATE_TILE), BF16)
    return _pcall(body, name="s5_prep", grid=(NL,), in_specs=[rowspec] * 3 + [bspec] * 2,
                  out_specs=[rowspec, rowspec, tabspec, tabspec, bspec, bspec],
                  out_shape=[rs, rs, ts, ts, bs, bs])(ar, ai, ldt, braw_r, braw_i)


def _s5_prep_bwd(ar, ai, ldt, braw_r, braw_i, dabr, dabi, dbsr, dbsi):
    NL, _, NS = ar.shape
    GT = NS // STATE_TILE

    def body(ar_ref, ai_ref, ldt_ref, br_ref, bi_ref, dabr_ref, dabi_ref, dbsr_ref, dbsi_ref,
             dar_ref, dai_ref, dldt_ref, dbr_ref, dbi_ref):
        for t in range(GT):
            sl = slice(t * STATE_TILE, (t + 1) * STATE_TILE)
            _, vjp = jax.vjp(_s5_params_math, ar_ref[:, sl], ai_ref[:, sl], ldt_ref[:, sl],
                             br_ref[t], bi_ref[t])
            dabr_row = jnp.sum(dabr_ref[:, sl], axis=0, keepdims=True)
            dabi_row = jnp.sum(dabi_ref[:, sl], axis=0, keepdims=True)
            dar, dai, dldt, dbr, dbi = vjp((dabr_row, dabi_row, dbsr_ref[t], dbsi_ref[t]))
            dar_ref[:, sl] = dar
            dai_ref[:, sl] = dai
            dldt_ref[:, sl] = dldt
            dbr_ref[t] = dbr
            dbi_ref[t] = dbi

    rowspec = pl.BlockSpec((None, 1, NS), lambda l: (l, 0, 0))
    row8spec = pl.BlockSpec((None, 8, NS), lambda l: (l, 0, 0))
    bspec = pl.BlockSpec((None, GT, LANES, STATE_TILE), lambda l: (l, 0, 0, 0))
    rs = jax.ShapeDtypeStruct((NL, 1, NS), F32)
    bs = jax.ShapeDtypeStruct((NL, GT, LANES, STATE_TILE), F32)
    return _pcall(body, name="s5_prep_bwd", grid=(NL,), in_specs=[rowspec] * 3 + [bspec] * 2 + [row8spec] * 2 + [bspec] * 2,
                  out_specs=[rowspec] * 3 + [bspec] * 2, out_shape=[rs, rs, rs, bs, bs])(
                      ar, ai, ldt, braw_r, braw_i, dabr, dabi, dbsr, dbsi)


def _to_streams(x, T):
    L, C = x.shape
    return x.reshape(L // T, 8, T // 8, C).transpose(0, 2, 1, 3).reshape(L, C)


def _from_streams(x, T):
    L, C = x.shape
    return x.reshape(L // T, T // 8, 8, C).transpose(0, 2, 1, 3).reshape(L, C)


def _s5_specs(L, SW, T, rev):
    GT = SW // LANES
    nc = L // T
    cidx = (lambda c: nc - 1 - c) if rev else (lambda c: c)
    rows = _s5_tab_rows(T)["ROWS"]
    return dict(
        GT=GT, nc=nc, cidx=cidx,
        chan=pl.BlockSpec((T, LANES), lambda j, c: (cidx(c), j)),
        state=pl.BlockSpec((T, STATE_TILE), lambda j, c: (cidx(c), j)),
        bmat=pl.BlockSpec((None, LANES, STATE_TILE), lambda j, c: (j, 0, 0)),
        cmat=pl.BlockSpec((None, STATE_TILE, LANES), lambda j, c: (j, 0, 0)),
        gmat=pl.BlockSpec((None, LANES, LANES), lambda j, c: (j, 0, 0)),
        cvec=pl.BlockSpec((1, LANES), lambda j, c: (0, j)),
        svec8=pl.BlockSpec((8, STATE_TILE), lambda j, c: (0, j)),
        tab=pl.BlockSpec((rows, STATE_TILE), lambda j, c: (0, j)),
    )


def _s5_fwd(u, sp, T):
    L, SW = u.shape
    S = _s5_specs(L, SW, T, False)
    NS = S["GT"] * STATE_TILE
    R = _s5_tab_rows(T)
    NG = R["NG"]

    def body(u_ref, bsr_ref, bsi_ref, tabr_ref, tabi_ref, cr_ref, ci_ref, d_ref, wg_ref, gb_ref,
             o_ref, sr_ref, si_ref, y_ref, carr_ref, cari_ref):
        @pl.when(pl.program_id(1) == 0)
        def _():
            carr_ref[...] = jnp.zeros_like(carr_ref)
            cari_ref[...] = jnp.zeros_like(cari_ref)

        uv = u_ref[...]
        ub = uv.astype(BF16)
        row8 = lax.broadcasted_iota(jnp.int32, (8, LANES), 0)
        nstrips = STATE_TILE // LANES

        def project(s):
            sl = slice(s * LANES, (s + 1) * LANES)
            return _dot(ub, bsr_ref[:, sl]), _dot(ub, bsi_ref[:, sl])

        ahead = project(0)
        for s in range(nstrips):
            sl = slice(s * LANES, (s + 1) * LANES)
            tab = lambda r0, n=1, sl=sl: (tabr_ref[r0:r0 + n, sl], tabi_ref[r0:r0 + n, sl])
            xr, xi = ahead
            if s + 1 < nstrips:
                ahead = project(s + 1)
            ar, ai = tab(R["POW"])
            lr, li = [xr[0:8]], [xi[0:8]]
            for g in range(1, NG):
                nr, ni = _cmul_add(xr[8 * g:8 * g + 8], xi[8 * g:8 * g + 8], ar, ai, lr[-1], li[-1])
                lr.append(nr)
                li.append(ni)
            yr, yi = lr[-1], li[-1]
            for k, dist in enumerate((1, 2, 4)):
                kr, ki = tab(R["LA"] + k)
                keep = row8 >= dist
                yr, yi = _cmul_add(yr, yi, kr, ki, jnp.where(keep, pltpu.roll(yr, dist, 0), 0.0),
                                   jnp.where(keep, pltpu.roll(yi, dist, 0), 0.0))
            c0r, c0i = carr_ref[7:8, sl], cari_ref[7:8, sl]
            par, pai = tab(R["PA"], 8)
            er, ei = _cmul_add(yr, yi, par, pai, c0r, c0i)
            carr_ref[:, sl] = er
            cari_ref[:, sl] = ei
            first = row8 == 0
            inr = jnp.where(first, c0r, pltpu.roll(er, 1, 0))
            ini = jnp.where(first, c0i, pltpu.roll(ei, 1, 0))
            for g in range(NG):
                pr, pi = tab(R["POW"] + g)
                outr, outi = _cmul_add(lr[g], li[g], pr, pi, inr, ini)
                sr_ref[8 * g:8 * g + 8, sl] = outr
                si_ref[8 * g:8 * g + 8, sl] = outi
        y = (_dot(sr_ref[...].astype(BF16), cr_ref[...]) - _dot(si_ref[...].astype(BF16), ci_ref[...])
             + d_ref[...] * uv)
        y_ref[...] = y
        yg = jax.nn.gelu(y)
        gate = jax.nn.sigmoid(_dot(yg.astype(BF16), wg_ref[...]) + gb_ref[...])
        o_ref[...] = yg * gate

    cs = jax.ShapeDtypeStruct((L, SW), F32)
    ss = jax.ShapeDtypeStruct((L, NS), F32)
    return _pcall(
        body, name="s5_fwd", grid=(S["GT"], S["nc"]),
        in_specs=[S["chan"], S["bmat"], S["bmat"], S["tab"], S["tab"], S["cmat"], S["cmat"],
                  S["cvec"], S["gmat"], S["cvec"]],
        out_specs=[S["chan"], S["state"], S["state"], S["chan"]], out_shape=[cs, ss, ss, cs],
        scratch=[pltpu.VMEM((8, STATE_TILE), F32), pltpu.VMEM((8, STATE_TILE), F32)],
    )(u, sp["bsr"], sp["bsi"], sp["tabr"], sp["tabi"], sp["crT"], sp["ciT"], sp["d"], sp["wg"], sp["gb"])


def _s5_bwd(do, u, s_re, s_im, ypre, sp, T):
    L, SW = u.shape
    S = _s5_specs(L, SW, T, True)
    GT, nc, cidx = S["GT"], S["nc"], S["cidx"]
    NS = GT * STATE_TILE
    R = _s5_tab_rows(T)
    NG = R["NG"]
    T8 = T // 8

    def body(do_ref, u_ref, sr_ref, si_ref, hr_ref, hi_ref, y_ref, bsr_ref, bsi_ref, tabr_ref, tabi_ref,
             cr_ref, ci_ref, d_ref, wg_ref, gb_ref,
             du_ref, dbsr_ref, dbsi_ref, dcr_ref, dci_ref, dd_ref, dwg_ref, dgb_ref, dar_ref, dai_ref,
             carr_ref, cari_ref, lam_r, lam_i):
        c = pl.program_id(1)
        first = c == 0

        @pl.when(first)
        def _():
            carr_ref[...] = jnp.zeros_like(carr_ref)
            cari_ref[...] = jnp.zeros_like(cari_ref)

        u = u_ref[...]
        ub = u.astype(BF16)
        y = y_ref[...]
        yg, gelu_vjp = jax.vjp(jax.nn.gelu, y)
        ygb = yg.astype(BF16)
        gate = jax.nn.sigmoid(_dot(ygb, wg_ref[...]) + gb_ref[...])
        dout = do_ref[...]
        dt = dout * yg * gate * (1.0 - gate)
        dtb = dt.astype(BF16)
        dyg = dout * gate + _dot(dtb, wg_ref[...], NT)
        (dy,) = gelu_vjp(dyg)
        dyb = dy.astype(BF16)
        _accumulate(dwg_ref, _dot(ygb, dtb, TN), first)
        _accumulate(dgb_ref, jnp.sum(dt, axis=0, keepdims=True), first)
        _accumulate(dd_ref, jnp.sum(dy * u, axis=0, keepdims=True), first)
        _accumulate(dcr_ref, _dot(sr_ref[...].astype(BF16), dyb, TN), first)
        _accumulate(dci_ref, -_dot(si_ref[...].astype(BF16), dyb, TN), first)
        earliest = cidx(c) == 0
        row8 = lax.broadcasted_iota(jnp.int32, (8, LANES), 0)
        nstrips = STATE_TILE // LANES

        def project(s):
            sl = slice(s * LANES, (s + 1) * LANES)
            return _dot(dyb, cr_ref[sl, :], NT), -_dot(dyb, ci_ref[sl, :], NT)

        ahead = project(0)
        for s in range(nstrips):
            sl = slice(s * LANES, (s + 1) * LANES)
            tab = lambda r0, n=1, sl=sl: (tabr_ref[r0:r0 + n, sl], -tabi_ref[r0:r0 + n, sl])
            xr, xi = ahead
            if s + 1 < nstrips:
                ahead = project(s + 1)
            ar, ai = tab(R["POW"])
            lr, li = [None] * NG, [None] * NG
            lr[NG - 1], li[NG - 1] = xr[8 * (NG - 1):8 * NG], xi[8 * (NG - 1):8 * NG]
            for g in range(NG - 2, -1, -1):
                lr[g], li[g] = _cmul_add(xr[8 * g:8 * g + 8], xi[8 * g:8 * g + 8], ar, ai, lr[g + 1], li[g + 1])
            yr, yi = lr[0], li[0]
            for k, dist in enumerate((1, 2, 4)):
                kr, ki = tab(R["LA"] + k)
                keep = row8 < 8 - dist
                yr, yi = _cmul_add(yr, yi, kr, ki, jnp.where(keep, pltpu.roll(yr, 8 - dist, 0), 0.0),
                                   jnp.where(keep, pltpu.roll(yi, 8 - dist, 0), 0.0))
            c0r, c0i = carr_ref[0:1, sl], cari_ref[0:1, sl]
            par, pai = tab(R["PAR"], 8)
            er, ei = _cmul_add(yr, yi, par, pai, c0r, c0i)
            carr_ref[:, sl] = er
            cari_ref[:, sl] = ei
            last = row8 == 7
            inr = jnp.where(last, c0r, pltpu.roll(er, 7, 0))
            ini = jnp.where(last, c0i, pltpu.roll(ei, 7, 0))
            hr0 = jnp.where(earliest, 0.0, hr_ref[7:8, sl])
            hi0 = jnp.where(earliest, 0.0, hi_ref[7:8, sl])
            endr, endi = sr_ref[8 * (NG - 1):8 * NG, sl], si_ref[8 * (NG - 1):8 * NG, sl]
            pvr = jnp.where(row8 == 0, hr0, pltpu.roll(endr, 1, 0))
            pvi = jnp.where(row8 == 0, hi0, pltpu.roll(endi, 1, 0))
            accr = jnp.zeros((8, LANES), F32)
            acci = jnp.zeros((8, LANES), F32)
            for g in range(NG):
                pr, pi = tab(R["POW"] + NG - 1 - g)
                outr, outi = _cmul_add(lr[g], li[g], pr, pi, inr, ini)
                lam_r[8 * g:8 * g + 8, sl] = outr
                lam_i[8 * g:8 * g + 8, sl] = outi
                accr = accr + (outr * pvr + outi * pvi)
                acci = acci + (outi * pvr - outr * pvi)
                pvr, pvi = sr_ref[8 * g:8 * g + 8, sl], si_ref[8 * g:8 * g + 8, sl]

            @pl.when(first)
            def _():
                dar_ref[:, sl] = accr
                dai_ref[:, sl] = acci

            @pl.when(jnp.logical_not(first))
            def _():
                dar_ref[:, sl] += accr
                dai_ref[:, sl] += acci

        lrb = lam_r[...].astype(BF16)
        lib = lam_i[...].astype(BF16)
        _accumulate(dbsr_ref, _dot(ub, lrb, TN), first)
        _accumulate(dbsi_ref, _dot(ub, lib, TN), first)
        du_ref[...] = (dy * d_ref[...] + _dot(lrb, bsr_ref[...], NT) + _dot(lib, bsi_ref[...], NT)).astype(BF16)

    halo = pl.BlockSpec((8, STATE_TILE), lambda j, c: (jnp.maximum(cidx(c) * T8 - 1, 0), j))
    f = lambda *s: jax.ShapeDtypeStruct(s, F32)
    return _pcall(
        body, name="s5_bwd", grid=(GT, nc),
        in_specs=[S["chan"], S["chan"], S["state"], S["state"], halo, halo, S["chan"], S["bmat"], S["bmat"],
                  S["tab"], S["tab"], S["cmat"], S["cmat"], S["cvec"], S["gmat"], S["cvec"]],
        out_specs=[S["chan"], S["bmat"], S["bmat"], S["cmat"], S["cmat"], S["cvec"], S["gmat"], S["cvec"],
                   S["svec8"], S["svec8"]],
        out_shape=[jax.ShapeDtypeStruct((L, SW), BF16), f(GT, LANES, STATE_TILE), f(GT, LANES, STATE_TILE), f(GT, STATE_TILE, LANES),
                   f(GT, STATE_TILE, LANES), f(1, SW), f(GT, LANES, LANES), f(1, SW), f(8, NS), f(8, NS)],
        scratch=[pltpu.VMEM((8, STATE_TILE), F32), pltpu.VMEM((8, STATE_TILE), F32),
                 pltpu.VMEM((T, STATE_TILE), F32), pltpu.VMEM((T, STATE_TILE), F32)],
    )(do, u, s_re, s_im, s_re, s_im, ypre, sp["bsr"], sp["bsi"], sp["tabr"], sp["tabi"],
      sp["crT"], sp["ciT"], sp["d"], sp["wg"], sp["gb"])


def _shifted_rows(x, k, edge):
    n = x.shape[0]
    rolled = pltpu.roll(x, k % n, 0)
    row8 = lax.broadcasted_iota(jnp.int32, (8, x.shape[1]), 0)
    if k > 0:
        fixed = rolled[0:8]
        for r, e in enumerate(edge):
            fixed = jnp.where(row8 == r, e, fixed)
        return jnp.concatenate([fixed, rolled[8:]], axis=0) if n > 8 else fixed
    fixed = rolled[n - 8:n]
    for r, e in enumerate(edge):
        fixed = jnp.where(row8 == 8 + k + r, e, fixed)
    return jnp.concatenate([rolled[:n - 8], fixed], axis=0) if n > 8 else fixed


def _conv_taps(xc, h6, h7):
    return _shifted_rows(xc, 1, [h7]), _shifted_rows(xc, 2, [h6, h7])


def _conv_halves(up_ref, halo_ref, w_ref, b_ref, tm, FS, first):
    outs, taps = [], []
    for s in (0, 1):
        xc = up_ref[s]
        h6 = jnp.where(first, 0.0, halo_ref[s, 6:7, :])
        h7 = jnp.where(first, 0.0, halo_ref[s, 7:8, :])
        x1, x2 = _conv_taps(xc, h6, h7)
        outs.append(b_ref[s] + x2 * w_ref[s, 0:1, :] + x1 * w_ref[s, 1:2, :] + xc * w_ref[s, 2:3, :])
        taps.append((x2, x1, xc))
    return outs, taps


def _convglu_specs(L, FS, tm, rev=False):
    t8 = tm // 8
    nt = L // tm
    tile = (lambda i: nt - 1 - i) if rev else (lambda i: i)
    return dict(
        up=pl.BlockSpec((2, None, tm, FS), lambda j, i: (0, j, tile(i), 0)),
        halo=pl.BlockSpec((2, None, 8, FS), lambda j, i: (0, j, jnp.maximum(tile(i) * t8 - 1, 0), 0)),
        w=pl.BlockSpec((2, None, 3, FS), lambda j, i: (0, j, 0, 0)),
        b=pl.BlockSpec((2, None, 1, FS), lambda j, i: (0, j, 0, 0)),
        act=pl.BlockSpec((None, tm, FS), lambda j, i: (j, tile(i), 0)),
    )


def _convglu_fwd(up, w, b):
    _, NSH, L, FS = up.shape
    tm = _row_tile(L)
    S = _convglu_specs(L, FS, tm)

    def body(up_ref, halo_ref, w_ref, b_ref, act_ref):
        (val, gate), _ = _conv_halves(up_ref, halo_ref, w_ref, b_ref, tm, FS, pl.program_id(1) == 0)
        act_ref[...] = _glu_math(val, gate).astype(BF16)

    return _pcall(body, name="convglu_fwd", grid=(NSH, L // tm), in_specs=[S["up"], S["halo"], S["w"], S["b"]],
                  out_specs=S["act"], out_shape=jax.ShapeDtypeStruct((NSH, L, FS), BF16))(up, up, w, b)


def _convglu_bwd(up, dact, w, b):
    _, NSH, L, FS = up.shape
    tm = _row_tile(L)
    nt = L // tm
    S = _convglu_specs(L, FS, tm, rev=True)

    def body(up_ref, halo_ref, w_ref, b_ref, dact_ref, dup_ref, dw_ref, db_ref, after_scr):
        step = pl.program_id(1)
        first = step == 0
        (val, gate), taps = _conv_halves(up_ref, halo_ref, w_ref, b_ref, tm, FS, step == nt - 1)
        _, vjp = jax.vjp(_glu_math, val, gate)
        dcs = vjp(dact_ref[...])
        for s in (0, 1):
            dc = dcs[s]
            n0 = jnp.where(first, 0.0, after_scr[s, 0:1, :])
            n1 = jnp.where(first, 0.0, after_scr[s, 1:2, :])
            x1 = _shifted_rows(dc, -1, [n0])
            x2 = _shifted_rows(dc, -2, [n0, n1])
            dup_ref[s] = (dc * w_ref[s, 2:3, :] + x1 * w_ref[s, 1:2, :] + x2 * w_ref[s, 0:1, :]).astype(BF16)
            after_scr[s] = dc[0:8]
            sums = [jnp.sum(dc * t, axis=0, keepdims=True) for t in taps[s]]
            dbs = jnp.sum(dc, axis=0, keepdims=True)

            @pl.when(first)
            def _():
                for t in range(3):
                    dw_ref[s, t:t + 1, :] = sums[t]
                db_ref[s] = dbs

            @pl.when(jnp.logical_not(first))
            def _():
                for t in range(3):
                    dw_ref[s, t:t + 1, :] += sums[t]
                db_ref[s] += dbs

    f = lambda *s: jax.ShapeDtypeStruct(s, F32)
    return _pcall(body, name="convglu_bwd", grid=(NSH, nt),
                  in_specs=[S["up"], S["halo"], S["w"], S["b"], S["act"]],
                  out_specs=[S["up"], S["w"], S["b"]],
                  out_shape=[jax.ShapeDtypeStruct((2, NSH, L, FS), BF16), f(2, NSH, 3, FS), f(2, NSH, 1, FS)],
                  scratch=[pltpu.VMEM((2, 8, FS), F32)])(up, up, w, b, dact)


def _loss_head(y, target):
    L, D = y.shape
    tm = _row_tile(L)

    def body(y_ref, t_ref, loss_ref, dy_ref):
        err = y_ref[...] - t_ref[...]
        dy_ref[...] = err / D
        part = 0.5 * jnp.sum(jnp.mean(err * err, axis=-1, keepdims=True), axis=0, keepdims=True)
        _accumulate(loss_ref, jnp.broadcast_to(part, (1, LANES)), pl.program_id(0) == 0)

    row = pl.BlockSpec((tm, D), lambda i: (i, 0))
    vec = pl.BlockSpec((1, LANES), lambda i: (0, 0))
    return _pcall(body, name="loss_head", grid=(L // tm,), in_specs=[row, row], out_specs=[vec, row],
                  out_shape=[jax.ShapeDtypeStruct((1, LANES), F32), jax.ShapeDtypeStruct((L, D), F32)])(y, target)


def _ada_fwd(c_all, ada_w, ada_b):
    NL, D, NC = ada_w.shape
    NB = c_all.shape[0]

    def body(c_ref, w_ref, b_ref, o_ref):
        cact = jax.nn.silu(c_ref[...])
        o_ref[...] = _dot(cact.astype(BF16), w_ref[...].astype(BF16)) + b_ref[...]

    return _pcall(body, name="ada_fwd", grid=(NL,),
                  in_specs=[pl.BlockSpec((NB, D), lambda l: (0, 0)), pl.BlockSpec((None, D, NC), lambda l: (l, 0, 0)),
                            pl.BlockSpec((None, 1, NC), lambda l: (l, 0, 0))],
                  out_specs=pl.BlockSpec((None, NB, NC), lambda l: (l, 0, 0)),
                  out_shape=jax.ShapeDtypeStruct((NL, NB, NC), F32))(c_all, ada_w, ada_b)


def _ada_bwd(c_all_t, dmod):
    D, NB = c_all_t.shape
    NL, _, NC = dmod.shape

    def body(c_ref, d_ref, o_ref):
        cact = jax.nn.silu(c_ref[...]).astype(BF16).astype(F32)
        o_ref[...] = _dot(cact, d_ref[...].astype(BF16).astype(F32))

    return _pcall(body, name="ada_bwd", grid=(NL,),
                  in_specs=[pl.BlockSpec((D, NB), lambda l: (0, 0)), pl.BlockSpec((None, NB, NC), lambda l: (l, 0, 0))],
                  out_specs=pl.BlockSpec((None, D, NC), lambda l: (l, 0, 0)),
                  out_shape=jax.ShapeDtypeStruct((NL, D, NC), F32))(c_all_t, dmod)


def _adamw(parts, w, m, v):
    NL, P, R, C = parts.shape
    tr = R if R <= 512 else max(t for t in range(16, 513, 16) if R % t == 0)

    def body(p_ref, w_ref, m_ref, v_ref, g_ref, d_ref, nm_ref, nv_ref):
        g = p_ref[0].astype(F32)
        for k in range(1, P):
            g = g + p_ref[k].astype(F32)
        m2 = ADAM_B1 * m_ref[...] + (1.0 - ADAM_B1) * g
        v2 = ADAM_B2 * v_ref[...] + (1.0 - ADAM_B2) * jnp.square(g)
        m_hat = m2 / (1.0 - ADAM_B1 ** ADAM_STEP)
        v_hat = v2 / (1.0 - ADAM_B2 ** ADAM_STEP)
        g_ref[...] = g
        d_ref[...] = -ADAM_LR * (m_hat / (jnp.sqrt(v_hat) + ADAM_EPS) + ADAM_WD * w_ref[...])
        nm_ref[...] = m2
        nv_ref[...] = v2

    pspec = pl.BlockSpec((None, P, tr, C), lambda l, i: (l, 0, i, 0))
    wspec = pl.BlockSpec((None, tr, C), lambda l, i: (l, i, 0))
    shp = jax.ShapeDtypeStruct((NL, R, C), F32)
    return _pcall(body, name="adamw", grid=(NL, R // tr), in_specs=[pspec, wspec, wspec, wspec],
                  out_specs=[wspec] * 4, out_shape=[shp] * 4)(parts, w, m, v)


def _block_diag(blocks):
    *lead, g, r, c = blocks.shape
    eye = jnp.eye(g, dtype=bool)[:, None, :, None]
    full = jnp.where(eye, blocks[..., :, :, None, :], 0.0)
    return full.reshape(*lead, g * r, g * c)


def _block_diag_extract(m, r, c):
    g = GROUPS_PER_TILE
    m5 = m.reshape(*m.shape[:-2], g, r, g, c)
    eye = jnp.eye(g, dtype=bool)[:, None, :, None]
    return jnp.sum(jnp.where(eye, m5, 0.0), axis=-2)


def kernel(x, c, ada_w, ada_b, norm1_g, w_in, q_norm_g, k_norm_g, ssm_a_re, ssm_a_im, ssm_log_dt, ssm_b_re, ssm_b_im, ssm_c_re, ssm_c_im, ssm_d, glu_w, glu_b, attn_out_g, ssm_out_g, w_out, norm2_g, ffn_w_up, ffn_conv_w, ffn_conv_b, ffn_w_down, loss_target, m_ada_w, m_ada_b, m_norm1_g, m_w_in, m_q_norm_g, m_k_norm_g, m_ssm_a_re, m_ssm_a_im, m_ssm_log_dt, m_ssm_b_re, m_ssm_b_im, m_ssm_c_re, m_ssm_c_im, m_ssm_d, m_glu_w, m_glu_b, m_attn_out_g, m_ssm_out_g, m_w_out, m_norm2_g, m_ffn_w_up, m_ffn_conv_w, m_ffn_conv_b, m_ffn_w_down, v_ada_w, v_ada_b, v_norm1_g, v_w_in, v_q_norm_g, v_k_norm_g, v_ssm_a_re, v_ssm_a_im, v_ssm_log_dt, v_ssm_b_re, v_ssm_b_im, v_ssm_c_re, v_ssm_c_im, v_ssm_d, v_glu_w, v_glu_b, v_attn_out_g, v_ssm_out_g, v_w_out, v_norm2_g, v_ffn_w_up, v_ffn_conv_w, v_ffn_conv_b, v_ffn_w_down):
    weights = dict(ada_w=ada_w, ada_b=ada_b, norm1_g=norm1_g, w_in=w_in, q_norm_g=q_norm_g, k_norm_g=k_norm_g,
                   ssm_a_re=ssm_a_re, ssm_a_im=ssm_a_im, ssm_log_dt=ssm_log_dt, ssm_b_re=ssm_b_re,
                   ssm_b_im=ssm_b_im, ssm_c_re=ssm_c_re, ssm_c_im=ssm_c_im, ssm_d=ssm_d, glu_w=glu_w, glu_b=glu_b,
                   attn_out_g=attn_out_g, ssm_out_g=ssm_out_g, w_out=w_out, norm2_g=norm2_g, ffn_w_up=ffn_w_up,
                   ffn_conv_w=ffn_conv_w, ffn_conv_b=ffn_conv_b, ffn_w_down=ffn_w_down)
    mom_m = dict(ada_w=m_ada_w, ada_b=m_ada_b, norm1_g=m_norm1_g, w_in=m_w_in, q_norm_g=m_q_norm_g,
                 k_norm_g=m_k_norm_g, ssm_a_re=m_ssm_a_re, ssm_a_im=m_ssm_a_im, ssm_log_dt=m_ssm_log_dt,
                 ssm_b_re=m_ssm_b_re, ssm_b_im=m_ssm_b_im, ssm_c_re=m_ssm_c_re, ssm_c_im=m_ssm_c_im, ssm_d=m_ssm_d,
                 glu_w=m_glu_w, glu_b=m_glu_b, attn_out_g=m_attn_out_g, ssm_out_g=m_ssm_out_g, w_out=m_w_out,
                 norm2_g=m_norm2_g, ffn_w_up=m_ffn_w_up, ffn_conv_w=m_ffn_conv_w, ffn_conv_b=m_ffn_conv_b,
                 ffn_w_down=m_ffn_w_down)
    mom_v = dict(ada_w=v_ada_w, ada_b=v_ada_b, norm1_g=v_norm1_g, w_in=v_w_in, q_norm_g=v_q_norm_g,
                 k_norm_g=v_k_norm_g, ssm_a_re=v_ssm_a_re, ssm_a_im=v_ssm_a_im, ssm_log_dt=v_ssm_log_dt,
                 ssm_b_re=v_ssm_b_re, ssm_b_im=v_ssm_b_im, ssm_c_re=v_ssm_c_re, ssm_c_im=v_ssm_c_im, ssm_d=v_ssm_d,
                 glu_w=v_glu_w, glu_b=v_glu_b, attn_out_g=v_attn_out_g, ssm_out_g=v_ssm_out_g, w_out=v_w_out,
                 norm2_g=v_norm2_g, ffn_w_up=v_ffn_w_up, ffn_conv_w=v_ffn_conv_w, ffn_conv_b=v_ffn_conv_b,
                 ffn_w_down=v_ffn_w_down)
    names = list(weights)
    big = ("ada_w", "w_in", "w_out", "ffn_w_up", "ffn_conv_w", "ffn_w_down")
    small = [n for n in names if n not in big]

    x = x[0]
    target = loss_target[0]
    L, D = x.shape
    NL = ada_w.shape[0]
    AW = D // 2
    SW = D - AW
    NH = AW // HEAD_DIM
    HP = AW // LANES
    G = SW // SSM_GROUP
    GT = SW // LANES
    NS = G * SSM_STATE
    NIN = w_in.shape[-1]
    FS = ffn_w_up.shape[-1]
    NSH = NDEV // 2
    NCA = ada_w.shape[-1]
    ROWS_OUT = w_out.shape[1]
    ROWS_DOWN = ffn_w_down.shape[1]
    B_ATT = min(L, 256)
    T_S5 = min(L, 1024)
    tm = _row_tile(L, 1024)
    ts = _row_tile(L, 512)
    ts2 = _row_tile(L, 256)
    me = _my_index()

    cpad = jnp.reshape(c, (D // LANES, LANES))
    c_all = _gather_small(cpad, "gather_c")
    c_all = c_all.reshape(NDEV, D)
    shards = [w_in.astype(BF16), w_out.astype(BF16), ffn_w_up.astype(BF16), ffn_conv_w, ffn_w_down.astype(BF16)]
    (first_w_in,) = _gather_shards([shards[0][0:1]], "gather_weights")

    def set_w_in(lp, w_in_g):
        lp.update(w_in_cols=jnp.swapaxes(w_in_g, 0, 1).reshape(D, NDEV * NIN))

    def set_other_weights(lp, gathered):
        w_out_g, w_up_g, conv_w_g, w_down_g = gathered
        lp.update(w_out=w_out_g.reshape(D, D), w_up=w_up_g, conv_w=conv_w_g.reshape(2, NSH, 3, FS),
                  w_down=w_down_g.reshape(NSH, 2 * ROWS_DOWN, D))

    conv_b_g = ffn_conv_b.reshape(NL, 2, NSH, 1, FS)

    ada_b_mine = lax.dynamic_slice_in_dim(ada_b, me * NCA, NCA, axis=1).reshape(NL, 1, NCA)
    mod_part = _ada_fwd(c_all, ada_w, ada_b_mine)
    mod_all = _gather_small(mod_part, "gather_mod")
    mod = lax.dynamic_index_in_dim(mod_all, me, axis=2, keepdims=False)
    mod = jnp.transpose(mod, (1, 0, 2)).reshape(NL, N_MOD, 1, D)

    row = lambda a: a.reshape(NL, 1, NS)
    ar_row, ai_row = row(ssm_a_re), row(ssm_a_im)
    ldt_row = row(jnp.broadcast_to(ssm_log_dt[:, :, None], (NL, G, SSM_STATE)))
    tiles = lambda a: a.reshape((NL, GT, GROUPS_PER_TILE) + a.shape[2:])
    braw_r = _block_diag(jnp.swapaxes(tiles(ssm_b_re), -1, -2))
    braw_i = _block_diag(jnp.swapaxes(tiles(ssm_b_im), -1, -2))
    crT = _block_diag(jnp.swapaxes(tiles(ssm_c_re), -1, -2)).astype(BF16)
    ciT = _block_diag(jnp.swapaxes(tiles(ssm_c_im), -1, -2)).astype(BF16)
    wg = _block_diag(tiles(glu_w)).astype(BF16)
    abr, abi, tabr, tabi, bsr, bsi = _s5_prep(ar_row, ai_row, ldt_row, braw_r, braw_i, T_S5)
    s5p = dict(bsr=bsr, bsi=bsi, tabr=tabr, tabi=tabi, crT=crT, ciT=ciT,
               d=ssm_d.reshape(NL, 1, SW), wg=wg, gb=glu_b.reshape(NL, 1, SW))

    gqk = jnp.concatenate([jnp.tile(q_norm_g, (1, NH)), jnp.tile(k_norm_g, (1, NH))], axis=1).reshape(NL, 1, 2 * AW)
    layer_params = dict(
        mod=mod, norm1_g=norm1_g.reshape(NL, 1, D), norm2_g=norm2_g.reshape(NL, 1, D), gqk=gqk,
        ga=attn_out_g.reshape(NL, 1, AW), gs=ssm_out_g.reshape(NL, 1, SW),
        conv_b=conv_b_g, s5=s5p)


    tn = min(D, 512)

    def resid_epilogue(acc, xres, gate):
        return acc, xres + gate * acc

    def layer_fwd(xin, lp, ride):
        sh1, sc1, g1, sh2, sc2, g2 = (lp["mod"][k] for k in range(N_MOD))
        h = _lnmod_fwd(xin, lp["norm1_g"], sh1, sc1)
        p = _matmul(
            "mm_in", h, lp["w_in_cols"], dims=NN, grid=(L // ts,),
            a_spec=pl.BlockSpec((ts, D), lambda i: (i, 0)), b_spec=pl.BlockSpec((D, NDEV * NIN), lambda i: (0, 0)),
            out_shape=jax.ShapeDtypeStruct((L, NDEV * NIN), F32),
            out_specs=pl.BlockSpec((ts, NDEV * NIN), lambda i: (i, 0)))
        qk = _qknorm_fwd(p, lp["gqk"], AW)
        o_attn, tot, gathered = _attn_fwd(qk, p, AW, B_ATT, ride)
        set_other_weights(lp, gathered[:4])
        next_gathered = gathered[4:]
        u_st = _to_streams(p[:, 3 * AW:], T_S5)
        o_st, s_re, s_im, ypre = _s5_fwd(u_st, lp["s5"], T_S5)
        o_ssm = _from_streams(o_st, T_S5)
        o = _outnorm_fwd(o_attn, o_ssm, lp["ga"], lp["gs"])
        a1, x_mid = _matmul(
            "mm_out", o, lp["w_out"], dims=NN, grid=(L // tm, D // tn),
            a_spec=pl.BlockSpec((tm, D), lambda i, j: (i, 0)), b_spec=pl.BlockSpec((D, tn), lambda i, j: (0, j)),
            extra=(xin, g1), extra_specs=(pl.BlockSpec((tm, tn), lambda i, j: (i, j)),
                                          pl.BlockSpec((1, tn), lambda i, j: (0, j))),
            epilogue=resid_epilogue,
            out_shape=[jax.ShapeDtypeStruct((L, D), F32)] * 2,
            out_specs=[pl.BlockSpec((tm, tn), lambda i, j: (i, j))] * 2)
        h2 = _lnmod_fwd(x_mid, lp["norm2_g"], sh2, sc2)
        up = _matmul(
            "mm_up", h2, lp["w_up"], dims=NN, grid=(L // tm, NDEV),
            a_spec=pl.BlockSpec((tm, D), lambda i, j: (i, 0)), b_spec=pl.BlockSpec((None, D, FS), lambda i, j: (j, 0, 0)),
            out_shape=jax.ShapeDtypeStruct((NDEV, L, FS), F32),
            out_specs=pl.BlockSpec((None, tm, FS), lambda i, j: (j, i, 0)))
        up = up.reshape(2, NSH, L, FS)
        act = _convglu_fwd(up, lp["conv_w"], lp["conv_b"])
        a2, x_out = _matmul(
            "mm_down", act, lp["w_down"], dims=NN, grid=(L // ts, D // tn),
            a_spec=pl.BlockSpec((NSH, ts, FS), lambda i, j: (0, i, 0)),
            b_spec=pl.BlockSpec((NSH, FS, tn), lambda i, j: (0, 0, j)),
            extra=(x_mid, g2), extra_specs=(pl.BlockSpec((ts, tn), lambda i, j: (i, j)),
                                            pl.BlockSpec((1, tn), lambda i, j: (0, j))),
            epilogue=resid_epilogue,
            out_shape=[jax.ShapeDtypeStruct((L, D), F32)] * 2,
            out_specs=[pl.BlockSpec((ts, tn), lambda i, j: (i, j))] * 2)
        res = dict(x=xin, h=h, p=p, qk=qk, tot=tot, o_attn=o_attn, o_ssm=o_ssm, u_st=u_st, s_re=s_re, s_im=s_im, ypre=ypre,
                   o=o, a1=a1, x_mid=x_mid, h2=h2, up=up, act=act, a2=a2)
        return x_out, res, next_gathered

    per_layer = [jax.tree.map(lambda a: a[l], layer_params) for l in range(NL)]
    y, residuals = x, []
    set_w_in(per_layer[0], first_w_in[0])
    for l in range(NL):
        ride = [s[l] for s in shards[1:]] + ([shards[0][l + 1]] if l + 1 < NL else [])
        y, res, next_w_in = layer_fwd(y, per_layer[l], ride)
        if next_w_in:
            set_w_in(per_layer[l + 1], next_w_in[0])
        residuals.append(res)

    loss_row, dy = _loss_head(y, target)
    loss = lax.psum(loss_row[0, 0], ("x", "y", "c"))

    def layer_bwd(dx, args, above):
        lp, r = args
        sh1, sc1, g1, sh2, sc2, g2 = (lp["mod"][k] for k in range(N_MOD))
        da2, dg2 = _gate_bwd(dx, r["a2"], g2)
        dact = _matmul(
            "mm_dact", da2, lp["w_down"], dims=NT, grid=(L // tm, NSH),
            a_spec=pl.BlockSpec((tm, D), lambda i, j: (i, 0)), b_spec=pl.BlockSpec((None, FS, D), lambda i, j: (j, 0, 0)),
            out_shape=jax.ShapeDtypeStruct((NSH, L, FS), F32),
            out_specs=pl.BlockSpec((None, tm, FS), lambda i, j: (j, i, 0)))
        dw_down = _matmul(
            "mm_dw_down", r["act"], da2, dims=TN, grid=(NSH, D // tn),
            a_spec=pl.BlockSpec((None, L, FS), lambda j, n: (j, 0, 0)),
            b_spec=pl.BlockSpec((L, tn), lambda j, n: (0, n)),
            out_shape=jax.ShapeDtypeStruct((NSH, FS, D), BF16),
            out_specs=pl.BlockSpec((None, FS, tn), lambda j, n: (j, 0, n)))
        dup4, dcw, dcb = _convglu_bwd(r["up"], dact, lp["conv_w"], lp["conv_b"])
        dup = dup4.reshape(NDEV, L, FS)
        dh2 = _matmul(
            "mm_dh2", dup, lp["w_up"], dims=NT, grid=(L // ts2,),
            a_spec=pl.BlockSpec((NDEV, ts2, FS), lambda i: (0, i, 0)),
            b_spec=pl.BlockSpec((NDEV, D, FS), lambda i: (0, 0, 0)),
            out_shape=jax.ShapeDtypeStruct((L, D), F32), out_specs=pl.BlockSpec((ts2, D), lambda i: (i, 0)))
        dw_up = _matmul(
            "mm_dw_up", r["h2"], dup, dims=TN, grid=(NDEV,),
            a_spec=pl.BlockSpec((L, D), lambda j: (0, 0)), b_spec=pl.BlockSpec((None, L, FS), lambda j: (j, 0, 0)),
            out_shape=jax.ShapeDtypeStruct((NDEV, D, FS), BF16),
            out_specs=pl.BlockSpec((None, D, FS), lambda j: (j, 0, 0)))
        dxm, dn2, dsh2, dsc2 = _lnmod_bwd(dh2, r["x_mid"], lp["norm2_g"], sh2, sc2, dx)
        da1, dg1 = _gate_bwd(dxm, r["a1"], g1)
        do = _matmul(
            "mm_do", da1, lp["w_out"], dims=NT, grid=(L // tm, D // tn),
            a_spec=pl.BlockSpec((tm, D), lambda i, j: (i, 0)), b_spec=pl.BlockSpec((tn, D), lambda i, j: (j, 0)),
            out_shape=jax.ShapeDtypeStruct((L, D), F32), out_specs=pl.BlockSpec((tm, tn), lambda i, j: (i, j)))
        dw_out = _matmul(
            "mm_dw_out", r["o"], da1, dims=TN, grid=(D // tn, D // tn),
            a_spec=pl.BlockSpec((L, tn), lambda m, n: (0, m)), b_spec=pl.BlockSpec((L, tn), lambda m, n: (0, n)),
            out_shape=jax.ShapeDtypeStruct((D, D), BF16), out_specs=pl.BlockSpec((tn, tn), lambda m, n: (m, n)))
        doa, dos, dga, dgs = _outnorm_bwd(do, r["o_attn"], r["o_ssm"], lp["ga"], lp["gs"])
        (du_st, dbsr, dbsi, dcr, dci, dd, dwg, dgb, dabr, dabi) = _s5_bwd(
            _to_streams(dos, T_S5), r["u_st"], r["s_re"], r["s_im"], r["ypre"], lp["s5"], T_S5)
        du = _from_streams(du_st, T_S5)
        ride = [dw_out.reshape(NDEV, ROWS_OUT, D), dw_up, dcw.reshape(NDEV, 3, FS),
                dw_down.reshape(NDEV, ROWS_DOWN, D)] + above
        dq, dk, dv, received = _attn_bwd(r["qk"], r["p"], doa, r["tot"], AW, B_ATT, ride)
        dqk, dgqk = _qknorm_bwd(jnp.concatenate([dq, dk], axis=1), r["p"], lp["gqk"], AW)
        dp = jnp.concatenate([dqk, dv.astype(BF16), du], axis=1)
        dh = _matmul(
            "mm_dh", dp, lp["w_in_cols"], dims=NT, grid=(L // tm,),
            a_spec=pl.BlockSpec((tm, NDEV * NIN), lambda i: (i, 0)), b_spec=pl.BlockSpec((D, NDEV * NIN), lambda i: (0, 0)),
            out_shape=jax.ShapeDtypeStruct((L, D), F32), out_specs=pl.BlockSpec((tm, D), lambda i: (i, 0)))
        dw_in = _matmul(
            "mm_dw_in", r["h"], dp, dims=TN, grid=(NDEV,),
            a_spec=pl.BlockSpec((L, D), lambda j: (0, 0)), b_spec=pl.BlockSpec((L, NIN), lambda j: (0, j)),
            out_shape=jax.ShapeDtypeStruct((NDEV, D, NIN), BF16),
            out_specs=pl.BlockSpec((None, D, NIN), lambda j: (j, 0, 0)))
        dx0, dn1, dsh1, dsc1 = _lnmod_bwd(dh, r["x"], lp["norm1_g"], sh1, sc1, dxm)
        grads = dict(
            dmod=jnp.concatenate([dsh1, dsc1, dg1, dsh2, dsc2, dg2], axis=1), dn1=dn1, dn2=dn2, dgqk=dgqk,
            dga=dga, dgs=dgs, dbsr=dbsr, dbsi=dbsi, dcr=dcr, dci=dci, dd=dd, dwg=dwg, dgb=dgb, dabr=dabr, dabi=dabi,
            dcb=dcb)
        return dx0, grads, dw_in, received

    grad_x, layer_grads, parts, above = dy, [None] * NL, [[None] * 5 for _ in range(NL)], []
    for l in reversed(range(NL)):
        grad_x, layer_grads[l], dw_in_l, received = layer_bwd(grad_x, (per_layer[l], residuals[l]), above)
        parts[l][1:] = received[:4]
        if above:
            parts[l + 1][0] = received[4]
        above = [dw_in_l]
    (parts[0][0],) = _scatter_direct(above, "scatter_weight_grads")
    gr = jax.tree.map(lambda *a: jnp.stack(a), *layer_grads)

    dar, dai, dldt, dbr_bd, dbi_bd = _s5_prep_bwd(ar_row, ai_row, ldt_row, braw_r, braw_i,
                                                  gr["dabr"], gr["dabi"], gr["dbsr"], gr["dbsi"])
    unt = lambda a: a.reshape((NL, G) + a.shape[3:])
    local = dict(
        ada_b=gr["dmod"].reshape(NL, N_MOD * D),
        norm1_g=gr["dn1"].reshape(NL, D), norm2_g=gr["dn2"].reshape(NL, D),
        q_norm_g=gr["dgqk"].reshape(NL, 2, NH, HEAD_DIM)[:, 0].sum(axis=1),
        k_norm_g=gr["dgqk"].reshape(NL, 2, NH, HEAD_DIM)[:, 1].sum(axis=1),
        ssm_a_re=dar.reshape(NL, G, SSM_STATE), ssm_a_im=dai.reshape(NL, G, SSM_STATE),
        ssm_log_dt=dldt.reshape(NL, G, SSM_STATE).sum(axis=-1),
        ssm_b_re=jnp.swapaxes(unt(_block_diag_extract(dbr_bd, SSM_GROUP, SSM_STATE)), -1, -2),
        ssm_b_im=jnp.swapaxes(unt(_block_diag_extract(dbi_bd, SSM_GROUP, SSM_STATE)), -1, -2),
        ssm_c_re=jnp.swapaxes(unt(_block_diag_extract(gr["dcr"], SSM_STATE, SSM_GROUP)), -1, -2),
        ssm_c_im=jnp.swapaxes(unt(_block_diag_extract(gr["dci"], SSM_STATE, SSM_GROUP)), -1, -2),
        ssm_d=gr["dd"].reshape(NL, G, SSM_GROUP),
        glu_w=unt(_block_diag_extract(gr["dwg"], SSM_GROUP, SSM_GROUP)),
        glu_b=gr["dgb"].reshape(NL, G, SSM_GROUP),
        attn_out_g=gr["dga"].reshape(NL, AW), ssm_out_g=gr["dgs"].reshape(NL, SW),
        ffn_conv_b=gr["dcb"].reshape(NL, 2 * NSH * FS),
    )

    def pack(tree):
        flat = jnp.concatenate([tree[n].reshape(-1) for n in small])
        pad = (-flat.shape[0]) % (512 * LANES)
        return jnp.pad(flat, (0, pad)).reshape(1, -1, LANES)

    (small_parts,) = _gather_shards([pack(local).astype(BF16)], "gather_small_grads")
    sg, sd, sm, sv = _adamw(small_parts, pack(weights), pack(mom_m), pack(mom_v))

    def unpack(buf):
        flat = buf.reshape(-1)
        out, off = {}, 0
        for n in small:
            size = weights[n].size
            out[n] = flat[off:off + size].reshape(weights[n].shape)
            off += size
        return out

    ug, ud, um, uv = unpack(sg), unpack(sd), unpack(sm), unpack(sv)
    results = {n: (ug[n], ud[n], um[n], uv[n]) for n in small}

    dmod_all = _gather_small(gr["dmod"].reshape(NL, N_MOD * D), "gather_dmod")
    dmod_mine = lax.dynamic_slice_in_dim(dmod_all, me * NCA, NCA, axis=2)
    d_ada_w = _ada_bwd(jnp.transpose(c_all), jnp.transpose(dmod_mine, (1, 0, 2)))
    results["ada_w"] = tuple(_adamw(d_ada_w[:, None], ada_w, m_ada_w, v_ada_w))

    for a, n in enumerate(("w_in", "w_out", "ffn_w_up", "ffn_conv_w", "ffn_w_down")):
        results[n] = tuple(_adamw(jnp.stack([parts[l][a] for l in range(NL)]), weights[n], mom_m[n], mom_v[n]))

    out = [loss, grad_x[None]]
    for k in range(4):
        out.extend(results[n][k] for n in names)
    return tuple(out)
```

```python
import jax
import jax.numpy as jnp
from jax import lax
from jax.experimental import pallas as pl
from jax.experimental.pallas import tpu as pltpu

F32 = jnp.float32
BF16 = jnp.bfloat16
NDEV = 8
LANES = 128
HEAD_DIM = 64
SSM_GROUP = 16
SSM_STATE = 64
GROUPS_PER_TILE = LANES // SSM_GROUP
STATE_TILE = GROUPS_PER_TILE * SSM_STATE
N_MOD = 6
ATTN_STRIP = 32
EPS = 1e-6
ADAM_LR, ADAM_B1, ADAM_B2, ADAM_EPS, ADAM_WD, ADAM_STEP = 0.001, 0.9, 0.999, 1e-08, 0.01, 10
VMEM_LIMIT = 48 * 1024 * 1024
MESH_IDS = pl.DeviceIdType.MESH

NN = (((1,), (0,)), ((), ()))
NT = (((1,), (1,)), ((), ()))
TN = (((0,), (0,)), ((), ()))


def _dot(a, b, dims=NN):
    return lax.dot_general(a, b, dims, preferred_element_type=F32)


def _pcall(body, *, name, out_shape, in_specs, out_specs, grid=(), scratch=()):
    return pl.pallas_call(
        body, name=name, grid=grid, in_specs=in_specs, out_specs=out_specs, out_shape=out_shape,
        scratch_shapes=list(scratch),
        compiler_params=pltpu.CompilerParams(vmem_limit_bytes=VMEM_LIMIT))


def _row_tile(n, want=512):
    t = min(n, want)
    assert n % t == 0
    return t


def _my_index():
    return 4 * lax.axis_index("x") + 2 * lax.axis_index("y") + lax.axis_index("c")


HBM_SPEC = pl.BlockSpec(memory_space=pltpu.HBM)


def _mesh_place():
    x, y, c = lax.axis_index("x"), lax.axis_index("y"), lax.axis_index("c")
    chips = [(1 - x, y), (x, 1 - y), (1 - x, 1 - y)]
    return x, y, c, chips


def _gather_small(arr, name):
    def body(in_ref, out_ref, send_sems, recv_sems, local_sem):
        x, y, c, _ = _mesh_place()
        me = 4 * x + 2 * y + c
        own = pltpu.make_async_copy(in_ref, out_ref.at[me], local_sem)
        own.start()
        sends, recvs = [], []
        for k in range(1, NDEV):
            px = 1 - x if k & 4 else x
            py = 1 - y if k & 2 else y
            pc = 1 - c if k & 1 else c
            common = dict(send_sem=send_sems.at[k - 1], recv_sem=recv_sems.at[k - 1],
                          device_id=(px, py, pc), device_id_type=MESH_IDS)
            snd = pltpu.make_async_remote_copy(src_ref=in_ref, dst_ref=out_ref.at[me], **common)
            snd.start()
            sends.append(snd)
            recvs.append(pltpu.make_async_remote_copy(
                src_ref=in_ref, dst_ref=out_ref.at[4 * px + 2 * py + pc], **common))
        for r in recvs:
            r.wait_recv()
        for s in sends:
            s.wait_send()
        own.wait()

    return pl.pallas_call(
        body, name=name, out_shape=jax.ShapeDtypeStruct((NDEV,) + arr.shape, arr.dtype),
        in_specs=[HBM_SPEC], out_specs=HBM_SPEC,
        scratch_shapes=[pltpu.SemaphoreType.DMA((NDEV - 1,)), pltpu.SemaphoreType.DMA((NDEV - 1,)),
                        pltpu.SemaphoreType.DMA(())],
    )(arr)


def _all_to_all(srcs, dsts, send_sems, recv_sems, scatter):
    x, y, c, _ = _mesh_place()
    me = 4 * x + 2 * y + c
    pairs = []
    for k in range(1, NDEV):
        px = 1 - x if k & 4 else x
        py = 1 - y if k & 2 else y
        pc = 1 - c if k & 1 else c
        peer = 4 * px + 2 * py + pc
        for a in range(len(srcs)):
            src = srcs[a].at[peer] if scatter else srcs[a]
            common = dict(send_sem=send_sems.at[k - 1, a], recv_sem=recv_sems.at[k - 1, a],
                          device_id=(px, py, pc), device_id_type=MESH_IDS)
            pairs.append((pltpu.make_async_remote_copy(src_ref=src, dst_ref=dsts[a].at[me], **common),
                          pltpu.make_async_remote_copy(src_ref=src, dst_ref=dsts[a].at[peer], **common)))

    def start():
        for send, _ in pairs:
            send.start()

    def wait():
        for _, arrival in pairs:
            arrival.wait_recv()
        for send, _ in pairs:
            send.wait_send()

    return start, wait


def _place_own(outs, arrs, scatter):
    me = _my_index()
    owns = [lax.dynamic_slice_in_dim(a, me, 1, axis=0) if scatter else a[None] for a in arrs]
    return [lax.dynamic_update_slice_in_dim(o, own, me, axis=0) for o, own in zip(outs, owns)]


def _exchange_shapes(arrs, scatter):
    return [jax.ShapeDtypeStruct(a.shape if scatter else (NDEV,) + a.shape, a.dtype) for a in arrs]


def _scatter_direct(arrs, name):
    n = len(arrs)

    def body(*refs):
        start, wait = _all_to_all(refs[:n], refs[n:2 * n], refs[2 * n], refs[2 * n + 1], True)
        start()
        wait()

    outs = pl.pallas_call(
        body, name=name, out_shape=_exchange_shapes(arrs, True), in_specs=[HBM_SPEC] * n, out_specs=[HBM_SPEC] * n,
        scratch_shapes=[pltpu.SemaphoreType.DMA((NDEV - 1, n)), pltpu.SemaphoreType.DMA((NDEV - 1, n))],
    )(*arrs)
    return _place_own(outs, arrs, True)


def _gather_shards(arrs, name):
    n = len(arrs)

    def body(*refs):
        ins, outs = refs[:n], refs[n:2 * n]
        send_sems, recv_sems = refs[2 * n:]
        x, y, c, chips = _mesh_place()
        dev = lambda px, py, pc: 4 * px + 2 * py + pc

        def copy(k, a, src, block, to):
            return pltpu.make_async_remote_copy(
                src_ref=src, dst_ref=outs[a].at[:, block], send_sem=send_sems.at[k, a],
                recv_sem=recv_sems.at[k, a], device_id=to, device_id_type=MESH_IDS)

        me = dev(x, y, c)
        sent = [copy(0, a, ins[a], me, (x, y, 1 - c)) for a in range(n)]
        sent += [copy(1 + j, a, ins[a], me, (px, py, c)) for j, (px, py) in enumerate(chips) for a in range(n)]
        for s in sent:
            s.start()
        for j, (px, py) in enumerate(chips):
            for a in range(n):
                blk = dev(px, py, c)
                copy(1 + j, a, ins[a], blk, (x, y, c)).wait_recv()
                forward = copy(4 + j, a, outs[a].at[:, blk], blk, (x, y, 1 - c))
                forward.start()
                sent.append(forward)
        for a in range(n):
            copy(0, a, ins[a], dev(x, y, 1 - c), (x, y, c)).wait_recv()
        for j, (px, py) in enumerate(chips):
            for a in range(n):
                copy(4 + j, a, ins[a], dev(px, py, 1 - c), (x, y, c)).wait_recv()
        for s in sent:
            s.wait_send()

    out_shape = [jax.ShapeDtypeStruct((a.shape[0], NDEV) + a.shape[1:], a.dtype) for a in arrs]
    outs = pl.pallas_call(
        body, name=name, out_shape=out_shape, in_specs=[HBM_SPEC] * n, out_specs=[HBM_SPEC] * n,
        scratch_shapes=[pltpu.SemaphoreType.DMA((7, n)), pltpu.SemaphoreType.DMA((7, n))],
    )(*arrs)
    me = _my_index()
    return [lax.dynamic_update_slice_in_dim(o, a[:, None], me, axis=1) for o, a in zip(outs, arrs)]


def _matmul(name, a, b, *, dims, grid, a_spec, b_spec, out_shape, out_specs, kaxis=None, acc_shape=None,
            extra=(), extra_specs=(), epilogue=None):
    nk = grid[kaxis] if kaxis is not None else 1
    ne = len(extra)
    multi = isinstance(out_shape, (list, tuple))
    n_out = len(out_shape) if multi else 1

    def body(*refs):
        a_ref, b_ref = refs[0], refs[1]
        ex = refs[2:2 + ne]
        outs = refs[2 + ne:2 + ne + n_out]

        def write(res):
            vals = epilogue(res, *[e[...] for e in ex]) if epilogue is not None else (res,)
            for o, v in zip(outs, vals):
                o[...] = v.astype(o.dtype)

        if len(a_ref.shape) == 3:
            part = _dot(a_ref[0].astype(BF16), b_ref[0].astype(BF16), dims)
            for s in range(1, a_ref.shape[0]):
                part = part + _dot(a_ref[s].astype(BF16), b_ref[s].astype(BF16), dims)
        else:
            part = _dot(a_ref[...].astype(BF16), b_ref[...].astype(BF16), dims)
        if nk == 1:
            write(part)
        else:
            acc = refs[-1]
            k = pl.program_id(kaxis)

            @pl.when(k == 0)
            def _():
                acc[...] = part

            @pl.when(k > 0)
            def _():
                acc[...] += part

            @pl.when(k == nk - 1)
            def _():
                write(acc[...])

    scratch = [pltpu.VMEM(acc_shape, F32)] if nk > 1 else []
    return _pcall(body, name=name, grid=grid, in_specs=[a_spec, b_spec, *extra_specs],
                  out_specs=out_specs, out_shape=out_shape, scratch=scratch)(a, b, *extra)


def _rms(x, g):
    inv = lax.rsqrt(jnp.mean(x * x, axis=-1, keepdims=True) + EPS)
    return x * inv * g


def _lnmod_math(x, g, sh, sc):
    return _rms(x, g) * (1.0 + sc) + sh


def _head_sums(x):
    row = lax.broadcasted_iota(jnp.int32, (LANES, LANES), 0) // HEAD_DIM
    col = lax.broadcasted_iota(jnp.int32, (LANES, LANES), 1) // HEAD_DIM
    same_head = jnp.where(row == col, 1.0, 0.0).astype(BF16)
    hi = x.astype(BF16)
    lo = (x - hi.astype(F32)).astype(BF16)
    return _dot(hi, same_head) + _dot(lo, same_head)


def _qkn_inv(p):
    return lax.rsqrt(_head_sums(p * p) / HEAD_DIM + EPS)


def _glu_math(val, gate):
    return jax.nn.gelu(gate) * val


def _accumulate(ref, val, first):
    @pl.when(first)
    def _():
        ref[...] = val

    @pl.when(jnp.logical_not(first))
    def _():
        ref[...] += val


def _lnmod_fwd(x, g, sh, sc):
    L, D = x.shape
    tm = _row_tile(L, 1024)

    def body(x_ref, g_ref, sh_ref, sc_ref, h_ref):
        h_ref[...] = _lnmod_math(x_ref[...], g_ref[...], sh_ref[...], sc_ref[...]).astype(BF16)

    row = pl.BlockSpec((tm, D), lambda i: (i, 0))
    vec = pl.BlockSpec((1, D), lambda i: (0, 0))
    return _pcall(body, name="lnmod_fwd", grid=(L // tm,), in_specs=[row, vec, vec, vec], out_specs=row,
                  out_shape=jax.ShapeDtypeStruct((L, D), BF16))(x, g, sh, sc)


def _lnmod_bwd(dh, x, g, sh, sc, dres):
    L, D = x.shape
    tm = _row_tile(L)

    def body(dh_ref, x_ref, g_ref, sh_ref, sc_ref, res_ref, dx_ref, dg_ref, dsh_ref, dsc_ref):
        _, vjp = jax.vjp(_lnmod_math, x_ref[...], g_ref[...], sh_ref[...], sc_ref[...])
        dx, dg, dsh, dsc = vjp(dh_ref[...])
        dx_ref[...] = dx + res_ref[...]
        first = pl.program_id(0) == 0
        _accumulate(dg_ref, dg, first)
        _accumulate(dsh_ref, dsh, first)
        _accumulate(dsc_ref, dsc, first)

    row = pl.BlockSpec((tm, D), lambda i: (i, 0))
    vec = pl.BlockSpec((1, D), lambda i: (0, 0))
    vs = jax.ShapeDtypeStruct((1, D), F32)
    return _pcall(body, name="lnmod_bwd", grid=(L // tm,), in_specs=[row, row, vec, vec, vec, row],
                  out_specs=[row, vec, vec, vec],
                  out_shape=[jax.ShapeDtypeStruct((L, D), F32), vs, vs, vs])(dh, x, g, sh, sc, dres)


def _gate_bwd(dx, a, gate):
    L, D = dx.shape
    tm = _row_tile(L, 1024)

    def body(dx_ref, a_ref, g_ref, da_ref, dg_ref):
        dxv = dx_ref[...]
        da_ref[...] = (g_ref[...] * dxv).astype(BF16)
        _accumulate(dg_ref, jnp.sum(dxv * a_ref[...], axis=0, keepdims=True), pl.program_id(0) == 0)

    row = pl.BlockSpec((tm, D), lambda i: (i, 0))
    vec = pl.BlockSpec((1, D), lambda i: (0, 0))
    return _pcall(body, name="gate_bwd", grid=(L // tm,), in_specs=[row, row, vec], out_specs=[row, vec],
                  out_shape=[jax.ShapeDtypeStruct((L, D), BF16), jax.ShapeDtypeStruct((1, D), F32)])(dx, a, gate)


def _qknorm_fwd(p, gqk, AW):
    L = p.shape[0]
    tm = _row_tile(L, 2048)
    ncol = 2 * AW // LANES

    def body(p_ref, g_ref, o_ref):
        p = p_ref[...]
        o_ref[...] = (p * _qkn_inv(p) * g_ref[...]).astype(BF16)

    blk = pl.BlockSpec((tm, LANES), lambda i, j: (i, j))
    vec = pl.BlockSpec((1, LANES), lambda i, j: (0, j))
    return _pcall(body, name="qknorm_fwd", grid=(L // tm, ncol), in_specs=[blk, vec], out_specs=blk,
                  out_shape=jax.ShapeDtypeStruct((L, 2 * AW), BF16))(p, gqk)


def _qknorm_bwd(dqk, p, gqk, AW):
    L = p.shape[0]
    tm = _row_tile(L, 2048)
    ncol = 2 * AW // LANES

    def body(d_ref, p_ref, g_ref, dp_ref, dg_ref):
        p, dy = p_ref[...], d_ref[...]
        inv = _qkn_inv(p)
        gdy = g_ref[...] * dy
        dp_ref[...] = (inv * gdy - p * (inv * inv * inv) * (_head_sums(p * gdy) / HEAD_DIM)).astype(BF16)
        _accumulate(dg_ref, jnp.sum(dy * p * inv, axis=0, keepdims=True), pl.program_id(1) == 0)

    blk = pl.BlockSpec((tm, LANES), lambda j, i: (i, j))
    vec = pl.BlockSpec((1, LANES), lambda j, i: (0, j))
    return _pcall(body, name="qknorm_bwd", grid=(ncol, L // tm), in_specs=[blk, blk, vec], out_specs=[blk, vec],
                  out_shape=[jax.ShapeDtypeStruct((L, 2 * AW), BF16),
                             jax.ShapeDtypeStruct((1, 2 * AW), F32)])(dqk, p, gqk)


def _outnorm_fwd(oa, os_, ga, gs):
    L, AW = oa.shape
    SW = os_.shape[1]
    tm = _row_tile(L)

    def body(oa_ref, os_ref, ga_ref, gs_ref, o_ref):
        o_ref[:, :AW] = _rms(oa_ref[...], ga_ref[...]).astype(BF16)
        o_ref[:, AW:] = _rms(os_ref[...], gs_ref[...]).astype(BF16)

    ra = pl.BlockSpec((tm, AW), lambda i: (i, 0))
    rs = pl.BlockSpec((tm, SW), lambda i: (i, 0))
    va = pl.BlockSpec((1, AW), lambda i: (0, 0))
    vs = pl.BlockSpec((1, SW), lambda i: (0, 0))
    ro = pl.BlockSpec((tm, AW + SW), lambda i: (i, 0))
    return _pcall(body, name="outnorm_fwd", grid=(L // tm,), in_specs=[ra, rs, va, vs], out_specs=ro,
                  out_shape=jax.ShapeDtypeStruct((L, AW + SW), BF16))(oa, os_, ga, gs)


def _outnorm_bwd(do, oa, os_, ga, gs):
    L, AW = oa.shape
    SW = os_.shape[1]
    tm = _row_tile(L)

    def body(do_ref, oa_ref, os_ref, ga_ref, gs_ref, doa_ref, dos_ref, dga_ref, dgs_ref):
        first = pl.program_id(0) == 0
        _, vjp_a = jax.vjp(_rms, oa_ref[...], ga_ref[...])
        doa, dga = vjp_a(do_ref[:, :AW])
        _, vjp_s = jax.vjp(_rms, os_ref[...], gs_ref[...])
        dos, dgs = vjp_s(do_ref[:, AW:])
        doa_ref[...] = doa
        dos_ref[...] = dos
        _accumulate(dga_ref, dga, first)
        _accumulate(dgs_ref, dgs, first)

    ra = pl.BlockSpec((tm, AW), lambda i: (i, 0))
    rs = pl.BlockSpec((tm, SW), lambda i: (i, 0))
    va = pl.BlockSpec((1, AW), lambda i: (0, 0))
    vs = pl.BlockSpec((1, SW), lambda i: (0, 0))
    ro = pl.BlockSpec((tm, AW + SW), lambda i: (i, 0))
    return _pcall(body, name="outnorm_bwd", grid=(L // tm,), in_specs=[ro, ra, rs, va, vs],
                  out_specs=[ra, rs, va, vs],
                  out_shape=[jax.ShapeDtypeStruct((L, AW), F32), jax.ShapeDtypeStruct((L, SW), F32),
                             jax.ShapeDtypeStruct((1, AW), F32), jax.ShapeDtypeStruct((1, SW), F32)])(
                                 do, oa, os_, ga, gs)


def _softplus_neg_abs(z):
    return jnp.log(1.0 + jnp.exp(-jnp.abs(z)))


def _attn_masks(B):
    row = lax.broadcasted_iota(jnp.int32, (B, B), 0)
    col = lax.broadcasted_iota(jnp.int32, (B, B), 1)
    strict = col < row
    upper = jnp.where(row > col, 1.0, 0.0).astype(BF16)
    lower = jnp.where(row < col, 1.0, 0.0).astype(BF16)
    return strict, upper, lower


def _ride_along(ride, scatter, HP, nb):
    n = len(ride)
    sems = [pltpu.SemaphoreType.DMA((NDEV - 1, n)), pltpu.SemaphoreType.DMA((NDEV - 1, n))] if n else []

    def hooks(srcs, dsts, send_sems, recv_sems):
        start, wait = _all_to_all(srcs, dsts, send_sems, recv_sems, scatter)
        hp, i = pl.program_id(0), pl.program_id(1)
        return (hp == 0) & (i == 0), start, (hp == HP - 1) & (i == nb - 1), wait

    return [HBM_SPEC] * n, _exchange_shapes(ride, scatter), sems, hooks


def _attn_fwd(qk, p, AW, B, ride=()):
    L = qk.shape[0]
    HP = AW // LANES
    nb = L // B
    n = len(ride)
    ride_specs, ride_shapes, ride_sems, ride_hooks = _ride_along(ride, False, HP, nb)

    def body(*refs):
        q_ref, k_ref, v_ref = refs[:3]
        o_ref, tot_ref = refs[3 + n:5 + n]
        scr = refs[5 + 2 * n:]
        if n:
            first, start, last, wait = ride_hooks(refs[3:3 + n], refs[5 + n:5 + 2 * n], scr[10], scr[11])
            pl.when(first)(start)
        lb_scr = (scr[0:2], scr[2:4])
        tail_scr = (scr[4:6], scr[6:8])
        sum_scr = scr[8:10]
        i = pl.program_id(1)
        m0 = lax.broadcasted_iota(jnp.int32, (1, LANES), 1) < HEAD_DIM
        strict, upper, _ = _attn_masks(B)
        q = q_ref[...] * 0.125
        zq = jnp.zeros_like(q)
        qh = (jnp.where(m0, q, zq), jnp.where(m0, zq, q))

        def keys(j):
            start = pl.multiple_of(jnp.maximum(j, 0) * B, B)
            return k_ref[pl.ds(start, B), :]

        def vals(j):
            start = pl.multiple_of(jnp.maximum(j, 0) * B, B)
            return v_ref[pl.ds(start, B), :].astype(BF16)

        strips = [slice(s, min(s + ATTN_STRIP, B)) for s in range(0, B, ATTN_STRIP)]

        def scores(j):
            kj = keys(j)
            return tuple(_dot(qh[h], kj, NT) for h in (0, 1))

        def logits(zs2, slot, diag):
            for h in (0, 1):
                z = zs2[h]
                his = []
                for rows in strips:
                    zs = z[rows]
                    lb = jnp.minimum(zs, 0.0) - _softplus_neg_abs(zs)
                    l1 = lb - zs
                    if diag:
                        l1 = jnp.where(strict[rows], l1, 0.0)
                    lb_scr[slot][h][rows, :] = lb
                    his.append(l1.astype(BF16))
                    rsum = jnp.sum(l1, axis=-1, keepdims=True)
                    if h == 0:
                        sum_scr[slot][rows, :] = jnp.broadcast_to(rsum, (rows.stop - rows.start, LANES))
                    else:
                        sum_scr[slot][rows, :] = jnp.where(m0, sum_scr[slot][rows, :], rsum)
                cat = lambda xs: jnp.concatenate(xs, axis=0)
                tail_scr[slot][h][...] = _dot(cat(his), upper)

        def attend(j, slot, diag):
            vj = vals(j)
            pv = []
            for h in (0, 1):
                ws = []
                for rows in strips:
                    w = jnp.exp(lb_scr[slot][h][rows, :] + tail_scr[slot][h][rows, :])
                    if diag:
                        w = jnp.where(strict[rows], w, 0.0)
                    ws.append(w.astype(BF16))
                pv.append(_dot(jnp.concatenate(ws, axis=0), vj))
            return jnp.where(m0, pv[0], pv[1])

        logits(scores(i), 0, True)

        @pl.when(i == 0)
        def _():
            o_ref[...] = attend(i, 0, True)
            tot_ref[...] = sum_scr[0][...]

        def half(j, slot):
            z = scores(j - 1)
            pv = attend(j, slot, False)
            logits(z, 1 - slot, False)
            o_ref[...] += pv * jnp.exp(tot_ref[...])
            tot_ref[...] += sum_scr[slot][...]

        @pl.when(i > 0)
        def _():
            z = scores(i - 1)
            pv = attend(i, 0, True)
            logits(z, 1, False)
            o_ref[...] = pv
            tot_ref[...] = sum_scr[0][...]

            @pl.loop(0, (i + 1) // 2)
            def _(t):
                j = i - 1 - 2 * t
                half(j, 1)

                @pl.when(j > 0)
                def _():
                    half(j - 1, 0)

        if n:
            pl.when(last)(wait)

    qspec = pl.BlockSpec((B, LANES), lambda hp, i: (i, hp))
    kspec = pl.BlockSpec((L, LANES), lambda hp, i: (0, HP + hp))
    vspec = pl.BlockSpec((L, LANES), lambda hp, i: (0, 2 * HP + hp))
    ospec = pl.BlockSpec((B, LANES), lambda hp, i: (i, hp))
    shp = jax.ShapeDtypeStruct((L, AW), F32)
    o, tot, *gathered = _pcall(
        body, name="attn_fwd", grid=(HP, nb), in_specs=[qspec, kspec, vspec] + ride_specs,
        out_specs=[ospec, ospec] + ride_specs, out_shape=[shp, shp] + ride_shapes,
        scratch=[pltpu.VMEM((B, B), F32)] * 8 + [pltpu.VMEM((B, LANES), F32)] * 2 + ride_sems)(qk, qk, p, *ride)
    return o, tot, _place_own(gathered, ride, False)


def _attn_bwd(qk, p, do, tot, AW, B, ride=()):
    L = qk.shape[0]
    HP = AW // LANES
    nb = L // B
    n = len(ride)
    ride_specs, ride_shapes, ride_sems, ride_hooks = _ride_along(ride, True, HP, nb)

    def body(*refs):
        q_ref, k_ref, v_ref, do_ref, tot_ref = refs[:5]
        dq_ref, dk_ref, dv_ref = refs[5 + n:8 + n]
        scr = refs[8 + 2 * n:]
        if n:
            first_step, start, last_step, wait = ride_hooks(refs[5:5 + n], refs[8 + n:8 + 2 * n], scr[15], scr[16])
            pl.when(first_step)(start)
        lb_scr, tail_scr, dw_scr, e_scr, beta_scr = scr[0:2], scr[2:4], scr[4:6], scr[6:8], scr[8:10]
        dos_scr, bsum_scr, left_scr, esum_scr, ecum_scr = scr[10:15]
        i = pl.program_id(1)
        m0 = lax.broadcasted_iota(jnp.int32, (1, LANES), 1) < HEAD_DIM
        strict, upper, lower = _attn_masks(B)
        strips = [slice(s, min(s + ATTN_STRIP, B)) for s in range(0, B, ATTN_STRIP)]
        cat = lambda xs: jnp.concatenate(xs, axis=0)

        @pl.when(i == 0)
        def _():
            dk_ref[...] = jnp.zeros_like(dk_ref)
            dv_ref[...] = jnp.zeros_like(dv_ref)

        q = q_ref[...] * 0.125
        zq = jnp.zeros_like(q)
        qh = (jnp.where(m0, q, zq), jnp.where(m0, zq, q))
        heads = lambda a: (jnp.where(m0, a, jnp.zeros_like(a)), jnp.where(m0, jnp.zeros_like(a), a))
        left_scr[...] = jnp.zeros_like(left_scr)
        ecum_scr[...] = jnp.zeros_like(ecum_scr)
        dq_ref[...] = jnp.zeros_like(dq_ref)

        def block_rows(j):
            return pl.ds(pl.multiple_of(j * B, B), B)

        def put_row_sums(ref, rows, h, rsum):
            if h == 0:
                ref[rows, :] = jnp.broadcast_to(rsum, (rows.stop - rows.start, LANES))
            else:
                ref[rows, :] = jnp.where(m0, ref[rows, :], rsum)

        def head_cols(x, h):
            other = pltpu.roll(x, HEAD_DIM, 1)
            full = jnp.where(m0, x, other) if h == 0 else jnp.where(m0, other, x)
            return jnp.concatenate([full] * (B // LANES), axis=1) if B > LANES else full

        def scores(j):
            kj = k_ref[block_rows(j), :]
            return tuple(_dot(qh[h], kj, NT) for h in (0, 1))

        def stage_a(j, zs2, diag):
            for h in (0, 1):
                his = []
                for rows in strips:
                    zs = zs2[h][rows]
                    lb = jnp.minimum(zs, 0.0) - _softplus_neg_abs(zs)
                    l1 = lb - zs
                    if diag:
                        l1 = jnp.where(strict[rows], l1, 0.0)
                    lb_scr[h][rows, :] = lb
                    his.append(l1.astype(BF16))
                    put_row_sums(bsum_scr, rows, h, jnp.sum(l1, axis=-1, keepdims=True))
                tail_scr[h][...] = _dot(cat(his), upper)
            bs = bsum_scr[...]
            scale = jnp.exp(tot_ref[...] - left_scr[...] - bs)
            left_scr[...] += bs
            dos = (do_ref[...] * scale).astype(BF16)
            dos_scr[...] = dos
            vj = v_ref[block_rows(j), :].astype(BF16)
            dosh = heads(dos)
            for h in (0, 1):
                dw_scr[h][...] = _dot(dosh[h], vj, NT)

        def stage_b(j, diag):
            dosh = heads(dos_scr[...])
            pres = []
            dv_blk = jnp.zeros((B, LANES), F32)
            for h in (0, 1):
                ehs, wbs = [], []
                for rows in strips:
                    lb = lb_scr[h][rows, :]
                    w = jnp.exp(lb + tail_scr[h][rows, :])
                    if diag:
                        w = jnp.where(strict[rows], w, 0.0)
                    beta_scr[h][rows, :] = jnp.exp(lb)
                    e = dw_scr[h][rows, :] * w
                    e_scr[h][rows, :] = e
                    ehs.append(e.astype(BF16))
                    wbs.append(w.astype(BF16))
                    put_row_sums(esum_scr, rows, h, jnp.sum(e, axis=-1, keepdims=True))
                pres.append(_dot(cat(ehs), lower))
                dv_blk = dv_blk + _dot(cat(wbs), dosh[h], TN)
            dv_ref[block_rows(j), :] += dv_blk
            return pres

        def stage_c(j, pres, diag):
            kh = heads(k_ref[block_rows(j), :])
            dq = jnp.zeros((B, LANES), F32)
            dk_blk = jnp.zeros((B, LANES), F32)
            for h in (0, 1):
                dzs = []
                for rows in strips:
                    e = e_scr[h][rows, :]
                    dl1 = pres[h][rows] + head_cols(ecum_scr[rows, :], h)
                    dz = e - beta_scr[h][rows, :] * (e + dl1)
                    if diag:
                        dz = jnp.where(strict[rows], dz, 0.0)
                    dzs.append(dz.astype(BF16))
                dzb = cat(dzs)
                dq = dq + _dot(dzb, kh[h])
                dk_blk = dk_blk + _dot(dzb, qh[h], TN)
            dq_ref[...] += dq
            dk_ref[block_rows(j), :] += dk_blk
            ecum_scr[...] += esum_scr[...]

        def step(t, diag_next):
            z = scores(t + 1)
            pres = stage_b(t, False)
            stage_a(t + 1, z, diag_next)
            stage_c(t, pres, False)

        @pl.when(i > 0)
        def _():
            stage_a(0, scores(0), False)

            @pl.loop(0, i - 1)
            def _(t):
                step(t, False)

            step(i - 1, True)

        @pl.when(i == 0)
        def _():
            stage_a(0, scores(0), True)

        stage_c(i, stage_b(i, True), True)
        dq_ref[...] = dq_ref[...] * 0.125
        if n:
            pl.when(last_step)(wait)

    qspec = pl.BlockSpec((B, LANES), lambda hp, i: (i, hp))
    kspec = pl.BlockSpec((L, LANES), lambda hp, i: (0, HP + hp))
    vspec = pl.BlockSpec((L, LANES), lambda hp, i: (0, 2 * HP + hp))
    full = pl.BlockSpec((L, LANES), lambda hp, i: (0, hp))
    shp = jax.ShapeDtypeStruct((L, AW), F32)
    dq, dk, dv, *received = _pcall(
        body, name="attn_bwd", grid=(HP, nb), in_specs=[qspec, kspec, vspec, qspec, qspec] + ride_specs,
        out_specs=[qspec, full, full] + ride_specs, out_shape=[shp, shp, shp] + ride_shapes,
        scratch=[pltpu.VMEM((B, B), F32)] * 10 + [pltpu.VMEM((B, LANES), BF16)]
        + [pltpu.VMEM((B, LANES), F32)] * 4 + ride_sems)(qk, qk, p, do, tot, *ride)
    return dq, dk, dv, _place_own(received, ride, True)


def _s5_disc(ar, ai, ldt):
    dt = jnp.exp(ldt)
    mag = jnp.exp(dt * ar)
    abr = mag * jnp.cos(dt * ai)
    abi = mag * jnp.sin(dt * ai)
    emr = abr - 1.0
    emi = abi
    den = ar * ar + ai * ai
    fr = (emr * ar + emi * ai) / den
    fi = (emi * ar - emr * ai) / den
    return abr, abi, fr, fi


def _s5_params_math(ar, ai, ldt, br, bi):
    abr, abi, fr, fi = _s5_disc(ar, ai, ldt)
    return abr, abi, fr * br - fi * bi, fr * bi + fi * br


def _cmul_add(xr, xi, kr, ki, sr, si):
    return xr + (kr * sr - ki * si), xi + (kr * si + ki * sr)


def _s5_tab_rows(T):
    ng = T // 8
    return dict(NG=ng, POW=0, PA=ng, PAR=ng + 8, LA=ng + 16, ROWS=ng + 24)


def _s5_prep(ar, ai, ldt, braw_r, braw_i, T):
    NL, _, NS = ar.shape
    GT = NS // STATE_TILE
    R = _s5_tab_rows(T)

    def body(ar_ref, ai_ref, ldt_ref, br_ref, bi_ref, abr_ref, abi_ref, tabr_ref, tabi_ref, bsr_ref, bsi_ref):
        for t in range(GT):
            sl = slice(t * STATE_TILE, (t + 1) * STATE_TILE)
            abr, abi, bsr, bsi = _s5_params_math(ar_ref[:, sl], ai_ref[:, sl], ldt_ref[:, sl],
                                                 br_ref[t], bi_ref[t])
            abr_ref[:, sl] = abr
            abi_ref[:, sl] = abi
            bsr_ref[t] = bsr.astype(BF16)
            bsi_ref[t] = bsi.astype(BF16)

            def put(row, vr, vi):
                tabr_ref[row:row + 1, sl] = vr
                tabi_ref[row:row + 1, sl] = vi

            pr, pi = abr, abi
            for g in range(R["NG"]):
                put(R["POW"] + g, pr, pi)
                if g + 1 < R["NG"]:
                    pr, pi = pr * abr - pi * abi, pr * abi + pi * abr
            big_r, big_i = pr, pi
            qr, qi = big_r, big_i
            for r in range(8):
                put(R["PA"] + r, qr, qi)
                put(R["PAR"] + 7 - r, qr, qi)
                qr, qi = qr * big_r - qi * big_i, qr * big_i + qi * big_r
            qr, qi = big_r, big_i
            for k in range(3):
                put(R["LA"] + k, qr, qi)
                qr, qi = qr * qr - qi * qi, 2.0 * qr * qi
            for k in range(3, 8):
                put(R["LA"] + k, jnp.zeros_like(qr), jnp.zeros_like(qi))

    rowspec = pl.BlockSpec((None, 1, NS), lambda l: (l, 0, 0))
    bspec = pl.BlockSpec((None, GT, LANES, STATE_TILE), lambda l: (l, 0, 0, 0))
    tabspec = pl.BlockSpec((None, R["ROWS"], NS), lambda l: (l, 0, 0))
    rs = jax.ShapeDtypeStruct((NL, 1, NS), F32)
    ts = jax.ShapeDtypeStruct((NL, R["ROWS"], NS), F32)
    bs = jax.ShapeDtypeStruct((NL, GT, LANES, STATE_TILE), BF16)
    return _pcall(body, name="s5_prep", grid=(NL,), in_specs=[rowspec] * 3 + [bspec] * 2,
                  out_specs=[rowspec, rowspec, tabspec, tabspec, bspec, bspec],
                  out_shape=[rs, rs, ts, ts, bs, bs])(ar, ai, ldt, braw_r, braw_i)


def _s5_prep_bwd(ar, ai, ldt, braw_r, braw_i, dabr, dabi, dbsr, dbsi):
    NL, _, NS = ar.shape
    GT = NS // STATE_TILE

    def body(ar_ref, ai_ref, ldt_ref, br_ref, bi_ref, dabr_ref, dabi_ref, dbsr_ref, dbsi_ref,
             dar_ref, dai_ref, dldt_ref, dbr_ref, dbi_ref):
        for t in range(GT):
            sl = slice(t * STATE_TILE, (t + 1) * STATE_TILE)
            _, vjp = jax.vjp(_s5_params_math, ar_ref[:, sl], ai_ref[:, sl], ldt_ref[:, sl],
                             br_ref[t], bi_ref[t])
            dabr_row = jnp.sum(dabr_ref[:, sl], axis=0, keepdims=True)
            dabi_row = jnp.sum(dabi_ref[:, sl], axis=0, keepdims=True)
            dar, dai, dldt, dbr, dbi = vjp((dabr_row, dabi_row, dbsr_ref[t], dbsi_ref[t]))
            dar_ref[:, sl] = dar
            dai_ref[:, sl] = dai
            dldt_ref[:, sl] = dldt
            dbr_ref[t] = dbr
            dbi_ref[t] = dbi

    rowspec = pl.BlockSpec((None, 1, NS), lambda l: (l, 0, 0))
    row8spec = pl.BlockSpec((None, 8, NS), lambda l: (l, 0, 0))
    bspec = pl.BlockSpec((None, GT, LANES, STATE_TILE), lambda l: (l, 0, 0, 0))
    rs = jax.ShapeDtypeStruct((NL, 1, NS), F32)
    bs = jax.ShapeDtypeStruct((NL, GT, LANES, STATE_TILE), F32)
    return _pcall(body, name="s5_prep_bwd", grid=(NL,), in_specs=[rowspec] * 3 + [bspec] * 2 + [row8spec] * 2 + [bspec] * 2,
                  out_specs=[rowspec] * 3 + [bspec] * 2, out_shape=[rs, rs, rs, bs, bs])(
                      ar, ai, ldt, braw_r, braw_i, dabr, dabi, dbsr, dbsi)


def _to_streams(x, T):
    L, C = x.shape
    return x.reshape(L // T, 8, T // 8, C).transpose(0, 2, 1, 3).reshape(L, C)


def _from_streams(x, T):
    L, C = x.shape
    return x.reshape(L // T, T // 8, 8, C).transpose(0, 2, 1, 3).reshape(L, C)


def _s5_specs(L, SW, T, rev):
    GT = SW // LANES
    nc = L // T
    cidx = (lambda c: nc - 1 - c) if rev else (lambda c: c)
    rows = _s5_tab_rows(T)["ROWS"]
    return dict(
        GT=GT, nc=nc, cidx=cidx,
        chan=pl.BlockSpec((T, LANES), lambda j, c: (cidx(c), j)),
        state=pl.BlockSpec((T, STATE_TILE), lambda j, c: (cidx(c), j)),
        bmat=pl.BlockSpec((None, LANES, STATE_TILE), lambda j, c: (j, 0, 0)),
        cmat=pl.BlockSpec((None, STATE_TILE, LANES), lambda j, c: (j, 0, 0)),
        gmat=pl.BlockSpec((None, LANES, LANES), lambda j, c: (j, 0, 0)),
        cvec=pl.BlockSpec((1, LANES), lambda j, c: (0, j)),
        svec8=pl.BlockSpec((8, STATE_TILE), lambda j, c: (0, j)),
        tab=pl.BlockSpec((rows, STATE_TILE), lambda j, c: (0, j)),
    )


def _s5_fwd(u, sp, T):
    L, SW = u.shape
    S = _s5_specs(L, SW, T, False)
    NS = S["GT"] * STATE_TILE
    R = _s5_tab_rows(T)
    NG = R["NG"]

    def body(u_ref, bsr_ref, bsi_ref, tabr_ref, tabi_ref, cr_ref, ci_ref, d_ref, wg_ref, gb_ref,
             o_ref, sr_ref, si_ref, y_ref, carr_ref, cari_ref):
        @pl.when(pl.program_id(1) == 0)
        def _():
            carr_ref[...] = jnp.zeros_like(carr_ref)
            cari_ref[...] = jnp.zeros_like(cari_ref)

        uv = u_ref[...]
        ub = uv.astype(BF16)
        row8 = lax.broadcasted_iota(jnp.int32, (8, LANES), 0)
        nstrips = STATE_TILE // LANES

        def project(s):
            sl = slice(s * LANES, (s + 1) * LANES)
            return _dot(ub, bsr_ref[:, sl]), _dot(ub, bsi_ref[:, sl])

        ahead = project(0)
        for s in range(nstrips):
            sl = slice(s * LANES, (s + 1) * LANES)
            tab = lambda r0, n=1, sl=sl: (tabr_ref[r0:r0 + n, sl], tabi_ref[r0:r0 + n, sl])
            xr, xi = ahead
            if s + 1 < nstrips:
                ahead = project(s + 1)
            ar, ai = tab(R["POW"])
            lr, li = [xr[0:8]], [xi[0:8]]
            for g in range(1, NG):
                nr, ni = _cmul_add(xr[8 * g:8 * g + 8], xi[8 * g:8 * g + 8], ar, ai, lr[-1], li[-1])
                lr.append(nr)
                li.append(ni)
            yr, yi = lr[-1], li[-1]
            for k, dist in enumerate((1, 2, 4)):
                kr, ki = tab(R["LA"] + k)
                keep = row8 >= dist
                yr, yi = _cmul_add(yr, yi, kr, ki, jnp.where(keep, pltpu.roll(yr, dist, 0), 0.0),
                                   jnp.where(keep, pltpu.roll(yi, dist, 0), 0.0))
            c0r, c0i = carr_ref[7:8, sl], cari_ref[7:8, sl]
            par, pai = tab(R["PA"], 8)
            er, ei = _cmul_add(yr, yi, par, pai, c0r, c0i)
            carr_ref[:, sl] = er
            cari_ref[:, sl] = ei
            first = row8 == 0
            inr = jnp.where(first, c0r, pltpu.roll(er, 1, 0))
            ini = jnp.where(first, c0i, pltpu.roll(ei, 1, 0))
            for g in range(NG):
                pr, pi = tab(R["POW"] + g)
                outr, outi = _cmul_add(lr[g], li[g], pr, pi, inr, ini)
                sr_ref[8 * g:8 * g + 8, sl] = outr
                si_ref[8 * g:8 * g + 8, sl] = outi
        y = (_dot(sr_ref[...].astype(BF16), cr_ref[...]) - _dot(si_ref[...].astype(BF16), ci_ref[...])
             + d_ref[...] * uv)
        y_ref[...] = y
        yg = jax.nn.gelu(y)
        gate = jax.nn.sigmoid(_dot(yg.astype(BF16), wg_ref[...]) + gb_ref[...])
        o_ref[...] = yg * gate

    cs = jax.ShapeDtypeStruct((L, SW), F32)
    ss = jax.ShapeDtypeStruct((L, NS), F32)
    return _pcall(
        body, name="s5_fwd", grid=(S["GT"], S["nc"]),
        in_specs=[S["chan"], S["bmat"], S["bmat"], S["tab"], S["tab"], S["cmat"], S["cmat"],
                  S["cvec"], S["gmat"], S["cvec"]],
        out_specs=[S["chan"], S["state"], S["state"], S["chan"]], out_shape=[cs, ss, ss, cs],
        scratch=[pltpu.VMEM((8, STATE_TILE), F32), pltpu.VMEM((8, STATE_TILE), F32)],
    )(u, sp["bsr"], sp["bsi"], sp["tabr"], sp["tabi"], sp["crT"], sp["ciT"], sp["d"], sp["wg"], sp["gb"])


def _s5_bwd(do, u, s_re, s_im, ypre, sp, T):
    L, SW = u.shape
    S = _s5_specs(L, SW, T, True)
    GT, nc, cidx = S["GT"], S["nc"], S["cidx"]
    NS = GT * STATE_TILE
    R = _s5_tab_rows(T)
    NG = R["NG"]
    T8 = T // 8

    def body(do_ref, u_ref, sr_ref, si_ref, hr_ref, hi_ref, y_ref, bsr_ref, bsi_ref, tabr_ref, tabi_ref,
             cr_ref, ci_ref, d_ref, wg_ref, gb_ref,
             du_ref, dbsr_ref, dbsi_ref, dcr_ref, dci_ref, dd_ref, dwg_ref, dgb_ref, dar_ref, dai_ref,
             carr_ref, cari_ref, lam_r, lam_i):
        c = pl.program_id(1)
        first = c == 0

        @pl.when(first)
        def _():
            carr_ref[...] = jnp.zeros_like(carr_ref)
            cari_ref[...] = jnp.zeros_like(cari_ref)

        u = u_ref[...]
        ub = u.astype(BF16)
        y = y_ref[...]
        yg, gelu_vjp = jax.vjp(jax.nn.gelu, y)
        ygb = yg.astype(BF16)
        gate = jax.nn.sigmoid(_dot(ygb, wg_ref[...]) + gb_ref[...])
        dout = do_ref[...]
        dt = dout * yg * gate * (1.0 - gate)
        dtb = dt.astype(BF16)
        dyg = dout * gate + _dot(dtb, wg_ref[...], NT)
        (dy,) = gelu_vjp(dyg)
        dyb = dy.astype(BF16)
        _accumulate(dwg_ref, _dot(ygb, dtb, TN), first)
        _accumulate(dgb_ref, jnp.sum(dt, axis=0, keepdims=True), first)
        _accumulate(dd_ref, jnp.sum(dy * u, axis=0, keepdims=True), first)
        _accumulate(dcr_ref, _dot(sr_ref[...].astype(BF16), dyb, TN), first)
        _accumulate(dci_ref, -_dot(si_ref[...].astype(BF16), dyb, TN), first)
        earliest = cidx(c) == 0
        row8 = lax.broadcasted_iota(jnp.int32, (8, LANES), 0)
        nstrips = STATE_TILE // LANES

        def project(s):
            sl = slice(s * LANES, (s + 1) * LANES)
            return _dot(dyb, cr_ref[sl, :], NT), -_dot(dyb, ci_ref[sl, :], NT)

        ahead = project(0)
        for s in range(nstrips):
            sl = slice(s * LANES, (s + 1) * LANES)
            tab = lambda r0, n=1, sl=sl: (tabr_ref[r0:r0 + n, sl], -tabi_ref[r0:r0 + n, sl])
            xr, xi = ahead
            if s + 1 < nstrips:
                ahead = project(s + 1)
            ar, ai = tab(R["POW"])
            lr, li = [None] * NG, [None] * NG
            lr[NG - 1], li[NG - 1] = xr[8 * (NG - 1):8 * NG], xi[8 * (NG - 1):8 * NG]
            for g in range(NG - 2, -1, -1):
                lr[g], li[g] = _cmul_add(xr[8 * g:8 * g + 8], xi[8 * g:8 * g + 8], ar, ai, lr[g + 1], li[g + 1])
            yr, yi = lr[0], li[0]
            for k, dist in enumerate((1, 2, 4)):
                kr, ki = tab(R["LA"] + k)
                keep = row8 < 8 - dist
                yr, yi = _cmul_add(yr, yi, kr, ki, jnp.where(keep, pltpu.roll(yr, 8 - dist, 0), 0.0),
                                   jnp.where(keep, pltpu.roll(yi, 8 - dist, 0), 0.0))
            c0r, c0i = carr_ref[0:1, sl], cari_ref[0:1, sl]
            par, pai = tab(R["PAR"], 8)
            er, ei = _cmul_add(yr, yi, par, pai, c0r, c0i)
            carr_ref[:, sl] = er
            cari_ref[:, sl] = ei
            last = row8 == 7
            inr = jnp.where(last, c0r, pltpu.roll(er, 7, 0))
            ini = jnp.where(last, c0i, pltpu.roll(ei, 7, 0))
            hr0 = jnp.where(earliest, 0.0, hr_ref[7:8, sl])
            hi0 = jnp.where(earliest, 0.0, hi_ref[7:8, sl])
            endr, endi = sr_ref[8 * (NG - 1):8 * NG, sl], si_ref[8 * (NG - 1):8 * NG, sl]
            pvr = jnp.where(row8 == 0, hr0, pltpu.roll(endr, 1, 0))
            pvi = jnp.where(row8 == 0, hi0, pltpu.roll(endi, 1, 0))
            accr = jnp.zeros((8, LANES), F32)
            acci = jnp.zeros((8, LANES), F32)
            for g in range(NG):
                pr, pi = tab(R["POW"] + NG - 1 - g)
                outr, outi = _cmul_add(lr[g], li[g], pr, pi, inr, ini)
                lam_r[8 * g:8 * g + 8, sl] = outr
                lam_i[8 * g:8 * g + 8, sl] = outi
                accr = accr + (outr * pvr + outi * pvi)
                acci = acci + (outi * pvr - outr * pvi)
                pvr, pvi = sr_ref[8 * g:8 * g + 8, sl], si_ref[8 * g:8 * g + 8, sl]

            @pl.when(first)
            def _():
                dar_ref[:, sl] = accr
                dai_ref[:, sl] = acci

            @pl.when(jnp.logical_not(first))
            def _():
                dar_ref[:, sl] += accr
                dai_ref[:, sl] += acci

        lrb = lam_r[...].astype(BF16)
        lib = lam_i[...].astype(BF16)
        _accumulate(dbsr_ref, _dot(ub, lrb, TN), first)
        _accumulate(dbsi_ref, _dot(ub, lib, TN), first)
        du_ref[...] = (dy * d_ref[...] + _dot(lrb, bsr_ref[...], NT) + _dot(lib, bsi_ref[...], NT)).astype(BF16)

    halo = pl.BlockSpec((8, STATE_TILE), lambda j, c: (jnp.maximum(cidx(c) * T8 - 1, 0), j))
    f = lambda *s: jax.ShapeDtypeStruct(s, F32)
    return _pcall(
        body, name="s5_bwd", grid=(GT, nc),
        in_specs=[S["chan"], S["chan"], S["state"], S["state"], halo, halo, S["chan"], S["bmat"], S["bmat"],
                  S["tab"], S["tab"], S["cmat"], S["cmat"], S["cvec"], S["gmat"], S["cvec"]],
        out_specs=[S["chan"], S["bmat"], S["bmat"], S["cmat"], S["cmat"], S["cvec"], S["gmat"], S["cvec"],
                   S["svec8"], S["svec8"]],
        out_shape=[jax.ShapeDtypeStruct((L, SW), BF16), f(GT, LANES, STATE_TILE), f(GT, LANES, STATE_TILE), f(GT, STATE_TILE, LANES),
                   f(GT, STATE_TILE, LANES), f(1, SW), f(GT, LANES, LANES), f(1, SW), f(8, NS), f(8, NS)],
        scratch=[pltpu.VMEM((8, STATE_TILE), F32), pltpu.VMEM((8, STATE_TILE), F32),
                 pltpu.VMEM((T, STATE_TILE), F32), pltpu.VMEM((T, STATE_TILE), F32)],
    )(do, u, s_re, s_im, s_re, s_im, ypre, sp["bsr"], sp["bsi"], sp["tabr"], sp["tabi"],
      sp["crT"], sp["ciT"], sp["d"], sp["wg"], sp["gb"])


def _shifted_rows(x, k, edge):
    n = x.shape[0]
    rolled = pltpu.roll(x, k % n, 0)
    row8 = lax.broadcasted_iota(jnp.int32, (8, x.shape[1]), 0)
    if k > 0:
        fixed = rolled[0:8]
        for r, e in enumerate(edge):
            fixed = jnp.where(row8 == r, e, fixed)
        return jnp.concatenate([fixed, rolled[8:]], axis=0) if n > 8 else fixed
    fixed = rolled[n - 8:n]
    for r, e in enumerate(edge):
        fixed = jnp.where(row8 == 8 + k + r, e, fixed)
    return jnp.concatenate([rolled[:n - 8], fixed], axis=0) if n > 8 else fixed


def _conv_taps(xc, h6, h7):
    return _shifted_rows(xc, 1, [h7]), _shifted_rows(xc, 2, [h6, h7])


def _conv_halves(up_ref, halo_ref, w_ref, b_ref, tm, FS, first):
    outs, taps = [], []
    for s in (0, 1):
        xc = up_ref[s]
        h6 = jnp.where(first, 0.0, halo_ref[s, 6:7, :])
        h7 = jnp.where(first, 0.0, halo_ref[s, 7:8, :])
        x1, x2 = _conv_taps(xc, h6, h7)
        outs.append(b_ref[s] + x2 * w_ref[s, 0:1, :] + x1 * w_ref[s, 1:2, :] + xc * w_ref[s, 2:3, :])
        taps.append((x2, x1, xc))
    return outs, taps


def _convglu_specs(L, FS, tm, rev=False):
    t8 = tm // 8
    nt = L // tm
    tile = (lambda i: nt - 1 - i) if rev else (lambda i: i)
    return dict(
        up=pl.BlockSpec((2, None, tm, FS), lambda j, i: (0, j, tile(i), 0)),
        halo=pl.BlockSpec((2, None, 8, FS), lambda j, i: (0, j, jnp.maximum(tile(i) * t8 - 1, 0), 0)),
        w=pl.BlockSpec((2, None, 3, FS), lambda j, i: (0, j, 0, 0)),
        b=pl.BlockSpec((2, None, 1, FS), lambda j, i: (0, j, 0, 0)),
        act=pl.BlockSpec((None, tm, FS), lambda j, i: (j, tile(i), 0)),
    )


def _convglu_fwd(up, w, b):
    _, NSH, L, FS = up.shape
    tm = _row_tile(L)
    S = _convglu_specs(L, FS, tm)

    def body(up_ref, halo_ref, w_ref, b_ref, act_ref):
        (val, gate), _ = _conv_halves(up_ref, halo_ref, w_ref, b_ref, tm, FS, pl.program_id(1) == 0)
        act_ref[...] = _glu_math(val, gate).astype(BF16)

    return _pcall(body, name="convglu_fwd", grid=(NSH, L // tm), in_specs=[S["up"], S["halo"], S["w"], S["b"]],
                  out_specs=S["act"], out_shape=jax.ShapeDtypeStruct((NSH, L, FS), BF16))(up, up, w, b)


def _convglu_bwd(up, dact, w, b):
    _, NSH, L, FS = up.shape
    tm = _row_tile(L)
    nt = L // tm
    S = _convglu_specs(L, FS, tm, rev=True)

    def body(up_ref, halo_ref, w_ref, b_ref, dact_ref, dup_ref, dw_ref, db_ref, after_scr):
        step = pl.program_id(1)
        first = step == 0
        (val, gate), taps = _conv_halves(up_ref, halo_ref, w_ref, b_ref, tm, FS, step == nt - 1)
        _, vjp = jax.vjp(_glu_math, val, gate)
        dcs = vjp(dact_ref[...])
        for s in (0, 1):
            dc = dcs[s]
            n0 = jnp.where(first, 0.0, after_scr[s, 0:1, :])
            n1 = jnp.where(first, 0.0, after_scr[s, 1:2, :])
            x1 = _shifted_rows(dc, -1, [n0])
            x2 = _shifted_rows(dc, -2, [n0, n1])
            dup_ref[s] = (dc * w_ref[s, 2:3, :] + x1 * w_ref[s, 1:2, :] + x2 * w_ref[s, 0:1, :]).astype(BF16)
            after_scr[s] = dc[0:8]
            sums = [jnp.sum(dc * t, axis=0, keepdims=True) for t in taps[s]]
            dbs = jnp.sum(dc, axis=0, keepdims=True)

            @pl.when(first)
            def _():
                for t in range(3):
                    dw_ref[s, t:t + 1, :] = sums[t]
                db_ref[s] = dbs

            @pl.when(jnp.logical_not(first))
            def _():
                for t in range(3):
                    dw_ref[s, t:t + 1, :] += sums[t]
                db_ref[s] += dbs

    f = lambda *s: jax.ShapeDtypeStruct(s, F32)
    return _pcall(body, name="convglu_bwd", grid=(NSH, nt),
                  in_specs=[S["up"], S["halo"], S["w"], S["b"], S["act"]],
                  out_specs=[S["up"], S["w"], S["b"]],
                  out_shape=[jax.ShapeDtypeStruct((2, NSH, L, FS), BF16), f(2, NSH, 3, FS), f(2, NSH, 1, FS)],
                  scratch=[pltpu.VMEM((2, 8, FS), F32)])(up, up, w, b, dact)


def _loss_head(y, target):
    L, D = y.shape
    tm = _row_tile(L)

    def body(y_ref, t_ref, loss_ref, dy_ref):
        err = y_ref[...] - t_ref[...]
        dy_ref[...] = err / D
        part = 0.5 * jnp.sum(jnp.mean(err * err, axis=-1, keepdims=True), axis=0, keepdims=True)
        _accumulate(loss_ref, jnp.broadcast_to(part, (1, LANES)), pl.program_id(0) == 0)

    row = pl.BlockSpec((tm, D), lambda i: (i, 0))
    vec = pl.BlockSpec((1, LANES), lambda i: (0, 0))
    return _pcall(body, name="loss_head", grid=(L // tm,), in_specs=[row, row], out_specs=[vec, row],
                  out_shape=[jax.ShapeDtypeStruct((1, LANES), F32), jax.ShapeDtypeStruct((L, D), F32)])(y, target)


def _ada_fwd(c_all, ada_w, ada_b):
    NL, D, NC = ada_w.shape
    NB = c_all.shape[0]

    def body(c_ref, w_ref, b_ref, o_ref):
        cact = jax.nn.silu(c_ref[...])
        o_ref[...] = _dot(cact.astype(BF16), w_ref[...].astype(BF16)) + b_ref[...]

    return _pcall(body, name="ada_fwd", grid=(NL,),
                  in_specs=[pl.BlockSpec((NB, D), lambda l: (0, 0)), pl.BlockSpec((None, D, NC), lambda l: (l, 0, 0)),
                            pl.BlockSpec((None, 1, NC), lambda l: (l, 0, 0))],
                  out_specs=pl.BlockSpec((None, NB, NC), lambda l: (l, 0, 0)),
                  out_shape=jax.ShapeDtypeStruct((NL, NB, NC), F32))(c_all, ada_w, ada_b)


def _ada_bwd(c_all_t, dmod):
    D, NB = c_all_t.shape
    NL, _, NC = dmod.shape

    def body(c_ref, d_ref, o_ref):
        cact = jax.nn.silu(c_ref[...]).astype(BF16).astype(F32)
        o_ref[...] = _dot(cact, d_ref[...].astype(BF16).astype(F32))

    return _pcall(body, name="ada_bwd", grid=(NL,),
                  in_specs=[pl.BlockSpec((D, NB), lambda l: (0, 0)), pl.BlockSpec((None, NB, NC), lambda l: (l, 0, 0))],
                  out_specs=pl.BlockSpec((None, D, NC), lambda l: (l, 0, 0)),
                  out_shape=jax.ShapeDtypeStruct((NL, D, NC), F32))(c_all_t, dmod)


def _adamw(parts, w, m, v):
    NL, P, R, C = parts.shape
    tr = R if R <= 512 else max(t for t in range(16, 513, 16) if R % t == 0)

    def body(p_ref, w_ref, m_ref, v_ref, g_ref, d_ref, nm_ref, nv_ref):
        g = p_ref[0].astype(F32)
        for k in range(1, P):
            g = g + p_ref[k].astype(F32)
        m2 = ADAM_B1 * m_ref[...] + (1.0 - ADAM_B1) * g
        v2 = ADAM_B2 * v_ref[...] + (1.0 - ADAM_B2) * jnp.square(g)
        m_hat = m2 / (1.0 - ADAM_B1 ** ADAM_STEP)
        v_hat = v2 / (1.0 - ADAM_B2 ** ADAM_STEP)
        g_ref[...] = g
        d_ref[...] = -ADAM_LR * (m_hat / (jnp.sqrt(v_hat) + ADAM_EPS) + ADAM_WD * w_ref[...])
        nm_ref[...] = m2
        nv_ref[...] = v2

    pspec = pl.BlockSpec((None, P, tr, C), lambda l, i: (l, 0, i, 0))
    wspec = pl.BlockSpec((None, tr, C), lambda l, i: (l, i, 0))
    shp = jax.ShapeDtypeStruct((NL, R, C), F32)
    return _pcall(body, name="adamw", grid=(NL, R // tr), in_specs=[pspec, wspec, wspec, wspec],
                  out_specs=[wspec] * 4, out_shape=[shp] * 4)(parts, w, m, v)


def _block_diag(blocks):
    *lead, g, r, c = blocks.shape
    eye = jnp.eye(g, dtype=bool)[:, None, :, None]
    full = jnp.where(eye, blocks[..., :, :, None, :], 0.0)
    return full.reshape(*lead, g * r, g * c)


def _block_diag_extract(m, r, c):
    g = GROUPS_PER_TILE
    m5 = m.reshape(*m.shape[:-2], g, r, g, c)
    eye = jnp.eye(g, dtype=bool)[:, None, :, None]
    return jnp.sum(jnp.where(eye, m5, 0.0), axis=-2)


def kernel(x, c, ada_w, ada_b, norm1_g, w_in, q_norm_g, k_norm_g, ssm_a_re, ssm_a_im, ssm_log_dt, ssm_b_re, ssm_b_im, ssm_c_re, ssm_c_im, ssm_d, glu_w, glu_b, attn_out_g, ssm_out_g, w_out, norm2_g, ffn_w_up, ffn_conv_w, ffn_conv_b, ffn_w_down, loss_target, m_ada_w, m_ada_b, m_norm1_g, m_w_in, m_q_norm_g, m_k_norm_g, m_ssm_a_re, m_ssm_a_im, m_ssm_log_dt, m_ssm_b_re, m_ssm_b_im, m_ssm_c_re, m_ssm_c_im, m_ssm_d, m_glu_w, m_glu_b, m_attn_out_g, m_ssm_out_g, m_w_out, m_norm2_g, m_ffn_w_up, m_ffn_conv_w, m_ffn_conv_b, m_ffn_w_down, v_ada_w, v_ada_b, v_norm1_g, v_w_in, v_q_norm_g, v_k_norm_g, v_ssm_a_re, v_ssm_a_im, v_ssm_log_dt, v_ssm_b_re, v_ssm_b_im, v_ssm_c_re, v_ssm_c_im, v_ssm_d, v_glu_w, v_glu_b, v_attn_out_g, v_ssm_out_g, v_w_out, v_norm2_g, v_ffn_w_up, v_ffn_conv_w, v_ffn_conv_b, v_ffn_w_down):
    weights = dict(ada_w=ada_w, ada_b=ada_b, norm1_g=norm1_g, w_in=w_in, q_norm_g=q_norm_g, k_norm_g=k_norm_g,
                   ssm_a_re=ssm_a_re, ssm_a_im=ssm_a_im, ssm_log_dt=ssm_log_dt, ssm_b_re=ssm_b_re,
                   ssm_b_im=ssm_b_im, ssm_c_re=ssm_c_re, ssm_c_im=ssm_c_im, ssm_d=ssm_d, glu_w=glu_w, glu_b=glu_b,
                   attn_out_g=attn_out_g, ssm_out_g=ssm_out_g, w_out=w_out, norm2_g=norm2_g, ffn_w_up=ffn_w_up,
                   ffn_conv_w=ffn_conv_w, ffn_conv_b=ffn_conv_b, ffn_w_down=ffn_w_down)
    mom_m = dict(ada_w=m_ada_w, ada_b=m_ada_b, norm1_g=m_norm1_g, w_in=m_w_in, q_norm_g=m_q_norm_g,
                 k_norm_g=m_k_norm_g, ssm_a_re=m_ssm_a_re, ssm_a_im=m_ssm_a_im, ssm_log_dt=m_ssm_log_dt,
                 ssm_b_re=m_ssm_b_re, ssm_b_im=m_ssm_b_im, ssm_c_re=m_ssm_c_re, ssm_c_im=m_ssm_c_im, ssm_d=m_ssm_d,
                 glu_w=m_glu_w, glu_b=m_glu_b, attn_out_g=m_attn_out_g, ssm_out_g=m_ssm_out_g, w_out=m_w_out,
                 norm2_g=m_norm2_g, ffn_w_up=m_ffn_w_up, ffn_conv_w=m_ffn_conv_w, ffn_conv_b=m_ffn_conv_b,
                 ffn_w_down=m_ffn_w_down)
    mom_v = dict(ada_w=v_ada_w, ada_b=v_ada_b, norm1_g=v_norm1_g, w_in=v_w_in, q_norm_g=v_q_norm_g,
                 k_norm_g=v_k_norm_g, ssm_a_re=v_ssm_a_re, ssm_a_im=v_ssm_a_im, ssm_log_dt=v_ssm_log_dt,
                 ssm_b_re=v_ssm_b_re, ssm_b_im=v_ssm_b_im, ssm_c_re=v_ssm_c_re, ssm_c_im=v_ssm_c_im, ssm_d=v_ssm_d,
                 glu_w=v_glu_w, glu_b=v_glu_b, attn_out_g=v_attn_out_g, ssm_out_g=v_ssm_out_g, w_out=v_w_out,
                 norm2_g=v_norm2_g, ffn_w_up=v_ffn_w_up, ffn_conv_w=v_ffn_conv_w, ffn_conv_b=v_ffn_conv_b,
                 ffn_w_down=v_ffn_w_down)
    names = list(weights)
    big = ("ada_w", "w_in", "w_out", "ffn_w_up", "ffn_conv_w", "ffn_w_down")
    small = [n for n in names if n not in big]

    x = x[0]
    target = loss_target[0]
    L, D = x.shape
    NL = ada_w.shape[0]
    AW = D // 2
    SW = D - AW
    NH = AW // HEAD_DIM
    HP = AW // LANES
    G = SW // SSM_GROUP
    GT = SW // LANES
    NS = G * SSM_STATE
    NIN = w_in.shape[-1]
    FS = ffn_w_up.shape[-1]
    NSH = NDEV // 2
    NCA = ada_w.shape[-1]
    ROWS_OUT = w_out.shape[1]
    ROWS_DOWN = ffn_w_down.shape[1]
    B_ATT = min(L, 256)
    T_S5 = min(L, 1024)
    tm = _row_tile(L, 1024)
    ts = _row_tile(L, 512)
    ts2 = _row_tile(L, 256)
    me = _my_index()

    cpad = jnp.reshape(c, (D // LANES, LANES))
    c_all = _gather_small(cpad, "gather_c")
    c_all = c_all.reshape(NDEV, D)
    shards = [w_in.astype(BF16), w_out.astype(BF16), ffn_w_up.astype(BF16), ffn_conv_w, ffn_w_down.astype(BF16)]
    (first_w_in,) = _gather_shards([shards[0][0:1]], "gather_weights")

    def set_w_in(lp, w_in_g):
        lp.update(w_in_cols=jnp.swapaxes(w_in_g, 0, 1).reshape(D, NDEV * NIN))

    def set_other_weights(lp, gathered):
        w_out_g, w_up_g, conv_w_g, w_down_g = gathered
        lp.update(w_out=w_out_g.reshape(D, D), w_up=w_up_g, conv_w=conv_w_g.reshape(2, NSH, 3, FS),
                  w_down=w_down_g.reshape(NSH, 2 * ROWS_DOWN, D))

    conv_b_g = ffn_conv_b.reshape(NL, 2, NSH, 1, FS)

    ada_b_mine = lax.dynamic_slice_in_dim(ada_b, me * NCA, NCA, axis=1).reshape(NL, 1, NCA)
    mod_part = _ada_fwd(c_all, ada_w, ada_b_mine)
    mod_all = _gather_small(mod_part, "gather_mod")
    mod = lax.dynamic_index_in_dim(mod_all, me, axis=2, keepdims=False)
    mod = jnp.transpose(mod, (1, 0, 2)).reshape(NL, N_MOD, 1, D)

    row = lambda a: a.reshape(NL, 1, NS)
    ar_row, ai_row = row(ssm_a_re), row(ssm_a_im)
    ldt_row = row(jnp.broadcast_to(ssm_log_dt[:, :, None], (NL, G, SSM_STATE)))
    tiles = lambda a: a.reshape((NL, GT, GROUPS_PER_TILE) + a.shape[2:])
    braw_r = _block_diag(jnp.swapaxes(tiles(ssm_b_re), -1, -2))
    braw_i = _block_diag(jnp.swapaxes(tiles(ssm_b_im), -1, -2))
    crT = _block_diag(jnp.swapaxes(tiles(ssm_c_re), -1, -2)).astype(BF16)
    ciT = _block_diag(jnp.swapaxes(tiles(ssm_c_im), -1, -2)).astype(BF16)
    wg = _block_diag(tiles(glu_w)).astype(BF16)
    abr, abi, tabr, tabi, bsr, bsi = _s5_prep(ar_row, ai_row, ldt_row, braw_r, braw_i, T_S5)
    s5p = dict(bsr=bsr, bsi=bsi, tabr=tabr, tabi=tabi, crT=crT, ciT=ciT,
               d=ssm_d.reshape(NL, 1, SW), wg=wg, gb=glu_b.reshape(NL, 1, SW))

    gqk = jnp.concatenate([jnp.tile(q_norm_g, (1, NH)), jnp.tile(k_norm_g, (1, NH))], axis=1).reshape(NL, 1, 2 * AW)
    layer_params = dict(
        mod=mod, norm1_g=norm1_g.reshape(NL, 1, D), norm2_g=norm2_g.reshape(NL, 1, D), gqk=gqk,
        ga=attn_out_g.reshape(NL, 1, AW), gs=ssm_out_g.reshape(NL, 1, SW),
        conv_b=conv_b_g, s5=s5p)


    tn = min(D, 512)

    def resid_epilogue(acc, xres, gate):
        return acc, xres + gate * acc

    def layer_fwd(xin, lp, ride):
        sh1, sc1, g1, sh2, sc2, g2 = (lp["mod"][k] for k in range(N_MOD))
        h = _lnmod_fwd(xin, lp["norm1_g"], sh1, sc1)
        p = _matmul(
            "mm_in", h, lp["w_in_cols"], dims=NN, grid=(L // ts,),
            a_spec=pl.BlockSpec((ts, D), lambda i: (i, 0)), b_spec=pl.BlockSpec((D, NDEV * NIN), lambda i: (0, 0)),
            out_shape=jax.ShapeDtypeStruct((L, NDEV * NIN), F32),
            out_specs=pl.BlockSpec((ts, NDEV * NIN), lambda i: (i, 0)))
        qk = _qknorm_fwd(p, lp["gqk"], AW)
        o_attn, tot, gathered = _attn_fwd(qk, p, AW, B_ATT, ride)
        set_other_weights(lp, gathered[:4])
        next_gathered = gathered[4:]
        u_st = _to_streams(p[:, 3 * AW:], T_S5)
        o_st, s_re, s_im, ypre = _s5_fwd(u_st, lp["s5"], T_S5)
        o_ssm = _from_streams(o_st, T_S5)
        o = _outnorm_fwd(o_attn, o_ssm, lp["ga"], lp["gs"])
        a1, x_mid = _matmul(
            "mm_out", o, lp["w_out"], dims=NN, grid=(L // tm, D // tn),
            a_spec=pl.BlockSpec((tm, D), lambda i, j: (i, 0)), b_spec=pl.BlockSpec((D, tn), lambda i, j: (0, j)),
            extra=(xin, g1), extra_specs=(pl.BlockSpec((tm, tn), lambda i, j: (i, j)),
                                          pl.BlockSpec((1, tn), lambda i, j: (0, j))),
            epilogue=resid_epilogue,
            out_shape=[jax.ShapeDtypeStruct((L, D), F32)] * 2,
            out_specs=[pl.BlockSpec((tm, tn), lambda i, j: (i, j))] * 2)
        h2 = _lnmod_fwd(x_mid, lp["norm2_g"], sh2, sc2)
        up = _matmul(
            "mm_up", h2, lp["w_up"], dims=NN, grid=(L // tm, NDEV),
            a_spec=pl.BlockSpec((tm, D), lambda i, j: (i, 0)), b_spec=pl.BlockSpec((None, D, FS), lambda i, j: (j, 0, 0)),
            out_shape=jax.ShapeDtypeStruct((NDEV, L, FS), F32),
            out_specs=pl.BlockSpec((None, tm, FS), lambda i, j: (j, i, 0)))
        up = up.reshape(2, NSH, L, FS)
        act = _convglu_fwd(up, lp["conv_w"], lp["conv_b"])
        a2, x_out = _matmul(
            "mm_down", act, lp["w_down"], dims=NN, grid=(L // ts, D // tn),
            a_spec=pl.BlockSpec((NSH, ts, FS), lambda i, j: (0, i, 0)),
            b_spec=pl.BlockSpec((NSH, FS, tn), lambda i, j: (0, 0, j)),
            extra=(x_mid, g2), extra_specs=(pl.BlockSpec((ts, tn), lambda i, j: (i, j)),
                                            pl.BlockSpec((1, tn), lambda i, j: (0, j))),
            epilogue=resid_epilogue,
            out_shape=[jax.ShapeDtypeStruct((L, D), F32)] * 2,
            out_specs=[pl.BlockSpec((ts, tn), lambda i, j: (i, j))] * 2)
        res = dict(x=xin, h=h, p=p, qk=qk, tot=tot, o_attn=o_attn, o_ssm=o_ssm, u_st=u_st, s_re=s_re, s_im=s_im, ypre=ypre,
                   o=o, a1=a1, x_mid=x_mid, h2=h2, up=up, act=act, a2=a2)
        return x_out, res, next_gathered

    per_layer = [jax.tree.map(lambda a: a[l], layer_params) for l in range(NL)]
    y, residuals = x, []
    set_w_in(per_layer[0], first_w_in[0])
    for l in range(NL):
        ride = [s[l] for s in shards[1:]] + ([shards[0][l + 1]] if l + 1 < NL else [])
        y, res, next_w_in = layer_fwd(y, per_layer[l], ride)
        if next_w_in:
            set_w_in(per_layer[l + 1], next_w_in[0])
        residuals.append(res)

    loss_row, dy = _loss_head(y, target)
    loss = lax.psum(loss_row[0, 0], ("x", "y", "c"))

    def layer_bwd(dx, args, above):
        lp, r = args
        sh1, sc1, g1, sh2, sc2, g2 = (lp["mod"][k] for k in range(N_MOD))
        da2, dg2 = _gate_bwd(dx, r["a2"], g2)
        dact = _matmul(
            "mm_dact", da2, lp["w_down"], dims=NT, grid=(L // tm, NSH),
            a_spec=pl.BlockSpec((tm, D), lambda i, j: (i, 0)), b_spec=pl.BlockSpec((None, FS, D), lambda i, j: (j, 0, 0)),
            out_shape=jax.ShapeDtypeStruct((NSH, L, FS), F32),
            out_specs=pl.BlockSpec((None, tm, FS), lambda i, j: (j, i, 0)))
        dw_down = _matmul(
            "mm_dw_down", r["act"], da2, dims=TN, grid=(NSH, D // tn),
            a_spec=pl.BlockSpec((None, L, FS), lambda j, n: (j, 0, 0)),
            b_spec=pl.BlockSpec((L, tn), lambda j, n: (0, n)),
            out_shape=jax.ShapeDtypeStruct((NSH, FS, D), BF16),
            out_specs=pl.BlockSpec((None, FS, tn), lambda j, n: (j, 0, n)))
        dup4, dcw, dcb = _convglu_bwd(r["up"], dact, lp["conv_w"], lp["conv_b"])
        dup = dup4.reshape(NDEV, L, FS)
        dh2 = _matmul(
            "mm_dh2", dup, lp["w_up"], dims=NT, grid=(L // ts2,),
            a_spec=pl.BlockSpec((NDEV, ts2, FS), lambda i: (0, i, 0)),
            b_spec=pl.BlockSpec((NDEV, D, FS), lambda i: (0, 0, 0)),
            out_shape=jax.ShapeDtypeStruct((L, D), F32), out_specs=pl.BlockSpec((ts2, D), lambda i: (i, 0)))
        dw_up = _matmul(
            "mm_dw_up", r["h2"], dup, dims=TN, grid=(NDEV,),
            a_spec=pl.BlockSpec((L, D), lambda j: (0, 0)), b_spec=pl.BlockSpec((None, L, FS), lambda j: (j, 0, 0)),
            out_shape=jax.ShapeDtypeStruct((NDEV, D, FS), BF16),
            out_specs=pl.BlockSpec((None, D, FS), lambda j: (j, 0, 0)))
        dxm, dn2, dsh2, dsc2 = _lnmod_bwd(dh2, r["x_mid"], lp["norm2_g"], sh2, sc2, dx)
        da1, dg1 = _gate_bwd(dxm, r["a1"], g1)
        do = _matmul(
            "mm_do", da1, lp["w_out"], dims=NT, grid=(L // tm, D // tn),
            a_spec=pl.BlockSpec((tm, D), lambda i, j: (i, 0)), b_spec=pl.BlockSpec((tn, D), lambda i, j: (j, 0)),
            out_shape=jax.ShapeDtypeStruct((L, D), F32), out_specs=pl.BlockSpec((tm, tn), lambda i, j: (i, j)))
        dw_out = _matmul(
            "mm_dw_out", r["o"], da1, dims=TN, grid=(D // tn, D // tn),
            a_spec=pl.BlockSpec((L, tn), lambda m, n: (0, m)), b_spec=pl.BlockSpec((L, tn), lambda m, n: (0, n)),
            out_shape=jax.ShapeDtypeStruct((D, D), BF16), out_specs=pl.BlockSpec((tn, tn), lambda m, n: (m, n)))
        doa, dos, dga, dgs = _outnorm_bwd(do, r["o_attn"], r["o_ssm"], lp["ga"], lp["gs"])
        (du_st, dbsr, dbsi, dcr, dci, dd, dwg, dgb, dabr, dabi) = _s5_bwd(
            _to_streams(dos, T_S5), r["u_st"], r["s_re"], r["s_im"], r["ypre"], lp["s5"], T_S5)
        du = _from_streams(du_st, T_S5)
        ride = [dw_out.reshape(NDEV, ROWS_OUT, D), dw_up, dcw.reshape(NDEV, 3, FS),
                dw_down.reshape(NDEV, ROWS_DOWN, D)] + above
        dq, dk, dv, received = _attn_bwd(r["qk"], r["p"], doa, r["tot"], AW, B_ATT, ride)
        dqk, dgqk = _qknorm_bwd(jnp.concatenate([dq, dk], axis=1), r["p"], lp["gqk"], AW)
        dp = jnp.concatenate([dqk, dv.astype(BF16), du], axis=1)
        dh = _matmul(
            "mm_dh", dp, lp["w_in_cols"], dims=NT, grid=(L // tm,),
            a_spec=pl.BlockSpec((tm, NDEV * NIN), lambda i: (i, 0)), b_spec=pl.BlockSpec((D, NDEV * NIN), lambda i: (0, 0)),
            out_shape=jax.ShapeDtypeStruct((L, D), F32), out_specs=pl.BlockSpec((tm, D), lambda i: (i, 0)))
        dw_in = _matmul(
            "mm_dw_in", r["h"], dp, dims=TN, grid=(NDEV,),
            a_spec=pl.BlockSpec((L, D), lambda j: (0, 0)), b_spec=pl.BlockSpec((L, NIN), lambda j: (0, j)),
            out_shape=jax.ShapeDtypeStruct((NDEV, D, NIN), BF16),
            out_specs=pl.BlockSpec((None, D, NIN), lambda j: (j, 0, 0)))
        dx0, dn1, dsh1, dsc1 = _lnmod_bwd(dh, r["x"], lp["norm1_g"], sh1, sc1, dxm)
        grads = dict(
            dmod=jnp.concatenate([dsh1, dsc1, dg1, dsh2, dsc2, dg2], axis=1), dn1=dn1, dn2=dn2, dgqk=dgqk,
            dga=dga, dgs=dgs, dbsr=dbsr, dbsi=dbsi, dcr=dcr, dci=dci, dd=dd, dwg=dwg, dgb=dgb, dabr=dabr, dabi=dabi,
            dcb=dcb)
        return dx0, grads, dw_in, received

    grad_x, layer_grads, parts, above = dy, [None] * NL, [[None] * 5 for _ in range(NL)], []
    for l in reversed(range(NL)):
        grad_x, layer_grads[l], dw_in_l, received = layer_bwd(grad_x, (per_layer[l], residuals[l]), above)
        parts[l][1:] = received[:4]
        if above:
            parts[l + 1][0] = received[4]
        above = [dw_in_l]
    (parts[0][0],) = _scatter_direct(above, "scatter_weight_grads")
    gr = jax.tree.map(lambda *a: jnp.stack(a), *layer_grads)

    dar, dai, dldt, dbr_bd, dbi_bd = _s5_prep_bwd(ar_row, ai_row, ldt_row, braw_r, braw_i,
                                                  gr["dabr"], gr["dabi"], gr["dbsr"], gr["dbsi"])
    unt = lambda a: a.reshape((NL, G) + a.shape[3:])
    local = dict(
        ada_b=gr["dmod"].reshape(NL, N_MOD * D),
        norm1_g=gr["dn1"].reshape(NL, D), norm2_g=gr["dn2"].reshape(NL, D),
        q_norm_g=gr["dgqk"].reshape(NL, 2, NH, HEAD_DIM)[:, 0].sum(axis=1),
        k_norm_g=gr["dgqk"].reshape(NL, 2, NH, HEAD_DIM)[:, 1].sum(axis=1),
        ssm_a_re=dar.reshape(NL, G, SSM_STATE), ssm_a_im=dai.reshape(NL, G, SSM_STATE),
        ssm_log_dt=dldt.reshape(NL, G, SSM_STATE).sum(axis=-1),
        ssm_b_re=jnp.swapaxes(unt(_block_diag_extract(dbr_bd, SSM_GROUP, SSM_STATE)), -1, -2),
        ssm_b_im=jnp.swapaxes(unt(_block_diag_extract(dbi_bd, SSM_GROUP, SSM_STATE)), -1, -2),
        ssm_c_re=jnp.swapaxes(unt(_block_diag_extract(gr["dcr"], SSM_STATE, SSM_GROUP)), -1, -2),
        ssm_c_im=jnp.swapaxes(unt(_block_diag_extract(gr["dci"], SSM_STATE, SSM_GROUP)), -1, -2),
        ssm_d=gr["dd"].reshape(NL, G, SSM_GROUP),
        glu_w=unt(_block_diag_extract(gr["dwg"], SSM_GROUP, SSM_GROUP)),
        glu_b=gr["dgb"].reshape(NL, G, SSM_GROUP),
        attn_out_g=gr["dga"].reshape(NL, AW), ssm_out_g=gr["dgs"].reshape(NL, SW),
        ffn_conv_b=gr["dcb"].reshape(NL, 2 * NSH * FS),
    )

    def pack(tree):
        flat = jnp.concatenate([tree[n].reshape(-1) for n in small])
        pad = (-flat.shape[0]) % (512 * LANES)
        return jnp.pad(flat, (0, pad)).reshape(1, -1, LANES)

    (small_parts,) = _gather_shards([pack(local).astype(BF16)], "gather_small_grads")
    sg, sd, sm, sv = _adamw(small_parts, pack(weights), pack(mom_m), pack(mom_v))

    def unpack(buf):
        flat = buf.reshape(-1)
        out, off = {}, 0
        for n in small:
            size = weights[n].size
            out[n] = flat[off:off + size].reshape(weights[n].shape)
            off += size
        return out

    ug, ud, um, uv = unpack(sg), unpack(sd), unpack(sm), unpack(sv)
    results = {n: (ug[n], ud[n], um[n], uv[n]) for n in small}

    dmod_all = _gather_small(gr["dmod"].reshape(NL, N_MOD * D), "gather_dmod")
    dmod_mine = lax.dynamic_slice_in_dim(dmod_all, me * NCA, NCA, axis=2)
    d_ada_w = _ada_bwd(jnp.transpose(c_all), jnp.transpose(dmod_mine, (1, 0, 2)))
    results["ada_w"] = tuple(_adamw(d_ada_w[:, None], ada_w, m_ada_w, v_ada_w))

    for a, n in enumerate(("w_in", "w_out", "ffn_w_up", "ffn_conv_w", "ffn_w_down")):
        results[n] = tuple(_adamw(jnp.stack([parts[l][a] for l in range(NL)]), weights[n], mom_m[n], mom_v[n]))

    out = [loss, grad_x[None]]
    for k in range(4):
        out.extend(results[n][k] for n in names)
    return tuple(out)
```
